```python
import math
import jax
import jax.numpy as jnp
from jax import lax
import numpy as np

D_MODEL = 2048
BATCH = 8
SEQ = 2048
DEPTH = 2

MLA_HEADS = 8
QK_NOPE = 128
QK_ROPE = 64
V_HEAD = 128
Q_LORA = 512
KV_LORA = 512
ROPE_THETA = 10000.0
Q_BLOCK = 128
GDN_HEADS = 8
GDN_DK = 128
GDN_DV = 128
CONV_WIDTH = 4
CHUNK = 64
GDN_QK = GDN_HEADS * GDN_DK
GDN_V = GDN_HEADS * GDN_DV
CONV_CH = 2 * GDN_QK + GDN_V
D_FF = ((8 * D_MODEL // 3 + 255) // 256) * 256
EPS = 1e-6
IN_SIZES = (Q_LORA, KV_LORA, QK_ROPE, GDN_QK, GDN_QK, GDN_V, GDN_V, GDN_HEADS, GDN_HEADS, 2 * D_MODEL)
IN_WIDTH = Q_LORA + KV_LORA + QK_ROPE + 2 * GDN_QK + 2 * GDN_V + 2 * GDN_HEADS + 2 * D_MODEL

kernel_name = 'hybrid_mla_gdn_adaln_block'


def _rmsnorm(x, w):
    xf = x.astype(jnp.float32)
    y = xf * lax.rsqrt(jnp.mean(xf * xf, axis=-1, keepdims=True) + EPS)
    return (y * w.astype(jnp.float32)).astype(x.dtype)


def _split_cols(p):
    outs, off = [], 0
    for n in IN_SIZES:
        outs.append(p[..., off:off + n])
        off += n
    return outs


def _rope_tables(positions):
    inv_freq = 1.0 / (ROPE_THETA ** (jnp.arange(0, QK_ROPE, 2, dtype=jnp.float32) / QK_ROPE))
    ang = positions.astype(jnp.float32)[..., None] * inv_freq
    return jnp.cos(ang), jnp.sin(ang)


def _rope(x, cos, sin):
    xf = x.astype(jnp.float32)
    x1, x2 = jnp.split(xf, 2, axis=-1)
    return jnp.concatenate([x1 * cos - x2 * sin, x2 * cos + x1 * sin], axis=-1).astype(x.dtype)


def _mla_branch(c_q, c_kv, k_pe, q_norm, kv_norm, w_uq, w_ukv, cos, sin):
    B, T, _ = c_q.shape
    q = (_rmsnorm(c_q, q_norm) @ w_uq).reshape(B, T, MLA_HEADS, QK_NOPE + QK_ROPE)
    q_nope, q_pe = q[..., :QK_NOPE], q[..., QK_NOPE:]
    q_pe = _rope(q_pe, cos[:, :, None, :], sin[:, :, None, :])
    kv = (_rmsnorm(c_kv, kv_norm) @ w_ukv).reshape(B, T, MLA_HEADS, QK_NOPE + V_HEAD)
    k_nope, v = kv[..., :QK_NOPE], kv[..., QK_NOPE:]
    k_pe = _rope(k_pe, cos, sin)
    scale = (QK_NOPE + QK_ROPE) ** -0.5
    outs = []
    for i in range(T // Q_BLOCK):
        q0, k_end = i * Q_BLOCK, (i + 1) * Q_BLOCK
        s = (jnp.einsum('bqhd,bkhd->bhqk', q_nope[:, q0:k_end], k_nope[:, :k_end])
             + jnp.einsum('bqhr,bkr->bhqk', q_pe[:, q0:k_end], k_pe[:, :k_end]))
        s = s.astype(jnp.float32) * scale
        mask = jnp.arange(k_end)[None, :] <= (q0 + jnp.arange(Q_BLOCK))[:, None]
        p = jax.nn.softmax(jnp.where(mask, s, -jnp.inf), axis=-1).astype(v.dtype)
        outs.append(jnp.einsum('bhqk,bkhd->bqhd', p, v[:, :k_end]))
    return jnp.concatenate(outs, axis=1).reshape(B, T, MLA_HEADS * V_HEAD)


def _causal_conv_silu(u, w):
    kern = w[:, None, :].astype(u.dtype)
    y = lax.conv_general_dilated(u, kern, window_strides=(1,), padding=[(CONV_WIDTH - 1, 0)],
                                 dimension_numbers=('NWC', 'WIO', 'NWC'),
                                 feature_group_count=u.shape[-1])
    return jax.nn.silu(y)


def _l2norm(x):
    xf = x.astype(jnp.float32)
    return xf * lax.rsqrt(jnp.sum(xf * xf, axis=-1, keepdims=True) + EPS)


def _gated_delta_chunked(q, k, v, beta, g):
    B, T, H, DK = q.shape
    DV = v.shape[-1]
    N = T // CHUNK
    to_chunks = lambda a: a.reshape(B, N, CHUNK, H, -1).transpose(0, 3, 1, 2, 4)
    q, k, v = to_chunks(q), to_chunks(k), to_chunks(v)
    beta = beta.reshape(B, N, CHUNK, H).transpose(0, 3, 1, 2)
    g = g.reshape(B, N, CHUNK, H).transpose(0, 3, 1, 2)
    G = jnp.cumsum(g, axis=-1)
    idx = jnp.arange(CHUNK)
    lower = idx[:, None] >= idx[None, :]
    strict = idx[:, None] > idx[None, :]
    diff = G[..., :, None] - G[..., None, :]
    decay = jnp.where(lower, jnp.exp(jnp.where(lower, diff, 0.0)), 0.0)
    kb = k * beta[..., None]
    Lmat = jnp.where(strict, jnp.einsum('bhncd,bhnsd->bhncs', kb, k) * decay, 0.0)
    A = Lmat + jnp.eye(CHUNK, dtype=jnp.float32)
    rhs = jnp.concatenate([v * beta[..., None], kb * jnp.exp(G)[..., None]], axis=-1)
    sol = lax.linalg.triangular_solve(A, rhs, left_side=True, lower=True, unit_diagonal=True)
    u, w = sol[..., :DV], sol[..., DV:]
    attn = jnp.where(lower, jnp.einsum('bhncd,bhnsd->bhncs', q, k) * decay, 0.0)

    def step(S, inp):
        q_c, k_c, u_c, w_c, G_c, a_c = inp
        v_new = u_c - jnp.einsum('bhcd,bhde->bhce', w_c, S)
        o = (jnp.einsum('bhcd,bhde->bhce', q_c * jnp.exp(G_c)[..., None], S)
             + jnp.einsum('bhcs,bhse->bhce', a_c, v_new))
        G_last = G_c[..., -1]
        k_dec = k_c * jnp.exp(G_last[..., None] - G_c)[..., None]
        S = S * jnp.exp(G_last)[..., None, None] + jnp.einsum('bhcd,bhce->bhde', k_dec, v_new)
        return S, o

    xs = tuple(jnp.moveaxis(a, 2, 0) for a in (q, k, u, w, G, attn))
    S0 = jnp.zeros((B, H, DK, DV), jnp.float32)
    _, o = lax.scan(step, S0, xs)
    return o.transpose(1, 0, 3, 2, 4).reshape(B, T, H, DV)


def _gdn_branch(qkv, z, b_logit, a_logit, conv_w, A_log, dt_bias, gdn_norm):
    B, T, _ = qkv.shape
    dtype = qkv.dtype
    qkv = _causal_conv_silu(qkv, conv_w)
    q = _l2norm(qkv[..., :GDN_QK].reshape(B, T, GDN_HEADS, GDN_DK)) * (GDN_DK ** -0.5)
    k = _l2norm(qkv[..., GDN_QK:2 * GDN_QK].reshape(B, T, GDN_HEADS, GDN_DK))
    v = qkv[..., 2 * GDN_QK:].reshape(B, T, GDN_HEADS, GDN_DV).astype(jnp.float32)
    beta = jax.nn.sigmoid(b_logit.astype(jnp.float32))
    g = -jnp.exp(A_log.astype(jnp.float32)) * jax.nn.softplus(a_logit.astype(jnp.float32) + dt_bias.astype(jnp.float32))
    o = _gated_delta_chunked(q, k, v, beta, g)
    o = o * lax.rsqrt(jnp.mean(o * o, axis=-1, keepdims=True) + EPS) * gdn_norm.astype(jnp.float32)
    o = o * jax.nn.silu(z.reshape(B, T, GDN_HEADS, GDN_DV).astype(jnp.float32))
    return o.reshape(B, T, GDN_V).astype(dtype)


def _fwd_setup_inputs(seed: int = 0) -> dict:
    key = jax.random.key(seed)
    ks = jax.random.split(key, 24)
    L, D = DEPTH, D_MODEL
    f32 = jnp.float32

    def nrm(k, shape, fan_in, gain=1.0):
        return gain * fan_in ** -0.5 * jax.random.normal(k, shape, f32)

    def gain(k, shape):
        return 1.0 + 0.02 * jax.random.normal(k, shape, f32)

    x = jax.random.normal(ks[0], (BATCH, SEQ, D), f32)
    c = jax.random.normal(ks[1], (BATCH, D), f32)
    positions = (jnp.arange(SEQ, dtype=jnp.int32)[None, :]
                 + jax.random.randint(ks[2], (BATCH, 1), 0, 1024, dtype=jnp.int32))
    dt = jnp.exp(jax.random.uniform(ks[15], (L, GDN_HEADS), f32, math.log(1e-3), math.log(1e-1)))
    return {
        'x': x,
        'c': c,
        'positions': positions,
        'w_ada': nrm(ks[3], (L, D, 6 * D), D, 0.5),
        'b_ada': 0.01 * jax.random.normal(ks[4], (L, 6 * D), f32),
        'norm_mix': gain(ks[5], (L, D)),
        'norm_ffn': gain(ks[6], (L, D)),
        'w_in': nrm(ks[7], (L, D, IN_WIDTH), D),
        'q_a_norm': gain(ks[8], (L, Q_LORA)),
        'kv_a_norm': gain(ks[9], (L, KV_LORA)),
        'w_uq': nrm(ks[10], (L, Q_LORA, MLA_HEADS * (QK_NOPE + QK_ROPE)), Q_LORA),
        'w_ukv': nrm(ks[11], (L, KV_LORA, MLA_HEADS * (QK_NOPE + V_HEAD)), KV_LORA),
        'w_o_mla': nrm(ks[12], (L, MLA_HEADS * V_HEAD, D), MLA_HEADS * V_HEAD),
        'conv_w': nrm(ks[13], (L, CONV_WIDTH, CONV_CH), CONV_WIDTH),
        'A_log': jnp.log(jax.random.uniform(ks[14], (L, GDN_HEADS), f32, 1.0, 16.0)),
        'dt_bias': dt + jnp.log(-jnp.expm1(-dt)),
        'gdn_norm': gain(ks[16], (L, GDN_DV)),
        'w_o_gdn': nrm(ks[17], (L, GDN_V, D), GDN_V),
        'w_o': nrm(ks[18], (L, D, D), D),
        'w_gate_up': nrm(ks[19], (L, D, 2 * D_FF), D),
        'w_down': nrm(ks[20], (L, D_FF, D), D_FF),
        'final_norm': gain(ks[21], (D,)),
    }


def _fwd_reference(x, c, positions, w_ada, b_ada, norm_mix, norm_ffn, w_in, q_a_norm, kv_a_norm,
              w_uq, w_ukv, w_o_mla, conv_w, A_log, dt_bias, gdn_norm, w_o_gdn, w_o,
              w_gate_up, w_down, final_norm):
    cos, sin = _rope_tables(positions)
    c_act = jax.nn.silu(c)
    for l in range(DEPTH):
        mod = c_act @ w_ada[l] + b_ada[l]
        sh_a, sc_a, gt_a, sh_f, sc_f, gt_f = [m[:, None, :] for m in jnp.split(mod, 6, axis=-1)]
        h = _rmsnorm(x, norm_mix[l]) * (1.0 + sc_a) + sh_a
        p = h @ w_in[l]
        c_q, c_kv, k_pe, q_g, k_g, v_g, z, b_logit, a_logit, gate_logits = _split_cols(p)
        y_a = _mla_branch(c_q, c_kv, k_pe, q_a_norm[l], kv_a_norm[l], w_uq[l], w_ukv[l], cos, sin) @ w_o_mla[l]
        qkv = jnp.concatenate([q_g, k_g, v_g], axis=-1)
        y_b = _gdn_branch(qkv, z, b_logit, a_logit, conv_w[l], A_log[l], dt_bias[l], gdn_norm[l]) @ w_o_gdn[l]
        g_a, g_b = jnp.split(jax.nn.sigmoid(gate_logits), 2, axis=-1)
        mix = (g_a * y_a + g_b * y_b) @ w_o[l]
        x = x + gt_a * mix
        h = _rmsnorm(x, norm_ffn[l]) * (1.0 + sc_f) + sh_f
        gate, up = jnp.split(h @ w_gate_up[l], 2, axis=-1)
        x = x + gt_f * ((jax.nn.silu(gate) * up) @ w_down[l])
    return _rmsnorm(x, final_norm)


import jax as _jax
import jax.numpy as _jnp

TWIN_FORMAT = 'train_step'
FWD_PARAMS = ['x', 'c', 'positions', 'w_ada', 'b_ada', 'norm_mix', 'norm_ffn', 'w_in', 'q_a_norm', 'kv_a_norm', 'w_uq', 'w_ukv', 'w_o_mla', 'conv_w', 'A_log', 'dt_bias', 'gdn_norm', 'w_o_gdn', 'w_o', 'w_gate_up', 'w_down', 'final_norm']
TWIN_WEIGHTS = ['w_ada', 'b_ada', 'norm_mix', 'norm_ffn', 'w_in', 'q_a_norm', 'kv_a_norm', 'w_uq', 'w_ukv', 'w_o_mla', 'conv_w', 'A_log', 'dt_bias', 'gdn_norm', 'w_o_gdn', 'w_o', 'w_gate_up', 'w_down', 'final_norm']
TWIN_DIFF_INPUT = 'x'
TWIN_INPUTS = ['x', 'c', 'positions', 'w_ada', 'b_ada', 'norm_mix', 'norm_ffn', 'w_in', 'q_a_norm', 'kv_a_norm', 'w_uq', 'w_ukv', 'w_o_mla', 'conv_w', 'A_log', 'dt_bias', 'gdn_norm', 'w_o_gdn', 'w_o', 'w_gate_up', 'w_down', 'final_norm', 'loss_target', 'm_w_ada', 'm_b_ada', 'm_norm_mix', 'm_norm_ffn', 'm_w_in', 'm_q_a_norm', 'm_kv_a_norm', 'm_w_uq', 'm_w_ukv', 'm_w_o_mla', 'm_conv_w', 'm_A_log', 'm_dt_bias', 'm_gdn_norm', 'm_w_o_gdn', 'm_w_o', 'm_w_gate_up', 'm_w_down', 'm_final_norm', 'v_w_ada', 'v_b_ada', 'v_norm_mix', 'v_norm_ffn', 'v_w_in', 'v_q_a_norm', 'v_kv_a_norm', 'v_w_uq', 'v_w_ukv', 'v_w_o_mla', 'v_conv_w', 'v_A_log', 'v_dt_bias', 'v_gdn_norm', 'v_w_o_gdn', 'v_w_o', 'v_w_gate_up', 'v_w_down', 'v_final_norm']
TWIN_OUTPUTS = ['loss', 'grad_x', 'grad_w_ada', 'grad_b_ada', 'grad_norm_mix', 'grad_norm_ffn', 'grad_w_in', 'grad_q_a_norm', 'grad_kv_a_norm', 'grad_w_uq', 'grad_w_ukv', 'grad_w_o_mla', 'grad_conv_w', 'grad_A_log', 'grad_dt_bias', 'grad_gdn_norm', 'grad_w_o_gdn', 'grad_w_o', 'grad_w_gate_up', 'grad_w_down', 'grad_final_norm', 'delta_w_ada', 'delta_b_ada', 'delta_norm_mix', 'delta_norm_ffn', 'delta_w_in', 'delta_q_a_norm', 'delta_kv_a_norm', 'delta_w_uq', 'delta_w_ukv', 'delta_w_o_mla', 'delta_conv_w', 'delta_A_log', 'delta_dt_bias', 'delta_gdn_norm', 'delta_w_o_gdn', 'delta_w_o', 'delta_w_gate_up', 'delta_w_down', 'delta_final_norm', 'new_m_w_ada', 'new_m_b_ada', 'new_m_norm_mix', 'new_m_norm_ffn', 'new_m_w_in', 'new_m_q_a_norm', 'new_m_kv_a_norm', 'new_m_w_uq', 'new_m_w_ukv', 'new_m_w_o_mla', 'new_m_conv_w', 'new_m_A_log', 'new_m_dt_bias', 'new_m_gdn_norm', 'new_m_w_o_gdn', 'new_m_w_o', 'new_m_w_gate_up', 'new_m_w_down', 'new_m_final_norm', 'new_v_w_ada', 'new_v_b_ada', 'new_v_norm_mix', 'new_v_norm_ffn', 'new_v_w_in', 'new_v_q_a_norm', 'new_v_kv_a_norm', 'new_v_w_uq', 'new_v_w_ukv', 'new_v_w_o_mla', 'new_v_conv_w', 'new_v_A_log', 'new_v_dt_bias', 'new_v_gdn_norm', 'new_v_w_o_gdn', 'new_v_w_o', 'new_v_w_gate_up', 'new_v_w_down', 'new_v_final_norm']
TWIN_LEAF_KINDS = {'loss': 'loss', 'grad_x': 'grad_x', 'grad_w_ada': 'grad_w', 'grad_b_ada': 'grad_w', 'grad_norm_mix': 'grad_w', 'grad_norm_ffn': 'grad_w', 'grad_w_in': 'grad_w', 'grad_q_a_norm': 'grad_w', 'grad_kv_a_norm': 'grad_w', 'grad_w_uq': 'grad_w', 'grad_w_ukv': 'grad_w', 'grad_w_o_mla': 'grad_w', 'grad_conv_w': 'grad_w', 'grad_A_log': 'grad_w', 'grad_dt_bias': 'grad_w', 'grad_gdn_norm': 'grad_w', 'grad_w_o_gdn': 'grad_w', 'grad_w_o': 'grad_w', 'grad_w_gate_up': 'grad_w', 'grad_w_down': 'grad_w', 'grad_final_norm': 'grad_w', 'delta_w_ada': 'delta_w', 'delta_b_ada': 'delta_w', 'delta_norm_mix': 'delta_w', 'delta_norm_ffn': 'delta_w', 'delta_w_in': 'delta_w', 'delta_q_a_norm': 'delta_w', 'delta_kv_a_norm': 'delta_w', 'delta_w_uq': 'delta_w', 'delta_w_ukv': 'delta_w', 'delta_w_o_mla': 'delta_w', 'delta_conv_w': 'delta_w', 'delta_A_log': 'delta_w', 'delta_dt_bias': 'delta_w', 'delta_gdn_norm': 'delta_w', 'delta_w_o_gdn': 'delta_w', 'delta_w_o': 'delta_w', 'delta_w_gate_up': 'delta_w', 'delta_w_down': 'delta_w', 'delta_final_norm': 'delta_w', 'new_m_w_ada': 'new_m', 'new_m_b_ada': 'new_m', 'new_m_norm_mix': 'new_m', 'new_m_norm_ffn': 'new_m', 'new_m_w_in': 'new_m', 'new_m_q_a_norm': 'new_m', 'new_m_kv_a_norm': 'new_m', 'new_m_w_uq': 'new_m', 'new_m_w_ukv': 'new_m', 'new_m_w_o_mla': 'new_m', 'new_m_conv_w': 'new_m', 'new_m_A_log': 'new_m', 'new_m_dt_bias': 'new_m', 'new_m_gdn_norm': 'new_m', 'new_m_w_o_gdn': 'new_m', 'new_m_w_o': 'new_m', 'new_m_w_gate_up': 'new_m', 'new_m_w_down': 'new_m', 'new_m_final_norm': 'new_m', 'new_v_w_ada': 'new_v', 'new_v_b_ada': 'new_v', 'new_v_norm_mix': 'new_v', 'new_v_norm_ffn': 'new_v', 'new_v_w_in': 'new_v', 'new_v_q_a_norm': 'new_v', 'new_v_kv_a_norm': 'new_v', 'new_v_w_uq': 'new_v', 'new_v_w_ukv': 'new_v', 'new_v_w_o_mla': 'new_v', 'new_v_conv_w': 'new_v', 'new_v_A_log': 'new_v', 'new_v_dt_bias': 'new_v', 'new_v_gdn_norm': 'new_v', 'new_v_w_o_gdn': 'new_v', 'new_v_w_o': 'new_v', 'new_v_w_gate_up': 'new_v', 'new_v_w_down': 'new_v', 'new_v_final_norm': 'new_v'}


def _forward(args):
    return _fwd_reference(*[args[k] for k in FWD_PARAMS])


def _output_shape():
    out = _jax.eval_shape(lambda: _forward(_fwd_setup_inputs(0)))
    return out.shape, out.dtype

N_MICROBATCH = 1
ADAM_LR = 0.001
ADAM_B1 = 0.9
ADAM_B2 = 0.999
ADAM_EPS = 1e-08
ADAM_WD = 0.01
ADAM_STEP = 10
PER_EXAMPLE_BATCH_AXIS = {'x': 0, 'c': 0, 'positions': 0, 'loss_target': 0}
SHARED_INPUTS = []
_WEIGHT_DTYPES = {'w_ada': _jnp.float32, 'b_ada': _jnp.float32, 'norm_mix': _jnp.float32, 'norm_ffn': _jnp.float32, 'w_in': _jnp.float32, 'q_a_norm': _jnp.float32, 'kv_a_norm': _jnp.float32, 'w_uq': _jnp.float32, 'w_ukv': _jnp.float32, 'w_o_mla': _jnp.float32, 'conv_w': _jnp.float32, 'A_log': _jnp.float32, 'dt_bias': _jnp.float32, 'gdn_norm': _jnp.float32, 'w_o_gdn': _jnp.float32, 'w_o': _jnp.float32, 'w_gate_up': _jnp.float32, 'w_down': _jnp.float32, 'final_norm': _jnp.float32}
MOMENT_SCALE = {'w_ada': 1.524433e-02, 'b_ada': 2.585075e-02, 'norm_mix': 1.143730e-02, 'norm_ffn': 1.832538e-02, 'w_in': 5.833526e-03, 'q_a_norm': 3.685971e-03, 'kv_a_norm': 8.040958e-03, 'w_uq': 2.068437e-03, 'w_ukv': 4.022176e-03, 'w_o_mla': 3.696968e-03, 'conv_w': 7.164805e-03, 'A_log': 3.848506e-02, 'dt_bias': 3.842281e-02, 'gdn_norm': 2.823911e-02, 'w_o_gdn': 6.821749e-03, 'w_o': 7.718621e-03, 'w_gate_up': 8.085250e-03, 'w_down': 1.319651e-02, 'final_norm': 8.002788e+00}


def _to_microbatches(a, axis):
    t = _jnp.moveaxis(a, axis, 0)
    t = t.reshape((N_MICROBATCH, t.shape[0] // N_MICROBATCH) + t.shape[1:])
    return _jnp.moveaxis(t, 1, axis + 1)


def setup_inputs(seed: int = 0) -> dict:
    inp = _fwd_setup_inputs(seed)
    key = _jax.random.fold_in(_jax.random.key(seed), 7919)
    shape, _ = _output_shape()
    out = dict(inp)
    out["loss_target"] = _jax.random.normal(_jax.random.fold_in(key, 0), shape, _jnp.float32)
    for i, name in enumerate(TWIN_WEIGHTS):
        w = inp[name].astype(_jnp.float32)
        if MOMENT_SCALE is None:
            s = _jnp.sqrt(_jnp.mean(_jnp.square(w)) + 1e-30)
        else:
            s = MOMENT_SCALE[name]
        km, kv = _jax.random.split(_jax.random.fold_in(key, i + 1))
        out[name] = w
        out["m_" + name] = s * _jax.random.normal(km, w.shape, _jnp.float32)
        out["v_" + name] = (s * s) * _jax.random.uniform(kv, w.shape, _jnp.float32, 0.5, 1.5)
    if N_MICROBATCH > 1:
        for name, axis in PER_EXAMPLE_BATCH_AXIS.items():
            out[name] = _to_microbatches(out[name], axis)
    return {'x': out['x'], 'c': out['c'], 'positions': out['positions'], 'w_ada': out['w_ada'], 'b_ada': out['b_ada'], 'norm_mix': out['norm_mix'], 'norm_ffn': out['norm_ffn'], 'w_in': out['w_in'], 'q_a_norm': out['q_a_norm'], 'kv_a_norm': out['kv_a_norm'], 'w_uq': out['w_uq'], 'w_ukv': out['w_ukv'], 'w_o_mla': out['w_o_mla'], 'conv_w': out['conv_w'], 'A_log': out['A_log'], 'dt_bias': out['dt_bias'], 'gdn_norm': out['gdn_norm'], 'w_o_gdn': out['w_o_gdn'], 'w_o': out['w_o'], 'w_gate_up': out['w_gate_up'], 'w_down': out['w_down'], 'final_norm': out['final_norm'], 'loss_target': out['loss_target'], 'm_w_ada': out['m_w_ada'], 'm_b_ada': out['m_b_ada'], 'm_norm_mix': out['m_norm_mix'], 'm_norm_ffn': out['m_norm_ffn'], 'm_w_in': out['m_w_in'], 'm_q_a_norm': out['m_q_a_norm'], 'm_kv_a_norm': out['m_kv_a_norm'], 'm_w_uq': out['m_w_uq'], 'm_w_ukv': out['m_w_ukv'], 'm_w_o_mla': out['m_w_o_mla'], 'm_conv_w': out['m_conv_w'], 'm_A_log': out['m_A_log'], 'm_dt_bias': out['m_dt_bias'], 'm_gdn_norm': out['m_gdn_norm'], 'm_w_o_gdn': out['m_w_o_gdn'], 'm_w_o': out['m_w_o'], 'm_w_gate_up': out['m_w_gate_up'], 'm_w_down': out['m_w_down'], 'm_final_norm': out['m_final_norm'], 'v_w_ada': out['v_w_ada'], 'v_b_ada': out['v_b_ada'], 'v_norm_mix': out['v_norm_mix'], 'v_norm_ffn': out['v_norm_ffn'], 'v_w_in': out['v_w_in'], 'v_q_a_norm': out['v_q_a_norm'], 'v_kv_a_norm': out['v_kv_a_norm'], 'v_w_uq': out['v_w_uq'], 'v_w_ukv': out['v_w_ukv'], 'v_w_o_mla': out['v_w_o_mla'], 'v_conv_w': out['v_conv_w'], 'v_A_log': out['v_A_log'], 'v_dt_bias': out['v_dt_bias'], 'v_gdn_norm': out['v_gdn_norm'], 'v_w_o_gdn': out['v_w_o_gdn'], 'v_w_o': out['v_w_o'], 'v_w_gate_up': out['v_w_gate_up'], 'v_w_down': out['v_w_down'], 'v_final_norm': out['v_final_norm']}


def _loss(weights, diff, rest, loss_target):
    with _jax.named_scope("forward"):
        args = {**rest, TWIN_DIFF_INPUT: diff, **{k: w.astype(_WEIGHT_DTYPES[k]) for k, w in weights.items()}}
        y = _forward(args)
    with _jax.named_scope("loss_head"):
        err = _jnp.square(y.astype(_jnp.float32) - loss_target)
        return 0.5 * _jnp.sum(_jnp.mean(err, axis=-1)) if err.ndim else 0.5 * err


def _adamw(w, g, m, v):
    m = ADAM_B1 * m + (1.0 - ADAM_B1) * g
    v = ADAM_B2 * v + (1.0 - ADAM_B2) * _jnp.square(g)
    m_hat = m / (1.0 - ADAM_B1 ** ADAM_STEP)
    v_hat = v / (1.0 - ADAM_B2 ** ADAM_STEP)
    delta = -ADAM_LR * (m_hat / (_jnp.sqrt(v_hat) + ADAM_EPS) + ADAM_WD * w)
    return delta, m, v


def reference(x, c, positions, w_ada, b_ada, norm_mix, norm_ffn, w_in, q_a_norm, kv_a_norm, w_uq, w_ukv, w_o_mla, conv_w, A_log, dt_bias, gdn_norm, w_o_gdn, w_o, w_gate_up, w_down, final_norm, loss_target, m_w_ada, m_b_ada, m_norm_mix, m_norm_ffn, m_w_in, m_q_a_norm, m_kv_a_norm, m_w_uq, m_w_ukv, m_w_o_mla, m_conv_w, m_A_log, m_dt_bias, m_gdn_norm, m_w_o_gdn, m_w_o, m_w_gate_up, m_w_down, m_final_norm, v_w_ada, v_b_ada, v_norm_mix, v_norm_ffn, v_w_in, v_q_a_norm, v_kv_a_norm, v_w_uq, v_w_ukv, v_w_o_mla, v_conv_w, v_A_log, v_dt_bias, v_gdn_norm, v_w_o_gdn, v_w_o, v_w_gate_up, v_w_down, v_final_norm):
    given = dict(x=x, c=c, positions=positions, w_ada=w_ada, b_ada=b_ada, norm_mix=norm_mix, norm_ffn=norm_ffn, w_in=w_in, q_a_norm=q_a_norm, kv_a_norm=kv_a_norm, w_uq=w_uq, w_ukv=w_ukv, w_o_mla=w_o_mla, conv_w=conv_w, A_log=A_log, dt_bias=dt_bias, gdn_norm=gdn_norm, w_o_gdn=w_o_gdn, w_o=w_o, w_gate_up=w_gate_up, w_down=w_down, final_norm=final_norm, loss_target=loss_target, m_w_ada=m_w_ada, m_b_ada=m_b_ada, m_norm_mix=m_norm_mix, m_norm_ffn=m_norm_ffn, m_w_in=m_w_in, m_q_a_norm=m_q_a_norm, m_kv_a_norm=m_kv_a_norm, m_w_uq=m_w_uq, m_w_ukv=m_w_ukv, m_w_o_mla=m_w_o_mla, m_conv_w=m_conv_w, m_A_log=m_A_log, m_dt_bias=m_dt_bias, m_gdn_norm=m_gdn_norm, m_w_o_gdn=m_w_o_gdn, m_w_o=m_w_o, m_w_gate_up=m_w_gate_up, m_w_down=m_w_down, m_final_norm=m_final_norm, v_w_ada=v_w_ada, v_b_ada=v_b_ada, v_norm_mix=v_norm_mix, v_norm_ffn=v_norm_ffn, v_w_in=v_w_in, v_q_a_norm=v_q_a_norm, v_kv_a_norm=v_kv_a_norm, v_w_uq=v_w_uq, v_w_ukv=v_w_ukv, v_w_o_mla=v_w_o_mla, v_conv_w=v_conv_w, v_A_log=v_A_log, v_dt_bias=v_dt_bias, v_gdn_norm=v_gdn_norm, v_w_o_gdn=v_w_o_gdn, v_w_o=v_w_o, v_w_gate_up=v_w_gate_up, v_w_down=v_w_down, v_final_norm=v_final_norm)
    weights = {n: given[n] for n in TWIN_WEIGHTS}
    shared = {n: given[n] for n in SHARED_INPUTS}
    per_example = {n: given[n] for n in ['x', 'c', 'positions']}
    grad_fn = _jax.value_and_grad(_loss, argnums=(0, 1))

    def one_microbatch(ex, loss_target):
        ex = dict(ex)
        diff = ex.pop(TWIN_DIFF_INPUT)
        return grad_fn(weights, diff, {**shared, **ex}, loss_target)

    if N_MICROBATCH == 1:
        loss, (grad_w, grad_x) = one_microbatch(per_example, given["loss_target"])
    else:
        def body(carry, xs):
            loss_sum, grad_sum = carry
            l_k, (gw_k, gx_k) = one_microbatch(xs[0], xs[1])
            with _jax.named_scope("update"):
                return (loss_sum + l_k, _jax.tree.map(_jnp.add, grad_sum, gw_k)), gx_k

        init = (_jnp.zeros((), _jnp.float32), _jax.tree.map(_jnp.zeros_like, weights))
        (loss, grad_w), grad_x = _jax.lax.scan(body, init, (per_example, given["loss_target"]))
    with _jax.named_scope("update"):
        delta_w, new_m, new_v = {}, {}, {}
        for n in TWIN_WEIGHTS:
            delta_w[n], new_m[n], new_v[n] = _adamw(weights[n], grad_w[n], given["m_" + n], given["v_" + n])
    return (loss, grad_x, *[grad_w[n] for n in TWIN_WEIGHTS], *[delta_w[n] for n in TWIN_WEIGHTS],
            *[new_m[n] for n in TWIN_WEIGHTS], *[new_v[n] for n in TWIN_WEIGHTS])
```

```python
import functools
import math

import jax
import jax.numpy as jnp
from jax import lax
from jax.experimental import pallas as pl
from jax.experimental.pallas import tpu as pltpu

F32 = jnp.float32
BF16 = jnp.bfloat16
SDS = jax.ShapeDtypeStruct
MESH = pl.DeviceIdType.MESH
AXES = ("x", "y", "c")

EPS = 1e-6
HEADS = 8
D_NOPE = 128
D_ROPE = 64
D_QK = 256
D_V = 128
Q_LORA = 512
KV_LORA = 512
CHUNK = 64
CONV_WIDTH = 4
GDN_W = HEADS * D_V
MAIN_FIXED = 2 * Q_LORA + 4 * GDN_W
LANES = 128
VMEM_LIMIT = 56 * 1024 * 1024
ADAM_LR, ADAM_B1, ADAM_B2, ADAM_EPS, ADAM_WD, ADAM_STEP = 0.001, 0.9, 0.999, 1e-8, 0.01, 10


def _pick(n, cands):
    for cand in cands:
        if n % cand == 0:
            return cand
    return n


def _cp(sem):
    return pltpu.CompilerParams(dimension_semantics=sem, vmem_limit_bytes=VMEM_LIMIT)


def _row_tile(t):
    return _pick(t, (256, 128, 64, 32, 16, 8))


def _dot(a, b):
    return jnp.dot(a.astype(BF16), b.astype(BF16), preferred_element_type=F32)


def _dot_nt(a, b):
    return lax.dot_general(a.astype(BF16), b.astype(BF16), (((1,), (1,)), ((), ())), preferred_element_type=F32)


def _dot_tn(a, b):
    return lax.dot_general(a.astype(BF16), b.astype(BF16), (((0,), (0,)), ((), ())), preferred_element_type=F32)


def _bdot(a, b, exact=False):
    if exact:
        return lax.dot_general(a, b, (((2,), (1,)), ((0,), (0,))), precision=lax.Precision.HIGHEST,
                               preferred_element_type=F32)
    return lax.dot_general(a.astype(BF16), b.astype(BF16), (((2,), (1,)), ((0,), (0,))), preferred_element_type=F32)


def _bdot_nt(a, b):
    return lax.dot_general(a.astype(BF16), b.astype(BF16), (((2,), (2,)), ((0,), (0,))), preferred_element_type=F32)


def _sigmoid(x):
    return 1.0 / (1.0 + jnp.exp(-x))


def _mm(a, b, *, tb=False, out_dtype=F32, acc_in=None, name):
    m, k = a.shape
    n = b.shape[0] if tb else b.shape[1]
    assert (b.shape[1] if tb else b.shape[0]) == k
    tm = _pick(m, (1024, 512, 256, 128))
    tn = _pick(n, (1024, 512, 256, 128))
    tk = k if k <= 2048 else _pick(k, (512, 256, 128))
    nk = k // tk
    has_acc = acc_in is not None

    def body_one_step(*refs):
        a_ref, b_ref = refs[:2]
        o_ref = refs[-1]
        acc = _dot_nt(a_ref[...], b_ref[...]) if tb else _dot(a_ref[...], b_ref[...])
        if has_acc:
            acc = acc + refs[2][...].astype(F32)
        o_ref[...] = acc.astype(out_dtype)

    if nk == 1:
        in_specs = [pl.BlockSpec((tm, k), lambda i, j: (i, 0)),
                    pl.BlockSpec((tn, k), lambda i, j: (j, 0)) if tb else pl.BlockSpec((k, tn), lambda i, j: (0, j))]
        args = [a, b]
        if has_acc:
            in_specs.append(pl.BlockSpec((tm, tn), lambda i, j: (i, j)))
            args.append(acc_in)
        return pl.pallas_call(
            body_one_step, grid=(m // tm, n // tn), in_specs=in_specs, out_specs=pl.BlockSpec((tm, tn), lambda i, j: (i, j)),
            out_shape=SDS((m, n), out_dtype), name=name, compiler_params=_cp(("parallel", "parallel")))(*args)

    def body(*refs):
        if has_acc:
            a_ref, b_ref, c_ref, o_ref, acc = refs
        else:
            a_ref, b_ref, o_ref, acc = refs
        kk = pl.program_id(2)

        @pl.when(kk == 0)
        def _():
            if has_acc:
                acc[...] = c_ref[...].astype(F32)
            else:
                acc[...] = jnp.zeros_like(acc)

        if tb:
            acc[...] += _dot_nt(a_ref[...], b_ref[...])
        else:
            acc[...] += _dot(a_ref[...], b_ref[...])

        @pl.when(kk == nk - 1)
        def _():
            o_ref[...] = acc[...].astype(out_dtype)

    in_specs = [pl.BlockSpec((tm, tk), lambda i, j, kk: (i, kk)),
                pl.BlockSpec((tn, tk), lambda i, j, kk: (j, kk)) if tb
                else pl.BlockSpec((tk, tn), lambda i, j, kk: (kk, j))]
    args = [a, b]
    if has_acc:
        in_specs.append(pl.BlockSpec((tm, tn), lambda i, j, kk: (i, j)))
        args.append(acc_in)
    return pl.pallas_call(
        body, grid=(m // tm, n // tn, nk), in_specs=in_specs,
        out_specs=pl.BlockSpec((tm, tn), lambda i, j, kk: (i, j)),
        out_shape=SDS((m, n), out_dtype), scratch_shapes=[pltpu.VMEM((tm, tn), F32)],
        name=name, compiler_params=_cp(("parallel", "parallel", "arbitrary")))(*args)


def _norm_fwd(x, col, width, w, sc, sh, name):
    t = x.shape[0]
    tm = _row_tile(t)

    def body(x_ref, w_ref, sc_ref, sh_ref, o_ref):
        xv = x_ref[...]
        r = lax.rsqrt(jnp.mean(xv * xv, axis=-1, keepdims=True) + EPS)
        n = xv * r * w_ref[...]
        o_ref[...] = (n * (1.0 + sc_ref[...]) + sh_ref[...]).astype(o_ref.dtype)

    row = pl.BlockSpec((1, width), lambda i: (0, 0))
    return pl.pallas_call(
        body, grid=(t // tm,), in_specs=[pl.BlockSpec((tm, width), lambda i: (i, col)), row, row, row],
        out_specs=pl.BlockSpec((tm, width), lambda i: (i, 0)), out_shape=SDS((t, width), BF16),
        name=name, compiler_params=_cp(("parallel",)))(x, w, sc, sh)


def _norm_bwd(x, col, width, w, sc, dh, dres, out_dtype, name):
    t = x.shape[0]
    tm = _row_tile(t)
    has_res = dres is not None

    def body(*refs):
        if has_res:
            x_ref, w_ref, sc_ref, dh_ref, dres_ref, dx_ref, dw_ref, dsc_ref, dsh_ref = refs
        else:
            x_ref, w_ref, sc_ref, dh_ref, dx_ref, dw_ref, dsc_ref, dsh_ref = refs

        @pl.when(pl.program_id(0) == 0)
        def _():
            dw_ref[...] = jnp.zeros_like(dw_ref)
            dsc_ref[...] = jnp.zeros_like(dsc_ref)
            dsh_ref[...] = jnp.zeros_like(dsh_ref)

        xv = x_ref[...]
        dhv = dh_ref[...].astype(F32)
        wv = w_ref[...]
        r = lax.rsqrt(jnp.mean(xv * xv, axis=-1, keepdims=True) + EPS)
        xh = xv * r
        n = xh * wv
        dsh_ref[...] += jnp.sum(dhv, axis=0, keepdims=True)
        dsc_ref[...] += jnp.sum(dhv * n, axis=0, keepdims=True)
        dn = dhv * (1.0 + sc_ref[...])
        dw_ref[...] += jnp.sum(dn * xh, axis=0, keepdims=True)
        dxh = dn * wv
        dx = r * (dxh - xh * jnp.mean(dxh * xh, axis=-1, keepdims=True))
        if has_res:
            dx = dx + dres_ref[...]
        dx_ref[...] = dx.astype(out_dtype)

    row = pl.BlockSpec((1, width), lambda i: (0, 0))
    blk = pl.BlockSpec((tm, width), lambda i: (i, 0))
    in_specs = [pl.BlockSpec((tm, width), lambda i: (i, col)), row, row, blk]
    args = [x, w, sc, dh]
    if has_res:
        in_specs.append(blk)
        args.append(dres)
    return pl.pallas_call(
        body, grid=(t // tm,), in_specs=in_specs, out_specs=[blk, row, row, row],
        out_shape=[SDS((t, width), out_dtype), SDS((1, width), F32), SDS((1, width), F32), SDS((1, width), F32)],
        name=name, compiler_params=_cp(("arbitrary",)))(*args)


def _rope128(x, tc, ts1, ts2):
    return x * tc + pltpu.roll(x, 96, 1) * ts1 + pltpu.roll(x, 32, 1) * ts2


def _rope128_t(d, tc, ts1, ts2):
    return d * tc + pltpu.roll(d * ts1, 32, 1) + pltpu.roll(d * ts2, 96, 1)


def _rope_q(q_raw, tc, ts1, ts2, transpose, out_dtype, name):
    t = q_raw.shape[0]
    tm = _row_tile(t)

    def body(q_ref, tc_ref, s1_ref, s2_ref, o_ref):
        fn = _rope128_t if transpose else _rope128
        for h in range(HEADS):
            base = h * D_QK
            o_ref[:, base:base + LANES] = q_ref[:, base:base + LANES].astype(out_dtype)
            x = q_ref[:, base + LANES:base + D_QK].astype(F32)
            o_ref[:, base + LANES:base + D_QK] = fn(x, tc_ref[...], s1_ref[...], s2_ref[...]).astype(out_dtype)

    blk = pl.BlockSpec((tm, HEADS * D_QK), lambda i: (i, 0))
    tab = pl.BlockSpec((tm, LANES), lambda i: (i, 0))
    return pl.pallas_call(
        body, grid=(t // tm,), in_specs=[blk, tab, tab, tab], out_specs=blk,
        out_shape=SDS((t, HEADS * D_QK), out_dtype), name=name, compiler_params=_cp(("parallel",)))(q_raw, tc, ts1, ts2)


def _k_assemble(kv_raw, p_small, tc, ts1, ts2, name):
    t = kv_raw.shape[0]
    tm = _row_tile(t)

    def body(kn_ref, ps_ref, tc_ref, s1_ref, s2_ref, o_ref):
        kpe = _rope128(ps_ref[...], tc_ref[...], s1_ref[...], s2_ref[...]).astype(BF16)
        for h in range(HEADS):
            o_ref[:, h * D_QK:h * D_QK + LANES] = kn_ref[:, h * LANES:(h + 1) * LANES].astype(BF16)
            o_ref[:, h * D_QK + LANES:(h + 1) * D_QK] = kpe

    tab = pl.BlockSpec((tm, LANES), lambda i: (i, 0))
    return pl.pallas_call(
        body, grid=(t // tm,),
        in_specs=[pl.BlockSpec((tm, HEADS * LANES), lambda i: (i, 0)), tab, tab, tab, tab],
        out_specs=pl.BlockSpec((tm, HEADS * D_QK), lambda i: (i, 0)),
        out_shape=SDS((t, HEADS * D_QK), BF16), name=name, compiler_params=_cp(("parallel",)))(kv_raw, p_small, tc, ts1, ts2)


def _k_assemble_bwd(dk, dv, tc, ts1, ts2, name):
    t = dk.shape[0]
    tm = _row_tile(t)

    def body(dk_ref, dv_ref, tc_ref, s1_ref, s2_ref, o_ref, pe_ref):
        acc = jnp.zeros((tm, LANES), F32)
        for h in range(HEADS):
            o_ref[:, h * LANES:(h + 1) * LANES] = dk_ref[:, h * D_QK:h * D_QK + LANES].astype(BF16)
            acc = acc + dk_ref[:, h * D_QK + LANES:(h + 1) * D_QK].astype(F32)
        o_ref[:, HEADS * LANES:] = dv_ref[...].astype(BF16)
        pe_ref[...] = _rope128_t(acc, tc_ref[...], s1_ref[...], s2_ref[...])

    tab = pl.BlockSpec((tm, LANES), lambda i: (i, 0))
    return pl.pallas_call(
        body, grid=(t // tm,),
        in_specs=[pl.BlockSpec((tm, HEADS * D_QK), lambda i: (i, 0)), pl.BlockSpec((tm, HEADS * LANES), lambda i: (i, 0)),
                  tab, tab, tab],
        out_specs=[pl.BlockSpec((tm, 2 * HEADS * LANES), lambda i: (i, 0)), tab],
        out_shape=[SDS((t, 2 * HEADS * LANES), BF16), SDS((t, LANES), F32)],
        name=name, compiler_params=_cp(("parallel",)))(dk, dv, tc, ts1, ts2)


def _attn_tile(t):
    return _pick(t, (256, 128, 64))


def _attn_fwd(q, k, v, v_off, name):
    t = q.shape[0]
    tq = _attn_tile(t)
    scale = (D_NOPE + D_ROPE) ** -0.5

    def body(q_ref, k_ref, v_ref, o_ref, lse_ref):
        for i in range(t // tq):
            n_k = (i + 1) * tq
            s = _dot_nt(q_ref[i * tq:(i + 1) * tq, :], k_ref[0:n_k, :]) * scale
            row = lax.broadcasted_iota(jnp.int32, (tq, n_k), 0) + i * tq
            colv = lax.broadcasted_iota(jnp.int32, (tq, n_k), 1)
            s = jnp.where(colv <= row, s, -jnp.inf)
            m = jnp.max(s, axis=-1, keepdims=True)
            p = jnp.exp(s - m)
            l = jnp.sum(p, axis=-1, keepdims=True)
            o = _dot(p, v_ref[0:n_k, :]) / l
            o_ref[i * tq:(i + 1) * tq, :] = o.astype(BF16)
            lse_ref[0, i * tq:(i + 1) * tq, :] = m + jnp.log(l)

    return pl.pallas_call(
        body, grid=(HEADS,),
        in_specs=[pl.BlockSpec((t, D_QK), lambda h: (0, h)), pl.BlockSpec((t, D_QK), lambda h: (0, h)),
                  pl.BlockSpec((t, D_V), lambda h: (0, v_off + h))],
        out_specs=[pl.BlockSpec((t, D_V), lambda h: (0, h)), pl.BlockSpec((1, t, 1), lambda h: (h, 0, 0))],
        out_shape=[SDS((t, HEADS * D_V), BF16), SDS((HEADS, t, 1), F32)],
        name=name, compiler_params=_cp(("parallel",)))(q, k, v)


def _attn_bwd(q, k, v, v_off, o, lse, do, name):
    t = q.shape[0]
    tq = _attn_tile(t)
    scale = (D_NOPE + D_ROPE) ** -0.5

    def body(q_ref, k_ref, v_ref, o_ref, lse_ref, do_ref, dq_ref, dk_ref, dv_ref):
        dk_ref[...] = jnp.zeros_like(dk_ref)
        dv_ref[...] = jnp.zeros_like(dv_ref)
        for i in range(t // tq):
            n_k = (i + 1) * tq
            rows = slice(i * tq, (i + 1) * tq)
            qi = q_ref[rows, :]
            doi = do_ref[rows, :].astype(F32)
            s = _dot_nt(qi, k_ref[0:n_k, :]) * scale
            row = lax.broadcasted_iota(jnp.int32, (tq, n_k), 0) + i * tq
            colv = lax.broadcasted_iota(jnp.int32, (tq, n_k), 1)
            p = jnp.where(colv <= row, jnp.exp(s - lse_ref[0, rows, :]), 0.0)
            dp = _dot_nt(doi, v_ref[0:n_k, :])
            delta = jnp.sum(doi * o_ref[rows, :].astype(F32), axis=-1, keepdims=True)
            ds = p * (dp - delta) * scale
            dq_ref[rows, :] = _dot(ds, k_ref[0:n_k, :])
            dk_ref[0:n_k, :] += _dot_tn(ds, qi)
            dv_ref[0:n_k, :] += _dot_tn(p, doi)

    qk_spec = pl.BlockSpec((t, D_QK), lambda h: (0, h))
    v_spec = pl.BlockSpec((t, D_V), lambda h: (0, h))
    return pl.pallas_call(
        body, grid=(HEADS,),
        in_specs=[qk_spec, qk_spec, pl.BlockSpec((t, D_V), lambda h: (0, v_off + h)), v_spec,
                  pl.BlockSpec((1, t, 1), lambda h: (h, 0, 0)), v_spec],
        out_specs=[qk_spec, qk_spec, v_spec],
        out_shape=[SDS((t, HEADS * D_QK), F32), SDS((t, HEADS * D_QK), F32), SDS((t, HEADS * D_V), F32)],
        name=name, compiler_params=_cp(("parallel",)))(q, k, v, o, lse, do)


CONV_COLS = 256


def _conv_pre(u, w_ref, rowi):
    acc = u * w_ref[CONV_WIDTH - 1:CONV_WIDTH, :]
    for sft in range(1, CONV_WIDTH):
        shifted = jnp.where(rowi >= sft, pltpu.roll(u, sft, 0), 0.0)
        acc = acc + shifted * w_ref[CONV_WIDTH - 1 - sft:CONV_WIDTH - sft, :]
    return acc


def _conv_fwd(p_main, conv_w, name):
    t = p_main.shape[0]
    off = 2 * Q_LORA // CONV_COLS

    def body(u_ref, w_ref, y_ref):
        u = u_ref[...]
        rowi = lax.broadcasted_iota(jnp.int32, u.shape, 0)
        pre = _conv_pre(u, w_ref, rowi)
        y_ref[...] = pre * _sigmoid(pre)

    return pl.pallas_call(
        body, grid=(3 * GDN_W // CONV_COLS,),
        in_specs=[pl.BlockSpec((t, CONV_COLS), lambda j: (0, off + j)), pl.BlockSpec((CONV_WIDTH, CONV_COLS), lambda j: (0, j))],
        out_specs=pl.BlockSpec((t, CONV_COLS), lambda j: (0, j)), out_shape=SDS((t, 3 * GDN_W), F32),
        name=name, compiler_params=_cp(("parallel",)))(p_main, conv_w)


def _conv_bwd(p_main, conv_w, dyc, name):
    t = p_main.shape[0]
    off = 2 * Q_LORA // CONV_COLS

    def body(u_ref, w_ref, dy_ref, du_ref, dw_ref):
        u = u_ref[...]
        rowi = lax.broadcasted_iota(jnp.int32, u.shape, 0)
        pre = _conv_pre(u, w_ref, rowi)
        sg = _sigmoid(pre)
        dpre = dy_ref[...] * sg * (1.0 + pre * (1.0 - sg))
        du = dpre * w_ref[CONV_WIDTH - 1:CONV_WIDTH, :]
        dw_ref[CONV_WIDTH - 1:CONV_WIDTH, :] = jnp.sum(dpre * u, axis=0, keepdims=True)
        for sft in range(1, CONV_WIDTH):
            back = jnp.where(rowi < t - sft, pltpu.roll(dpre, t - sft, 0), 0.0)
            du = du + back * w_ref[CONV_WIDTH - 1 - sft:CONV_WIDTH - sft, :]
            shifted = jnp.where(rowi >= sft, pltpu.roll(u, sft, 0), 0.0)
            dw_ref[CONV_WIDTH - 1 - sft:CONV_WIDTH - sft, :] = jnp.sum(dpre * shifted, axis=0, keepdims=True)
        du_ref[...] = du.astype(BF16)

    blk = pl.BlockSpec((t, CONV_COLS), lambda j: (0, j))
    wblk = pl.BlockSpec((CONV_WIDTH, CONV_COLS), lambda j: (0, j))
    return pl.pallas_call(
        body, grid=(3 * GDN_W // CONV_COLS,),
        in_specs=[pl.BlockSpec((t, CONV_COLS), lambda j: (0, off + j)), wblk, blk],
        out_specs=[blk, wblk], out_shape=[SDS((t, 3 * GDN_W), BF16), SDS((CONV_WIDTH, 3 * GDN_W), F32)],
        name=name, compiler_params=_cp(("parallel",)))(p_main, conv_w, dyc)


B_LO, A_LO, A_HI = D_ROPE, D_ROPE + HEADS, D_ROPE + 2 * HEADS


def _softplus(z):
    e = jnp.exp(-jnp.abs(z))
    log1p = jnp.where(e < 0.01, e * (1.0 - e * (0.5 - e * (1.0 / 3.0))), jnp.log(1.0 + e))
    return jnp.maximum(z, 0.0) + log1p


def _gdn_gates(p_small, a_row, dt_row, name):
    t = p_small.shape[0]

    def body(ps_ref, a_ref, dt_ref, g_ref, gc_ref):
        x = ps_ref[...]
        lane = lax.broadcasted_iota(jnp.int32, x.shape, 1)
        is_g = (lane >= A_LO) & (lane < A_HI)
        g = jnp.where(is_g, -jnp.exp(a_ref[...]) * _softplus(x + dt_ref[...]), 0.0)
        g_ref[...] = jnp.where(is_g, g, _sigmoid(x))
        pos = lax.broadcasted_iota(jnp.int32, x.shape, 0) % CHUNK
        acc = g
        sft = 1
        while sft < CHUNK:
            acc = acc + jnp.where(pos >= sft, pltpu.roll(acc, sft, 0), 0.0)
            sft *= 2
        gc_ref[...] = acc

    full = pl.BlockSpec((t, LANES), lambda i: (0, 0))
    row = pl.BlockSpec((1, LANES), lambda i: (0, 0))
    return pl.pallas_call(
        body, grid=(1,), in_specs=[full, row, row], out_specs=[full, full],
        out_shape=[SDS((t, LANES), F32), SDS((t, LANES), F32)], name=name,
        compiler_params=_cp(("arbitrary",)))(p_small, a_row, dt_row)


def _gdn_gates_bwd(p_small, a_row, dt_row, gates, dgates, dkpe, name):
    t = p_small.shape[0]

    def body(ps_ref, a_ref, dt_ref, g_ref, db_ref, dkpe_ref, dp_ref, da_ref, ddt_ref):
        x = ps_ref[...]
        lane = lax.broadcasted_iota(jnp.int32, x.shape, 1)
        is_g = (lane >= A_LO) & (lane < A_HI)
        is_b = (lane >= B_LO) & (lane < A_LO)
        pos = lax.broadcasted_iota(jnp.int32, x.shape, 0) % CHUNK
        acc = jnp.where(is_g, db_ref[...], 0.0)
        sft = 1
        while sft < CHUNK:
            acc = acc + jnp.where(pos < CHUNK - sft, pltpu.roll(acc, t - sft, 0), 0.0)
            sft *= 2
        dg = acc
        gv = g_ref[...]
        dz = jnp.where(is_g, dg * (-jnp.exp(a_ref[...])) * _sigmoid(x + dt_ref[...]), 0.0)
        da_ref[...] = jnp.sum(jnp.where(is_g, dg * gv, 0.0), axis=0, keepdims=True)
        ddt_ref[...] = jnp.sum(dz, axis=0, keepdims=True)
        dlb = jnp.where(is_b, db_ref[...] * gv * (1.0 - gv), 0.0)
        dp_ref[...] = (jnp.where(lane < D_ROPE, dkpe_ref[...], 0.0) + dlb + dz).astype(BF16)

    full = pl.BlockSpec((t, LANES), lambda i: (0, 0))
    row = pl.BlockSpec((1, LANES), lambda i: (0, 0))
    return pl.pallas_call(
        body, grid=(1,), in_specs=[full, row, row, full, full, full], out_specs=[full, row, row],
        out_shape=[SDS((t, LANES), BF16), SDS((1, LANES), F32), SDS((1, LANES), F32)], name=name,
        compiler_params=_cp(("arbitrary",)))(p_small, a_row, dt_row, gates, dgates, dkpe)


def _tri_inv(l, eye):
    x = eye - l
    p = _bdot(l, l, exact=True)
    steps = int(math.log2(CHUNK)) - 1
    for s in range(steps):
        x = x + _bdot(x, p, exact=True)
        if s < steps - 1:
            p = _bdot(p, p, exact=True)
    return x


def _l2n(x3):
    r = lax.rsqrt(jnp.sum(x3 * x3, axis=-1, keepdims=True) + EPS)
    return x3 * r, r


def _head_col(a_ref, lane_lo, n):
    a = a_ref[...]
    lane = lax.broadcasted_iota(jnp.int32, a.shape, 1)
    col = jnp.sum(jnp.where(lane == lane_lo + pl.program_id(0), a, 0.0), axis=-1, keepdims=True)
    return col.reshape(n, CHUNK, 1)


def _gdn_common(q3, k3, v3, b, gc):
    n = q3.shape[0]
    ri = lax.broadcasted_iota(jnp.int32, (n, CHUNK, CHUNK), 1)
    ci = lax.broadcasted_iota(jnp.int32, (n, CHUNK, CHUNK), 2)
    lower, strict = ri >= ci, ri > ci
    eye = (ri == ci).astype(F32)
    gr = jnp.sum(gc * eye, axis=1, keepdims=True)
    qh, rq = _l2n(q3)
    qn = qh * (D_V ** -0.5)
    kn, rk = _l2n(k3)
    dec = jnp.where(lower, jnp.exp(jnp.where(lower, gc - gr, 0.0)), 0.0)
    kb = kn * b
    mm = _bdot_nt(kb, kn)
    tinv = _tri_inv(jnp.where(strict, mm * dec, 0.0), eye)
    gam = jnp.exp(gc)
    u = _bdot(tinv, v3 * b, exact=True)
    w = _bdot(tinv, kb * gam, exact=True)
    qk = _bdot_nt(qn, kn)
    aqk = jnp.where(lower, qk * dec, 0.0)
    gl = gc[:, CHUNK - 1:CHUNK, :]
    kdf = jnp.exp(gl - gc)
    return dict(ri=ri, ci=ci, lower=lower, strict=strict, eye=eye, qh=qh, rq=rq, qn=qn, kn=kn, rk=rk, dec=dec, kb=kb,
                mm=mm, gam=gam, u=u, w=w, qk=qk, aqk=aqk, gl=gl, kdf=kdf, kd=kn * kdf, gr=gr)


def _gdn_fwd(yc, p_main, gates, gcum, gn, name):
    t = yc.shape[0]
    n = t // CHUNK
    z_off = (2 * Q_LORA + 3 * GDN_W) // D_V

    def body(q_ref, k_ref, v_ref, z_ref, gt_ref, gcum_ref, gn_ref, o_ref, g_ref, s_ref, u_s, w_s, qg_s, kd_s, a_s, e_s):
        c = _gdn_common(q_ref[...].reshape(n, CHUNK, D_V), k_ref[...].reshape(n, CHUNK, D_V),
                        v_ref[...].reshape(n, CHUNK, D_V), _head_col(gt_ref, B_LO, n), _head_col(gcum_ref, A_LO, n))
        u_s[...] = c["u"]
        w_s[...] = c["w"]
        qg_s[...] = c["qn"] * c["gam"]
        kd_s[...] = c["kd"]
        a_s[...] = c["aqk"]
        e_s[...] = jnp.broadcast_to(jnp.exp(c["gl"]), (n, 1, D_V))

        def step(i, s):
            s_ref[0, i] = s
            v_new = u_s[i] - _dot(w_s[i], s)
            o = _dot(qg_s[i], s) + _dot(a_s[i], v_new)
            o_ref[pl.ds(pl.multiple_of(i * CHUNK, CHUNK), CHUNK), :] = o
            return s * e_s[i] + _dot_tn(kd_s[i], v_new)

        lax.fori_loop(0, n, step, jnp.zeros((D_V, D_V), F32))
        o = o_ref[...]
        zz = z_ref[...]
        on = o * lax.rsqrt(jnp.mean(o * o, axis=-1, keepdims=True) + EPS) * gn_ref[...]
        g_ref[...] = (on * zz * _sigmoid(zz)).astype(BF16)

    col = lambda off: pl.BlockSpec((t, D_V), lambda h: (0, off + h))
    lanes = pl.BlockSpec((t, LANES), lambda h: (0, 0))
    big = pltpu.VMEM((n, CHUNK, D_V), F32)
    return pl.pallas_call(
        body, grid=(HEADS,),
        in_specs=[col(0), col(HEADS), col(2 * HEADS), col(z_off), lanes, lanes, pl.BlockSpec((1, D_V), lambda h: (0, 0))],
        out_specs=[col(0), col(0), pl.BlockSpec((1, n, D_V, D_V), lambda h: (h, 0, 0, 0))],
        out_shape=[SDS((t, GDN_W), F32), SDS((t, GDN_W), BF16), SDS((HEADS, n, D_V, D_V), F32)],
        scratch_shapes=[big, big, big, big, pltpu.VMEM((n, CHUNK, CHUNK), F32), pltpu.VMEM((n, 1, D_V), F32)],
        name=name, compiler_params=_cp(("parallel",)))(yc, yc, yc, p_main, gates, gcum, gn)


def _gdn_bwd(yc, p_main, gates, gcum, gn, o_raw, states, dgated, name):
    t = yc.shape[0]
    n = t // CHUNK
    z_off = (2 * Q_LORA + 3 * GDN_W) // D_V

    def body(q_ref, k_ref, v_ref, z_ref, gt_ref, gcum_ref, gn_ref, o_ref, s_ref, dg_ref,
             dq_ref, dk_ref, dv_ref, dz_ref, dgt_ref, dgn_ref,
             u_s, w_s, qg_s, kd_s, at_s, e_s, do_s, du_s, dw_s, dqg_s, dkd_s, da_s, dat_s, dgs_s):
        @pl.when(pl.program_id(0) == 0)
        def _():
            dgn_ref[...] = jnp.zeros_like(dgn_ref)
            dgt_ref[...] = jnp.zeros_like(dgt_ref)

        o = o_ref[...]
        zz = z_ref[...]
        dgv = dg_ref[...]
        gnv = gn_ref[...]
        r = lax.rsqrt(jnp.mean(o * o, axis=-1, keepdims=True) + EPS)
        oh = o * r
        sg = _sigmoid(zz)
        don = dgv * zz * sg
        dz_ref[...] = (dgv * oh * gnv * sg * (1.0 + zz * (1.0 - sg))).astype(BF16)
        dgn_ref[...] += jnp.sum(don * oh, axis=0, keepdims=True)
        doh = don * gnv
        do_s[...] = (r * (doh - oh * jnp.mean(doh * oh, axis=-1, keepdims=True))).reshape(n, CHUNK, D_V)

        q3 = q_ref[...].reshape(n, CHUNK, D_V)
        k3 = k_ref[...].reshape(n, CHUNK, D_V)
        v3 = v_ref[...].reshape(n, CHUNK, D_V)
        b, gc = _head_col(gt_ref, B_LO, n), _head_col(gcum_ref, A_LO, n)
        c = _gdn_common(q3, k3, v3, b, gc)
        gr = c["gr"]
        ri, ci = c["ri"], c["ci"]
        upper, sup = ci >= ri, ci > ri
        dect = jnp.where(upper, jnp.exp(jnp.where(upper, gr - gc, 0.0)), 0.0)
        tinv_t = _tri_inv(jnp.where(sup, _bdot_nt(c["kn"], c["kb"]) * dect, 0.0), c["eye"])
        u_s[...] = c["u"]
        w_s[...] = c["w"]
        qg_s[...] = c["qn"] * c["gam"]
        kd_s[...] = c["kd"]
        at_s[...] = jnp.where(upper, _bdot_nt(c["kn"], c["qn"]) * dect, 0.0)
        e_s[...] = jnp.broadcast_to(jnp.exp(c["gl"]), (n, 1, D_V))

        def step(j, ds):
            i = n - 1 - j
            s = s_ref[0, i]
            do_i = do_s[i]
            v_new = u_s[i] - _dot(w_s[i], s)
            dvn = _dot(at_s[i], do_i) + _dot(kd_s[i], ds)
            da_s[i] = _dot_nt(do_i, v_new)
            dat_s[i] = _dot_nt(v_new, do_i)
            dqg_s[i] = _dot_nt(do_i, s)
            dw_s[i] = -_dot_nt(dvn, s)
            dkd_s[i] = _dot_nt(v_new, ds)
            du_s[i] = dvn
            dgs_s[i] = jnp.broadcast_to(jnp.sum(jnp.sum(s * ds, axis=1, keepdims=True), axis=0, keepdims=True), (1, D_V))
            return _dot_tn(qg_s[i], do_i) + e_s[i] * ds - _dot_tn(w_s[i], dvn)

        lax.fori_loop(0, n, step, jnp.zeros((D_V, D_V), F32))

        du, dw, dqg, dkd = du_s[...], dw_s[...], dqg_s[...], dkd_s[...]
        lower, strict, dec = c["lower"], c["strict"], c["dec"]
        kn, kb, qn, gam, kdf = c["kn"], c["kb"], c["qn"], c["gam"], c["kdf"]
        drv = _bdot(tinv_t, du, exact=True)
        drk = _bdot(tinv_t, dw, exact=True)
        dl = jnp.where(strict, -(_bdot_nt(drv, c["u"]) + _bdot_nt(drk, c["w"])), 0.0)
        dlt = jnp.where(sup, -(_bdot_nt(c["u"], drv) + _bdot_nt(c["w"], drk)), 0.0)
        da = jnp.where(lower, da_s[...], 0.0)
        dat = jnp.where(upper, dat_s[...], 0.0)
        e = (dl * c["mm"] + da * c["qk"]) * dec
        col_sums = jnp.sum(e, axis=1, keepdims=True)
        dgc = jnp.sum(e, axis=2, keepdims=True) - jnp.sum(col_sums * c["eye"], axis=2, keepdims=True)
        dkb = _bdot(dl * dec, kn) + gam * drk
        dkn = _bdot(dlt * dect, kb) + _bdot(dat * dect, qn) + b * dkb + dkd * kdf
        dqn = _bdot(da * dec, kn) + gam * dqg
        dgam = jnp.sum(drk * kb, axis=-1, keepdims=True) + jnp.sum(dqg * qn, axis=-1, keepdims=True)
        dbeta = jnp.sum(dkb * kn, axis=-1, keepdims=True) + jnp.sum(drv * v3, axis=-1, keepdims=True)
        dv_ref[...] = (b * drv).reshape(t, D_V)
        ee = jnp.sum(dkd * kn, axis=-1, keepdims=True) * kdf
        dgc = dgc + dgam * gam - ee
        rowc = lax.broadcasted_iota(jnp.int32, (n, CHUNK, 1), 1)
        tail = jnp.sum(ee, axis=1, keepdims=True) + dgs_s[...][:, :, 0:1] * jnp.exp(c["gl"])
        dgc = dgc + jnp.where(rowc == CHUNK - 1, tail, 0.0)
        lane = lax.broadcasted_iota(jnp.int32, (t, LANES), 1)
        head = pl.program_id(0)
        dgt_ref[...] += (jnp.where(lane == B_LO + head, dbeta.reshape(t, 1), 0.0)
                         + jnp.where(lane == A_LO + head, dgc.reshape(t, 1), 0.0))
        sc = D_V ** -0.5
        qh, rq, rk = c["qh"], c["rq"], c["rk"]
        dq_ref[...] = (rq * (sc * dqn - qh * jnp.sum(sc * dqn * qh, axis=-1, keepdims=True))).reshape(t, D_V)
        dk_ref[...] = (rk * (dkn - kn * jnp.sum(dkn * kn, axis=-1, keepdims=True))).reshape(t, D_V)

    once = pl.Buffered(1)
    col = lambda off: pl.BlockSpec((t, D_V), lambda h: (0, off + h), pipeline_mode=once)
    out_col = pl.BlockSpec((t, D_V), lambda h: (0, h))
    lanes = pl.BlockSpec((t, LANES), lambda h: (0, 0))
    row = pl.BlockSpec((1, D_V), lambda h: (0, 0))
    big = pltpu.VMEM((n, CHUNK, D_V), F32)
    sq = pltpu.VMEM((n, CHUNK, CHUNK), F32)
    small = pltpu.VMEM((n, 1, D_V), F32)
    return pl.pallas_call(
        body, grid=(HEADS,),
        in_specs=[col(0), col(HEADS), col(2 * HEADS), col(z_off), lanes, lanes, row, col(0),
                  pl.BlockSpec((1, n, D_V, D_V), lambda h: (h, 0, 0, 0), pipeline_mode=once), col(0)],
        out_specs=[out_col, out_col, out_col, out_col, lanes, row],
        out_shape=[SDS((t, GDN_W), F32), SDS((t, GDN_W), F32), SDS((t, GDN_W), F32), SDS((t, GDN_W), BF16),
                   SDS((t, LANES), F32), SDS((1, D_V), F32)],
        scratch_shapes=[big, big, big, big, sq, small, big, big, big, big, big, sq, sq, small],
        name=name, compiler_params=_cp(("arbitrary",)))(yc, yc, yc, p_main, gates, gcum, gn, o_raw, states, dgated)


def _col_tile(d):
    return _pick(d, (512, 256, 128))


def _mix_fwd(y_a, y_b, p_main, name):
    t, d = y_a.shape
    tm, cw = _row_tile(t), _col_tile(d)
    off_a, off_b = MAIN_FIXED // cw, (MAIN_FIXED + d) // cw

    def body(ya_ref, yb_ref, ga_ref, gb_ref, u_ref):
        u_ref[...] = (_sigmoid(ga_ref[...]) * ya_ref[...] + _sigmoid(gb_ref[...]) * yb_ref[...]).astype(BF16)

    blk = pl.BlockSpec((tm, cw), lambda i, j: (i, j))
    return pl.pallas_call(
        body, grid=(t // tm, d // cw),
        in_specs=[blk, blk, pl.BlockSpec((tm, cw), lambda i, j: (i, off_a + j)), pl.BlockSpec((tm, cw), lambda i, j: (i, off_b + j))],
        out_specs=blk, out_shape=SDS((t, d), BF16), name=name,
        compiler_params=_cp(("parallel", "parallel")))(y_a, y_b, p_main, p_main)


def _mix_bwd(du, y_a, y_b, p_main, name):
    t, d = y_a.shape
    tm, cw = _row_tile(t), _col_tile(d)
    off_a, off_b = MAIN_FIXED // cw, (MAIN_FIXED + d) // cw
    nb = d // cw

    def body(du_ref, ya_ref, yb_ref, ga_ref, gb_ref, dya_ref, dyb_ref, dla_ref, dlb_ref):
        duv = du_ref[...]
        ga, gb = _sigmoid(ga_ref[...]), _sigmoid(gb_ref[...])
        dya_ref[...] = (duv * ga).astype(BF16)
        dyb_ref[...] = (duv * gb).astype(BF16)
        dla_ref[...] = (duv * ya_ref[...] * ga * (1.0 - ga)).astype(BF16)
        dlb_ref[...] = (duv * yb_ref[...] * gb * (1.0 - gb)).astype(BF16)

    blk = pl.BlockSpec((tm, cw), lambda i, j: (i, j))
    outs = pl.pallas_call(
        body, grid=(t // tm, nb),
        in_specs=[blk, blk, blk, pl.BlockSpec((tm, cw), lambda i, j: (i, off_a + j)),
                  pl.BlockSpec((tm, cw), lambda i, j: (i, off_b + j))],
        out_specs=[blk, blk, blk, blk],
        out_shape=[SDS((t, d), BF16), SDS((t, d), BF16), SDS((t, d), BF16), SDS((t, d), BF16)], name=name,
        compiler_params=_cp(("parallel", "parallel")))(du, y_a, y_b, p_main, p_main)
    return outs


def _gate_res(x, y, gt, name):
    t, d = x.shape
    tm = _row_tile(t)

    def body(x_ref, y_ref, g_ref, o_ref):
        o_ref[...] = x_ref[...] + g_ref[...] * y_ref[...]

    blk = pl.BlockSpec((tm, d), lambda i: (i, 0))
    return pl.pallas_call(
        body, grid=(t // tm,), in_specs=[blk, blk, pl.BlockSpec((1, d), lambda i: (0, 0))], out_specs=blk,
        out_shape=SDS((t, d), F32), name=name, compiler_params=_cp(("parallel",)))(x, y, gt)


def _gate_res_bwd(dx, y, gt, name):
    t, d = dx.shape
    tm = _row_tile(t)

    def body(dx_ref, y_ref, g_ref, dg_ref, dy_ref):
        @pl.when(pl.program_id(0) == 0)
        def _():
            dg_ref[...] = jnp.zeros_like(dg_ref)

        dxv = dx_ref[...]
        dg_ref[...] += jnp.sum(dxv * y_ref[...], axis=0, keepdims=True)
        dy_ref[...] = (dxv * g_ref[...]).astype(BF16)

    blk = pl.BlockSpec((tm, d), lambda i: (i, 0))
    row = pl.BlockSpec((1, d), lambda i: (0, 0))
    return pl.pallas_call(
        body, grid=(t // tm,), in_specs=[blk, blk, row], out_specs=[row, blk],
        out_shape=[SDS((1, d), F32), SDS((t, d), BF16)], name=name, compiler_params=_cp(("arbitrary",)))(dx, y, gt)


def _swiglu_fwd(gu, name):
    t, f2 = gu.shape
    f = f2 // 2
    tm, cw = _row_tile(t), _col_tile(f)
    nb = f // cw

    def body(g_ref, u_ref, o_ref):
        g = g_ref[...]
        o_ref[...] = (g * _sigmoid(g) * u_ref[...]).astype(BF16)

    return pl.pallas_call(
        body, grid=(t // tm, nb),
        in_specs=[pl.BlockSpec((tm, cw), lambda i, j: (i, j)), pl.BlockSpec((tm, cw), lambda i, j: (i, nb + j))],
        out_specs=pl.BlockSpec((tm, cw), lambda i, j: (i, j)), out_shape=SDS((t, f), BF16), name=name,
        compiler_params=_cp(("parallel", "parallel")))(gu, gu)


def _swiglu_bwd(gu, da, name):
    t, f2 = gu.shape
    f = f2 // 2
    tm, cw = _row_tile(t), _col_tile(f)
    nb = f // cw

    def body(g_ref, u_ref, da_ref, dg_ref, dup_ref):
        g = g_ref[...]
        dav = da_ref[...]
        sg = _sigmoid(g)
        dg_ref[...] = (dav * u_ref[...] * sg * (1.0 + g * (1.0 - sg))).astype(BF16)
        dup_ref[...] = (dav * g * sg).astype(BF16)

    blk = pl.BlockSpec((tm, cw), lambda i, j: (i, j))
    dg, dup = pl.pallas_call(
        body, grid=(t // tm, nb),
        in_specs=[blk, pl.BlockSpec((tm, cw), lambda i, j: (i, nb + j)), blk], out_specs=[blk, blk],
        out_shape=[SDS((t, f), BF16), SDS((t, f), BF16)], name=name,
        compiler_params=_cp(("parallel", "parallel")))(gu, gu, da)
    return dg, dup


def _loss_head(x, w, target, name):
    t, d = x.shape
    tm = _row_tile(t)

    def body(x_ref, w_ref, t_ref, l_ref, dx_ref, dw_ref):
        @pl.when(pl.program_id(0) == 0)
        def _():
            l_ref[...] = jnp.zeros_like(l_ref)
            dw_ref[...] = jnp.zeros_like(dw_ref)

        xv = x_ref[...]
        wv = w_ref[...]
        r = lax.rsqrt(jnp.mean(xv * xv, axis=-1, keepdims=True) + EPS)
        xh = xv * r
        err = xh * wv - t_ref[...]
        per_tok = jnp.mean(err * err, axis=-1, keepdims=True)
        l_ref[...] += 0.5 * jnp.sum(per_tok, axis=0, keepdims=True)
        dy = err * (1.0 / d)
        dw_ref[...] += jnp.sum(dy * xh, axis=0, keepdims=True)
        dxh = dy * wv
        dx_ref[...] = r * (dxh - xh * jnp.mean(dxh * xh, axis=-1, keepdims=True))

    blk = pl.BlockSpec((tm, d), lambda i: (i, 0))
    row = pl.BlockSpec((1, d), lambda i: (0, 0))
    return pl.pallas_call(
        body, grid=(t // tm,), in_specs=[blk, row, blk],
        out_specs=[pl.BlockSpec((1, LANES), lambda i: (0, 0)), blk, row],
        out_shape=[SDS((1, LANES), F32), SDS((t, d), F32), SDS((1, d), F32)], name=name,
        compiler_params=_cp(("arbitrary",)))(x, w, target)


def _adamw(w, g, m, v, name):
    shape = w.shape
    cols = shape[-1]
    rows = w.size // cols
    w2, g2, m2, v2 = (a.reshape(rows, cols) for a in (w, g, m, v))
    lanes_padded = -(-cols // LANES) * LANES
    budget_rows = max(8, (20 * 1024 * 1024) // (lanes_padded * 4 * 16))
    tr = rows
    if rows > budget_rows:
        tr = _pick(rows, tuple(c for c in (1024, 512, 256, 128, 64, 32, 16, 8) if c <= budget_rows))
    c1 = 1.0 / (1.0 - ADAM_B1 ** ADAM_STEP)
    c2 = 1.0 / (1.0 - ADAM_B2 ** ADAM_STEP)

    def body(w_ref, g_ref, m_ref, v_ref, d_ref, mo_ref, vo_ref):
        gv = g_ref[...]
        mn = ADAM_B1 * m_ref[...] + (1.0 - ADAM_B1) * gv
        vn = ADAM_B2 * v_ref[...] + (1.0 - ADAM_B2) * (gv * gv)
        mo_ref[...] = mn
        vo_ref[...] = vn
        d_ref[...] = -ADAM_LR * ((mn * c1) / (jnp.sqrt(vn * c2) + ADAM_EPS) + ADAM_WD * w_ref[...])

    blk = pl.BlockSpec((tr, cols), lambda i: (i, 0))
    outs = pl.pallas_call(
        body, grid=(rows // tr,), in_specs=[blk, blk, blk, blk], out_specs=[blk, blk, blk],
        out_shape=[SDS((rows, cols), F32)] * 3, name=name, compiler_params=_cp(("parallel",)))(w2, g2, m2, v2)
    return tuple(o.reshape(shape) for o in outs)


KPE_LO = 2 * Q_LORA
QKVZ_LO = KPE_LO + D_ROPE
BA_LO = QKVZ_LO + 4 * GDN_W
GATE_LO = BA_LO + 2 * HEADS


def _lay_w_in(w_in):
    d = w_in.shape[0]
    main = jnp.concatenate([w_in[:, :KPE_LO], w_in[:, QKVZ_LO:BA_LO], w_in[:, GATE_LO:]], axis=1)
    small = jnp.concatenate([w_in[:, KPE_LO:QKVZ_LO], w_in[:, BA_LO:GATE_LO],
                             jnp.zeros((d, LANES - D_ROPE - 2 * HEADS), w_in.dtype)], axis=1)
    return main, small


def _unlay_w_in(g_main, g_small):
    return jnp.concatenate([g_main[:, :KPE_LO], g_small[:, :D_ROPE], g_main[:, KPE_LO:KPE_LO + 4 * GDN_W],
                            g_small[:, D_ROPE:D_ROPE + 2 * HEADS], g_main[:, MAIN_FIXED:]], axis=1)


def _lay_w_uq(w_uq):
    r = w_uq.reshape(Q_LORA, HEADS, D_NOPE + D_ROPE)
    r = jnp.pad(r, ((0, 0), (0, 0), (0, D_QK - D_NOPE - D_ROPE)))
    return r.reshape(Q_LORA, HEADS * D_QK)


def _unlay_w_uq(g):
    return g.reshape(Q_LORA, HEADS, D_QK)[:, :, :D_NOPE + D_ROPE].reshape(Q_LORA, HEADS * (D_NOPE + D_ROPE))


def _lay_w_ukv(w_ukv):
    return w_ukv.reshape(KV_LORA, HEADS, 2, D_V).transpose(0, 2, 1, 3).reshape(KV_LORA, 2 * HEADS * D_V)


def _unlay_w_ukv(g):
    return g.reshape(KV_LORA, 2, HEADS, D_V).transpose(0, 2, 1, 3).reshape(KV_LORA, 2 * HEADS * D_V)


def _lane_row(vec, lo):
    return jnp.pad(vec.reshape(1, -1), ((0, 0), (lo, LANES - lo - vec.shape[0])))


def _rope_tables(positions):
    half = D_ROPE // 2
    inv_freq = 1.0 / (10000.0 ** (jnp.arange(0, D_ROPE, 2, dtype=F32) / D_ROPE))
    ang = positions.astype(F32)[:, None] * inv_freq
    cos, sin = jnp.cos(ang), jnp.sin(ang)
    t = positions.shape[0]
    zeros = lambda n: jnp.zeros((t, n), F32)
    tc = jnp.concatenate([cos, cos, zeros(LANES - D_ROPE)], axis=1)
    ts1 = jnp.concatenate([-sin, zeros(LANES - half)], axis=1)
    ts2 = jnp.concatenate([zeros(half), sin, zeros(LANES - D_ROPE)], axis=1)
    return tc, ts1, ts2


def _layer_fwd(x, mod, wt, tabs, tag):
    t, d = x.shape
    sh_a, sc_a, gt_a, sh_f, sc_f, gt_f = mod
    zero_l = jnp.zeros((1, Q_LORA), F32)
    s = dict(x=x)
    s["h1"] = _norm_fwd(x, 0, d, wt["norm_mix"], sc_a, sh_a, f"{tag}_norm_mix")
    s["p_main"] = _mm(s["h1"], wt["w_main"], name=f"{tag}_in_main")
    s["p_small"] = _mm(s["h1"], wt["w_small"], name=f"{tag}_in_small")
    s["cqn"] = _norm_fwd(s["p_main"], 0, Q_LORA, wt["q_a_norm"], zero_l, zero_l, f"{tag}_q_norm")
    s["ckvn"] = _norm_fwd(s["p_main"], 1, KV_LORA, wt["kv_a_norm"], zero_l, zero_l, f"{tag}_kv_norm")
    q_raw = _mm(s["cqn"], wt["w_uq"], name=f"{tag}_uq")
    s["kv_raw"] = _mm(s["ckvn"], wt["w_ukv"], name=f"{tag}_ukv")
    s["q_r"] = _rope_q(q_raw, *tabs, False, BF16, f"{tag}_rope_q")
    s["k_r"] = _k_assemble(s["kv_raw"], s["p_small"], *tabs, f"{tag}_k_asm")
    s["o"], s["lse"] = _attn_fwd(s["q_r"], s["k_r"], s["kv_raw"], HEADS, f"{tag}_attn")
    s["y_a"] = _mm(s["o"], wt["w_o_mla"], name=f"{tag}_o_mla")
    s["yc"] = _conv_fwd(s["p_main"], wt["conv_w"], f"{tag}_conv")
    s["gates"], s["gcum"] = _gdn_gates(s["p_small"], wt["a_row"], wt["dt_row"], f"{tag}_gates")
    s["o_raw"], s["gated"], s["states"] = _gdn_fwd(s["yc"], s["p_main"], s["gates"], s["gcum"], wt["gdn_norm"], f"{tag}_gdn")
    s["y_b"] = _mm(s["gated"], wt["w_o_gdn"], name=f"{tag}_o_gdn")
    s["u"] = _mix_fwd(s["y_a"], s["y_b"], s["p_main"], f"{tag}_mix")
    s["mixo"] = _mm(s["u"], wt["w_o"], name=f"{tag}_o")
    s["x2"] = _gate_res(x, s["mixo"], gt_a, f"{tag}_res_a")
    s["h2"] = _norm_fwd(s["x2"], 0, d, wt["norm_ffn"], sc_f, sh_f, f"{tag}_norm_ffn")
    s["gu"] = _mm(s["h2"], wt["w_gate_up"], name=f"{tag}_gate_up")
    s["a"] = _swiglu_fwd(s["gu"], f"{tag}_swiglu")
    s["f"] = _mm(s["a"], wt["w_down"], name=f"{tag}_down")
    return _gate_res(s["x2"], s["f"], gt_f, f"{tag}_res_f"), s


def _layer_bwd(dx3, s, mod, wt, tabs, tag):
    x = s["x"]
    t, d = x.shape
    sh_a, sc_a, gt_a, sh_f, sc_f, gt_f = mod
    zero_l = jnp.zeros((1, Q_LORA), F32)
    g = {}
    dgt_f, df = _gate_res_bwd(dx3, s["f"], gt_f, f"{tag}_b_res_f")
    da = _mm(df, wt["w_down"], tb=True, name=f"{tag}_b_down_x")
    g["w_down"] = _mm(s["a"].T, df, out_dtype=BF16, name=f"{tag}_b_down_w")
    dgate, dup = _swiglu_bwd(s["gu"], da, f"{tag}_b_swiglu")
    dgu = jnp.concatenate([dgate, dup], axis=1)
    dh2 = _mm(dgu, wt["w_gate_up"], tb=True, name=f"{tag}_b_gate_up_x")
    g["w_gate_up"] = _mm(s["h2"].T, dgu, out_dtype=BF16, name=f"{tag}_b_gate_up_w")
    dx2, g["norm_ffn"], dsc_f, dsh_f = _norm_bwd(s["x2"], 0, d, wt["norm_ffn"], sc_f, dh2, dx3, F32, f"{tag}_b_norm_ffn")
    dgt_a, dmixo = _gate_res_bwd(dx2, s["mixo"], gt_a, f"{tag}_b_res_a")
    du = _mm(dmixo, wt["w_o"], tb=True, name=f"{tag}_b_o_x")
    g["w_o"] = _mm(s["u"].T, dmixo, out_dtype=BF16, name=f"{tag}_b_o_w")
    dy_a, dy_b, dl_a, dl_b = _mix_bwd(du, s["y_a"], s["y_b"], s["p_main"], f"{tag}_b_mix")
    dgated = _mm(dy_b, wt["w_o_gdn"], tb=True, name=f"{tag}_b_o_gdn_x")
    g["w_o_gdn"] = _mm(s["gated"].T, dy_b, out_dtype=BF16, name=f"{tag}_b_o_gdn_w")
    dq_c, dk_c, dv_c, dz, dgates, g["gdn_norm"] = _gdn_bwd(
        s["yc"], s["p_main"], s["gates"], s["gcum"], wt["gdn_norm"], s["o_raw"], s["states"], dgated, f"{tag}_b_gdn")
    du_conv, g["conv_w"] = _conv_bwd(s["p_main"], wt["conv_w"], jnp.concatenate([dq_c, dk_c, dv_c], axis=1), f"{tag}_b_conv")
    do = _mm(dy_a, wt["w_o_mla"], tb=True, name=f"{tag}_b_o_mla_x")
    g["w_o_mla"] = _mm(s["o"].T, dy_a, out_dtype=BF16, name=f"{tag}_b_o_mla_w")
    dq_r, dk_r, dv = _attn_bwd(s["q_r"], s["k_r"], s["kv_raw"], HEADS, s["o"], s["lse"], do, f"{tag}_b_attn")
    dq_raw = _rope_q(dq_r, *tabs, True, BF16, f"{tag}_b_rope_q")
    dkv_raw, dkpe = _k_assemble_bwd(dk_r, dv, *tabs, f"{tag}_b_k_asm")
    dcqn = _mm(dq_raw, wt["w_uq"], tb=True, name=f"{tag}_b_uq_x")
    g["w_uq"] = _mm(s["cqn"].T, dq_raw, out_dtype=BF16, name=f"{tag}_b_uq_w")
    dckvn = _mm(dkv_raw, wt["w_ukv"], tb=True, name=f"{tag}_b_ukv_x")
    g["w_ukv"] = _mm(s["ckvn"].T, dkv_raw, out_dtype=BF16, name=f"{tag}_b_ukv_w")
    dc_q, g["q_a_norm"], _, _ = _norm_bwd(s["p_main"], 0, Q_LORA, wt["q_a_norm"], zero_l, dcqn, None, BF16, f"{tag}_b_q_norm")
    dc_kv, g["kv_a_norm"], _, _ = _norm_bwd(s["p_main"], 1, KV_LORA, wt["kv_a_norm"], zero_l, dckvn, None, BF16,
                                            f"{tag}_b_kv_norm")
    dp_small, g["a_row"], g["dt_row"] = _gdn_gates_bwd(
        s["p_small"], wt["a_row"], wt["dt_row"], s["gates"], dgates, dkpe, f"{tag}_b_gates")
    dp_main = jnp.concatenate([dc_q, dc_kv, du_conv, dz, dl_a, dl_b], axis=1)
    h1t = s["h1"].T
    dh1 = _mm(dp_small, wt["w_small"], tb=True, name=f"{tag}_b_in_small_x")
    dh1 = _mm(dp_main, wt["w_main"], tb=True, acc_in=dh1, name=f"{tag}_b_in_main_x")
    g["w_main"] = _mm(h1t, dp_main, out_dtype=BF16, name=f"{tag}_b_in_main_w")
    g["w_small"] = _mm(h1t, dp_small, out_dtype=BF16, name=f"{tag}_b_in_small_w")
    dx, g["norm_mix"], dsc_a, dsh_a = _norm_bwd(x, 0, d, wt["norm_mix"], sc_a, dh1, dx2, F32, f"{tag}_b_norm_mix")
    return dx, (dsh_a, dsc_a, dgt_a, dsh_f, dsc_f, dgt_f), g


def _layer_weights(full, l):
    w_main, w_small = _lay_w_in(full["w_in"][l])
    return dict(
        w_main=w_main, w_small=w_small, w_uq=_lay_w_uq(full["w_uq"][l]), w_ukv=_lay_w_ukv(full["w_ukv"][l]),
        w_o_mla=full["w_o_mla"][l], w_o_gdn=full["w_o_gdn"][l], w_o=full["w_o"][l], w_gate_up=full["w_gate_up"][l],
        w_down=full["w_down"][l], conv_w=full["conv_w"][l],
        norm_mix=full["norm_mix"][l][None], norm_ffn=full["norm_ffn"][l][None],
        q_a_norm=full["q_a_norm"][l][None], kv_a_norm=full["kv_a_norm"][l][None], gdn_norm=full["gdn_norm"][l][None],
        a_row=_lane_row(full["A_log"][l], A_LO), dt_row=_lane_row(full["dt_bias"][l], A_LO))


def _small_grads_ref_layout(g):
    return dict(
        conv_w=g["conv_w"], norm_mix=g["norm_mix"][0], norm_ffn=g["norm_ffn"][0], q_a_norm=g["q_a_norm"][0],
        kv_a_norm=g["kv_a_norm"][0], gdn_norm=g["gdn_norm"][0], A_log=g["a_row"][0, A_LO:A_HI],
        dt_bias=g["dt_row"][0, A_LO:A_HI])


def _local_step(x, mods, target, final_norm, full, positions):
    tabs = _rope_tables(positions)
    depth = len(mods)
    wts = [_layer_weights(full, l) for l in range(depth)]
    saved = []
    h = x
    for l in range(depth):
        h, s = _layer_fwd(h, mods[l], wts[l], tabs, f"l{l}")
        saved.append(s)
    loss, dh, dfn = _loss_head(h, final_norm[None], target, "loss_head")
    dmods, grads = [None] * depth, [None] * depth
    for l in reversed(range(depth)):
        dh, dmods[l], grads[l] = _layer_bwd(dh, saved[l], mods[l], wts[l], tabs, f"l{l}")
    return loss, dh, dmods, grads, dfn[0]


HBM_SPEC = pl.BlockSpec(memory_space=pl.ANY)
VMEM_SPEC = pl.BlockSpec(memory_space=pltpu.VMEM)
N_CHIPS = 4
N_DEV = 8


def _me():
    return lax.axis_index("x"), lax.axis_index("y"), lax.axis_index("c")


def _flip(pos, f):
    mx, my, mc = pos
    fx, fy, fc = (f >> 2) & 1, (f >> 1) & 1, f & 1
    return ((mx + fx) % 2, (my + fy) % 2, (mc + fc) % 2)


def _all_gather_small(x, name):
    r, n = x.shape

    def body(x_ref, out_ref, send_sems, recv_sems, local_sem):
        me = _me()
        row = lambda p: 4 * p[0] + 2 * p[1] + p[2]
        mine = pltpu.make_async_copy(x_ref, out_ref.at[row(me)], local_sem)
        mine.start()

        def copy(f, origin):
            return pltpu.make_async_remote_copy(
                src_ref=x_ref, dst_ref=out_ref.at[row(origin)], send_sem=send_sems.at[f - 1], recv_sem=recv_sems.at[f - 1],
                device_id=_flip(me, f), device_id_type=MESH)

        sends = [copy(f, me) for f in range(1, N_DEV)]
        for cp in sends:
            cp.start()
        for f in range(1, N_DEV):
            copy(f, _flip(me, f)).wait_recv()
        for cp in sends:
            cp.wait_send()
        mine.wait()

    return pl.pallas_call(
        body, out_shape=SDS((N_DEV, r, n), x.dtype), in_specs=[VMEM_SPEC], out_specs=VMEM_SPEC,
        scratch_shapes=[pltpu.SemaphoreType.DMA((N_DEV - 1,)), pltpu.SemaphoreType.DMA((N_DEV - 1,)), pltpu.SemaphoreType.DMA],
        name=name, compiler_params=pltpu.CompilerParams(vmem_limit_bytes=VMEM_LIMIT))(x)


CHIP_FLIPS = (2, 4, 6)
SIBLING = 1


def _chip_of(pos):
    return 2 * pos[0] + pos[1]


def _dma_sems(n):
    return [pltpu.SemaphoreType.DMA((n,)), pltpu.SemaphoreType.DMA((n,))]


def _gather_weights(shards, name):
    n_arr = len(shards)

    def body(*refs):
        ins, outs = refs[:n_arr], refs[n_arr:2 * n_arr]
        send_sems, recv_sems = refs[2 * n_arr:]
        me = _me()
        layer = me[2]

        def ici(a, k, origin):
            return pltpu.make_async_remote_copy(
                src_ref=ins[a].at[layer], dst_ref=outs[a].at[layer, _chip_of(origin)], send_sem=send_sems.at[6 * a + k],
                recv_sem=recv_sems.at[6 * a + k], device_id=_flip(me, CHIP_FLIPS[k]), device_id_type=MESH)

        def d2d(a, k, lay):
            slab = outs[a].at[lay, _chip_of(_flip(me, CHIP_FLIPS[k]))]
            return pltpu.make_async_remote_copy(
                src_ref=slab, dst_ref=slab, send_sem=send_sems.at[6 * a + 3 + k], recv_sem=recv_sems.at[6 * a + 3 + k],
                device_id=_flip(me, SIBLING), device_id_type=MESH)

        sends = [ici(a, k, me) for a in range(n_arr) for k in range(3)]
        for cp in sends:
            cp.start()
        passed = []
        for a in range(n_arr):
            for k in range(3):
                ici(a, k, _flip(me, CHIP_FLIPS[k])).wait_recv()
                passed.append(d2d(a, k, layer))
                passed[-1].start()
        for a in range(n_arr):
            for k in range(3):
                d2d(a, k, 1 - layer).wait_recv()
        for cp in sends + passed:
            cp.wait_send()

    return pl.pallas_call(
        body, out_shape=[SDS((2, N_CHIPS) + s.shape[1:], s.dtype) for s in shards],
        in_specs=[HBM_SPEC] * n_arr, out_specs=[HBM_SPEC] * n_arr, scratch_shapes=_dma_sems(6 * n_arr), name=name)(*shards)


def _send_other_layer(g0, g1, name):
    n_arr = len(g0)

    def body(*refs):
        in0, in1, outs = refs[:n_arr], refs[n_arr:2 * n_arr], refs[2 * n_arr:3 * n_arr]
        send_sems, recv_sems = refs[3 * n_arr:]
        me = _me()

        def copy(a, src):
            return pltpu.make_async_remote_copy(
                src_ref=src, dst_ref=outs[a], send_sem=send_sems.at[a], recv_sem=recv_sems.at[a],
                device_id=_flip(me, SIBLING), device_id_type=MESH)

        @pl.when(me[2] == 0)
        def _():
            for a in range(n_arr):
                copy(a, in1[a]).start()

        @pl.when(me[2] == 1)
        def _():
            for a in range(n_arr):
                copy(a, in0[a]).start()

        for a in range(n_arr):
            copy(a, in0[a]).wait()

    return pl.pallas_call(
        body, out_shape=[SDS(g.shape, g.dtype) for g in g0], in_specs=[HBM_SPEC] * (2 * n_arr),
        out_specs=[HBM_SPEC] * n_arr, scratch_shapes=_dma_sems(n_arr), name=name)(*g0, *g1)


def _scatter_chips(ps, name):
    n_arr = len(ps)

    def body(*refs):
        ins, outs = refs[:n_arr], refs[n_arr:2 * n_arr]
        send_sems, recv_sems = refs[2 * n_arr:]
        me = _me()

        def copy(a, k):
            return pltpu.make_async_remote_copy(
                src_ref=ins[a].at[_chip_of(_flip(me, CHIP_FLIPS[k]))], dst_ref=outs[a].at[k],
                send_sem=send_sems.at[3 * a + k], recv_sem=recv_sems.at[3 * a + k],
                device_id=_flip(me, CHIP_FLIPS[k]), device_id_type=MESH)

        copies = [copy(a, k) for a in range(n_arr) for k in range(3)]
        for cp in copies:
            cp.start()
        for cp in copies:
            cp.wait_recv()
        for cp in copies:
            cp.wait_send()

    return pl.pallas_call(
        body, out_shape=[SDS((3,) + p.shape[1:], p.dtype) for p in ps], in_specs=[HBM_SPEC] * n_arr,
        out_specs=[HBM_SPEC] * n_arr, scratch_shapes=_dma_sems(3 * n_arr), name=name)(*ps)


def _swap_layers(rs, name):
    n_arr = len(rs)

    def body(*refs):
        ins, outs = refs[:n_arr], refs[n_arr:2 * n_arr]
        send_sems, recv_sems = refs[2 * n_arr:]
        me = _me()
        copies = [pltpu.make_async_remote_copy(
            src_ref=ins[a], dst_ref=outs[a], send_sem=send_sems.at[a], recv_sem=recv_sems.at[a],
            device_id=_flip(me, SIBLING), device_id_type=MESH) for a in range(n_arr)]
        for cp in copies:
            cp.start()
        for cp in copies:
            cp.wait()

    return pl.pallas_call(
        body, out_shape=[SDS(r.shape, r.dtype) for r in rs], in_specs=[HBM_SPEC] * n_arr, out_specs=[HBM_SPEC] * n_arr,
        scratch_shapes=_dma_sems(n_arr), name=name)(*rs)


def _add_sibling(g0, g1, got, layer, col_shards, name):
    k, n = g0.shape
    tr = _pick(k, (512, 256, 128))
    width = n // N_CHIPS if col_shards else n
    cw = _pick(width, (1024, 512, 256, 128))
    per = width // cw

    def body(l_ref, a0_ref, a1_ref, b_ref, o_ref):
        mine = jnp.where(l_ref[0] == 0, a0_ref[...], a1_ref[...])
        o_ref[...] = (mine.astype(F32) + b_ref[...].astype(F32)).astype(o_ref.dtype)

    blk = pl.BlockSpec((tr, cw), lambda i, j, l: (i, j))
    if col_shards:
        out_spec = pl.BlockSpec((None, tr, cw), lambda i, j, l: (j // per, i, j % per))
        out_shape = SDS((N_CHIPS, k, width), g0.dtype)
    else:
        out_spec, out_shape = blk, SDS((k, n), g0.dtype)
    grid_spec = pltpu.PrefetchScalarGridSpec(
        num_scalar_prefetch=1, grid=(k // tr, n // cw), in_specs=[blk, blk, blk], out_specs=out_spec)
    return pl.pallas_call(body, grid_spec=grid_spec, out_shape=out_shape, name=name,
                          compiler_params=_cp(("parallel", "parallel")))(layer, g0, g1, got)


def _sum_chips(p, got, chip, name):
    _, k, ns = p.shape
    tr = _pick(k, (256, 128, 64, 32, 16))

    def body(c_ref, a_ref, b_ref, o_ref):
        acc = a_ref[0].astype(F32)
        for j in range(3):
            acc = acc + b_ref[j].astype(F32)
        o_ref[...] = acc

    grid_spec = pltpu.PrefetchScalarGridSpec(
        num_scalar_prefetch=1, grid=(k // tr,),
        in_specs=[pl.BlockSpec((1, tr, ns), lambda i, c: (c[0], i, 0)), pl.BlockSpec((3, tr, ns), lambda i, c: (0, i, 0))],
        out_specs=pl.BlockSpec((tr, ns), lambda i, c: (i, 0)))
    return pl.pallas_call(
        body, grid_spec=grid_spec, out_shape=SDS((k, ns), F32), name=name, compiler_params=_cp(("parallel",)))(chip, p, got)


def _sum_devices(g, name):
    _, _, n = g.shape

    def body(g_ref, o_ref):
        acc = g_ref[0]
        for k in range(1, N_DEV):
            acc = acc + g_ref[k]
        o_ref[...] = acc

    return pl.pallas_call(body, out_shape=SDS((1, n), F32), in_specs=[VMEM_SPEC], out_specs=VMEM_SPEC, name=name)(g)


def _silu_rows(c, name):
    def body(c_ref, o_ref):
        v = c_ref[...]
        o_ref[...] = v * _sigmoid(v)

    return pl.pallas_call(body, out_shape=SDS(c.shape, F32), in_specs=[VMEM_SPEC], out_specs=VMEM_SPEC, name=name)(c)


BIG = (("w_in", 2), ("w_uq", 2), ("w_ukv", 2), ("w_o_mla", 2), ("w_o_gdn", 2), ("w_o", 1), ("w_gate_up", 2), ("w_down", 1))
KERNEL_BIG = ("w_main", "w_small", "w_uq", "w_ukv", "w_o_mla", "w_o_gdn", "w_o", "w_gate_up", "w_down")
COL_SHARDED_AS_IS = ("w_o_mla", "w_o_gdn", "w_gate_up")
SMALL = ("norm_mix", "norm_ffn", "q_a_norm", "kv_a_norm", "A_log", "dt_bias", "gdn_norm")
WEIGHTS = ("w_ada", "b_ada", "norm_mix", "norm_ffn", "w_in", "q_a_norm", "kv_a_norm", "w_uq", "w_ukv", "w_o_mla", "conv_w",
           "A_log", "dt_bias", "gdn_norm", "w_o_gdn", "w_o", "w_gate_up", "w_down", "final_norm")
ADA_PAD = 16
K_PAD = 128


def _pad_to(a, n, axis):
    pad = [(0, 0)] * a.ndim
    pad[axis] = (0, n - a.shape[axis])
    return jnp.pad(a, pad)


def kernel(x, c, positions, w_ada, b_ada, norm_mix, norm_ffn, w_in, q_a_norm, kv_a_norm, w_uq, w_ukv, w_o_mla, conv_w, A_log, dt_bias, gdn_norm, w_o_gdn, w_o, w_gate_up, w_down, final_norm, loss_target, m_w_ada, m_b_ada, m_norm_mix, m_norm_ffn, m_w_in, m_q_a_norm, m_kv_a_norm, m_w_uq, m_w_ukv, m_w_o_mla, m_conv_w, m_A_log, m_dt_bias, m_gdn_norm, m_w_o_gdn, m_w_o, m_w_gate_up, m_w_down, m_final_norm, v_w_ada, v_b_ada, v_norm_mix, v_norm_ffn, v_w_in, v_q_a_norm, v_kv_a_norm, v_w_uq, v_w_ukv, v_w_o_mla, v_conv_w, v_A_log, v_dt_bias, v_gdn_norm, v_w_o_gdn, v_w_o, v_w_gate_up, v_w_down, v_final_norm):
    env = dict(locals())
    w = {n: env[n] for n in WEIGHTS}
    depth, d = norm_mix.shape
    t = x.shape[1]
    me = _me()
    chip = _chip_of(me)
    dev = 4 * me[0] + 2 * me[1] + me[2]
    ada_cols = w_ada.shape[2]

    w16 = [w[n].astype(BF16) for n, _ in BIG]
    full = {}
    for (n, axis), own, got in zip(BIG, w16, _gather_weights(w16, "gather_weights")):
        full[n] = jnp.concatenate([jnp.where(chip == j, own, got[:, j]) for j in range(N_CHIPS)], axis=axis)
    conv_all = _all_gather_small(conv_w.reshape(1, -1), "gather_conv").reshape((N_DEV,) + conv_w.shape)
    full["conv_w"] = jnp.concatenate([conv_all[2 * j] for j in range(N_CHIPS)], axis=2)
    for n in SMALL:
        full[n] = w[n]

    c_all = _all_gather_small(c, "gather_c").reshape(N_DEV, d)
    c_act = _silu_rows(_pad_to(c_all, ADA_PAD, 0), "silu_c")
    b_cols = lax.dynamic_slice_in_dim(b_ada, chip * ada_cols, ada_cols, axis=1)
    mod_cols = jnp.stack([
        _mm(c_act, w_ada[l], acc_in=jnp.broadcast_to(b_cols[l][None], (ADA_PAD, ada_cols)), name=f"ada_l{l}")[:N_DEV]
        for l in range(depth)])
    mod_all = _all_gather_small(mod_cols.reshape(depth * N_DEV, ada_cols), "gather_mod")
    mod_all = mod_all.reshape(N_DEV, depth, N_DEV, ada_cols)
    mods = []
    for l in range(depth):
        mine = jnp.concatenate([lax.dynamic_index_in_dim(mod_all[2 * j, l], dev, axis=0, keepdims=True)
                                for j in range(N_CHIPS)], axis=1)
        mods.append(tuple(mine[:, i * d:(i + 1) * d] for i in range(6)))

    loss_part, dx, dmods, grads, dfn = _local_step(x[0], mods, loss_target[0], final_norm, full, positions[0])
    loss = lax.psum(loss_part[0, 0], AXES)

    layer_idx = me[2].astype(jnp.int32).reshape(1)
    chip_idx = chip.astype(jnp.int32).reshape(1)
    g0, g1 = ([grads[l][n] for n in KERNEL_BIG] for l in range(depth))
    from_sibling = _send_other_layer(g0, g1, "rs_sibling")
    sums = {n: _add_sibling(a0, a1, b, layer_idx, n in COL_SHARDED_AS_IS, f"rs_add_{n}")
            for n, a0, a1, b in zip(KERNEL_BIG, g0, g1, from_sibling)}

    def col_shards(g):
        return g.reshape(g.shape[0], N_CHIPS, g.shape[1] // N_CHIPS).transpose(1, 0, 2)

    row_shards = lambda g: g.reshape(N_CHIPS, g.shape[0] // N_CHIPS, g.shape[1])
    chip_sums = dict(
        w_in=col_shards(_unlay_w_in(sums["w_main"], sums["w_small"])), w_uq=col_shards(_unlay_w_uq(sums["w_uq"])),
        w_ukv=col_shards(_unlay_w_ukv(sums["w_ukv"])), w_o_mla=sums["w_o_mla"], w_o_gdn=sums["w_o_gdn"],
        w_o=row_shards(sums["w_o"]), w_gate_up=sums["w_gate_up"], w_down=row_shards(sums["w_down"]))
    p_list = [chip_sums[n] for n, _ in BIG]
    from_chips = _scatter_chips(p_list, "rs_chips")
    reduced = [_sum_chips(p, got, chip_idx, f"rs_sum_{n}") for (n, _), p, got in zip(BIG, p_list, from_chips)]
    other = _swap_layers(reduced, "rs_swap")
    g_out = {}
    for (n, _), mine, theirs in zip(BIG, reduced, other):
        g_out[n] = jnp.stack([jnp.where(me[2] == l, mine, theirs) for l in range(depth)])

    small = [_small_grads_ref_layout(grads[l]) for l in range(depth)]
    small_parts = [jnp.concatenate(dmods[l], axis=1).reshape(-1) for l in range(depth)]
    small_parts += [jnp.stack([small[l][n] for l in range(depth)]).reshape(-1) for n in SMALL]
    small_parts.append(dfn)
    small_sizes = [p.shape[0] for p in small_parts]
    packed = jnp.concatenate(small_parts)
    n_small = -(-packed.shape[0] // LANES) * LANES
    small_all = _all_gather_small(_pad_to(packed, n_small, 0).reshape(1, n_small), "gather_small_grads")
    small_sum = _sum_devices(small_all, "sum_small_grads")[0]
    offs = [0]
    for sz in small_sizes:
        offs.append(offs[-1] + sz)
    g_out["b_ada"] = jnp.stack([small_sum[offs[l]:offs[l + 1]] for l in range(depth)])
    for i, n in enumerate(SMALL):
        g_out[n] = small_sum[offs[depth + i]:offs[depth + i + 1]].reshape(w[n].shape)
    g_out["final_norm"] = small_sum[offs[depth + len(SMALL)]:offs[depth + len(SMALL) + 1]]

    c_act_t = _pad_to(c_act[:N_DEV].T, K_PAD, 1)
    g_ada = []
    for l in range(depth):
        dmod_l = small_all[:, 0, offs[l]:offs[l + 1]]
        dmod_cols = lax.dynamic_slice_in_dim(dmod_l, chip * ada_cols, ada_cols, axis=1)
        g_ada.append(_mm(c_act_t, _pad_to(dmod_cols, K_PAD, 0), name=f"ada_grad_l{l}"))
    g_out["w_ada"] = jnp.stack(g_ada)

    conv_g = jnp.stack([small[l]["conv_w"] for l in range(depth)])
    conv_all_g = _all_gather_small(conv_g.reshape(1, -1), "gather_conv_grads")
    conv_sum = _sum_devices(conv_all_g, "sum_conv_grads").reshape(conv_g.shape)
    n_cc = conv_w.shape[2]
    g_out["conv_w"] = lax.dynamic_slice_in_dim(conv_sum, chip * n_cc, n_cc, axis=2)

    deltas, new_m, new_v = {}, {}, {}
    for n in WEIGHTS:
        deltas[n], new_m[n], new_v[n] = _adamw(w[n], g_out[n], env["m_" + n], env["v_" + n], f"adamw_{n}")
    return (loss, dx[None], *[g_out[n] for n in WEIGHTS], *[deltas[n] for n in WEIGHTS],
            *[new_m[n] for n in WEIGHTS], *[new_v[n] for n in WEIGHTS])
```

```python
import functools
import math

import jax
import jax.numpy as jnp
from jax import lax
from jax.experimental import pallas as pl
from jax.experimental.pallas import tpu as pltpu

F32 = jnp.float32
BF16 = jnp.bfloat16
SDS = jax.ShapeDtypeStruct
MESH = pl.DeviceIdType.MESH
AXES = ("x", "y", "c")

EPS = 1e-6
HEADS = 8
D_NOPE = 128
D_ROPE = 64
D_QK = 256
D_V = 128
Q_LORA = 512
KV_LORA = 512
CHUNK = 64
CONV_WIDTH = 4
GDN_W = HEADS * D_V
MAIN_FIXED = 2 * Q_LORA + 4 * GDN_W
LANES = 128
VMEM_LIMIT = 56 * 1024 * 1024
ADAM_LR, ADAM_B1, ADAM_B2, ADAM_EPS, ADAM_WD, ADAM_STEP = 0.001, 0.9, 0.999, 1e-8, 0.01, 10


def _pick(n, cands):
    for cand in cands:
        if n % cand == 0:
            return cand
    return n


def _cp(sem):
    return pltpu.CompilerParams(dimension_semantics=sem, vmem_limit_bytes=VMEM_LIMIT)


def _row_tile(t):
    return _pick(t, (256, 128, 64, 32, 16, 8))


def _dot(a, b):
    return jnp.dot(a.astype(BF16), b.astype(BF16), preferred_element_type=F32)


def _dot_nt(a, b):
    return lax.dot_general(a.astype(BF16), b.astype(BF16), (((1,), (1,)), ((), ())), preferred_element_type=F32)


def _dot_tn(a, b):
    return lax.dot_general(a.astype(BF16), b.astype(BF16), (((0,), (0,)), ((), ())), preferred_element_type=F32)


def _bdot(a, b, exact=False):
    if exact:
        return lax.dot_general(a, b, (((2,), (1,)), ((0,), (0,))), precision=lax.Precision.HIGHEST,
                               preferred_element_type=F32)
    return lax.dot_general(a.astype(BF16), b.astype(BF16), (((2,), (1,)), ((0,), (0,))), preferred_element_type=F32)


def _bdot_nt(a, b):
    return lax.dot_general(a.astype(BF16), b.astype(BF16), (((2,), (2,)), ((0,), (0,))), preferred_element_type=F32)


def _sigmoid(x):
    return 1.0 / (1.0 + jnp.exp(-x))


def _mm(a, b, *, tb=False, out_dtype=F32, acc_in=None, name):
    m, k = a.shape
    n = b.shape[0] if tb else b.shape[1]
    assert (b.shape[1] if tb else b.shape[0]) == k
    tm = _pick(m, (1024, 512, 256, 128))
    tn = _pick(n, (1024, 512, 256, 128))
    tk = k if k <= 2048 else _pick(k, (512, 256, 128))
    nk = k // tk
    has_acc = acc_in is not None

    def body_one_step(*refs):
        a_ref, b_ref = refs[:2]
        o_ref = refs[-1]
        acc = _dot_nt(a_ref[...], b_ref[...]) if tb else _dot(a_ref[...], b_ref[...])
        if has_acc:
            acc = acc + refs[2][...].astype(F32)
        o_ref[...] = acc.astype(out_dtype)

    if nk == 1:
        in_specs = [pl.BlockSpec((tm, k), lambda i, j: (i, 0)),
                    pl.BlockSpec((tn, k), lambda i, j: (j, 0)) if tb else pl.BlockSpec((k, tn), lambda i, j: (0, j))]
        args = [a, b]
        if has_acc:
            in_specs.append(pl.BlockSpec((tm, tn), lambda i, j: (i, j)))
            args.append(acc_in)
        return pl.pallas_call(
            body_one_step, grid=(m // tm, n // tn), in_specs=in_specs, out_specs=pl.BlockSpec((tm, tn), lambda i, j: (i, j)),
            out_shape=SDS((m, n), out_dtype), name=name, compiler_params=_cp(("parallel", "parallel")))(*args)

    def body(*refs):
        if has_acc:
            a_ref, b_ref, c_ref, o_ref, acc = refs
        else:
            a_ref, b_ref, o_ref, acc = refs
        kk = pl.program_id(2)

        @pl.when(kk == 0)
        def _():
            if has_acc:
                acc[...] = c_ref[...].astype(F32)
            else:
                acc[...] = jnp.zeros_like(acc)

        if tb:
            acc[...] += _dot_nt(a_ref[...], b_ref[...])
        else:
            acc[...] += _dot(a_ref[...], b_ref[...])

        @pl.when(kk == nk - 1)
        def _():
            o_ref[...] = acc[...].astype(out_dtype)

    in_specs = [pl.BlockSpec((tm, tk), lambda i, j, kk: (i, kk)),
                pl.BlockSpec((tn, tk), lambda i, j, kk: (j, kk)) if tb
                else pl.BlockSpec((tk, tn), lambda i, j, kk: (kk, j))]
    args = [a, b]
    if has_acc:
        in_specs.append(pl.BlockSpec((tm, tn), lambda i, j, kk: (i, j)))
        args.append(acc_in)
    return pl.pallas_call(
        body, grid=(m // tm, n // tn, nk), in_specs=in_specs,
        out_specs=pl.BlockSpec((tm, tn), lambda i, j, kk: (i, j)),
        out_shape=SDS((m, n), out_dtype), scratch_shapes=[pltpu.VMEM((tm, tn), F32)],
        name=name, compiler_params=_cp(("parallel", "parallel", "arbitrary")))(*args)


def _norm_fwd(x, col, width, w, sc, sh, name):
    t = x.shape[0]
    tm = _row_tile(t)

    def body(x_ref, w_ref, sc_ref, sh_ref, o_ref):
        xv = x_ref[...]
        r = lax.rsqrt(jnp.mean(xv * xv, axis=-1, keepdims=True) + EPS)
        n = xv * r * w_ref[...]
        o_ref[...] = (n * (1.0 + sc_ref[...]) + sh_ref[...]).astype(o_ref.dtype)

    row = pl.BlockSpec((1, width), lambda i: (0, 0))
    return pl.pallas_call(
        body, grid=(t // tm,), in_specs=[pl.BlockSpec((tm, width), lambda i: (i, col)), row, row, row],
        out_specs=pl.BlockSpec((tm, width), lambda i: (i, 0)), out_shape=SDS((t, width), BF16),
        name=name, compiler_params=_cp(("parallel",)))(x, w, sc, sh)


def _norm_bwd(x, col, width, w, sc, dh, dres, out_dtype, name):
    t = x.shape[0]
    tm = _row_tile(t)
    has_res = dres is not None

    def body(*refs):
        if has_res:
            x_ref, w_ref, sc_ref, dh_ref, dres_ref, dx_ref, dw_ref, dsc_ref, dsh_ref = refs
        else:
            x_ref, w_ref, sc_ref, dh_ref, dx_ref, dw_ref, dsc_ref, dsh_ref = refs

        @pl.when(pl.program_id(0) == 0)
        def _():
            dw_ref[...] = jnp.zeros_like(dw_ref)
            dsc_ref[...] = jnp.zeros_like(dsc_ref)
            dsh_ref[...] = jnp.zeros_like(dsh_ref)

        xv = x_ref[...]
        dhv = dh_ref[...].astype(F32)
        wv = w_ref[...]
        r = lax.rsqrt(jnp.mean(xv * xv, axis=-1, keepdims=True) + EPS)
        xh = xv * r
        n = xh * wv
        dsh_ref[...] += jnp.sum(dhv, axis=0, keepdims=True)
        dsc_ref[...] += jnp.sum(dhv * n, axis=0, keepdims=True)
        dn = dhv * (1.0 + sc_ref[...])
        dw_ref[...] += jnp.sum(dn * xh, axis=0, keepdims=True)
        dxh = dn * wv
        dx = r * (dxh - xh * jnp.mean(dxh * xh, axis=-1, keepdims=True))
        if has_res:
            dx = dx + dres_ref[...]
        dx_ref[...] = dx.astype(out_dtype)

    row = pl.BlockSpec((1, width), lambda i: (0, 0))
    blk = pl.BlockSpec((tm, width), lambda i: (i, 0))
    in_specs = [pl.BlockSpec((tm, width), lambda i: (i, col)), row, row, blk]
    args = [x, w, sc, dh]
    if has_res:
        in_specs.append(blk)
        args.append(dres)
    return pl.pallas_call(
        body, grid=(t // tm,), in_specs=in_specs, out_specs=[blk, row, row, row],
        out_shape=[SDS((t, width), out_dtype), SDS((1, width), F32), SDS((1, width), F32), SDS((1, width), F32)],
        name=name, compiler_params=_cp(("arbitrary",)))(*args)


def _rope128(x, tc, ts1, ts2):
    return x * tc + pltpu.roll(x, 96, 1) * ts1 + pltpu.roll(x, 32, 1) * ts2


def _rope128_t(d, tc, ts1, ts2):
    return d * tc + pltpu.roll(d * ts1, 32, 1) + pltpu.roll(d * ts2, 96, 1)


def _rope_q(q_raw, tc, ts1, ts2, transpose, out_dtype, name):
    t = q_raw.shape[0]
    tm = _row_tile(t)

    def body(q_ref, tc_ref, s1_ref, s2_ref, o_ref):
        fn = _rope128_t if transpose else _rope128
        for h in range(HEADS):
            base = h * D_QK
            o_ref[:, base:base + LANES] = q_ref[:, base:base + LANES].astype(out_dtype)
            x = q_ref[:, base + LANES:base + D_QK].astype(F32)
            o_ref[:, base + LANES:base + D_QK] = fn(x, tc_ref[...], s1_ref[...], s2_ref[...]).astype(out_dtype)

    blk = pl.BlockSpec((tm, HEADS * D_QK), lambda i: (i, 0))
    tab = pl.BlockSpec((tm, LANES), lambda i: (i, 0))
    return pl.pallas_call(
        body, grid=(t // tm,), in_specs=[blk, tab, tab, tab], out_specs=blk,
        out_shape=SDS((t, HEADS * D_QK), out_dtype), name=name, compiler_params=_cp(("parallel",)))(q_raw, tc, ts1, ts2)


def _k_assemble(kv_raw, p_small, tc, ts1, ts2, name):
    t = kv_raw.shape[0]
    tm = _row_tile(t)

    def body(kn_ref, ps_ref, tc_ref, s1_ref, s2_ref, o_ref):
        kpe = _rope128(ps_ref[...], tc_ref[...], s1_ref[...], s2_ref[...]).astype(BF16)
        for h in range(HEADS):
            o_ref[:, h * D_QK:h * D_QK + LANES] = kn_ref[:, h * LANES:(h + 1) * LANES].astype(BF16)
            o_ref[:, h * D_QK + LANES:(h + 1) * D_QK] = kpe

    tab = pl.BlockSpec((tm, LANES), lambda i: (i, 0))
    return pl.pallas_call(
        body, grid=(t // tm,),
        in_specs=[pl.BlockSpec((tm, HEADS * LANES), lambda i: (i, 0)), tab, tab, tab, tab],
        out_specs=pl.BlockSpec((tm, HEADS * D_QK), lambda i: (i, 0)),
        out_shape=SDS((t, HEADS * D_QK), BF16), name=name, compiler_params=_cp(("parallel",)))(kv_raw, p_small, tc, ts1, ts2)


def _k_assemble_bwd(dk, dv, tc, ts1, ts2, name):
    t = dk.shape[0]
    tm = _row_tile(t)

    def body(dk_ref, dv_ref, tc_ref, s1_ref, s2_ref, o_ref, pe_ref):
        acc = jnp.zeros((tm, LANES), F32)
        for h in range(HEADS):
            o_ref[:, h * LANES:(h + 1) * LANES] = dk_ref[:, h * D_QK:h * D_QK + LANES].astype(BF16)
            acc = acc + dk_ref[:, h * D_QK + LANES:(h + 1) * D_QK].astype(F32)
        o_ref[:, HEADS * LANES:] = dv_ref[...].astype(BF16)
        pe_ref[...] = _rope128_t(acc, tc_ref[...], s1_ref[...], s2_ref[...])

    tab = pl.BlockSpec((tm, LANES), lambda i: (i, 0))
    return pl.pallas_call(
        body, grid=(t // tm,),
        in_specs=[pl.BlockSpec((tm, HEADS * D_QK), lambda i: (i, 0)), pl.BlockSpec((tm, HEADS * LANES), lambda i: (i, 0)),
                  tab, tab, tab],
        out_specs=[pl.BlockSpec((tm, 2 * HEADS * LANES), lambda i: (i, 0)), tab],
        out_shape=[SDS((t, 2 * HEADS * LANES), BF16), SDS((t, LANES), F32)],
        name=name, compiler_params=_cp(("parallel",)))(dk, dv, tc, ts1, ts2)


def _attn_tile(t):
    return _pick(t, (256, 128, 64))


def _attn_fwd(q, k, v, v_off, name):
    t = q.shape[0]
    tq = _attn_tile(t)
    scale = (D_NOPE + D_ROPE) ** -0.5

    def body(q_ref, k_ref, v_ref, o_ref, lse_ref):
        for i in range(t // tq):
            n_k = (i + 1) * tq
            s = _dot_nt(q_ref[i * tq:(i + 1) * tq, :], k_ref[0:n_k, :]) * scale
            row = lax.broadcasted_iota(jnp.int32, (tq, n_k), 0) + i * tq
            colv = lax.broadcasted_iota(jnp.int32, (tq, n_k), 1)
            s = jnp.where(colv <= row, s, -jnp.inf)
            m = jnp.max(s, axis=-1, keepdims=True)
            p = jnp.exp(s - m)
            l = jnp.sum(p, axis=-1, keepdims=True)
            o = _dot(p, v_ref[0:n_k, :]) / l
            o_ref[i * tq:(i + 1) * tq, :] = o.astype(BF16)
            lse_ref[0, i * tq:(i + 1) * tq, :] = m + jnp.log(l)

    return pl.pallas_call(
        body, grid=(HEADS,),
        in_specs=[pl.BlockSpec((t, D_QK), lambda h: (0, h)), pl.BlockSpec((t, D_QK), lambda h: (0, h)),
                  pl.BlockSpec((t, D_V), lambda h: (0, v_off + h))],
        out_specs=[pl.BlockSpec((t, D_V), lambda h: (0, h)), pl.BlockSpec((1, t, 1), lambda h: (h, 0, 0))],
        out_shape=[SDS((t, HEADS * D_V), BF16), SDS((HEADS, t, 1), F32)],
        name=name, compiler_params=_cp(("parallel",)))(q, k, v)


def _attn_bwd(q, k, v, v_off, o, lse, do, name):
    t = q.shape[0]
    tq = _attn_tile(t)
    scale = (D_NOPE + D_ROPE) ** -0.5

    def body(q_ref, k_ref, v_ref, o_ref, lse_ref, do_ref, dq_ref, dk_ref, dv_ref):
        dk_ref[...] = jnp.zeros_like(dk_ref)
        dv_ref[...] = jnp.zeros_like(dv_ref)
        for i in range(t // tq):
            n_k = (i + 1) * tq
            rows = slice(i * tq, (i + 1) * tq)
            qi = q_ref[rows, :]
            doi = do_ref[rows, :].astype(F32)
            s = _dot_nt(qi, k_ref[0:n_k, :]) * scale
            row = lax.broadcasted_iota(jnp.int32, (tq, n_k), 0) + i * tq
            colv = lax.broadcasted_iota(jnp.int32, (tq, n_k), 1)
            p = jnp.where(colv <= row, jnp.exp(s - lse_ref[0, rows, :]), 0.0)
            dp = _dot_nt(doi, v_ref[0:n_k, :])
            delta = jnp.sum(doi * o_ref[rows, :].astype(F32), axis=-1, keepdims=True)
            ds = p * (dp - delta) * scale
            dq_ref[rows, :] = _dot(ds, k_ref[0:n_k, :])
            dk_ref[0:n_k, :] += _dot_tn(ds, qi)
            dv_ref[0:n_k, :] += _dot_tn(p, doi)

    qk_spec = pl.BlockSpec((t, D_QK), lambda h: (0, h))
    v_spec = pl.BlockSpec((t, D_V), lambda h: (0, h))
    return pl.pallas_call(
        body, grid=(HEADS,),
        in_specs=[qk_spec, qk_spec, pl.BlockSpec((t, D_V), lambda h: (0, v_off + h)), v_spec,
                  pl.BlockSpec((1, t, 1), lambda h: (h, 0, 0)), v_spec],
        out_specs=[qk_spec, qk_spec, v_spec],
        out_shape=[SDS((t, HEADS * D_QK), F32), SDS((t, HEADS * D_QK), F32), SDS((t, HEADS * D_V), F32)],
        name=name, compiler_params=_cp(("parallel",)))(q, k, v, o, lse, do)


CONV_COLS = 256


def _conv_pre(u, w_ref, rowi):
    acc = u * w_ref[CONV_WIDTH - 1:CONV_WIDTH, :]
    for sft in range(1, CONV_WIDTH):
        shifted = jnp.where(rowi >= sft, pltpu.roll(u, sft, 0), 0.0)
        acc = acc + shifted * w_ref[CONV_WIDTH - 1 - sft:CONV_WIDTH - sft, :]
    return acc


def _conv_fwd(p_main, conv_w, name):
    t = p_main.shape[0]
    off = 2 * Q_LORA // CONV_COLS

    def body(u_ref, w_ref, y_ref):
        u = u_ref[...]
        rowi = lax.broadcasted_iota(jnp.int32, u.shape, 0)
        pre = _conv_pre(u, w_ref, rowi)
        y_ref[...] = pre * _sigmoid(pre)

    return pl.pallas_call(
        body, grid=(3 * GDN_W // CONV_COLS,),
        in_specs=[pl.BlockSpec((t, CONV_COLS), lambda j: (0, off + j)), pl.BlockSpec((CONV_WIDTH, CONV_COLS), lambda j: (0, j))],
        out_specs=pl.BlockSpec((t, CONV_COLS), lambda j: (0, j)), out_shape=SDS((t, 3 * GDN_W), F32),
        name=name, compiler_params=_cp(("parallel",)))(p_main, conv_w)


def _conv_bwd(p_main, conv_w, dyc, name):
    t = p_main.shape[0]
    off = 2 * Q_LORA // CONV_COLS

    def body(u_ref, w_ref, dy_ref, du_ref, dw_ref):
        u = u_ref[...]
        rowi = lax.broadcasted_iota(jnp.int32, u.shape, 0)
        pre = _conv_pre(u, w_ref, rowi)
        sg = _sigmoid(pre)
        dpre = dy_ref[...] * sg * (1.0 + pre * (1.0 - sg))
        du = dpre * w_ref[CONV_WIDTH - 1:CONV_WIDTH, :]
        dw_ref[CONV_WIDTH - 1:CONV_WIDTH, :] = jnp.sum(dpre * u, axis=0, keepdims=True)
        for sft in range(1, CONV_WIDTH):
            back = jnp.where(rowi < t - sft, pltpu.roll(dpre, t - sft, 0), 0.0)
            du = du + back * w_ref[CONV_WIDTH - 1 - sft:CONV_WIDTH - sft, :]
            shifted = jnp.where(rowi >= sft, pltpu.roll(u, sft, 0), 0.0)
            dw_ref[CONV_WIDTH - 1 - sft:CONV_WIDTH - sft, :] = jnp.sum(dpre * shifted, axis=0, keepdims=True)
        du_ref[...] = du.astype(BF16)

    blk = pl.BlockSpec((t, CONV_COLS), lambda j: (0, j))
    wblk = pl.BlockSpec((CONV_WIDTH, CONV_COLS), lambda j: (0, j))
    return pl.pallas_call(
        body, grid=(3 * GDN_W // CONV_COLS,),
        in_specs=[pl.BlockSpec((t, CONV_COLS), lambda j: (0, off + j)), wblk, blk],
        out_specs=[blk, wblk], out_shape=[SDS((t, 3 * GDN_W), BF16), SDS((CONV_WIDTH, 3 * GDN_W), F32)],
        name=name, compiler_params=_cp(("parallel",)))(p_main, conv_w, dyc)


B_LO, A_LO, A_HI = D_ROPE, D_ROPE + HEADS, D_ROPE + 2 * HEADS


def _softplus(z):
    e = jnp.exp(-jnp.abs(z))
    log1p = jnp.where(e < 0.01, e * (1.0 - e * (0.5 - e * (1.0 / 3.0))), jnp.log(1.0 + e))
    return jnp.maximum(z, 0.0) + log1p


def _gdn_gates(p_small, a_row, dt_row, name):
    t = p_small.shape[0]

    def body(ps_ref, a_ref, dt_ref, g_ref, gc_ref):
        x = ps_ref[...]
        lane = lax.broadcasted_iota(jnp.int32, x.shape, 1)
        is_g = (lane >= A_LO) & (lane < A_HI)
        g = jnp.where(is_g, -jnp.exp(a_ref[...]) * _softplus(x + dt_ref[...]), 0.0)
        g_ref[...] = jnp.where(is_g, g, _sigmoid(x))
        pos = lax.broadcasted_iota(jnp.int32, x.shape, 0) % CHUNK
        acc = g
        sft = 1
        while sft < CHUNK:
            acc = acc + jnp.where(pos >= sft, pltpu.roll(acc, sft, 0), 0.0)
            sft *= 2
        gc_ref[...] = acc

    full = pl.BlockSpec((t, LANES), lambda i: (0, 0))
    row = pl.BlockSpec((1, LANES), lambda i: (0, 0))
    return pl.pallas_call(
        body, grid=(1,), in_specs=[full, row, row], out_specs=[full, full],
        out_shape=[SDS((t, LANES), F32), SDS((t, LANES), F32)], name=name,
        compiler_params=_cp(("arbitrary",)))(p_small, a_row, dt_row)


def _gdn_gates_bwd(p_small, a_row, dt_row, gates, dgates, dkpe, name):
    t = p_small.shape[0]

    def body(ps_ref, a_ref, dt_ref, g_ref, db_ref, dkpe_ref, dp_ref, da_ref, ddt_ref):
        x = ps_ref[...]
        lane = lax.broadcasted_iota(jnp.int32, x.shape, 1)
        is_g = (lane >= A_LO) & (lane < A_HI)
        is_b = (lane >= B_LO) & (lane < A_LO)
        pos = lax.broadcasted_iota(jnp.int32, x.shape, 0) % CHUNK
        acc = jnp.where(is_g, db_ref[...], 0.0)
        sft = 1
        while sft < CHUNK:
            acc = acc + jnp.where(pos < CHUNK - sft, pltpu.roll(acc, t - sft, 0), 0.0)
            sft *= 2
        dg = acc
        gv = g_ref[...]
        dz = jnp.where(is_g, dg * (-jnp.exp(a_ref[...])) * _sigmoid(x + dt_ref[...]), 0.0)
        da_ref[...] = jnp.sum(jnp.where(is_g, dg * gv, 0.0), axis=0, keepdims=True)
        ddt_ref[...] = jnp.sum(dz, axis=0, keepdims=True)
        dlb = jnp.where(is_b, db_ref[...] * gv * (1.0 - gv), 0.0)
        dp_ref[...] = (jnp.where(lane < D_ROPE, dkpe_ref[...], 0.0) + dlb + dz).astype(BF16)

    full = pl.BlockSpec((t, LANES), lambda i: (0, 0))
    row = pl.BlockSpec((1, LANES), lambda i: (0, 0))
    return pl.pallas_call(
        body, grid=(1,), in_specs=[full, row, row, full, full, full], out_specs=[full, row, row],
        out_shape=[SDS((t, LANES), BF16), SDS((1, LANES), F32), SDS((1, LANES), F32)], name=name,
        compiler_params=_cp(("arbitrary",)))(p_small, a_row, dt_row, gates, dgates, dkpe)


def _tri_inv(l, eye):
    x = eye - l
    p = _bdot(l, l, exact=True)
    steps = int(math.log2(CHUNK)) - 1
    for s in range(steps):
        x = x + _bdot(x, p, exact=True)
        if s < steps - 1:
            p = _bdot(p, p, exact=True)
    return x


def _l2n(x3):
    r = lax.rsqrt(jnp.sum(x3 * x3, axis=-1, keepdims=True) + EPS)
    return x3 * r, r


def _head_col(a_ref, lane_lo, n):
    a = a_ref[...]
    lane = lax.broadcasted_iota(jnp.int32, a.shape, 1)
    col = jnp.sum(jnp.where(lane == lane_lo + pl.program_id(0), a, 0.0), axis=-1, keepdims=True)
    return col.reshape(n, CHUNK, 1)


def _gdn_common(q3, k3, v3, b, gc):
    n = q3.shape[0]
    ri = lax.broadcasted_iota(jnp.int32, (n, CHUNK, CHUNK), 1)
    ci = lax.broadcasted_iota(jnp.int32, (n, CHUNK, CHUNK), 2)
    lower, strict = ri >= ci, ri > ci
    eye = (ri == ci).astype(F32)
    gr = jnp.sum(gc * eye, axis=1, keepdims=True)
    qh, rq = _l2n(q3)
    qn = qh * (D_V ** -0.5)
    kn, rk = _l2n(k3)
    dec = jnp.where(lower, jnp.exp(jnp.where(lower, gc - gr, 0.0)), 0.0)
    kb = kn * b
    mm = _bdot_nt(kb, kn)
    tinv = _tri_inv(jnp.where(strict, mm * dec, 0.0), eye)
    gam = jnp.exp(gc)
    u = _bdot(tinv, v3 * b, exact=True)
    w = _bdot(tinv, kb * gam, exact=True)
    qk = _bdot_nt(qn, kn)
    aqk = jnp.where(lower, qk * dec, 0.0)
    gl = gc[:, CHUNK - 1:CHUNK, :]
    kdf = jnp.exp(gl - gc)
    return dict(ri=ri, ci=ci, lower=lower, strict=strict, eye=eye, qh=qh, rq=rq, qn=qn, kn=kn, rk=rk, dec=dec, kb=kb,
                mm=mm, gam=gam, u=u, w=w, qk=qk, aqk=aqk, gl=gl, kdf=kdf, kd=kn * kdf, gr=gr)


def _gdn_fwd(yc, p_main, gates, gcum, gn, name):
    t = yc.shape[0]
    n = t // CHUNK
    z_off = (2 * Q_LORA + 3 * GDN_W) // D_V

    def body(q_ref, k_ref, v_ref, z_ref, gt_ref, gcum_ref, gn_ref, o_ref, g_ref, s_ref, u_s, w_s, qg_s, kd_s, a_s, e_s):
        c = _gdn_common(q_ref[...].reshape(n, CHUNK, D_V), k_ref[...].reshape(n, CHUNK, D_V),
                        v_ref[...].reshape(n, CHUNK, D_V), _head_col(gt_ref, B_LO, n), _head_col(gcum_ref, A_LO, n))
        u_s[...] = c["u"]
        w_s[...] = c["w"]
        qg_s[...] = c["qn"] * c["gam"]
        kd_s[...] = c["kd"]
        a_s[...] = c["aqk"]
        e_s[...] = jnp.broadcast_to(jnp.exp(c["gl"]), (n, 1, D_V))

        def step(i, s):
            s_ref[0, i] = s
            v_new = u_s[i] - _dot(w_s[i], s)
            o = _dot(qg_s[i], s) + _dot(a_s[i], v_new)
            o_ref[pl.ds(pl.multiple_of(i * CHUNK, CHUNK), CHUNK), :] = o
            return s * e_s[i] + _dot_tn(kd_s[i], v_new)

        lax.fori_loop(0, n, step, jnp.zeros((D_V, D_V), F32))
        o = o_ref[...]
        zz = z_ref[...]
        on = o * lax.rsqrt(jnp.mean(o * o, axis=-1, keepdims=True) + EPS) * gn_ref[...]
        g_ref[...] = (on * zz * _sigmoid(zz)).astype(BF16)

    col = lambda off: pl.BlockSpec((t, D_V), lambda h: (0, off + h))
    lanes = pl.BlockSpec((t, LANES), lambda h: (0, 0))
    big = pltpu.VMEM((n, CHUNK, D_V), F32)
    return pl.pallas_call(
        body, grid=(HEADS,),
        in_specs=[col(0), col(HEADS), col(2 * HEADS), col(z_off), lanes, lanes, pl.BlockSpec((1, D_V), lambda h: (0, 0))],
        out_specs=[col(0), col(0), pl.BlockSpec((1, n, D_V, D_V), lambda h: (h, 0, 0, 0))],
        out_shape=[SDS((t, GDN_W), F32), SDS((t, GDN_W), BF16), SDS((HEADS, n, D_V, D_V), F32)],
        scratch_shapes=[big, big, big, big, pltpu.VMEM((n, CHUNK, CHUNK), F32), pltpu.VMEM((n, 1, D_V), F32)],
        name=name, compiler_params=_cp(("parallel",)))(yc, yc, yc, p_main, gates, gcum, gn)


def _gdn_bwd(yc, p_main, gates, gcum, gn, o_raw, states, dgated, name):
    t = yc.shape[0]
    n = t // CHUNK
    z_off = (2 * Q_LORA + 3 * GDN_W) // D_V

    def body(q_ref, k_ref, v_ref, z_ref, gt_ref, gcum_ref, gn_ref, o_ref, s_ref, dg_ref,
             dq_ref, dk_ref, dv_ref, dz_ref, dgt_ref, dgn_ref,
             u_s, w_s, qg_s, kd_s, at_s, e_s, do_s, du_s, dw_s, dqg_s, dkd_s, da_s, dat_s, dgs_s):
        @pl.when(pl.program_id(0) == 0)
        def _():
            dgn_ref[...] = jnp.zeros_like(dgn_ref)
            dgt_ref[...] = jnp.zeros_like(dgt_ref)

        o = o_ref[...]
        zz = z_ref[...]
        dgv = dg_ref[...]
        gnv = gn_ref[...]
        r = lax.rsqrt(jnp.mean(o * o, axis=-1, keepdims=True) + EPS)
        oh = o * r
        sg = _sigmoid(zz)
        don = dgv * zz * sg
        dz_ref[...] = (dgv * oh * gnv * sg * (1.0 + zz * (1.0 - sg))).astype(BF16)
        dgn_ref[...] += jnp.sum(don * oh, axis=0, keepdims=True)
        doh = don * gnv
        do_s[...] = (r * (doh - oh * jnp.mean(doh * oh, axis=-1, keepdims=True))).reshape(n, CHUNK, D_V)

        q3 = q_ref[...].reshape(n, CHUNK, D_V)
        k3 = k_ref[...].reshape(n, CHUNK, D_V)
        v3 = v_ref[...].reshape(n, CHUNK, D_V)
        b, gc = _head_col(gt_ref, B_LO, n), _head_col(gcum_ref, A_LO, n)
        c = _gdn_common(q3, k3, v3, b, gc)
        gr = c["gr"]
        ri, ci = c["ri"], c["ci"]
        upper, sup = ci >= ri, ci > ri
        dect = jnp.where(upper, jnp.exp(jnp.where(upper, gr - gc, 0.0)), 0.0)
        tinv_t = _tri_inv(jnp.where(sup, _bdot_nt(c["kn"], c["kb"]) * dect, 0.0), c["eye"])
        u_s[...] = c["u"]
        w_s[...] = c["w"]
        qg_s[...] = c["qn"] * c["gam"]
        kd_s[...] = c["kd"]
        at_s[...] = jnp.where(upper, _bdot_nt(c["kn"], c["qn"]) * dect, 0.0)
        e_s[...] = jnp.broadcast_to(jnp.exp(c["gl"]), (n, 1, D_V))

        def step(j, ds):
            i = n - 1 - j
            s = s_ref[0, i]
            do_i = do_s[i]
            v_new = u_s[i] - _dot(w_s[i], s)
            dvn = _dot(at_s[i], do_i) + _dot(kd_s[i], ds)
            da_s[i] = _dot_nt(do_i, v_new)
            dat_s[i] = _dot_nt(v_new, do_i)
            dqg_s[i] = _dot_nt(do_i, s)
            dw_s[i] = -_dot_nt(dvn, s)
            dkd_s[i] = _dot_nt(v_new, ds)
            du_s[i] = dvn
            dgs_s[i] = jnp.broadcast_to(jnp.sum(jnp.sum(s * ds, axis=1, keepdims=True), axis=0, keepdims=True), (1, D_V))
            return _dot_tn(qg_s[i], do_i) + e_s[i] * ds - _dot_tn(w_s[i], dvn)

        lax.fori_loop(0, n, step, jnp.zeros((D_V, D_V), F32))

        du, dw, dqg, dkd = du_s[...], dw_s[...], dqg_s[...], dkd_s[...]
        lower, strict, dec = c["lower"], c["strict"], c["dec"]
        kn, kb, qn, gam, kdf = c["kn"], c["kb"], c["qn"], c["gam"], c["kdf"]
        drv = _bdot(tinv_t, du, exact=True)
        drk = _bdot(tinv_t, dw, exact=True)
        dl = jnp.where(strict, -(_bdot_nt(drv, c["u"]) + _bdot_nt(drk, c["w"])), 0.0)
        dlt = jnp.where(sup, -(_bdot_nt(c["u"], drv) + _bdot_nt(c["w"], drk)), 0.0)
        da = jnp.where(lower, da_s[...], 0.0)
        dat = jnp.where(upper, dat_s[...], 0.0)
        e = (dl * c["mm"] + da * c["qk"]) * dec
        col_sums = jnp.sum(e, axis=1, keepdims=True)
        dgc = jnp.sum(e, axis=2, keepdims=True) - jnp.sum(col_sums * c["eye"], axis=2, keepdims=True)
        dkb = _bdot(dl * dec, kn) + gam * drk
        dkn = _bdot(dlt * dect, kb) + _bdot(dat * dect, qn) + b * dkb + dkd * kdf
        dqn = _bdot(da * dec, kn) + gam * dqg
        dgam = jnp.sum(drk * kb, axis=-1, keepdims=True) + jnp.sum(dqg * qn, axis=-1, keepdims=True)
        dbeta = jnp.sum(dkb * kn, axis=-1, keepdims=True) + jnp.sum(drv * v3, axis=-1, keepdims=True)
        dv_ref[...] = (b * drv).reshape(t, D_V)
        ee = jnp.sum(dkd * kn, axis=-1, keepdims=True) * kdf
        dgc = dgc + dgam * gam - ee
        rowc = lax.broadcasted_iota(jnp.int32, (n, CHUNK, 1), 1)
        tail = jnp.sum(ee, axis=1, keepdims=True) + dgs_s[...][:, :, 0:1] * jnp.exp(c["gl"])
        dgc = dgc + jnp.where(rowc == CHUNK - 1, tail, 0.0)
        lane = lax.broadcasted_iota(jnp.int32, (t, LANES), 1)
        head = pl.program_id(0)
        dgt_ref[...] += (jnp.where(lane == B_LO + head, dbeta.reshape(t, 1), 0.0)
                         + jnp.where(lane == A_LO + head, dgc.reshape(t, 1), 0.0))
        sc = D_V ** -0.5
        qh, rq, rk = c["qh"], c["rq"], c["rk"]
        dq_ref[...] = (rq * (sc * dqn - qh * jnp.sum(sc * dqn * qh, axis=-1, keepdims=True))).reshape(t, D_V)
        dk_ref[...] = (rk * (dkn - kn * jnp.sum(dkn * kn, axis=-1, keepdims=True))).reshape(t, D_V)

    once = pl.Buffered(1)
    col = lambda off: pl.BlockSpec((t, D_V), lambda h: (0, off + h), pipeline_mode=once)
    out_col = pl.BlockSpec((t, D_V), lambda h: (0, h))
    lanes = pl.BlockSpec((t, LANES), lambda h: (0, 0))
    row = pl.BlockSpec((1, D_V), lambda h: (0, 0))
    big = pltpu.VMEM((n, CHUNK, D_V), F32)
    sq = pltpu.VMEM((n, CHUNK, CHUNK), F32)
    small = pltpu.VMEM((n, 1, D_V), F32)
    return pl.pallas_call(
        body, grid=(HEADS,),
        in_specs=[col(0), col(HEADS), col(2 * HEADS), col(z_off), lanes, lanes, row, col(0),
                  pl.BlockSpec((1, n, D_V, D_V), lambda h: (h, 0, 0, 0), pipeline_mode=once), col(0)],
        out_specs=[out_col, out_col, out_col, out_col, lanes, row],
        out_shape=[SDS((t, GDN_W), F32), SDS((t, GDN_W), F32), SDS((t, GDN_W), F32), SDS((t, GDN_W), BF16),
                   SDS((t, LANES), F32), SDS((1, D_V), F32)],
        scratch_shapes=[big, big, big, big, sq, small, big, big, big, big, big, sq, sq, small],
        name=name, compiler_params=_cp(("arbitrary",)))(yc, yc, yc, p_main, gates, gcum, gn, o_raw, states, dgated)


def _col_tile(d):
    return _pick(d, (512, 256, 128))


def _mix_fwd(y_a, y_b, p_main, name):
    t, d = y_a.shape
    tm, cw = _row_tile(t), _col_tile(d)
    off_a, off_b = MAIN_FIXED // cw, (MAIN_FIXED + d) // cw

    def body(ya_ref, yb_ref, ga_ref, gb_ref, u_ref):
        u_ref[...] = (_sigmoid(ga_ref[...]) * ya_ref[...] + _sigmoid(gb_ref[...]) * yb_ref[...]).astype(BF16)

    blk = pl.BlockSpec((tm, cw), lambda i, j: (i, j))
    return pl.pallas_call(
        body, grid=(t // tm, d // cw),
        in_specs=[blk, blk, pl.BlockSpec((tm, cw), lambda i, j: (i, off_a + j)), pl.BlockSpec((tm, cw), lambda i, j: (i, off_b + j))],
        out_specs=blk, out_shape=SDS((t, d), BF16), name=name,
        compiler_params=_cp(("parallel", "parallel")))(y_a, y_b, p_main, p_main)


def _mix_bwd(du, y_a, y_b, p_main, name):
    t, d = y_a.shape
    tm, cw = _row_tile(t), _col_tile(d)
    off_a, off_b = MAIN_FIXED // cw, (MAIN_FIXED + d) // cw
    nb = d // cw

    def body(du_ref, ya_ref, yb_ref, ga_ref, gb_ref, dya_ref, dyb_ref, dla_ref, dlb_ref):
        duv = du_ref[...]
        ga, gb = _sigmoid(ga_ref[...]), _sigmoid(gb_ref[...])
        dya_ref[...] = (duv * ga).astype(BF16)
        dyb_ref[...] = (duv * gb).astype(BF16)
        dla_ref[...] = (duv * ya_ref[...] * ga * (1.0 - ga)).astype(BF16)
        dlb_ref[...] = (duv * yb_ref[...] * gb * (1.0 - gb)).astype(BF16)

    blk = pl.BlockSpec((tm, cw), lambda i, j: (i, j))
    outs = pl.pallas_call(
        body, grid=(t // tm, nb),
        in_specs=[blk, blk, blk, pl.BlockSpec((tm, cw), lambda i, j: (i, off_a + j)),
                  pl.BlockSpec((tm, cw), lambda i, j: (i, off_b + j))],
        out_specs=[blk, blk, blk, blk],
        out_shape=[SDS((t, d), BF16), SDS((t, d), BF16), SDS((t, d), BF16), SDS((t, d), BF16)], name=name,
        compiler_params=_cp(("parallel", "parallel")))(du, y_a, y_b, p_main, p_main)
    return outs


def _gate_res(x, y, gt, name):
    t, d = x.shape
    tm = _row_tile(t)

    def body(x_ref, y_ref, g_ref, o_ref):
        o_ref[...] = x_ref[...] + g_ref[...] * y_ref[...]

    blk = pl.BlockSpec((tm, d), lambda i: (i, 0))
    return pl.pallas_call(
        body, grid=(t // tm,), in_specs=[blk, blk, pl.BlockSpec((1, d), lambda i: (0, 0))], out_specs=blk,
        out_shape=SDS((t, d), F32), name=name, compiler_params=_cp(("parallel",)))(x, y, gt)


def _gate_res_bwd(dx, y, gt, name):
    t, d = dx.shape
    tm = _row_tile(t)

    def body(dx_ref, y_ref, g_ref, dg_ref, dy_ref):
        @pl.when(pl.program_id(0) == 0)
        def _():
            dg_ref[...] = jnp.zeros_like(dg_ref)

        dxv = dx_ref[...]
        dg_ref[...] += jnp.sum(dxv * y_ref[...], axis=0, keepdims=True)
        dy_ref[...] = (dxv * g_ref[...]).astype(BF16)

    blk = pl.BlockSpec((tm, d), lambda i: (i, 0))
    row = pl.BlockSpec((1, d), lambda i: (0, 0))
    return pl.pallas_call(
        body, grid=(t // tm,), in_specs=[blk, blk, row], out_specs=[row, blk],
        out_shape=[SDS((1, d), F32), SDS((t, d), BF16)], name=name, compiler_params=_cp(("arbitrary",)))(dx, y, gt)


def _swiglu_fwd(gu, name):
    t, f2 = gu.shape
    f = f2 // 2
    tm, cw = _row_tile(t), _col_tile(f)
    nb = f // cw

    def body(g_ref, u_ref, o_ref):
        g = g_ref[...]
        o_ref[...] = (g * _sigmoid(g) * u_ref[...]).astype(BF16)

    return pl.pallas_call(
        body, grid=(t // tm, nb),
        in_specs=[pl.BlockSpec((tm, cw), lambda i, j: (i, j)), pl.BlockSpec((tm, cw), lambda i, j: (i, nb + j))],
        out_specs=pl.BlockSpec((tm, cw), lambda i, j: (i, j)), out_shape=SDS((t, f), BF16), name=name,
        compiler_params=_cp(("parallel", "parallel")))(gu, gu)


def _swiglu_bwd(gu, da, name):
    t, f2 = gu.shape
    f = f2 // 2
    tm, cw = _row_tile(t), _col_tile(f)
    nb = f // cw

    def body(g_ref, u_ref, da_ref, dg_ref, dup_ref):
        g = g_ref[...]
        dav = da_ref[...]
        sg = _sigmoid(g)
        dg_ref[...] = (dav * u_ref[...] * sg * (1.0 + g * (1.0 - sg))).astype(BF16)
        dup_ref[...] = (dav * g * sg).astype(BF16)

    blk = pl.BlockSpec((tm, cw), lambda i, j: (i, j))
    dg, dup = pl.pallas_call(
        body, grid=(t // tm, nb),
        in_specs=[blk, pl.BlockSpec((tm, cw), lambda i, j: (i, nb + j)), blk], out_specs=[blk, blk],
        out_shape=[SDS((t, f), BF16), SDS((t, f), BF16)], name=name,
        compiler_params=_cp(("parallel", "parallel")))(gu, gu, da)
    return dg, dup


def _loss_head(x, w, target, name):
    t, d = x.shape
    tm = _row_tile(t)

    def body(x_ref, w_ref, t_ref, l_ref, dx_ref, dw_ref):
        @pl.when(pl.program_id(0) == 0)
        def _():
            l_ref[...] = jnp.zeros_like(l_ref)
            dw_ref[...] = jnp.zeros_like(dw_ref)

        xv = x_ref[...]
        wv = w_ref[...]
        r = lax.rsqrt(jnp.mean(xv * xv, axis=-1, keepdims=True) + EPS)
        xh = xv * r
        err = xh * wv - t_ref[...]
        per_tok = jnp.mean(err * err, axis=-1, keepdims=True)
        l_ref[...] += 0.5 * jnp.sum(per_tok, axis=0, keepdims=True)
        dy = err * (1.0 / d)
        dw_ref[...] += jnp.sum(dy * xh, axis=0, keepdims=True)
        dxh = dy * wv
        dx_ref[...] = r * (dxh - xh * jnp.mean(dxh * xh, axis=-1, keepdims=True))

    blk = pl.BlockSpec((tm, d), lambda i: (i, 0))
    row = pl.BlockSpec((1, d), lambda i: (0, 0))
    return pl.pallas_call(
        body, grid=(t // tm,), in_specs=[blk, row, blk],
        out_specs=[pl.BlockSpec((1, LANES), lambda i: (0, 0)), blk, row],
        out_shape=[SDS((1, LANES), F32), SDS((t, d), F32), SDS((1, d), F32)], name=name,
        compiler_params=_cp(("arbitrary",)))(x, w, target)


def _adamw(w, g, m, v, name):
    shape = w.shape
    cols = shape[-1]
    rows = w.size // cols
    w2, g2, m2, v2 = (a.reshape(rows, cols) for a in (w, g, m, v))
    lanes_padded = -(-cols // LANES) * LANES
    budget_rows = max(8, (20 * 1024 * 1024) // (lanes_padded * 4 * 16))
    tr = rows
    if rows > budget_rows:
        tr = _pick(rows, tuple(c for c in (1024, 512, 256, 128, 64, 32, 16, 8) if c <= budget_rows))
    c1 = 1.0 / (1.0 - ADAM_B1 ** ADAM_STEP)
    c2 = 1.0 / (1.0 - ADAM_B2 ** ADAM_STEP)

    def body(w_ref, g_ref, m_ref, v_ref, d_ref, mo_ref, vo_ref):
        gv = g_ref[...]
        mn = ADAM_B1 * m_ref[...] + (1.0 - ADAM_B1) * gv
        vn = ADAM_B2 * v_ref[...] + (1.0 - ADAM_B2) * (gv * gv)
        mo_ref[...] = mn
        vo_ref[...] = vn
        d_ref[...] = -ADAM_LR * ((mn * c1) / (jnp.sqrt(vn * c2) + ADAM_EPS) + ADAM_WD * w_ref[...])

    blk = pl.BlockSpec((tr, cols), lambda i: (i, 0))
    outs = pl.pallas_call(
        body, grid=(rows // tr,), in_specs=[blk, blk, blk, blk], out_specs=[blk, blk, blk],
        out_shape=[SDS((rows, cols), F32)] * 3, name=name, compiler_params=_cp(("parallel",)))(w2, g2, m2, v2)
    return tuple(o.reshape(shape) for o in outs)


KPE_LO = 2 * Q_LORA
QKVZ_LO = KPE_LO + D_ROPE
BA_LO = QKVZ_LO + 4 * GDN_W
GATE_LO = BA_LO + 2 * HEADS


def _lay_w_in(w_in):
    d = w_in.shape[0]
    main = jnp.concatenate([w_in[:, :KPE_LO], w_in[:, QKVZ_LO:BA_LO], w_in[:, GATE_LO:]], axis=1)
    small = jnp.concatenate([w_in[:, KPE_LO:QKVZ_LO], w_in[:, BA_LO:GATE_LO],
                             jnp.zeros((d, LANES - D_ROPE - 2 * HEADS), w_in.dtype)], axis=1)
    return main, small


def _unlay_w_in(g_main, g_small):
    return jnp.concatenate([g_main[:, :KPE_LO], g_small[:, :D_ROPE], g_main[:, KPE_LO:KPE_LO + 4 * GDN_W],
                            g_small[:, D_ROPE:D_ROPE + 2 * HEADS], g_main[:, MAIN_FIXED:]], axis=1)


def _lay_w_uq(w_uq):
    r = w_uq.reshape(Q_LORA, HEADS, D_NOPE + D_ROPE)
    r = jnp.pad(r, ((0, 0), (0, 0), (0, D_QK - D_NOPE - D_ROPE)))
    return r.reshape(Q_LORA, HEADS * D_QK)


def _unlay_w_uq(g):
    rows = g.shape[0]
    return g.reshape(rows, HEADS, D_QK)[:, :, :D_NOPE + D_ROPE].reshape(rows, HEADS * (D_NOPE + D_ROPE))


def _lay_w_ukv(w_ukv):
    return w_ukv.reshape(KV_LORA, HEADS, 2, D_V).transpose(0, 2, 1, 3).reshape(KV_LORA, 2 * HEADS * D_V)


def _unlay_w_ukv(g):
    rows = g.shape[0]
    return g.reshape(rows, 2, HEADS, D_V).transpose(0, 2, 1, 3).reshape(rows, 2 * HEADS * D_V)


def _lane_row(vec, lo):
    return jnp.pad(vec.reshape(1, -1), ((0, 0), (lo, LANES - lo - vec.shape[0])))


def _rope_tables(positions):
    half = D_ROPE // 2
    inv_freq = 1.0 / (10000.0 ** (jnp.arange(0, D_ROPE, 2, dtype=F32) / D_ROPE))
    ang = positions.astype(F32)[:, None] * inv_freq
    cos, sin = jnp.cos(ang), jnp.sin(ang)
    t = positions.shape[0]
    zeros = lambda n: jnp.zeros((t, n), F32)
    tc = jnp.concatenate([cos, cos, zeros(LANES - D_ROPE)], axis=1)
    ts1 = jnp.concatenate([-sin, zeros(LANES - half)], axis=1)
    ts2 = jnp.concatenate([zeros(half), sin, zeros(LANES - D_ROPE)], axis=1)
    return tc, ts1, ts2


def _layer_fwd(x, mod, wt, tabs, tag, mid=None):
    t, d = x.shape
    sh_a, sc_a, gt_a, sh_f, sc_f, gt_f = mod
    zero_l = jnp.zeros((1, Q_LORA), F32)
    s = dict(x=x)
    s["h1"] = _norm_fwd(x, 0, d, wt["norm_mix"], sc_a, sh_a, f"{tag}_norm_mix")
    s["p_main"] = _mm(s["h1"], wt["w_main"], name=f"{tag}_in_main")
    s["p_small"] = _mm(s["h1"], wt["w_small"], name=f"{tag}_in_small")
    s["cqn"] = _norm_fwd(s["p_main"], 0, Q_LORA, wt["q_a_norm"], zero_l, zero_l, f"{tag}_q_norm")
    s["ckvn"] = _norm_fwd(s["p_main"], 1, KV_LORA, wt["kv_a_norm"], zero_l, zero_l, f"{tag}_kv_norm")
    q_raw = _mm(s["cqn"], wt["w_uq"], name=f"{tag}_uq")
    s["kv_raw"] = _mm(s["ckvn"], wt["w_ukv"], name=f"{tag}_ukv")
    s["q_r"] = _rope_q(q_raw, *tabs, False, BF16, f"{tag}_rope_q")
    s["k_r"] = _k_assemble(s["kv_raw"], s["p_small"], *tabs, f"{tag}_k_asm")
    s["o"], s["lse"] = _attn_fwd(s["q_r"], s["k_r"], s["kv_raw"], HEADS, f"{tag}_attn")
    s["y_a"] = _mm(s["o"], wt["w_o_mla"], name=f"{tag}_o_mla")
    s["yc"] = _conv_fwd(s["p_main"], wt["conv_w"], f"{tag}_conv")
    s["gates"], s["gcum"] = _gdn_gates(s["p_small"], wt["a_row"], wt["dt_row"], f"{tag}_gates")
    s["o_raw"], s["gated"], s["states"] = _gdn_fwd(s["yc"], s["p_main"], s["gates"], s["gcum"], wt["gdn_norm"], f"{tag}_gdn")
    s["y_b"] = _mm(s["gated"], wt["w_o_gdn"], name=f"{tag}_o_gdn")
    if mid is not None:
        mid(s["y_b"])
    s["u"] = _mix_fwd(s["y_a"], s["y_b"], s["p_main"], f"{tag}_mix")
    s["mixo"] = _mm(s["u"], wt["w_o"], name=f"{tag}_o")
    s["x2"] = _gate_res(x, s["mixo"], gt_a, f"{tag}_res_a")
    s["h2"] = _norm_fwd(s["x2"], 0, d, wt["norm_ffn"], sc_f, sh_f, f"{tag}_norm_ffn")
    s["gu"] = _mm(s["h2"], wt["w_gate_up"], name=f"{tag}_gate_up")
    s["a"] = _swiglu_fwd(s["gu"], f"{tag}_swiglu")
    s["f"] = _mm(s["a"], wt["w_down"], name=f"{tag}_down")
    return _gate_res(s["x2"], s["f"], gt_f, f"{tag}_res_f"), s


def _layer_bwd(dx3, s, mod, wt, tabs, tag, stages=None):
    x = s["x"]
    t, d = x.shape
    sh_a, sc_a, gt_a, sh_f, sc_f, gt_f = mod
    zero_l = jnp.zeros((1, Q_LORA), F32)
    g = {}
    dgt_f, df = _gate_res_bwd(dx3, s["f"], gt_f, f"{tag}_b_res_f")
    da = _mm(df, wt["w_down"], tb=True, name=f"{tag}_b_down_x")
    g["w_down"] = _mm(s["a"].T, df, out_dtype=BF16, name=f"{tag}_b_down_w")
    dgate, dup = _swiglu_bwd(s["gu"], da, f"{tag}_b_swiglu")
    dgu = jnp.concatenate([dgate, dup], axis=1)
    dh2 = _mm(dgu, wt["w_gate_up"], tb=True, name=f"{tag}_b_gate_up_x")
    g["w_gate_up"] = _mm(s["h2"].T, dgu, out_dtype=BF16, name=f"{tag}_b_gate_up_w")
    if stages is not None:
        gt_a = gt_a + stages[0](g["w_gate_up"])
    dx2, g["norm_ffn"], dsc_f, dsh_f = _norm_bwd(s["x2"], 0, d, wt["norm_ffn"], sc_f, dh2, dx3, F32, f"{tag}_b_norm_ffn")
    dgt_a, dmixo = _gate_res_bwd(dx2, s["mixo"], gt_a, f"{tag}_b_res_a")
    du = _mm(dmixo, wt["w_o"], tb=True, name=f"{tag}_b_o_x")
    g["w_o"] = _mm(s["u"].T, dmixo, out_dtype=BF16, name=f"{tag}_b_o_w")
    dy_a, dy_b, dl_a, dl_b = _mix_bwd(du, s["y_a"], s["y_b"], s["p_main"], f"{tag}_b_mix")
    dgated = _mm(dy_b, wt["w_o_gdn"], tb=True, name=f"{tag}_b_o_gdn_x")
    g["w_o_gdn"] = _mm(s["gated"].T, dy_b, out_dtype=BF16, name=f"{tag}_b_o_gdn_w")
    dq_c, dk_c, dv_c, dz, dgates, g["gdn_norm"] = _gdn_bwd(
        s["yc"], s["p_main"], s["gates"], s["gcum"], wt["gdn_norm"], s["o_raw"], s["states"], dgated, f"{tag}_b_gdn")
    du_conv, g["conv_w"] = _conv_bwd(s["p_main"], wt["conv_w"], jnp.concatenate([dq_c, dk_c, dv_c], axis=1), f"{tag}_b_conv")
    do = _mm(dy_a, wt["w_o_mla"], tb=True, name=f"{tag}_b_o_mla_x")
    g["w_o_mla"] = _mm(s["o"].T, dy_a, out_dtype=BF16, name=f"{tag}_b_o_mla_w")
    dq_r, dk_r, dv = _attn_bwd(s["q_r"], s["k_r"], s["kv_raw"], HEADS, s["o"], s["lse"], do, f"{tag}_b_attn")
    q_a_norm = wt["q_a_norm"]
    if stages is not None:
        q_a_norm = q_a_norm + stages[1](dv)
    dq_raw = _rope_q(dq_r, *tabs, True, BF16, f"{tag}_b_rope_q")
    dkv_raw, dkpe = _k_assemble_bwd(dk_r, dv, *tabs, f"{tag}_b_k_asm")
    dcqn = _mm(dq_raw, wt["w_uq"], tb=True, name=f"{tag}_b_uq_x")
    g["w_uq"] = _mm(s["cqn"].T, dq_raw, out_dtype=BF16, name=f"{tag}_b_uq_w")
    dckvn = _mm(dkv_raw, wt["w_ukv"], tb=True, name=f"{tag}_b_ukv_x")
    g["w_ukv"] = _mm(s["ckvn"].T, dkv_raw, out_dtype=BF16, name=f"{tag}_b_ukv_w")
    dc_q, g["q_a_norm"], _, _ = _norm_bwd(s["p_main"], 0, Q_LORA, q_a_norm, zero_l, dcqn, None, BF16, f"{tag}_b_q_norm")
    dc_kv, g["kv_a_norm"], _, _ = _norm_bwd(s["p_main"], 1, KV_LORA, wt["kv_a_norm"], zero_l, dckvn, None, BF16,
                                            f"{tag}_b_kv_norm")
    dp_small, g["a_row"], g["dt_row"] = _gdn_gates_bwd(
        s["p_small"], wt["a_row"], wt["dt_row"], s["gates"], dgates, dkpe, f"{tag}_b_gates")
    dp_main = jnp.concatenate([dc_q, dc_kv, du_conv, dz, dl_a, dl_b], axis=1)
    h1t = s["h1"].T
    dh1 = _mm(dp_small, wt["w_small"], tb=True, name=f"{tag}_b_in_small_x")
    dh1 = _mm(dp_main, wt["w_main"], tb=True, acc_in=dh1, name=f"{tag}_b_in_main_x")
    g["w_main"] = _mm(h1t, dp_main, out_dtype=BF16, name=f"{tag}_b_in_main_w")
    g["w_small"] = _mm(h1t, dp_small, out_dtype=BF16, name=f"{tag}_b_in_small_w")
    dx, g["norm_mix"], dsc_a, dsh_a = _norm_bwd(x, 0, d, wt["norm_mix"], sc_a, dh1, dx2, F32, f"{tag}_b_norm_mix")
    if stages is not None:
        stages[2](dx)
    return dx, (dsh_a, dsc_a, dgt_a, dsh_f, dsc_f, dgt_f), g


def _layer_weights(big, full, l):
    w_main, w_small = _lay_w_in(big["w_in"])
    return dict(
        w_main=w_main, w_small=w_small, w_uq=_lay_w_uq(big["w_uq"]), w_ukv=_lay_w_ukv(big["w_ukv"]),
        w_o_mla=big["w_o_mla"], w_o_gdn=big["w_o_gdn"], w_o=big["w_o"], w_gate_up=big["w_gate_up"],
        w_down=big["w_down"], conv_w=full["conv_w"][l],
        norm_mix=full["norm_mix"][l][None], norm_ffn=full["norm_ffn"][l][None],
        q_a_norm=full["q_a_norm"][l][None], kv_a_norm=full["kv_a_norm"][l][None], gdn_norm=full["gdn_norm"][l][None],
        a_row=_lane_row(full["A_log"][l], A_LO), dt_row=_lane_row(full["dt_bias"][l], A_LO))


def _small_grads_ref_layout(g):
    return dict(
        conv_w=g["conv_w"], norm_mix=g["norm_mix"][0], norm_ffn=g["norm_ffn"][0], q_a_norm=g["q_a_norm"][0],
        kv_a_norm=g["kv_a_norm"][0], gdn_norm=g["gdn_norm"][0], A_log=g["a_row"][0, A_LO:A_HI],
        dt_bias=g["dt_row"][0, A_LO:A_HI])


def _local_step(x, mods, target, final_norm, full, positions):
    tabs = _rope_tables(positions)
    depth = len(mods)
    wts = [_layer_weights({n: full[n][l] for n, _ in BIG}, full, l) for l in range(depth)]
    saved = []
    h = x
    for l in range(depth):
        h, s = _layer_fwd(h, mods[l], wts[l], tabs, f"l{l}")
        saved.append(s)
    loss, dh, dfn = _loss_head(h, final_norm[None], target, "loss_head")
    dmods, grads = [None] * depth, [None] * depth
    for l in reversed(range(depth)):
        dh, dmods[l], grads[l] = _layer_bwd(dh, saved[l], mods[l], wts[l], tabs, f"l{l}")
    return loss, dh, dmods, grads, dfn[0]


HBM_SPEC = pl.BlockSpec(memory_space=pl.ANY)
VMEM_SPEC = pl.BlockSpec(memory_space=pltpu.VMEM)
N_CHIPS = 4
N_DEV = 8


def _me():
    return lax.axis_index("x"), lax.axis_index("y"), lax.axis_index("c")


def _flip(pos, f):
    mx, my, mc = pos
    fx, fy, fc = (f >> 2) & 1, (f >> 1) & 1, f & 1
    return ((mx + fx) % 2, (my + fy) % 2, (mc + fc) % 2)


def _all_gather_small(x, name):
    r, n = x.shape

    def body(x_ref, out_ref, send_sems, recv_sems, local_sem):
        me = _me()
        row = lambda p: 4 * p[0] + 2 * p[1] + p[2]
        mine = pltpu.make_async_copy(x_ref, out_ref.at[row(me)], local_sem)
        mine.start()

        def copy(f, origin):
            return pltpu.make_async_remote_copy(
                src_ref=x_ref, dst_ref=out_ref.at[row(origin)], send_sem=send_sems.at[f - 1], recv_sem=recv_sems.at[f - 1],
                device_id=_flip(me, f), device_id_type=MESH)

        sends = [copy(f, me) for f in range(1, N_DEV)]
        for cp in sends:
            cp.start()
        for f in range(1, N_DEV):
            copy(f, _flip(me, f)).wait_recv()
        for cp in sends:
            cp.wait_send()
        mine.wait()

    return pl.pallas_call(
        body, out_shape=SDS((N_DEV, r, n), x.dtype), in_specs=[VMEM_SPEC], out_specs=VMEM_SPEC,
        scratch_shapes=[pltpu.SemaphoreType.DMA((N_DEV - 1,)), pltpu.SemaphoreType.DMA((N_DEV - 1,)), pltpu.SemaphoreType.DMA],
        name=name, compiler_params=pltpu.CompilerParams(vmem_limit_bytes=VMEM_LIMIT))(x)


CHIP_FLIPS = (2, 4, 6)
SIBLING = 1


def _chip_of(pos):
    return 2 * pos[0] + pos[1]


def _dma_sems(n):
    return [pltpu.SemaphoreType.DMA((n,)), pltpu.SemaphoreType.DMA((n,))]


def _gather_weights(shards, name):
    n_arr = len(shards)

    def body(*refs):
        ins, outs = refs[:n_arr], refs[n_arr:2 * n_arr]
        send_sems, recv_sems = refs[2 * n_arr:]
        me = _me()
        layer = me[2]

        def ici(a, k, origin):
            return pltpu.make_async_remote_copy(
                src_ref=ins[a].at[layer], dst_ref=outs[a].at[layer, _chip_of(origin)], send_sem=send_sems.at[6 * a + k],
                recv_sem=recv_sems.at[6 * a + k], device_id=_flip(me, CHIP_FLIPS[k]), device_id_type=MESH)

        def d2d(a, k, lay):
            slab = outs[a].at[lay, _chip_of(_flip(me, CHIP_FLIPS[k]))]
            return pltpu.make_async_remote_copy(
                src_ref=slab, dst_ref=slab, send_sem=send_sems.at[6 * a + 3 + k], recv_sem=recv_sems.at[6 * a + 3 + k],
                device_id=_flip(me, SIBLING), device_id_type=MESH)

        sends = [ici(a, k, me) for a in range(n_arr) for k in range(3)]
        for cp in sends:
            cp.start()
        passed = []
        for a in range(n_arr):
            for k in range(3):
                ici(a, k, _flip(me, CHIP_FLIPS[k])).wait_recv()
                passed.append(d2d(a, k, layer))
                passed[-1].start()
        for a in range(n_arr):
            for k in range(3):
                d2d(a, k, 1 - layer).wait_recv()
        for cp in sends + passed:
            cp.wait_send()

    return pl.pallas_call(
        body, out_shape=[SDS((2, N_CHIPS) + s.shape[1:], s.dtype) for s in shards],
        in_specs=[HBM_SPEC] * n_arr, out_specs=[HBM_SPEC] * n_arr, scratch_shapes=_dma_sems(6 * n_arr), name=name)(*shards)


def _send_other_layer(g0, g1, name):
    n_arr = len(g0)

    def body(*refs):
        in0, in1, outs = refs[:n_arr], refs[n_arr:2 * n_arr], refs[2 * n_arr:3 * n_arr]
        send_sems, recv_sems = refs[3 * n_arr:]
        me = _me()

        def copy(a, src):
            return pltpu.make_async_remote_copy(
                src_ref=src, dst_ref=outs[a], send_sem=send_sems.at[a], recv_sem=recv_sems.at[a],
                device_id=_flip(me, SIBLING), device_id_type=MESH)

        @pl.when(me[2] == 0)
        def _():
            for a in range(n_arr):
                copy(a, in1[a]).start()

        @pl.when(me[2] == 1)
        def _():
            for a in range(n_arr):
                copy(a, in0[a]).start()

        for a in range(n_arr):
            copy(a, in0[a]).wait()

    return pl.pallas_call(
        body, out_shape=[SDS(g.shape, g.dtype) for g in g0], in_specs=[HBM_SPEC] * (2 * n_arr),
        out_specs=[HBM_SPEC] * n_arr, scratch_shapes=_dma_sems(n_arr), name=name)(*g0, *g1)


def _scatter_chips(ps, name):
    n_arr = len(ps)

    def body(*refs):
        ins, outs = refs[:n_arr], refs[n_arr:2 * n_arr]
        send_sems, recv_sems = refs[2 * n_arr:]
        me = _me()

        def copy(a, k):
            return pltpu.make_async_remote_copy(
                src_ref=ins[a].at[_chip_of(_flip(me, CHIP_FLIPS[k]))], dst_ref=outs[a].at[k],
                send_sem=send_sems.at[3 * a + k], recv_sem=recv_sems.at[3 * a + k],
                device_id=_flip(me, CHIP_FLIPS[k]), device_id_type=MESH)

        copies = [copy(a, k) for a in range(n_arr) for k in range(3)]
        for cp in copies:
            cp.start()
        for cp in copies:
            cp.wait_recv()
        for cp in copies:
            cp.wait_send()

    return pl.pallas_call(
        body, out_shape=[SDS((3,) + p.shape[1:], p.dtype) for p in ps], in_specs=[HBM_SPEC] * n_arr,
        out_specs=[HBM_SPEC] * n_arr, scratch_shapes=_dma_sems(3 * n_arr), name=name)(*ps)


def _swap_layers(rs, name):
    n_arr = len(rs)

    def body(*refs):
        ins, outs = refs[:n_arr], refs[n_arr:2 * n_arr]
        send_sems, recv_sems = refs[2 * n_arr:]
        me = _me()
        copies = [pltpu.make_async_remote_copy(
            src_ref=ins[a], dst_ref=outs[a], send_sem=send_sems.at[a], recv_sem=recv_sems.at[a],
            device_id=_flip(me, SIBLING), device_id_type=MESH) for a in range(n_arr)]
        for cp in copies:
            cp.start()
        for cp in copies:
            cp.wait()

    return pl.pallas_call(
        body, out_shape=[SDS(r.shape, r.dtype) for r in rs], in_specs=[HBM_SPEC] * n_arr, out_specs=[HBM_SPEC] * n_arr,
        scratch_shapes=_dma_sems(n_arr), name=name)(*rs)


def _add_sibling(g0, g1, got, layer, col_shards, name):
    k, n = g0.shape
    tr = _pick(k, (512, 256, 128))
    width = n // N_CHIPS if col_shards else n
    cw = _pick(width, (1024, 512, 256, 128))
    per = width // cw

    def body(l_ref, a0_ref, a1_ref, b_ref, o_ref):
        mine = jnp.where(l_ref[0] == 0, a0_ref[...], a1_ref[...])
        o_ref[...] = (mine.astype(F32) + b_ref[...].astype(F32)).astype(o_ref.dtype)

    blk = pl.BlockSpec((tr, cw), lambda i, j, l: (i, j))
    if col_shards:
        out_spec = pl.BlockSpec((None, tr, cw), lambda i, j, l: (j // per, i, j % per))
        out_shape = SDS((N_CHIPS, k, width), g0.dtype)
    else:
        out_spec, out_shape = blk, SDS((k, n), g0.dtype)
    grid_spec = pltpu.PrefetchScalarGridSpec(
        num_scalar_prefetch=1, grid=(k // tr, n // cw), in_specs=[blk, blk, blk], out_specs=out_spec)
    return pl.pallas_call(body, grid_spec=grid_spec, out_shape=out_shape, name=name,
                          compiler_params=_cp(("parallel", "parallel")))(layer, g0, g1, got)


def _sum_chips(p, got, chip, name):
    _, k, ns = p.shape
    tr = _pick(k, (256, 128, 64, 32, 16))

    def body(c_ref, a_ref, b_ref, o_ref):
        acc = a_ref[0].astype(F32)
        for j in range(3):
            acc = acc + b_ref[j].astype(F32)
        o_ref[...] = acc

    grid_spec = pltpu.PrefetchScalarGridSpec(
        num_scalar_prefetch=1, grid=(k // tr,),
        in_specs=[pl.BlockSpec((1, tr, ns), lambda i, c: (c[0], i, 0)), pl.BlockSpec((3, tr, ns), lambda i, c: (0, i, 0))],
        out_specs=pl.BlockSpec((tr, ns), lambda i, c: (i, 0)))
    return pl.pallas_call(
        body, grid_spec=grid_spec, out_shape=SDS((k, ns), F32), name=name, compiler_params=_cp(("parallel",)))(chip, p, got)


SEM_SPEC = pl.BlockSpec(memory_space=pltpu.SEMAPHORE)
HBM_ONLY = pl.BlockSpec(memory_space=pltpu.HBM)
DATAFLOW = pltpu.SideEffectType.DATAFLOW_SIDE_EFFECTING


def _in_hbm(a):
    return pltpu.with_memory_space_constraint(a, pltpu.HBM)


def _copies_start(name, srcs, lands, plan, n_copies):
    ns, nl = len(srcs), len(lands)

    def body(*refs):
        src_refs, land_refs = refs[:ns], refs[ns:ns + nl]
        send_sems, recv_sems = refs[ns + nl], refs[ns + nl + 1]
        token = refs[-1]
        for i, (src, dst, peer) in enumerate(plan(_me(), src_refs, land_refs)):
            pltpu.make_async_remote_copy(src_ref=src, dst_ref=dst, send_sem=send_sems.at[i], recv_sem=recv_sems.at[i],
                                         device_id=peer, device_id_type=MESH).start()
        token[...] = jnp.zeros_like(token)

    outs = pl.pallas_call(
        body, name=name,
        out_shape=(pltpu.SemaphoreType.DMA((n_copies,)), pltpu.SemaphoreType.DMA((n_copies,)),
                   *[pltpu.HBM(l.shape, l.dtype) for l in lands], SDS((8, LANES), F32)),
        in_specs=[HBM_ONLY] * (ns + nl), out_specs=(SEM_SPEC, SEM_SPEC, *[HBM_ONLY] * nl, VMEM_SPEC),
        input_output_aliases={ns + i: 2 + i for i in range(nl)},
        compiler_params=pltpu.CompilerParams(has_side_effects=DATAFLOW),
    )(*[_in_hbm(s) for s in srcs], *[_in_hbm(l) for l in lands])
    return outs[0], outs[1], list(outs[2:2 + nl]), outs[-1]


def _copies_wait(name, srcs, lands, send_sems, recv_sems, plan, after):
    ns, nl = len(srcs), len(lands)

    def body(*refs):
        src_refs, land_refs = refs[:ns], refs[ns:ns + nl]
        send_ref, recv_ref = refs[ns + nl], refs[ns + nl + 1]
        for i, (src, dst, peer) in enumerate(plan(_me(), src_refs, land_refs)):
            cp = pltpu.make_async_remote_copy(src_ref=src, dst_ref=dst, send_sem=send_ref.at[i], recv_sem=recv_ref.at[i],
                                              device_id=peer, device_id_type=MESH)
            cp.wait_send()
            cp.wait_recv()

    outs = pl.pallas_call(
        body, name=name, out_shape=[pltpu.HBM(l.shape, l.dtype) for l in lands],
        in_specs=[HBM_ONLY] * (ns + nl) + [SEM_SPEC, SEM_SPEC, HBM_SPEC], out_specs=[HBM_ONLY] * nl,
        input_output_aliases={ns + i: i for i in range(nl)},
        compiler_params=pltpu.CompilerParams(has_side_effects=DATAFLOW),
    )(*[_in_hbm(s) for s in srcs], *lands, send_sems, recv_sems, after)
    return list(outs)


def _half(ref, rows, axis):
    idx = [slice(None)] * axis + [rows]
    return ref.at[tuple(idx)]


def _gather_plans(layer, halves):
    def ici(me, srcs, lands):
        out = []
        for a, kh in enumerate(halves):
            rows = pl.ds(pl.multiple_of(me[2] * kh, 16), kh)
            for k in range(3):
                out.append((srcs[a].at[layer, rows], lands[a].at[_chip_of(me), rows], _flip(me, CHIP_FLIPS[k])))
        return out

    def d2d(me, srcs, lands):
        out = []
        for a, kh in enumerate(halves):
            rows = pl.ds(pl.multiple_of(me[2] * kh, 16), kh)
            for k in range(3):
                slab = lands[a].at[_chip_of(_flip(me, CHIP_FLIPS[k])), rows]
                out.append((slab, slab, _flip(me, SIBLING)))
        return out

    return ici, d2d


def _to_sibling_plan(halves, axes):
    def plan(me, srcs, lands):
        out = []
        for a, (kh, axis) in enumerate(zip(halves, axes)):
            rows = pl.ds(pl.multiple_of((1 - me[2]) * kh, 16), kh)
            out.append((_half(srcs[a], rows, axis), lands[a], _flip(me, SIBLING)))
        return out

    return plan


def _to_chips_plan(n_arr):
    def plan(me, srcs, lands):
        out = []
        for a in range(n_arr):
            for k in range(3):
                peer = _flip(me, CHIP_FLIPS[k])
                out.append((srcs[a].at[_chip_of(peer)], lands[a].at[k], peer))
        return out

    return plan


def _swap_plan(n_arr):
    def plan(me, srcs, lands):
        return [(srcs[a], lands[a], _flip(me, SIBLING)) for a in range(n_arr)]

    return plan


def _add_half(g, got, half, col_shards, name):
    s, kh, n = got.shape
    tr = _pick(kh, (512, 256, 128, 64, 32, 16))
    nrb = kh // tr
    width = n // N_CHIPS if col_shards else n
    cw = _pick(width, (1024, 512, 256, 128))
    per = width // cw

    def body(h_ref, a_ref, b_ref, o_ref):
        o_ref[...] = (a_ref[...].astype(F32) + b_ref[...].astype(F32)).astype(o_ref.dtype)

    in_specs = [pl.BlockSpec((None, tr, cw), lambda j, i, c, h: (j, h[0] * nrb + i, c)),
                pl.BlockSpec((None, tr, cw), lambda j, i, c, h: (j, i, c))]
    if col_shards:
        assert s == 1
        out_spec = pl.BlockSpec((None, tr, cw), lambda j, i, c, h: (c // per, i, c % per))
        out_shape = SDS((N_CHIPS, kh, width), g.dtype)
    else:
        out_spec, out_shape = in_specs[1], SDS((s, kh, n), g.dtype)
    grid_spec = pltpu.PrefetchScalarGridSpec(num_scalar_prefetch=1, grid=(s, nrb, n // cw), in_specs=in_specs,
                                             out_specs=out_spec)
    return pl.pallas_call(body, grid_spec=grid_spec, out_shape=out_shape, name=name,
                          compiler_params=_cp(("parallel", "parallel", "parallel")))(half, g, got)


def _sum_devices(g, name):
    _, _, n = g.shape

    def body(g_ref, o_ref):
        acc = g_ref[0]
        for k in range(1, N_DEV):
            acc = acc + g_ref[k]
        o_ref[...] = acc

    return pl.pallas_call(body, out_shape=SDS((1, n), F32), in_specs=[VMEM_SPEC], out_specs=VMEM_SPEC, name=name)(g)


def _silu_rows(c, name):
    def body(c_ref, o_ref):
        v = c_ref[...]
        o_ref[...] = v * _sigmoid(v)

    return pl.pallas_call(body, out_shape=SDS(c.shape, F32), in_specs=[VMEM_SPEC], out_specs=VMEM_SPEC, name=name)(c)


BIG = (("w_in", 2), ("w_uq", 2), ("w_ukv", 2), ("w_o_mla", 2), ("w_o_gdn", 2), ("w_o", 1), ("w_gate_up", 2), ("w_down", 1))
KERNEL_BIG = ("w_main", "w_small", "w_uq", "w_ukv", "w_o_mla", "w_o_gdn", "w_o", "w_gate_up", "w_down")
COL_SHARDED_AS_IS = ("w_o_mla", "w_o_gdn", "w_gate_up")
ROW_SHARDED = ("w_o", "w_down")
SMALL = ("norm_mix", "norm_ffn", "q_a_norm", "kv_a_norm", "A_log", "dt_bias", "gdn_norm")
WEIGHTS = ("w_ada", "b_ada", "norm_mix", "norm_ffn", "w_in", "q_a_norm", "kv_a_norm", "w_uq", "w_ukv", "w_o_mla", "conv_w",
           "A_log", "dt_bias", "gdn_norm", "w_o_gdn", "w_o", "w_gate_up", "w_down", "final_norm")
ADA_PAD = 16
K_PAD = 128


def _pad_to(a, n, axis):
    pad = [(0, 0)] * a.ndim
    pad[axis] = (0, n - a.shape[axis])
    return jnp.pad(a, pad)


def kernel(x, c, positions, w_ada, b_ada, norm_mix, norm_ffn, w_in, q_a_norm, kv_a_norm, w_uq, w_ukv, w_o_mla, conv_w, A_log, dt_bias, gdn_norm, w_o_gdn, w_o, w_gate_up, w_down, final_norm, loss_target, m_w_ada, m_b_ada, m_norm_mix, m_norm_ffn, m_w_in, m_q_a_norm, m_kv_a_norm, m_w_uq, m_w_ukv, m_w_o_mla, m_conv_w, m_A_log, m_dt_bias, m_gdn_norm, m_w_o_gdn, m_w_o, m_w_gate_up, m_w_down, m_final_norm, v_w_ada, v_b_ada, v_norm_mix, v_norm_ffn, v_w_in, v_q_a_norm, v_kv_a_norm, v_w_uq, v_w_ukv, v_w_o_mla, v_conv_w, v_A_log, v_dt_bias, v_gdn_norm, v_w_o_gdn, v_w_o, v_w_gate_up, v_w_down, v_final_norm):
    env = dict(locals())
    w = {n: env[n] for n in WEIGHTS}
    depth, d = norm_mix.shape
    t = x.shape[1]
    me = _me()
    chip = _chip_of(me)
    dev = 4 * me[0] + 2 * me[1] + me[2]
    ada_cols = w_ada.shape[2]

    half_idx = me[2].astype(jnp.int32).reshape(1)
    chip_idx = chip.astype(jnp.int32).reshape(1)
    w16 = [w[n].astype(BF16) for n, _ in BIG]
    w_halves = [a.shape[1] // 2 for a in w16]
    n_gather = 3 * len(w16)
    gather_plans = [_gather_plans(l, w_halves) for l in range(depth)]
    in_flight, token = [], jnp.zeros((8, LANES), F32)
    for l in range(depth):
        landing = [lax.empty((N_CHIPS,) + a.shape[1:], BF16) for a in w16]
        send_s, recv_s, landing, token = _copies_start(f"gather_ici_start_l{l}", w16 + [token], landing, gather_plans[l][0],
                                                       n_gather)
        in_flight.append((send_s, recv_s, landing))

    def pass_to_sibling(l, after):
        send_s, recv_s, landing = in_flight[l]
        landing = _copies_wait(f"gather_ici_wait_l{l}", w16 + [token], landing, send_s, recv_s, gather_plans[l][0], after)
        in_flight[l] = _copies_start(f"gather_d2d_start_l{l}", [], landing, gather_plans[l][1], n_gather)

    def gathered_layer(l, after):
        send_s, recv_s, landing, tok = in_flight[l]
        landing = _copies_wait(f"gather_d2d_wait_l{l}", [], landing, send_s, recv_s, gather_plans[l][1],
                               tok if after is None else after)
        return {n: jnp.concatenate([jnp.where(chip == j, own[l], got[j]) for j in range(N_CHIPS)], axis=axis - 1)
                for (n, axis), own, got in zip(BIG, w16, landing)}

    full = {}
    conv_all = _all_gather_small(conv_w.reshape(1, -1), "gather_conv").reshape((N_DEV,) + conv_w.shape)
    full["conv_w"] = jnp.concatenate([conv_all[2 * j] for j in range(N_CHIPS)], axis=2)
    for n in SMALL:
        full[n] = w[n]

    c_all = _all_gather_small(c, "gather_c").reshape(N_DEV, d)
    c_act = _silu_rows(_pad_to(c_all, ADA_PAD, 0), "silu_c")
    b_cols = lax.dynamic_slice_in_dim(b_ada, chip * ada_cols, ada_cols, axis=1)
    mod_cols = jnp.stack([
        _mm(c_act, w_ada[l], acc_in=jnp.broadcast_to(b_cols[l][None], (ADA_PAD, ada_cols)), name=f"ada_l{l}")[:N_DEV]
        for l in range(depth)])
    mod_all = _all_gather_small(mod_cols.reshape(depth * N_DEV, ada_cols), "gather_mod")
    mod_all = mod_all.reshape(N_DEV, depth, N_DEV, ada_cols)
    mods = []
    for l in range(depth):
        mine = jnp.concatenate([lax.dynamic_index_in_dim(mod_all[2 * j, l], dev, axis=0, keepdims=True)
                                for j in range(N_CHIPS)], axis=1)
        mods.append(tuple(mine[:, i * d:(i + 1) * d] for i in range(6)))

    tabs = _rope_tables(positions[0])
    pass_to_sibling(0, token)
    wts = [_layer_weights(gathered_layer(0, None), full, 0)]
    h, s0 = _layer_fwd(x[0], mods[0], wts[0], tabs, "l0", mid=functools.partial(pass_to_sibling, 1))
    wts.append(_layer_weights(gathered_layer(1, h), full, 1))
    h, s1 = _layer_fwd(h, mods[1], wts[1], tabs, "l1")
    saved = [s0, s1]
    loss_part, dh, dfn = _loss_head(h, final_norm[None], loss_target[0], "loss_head")
    dfn = dfn[0]
    loss = lax.psum(loss_part[0, 0], AXES)

    def col_shards(g):
        return g.reshape(g.shape[0], N_CHIPS, g.shape[1] // N_CHIPS).transpose(1, 0, 2)

    def reduce_scatter_stages(l, g):
        tag = f"rs_l{l}"
        srcs = [g[n].reshape(N_CHIPS, -1, g[n].shape[1]) if n in ROW_SHARDED else g[n] for n in KERNEL_BIG]
        axes = [1 if n in ROW_SHARDED else 0 for n in KERNEL_BIG]
        halves = [a.shape[ax] // 2 for a, ax in zip(srcs, axes)]
        got_shapes = [a.shape[:ax] + (kh,) + a.shape[ax + 1:] for a, ax, kh in zip(srcs, axes, halves)]
        plan_a, plan_c, plan_e = _to_sibling_plan(halves, axes), _to_chips_plan(len(BIG)), _swap_plan(len(BIG))
        st, out = {}, {}
        st["a"] = _copies_start(f"{tag}_sibling_start", srcs, [lax.empty(sh, BF16) for sh in got_shapes], plan_a, len(srcs))

        def after_or(tok, after):
            return tok if after is None else after

        def stage0(after):
            send_s, recv_s, landing, tok = st["a"]
            got = _copies_wait(f"{tag}_sibling_wait", srcs, landing, send_s, recv_s, plan_a, after_or(tok, after))
            sums = {}
            for n, a, b in zip(KERNEL_BIG, srcs, got):
                a3, b3 = (v if v.ndim == 3 else v[None] for v in (a, b))
                r = _add_half(a3, b3, half_idx, n in COL_SHARDED_AS_IS, f"{tag}_add_{n}")
                sums[n] = r if (n in COL_SHARDED_AS_IS or n in ROW_SHARDED) else r[0]
            chip_sums = dict(
                w_in=col_shards(_unlay_w_in(sums["w_main"], sums["w_small"])), w_uq=col_shards(_unlay_w_uq(sums["w_uq"])),
                w_ukv=col_shards(_unlay_w_ukv(sums["w_ukv"])), w_o_mla=sums["w_o_mla"], w_o_gdn=sums["w_o_gdn"],
                w_o=sums["w_o"], w_gate_up=sums["w_gate_up"], w_down=sums["w_down"])
            st["p"] = [chip_sums[n] for n, _ in BIG]
            st["c"] = _copies_start(f"{tag}_chips_start", st["p"], [lax.empty((3,) + p.shape[1:], BF16) for p in st["p"]],
                                    plan_c, 3 * len(BIG))
            return st["c"][3][0, 0]

        def stage1(after):
            send_s, recv_s, landing, tok = st["c"]
            got = _copies_wait(f"{tag}_chips_wait", st["p"], landing, send_s, recv_s, plan_c, after_or(tok, after))
            st["r"] = [_sum_chips(p, q, chip_idx, f"{tag}_sum_{n}") for (n, _), p, q in zip(BIG, st["p"], got)]
            st["e"] = _copies_start(f"{tag}_swap_start", st["r"], [lax.empty(r.shape, F32) for r in st["r"]], plan_e, len(BIG))
            return st["e"][3][0, 0]

        def stage2(after):
            send_s, recv_s, landing, tok = st["e"]
            got = _copies_wait(f"{tag}_swap_wait", st["r"], landing, send_s, recv_s, plan_e, after_or(tok, after))
            for (n, _), mine, theirs in zip(BIG, st["r"], got):
                out[n] = jnp.where(me[2] == 0, jnp.concatenate([mine, theirs]), jnp.concatenate([theirs, mine]))

        return (stage0, stage1, stage2), out, st["a"][3][0, 0]

    dmods, grads = [None] * depth, [None] * depth
    dh, dmods[1], grads[1] = _layer_bwd(dh, saved[1], mods[1], wts[1], tabs, "l1")
    stages1, reduced1, started = reduce_scatter_stages(1, grads[1])
    tied = mods[0][:5] + (mods[0][5] + started,)
    dx, dmods[0], grads[0] = _layer_bwd(dh, saved[0], tied, wts[0], tabs, "l0", stages=stages1)
    stages0, reduced0, _ = reduce_scatter_stages(0, grads[0])
    for stage in stages0:
        stage(None)
    g_out = {n: jnp.stack([reduced0[n], reduced1[n]]) for n, _ in BIG}

    small = [_small_grads_ref_layout(grads[l]) for l in range(depth)]
    small_parts = [jnp.concatenate(dmods[l], axis=1).reshape(-1) for l in range(depth)]
    small_parts += [jnp.stack([small[l][n] for l in range(depth)]).reshape(-1) for n in SMALL]
    small_parts.append(dfn)
    small_sizes = [p.shape[0] for p in small_parts]
    packed = jnp.concatenate(small_parts)
    n_small = -(-packed.shape[0] // LANES) * LANES
    small_all = _all_gather_small(_pad_to(packed, n_small, 0).reshape(1, n_small), "gather_small_grads")
    small_sum = _sum_devices(small_all, "sum_small_grads")[0]
    offs = [0]
    for sz in small_sizes:
        offs.append(offs[-1] + sz)
    g_out["b_ada"] = jnp.stack([small_sum[offs[l]:offs[l + 1]] for l in range(depth)])
    for i, n in enumerate(SMALL):
        g_out[n] = small_sum[offs[depth + i]:offs[depth + i + 1]].reshape(w[n].shape)
    g_out["final_norm"] = small_sum[offs[depth + len(SMALL)]:offs[depth + len(SMALL) + 1]]

    c_act_t = _pad_to(c_act[:N_DEV].T, K_PAD, 1)
    g_ada = []
    for l in range(depth):
        dmod_l = small_all[:, 0, offs[l]:offs[l + 1]]
        dmod_cols = lax.dynamic_slice_in_dim(dmod_l, chip * ada_cols, ada_cols, axis=1)
        g_ada.append(_mm(c_act_t, _pad_to(dmod_cols, K_PAD, 0), name=f"ada_grad_l{l}"))
    g_out["w_ada"] = jnp.stack(g_ada)

    conv_g = jnp.stack([small[l]["conv_w"] for l in range(depth)])
    conv_all_g = _all_gather_small(conv_g.reshape(1, -1), "gather_conv_grads")
    conv_sum = _sum_devices(conv_all_g, "sum_conv_grads").reshape(conv_g.shape)
    n_cc = conv_w.shape[2]
    g_out["conv_w"] = lax.dynamic_slice_in_dim(conv_sum, chip * n_cc, n_cc, axis=2)

    deltas, new_m, new_v = {}, {}, {}
    for n in WEIGHTS:
        deltas[n], new_m[n], new_v[n] = _adamw(w[n], g_out[n], env["m_" + n], env["v_" + n], f"adamw_{n}")
    return (loss, dx[None], *[g_out[n] for n in WEIGHTS], *[deltas[n] for n in WEIGHTS],
            *[new_m[n] for n in WEIGHTS], *[new_v[n] for n in WEIGHTS])
```

```python
import functools
import math

import jax
import jax.numpy as jnp
from jax import lax
from jax.experimental import pallas as pl
from jax.experimental.pallas import tpu as pltpu

F32 = jnp.float32
BF16 = jnp.bfloat16
SDS = jax.ShapeDtypeStruct
MESH = pl.DeviceIdType.MESH
AXES = ("x", "y", "c")

EPS = 1e-6
HEADS = 8
D_NOPE = 128
D_ROPE = 64
D_QK = 256
D_V = 128
Q_LORA = 512
KV_LORA = 512
CHUNK = 64
CONV_WIDTH = 4
GDN_W = HEADS * D_V
MAIN_FIXED = 2 * Q_LORA + 4 * GDN_W
LANES = 128
VMEM_LIMIT = 56 * 1024 * 1024
ADAM_LR, ADAM_B1, ADAM_B2, ADAM_EPS, ADAM_WD, ADAM_STEP = 0.001, 0.9, 0.999, 1e-8, 0.01, 10


def _pick(n, cands):
    for cand in cands:
        if n % cand == 0:
            return cand
    return n


def _cp(sem):
    return pltpu.CompilerParams(dimension_semantics=sem, vmem_limit_bytes=VMEM_LIMIT)


def _row_tile(t):
    return _pick(t, (256, 128, 64, 32, 16, 8))


def _dot(a, b):
    return jnp.dot(a.astype(BF16), b.astype(BF16), preferred_element_type=F32)


def _dot_nt(a, b):
    return lax.dot_general(a.astype(BF16), b.astype(BF16), (((1,), (1,)), ((), ())), preferred_element_type=F32)


def _dot_tn(a, b):
    return lax.dot_general(a.astype(BF16), b.astype(BF16), (((0,), (0,)), ((), ())), preferred_element_type=F32)


def _bdot(a, b, exact=False):
    if exact:
        return lax.dot_general(a, b, (((2,), (1,)), ((0,), (0,))), precision=lax.Precision.HIGHEST,
                               preferred_element_type=F32)
    return lax.dot_general(a.astype(BF16), b.astype(BF16), (((2,), (1,)), ((0,), (0,))), preferred_element_type=F32)


def _bdot_nt(a, b):
    return lax.dot_general(a.astype(BF16), b.astype(BF16), (((2,), (2,)), ((0,), (0,))), preferred_element_type=F32)


def _sigmoid(x):
    return 1.0 / (1.0 + jnp.exp(-x))


def _mm(a, b, *, tb=False, out_dtype=F32, acc_in=None, name):
    m, k = a.shape
    n = b.shape[0] if tb else b.shape[1]
    assert (b.shape[1] if tb else b.shape[0]) == k
    tm = _pick(m, (1024, 512, 256, 128))
    tn = _pick(n, (1024, 512, 256, 128))
    tk = k if k <= 2048 else _pick(k, (512, 256, 128))
    nk = k // tk
    has_acc = acc_in is not None

    def body_one_step(*refs):
        a_ref, b_ref = refs[:2]
        o_ref = refs[-1]
        acc = _dot_nt(a_ref[...], b_ref[...]) if tb else _dot(a_ref[...], b_ref[...])
        if has_acc:
            acc = acc + refs[2][...].astype(F32)
        o_ref[...] = acc.astype(out_dtype)

    if nk == 1:
        in_specs = [pl.BlockSpec((tm, k), lambda i, j: (i, 0)),
                    pl.BlockSpec((tn, k), lambda i, j: (j, 0)) if tb else pl.BlockSpec((k, tn), lambda i, j: (0, j))]
        args = [a, b]
        if has_acc:
            in_specs.append(pl.BlockSpec((tm, tn), lambda i, j: (i, j)))
            args.append(acc_in)
        return pl.pallas_call(
            body_one_step, grid=(m // tm, n // tn), in_specs=in_specs, out_specs=pl.BlockSpec((tm, tn), lambda i, j: (i, j)),
            out_shape=SDS((m, n), out_dtype), name=name, compiler_params=_cp(("parallel", "parallel")))(*args)

    def body(*refs):
        if has_acc:
            a_ref, b_ref, c_ref, o_ref, acc = refs
        else:
            a_ref, b_ref, o_ref, acc = refs
        kk = pl.program_id(2)

        @pl.when(kk == 0)
        def _():
            if has_acc:
                acc[...] = c_ref[...].astype(F32)
            else:
                acc[...] = jnp.zeros_like(acc)

        if tb:
            acc[...] += _dot_nt(a_ref[...], b_ref[...])
        else:
            acc[...] += _dot(a_ref[...], b_ref[...])

        @pl.when(kk == nk - 1)
        def _():
            o_ref[...] = acc[...].astype(out_dtype)

    in_specs = [pl.BlockSpec((tm, tk), lambda i, j, kk: (i, kk)),
                pl.BlockSpec((tn, tk), lambda i, j, kk: (j, kk)) if tb
                else pl.BlockSpec((tk, tn), lambda i, j, kk: (kk, j))]
    args = [a, b]
    if has_acc:
        in_specs.append(pl.BlockSpec((tm, tn), lambda i, j, kk: (i, j)))
        args.append(acc_in)
    return pl.pallas_call(
        body, grid=(m // tm, n // tn, nk), in_specs=in_specs,
        out_specs=pl.BlockSpec((tm, tn), lambda i, j, kk: (i, j)),
        out_shape=SDS((m, n), out_dtype), scratch_shapes=[pltpu.VMEM((tm, tn), F32)],
        name=name, compiler_params=_cp(("parallel", "parallel", "arbitrary")))(*args)


def _norm_fwd(x, col, width, w, sc, sh, name):
    t = x.shape[0]
    tm = _row_tile(t)

    def body(x_ref, w_ref, sc_ref, sh_ref, o_ref):
        xv = x_ref[...]
        r = lax.rsqrt(jnp.mean(xv * xv, axis=-1, keepdims=True) + EPS)
        n = xv * r * w_ref[...]
        o_ref[...] = (n * (1.0 + sc_ref[...]) + sh_ref[...]).astype(o_ref.dtype)

    row = pl.BlockSpec((1, width), lambda i: (0, 0))
    return pl.pallas_call(
        body, grid=(t // tm,), in_specs=[pl.BlockSpec((tm, width), lambda i: (i, col)), row, row, row],
        out_specs=pl.BlockSpec((tm, width), lambda i: (i, 0)), out_shape=SDS((t, width), BF16),
        name=name, compiler_params=_cp(("parallel",)))(x, w, sc, sh)


def _norm_bwd(x, col, width, w, sc, dh, dres, out_dtype, name):
    t = x.shape[0]
    tm = _row_tile(t)
    has_res = dres is not None

    def body(*refs):
        if has_res:
            x_ref, w_ref, sc_ref, dh_ref, dres_ref, dx_ref, dw_ref, dsc_ref, dsh_ref = refs
        else:
            x_ref, w_ref, sc_ref, dh_ref, dx_ref, dw_ref, dsc_ref, dsh_ref = refs

        @pl.when(pl.program_id(0) == 0)
        def _():
            dw_ref[...] = jnp.zeros_like(dw_ref)
            dsc_ref[...] = jnp.zeros_like(dsc_ref)
            dsh_ref[...] = jnp.zeros_like(dsh_ref)

        xv = x_ref[...]
        dhv = dh_ref[...].astype(F32)
        wv = w_ref[...]
        r = lax.rsqrt(jnp.mean(xv * xv, axis=-1, keepdims=True) + EPS)
        xh = xv * r
        n = xh * wv
        dsh_ref[...] += jnp.sum(dhv, axis=0, keepdims=True)
        dsc_ref[...] += jnp.sum(dhv * n, axis=0, keepdims=True)
        dn = dhv * (1.0 + sc_ref[...])
        dw_ref[...] += jnp.sum(dn * xh, axis=0, keepdims=True)
        dxh = dn * wv
        dx = r * (dxh - xh * jnp.mean(dxh * xh, axis=-1, keepdims=True))
        if has_res:
            dx = dx + dres_ref[...]
        dx_ref[...] = dx.astype(out_dtype)

    row = pl.BlockSpec((1, width), lambda i: (0, 0))
    blk = pl.BlockSpec((tm, width), lambda i: (i, 0))
    in_specs = [pl.BlockSpec((tm, width), lambda i: (i, col)), row, row, blk]
    args = [x, w, sc, dh]
    if has_res:
        in_specs.append(blk)
        args.append(dres)
    return pl.pallas_call(
        body, grid=(t // tm,), in_specs=in_specs, out_specs=[blk, row, row, row],
        out_shape=[SDS((t, width), out_dtype), SDS((1, width), F32), SDS((1, width), F32), SDS((1, width), F32)],
        name=name, compiler_params=_cp(("arbitrary",)))(*args)


def _rope128(x, tc, ts1, ts2):
    return x * tc + pltpu.roll(x, 96, 1) * ts1 + pltpu.roll(x, 32, 1) * ts2


def _rope128_t(d, tc, ts1, ts2):
    return d * tc + pltpu.roll(d * ts1, 32, 1) + pltpu.roll(d * ts2, 96, 1)


def _rope_q(q_raw, tc, ts1, ts2, transpose, out_dtype, name):
    t = q_raw.shape[0]
    tm = _row_tile(t)

    def body(q_ref, tc_ref, s1_ref, s2_ref, o_ref):
        fn = _rope128_t if transpose else _rope128
        for h in range(HEADS):
            base = h * D_QK
            o_ref[:, base:base + LANES] = q_ref[:, base:base + LANES].astype(out_dtype)
            x = q_ref[:, base + LANES:base + D_QK].astype(F32)
            o_ref[:, base + LANES:base + D_QK] = fn(x, tc_ref[...], s1_ref[...], s2_ref[...]).astype(out_dtype)

    blk = pl.BlockSpec((tm, HEADS * D_QK), lambda i: (i, 0))
    tab = pl.BlockSpec((tm, LANES), lambda i: (i, 0))
    return pl.pallas_call(
        body, grid=(t // tm,), in_specs=[blk, tab, tab, tab], out_specs=blk,
        out_shape=SDS((t, HEADS * D_QK), out_dtype), name=name, compiler_params=_cp(("parallel",)))(q_raw, tc, ts1, ts2)


def _k_assemble(kv_raw, p_small, tc, ts1, ts2, name):
    t = kv_raw.shape[0]
    tm = _row_tile(t)

    def body(kn_ref, ps_ref, tc_ref, s1_ref, s2_ref, o_ref):
        kpe = _rope128(ps_ref[...], tc_ref[...], s1_ref[...], s2_ref[...]).astype(BF16)
        for h in range(HEADS):
            o_ref[:, h * D_QK:h * D_QK + LANES] = kn_ref[:, h * LANES:(h + 1) * LANES].astype(BF16)
            o_ref[:, h * D_QK + LANES:(h + 1) * D_QK] = kpe

    tab = pl.BlockSpec((tm, LANES), lambda i: (i, 0))
    return pl.pallas_call(
        body, grid=(t // tm,),
        in_specs=[pl.BlockSpec((tm, HEADS * LANES), lambda i: (i, 0)), tab, tab, tab, tab],
        out_specs=pl.BlockSpec((tm, HEADS * D_QK), lambda i: (i, 0)),
        out_shape=SDS((t, HEADS * D_QK), BF16), name=name, compiler_params=_cp(("parallel",)))(kv_raw, p_small, tc, ts1, ts2)


def _k_assemble_bwd(dk, dv, tc, ts1, ts2, name):
    t = dk.shape[0]
    tm = _row_tile(t)

    def body(dk_ref, dv_ref, tc_ref, s1_ref, s2_ref, o_ref, pe_ref):
        acc = jnp.zeros((tm, LANES), F32)
        for h in range(HEADS):
            o_ref[:, h * LANES:(h + 1) * LANES] = dk_ref[:, h * D_QK:h * D_QK + LANES].astype(BF16)
            acc = acc + dk_ref[:, h * D_QK + LANES:(h + 1) * D_QK].astype(F32)
        o_ref[:, HEADS * LANES:] = dv_ref[...].astype(BF16)
        pe_ref[...] = _rope128_t(acc, tc_ref[...], s1_ref[...], s2_ref[...])

    tab = pl.BlockSpec((tm, LANES), lambda i: (i, 0))
    return pl.pallas_call(
        body, grid=(t // tm,),
        in_specs=[pl.BlockSpec((tm, HEADS * D_QK), lambda i: (i, 0)), pl.BlockSpec((tm, HEADS * LANES), lambda i: (i, 0)),
                  tab, tab, tab],
        out_specs=[pl.BlockSpec((tm, 2 * HEADS * LANES), lambda i: (i, 0)), tab],
        out_shape=[SDS((t, 2 * HEADS * LANES), BF16), SDS((t, LANES), F32)],
        name=name, compiler_params=_cp(("parallel",)))(dk, dv, tc, ts1, ts2)


def _attn_tile(t):
    return _pick(t, (256, 128, 64))


def _attn_fwd(q, k, v, v_off, name):
    t = q.shape[0]
    tq = _attn_tile(t)
    scale = (D_NOPE + D_ROPE) ** -0.5

    def body(q_ref, k_ref, v_ref, o_ref, lse_ref):
        for i in range(t // tq):
            n_k = (i + 1) * tq
            s = _dot_nt(q_ref[i * tq:(i + 1) * tq, :], k_ref[0:n_k, :]) * scale
            row = lax.broadcasted_iota(jnp.int32, (tq, n_k), 0) + i * tq
            colv = lax.broadcasted_iota(jnp.int32, (tq, n_k), 1)
            s = jnp.where(colv <= row, s, -jnp.inf)
            m = jnp.max(s, axis=-1, keepdims=True)
            p = jnp.exp(s - m)
            l = jnp.sum(p, axis=-1, keepdims=True)
            o = _dot(p, v_ref[0:n_k, :]) / l
            o_ref[i * tq:(i + 1) * tq, :] = o.astype(BF16)
            lse_ref[0, i * tq:(i + 1) * tq, :] = m + jnp.log(l)

    return pl.pallas_call(
        body, grid=(HEADS,),
        in_specs=[pl.BlockSpec((t, D_QK), lambda h: (0, h)), pl.BlockSpec((t, D_QK), lambda h: (0, h)),
                  pl.BlockSpec((t, D_V), lambda h: (0, v_off + h))],
        out_specs=[pl.BlockSpec((t, D_V), lambda h: (0, h)), pl.BlockSpec((1, t, 1), lambda h: (h, 0, 0))],
        out_shape=[SDS((t, HEADS * D_V), BF16), SDS((HEADS, t, 1), F32)],
        name=name, compiler_params=_cp(("parallel",)))(q, k, v)


def _attn_bwd(q, k, v, v_off, o, lse, do, name):
    t = q.shape[0]
    tq = _attn_tile(t)
    scale = (D_NOPE + D_ROPE) ** -0.5

    def body(q_ref, k_ref, v_ref, o_ref, lse_ref, do_ref, dq_ref, dk_ref, dv_ref):
        dk_ref[...] = jnp.zeros_like(dk_ref)
        dv_ref[...] = jnp.zeros_like(dv_ref)
        for i in range(t // tq):
            n_k = (i + 1) * tq
            rows = slice(i * tq, (i + 1) * tq)
            qi = q_ref[rows, :]
            doi = do_ref[rows, :].astype(F32)
            s = _dot_nt(qi, k_ref[0:n_k, :]) * scale
            row = lax.broadcasted_iota(jnp.int32, (tq, n_k), 0) + i * tq
            colv = lax.broadcasted_iota(jnp.int32, (tq, n_k), 1)
            p = jnp.where(colv <= row, jnp.exp(s - lse_ref[0, rows, :]), 0.0)
            dp = _dot_nt(doi, v_ref[0:n_k, :])
            delta = jnp.sum(doi * o_ref[rows, :].astype(F32), axis=-1, keepdims=True)
            ds = p * (dp - delta) * scale
            dq_ref[rows, :] = _dot(ds, k_ref[0:n_k, :])
            dk_ref[0:n_k, :] += _dot_tn(ds, qi)
            dv_ref[0:n_k, :] += _dot_tn(p, doi)

    qk_spec = pl.BlockSpec((t, D_QK), lambda h: (0, h))
    v_spec = pl.BlockSpec((t, D_V), lambda h: (0, h))
    return pl.pallas_call(
        body, grid=(HEADS,),
        in_specs=[qk_spec, qk_spec, pl.BlockSpec((t, D_V), lambda h: (0, v_off + h)), v_spec,
                  pl.BlockSpec((1, t, 1), lambda h: (h, 0, 0)), v_spec],
        out_specs=[qk_spec, qk_spec, v_spec],
        out_shape=[SDS((t, HEADS * D_QK), F32), SDS((t, HEADS * D_QK), F32), SDS((t, HEADS * D_V), F32)],
        name=name, compiler_params=_cp(("parallel",)))(q, k, v, o, lse, do)


CONV_COLS = 256


def _conv_pre(u, w_ref, rowi):
    acc = u * w_ref[CONV_WIDTH - 1:CONV_WIDTH, :]
    for sft in range(1, CONV_WIDTH):
        shifted = jnp.where(rowi >= sft, pltpu.roll(u, sft, 0), 0.0)
        acc = acc + shifted * w_ref[CONV_WIDTH - 1 - sft:CONV_WIDTH - sft, :]
    return acc


def _conv_fwd(p_main, conv_w, name):
    t = p_main.shape[0]
    off = 2 * Q_LORA // CONV_COLS

    def body(u_ref, w_ref, y_ref):
        u = u_ref[...]
        rowi = lax.broadcasted_iota(jnp.int32, u.shape, 0)
        pre = _conv_pre(u, w_ref, rowi)
        y_ref[...] = pre * _sigmoid(pre)

    return pl.pallas_call(
        body, grid=(3 * GDN_W // CONV_COLS,),
        in_specs=[pl.BlockSpec((t, CONV_COLS), lambda j: (0, off + j)), pl.BlockSpec((CONV_WIDTH, CONV_COLS), lambda j: (0, j))],
        out_specs=pl.BlockSpec((t, CONV_COLS), lambda j: (0, j)), out_shape=SDS((t, 3 * GDN_W), F32),
        name=name, compiler_params=_cp(("parallel",)))(p_main, conv_w)


def _conv_bwd(p_main, conv_w, dyc, name):
    t = p_main.shape[0]
    off = 2 * Q_LORA // CONV_COLS

    def body(u_ref, w_ref, dy_ref, du_ref, dw_ref):
        u = u_ref[...]
        rowi = lax.broadcasted_iota(jnp.int32, u.shape, 0)
        pre = _conv_pre(u, w_ref, rowi)
        sg = _sigmoid(pre)
        dpre = dy_ref[...] * sg * (1.0 + pre * (1.0 - sg))
        du = dpre * w_ref[CONV_WIDTH - 1:CONV_WIDTH, :]
        dw_ref[CONV_WIDTH - 1:CONV_WIDTH, :] = jnp.sum(dpre * u, axis=0, keepdims=True)
        for sft in range(1, CONV_WIDTH):
            back = jnp.where(rowi < t - sft, pltpu.roll(dpre, t - sft, 0), 0.0)
            du = du + back * w_ref[CONV_WIDTH - 1 - sft:CONV_WIDTH - sft, :]
            shifted = jnp.where(rowi >= sft, pltpu.roll(u, sft, 0), 0.0)
            dw_ref[CONV_WIDTH - 1 - sft:CONV_WIDTH - sft, :] = jnp.sum(dpre * shifted, axis=0, keepdims=True)
        du_ref[...] = du.astype(BF16)

    blk = pl.BlockSpec((t, CONV_COLS), lambda j: (0, j))
    wblk = pl.BlockSpec((CONV_WIDTH, CONV_COLS), lambda j: (0, j))
    return pl.pallas_call(
        body, grid=(3 * GDN_W // CONV_COLS,),
        in_specs=[pl.BlockSpec((t, CONV_COLS), lambda j: (0, off + j)), wblk, blk],
        out_specs=[blk, wblk], out_shape=[SDS((t, 3 * GDN_W), BF16), SDS((CONV_WIDTH, 3 * GDN_W), F32)],
        name=name, compiler_params=_cp(("parallel",)))(p_main, conv_w, dyc)


B_LO, A_LO, A_HI = D_ROPE, D_ROPE + HEADS, D_ROPE + 2 * HEADS


def _softplus(z):
    e = jnp.exp(-jnp.abs(z))
    log1p = jnp.where(e < 0.01, e * (1.0 - e * (0.5 - e * (1.0 / 3.0))), jnp.log(1.0 + e))
    return jnp.maximum(z, 0.0) + log1p


def _gdn_gates(p_small, a_row, dt_row, name):
    t = p_small.shape[0]

    def body(ps_ref, a_ref, dt_ref, g_ref, gc_ref):
        x = ps_ref[...]
        lane = lax.broadcasted_iota(jnp.int32, x.shape, 1)
        is_g = (lane >= A_LO) & (lane < A_HI)
        g = jnp.where(is_g, -jnp.exp(a_ref[...]) * _softplus(x + dt_ref[...]), 0.0)
        g_ref[...] = jnp.where(is_g, g, _sigmoid(x))
        pos = lax.broadcasted_iota(jnp.int32, x.shape, 0) % CHUNK
        acc = g
        sft = 1
        while sft < CHUNK:
            acc = acc + jnp.where(pos >= sft, pltpu.roll(acc, sft, 0), 0.0)
            sft *= 2
        gc_ref[...] = acc

    full = pl.BlockSpec((t, LANES), lambda i: (0, 0))
    row = pl.BlockSpec((1, LANES), lambda i: (0, 0))
    return pl.pallas_call(
        body, grid=(1,), in_specs=[full, row, row], out_specs=[full, full],
        out_shape=[SDS((t, LANES), F32), SDS((t, LANES), F32)], name=name,
        compiler_params=_cp(("arbitrary",)))(p_small, a_row, dt_row)


def _gdn_gates_bwd(p_small, a_row, dt_row, gates, dgates, dkpe, name):
    t = p_small.shape[0]

    def body(ps_ref, a_ref, dt_ref, g_ref, db_ref, dkpe_ref, dp_ref, da_ref, ddt_ref):
        x = ps_ref[...]
        lane = lax.broadcasted_iota(jnp.int32, x.shape, 1)
        is_g = (lane >= A_LO) & (lane < A_HI)
        is_b = (lane >= B_LO) & (lane < A_LO)
        pos = lax.broadcasted_iota(jnp.int32, x.shape, 0) % CHUNK
        acc = jnp.where(is_g, db_ref[...], 0.0)
        sft = 1
        while sft < CHUNK:
            acc = acc + jnp.where(pos < CHUNK - sft, pltpu.roll(acc, t - sft, 0), 0.0)
            sft *= 2
        dg = acc
        gv = g_ref[...]
        dz = jnp.where(is_g, dg * (-jnp.exp(a_ref[...])) * _sigmoid(x + dt_ref[...]), 0.0)
        da_ref[...] = jnp.sum(jnp.where(is_g, dg * gv, 0.0), axis=0, keepdims=True)
        ddt_ref[...] = jnp.sum(dz, axis=0, keepdims=True)
        dlb = jnp.where(is_b, db_ref[...] * gv * (1.0 - gv), 0.0)
        dp_ref[...] = (jnp.where(lane < D_ROPE, dkpe_ref[...], 0.0) + dlb + dz).astype(BF16)

    full = pl.BlockSpec((t, LANES), lambda i: (0, 0))
    row = pl.BlockSpec((1, LANES), lambda i: (0, 0))
    return pl.pallas_call(
        body, grid=(1,), in_specs=[full, row, row, full, full, full], out_specs=[full, row, row],
        out_shape=[SDS((t, LANES), BF16), SDS((1, LANES), F32), SDS((1, LANES), F32)], name=name,
        compiler_params=_cp(("arbitrary",)))(p_small, a_row, dt_row, gates, dgates, dkpe)


def _tri_inv(l, eye):
    x = eye - l
    p = _bdot(l, l, exact=True)
    steps = int(math.log2(CHUNK)) - 1
    for s in range(steps):
        x = x + _bdot(x, p, exact=True)
        if s < steps - 1:
            p = _bdot(p, p, exact=True)
    return x


def _l2n(x3):
    r = lax.rsqrt(jnp.sum(x3 * x3, axis=-1, keepdims=True) + EPS)
    return x3 * r, r


def _head_col(a_ref, lane_lo, n):
    a = a_ref[...]
    lane = lax.broadcasted_iota(jnp.int32, a.shape, 1)
    col = jnp.sum(jnp.where(lane == lane_lo + pl.program_id(0), a, 0.0), axis=-1, keepdims=True)
    return col.reshape(n, CHUNK, 1)


def _gdn_common(q3, k3, v3, b, gc):
    n = q3.shape[0]
    ri = lax.broadcasted_iota(jnp.int32, (n, CHUNK, CHUNK), 1)
    ci = lax.broadcasted_iota(jnp.int32, (n, CHUNK, CHUNK), 2)
    lower, strict = ri >= ci, ri > ci
    eye = (ri == ci).astype(F32)
    gr = jnp.sum(gc * eye, axis=1, keepdims=True)
    qh, rq = _l2n(q3)
    qn = qh * (D_V ** -0.5)
    kn, rk = _l2n(k3)
    dec = jnp.where(lower, jnp.exp(jnp.where(lower, gc - gr, 0.0)), 0.0)
    kb = kn * b
    mm = _bdot_nt(kb, kn)
    tinv = _tri_inv(jnp.where(strict, mm * dec, 0.0), eye)
    gam = jnp.exp(gc)
    u = _bdot(tinv, v3 * b, exact=True)
    w = _bdot(tinv, kb * gam, exact=True)
    qk = _bdot_nt(qn, kn)
    aqk = jnp.where(lower, qk * dec, 0.0)
    gl = gc[:, CHUNK - 1:CHUNK, :]
    kdf = jnp.exp(gl - gc)
    return dict(ri=ri, ci=ci, lower=lower, strict=strict, eye=eye, qh=qh, rq=rq, qn=qn, kn=kn, rk=rk, dec=dec, kb=kb,
                mm=mm, gam=gam, u=u, w=w, qk=qk, aqk=aqk, gl=gl, kdf=kdf, kd=kn * kdf, gr=gr)


def _gdn_fwd(yc, p_main, gates, gcum, gn, name):
    t = yc.shape[0]
    n = t // CHUNK
    z_off = (2 * Q_LORA + 3 * GDN_W) // D_V

    def body(q_ref, k_ref, v_ref, z_ref, gt_ref, gcum_ref, gn_ref, o_ref, g_ref, s_ref, u_s, w_s, qg_s, kd_s, a_s, e_s):
        c = _gdn_common(q_ref[...].reshape(n, CHUNK, D_V), k_ref[...].reshape(n, CHUNK, D_V),
                        v_ref[...].reshape(n, CHUNK, D_V), _head_col(gt_ref, B_LO, n), _head_col(gcum_ref, A_LO, n))
        u_s[...] = c["u"]
        w_s[...] = c["w"]
        qg_s[...] = c["qn"] * c["gam"]
        kd_s[...] = c["kd"]
        a_s[...] = c["aqk"]
        e_s[...] = jnp.broadcast_to(jnp.exp(c["gl"]), (n, 1, D_V))

        def step(i, s):
            s_ref[0, i] = s
            v_new = u_s[i] - _dot(w_s[i], s)
            o = _dot(qg_s[i], s) + _dot(a_s[i], v_new)
            o_ref[pl.ds(pl.multiple_of(i * CHUNK, CHUNK), CHUNK), :] = o
            return s * e_s[i] + _dot_tn(kd_s[i], v_new)

        lax.fori_loop(0, n, step, jnp.zeros((D_V, D_V), F32))
        o = o_ref[...]
        zz = z_ref[...]
        on = o * lax.rsqrt(jnp.mean(o * o, axis=-1, keepdims=True) + EPS) * gn_ref[...]
        g_ref[...] = (on * zz * _sigmoid(zz)).astype(BF16)

    col = lambda off: pl.BlockSpec((t, D_V), lambda h: (0, off + h))
    lanes = pl.BlockSpec((t, LANES), lambda h: (0, 0))
    big = pltpu.VMEM((n, CHUNK, D_V), F32)
    return pl.pallas_call(
        body, grid=(HEADS,),
        in_specs=[col(0), col(HEADS), col(2 * HEADS), col(z_off), lanes, lanes, pl.BlockSpec((1, D_V), lambda h: (0, 0))],
        out_specs=[col(0), col(0), pl.BlockSpec((1, n, D_V, D_V), lambda h: (h, 0, 0, 0))],
        out_shape=[SDS((t, GDN_W), F32), SDS((t, GDN_W), BF16), SDS((HEADS, n, D_V, D_V), F32)],
        scratch_shapes=[big, big, big, big, pltpu.VMEM((n, CHUNK, CHUNK), F32), pltpu.VMEM((n, 1, D_V), F32)],
        name=name, compiler_params=_cp(("parallel",)))(yc, yc, yc, p_main, gates, gcum, gn)


def _gdn_bwd(yc, p_main, gates, gcum, gn, o_raw, states, dgated, name):
    t = yc.shape[0]
    n = t // CHUNK
    z_off = (2 * Q_LORA + 3 * GDN_W) // D_V

    def body(q_ref, k_ref, v_ref, z_ref, gt_ref, gcum_ref, gn_ref, o_ref, s_ref, dg_ref,
             dq_ref, dk_ref, dv_ref, dz_ref, dgt_ref, dgn_ref,
             u_s, w_s, qg_s, kd_s, at_s, e_s, do_s, du_s, dw_s, dqg_s, dkd_s, da_s, dat_s, dgs_s):
        @pl.when(pl.program_id(0) == 0)
        def _():
            dgn_ref[...] = jnp.zeros_like(dgn_ref)
            dgt_ref[...] = jnp.zeros_like(dgt_ref)

        o = o_ref[...]
        zz = z_ref[...]
        dgv = dg_ref[...]
        gnv = gn_ref[...]
        r = lax.rsqrt(jnp.mean(o * o, axis=-1, keepdims=True) + EPS)
        oh = o * r
        sg = _sigmoid(zz)
        don = dgv * zz * sg
        dz_ref[...] = (dgv * oh * gnv * sg * (1.0 + zz * (1.0 - sg))).astype(BF16)
        dgn_ref[...] += jnp.sum(don * oh, axis=0, keepdims=True)
        doh = don * gnv
        do_s[...] = (r * (doh - oh * jnp.mean(doh * oh, axis=-1, keepdims=True))).reshape(n, CHUNK, D_V)

        q3 = q_ref[...].reshape(n, CHUNK, D_V)
        k3 = k_ref[...].reshape(n, CHUNK, D_V)
        v3 = v_ref[...].reshape(n, CHUNK, D_V)
        b, gc = _head_col(gt_ref, B_LO, n), _head_col(gcum_ref, A_LO, n)
        c = _gdn_common(q3, k3, v3, b, gc)
        gr = c["gr"]
        ri, ci = c["ri"], c["ci"]
        upper, sup = ci >= ri, ci > ri
        dect = jnp.where(upper, jnp.exp(jnp.where(upper, gr - gc, 0.0)), 0.0)
        tinv_t = _tri_inv(jnp.where(sup, _bdot_nt(c["kn"], c["kb"]) * dect, 0.0), c["eye"])
        u_s[...] = c["u"]
        w_s[...] = c["w"]
        qg_s[...] = c["qn"] * c["gam"]
        kd_s[...] = c["kd"]
        at_s[...] = jnp.where(upper, _bdot_nt(c["kn"], c["qn"]) * dect, 0.0)
        e_s[...] = jnp.broadcast_to(jnp.exp(c["gl"]), (n, 1, D_V))

        def step(j, ds):
            i = n - 1 - j
            s = s_ref[0, i]
            do_i = do_s[i]
            v_new = u_s[i] - _dot(w_s[i], s)
            dvn = _dot(at_s[i], do_i) + _dot(kd_s[i], ds)
            da_s[i] = _dot_nt(do_i, v_new)
            dat_s[i] = _dot_nt(v_new, do_i)
            dqg_s[i] = _dot_nt(do_i, s)
            dw_s[i] = -_dot_nt(dvn, s)
            dkd_s[i] = _dot_nt(v_new, ds)
            du_s[i] = dvn
            dgs_s[i] = jnp.broadcast_to(jnp.sum(jnp.sum(s * ds, axis=1, keepdims=True), axis=0, keepdims=True), (1, D_V))
            return _dot_tn(qg_s[i], do_i) + e_s[i] * ds - _dot_tn(w_s[i], dvn)

        lax.fori_loop(0, n, step, jnp.zeros((D_V, D_V), F32))

        du, dw, dqg, dkd = du_s[...], dw_s[...], dqg_s[...], dkd_s[...]
        lower, strict, dec = c["lower"], c["strict"], c["dec"]
        kn, kb, qn, gam, kdf = c["kn"], c["kb"], c["qn"], c["gam"], c["kdf"]
        drv = _bdot(tinv_t, du, exact=True)
        drk = _bdot(tinv_t, dw, exact=True)
        dl = jnp.where(strict, -(_bdot_nt(drv, c["u"]) + _bdot_nt(drk, c["w"])), 0.0)
        dlt = jnp.where(sup, -(_bdot_nt(c["u"], drv) + _bdot_nt(c["w"], drk)), 0.0)
        da = jnp.where(lower, da_s[...], 0.0)
        dat = jnp.where(upper, dat_s[...], 0.0)
        e = (dl * c["mm"] + da * c["qk"]) * dec
        col_sums = jnp.sum(e, axis=1, keepdims=True)
        dgc = jnp.sum(e, axis=2, keepdims=True) - jnp.sum(col_sums * c["eye"], axis=2, keepdims=True)
        dkb = _bdot(dl * dec, kn) + gam * drk
        dkn = _bdot(dlt * dect, kb) + _bdot(dat * dect, qn) + b * dkb + dkd * kdf
        dqn = _bdot(da * dec, kn) + gam * dqg
        dgam = jnp.sum(drk * kb, axis=-1, keepdims=True) + jnp.sum(dqg * qn, axis=-1, keepdims=True)
        dbeta = jnp.sum(dkb * kn, axis=-1, keepdims=True) + jnp.sum(drv * v3, axis=-1, keepdims=True)
        dv_ref[...] = (b * drv).reshape(t, D_V)
        ee = jnp.sum(dkd * kn, axis=-1, keepdims=True) * kdf
        dgc = dgc + dgam * gam - ee
        rowc = lax.broadcasted_iota(jnp.int32, (n, CHUNK, 1), 1)
        tail = jnp.sum(ee, axis=1, keepdims=True) + dgs_s[...][:, :, 0:1] * jnp.exp(c["gl"])
        dgc = dgc + jnp.where(rowc == CHUNK - 1, tail, 0.0)
        lane = lax.broadcasted_iota(jnp.int32, (t, LANES), 1)
        head = pl.program_id(0)
        dgt_ref[...] += (jnp.where(lane == B_LO + head, dbeta.reshape(t, 1), 0.0)
                         + jnp.where(lane == A_LO + head, dgc.reshape(t, 1), 0.0))
        sc = D_V ** -0.5
        qh, rq, rk = c["qh"], c["rq"], c["rk"]
        dq_ref[...] = (rq * (sc * dqn - qh * jnp.sum(sc * dqn * qh, axis=-1, keepdims=True))).reshape(t, D_V)
        dk_ref[...] = (rk * (dkn - kn * jnp.sum(dkn * kn, axis=-1, keepdims=True))).reshape(t, D_V)

    once = pl.Buffered(1)
    col = lambda off: pl.BlockSpec((t, D_V), lambda h: (0, off + h), pipeline_mode=once)
    out_col = pl.BlockSpec((t, D_V), lambda h: (0, h))
    lanes = pl.BlockSpec((t, LANES), lambda h: (0, 0))
    row = pl.BlockSpec((1, D_V), lambda h: (0, 0))
    big = pltpu.VMEM((n, CHUNK, D_V), F32)
    sq = pltpu.VMEM((n, CHUNK, CHUNK), F32)
    small = pltpu.VMEM((n, 1, D_V), F32)
    return pl.pallas_call(
        body, grid=(HEADS,),
        in_specs=[col(0), col(HEADS), col(2 * HEADS), col(z_off), lanes, lanes, row, col(0),
                  pl.BlockSpec((1, n, D_V, D_V), lambda h: (h, 0, 0, 0), pipeline_mode=once), col(0)],
        out_specs=[out_col, out_col, out_col, out_col, lanes, row],
        out_shape=[SDS((t, GDN_W), F32), SDS((t, GDN_W), F32), SDS((t, GDN_W), F32), SDS((t, GDN_W), BF16),
                   SDS((t, LANES), F32), SDS((1, D_V), F32)],
        scratch_shapes=[big, big, big, big, sq, small, big, big, big, big, big, sq, sq, small],
        name=name, compiler_params=_cp(("arbitrary",)))(yc, yc, yc, p_main, gates, gcum, gn, o_raw, states, dgated)


def _col_tile(d):
    return _pick(d, (512, 256, 128))


def _mix_fwd(y_a, y_b, p_main, name):
    t, d = y_a.shape
    tm, cw = _row_tile(t), _col_tile(d)
    off_a, off_b = MAIN_FIXED // cw, (MAIN_FIXED + d) // cw

    def body(ya_ref, yb_ref, ga_ref, gb_ref, u_ref):
        u_ref[...] = (_sigmoid(ga_ref[...]) * ya_ref[...] + _sigmoid(gb_ref[...]) * yb_ref[...]).astype(BF16)

    blk = pl.BlockSpec((tm, cw), lambda i, j: (i, j))
    return pl.pallas_call(
        body, grid=(t // tm, d // cw),
        in_specs=[blk, blk, pl.BlockSpec((tm, cw), lambda i, j: (i, off_a + j)), pl.BlockSpec((tm, cw), lambda i, j: (i, off_b + j))],
        out_specs=blk, out_shape=SDS((t, d), BF16), name=name,
        compiler_params=_cp(("parallel", "parallel")))(y_a, y_b, p_main, p_main)


def _mix_bwd(du, y_a, y_b, p_main, name):
    t, d = y_a.shape
    tm, cw = _row_tile(t), _col_tile(d)
    off_a, off_b = MAIN_FIXED // cw, (MAIN_FIXED + d) // cw
    nb = d // cw

    def body(du_ref, ya_ref, yb_ref, ga_ref, gb_ref, dya_ref, dyb_ref, dla_ref, dlb_ref):
        duv = du_ref[...]
        ga, gb = _sigmoid(ga_ref[...]), _sigmoid(gb_ref[...])
        dya_ref[...] = (duv * ga).astype(BF16)
        dyb_ref[...] = (duv * gb).astype(BF16)
        dla_ref[...] = (duv * ya_ref[...] * ga * (1.0 - ga)).astype(BF16)
        dlb_ref[...] = (duv * yb_ref[...] * gb * (1.0 - gb)).astype(BF16)

    blk = pl.BlockSpec((tm, cw), lambda i, j: (i, j))
    outs = pl.pallas_call(
        body, grid=(t // tm, nb),
        in_specs=[blk, blk, blk, pl.BlockSpec((tm, cw), lambda i, j: (i, off_a + j)),
                  pl.BlockSpec((tm, cw), lambda i, j: (i, off_b + j))],
        out_specs=[blk, blk, blk, blk],
        out_shape=[SDS((t, d), BF16), SDS((t, d), BF16), SDS((t, d), BF16), SDS((t, d), BF16)], name=name,
        compiler_params=_cp(("parallel", "parallel")))(du, y_a, y_b, p_main, p_main)
    return outs


def _gate_res(x, y, gt, name):
    t, d = x.shape
    tm = _row_tile(t)

    def body(x_ref, y_ref, g_ref, o_ref):
        o_ref[...] = x_ref[...] + g_ref[...] * y_ref[...]

    blk = pl.BlockSpec((tm, d), lambda i: (i, 0))
    return pl.pallas_call(
        body, grid=(t // tm,), in_specs=[blk, blk, pl.BlockSpec((1, d), lambda i: (0, 0))], out_specs=blk,
        out_shape=SDS((t, d), F32), name=name, compiler_params=_cp(("parallel",)))(x, y, gt)


def _gate_res_bwd(dx, y, gt, name):
    t, d = dx.shape
    tm = _row_tile(t)

    def body(dx_ref, y_ref, g_ref, dg_ref, dy_ref):
        @pl.when(pl.program_id(0) == 0)
        def _():
            dg_ref[...] = jnp.zeros_like(dg_ref)

        dxv = dx_ref[...]
        dg_ref[...] += jnp.sum(dxv * y_ref[...], axis=0, keepdims=True)
        dy_ref[...] = (dxv * g_ref[...]).astype(BF16)

    blk = pl.BlockSpec((tm, d), lambda i: (i, 0))
    row = pl.BlockSpec((1, d), lambda i: (0, 0))
    return pl.pallas_call(
        body, grid=(t // tm,), in_specs=[blk, blk, row], out_specs=[row, blk],
        out_shape=[SDS((1, d), F32), SDS((t, d), BF16)], name=name, compiler_params=_cp(("arbitrary",)))(dx, y, gt)


def _swiglu_fwd(gu, name):
    t, f2 = gu.shape
    f = f2 // 2
    tm, cw = _row_tile(t), _col_tile(f)
    nb = f // cw

    def body(g_ref, u_ref, o_ref):
        g = g_ref[...]
        o_ref[...] = (g * _sigmoid(g) * u_ref[...]).astype(BF16)

    return pl.pallas_call(
        body, grid=(t // tm, nb),
        in_specs=[pl.BlockSpec((tm, cw), lambda i, j: (i, j)), pl.BlockSpec((tm, cw), lambda i, j: (i, nb + j))],
        out_specs=pl.BlockSpec((tm, cw), lambda i, j: (i, j)), out_shape=SDS((t, f), BF16), name=name,
        compiler_params=_cp(("parallel", "parallel")))(gu, gu)


def _swiglu_bwd(gu, da, name):
    t, f2 = gu.shape
    f = f2 // 2
    tm, cw = _row_tile(t), _col_tile(f)
    nb = f // cw

    def body(g_ref, u_ref, da_ref, dg_ref, dup_ref):
        g = g_ref[...]
        dav = da_ref[...]
        sg = _sigmoid(g)
        dg_ref[...] = (dav * u_ref[...] * sg * (1.0 + g * (1.0 - sg))).astype(BF16)
        dup_ref[...] = (dav * g * sg).astype(BF16)

    blk = pl.BlockSpec((tm, cw), lambda i, j: (i, j))
    dg, dup = pl.pallas_call(
        body, grid=(t // tm, nb),
        in_specs=[blk, pl.BlockSpec((tm, cw), lambda i, j: (i, nb + j)), blk], out_specs=[blk, blk],
        out_shape=[SDS((t, f), BF16), SDS((t, f), BF16)], name=name,
        compiler_params=_cp(("parallel", "parallel")))(gu, gu, da)
    return dg, dup


def _loss_head(x, w, target, name):
    t, d = x.shape
    tm = _row_tile(t)

    def body(x_ref, w_ref, t_ref, l_ref, dx_ref, dw_ref):
        @pl.when(pl.program_id(0) == 0)
        def _():
            l_ref[...] = jnp.zeros_like(l_ref)
            dw_ref[...] = jnp.zeros_like(dw_ref)

        xv = x_ref[...]
        wv = w_ref[...]
        r = lax.rsqrt(jnp.mean(xv * xv, axis=-1, keepdims=True) + EPS)
        xh = xv * r
        err = xh * wv - t_ref[...]
        per_tok = jnp.mean(err * err, axis=-1, keepdims=True)
        l_ref[...] += 0.5 * jnp.sum(per_tok, axis=0, keepdims=True)
        dy = err * (1.0 / d)
        dw_ref[...] += jnp.sum(dy * xh, axis=0, keepdims=True)
        dxh = dy * wv
        dx_ref[...] = r * (dxh - xh * jnp.mean(dxh * xh, axis=-1, keepdims=True))

    blk = pl.BlockSpec((tm, d), lambda i: (i, 0))
    row = pl.BlockSpec((1, d), lambda i: (0, 0))
    return pl.pallas_call(
        body, grid=(t // tm,), in_specs=[blk, row, blk],
        out_specs=[pl.BlockSpec((1, LANES), lambda i: (0, 0)), blk, row],
        out_shape=[SDS((1, LANES), F32), SDS((t, d), F32), SDS((1, d), F32)], name=name,
        compiler_params=_cp(("arbitrary",)))(x, w, target)


def _adamw(w, g, m, v, name):
    shape = w.shape
    cols = shape[-1]
    rows = w.size // cols
    w2, g2, m2, v2 = (a.reshape(rows, cols) for a in (w, g, m, v))
    lanes_padded = -(-cols // LANES) * LANES
    budget_rows = max(8, (20 * 1024 * 1024) // (lanes_padded * 4 * 16))
    tr = rows
    if rows > budget_rows:
        tr = _pick(rows, tuple(c for c in (1024, 512, 256, 128, 64, 32, 16, 8) if c <= budget_rows))
    c1 = 1.0 / (1.0 - ADAM_B1 ** ADAM_STEP)
    c2 = 1.0 / (1.0 - ADAM_B2 ** ADAM_STEP)

    def body(w_ref, g_ref, m_ref, v_ref, d_ref, mo_ref, vo_ref):
        gv = g_ref[...]
        mn = ADAM_B1 * m_ref[...] + (1.0 - ADAM_B1) * gv
        vn = ADAM_B2 * v_ref[...] + (1.0 - ADAM_B2) * (gv * gv)
        mo_ref[...] = mn
        vo_ref[...] = vn
        d_ref[...] = -ADAM_LR * ((mn * c1) / (jnp.sqrt(vn * c2) + ADAM_EPS) + ADAM_WD * w_ref[...])

    blk = pl.BlockSpec((tr, cols), lambda i: (i, 0))
    outs = pl.pallas_call(
        body, grid=(rows // tr,), in_specs=[blk, blk, blk, blk], out_specs=[blk, blk, blk],
        out_shape=[SDS((rows, cols), F32)] * 3, name=name, compiler_params=_cp(("parallel",)))(w2, g2, m2, v2)
    return tuple(o.reshape(shape) for o in outs)


KPE_LO = 2 * Q_LORA
QKVZ_LO = KPE_LO + D_ROPE
BA_LO = QKVZ_LO + 4 * GDN_W
GATE_LO = BA_LO + 2 * HEADS


def _lay_w_in(w_in):
    d = w_in.shape[0]
    main = jnp.concatenate([w_in[:, :KPE_LO], w_in[:, QKVZ_LO:BA_LO], w_in[:, GATE_LO:]], axis=1)
    small = jnp.concatenate([w_in[:, KPE_LO:QKVZ_LO], w_in[:, BA_LO:GATE_LO],
                             jnp.zeros((d, LANES - D_ROPE - 2 * HEADS), w_in.dtype)], axis=1)
    return main, small


def _unlay_w_in(g_main, g_small):
    return jnp.concatenate([g_main[:, :KPE_LO], g_small[:, :D_ROPE], g_main[:, KPE_LO:KPE_LO + 4 * GDN_W],
                            g_small[:, D_ROPE:D_ROPE + 2 * HEADS], g_main[:, MAIN_FIXED:]], axis=1)


def _lay_w_uq(w_uq):
    r = w_uq.reshape(Q_LORA, HEADS, D_NOPE + D_ROPE)
    r = jnp.pad(r, ((0, 0), (0, 0), (0, D_QK - D_NOPE - D_ROPE)))
    return r.reshape(Q_LORA, HEADS * D_QK)


def _unlay_w_uq(g):
    rows = g.shape[0]
    return g.reshape(rows, HEADS, D_QK)[:, :, :D_NOPE + D_ROPE].reshape(rows, HEADS * (D_NOPE + D_ROPE))


def _lay_w_ukv(w_ukv):
    return w_ukv.reshape(KV_LORA, HEADS, 2, D_V).transpose(0, 2, 1, 3).reshape(KV_LORA, 2 * HEADS * D_V)


def _unlay_w_ukv(g):
    rows = g.shape[0]
    return g.reshape(rows, 2, HEADS, D_V).transpose(0, 2, 1, 3).reshape(rows, 2 * HEADS * D_V)


def _lane_row(vec, lo):
    return jnp.pad(vec.reshape(1, -1), ((0, 0), (lo, LANES - lo - vec.shape[0])))


def _rope_tables(positions):
    half = D_ROPE // 2
    inv_freq = 1.0 / (10000.0 ** (jnp.arange(0, D_ROPE, 2, dtype=F32) / D_ROPE))
    ang = positions.astype(F32)[:, None] * inv_freq
    cos, sin = jnp.cos(ang), jnp.sin(ang)
    t = positions.shape[0]
    zeros = lambda n: jnp.zeros((t, n), F32)
    tc = jnp.concatenate([cos, cos, zeros(LANES - D_ROPE)], axis=1)
    ts1 = jnp.concatenate([-sin, zeros(LANES - half)], axis=1)
    ts2 = jnp.concatenate([zeros(half), sin, zeros(LANES - D_ROPE)], axis=1)
    return tc, ts1, ts2


def _layer_fwd(x, mod, wt, tabs, tag, late_weights):
    t, d = x.shape
    sh_a, sc_a, gt_a, sh_f, sc_f, gt_f = mod
    zero_l = jnp.zeros((1, Q_LORA), F32)
    s = dict(x=x)
    s["h1"] = _norm_fwd(x, 0, d, wt["norm_mix"], sc_a, sh_a, f"{tag}_norm_mix")
    s["p_main"] = _mm(s["h1"], wt["w_main"], name=f"{tag}_in_main")
    s["p_small"] = _mm(s["h1"], wt["w_small"], name=f"{tag}_in_small")
    s["cqn"] = _norm_fwd(s["p_main"], 0, Q_LORA, wt["q_a_norm"], zero_l, zero_l, f"{tag}_q_norm")
    s["ckvn"] = _norm_fwd(s["p_main"], 1, KV_LORA, wt["kv_a_norm"], zero_l, zero_l, f"{tag}_kv_norm")
    q_raw = _mm(s["cqn"], wt["w_uq"], name=f"{tag}_uq")
    s["kv_raw"] = _mm(s["ckvn"], wt["w_ukv"], name=f"{tag}_ukv")
    s["q_r"] = _rope_q(q_raw, *tabs, False, BF16, f"{tag}_rope_q")
    s["k_r"] = _k_assemble(s["kv_raw"], s["p_small"], *tabs, f"{tag}_k_asm")
    s["o"], s["lse"] = _attn_fwd(s["q_r"], s["k_r"], s["kv_raw"], HEADS, f"{tag}_attn")
    s["yc"] = _conv_fwd(s["p_main"], wt["conv_w"], f"{tag}_conv")
    s["gates"], s["gcum"] = _gdn_gates(s["p_small"], wt["a_row"], wt["dt_row"], f"{tag}_gates")
    s["o_raw"], s["gated"], s["states"] = _gdn_fwd(s["yc"], s["p_main"], s["gates"], s["gcum"], wt["gdn_norm"], f"{tag}_gdn")
    wt = {**wt, **late_weights(s["gated"])}
    s["y_a"] = _mm(s["o"], wt["w_o_mla"], name=f"{tag}_o_mla")
    s["y_b"] = _mm(s["gated"], wt["w_o_gdn"], name=f"{tag}_o_gdn")
    s["u"] = _mix_fwd(s["y_a"], s["y_b"], s["p_main"], f"{tag}_mix")
    s["mixo"] = _mm(s["u"], wt["w_o"], name=f"{tag}_o")
    s["x2"] = _gate_res(x, s["mixo"], gt_a, f"{tag}_res_a")
    s["h2"] = _norm_fwd(s["x2"], 0, d, wt["norm_ffn"], sc_f, sh_f, f"{tag}_norm_ffn")
    s["gu"] = _mm(s["h2"], wt["w_gate_up"], name=f"{tag}_gate_up")
    s["a"] = _swiglu_fwd(s["gu"], f"{tag}_swiglu")
    s["f"] = _mm(s["a"], wt["w_down"], name=f"{tag}_down")
    return _gate_res(s["x2"], s["f"], gt_f, f"{tag}_res_f"), s, wt


def _layer_bwd(dx3, s, mod, wt, tabs, tag, after_ffn=None, after_gdn=None):
    x = s["x"]
    t, d = x.shape
    sh_a, sc_a, gt_a, sh_f, sc_f, gt_f = mod
    zero_l = jnp.zeros((1, Q_LORA), F32)
    g = {}
    dgt_f, df = _gate_res_bwd(dx3, s["f"], gt_f, f"{tag}_b_res_f")
    da = _mm(df, wt["w_down"], tb=True, name=f"{tag}_b_down_x")
    g["w_down"] = _mm(s["a"].T, df, out_dtype=BF16, name=f"{tag}_b_down_w")
    dgate, dup = _swiglu_bwd(s["gu"], da, f"{tag}_b_swiglu")
    dgu = jnp.concatenate([dgate, dup], axis=1)
    dh2 = _mm(dgu, wt["w_gate_up"], tb=True, name=f"{tag}_b_gate_up_x")
    g["w_gate_up"] = _mm(s["h2"].T, dgu, out_dtype=BF16, name=f"{tag}_b_gate_up_w")
    if after_ffn is not None:
        gt_a = gt_a + after_ffn(g)
    dx2, g["norm_ffn"], dsc_f, dsh_f = _norm_bwd(s["x2"], 0, d, wt["norm_ffn"], sc_f, dh2, dx3, F32, f"{tag}_b_norm_ffn")
    dgt_a, dmixo = _gate_res_bwd(dx2, s["mixo"], gt_a, f"{tag}_b_res_a")
    du = _mm(dmixo, wt["w_o"], tb=True, name=f"{tag}_b_o_x")
    g["w_o"] = _mm(s["u"].T, dmixo, out_dtype=BF16, name=f"{tag}_b_o_w")
    dy_a, dy_b, dl_a, dl_b = _mix_bwd(du, s["y_a"], s["y_b"], s["p_main"], f"{tag}_b_mix")
    dgated = _mm(dy_b, wt["w_o_gdn"], tb=True, name=f"{tag}_b_o_gdn_x")
    g["w_o_gdn"] = _mm(s["gated"].T, dy_b, out_dtype=BF16, name=f"{tag}_b_o_gdn_w")
    dq_c, dk_c, dv_c, dz, dgates, g["gdn_norm"] = _gdn_bwd(
        s["yc"], s["p_main"], s["gates"], s["gcum"], wt["gdn_norm"], s["o_raw"], s["states"], dgated, f"{tag}_b_gdn")
    du_conv, g["conv_w"] = _conv_bwd(s["p_main"], wt["conv_w"], jnp.concatenate([dq_c, dk_c, dv_c], axis=1), f"{tag}_b_conv")
    do = _mm(dy_a, wt["w_o_mla"], tb=True, name=f"{tag}_b_o_mla_x")
    g["w_o_mla"] = _mm(s["o"].T, dy_a, out_dtype=BF16, name=f"{tag}_b_o_mla_w")
    dq_r, dk_r, dv = _attn_bwd(s["q_r"], s["k_r"], s["kv_raw"], HEADS, s["o"], s["lse"], do, f"{tag}_b_attn")
    q_a_norm = wt["q_a_norm"]
    if after_gdn is not None:
        q_a_norm = q_a_norm + after_gdn(du_conv)
    dq_raw = _rope_q(dq_r, *tabs, True, BF16, f"{tag}_b_rope_q")
    dkv_raw, dkpe = _k_assemble_bwd(dk_r, dv, *tabs, f"{tag}_b_k_asm")
    dcqn = _mm(dq_raw, wt["w_uq"], tb=True, name=f"{tag}_b_uq_x")
    g["w_uq"] = _mm(s["cqn"].T, dq_raw, out_dtype=BF16, name=f"{tag}_b_uq_w")
    dckvn = _mm(dkv_raw, wt["w_ukv"], tb=True, name=f"{tag}_b_ukv_x")
    g["w_ukv"] = _mm(s["ckvn"].T, dkv_raw, out_dtype=BF16, name=f"{tag}_b_ukv_w")
    dc_q, g["q_a_norm"], _, _ = _norm_bwd(s["p_main"], 0, Q_LORA, q_a_norm, zero_l, dcqn, None, BF16, f"{tag}_b_q_norm")
    dc_kv, g["kv_a_norm"], _, _ = _norm_bwd(s["p_main"], 1, KV_LORA, wt["kv_a_norm"], zero_l, dckvn, None, BF16,
                                            f"{tag}_b_kv_norm")
    dp_small, g["a_row"], g["dt_row"] = _gdn_gates_bwd(
        s["p_small"], wt["a_row"], wt["dt_row"], s["gates"], dgates, dkpe, f"{tag}_b_gates")
    dp_main = jnp.concatenate([dc_q, dc_kv, du_conv, dz, dl_a, dl_b], axis=1)
    h1t = s["h1"].T
    dh1 = _mm(dp_small, wt["w_small"], tb=True, name=f"{tag}_b_in_small_x")
    dh1 = _mm(dp_main, wt["w_main"], tb=True, acc_in=dh1, name=f"{tag}_b_in_main_x")
    g["w_main"] = _mm(h1t, dp_main, out_dtype=BF16, name=f"{tag}_b_in_main_w")
    g["w_small"] = _mm(h1t, dp_small, out_dtype=BF16, name=f"{tag}_b_in_small_w")
    dx, g["norm_mix"], dsc_a, dsh_a = _norm_bwd(x, 0, d, wt["norm_mix"], sc_a, dh1, dx2, F32, f"{tag}_b_norm_mix")
    return dx, (dsh_a, dsc_a, dgt_a, dsh_f, dsc_f, dgt_f), g


def _layer_weights(big, full, l):
    w_main, w_small = _lay_w_in(big["w_in"])
    return dict(
        w_main=w_main, w_small=w_small, w_uq=_lay_w_uq(big["w_uq"]), w_ukv=_lay_w_ukv(big["w_ukv"]),
        conv_w=full["conv_w"][l],
        norm_mix=full["norm_mix"][l][None], norm_ffn=full["norm_ffn"][l][None],
        q_a_norm=full["q_a_norm"][l][None], kv_a_norm=full["kv_a_norm"][l][None], gdn_norm=full["gdn_norm"][l][None],
        a_row=_lane_row(full["A_log"][l], A_LO), dt_row=_lane_row(full["dt_bias"][l], A_LO))


def _small_grads_ref_layout(g):
    return dict(
        conv_w=g["conv_w"], norm_mix=g["norm_mix"][0], norm_ffn=g["norm_ffn"][0], q_a_norm=g["q_a_norm"][0],
        kv_a_norm=g["kv_a_norm"][0], gdn_norm=g["gdn_norm"][0], A_log=g["a_row"][0, A_LO:A_HI],
        dt_bias=g["dt_row"][0, A_LO:A_HI])


def _local_step(x, mods, target, final_norm, full, positions):
    tabs = _rope_tables(positions)
    depth = len(mods)
    wts, saved = [None] * depth, []
    h = x
    for l in range(depth):
        early = _layer_weights({n: full[n][l] for n in FIRST_NEEDED}, full, l)
        h, s, wts[l] = _layer_fwd(h, mods[l], early, tabs, f"l{l}", lambda _, l=l: {n: full[n][l] for n in LATER_NEEDED})
        saved.append(s)
    loss, dh, dfn = _loss_head(h, final_norm[None], target, "loss_head")
    dmods, grads = [None] * depth, [None] * depth
    for l in reversed(range(depth)):
        dh, dmods[l], grads[l] = _layer_bwd(dh, saved[l], mods[l], wts[l], tabs, f"l{l}")
    return loss, dh, dmods, grads, dfn[0]


HBM_SPEC = pl.BlockSpec(memory_space=pl.ANY)
VMEM_SPEC = pl.BlockSpec(memory_space=pltpu.VMEM)
N_CHIPS = 4
N_DEV = 8


def _me():
    return lax.axis_index("x"), lax.axis_index("y"), lax.axis_index("c")


def _flip(pos, f):
    mx, my, mc = pos
    fx, fy, fc = (f >> 2) & 1, (f >> 1) & 1, f & 1
    return ((mx + fx) % 2, (my + fy) % 2, (mc + fc) % 2)


def _all_gather_small(x, name):
    r, n = x.shape

    def body(x_ref, out_ref, send_sems, recv_sems, local_sem):
        me = _me()
        row = lambda p: 4 * p[0] + 2 * p[1] + p[2]
        mine = pltpu.make_async_copy(x_ref, out_ref.at[row(me)], local_sem)
        mine.start()

        def copy(f, origin):
            return pltpu.make_async_remote_copy(
                src_ref=x_ref, dst_ref=out_ref.at[row(origin)], send_sem=send_sems.at[f - 1], recv_sem=recv_sems.at[f - 1],
                device_id=_flip(me, f), device_id_type=MESH)

        sends = [copy(f, me) for f in range(1, N_DEV)]
        for cp in sends:
            cp.start()
        for f in range(1, N_DEV):
            copy(f, _flip(me, f)).wait_recv()
        for cp in sends:
            cp.wait_send()
        mine.wait()

    return pl.pallas_call(
        body, out_shape=SDS((N_DEV, r, n), x.dtype), in_specs=[VMEM_SPEC], out_specs=VMEM_SPEC,
        scratch_shapes=[pltpu.SemaphoreType.DMA((N_DEV - 1,)), pltpu.SemaphoreType.DMA((N_DEV - 1,)), pltpu.SemaphoreType.DMA],
        name=name, compiler_params=pltpu.CompilerParams(vmem_limit_bytes=VMEM_LIMIT))(x)


CHIP_FLIPS = (2, 4, 6)
SIBLING = 1


def _chip_of(pos):
    return 2 * pos[0] + pos[1]


def _dma_sems(n):
    return [pltpu.SemaphoreType.DMA((n,)), pltpu.SemaphoreType.DMA((n,))]


def _gather_weights(shards, name):
    n_arr = len(shards)

    def body(*refs):
        ins, outs = refs[:n_arr], refs[n_arr:2 * n_arr]
        send_sems, recv_sems = refs[2 * n_arr:]
        me = _me()
        layer = me[2]

        def ici(a, k, origin):
            return pltpu.make_async_remote_copy(
                src_ref=ins[a].at[layer], dst_ref=outs[a].at[layer, _chip_of(origin)], send_sem=send_sems.at[6 * a + k],
                recv_sem=recv_sems.at[6 * a + k], device_id=_flip(me, CHIP_FLIPS[k]), device_id_type=MESH)

        def d2d(a, k, lay):
            slab = outs[a].at[lay, _chip_of(_flip(me, CHIP_FLIPS[k]))]
            return pltpu.make_async_remote_copy(
                src_ref=slab, dst_ref=slab, send_sem=send_sems.at[6 * a + 3 + k], recv_sem=recv_sems.at[6 * a + 3 + k],
                device_id=_flip(me, SIBLING), device_id_type=MESH)

        sends = [ici(a, k, me) for a in range(n_arr) for k in range(3)]
        for cp in sends:
            cp.start()
        passed = []
        for a in range(n_arr):
            for k in range(3):
                ici(a, k, _flip(me, CHIP_FLIPS[k])).wait_recv()
                passed.append(d2d(a, k, layer))
                passed[-1].start()
        for a in range(n_arr):
            for k in range(3):
                d2d(a, k, 1 - layer).wait_recv()
        for cp in sends + passed:
            cp.wait_send()

    return pl.pallas_call(
        body, out_shape=[SDS((2, N_CHIPS) + s.shape[1:], s.dtype) for s in shards],
        in_specs=[HBM_SPEC] * n_arr, out_specs=[HBM_SPEC] * n_arr, scratch_shapes=_dma_sems(6 * n_arr), name=name)(*shards)


def _send_other_layer(g0, g1, name):
    n_arr = len(g0)

    def body(*refs):
        in0, in1, outs = refs[:n_arr], refs[n_arr:2 * n_arr], refs[2 * n_arr:3 * n_arr]
        send_sems, recv_sems = refs[3 * n_arr:]
        me = _me()

        def copy(a, src):
            return pltpu.make_async_remote_copy(
                src_ref=src, dst_ref=outs[a], send_sem=send_sems.at[a], recv_sem=recv_sems.at[a],
                device_id=_flip(me, SIBLING), device_id_type=MESH)

        @pl.when(me[2] == 0)
        def _():
            for a in range(n_arr):
                copy(a, in1[a]).start()

        @pl.when(me[2] == 1)
        def _():
            for a in range(n_arr):
                copy(a, in0[a]).start()

        for a in range(n_arr):
            copy(a, in0[a]).wait()

    return pl.pallas_call(
        body, out_shape=[SDS(g.shape, g.dtype) for g in g0], in_specs=[HBM_SPEC] * (2 * n_arr),
        out_specs=[HBM_SPEC] * n_arr, scratch_shapes=_dma_sems(n_arr), name=name)(*g0, *g1)


def _scatter_chips(ps, name):
    n_arr = len(ps)

    def body(*refs):
        ins, outs = refs[:n_arr], refs[n_arr:2 * n_arr]
        send_sems, recv_sems = refs[2 * n_arr:]
        me = _me()

        def copy(a, k):
            return pltpu.make_async_remote_copy(
                src_ref=ins[a].at[_chip_of(_flip(me, CHIP_FLIPS[k]))], dst_ref=outs[a].at[k],
                send_sem=send_sems.at[3 * a + k], recv_sem=recv_sems.at[3 * a + k],
                device_id=_flip(me, CHIP_FLIPS[k]), device_id_type=MESH)

        copies = [copy(a, k) for a in range(n_arr) for k in range(3)]
        for cp in copies:
            cp.start()
        for cp in copies:
            cp.wait_recv()
        for cp in copies:
            cp.wait_send()

    return pl.pallas_call(
        body, out_shape=[SDS((3,) + p.shape[1:], p.dtype) for p in ps], in_specs=[HBM_SPEC] * n_arr,
        out_specs=[HBM_SPEC] * n_arr, scratch_shapes=_dma_sems(3 * n_arr), name=name)(*ps)


def _swap_layers(rs, name):
    n_arr = len(rs)

    def body(*refs):
        ins, outs = refs[:n_arr], refs[n_arr:2 * n_arr]
        send_sems, recv_sems = refs[2 * n_arr:]
        me = _me()
        copies = [pltpu.make_async_remote_copy(
            src_ref=ins[a], dst_ref=outs[a], send_sem=send_sems.at[a], recv_sem=recv_sems.at[a],
            device_id=_flip(me, SIBLING), device_id_type=MESH) for a in range(n_arr)]
        for cp in copies:
            cp.start()
        for cp in copies:
            cp.wait()

    return pl.pallas_call(
        body, out_shape=[SDS(r.shape, r.dtype) for r in rs], in_specs=[HBM_SPEC] * n_arr, out_specs=[HBM_SPEC] * n_arr,
        scratch_shapes=_dma_sems(n_arr), name=name)(*rs)


def _add_sibling(g0, g1, got, layer, col_shards, name):
    k, n = g0.shape
    tr = _pick(k, (512, 256, 128))
    width = n // N_CHIPS if col_shards else n
    cw = _pick(width, (1024, 512, 256, 128))
    per = width // cw

    def body(l_ref, a0_ref, a1_ref, b_ref, o_ref):
        mine = jnp.where(l_ref[0] == 0, a0_ref[...], a1_ref[...])
        o_ref[...] = (mine.astype(F32) + b_ref[...].astype(F32)).astype(o_ref.dtype)

    blk = pl.BlockSpec((tr, cw), lambda i, j, l: (i, j))
    if col_shards:
        out_spec = pl.BlockSpec((None, tr, cw), lambda i, j, l: (j // per, i, j % per))
        out_shape = SDS((N_CHIPS, k, width), g0.dtype)
    else:
        out_spec, out_shape = blk, SDS((k, n), g0.dtype)
    grid_spec = pltpu.PrefetchScalarGridSpec(
        num_scalar_prefetch=1, grid=(k // tr, n // cw), in_specs=[blk, blk, blk], out_specs=out_spec)
    return pl.pallas_call(body, grid_spec=grid_spec, out_shape=out_shape, name=name,
                          compiler_params=_cp(("parallel", "parallel")))(layer, g0, g1, got)


def _sum_chips(p, got, chip, name):
    _, k, ns = p.shape
    tr = _pick(k, (256, 128, 64, 32, 16))

    def body(c_ref, a_ref, b_ref, o_ref):
        acc = a_ref[0].astype(F32)
        for j in range(3):
            acc = acc + b_ref[j].astype(F32)
        o_ref[...] = acc

    grid_spec = pltpu.PrefetchScalarGridSpec(
        num_scalar_prefetch=1, grid=(k // tr,),
        in_specs=[pl.BlockSpec((1, tr, ns), lambda i, c: (c[0], i, 0)), pl.BlockSpec((3, tr, ns), lambda i, c: (0, i, 0))],
        out_specs=pl.BlockSpec((tr, ns), lambda i, c: (i, 0)))
    return pl.pallas_call(
        body, grid_spec=grid_spec, out_shape=SDS((k, ns), F32), name=name, compiler_params=_cp(("parallel",)))(chip, p, got)


SEM_SPEC = pl.BlockSpec(memory_space=pltpu.SEMAPHORE)
HBM_ONLY = pl.BlockSpec(memory_space=pltpu.HBM)
DATAFLOW = pltpu.SideEffectType.DATAFLOW_SIDE_EFFECTING


def _in_hbm(a):
    return pltpu.with_memory_space_constraint(a, pltpu.HBM)


def _copies_start(name, srcs, lands, plan, n_copies):
    ns, nl = len(srcs), len(lands)

    def body(*refs):
        src_refs, land_refs = refs[:ns], refs[ns:ns + nl]
        send_sems, recv_sems = refs[ns + nl], refs[ns + nl + 1]
        token = refs[-1]
        for i, (src, dst, peer) in enumerate(plan(_me(), src_refs, land_refs)):
            pltpu.make_async_remote_copy(src_ref=src, dst_ref=dst, send_sem=send_sems.at[i], recv_sem=recv_sems.at[i],
                                         device_id=peer, device_id_type=MESH).start()
        token[...] = jnp.zeros_like(token)

    outs = pl.pallas_call(
        body, name=name,
        out_shape=(pltpu.SemaphoreType.DMA((n_copies,)), pltpu.SemaphoreType.DMA((n_copies,)),
                   *[pltpu.HBM(l.shape, l.dtype) for l in lands], SDS((8, LANES), F32)),
        in_specs=[HBM_ONLY] * (ns + nl), out_specs=(SEM_SPEC, SEM_SPEC, *[HBM_ONLY] * nl, VMEM_SPEC),
        input_output_aliases={ns + i: 2 + i for i in range(nl)},
        compiler_params=pltpu.CompilerParams(has_side_effects=DATAFLOW),
    )(*[_in_hbm(s) for s in srcs], *[_in_hbm(l) for l in lands])
    return outs[0], outs[1], list(outs[2:2 + nl]), outs[-1]


def _copies_wait(name, srcs, lands, send_sems, recv_sems, plan, after):
    ns, nl = len(srcs), len(lands)

    def body(*refs):
        src_refs, land_refs = refs[:ns], refs[ns:ns + nl]
        send_ref, recv_ref = refs[ns + nl], refs[ns + nl + 1]
        for i, (src, dst, peer) in enumerate(plan(_me(), src_refs, land_refs)):
            cp = pltpu.make_async_remote_copy(src_ref=src, dst_ref=dst, send_sem=send_ref.at[i], recv_sem=recv_ref.at[i],
                                              device_id=peer, device_id_type=MESH)
            cp.wait_send()
            cp.wait_recv()

    outs = pl.pallas_call(
        body, name=name, out_shape=[pltpu.HBM(l.shape, l.dtype) for l in lands],
        in_specs=[HBM_ONLY] * (ns + nl) + [SEM_SPEC, SEM_SPEC, HBM_SPEC], out_specs=[HBM_ONLY] * nl,
        input_output_aliases={ns + i: i for i in range(nl)},
        compiler_params=pltpu.CompilerParams(has_side_effects=DATAFLOW),
    )(*[_in_hbm(s) for s in srcs], *lands, send_sems, recv_sems, after)
    return list(outs)


def _half(ref, rows, axis):
    idx = [slice(None)] * axis + [rows]
    return ref.at[tuple(idx)]


def _gather_plans(layer, halves):
    def ici(me, srcs, lands):
        out = []
        for a, kh in enumerate(halves):
            rows = pl.ds(pl.multiple_of(me[2] * kh, 16), kh)
            for k in range(3):
                out.append((srcs[a].at[layer, rows], lands[a].at[_chip_of(me), rows], _flip(me, CHIP_FLIPS[k])))
        return out

    def d2d(me, srcs, lands):
        out = []
        for a, kh in enumerate(halves):
            rows = pl.ds(pl.multiple_of(me[2] * kh, 16), kh)
            for k in range(3):
                slab = lands[a].at[_chip_of(_flip(me, CHIP_FLIPS[k])), rows]
                out.append((slab, slab, _flip(me, SIBLING)))
        return out

    return ici, d2d


def _to_sibling_plan(halves, axes):
    def plan(me, srcs, lands):
        out = []
        for a, (kh, axis) in enumerate(zip(halves, axes)):
            rows = pl.ds(pl.multiple_of((1 - me[2]) * kh, 16), kh)
            out.append((_half(srcs[a], rows, axis), lands[a], _flip(me, SIBLING)))
        return out

    return plan


def _to_chips_plan(n_arr):
    def plan(me, srcs, lands):
        out = []
        for a in range(n_arr):
            for k in range(3):
                peer = _flip(me, CHIP_FLIPS[k])
                out.append((srcs[a].at[_chip_of(peer)], lands[a].at[k], peer))
        return out

    return plan


def _swap_plan(n_arr):
    def plan(me, srcs, lands):
        return [(srcs[a], lands[a], _flip(me, SIBLING)) for a in range(n_arr)]

    return plan


def _add_half(g, got, half, col_shards, name):
    s, kh, n = got.shape
    tr = _pick(kh, (512, 256, 128, 64, 32, 16))
    nrb = kh // tr
    width = n // N_CHIPS if col_shards else n
    cw = _pick(width, (1024, 512, 256, 128))
    per = width // cw

    def body(h_ref, a_ref, b_ref, o_ref):
        o_ref[...] = (a_ref[...].astype(F32) + b_ref[...].astype(F32)).astype(o_ref.dtype)

    in_specs = [pl.BlockSpec((None, tr, cw), lambda j, i, c, h: (j, h[0] * nrb + i, c)),
                pl.BlockSpec((None, tr, cw), lambda j, i, c, h: (j, i, c))]
    if col_shards:
        assert s == 1
        out_spec = pl.BlockSpec((None, tr, cw), lambda j, i, c, h: (c // per, i, c % per))
        out_shape = SDS((N_CHIPS, kh, width), g.dtype)
    else:
        out_spec, out_shape = in_specs[1], SDS((s, kh, n), g.dtype)
    grid_spec = pltpu.PrefetchScalarGridSpec(num_scalar_prefetch=1, grid=(s, nrb, n // cw), in_specs=in_specs,
                                             out_specs=out_spec)
    return pl.pallas_call(body, grid_spec=grid_spec, out_shape=out_shape, name=name,
                          compiler_params=_cp(("parallel", "parallel", "parallel")))(half, g, got)


def _sum_devices(g, name):
    _, _, n = g.shape

    def body(g_ref, o_ref):
        acc = g_ref[0]
        for k in range(1, N_DEV):
            acc = acc + g_ref[k]
        o_ref[...] = acc

    return pl.pallas_call(body, out_shape=SDS((1, n), F32), in_specs=[VMEM_SPEC], out_specs=VMEM_SPEC, name=name)(g)


def _silu_rows(c, name):
    def body(c_ref, o_ref):
        v = c_ref[...]
        o_ref[...] = v * _sigmoid(v)

    return pl.pallas_call(body, out_shape=SDS(c.shape, F32), in_specs=[VMEM_SPEC], out_specs=VMEM_SPEC, name=name)(c)


BIG = (("w_in", 2), ("w_uq", 2), ("w_ukv", 2), ("w_o_mla", 2), ("w_o_gdn", 2), ("w_o", 1), ("w_gate_up", 2), ("w_down", 1))
KERNEL_BIG = ("w_main", "w_small", "w_uq", "w_ukv", "w_o_mla", "w_o_gdn", "w_o", "w_gate_up", "w_down")
COL_SHARDED_AS_IS = ("w_o_mla", "w_o_gdn", "w_gate_up")
ROW_SHARDED = ("w_o", "w_down")
FIRST_NEEDED = ("w_in", "w_uq", "w_ukv")
LATER_NEEDED = ("w_o_mla", "w_o_gdn", "w_o", "w_gate_up", "w_down")
FFN_GRADS = ("w_gate_up", "w_down")
MIXER_GRADS = ("w_in", "w_uq", "w_ukv", "w_o_mla", "w_o_gdn", "w_o")
MIXER_GRADS_KERNEL = ("w_main", "w_small", "w_uq", "w_ukv", "w_o_mla", "w_o_gdn", "w_o")
SMALL = ("norm_mix", "norm_ffn", "q_a_norm", "kv_a_norm", "A_log", "dt_bias", "gdn_norm")
WEIGHTS = ("w_ada", "b_ada", "norm_mix", "norm_ffn", "w_in", "q_a_norm", "kv_a_norm", "w_uq", "w_ukv", "w_o_mla", "conv_w",
           "A_log", "dt_bias", "gdn_norm", "w_o_gdn", "w_o", "w_gate_up", "w_down", "final_norm")
ADA_PAD = 16
K_PAD = 128


def _pad_to(a, n, axis):
    pad = [(0, 0)] * a.ndim
    pad[axis] = (0, n - a.shape[axis])
    return jnp.pad(a, pad)


def kernel(x, c, positions, w_ada, b_ada, norm_mix, norm_ffn, w_in, q_a_norm, kv_a_norm, w_uq, w_ukv, w_o_mla, conv_w, A_log, dt_bias, gdn_norm, w_o_gdn, w_o, w_gate_up, w_down, final_norm, loss_target, m_w_ada, m_b_ada, m_norm_mix, m_norm_ffn, m_w_in, m_q_a_norm, m_kv_a_norm, m_w_uq, m_w_ukv, m_w_o_mla, m_conv_w, m_A_log, m_dt_bias, m_gdn_norm, m_w_o_gdn, m_w_o, m_w_gate_up, m_w_down, m_final_norm, v_w_ada, v_b_ada, v_norm_mix, v_norm_ffn, v_w_in, v_q_a_norm, v_kv_a_norm, v_w_uq, v_w_ukv, v_w_o_mla, v_conv_w, v_A_log, v_dt_bias, v_gdn_norm, v_w_o_gdn, v_w_o, v_w_gate_up, v_w_down, v_final_norm):
    env = dict(locals())
    w = {n: env[n] for n in WEIGHTS}
    depth, d = norm_mix.shape
    t = x.shape[1]
    me = _me()
    chip = _chip_of(me)
    dev = 4 * me[0] + 2 * me[1] + me[2]
    ada_cols = w_ada.shape[2]

    half_idx = me[2].astype(jnp.int32).reshape(1)
    chip_idx = chip.astype(jnp.int32).reshape(1)
    w16 = {n: w[n].astype(BF16) for n, _ in BIG}
    shard_axis = dict(BIG)
    gather = {}

    def start_group(key, layer, names, dep):
        srcs = [w16[n] for n in names]
        plans = _gather_plans(layer, [a.shape[1] // 2 for a in srcs])
        landing = [lax.empty((N_CHIPS,) + a.shape[1:], BF16) for a in srcs]
        send_s, recv_s, landing, tok = _copies_start(f"gather_{key}_ici_start", srcs + [dep], landing, plans[0], 3 * len(names))
        gather[key] = dict(layer=layer, names=names, srcs=srcs, plans=plans, ici=(send_s, recv_s, landing), tok=tok)

    def pass_to_sibling(key, after):
        st = gather[key]
        send_s, recv_s, landing = st["ici"]
        landing = _copies_wait(f"gather_{key}_ici_wait", st["srcs"] + [st["tok"]], landing, send_s, recv_s, st["plans"][0],
                               st["tok"] if after is None else after)
        st["d2d"] = _copies_start(f"gather_{key}_d2d_start", [], landing, st["plans"][1], 3 * len(st["names"]))
        return st["d2d"][3]

    def gathered(key):
        st = gather[key]
        send_s, recv_s, landing, tok = st["d2d"]
        landing = _copies_wait(f"gather_{key}_d2d_wait", [], landing, send_s, recv_s, st["plans"][1], tok)
        return {n: jnp.concatenate([jnp.where(chip == j, own[st["layer"]], got[j]) for j in range(N_CHIPS)],
                                   axis=shard_axis[n] - 1)
                for n, own, got in zip(st["names"], st["srcs"], landing)}

    full = {}
    conv_all = _all_gather_small(conv_w.reshape(1, -1), "gather_conv").reshape((N_DEV,) + conv_w.shape)
    full["conv_w"] = jnp.concatenate([conv_all[2 * j] for j in range(N_CHIPS)], axis=2)
    for n in SMALL:
        full[n] = w[n]

    c_all = _all_gather_small(c, "gather_c").reshape(N_DEV, d)
    c_act = _silu_rows(_pad_to(c_all, ADA_PAD, 0), "silu_c")
    b_cols = lax.dynamic_slice_in_dim(b_ada, chip * ada_cols, ada_cols, axis=1)
    mod_cols = jnp.stack([
        _mm(c_act, w_ada[l], acc_in=jnp.broadcast_to(b_cols[l][None], (ADA_PAD, ada_cols)), name=f"ada_l{l}")[:N_DEV]
        for l in range(depth)])
    mod_all = _all_gather_small(mod_cols.reshape(depth * N_DEV, ada_cols), "gather_mod")
    mod_all = mod_all.reshape(N_DEV, depth, N_DEV, ada_cols)
    mods = []
    for l in range(depth):
        mine = jnp.concatenate([lax.dynamic_index_in_dim(mod_all[2 * j, l], dev, axis=0, keepdims=True)
                                for j in range(N_CHIPS)], axis=1)
        mods.append(tuple(mine[:, i * d:(i + 1) * d] for i in range(6)))

    tabs = _rope_tables(positions[0])
    start_group("l0a", 0, FIRST_NEEDED, mods[depth - 1][5])
    start_group("l0b", 0, LATER_NEEDED, pass_to_sibling("l0a", None))

    def late_weights(key, next_key, next_layer, behind):
        tok = pass_to_sibling(key, behind)
        if next_key is not None:
            start_group(next_key, next_layer, FIRST_NEEDED, tok)
        return gathered(key)

    wts, saved = [None] * depth, [None] * depth
    h, saved[0], wts[0] = _layer_fwd(x[0], mods[0], _layer_weights(gathered("l0a"), full, 0), tabs, "l0",
                                     functools.partial(late_weights, "l0b", "l1a", 1))
    start_group("l1b", 1, LATER_NEEDED, pass_to_sibling("l1a", h))
    h, saved[1], wts[1] = _layer_fwd(h, mods[1], _layer_weights(gathered("l1a"), full, 1), tabs, "l1",
                                     functools.partial(late_weights, "l1b", None, None))
    loss_part, dh, dfn = _loss_head(h, final_norm[None], loss_target[0], "loss_head")
    dfn = dfn[0]
    loss = lax.psum(loss_part[0, 0], AXES)

    def col_shards(g):
        return g.reshape(g.shape[0], N_CHIPS, g.shape[1] // N_CHIPS).transpose(1, 0, 2)

    def reduce_scatter_stages(tag, g, knames, names):
        srcs = [g[n].reshape(N_CHIPS, -1, g[n].shape[1]) if n in ROW_SHARDED else g[n] for n in knames]
        axes = [1 if n in ROW_SHARDED else 0 for n in knames]
        halves = [a.shape[ax] // 2 for a, ax in zip(srcs, axes)]
        got_shapes = [a.shape[:ax] + (kh,) + a.shape[ax + 1:] for a, ax, kh in zip(srcs, axes, halves)]
        plan_a, plan_c, plan_e = _to_sibling_plan(halves, axes), _to_chips_plan(len(names)), _swap_plan(len(names))
        st, out = {}, {}
        st["a"] = _copies_start(f"{tag}_sibling_start", srcs, [lax.empty(sh, BF16) for sh in got_shapes], plan_a, len(srcs))

        def after_or(tok, after):
            return tok if after is None else after

        def stage0(after):
            send_s, recv_s, landing, tok = st["a"]
            got = _copies_wait(f"{tag}_sibling_wait", srcs, landing, send_s, recv_s, plan_a, after_or(tok, after))
            sums = {}
            for n, a, b in zip(knames, srcs, got):
                a3, b3 = (v if v.ndim == 3 else v[None] for v in (a, b))
                r = _add_half(a3, b3, half_idx, n in COL_SHARDED_AS_IS, f"{tag}_add_{n}")
                sums[n] = r if (n in COL_SHARDED_AS_IS or n in ROW_SHARDED) else r[0]
            if "w_main" in sums:
                sums["w_in"] = col_shards(_unlay_w_in(sums["w_main"], sums["w_small"]))
                sums["w_uq"] = col_shards(_unlay_w_uq(sums["w_uq"]))
                sums["w_ukv"] = col_shards(_unlay_w_ukv(sums["w_ukv"]))
            st["p"] = [sums[n] for n in names]
            st["c"] = _copies_start(f"{tag}_chips_start", st["p"], [lax.empty((3,) + p.shape[1:], BF16) for p in st["p"]],
                                    plan_c, 3 * len(names))
            return st["c"][3][0, 0]

        def stage1(after):
            send_s, recv_s, landing, tok = st["c"]
            got = _copies_wait(f"{tag}_chips_wait", st["p"], landing, send_s, recv_s, plan_c, after_or(tok, after))
            st["r"] = [_sum_chips(p, q, chip_idx, f"{tag}_sum_{n}") for n, p, q in zip(names, st["p"], got)]
            st["e"] = _copies_start(f"{tag}_swap_start", st["r"], [lax.empty(r.shape, F32) for r in st["r"]], plan_e, len(names))
            return st["e"][3][0, 0]

        def stage2(after):
            send_s, recv_s, landing, tok = st["e"]
            got = _copies_wait(f"{tag}_swap_wait", st["r"], landing, send_s, recv_s, plan_e, after_or(tok, after))
            for n, mine, theirs in zip(names, st["r"], got):
                out[n] = jnp.where(me[2] == 0, jnp.concatenate([mine, theirs]), jnp.concatenate([theirs, mine]))

        return (stage0, stage1, stage2), out, st["a"][3][0, 0]

    dmods, grads, groups = [None] * depth, [None] * depth, {}

    def ffn_group_l1(g):
        groups["l1_ffn"] = reduce_scatter_stages("rs_l1_ffn", g, FFN_GRADS, FFN_GRADS)
        return groups["l1_ffn"][2]

    dh, dmods[1], grads[1] = _layer_bwd(dh, saved[1], mods[1], wts[1], tabs, "l1", after_ffn=ffn_group_l1)
    groups["l1_mix"] = reduce_scatter_stages("rs_l1_mix", grads[1], MIXER_GRADS_KERNEL, MIXER_GRADS)
    tied = mods[0][:5] + (mods[0][5] + groups["l1_mix"][2],)

    def ffn_group_l0(g):
        behind = g["w_gate_up"]
        tok = groups["l1_ffn"][0][0](behind) + groups["l1_mix"][0][0](behind)
        groups["l0_ffn"] = reduce_scatter_stages("rs_l0_ffn", g, FFN_GRADS, FFN_GRADS)
        return tok + groups["l0_ffn"][2]

    def after_gdn_l0(behind):
        return groups["l0_ffn"][0][0](behind)

    dx, dmods[0], grads[0] = _layer_bwd(dh, saved[0], tied, wts[0], tabs, "l0", after_ffn=ffn_group_l0, after_gdn=after_gdn_l0)
    groups["l0_mix"] = reduce_scatter_stages("rs_l0_mix", grads[0], MIXER_GRADS_KERNEL, MIXER_GRADS)
    for key in ("l1_ffn", "l1_mix", "l0_ffn"):
        groups[key][0][1](dx)
    for stage in groups["l0_mix"][0]:
        stage(None)
    for key in ("l1_ffn", "l1_mix", "l0_ffn"):
        groups[key][0][2](groups["l0_mix"][1]["w_in"])
    g_out = {n: jnp.stack([groups[f"l{l}_ffn" if n in FFN_GRADS else f"l{l}_mix"][1][n] for l in range(depth)])
             for n, _ in BIG}

    small = [_small_grads_ref_layout(grads[l]) for l in range(depth)]
    small_parts = [jnp.concatenate(dmods[l], axis=1).reshape(-1) for l in range(depth)]
    small_parts += [jnp.stack([small[l][n] for l in range(depth)]).reshape(-1) for n in SMALL]
    small_parts.append(dfn)
    small_sizes = [p.shape[0] for p in small_parts]
    packed = jnp.concatenate(small_parts)
    n_small = -(-packed.shape[0] // LANES) * LANES
    small_all = _all_gather_small(_pad_to(packed, n_small, 0).reshape(1, n_small), "gather_small_grads")
    small_sum = _sum_devices(small_all, "sum_small_grads")[0]
    offs = [0]
    for sz in small_sizes:
        offs.append(offs[-1] + sz)
    g_out["b_ada"] = jnp.stack([small_sum[offs[l]:offs[l + 1]] for l in range(depth)])
    for i, n in enumerate(SMALL):
        g_out[n] = small_sum[offs[depth + i]:offs[depth + i + 1]].reshape(w[n].shape)
    g_out["final_norm"] = small_sum[offs[depth + len(SMALL)]:offs[depth + len(SMALL) + 1]]

    c_act_t = _pad_to(c_act[:N_DEV].T, K_PAD, 1)
    g_ada = []
    for l in range(depth):
        dmod_l = small_all[:, 0, offs[l]:offs[l + 1]]
        dmod_cols = lax.dynamic_slice_in_dim(dmod_l, chip * ada_cols, ada_cols, axis=1)
        g_ada.append(_mm(c_act_t, _pad_to(dmod_cols, K_PAD, 0), name=f"ada_grad_l{l}"))
    g_out["w_ada"] = jnp.stack(g_ada)

    conv_g = jnp.stack([small[l]["conv_w"] for l in range(depth)])
    conv_all_g = _all_gather_small(conv_g.reshape(1, -1), "gather_conv_grads")
    conv_sum = _sum_devices(conv_all_g, "sum_conv_grads").reshape(conv_g.shape)
    n_cc = conv_w.shape[2]
    g_out["conv_w"] = lax.dynamic_slice_in_dim(conv_sum, chip * n_cc, n_cc, axis=2)

    deltas, new_m, new_v = {}, {}, {}
    for n in WEIGHTS:
        deltas[n], new_m[n], new_v[n] = _adamw(w[n], g_out[n], env["m_" + n], env["v_" + n], f"adamw_{n}")
    return (loss, dx[None], *[g_out[n] for n in WEIGHTS], *[deltas[n] for n in WEIGHTS],
            *[new_m[n] for n in WEIGHTS], *[new_v[n] for n in WEIGHTS])
```

```python
import functools
import math

import jax
import jax.numpy as jnp
from jax import lax
from jax.experimental import pallas as pl
from jax.experimental.pallas import tpu as pltpu

F32 = jnp.float32
BF16 = jnp.bfloat16
SDS = jax.ShapeDtypeStruct
MESH = pl.DeviceIdType.MESH
AXES = ("x", "y", "c")

EPS = 1e-6
HEADS = 8
D_NOPE = 128
D_ROPE = 64
D_QK = 256
D_V = 128
Q_LORA = 512
KV_LORA = 512
CHUNK = 64
CONV_WIDTH = 4
GDN_W = HEADS * D_V
MAIN_FIXED = 2 * Q_LORA + 4 * GDN_W
LANES = 128
VMEM_LIMIT = 56 * 1024 * 1024
ADAM_LR, ADAM_B1, ADAM_B2, ADAM_EPS, ADAM_WD, ADAM_STEP = 0.001, 0.9, 0.999, 1e-8, 0.01, 10


def _pick(n, cands):
    for cand in cands:
        if n % cand == 0:
            return cand
    return n


def _cp(sem):
    return pltpu.CompilerParams(dimension_semantics=sem, vmem_limit_bytes=VMEM_LIMIT)


def _row_tile(t):
    return _pick(t, (256, 128, 64, 32, 16, 8))


def _dot(a, b):
    return jnp.dot(a.astype(BF16), b.astype(BF16), preferred_element_type=F32)


def _dot_nt(a, b):
    return lax.dot_general(a.astype(BF16), b.astype(BF16), (((1,), (1,)), ((), ())), preferred_element_type=F32)


def _dot_tn(a, b):
    return lax.dot_general(a.astype(BF16), b.astype(BF16), (((0,), (0,)), ((), ())), preferred_element_type=F32)


def _bdot(a, b, exact=False):
    if exact:
        return lax.dot_general(a, b, (((2,), (1,)), ((0,), (0,))), precision=lax.Precision.HIGHEST,
                               preferred_element_type=F32)
    return lax.dot_general(a.astype(BF16), b.astype(BF16), (((2,), (1,)), ((0,), (0,))), preferred_element_type=F32)


def _bdot_nt(a, b):
    return lax.dot_general(a.astype(BF16), b.astype(BF16), (((2,), (2,)), ((0,), (0,))), preferred_element_type=F32)


def _sigmoid(x):
    return 1.0 / (1.0 + jnp.exp(-x))


def _mm(a, b, *, tb=False, out_dtype=F32, acc_in=None, name):
    m, k = a.shape
    n = b.shape[0] if tb else b.shape[1]
    assert (b.shape[1] if tb else b.shape[0]) == k
    tm = _pick(m, (1024, 512, 256, 128))
    tn = _pick(n, (1024, 512, 256, 128))
    tk = k if k <= 2048 else _pick(k, (512, 256, 128))
    nk = k // tk
    has_acc = acc_in is not None

    def body_one_step(*refs):
        a_ref, b_ref = refs[:2]
        o_ref = refs[-1]
        acc = _dot_nt(a_ref[...], b_ref[...]) if tb else _dot(a_ref[...], b_ref[...])
        if has_acc:
            acc = acc + refs[2][...].astype(F32)
        o_ref[...] = acc.astype(out_dtype)

    if nk == 1:
        in_specs = [pl.BlockSpec((tm, k), lambda i, j: (i, 0)),
                    pl.BlockSpec((tn, k), lambda i, j: (j, 0)) if tb else pl.BlockSpec((k, tn), lambda i, j: (0, j))]
        args = [a, b]
        if has_acc:
            in_specs.append(pl.BlockSpec((tm, tn), lambda i, j: (i, j)))
            args.append(acc_in)
        return pl.pallas_call(
            body_one_step, grid=(m // tm, n // tn), in_specs=in_specs, out_specs=pl.BlockSpec((tm, tn), lambda i, j: (i, j)),
            out_shape=SDS((m, n), out_dtype), name=name, compiler_params=_cp(("parallel", "parallel")))(*args)

    def body(*refs):
        if has_acc:
            a_ref, b_ref, c_ref, o_ref, acc = refs
        else:
            a_ref, b_ref, o_ref, acc = refs
        kk = pl.program_id(2)

        @pl.when(kk == 0)
        def _():
            if has_acc:
                acc[...] = c_ref[...].astype(F32)
            else:
                acc[...] = jnp.zeros_like(acc)

        if tb:
            acc[...] += _dot_nt(a_ref[...], b_ref[...])
        else:
            acc[...] += _dot(a_ref[...], b_ref[...])

        @pl.when(kk == nk - 1)
        def _():
            o_ref[...] = acc[...].astype(out_dtype)

    in_specs = [pl.BlockSpec((tm, tk), lambda i, j, kk: (i, kk)),
                pl.BlockSpec((tn, tk), lambda i, j, kk: (j, kk)) if tb
                else pl.BlockSpec((tk, tn), lambda i, j, kk: (kk, j))]
    args = [a, b]
    if has_acc:
        in_specs.append(pl.BlockSpec((tm, tn), lambda i, j, kk: (i, j)))
        args.append(acc_in)
    return pl.pallas_call(
        body, grid=(m // tm, n // tn, nk), in_specs=in_specs,
        out_specs=pl.BlockSpec((tm, tn), lambda i, j, kk: (i, j)),
        out_shape=SDS((m, n), out_dtype), scratch_shapes=[pltpu.VMEM((tm, tn), F32)],
        name=name, compiler_params=_cp(("parallel", "parallel", "arbitrary")))(*args)


def _norm_fwd(x, col, width, w, sc, sh, name):
    t = x.shape[0]
    tm = _row_tile(t)

    def body(x_ref, w_ref, sc_ref, sh_ref, o_ref):
        xv = x_ref[...]
        r = lax.rsqrt(jnp.mean(xv * xv, axis=-1, keepdims=True) + EPS)
        n = xv * r * w_ref[...]
        o_ref[...] = (n * (1.0 + sc_ref[...]) + sh_ref[...]).astype(o_ref.dtype)

    row = pl.BlockSpec((1, width), lambda i: (0, 0))
    return pl.pallas_call(
        body, grid=(t // tm,), in_specs=[pl.BlockSpec((tm, width), lambda i: (i, col)), row, row, row],
        out_specs=pl.BlockSpec((tm, width), lambda i: (i, 0)), out_shape=SDS((t, width), BF16),
        name=name, compiler_params=_cp(("parallel",)))(x, w, sc, sh)


def _norm_bwd(x, col, width, w, sc, dh, dres, out_dtype, name):
    t = x.shape[0]
    tm = _row_tile(t)
    has_res = dres is not None

    def body(*refs):
        if has_res:
            x_ref, w_ref, sc_ref, dh_ref, dres_ref, dx_ref, dw_ref, dsc_ref, dsh_ref = refs
        else:
            x_ref, w_ref, sc_ref, dh_ref, dx_ref, dw_ref, dsc_ref, dsh_ref = refs

        @pl.when(pl.program_id(0) == 0)
        def _():
            dw_ref[...] = jnp.zeros_like(dw_ref)
            dsc_ref[...] = jnp.zeros_like(dsc_ref)
            dsh_ref[...] = jnp.zeros_like(dsh_ref)

        xv = x_ref[...]
        dhv = dh_ref[...].astype(F32)
        wv = w_ref[...]
        r = lax.rsqrt(jnp.mean(xv * xv, axis=-1, keepdims=True) + EPS)
        xh = xv * r
        n = xh * wv
        dsh_ref[...] += jnp.sum(dhv, axis=0, keepdims=True)
        dsc_ref[...] += jnp.sum(dhv * n, axis=0, keepdims=True)
        dn = dhv * (1.0 + sc_ref[...])
        dw_ref[...] += jnp.sum(dn * xh, axis=0, keepdims=True)
        dxh = dn * wv
        dx = r * (dxh - xh * jnp.mean(dxh * xh, axis=-1, keepdims=True))
        if has_res:
            dx = dx + dres_ref[...]
        dx_ref[...] = dx.astype(out_dtype)

    row = pl.BlockSpec((1, width), lambda i: (0, 0))
    blk = pl.BlockSpec((tm, width), lambda i: (i, 0))
    in_specs = [pl.BlockSpec((tm, width), lambda i: (i, col)), row, row, blk]
    args = [x, w, sc, dh]
    if has_res:
        in_specs.append(blk)
        args.append(dres)
    return pl.pallas_call(
        body, grid=(t // tm,), in_specs=in_specs, out_specs=[blk, row, row, row],
        out_shape=[SDS((t, width), out_dtype), SDS((1, width), F32), SDS((1, width), F32), SDS((1, width), F32)],
        name=name, compiler_params=_cp(("arbitrary",)))(*args)


def _rope128(x, tc, ts1, ts2):
    return x * tc + pltpu.roll(x, 96, 1) * ts1 + pltpu.roll(x, 32, 1) * ts2


def _rope128_t(d, tc, ts1, ts2):
    return d * tc + pltpu.roll(d * ts1, 32, 1) + pltpu.roll(d * ts2, 96, 1)


def _rope_q(q_raw, tc, ts1, ts2, transpose, out_dtype, name):
    t = q_raw.shape[0]
    tm = _row_tile(t)

    def body(q_ref, tc_ref, s1_ref, s2_ref, o_ref):
        fn = _rope128_t if transpose else _rope128
        for h in range(HEADS):
            base = h * D_QK
            o_ref[:, base:base + LANES] = q_ref[:, base:base + LANES].astype(out_dtype)
            x = q_ref[:, base + LANES:base + D_QK].astype(F32)
            o_ref[:, base + LANES:base + D_QK] = fn(x, tc_ref[...], s1_ref[...], s2_ref[...]).astype(out_dtype)

    blk = pl.BlockSpec((tm, HEADS * D_QK), lambda i: (i, 0))
    tab = pl.BlockSpec((tm, LANES), lambda i: (i, 0))
    return pl.pallas_call(
        body, grid=(t // tm,), in_specs=[blk, tab, tab, tab], out_specs=blk,
        out_shape=SDS((t, HEADS * D_QK), out_dtype), name=name, compiler_params=_cp(("parallel",)))(q_raw, tc, ts1, ts2)


def _k_assemble(kv_raw, p_small, tc, ts1, ts2, name):
    t = kv_raw.shape[0]
    tm = _row_tile(t)

    def body(kn_ref, ps_ref, tc_ref, s1_ref, s2_ref, o_ref):
        kpe = _rope128(ps_ref[...], tc_ref[...], s1_ref[...], s2_ref[...]).astype(BF16)
        for h in range(HEADS):
            o_ref[:, h * D_QK:h * D_QK + LANES] = kn_ref[:, h * LANES:(h + 1) * LANES].astype(BF16)
            o_ref[:, h * D_QK + LANES:(h + 1) * D_QK] = kpe

    tab = pl.BlockSpec((tm, LANES), lambda i: (i, 0))
    return pl.pallas_call(
        body, grid=(t // tm,),
        in_specs=[pl.BlockSpec((tm, HEADS * LANES), lambda i: (i, 0)), tab, tab, tab, tab],
        out_specs=pl.BlockSpec((tm, HEADS * D_QK), lambda i: (i, 0)),
        out_shape=SDS((t, HEADS * D_QK), BF16), name=name, compiler_params=_cp(("parallel",)))(kv_raw, p_small, tc, ts1, ts2)


def _k_assemble_bwd(dk, dv, tc, ts1, ts2, name):
    t = dk.shape[0]
    tm = _row_tile(t)

    def body(dk_ref, dv_ref, tc_ref, s1_ref, s2_ref, o_ref, pe_ref):
        acc = jnp.zeros((tm, LANES), F32)
        for h in range(HEADS):
            o_ref[:, h * LANES:(h + 1) * LANES] = dk_ref[:, h * D_QK:h * D_QK + LANES].astype(BF16)
            acc = acc + dk_ref[:, h * D_QK + LANES:(h + 1) * D_QK].astype(F32)
        o_ref[:, HEADS * LANES:] = dv_ref[...].astype(BF16)
        pe_ref[...] = _rope128_t(acc, tc_ref[...], s1_ref[...], s2_ref[...])

    tab = pl.BlockSpec((tm, LANES), lambda i: (i, 0))
    return pl.pallas_call(
        body, grid=(t // tm,),
        in_specs=[pl.BlockSpec((tm, HEADS * D_QK), lambda i: (i, 0)), pl.BlockSpec((tm, HEADS * LANES), lambda i: (i, 0)),
                  tab, tab, tab],
        out_specs=[pl.BlockSpec((tm, 2 * HEADS * LANES), lambda i: (i, 0)), tab],
        out_shape=[SDS((t, 2 * HEADS * LANES), BF16), SDS((t, LANES), F32)],
        name=name, compiler_params=_cp(("parallel",)))(dk, dv, tc, ts1, ts2)


def _attn_tile(t):
    return _pick(t, (256, 128, 64))


def _attn_fwd(q, k, v, v_off, name):
    t = q.shape[0]
    tq = _attn_tile(t)
    scale = (D_NOPE + D_ROPE) ** -0.5

    def body(q_ref, k_ref, v_ref, o_ref, lse_ref):
        for i in range(t // tq):
            n_k = (i + 1) * tq
            s = _dot_nt(q_ref[i * tq:(i + 1) * tq, :], k_ref[0:n_k, :]) * scale
            row = lax.broadcasted_iota(jnp.int32, (tq, n_k), 0) + i * tq
            colv = lax.broadcasted_iota(jnp.int32, (tq, n_k), 1)
            s = jnp.where(colv <= row, s, -jnp.inf)
            m = jnp.max(s, axis=-1, keepdims=True)
            p = jnp.exp(s - m)
            l = jnp.sum(p, axis=-1, keepdims=True)
            o = _dot(p, v_ref[0:n_k, :]) / l
            o_ref[i * tq:(i + 1) * tq, :] = o.astype(BF16)
            lse_ref[0, i * tq:(i + 1) * tq, :] = m + jnp.log(l)

    return pl.pallas_call(
        body, grid=(HEADS,),
        in_specs=[pl.BlockSpec((t, D_QK), lambda h: (0, h)), pl.BlockSpec((t, D_QK), lambda h: (0, h)),
                  pl.BlockSpec((t, D_V), lambda h: (0, v_off + h))],
        out_specs=[pl.BlockSpec((t, D_V), lambda h: (0, h)), pl.BlockSpec((1, t, 1), lambda h: (h, 0, 0))],
        out_shape=[SDS((t, HEADS * D_V), BF16), SDS((HEADS, t, 1), F32)],
        name=name, compiler_params=_cp(("parallel",)))(q, k, v)


def _attn_bwd(q, k, v, v_off, o, lse, do, name):
    t = q.shape[0]
    tq = _attn_tile(t)
    scale = (D_NOPE + D_ROPE) ** -0.5

    def body(q_ref, k_ref, v_ref, o_ref, lse_ref, do_ref, dq_ref, dk_ref, dv_ref):
        dk_ref[...] = jnp.zeros_like(dk_ref)
        dv_ref[...] = jnp.zeros_like(dv_ref)
        for i in range(t // tq):
            n_k = (i + 1) * tq
            rows = slice(i * tq, (i + 1) * tq)
            qi = q_ref[rows, :]
            doi = do_ref[rows, :].astype(F32)
            s = _dot_nt(qi, k_ref[0:n_k, :]) * scale
            row = lax.broadcasted_iota(jnp.int32, (tq, n_k), 0) + i * tq
            colv = lax.broadcasted_iota(jnp.int32, (tq, n_k), 1)
            p = jnp.where(colv <= row, jnp.exp(s - lse_ref[0, rows, :]), 0.0)
            dp = _dot_nt(doi, v_ref[0:n_k, :])
            delta = jnp.sum(doi * o_ref[rows, :].astype(F32), axis=-1, keepdims=True)
            ds = p * (dp - delta) * scale
            dq_ref[rows, :] = _dot(ds, k_ref[0:n_k, :])
            dk_ref[0:n_k, :] += _dot_tn(ds, qi)
            dv_ref[0:n_k, :] += _dot_tn(p, doi)

    qk_spec = pl.BlockSpec((t, D_QK), lambda h: (0, h))
    v_spec = pl.BlockSpec((t, D_V), lambda h: (0, h))
    return pl.pallas_call(
        body, grid=(HEADS,),
        in_specs=[qk_spec, qk_spec, pl.BlockSpec((t, D_V), lambda h: (0, v_off + h)), v_spec,
                  pl.BlockSpec((1, t, 1), lambda h: (h, 0, 0)), v_spec],
        out_specs=[qk_spec, qk_spec, v_spec],
        out_shape=[SDS((t, HEADS * D_QK), F32), SDS((t, HEADS * D_QK), F32), SDS((t, HEADS * D_V), F32)],
        name=name, compiler_params=_cp(("parallel",)))(q, k, v, o, lse, do)


CONV_COLS = 256


def _conv_pre(u, w_ref, rowi):
    acc = u * w_ref[CONV_WIDTH - 1:CONV_WIDTH, :]
    for sft in range(1, CONV_WIDTH):
        shifted = jnp.where(rowi >= sft, pltpu.roll(u, sft, 0), 0.0)
        acc = acc + shifted * w_ref[CONV_WIDTH - 1 - sft:CONV_WIDTH - sft, :]
    return acc


def _conv_fwd(p_main, conv_w, name):
    t = p_main.shape[0]
    off = 2 * Q_LORA // CONV_COLS

    def body(u_ref, w_ref, y_ref):
        u = u_ref[...]
        rowi = lax.broadcasted_iota(jnp.int32, u.shape, 0)
        pre = _conv_pre(u, w_ref, rowi)
        y_ref[...] = pre * _sigmoid(pre)

    return pl.pallas_call(
        body, grid=(3 * GDN_W // CONV_COLS,),
        in_specs=[pl.BlockSpec((t, CONV_COLS), lambda j: (0, off + j)), pl.BlockSpec((CONV_WIDTH, CONV_COLS), lambda j: (0, j))],
        out_specs=pl.BlockSpec((t, CONV_COLS), lambda j: (0, j)), out_shape=SDS((t, 3 * GDN_W), F32),
        name=name, compiler_params=_cp(("parallel",)))(p_main, conv_w)


def _conv_bwd(p_main, conv_w, dyc, name):
    t = p_main.shape[0]
    off = 2 * Q_LORA // CONV_COLS

    def body(u_ref, w_ref, dy_ref, du_ref, dw_ref):
        u = u_ref[...]
        rowi = lax.broadcasted_iota(jnp.int32, u.shape, 0)
        pre = _conv_pre(u, w_ref, rowi)
        sg = _sigmoid(pre)
        dpre = dy_ref[...] * sg * (1.0 + pre * (1.0 - sg))
        du = dpre * w_ref[CONV_WIDTH - 1:CONV_WIDTH, :]
        dw_ref[CONV_WIDTH - 1:CONV_WIDTH, :] = jnp.sum(dpre * u, axis=0, keepdims=True)
        for sft in range(1, CONV_WIDTH):
            back = jnp.where(rowi < t - sft, pltpu.roll(dpre, t - sft, 0), 0.0)
            du = du + back * w_ref[CONV_WIDTH - 1 - sft:CONV_WIDTH - sft, :]
            shifted = jnp.where(rowi >= sft, pltpu.roll(u, sft, 0), 0.0)
            dw_ref[CONV_WIDTH - 1 - sft:CONV_WIDTH - sft, :] = jnp.sum(dpre * shifted, axis=0, keepdims=True)
        du_ref[...] = du.astype(BF16)

    blk = pl.BlockSpec((t, CONV_COLS), lambda j: (0, j))
    wblk = pl.BlockSpec((CONV_WIDTH, CONV_COLS), lambda j: (0, j))
    return pl.pallas_call(
        body, grid=(3 * GDN_W // CONV_COLS,),
        in_specs=[pl.BlockSpec((t, CONV_COLS), lambda j: (0, off + j)), wblk, blk],
        out_specs=[blk, wblk], out_shape=[SDS((t, 3 * GDN_W), BF16), SDS((CONV_WIDTH, 3 * GDN_W), F32)],
        name=name, compiler_params=_cp(("parallel",)))(p_main, conv_w, dyc)


B_LO, A_LO, A_HI = D_ROPE, D_ROPE + HEADS, D_ROPE + 2 * HEADS


def _softplus(z):
    e = jnp.exp(-jnp.abs(z))
    log1p = jnp.where(e < 0.01, e * (1.0 - e * (0.5 - e * (1.0 / 3.0))), jnp.log(1.0 + e))
    return jnp.maximum(z, 0.0) + log1p


def _gdn_gates(p_small, a_row, dt_row, name):
    t = p_small.shape[0]

    def body(ps_ref, a_ref, dt_ref, g_ref, gc_ref):
        x = ps_ref[...]
        lane = lax.broadcasted_iota(jnp.int32, x.shape, 1)
        is_g = (lane >= A_LO) & (lane < A_HI)
        g = jnp.where(is_g, -jnp.exp(a_ref[...]) * _softplus(x + dt_ref[...]), 0.0)
        g_ref[...] = jnp.where(is_g, g, _sigmoid(x))
        pos = lax.broadcasted_iota(jnp.int32, x.shape, 0) % CHUNK
        acc = g
        sft = 1
        while sft < CHUNK:
            acc = acc + jnp.where(pos >= sft, pltpu.roll(acc, sft, 0), 0.0)
            sft *= 2
        gc_ref[...] = acc

    full = pl.BlockSpec((t, LANES), lambda i: (0, 0))
    row = pl.BlockSpec((1, LANES), lambda i: (0, 0))
    return pl.pallas_call(
        body, grid=(1,), in_specs=[full, row, row], out_specs=[full, full],
        out_shape=[SDS((t, LANES), F32), SDS((t, LANES), F32)], name=name,
        compiler_params=_cp(("arbitrary",)))(p_small, a_row, dt_row)


def _gdn_gates_bwd(p_small, a_row, dt_row, gates, dgates, dkpe, name):
    t = p_small.shape[0]

    def body(ps_ref, a_ref, dt_ref, g_ref, db_ref, dkpe_ref, dp_ref, da_ref, ddt_ref):
        x = ps_ref[...]
        lane = lax.broadcasted_iota(jnp.int32, x.shape, 1)
        is_g = (lane >= A_LO) & (lane < A_HI)
        is_b = (lane >= B_LO) & (lane < A_LO)
        pos = lax.broadcasted_iota(jnp.int32, x.shape, 0) % CHUNK
        acc = jnp.where(is_g, db_ref[...], 0.0)
        sft = 1
        while sft < CHUNK:
            acc = acc + jnp.where(pos < CHUNK - sft, pltpu.roll(acc, t - sft, 0), 0.0)
            sft *= 2
        dg = acc
        gv = g_ref[...]
        dz = jnp.where(is_g, dg * (-jnp.exp(a_ref[...])) * _sigmoid(x + dt_ref[...]), 0.0)
        da_ref[...] = jnp.sum(jnp.where(is_g, dg * gv, 0.0), axis=0, keepdims=True)
        ddt_ref[...] = jnp.sum(dz, axis=0, keepdims=True)
        dlb = jnp.where(is_b, db_ref[...] * gv * (1.0 - gv), 0.0)
        dp_ref[...] = (jnp.where(lane < D_ROPE, dkpe_ref[...], 0.0) + dlb + dz).astype(BF16)

    full = pl.BlockSpec((t, LANES), lambda i: (0, 0))
    row = pl.BlockSpec((1, LANES), lambda i: (0, 0))
    return pl.pallas_call(
        body, grid=(1,), in_specs=[full, row, row, full, full, full], out_specs=[full, row, row],
        out_shape=[SDS((t, LANES), BF16), SDS((1, LANES), F32), SDS((1, LANES), F32)], name=name,
        compiler_params=_cp(("arbitrary",)))(p_small, a_row, dt_row, gates, dgates, dkpe)


def _tri_inv(l, eye):
    x = eye - l
    p = _bdot(l, l, exact=True)
    steps = int(math.log2(CHUNK)) - 1
    for s in range(steps):
        x = x + _bdot(x, p, exact=True)
        if s < steps - 1:
            p = _bdot(p, p, exact=True)
    return x


def _l2n(x3):
    r = lax.rsqrt(jnp.sum(x3 * x3, axis=-1, keepdims=True) + EPS)
    return x3 * r, r


def _head_col(a_ref, lane_lo, n):
    a = a_ref[...]
    lane = lax.broadcasted_iota(jnp.int32, a.shape, 1)
    col = jnp.sum(jnp.where(lane == lane_lo + pl.program_id(0), a, 0.0), axis=-1, keepdims=True)
    return col.reshape(n, CHUNK, 1)


def _gdn_common(q3, k3, v3, b, gc):
    n = q3.shape[0]
    ri = lax.broadcasted_iota(jnp.int32, (n, CHUNK, CHUNK), 1)
    ci = lax.broadcasted_iota(jnp.int32, (n, CHUNK, CHUNK), 2)
    lower, strict = ri >= ci, ri > ci
    eye = (ri == ci).astype(F32)
    gr = jnp.sum(gc * eye, axis=1, keepdims=True)
    qh, rq = _l2n(q3)
    qn = qh * (D_V ** -0.5)
    kn, rk = _l2n(k3)
    dec = jnp.where(lower, jnp.exp(jnp.where(lower, gc - gr, 0.0)), 0.0)
    kb = kn * b
    mm = _bdot_nt(kb, kn)
    tinv = _tri_inv(jnp.where(strict, mm * dec, 0.0), eye)
    gam = jnp.exp(gc)
    u = _bdot(tinv, v3 * b, exact=True)
    w = _bdot(tinv, kb * gam, exact=True)
    qk = _bdot_nt(qn, kn)
    aqk = jnp.where(lower, qk * dec, 0.0)
    gl = gc[:, CHUNK - 1:CHUNK, :]
    kdf = jnp.exp(gl - gc)
    return dict(ri=ri, ci=ci, lower=lower, strict=strict, eye=eye, qh=qh, rq=rq, qn=qn, kn=kn, rk=rk, dec=dec, kb=kb,
                mm=mm, gam=gam, u=u, w=w, qk=qk, aqk=aqk, gl=gl, kdf=kdf, kd=kn * kdf, gr=gr)


def _gdn_fwd(yc, p_main, gates, gcum, gn, name):
    t = yc.shape[0]
    n = t // CHUNK
    z_off = (2 * Q_LORA + 3 * GDN_W) // D_V

    def body(q_ref, k_ref, v_ref, z_ref, gt_ref, gcum_ref, gn_ref, o_ref, g_ref, s_ref, u_s, w_s, qg_s, kd_s, a_s, e_s):
        c = _gdn_common(q_ref[...].reshape(n, CHUNK, D_V), k_ref[...].reshape(n, CHUNK, D_V),
                        v_ref[...].reshape(n, CHUNK, D_V), _head_col(gt_ref, B_LO, n), _head_col(gcum_ref, A_LO, n))
        u_s[...] = c["u"]
        w_s[...] = c["w"]
        qg_s[...] = c["qn"] * c["gam"]
        kd_s[...] = c["kd"]
        a_s[...] = c["aqk"]
        e_s[...] = jnp.broadcast_to(jnp.exp(c["gl"]), (n, 1, D_V))

        def step(i, s):
            s_ref[0, i] = s
            v_new = u_s[i] - _dot(w_s[i], s)
            o = _dot(qg_s[i], s) + _dot(a_s[i], v_new)
            o_ref[pl.ds(pl.multiple_of(i * CHUNK, CHUNK), CHUNK), :] = o
            return s * e_s[i] + _dot_tn(kd_s[i], v_new)

        lax.fori_loop(0, n, step, jnp.zeros((D_V, D_V), F32))
        o = o_ref[...]
        zz = z_ref[...]
        on = o * lax.rsqrt(jnp.mean(o * o, axis=-1, keepdims=True) + EPS) * gn_ref[...]
        g_ref[...] = (on * zz * _sigmoid(zz)).astype(BF16)

    col = lambda off: pl.BlockSpec((t, D_V), lambda h: (0, off + h))
    lanes = pl.BlockSpec((t, LANES), lambda h: (0, 0))
    big = pltpu.VMEM((n, CHUNK, D_V), F32)
    return pl.pallas_call(
        body, grid=(HEADS,),
        in_specs=[col(0), col(HEADS), col(2 * HEADS), col(z_off), lanes, lanes, pl.BlockSpec((1, D_V), lambda h: (0, 0))],
        out_specs=[col(0), col(0), pl.BlockSpec((1, n, D_V, D_V), lambda h: (h, 0, 0, 0))],
        out_shape=[SDS((t, GDN_W), F32), SDS((t, GDN_W), BF16), SDS((HEADS, n, D_V, D_V), F32)],
        scratch_shapes=[big, big, big, big, pltpu.VMEM((n, CHUNK, CHUNK), F32), pltpu.VMEM((n, 1, D_V), F32)],
        name=name, compiler_params=_cp(("parallel",)))(yc, yc, yc, p_main, gates, gcum, gn)


def _gdn_bwd(yc, p_main, gates, gcum, gn, o_raw, states, dgated, name):
    t = yc.shape[0]
    n = t // CHUNK
    z_off = (2 * Q_LORA + 3 * GDN_W) // D_V

    def body(q_ref, k_ref, v_ref, z_ref, gt_ref, gcum_ref, gn_ref, o_ref, s_ref, dg_ref,
             dq_ref, dk_ref, dv_ref, dz_ref, dgt_ref, dgn_ref,
             u_s, w_s, qg_s, kd_s, at_s, e_s, do_s, du_s, dw_s, dqg_s, dkd_s, da_s, dat_s, dgs_s):
        @pl.when(pl.program_id(0) == 0)
        def _():
            dgn_ref[...] = jnp.zeros_like(dgn_ref)
            dgt_ref[...] = jnp.zeros_like(dgt_ref)

        o = o_ref[...]
        zz = z_ref[...]
        dgv = dg_ref[...]
        gnv = gn_ref[...]
        r = lax.rsqrt(jnp.mean(o * o, axis=-1, keepdims=True) + EPS)
        oh = o * r
        sg = _sigmoid(zz)
        don = dgv * zz * sg
        dz_ref[...] = (dgv * oh * gnv * sg * (1.0 + zz * (1.0 - sg))).astype(BF16)
        dgn_ref[...] += jnp.sum(don * oh, axis=0, keepdims=True)
        doh = don * gnv
        do_s[...] = (r * (doh - oh * jnp.mean(doh * oh, axis=-1, keepdims=True))).reshape(n, CHUNK, D_V)

        q3 = q_ref[...].reshape(n, CHUNK, D_V)
        k3 = k_ref[...].reshape(n, CHUNK, D_V)
        v3 = v_ref[...].reshape(n, CHUNK, D_V)
        b, gc = _head_col(gt_ref, B_LO, n), _head_col(gcum_ref, A_LO, n)
        c = _gdn_common(q3, k3, v3, b, gc)
        gr = c["gr"]
        ri, ci = c["ri"], c["ci"]
        upper, sup = ci >= ri, ci > ri
        dect = jnp.where(upper, jnp.exp(jnp.where(upper, gr - gc, 0.0)), 0.0)
        tinv_t = _tri_inv(jnp.where(sup, _bdot_nt(c["kn"], c["kb"]) * dect, 0.0), c["eye"])
        u_s[...] = c["u"]
        w_s[...] = c["w"]
        qg_s[...] = c["qn"] * c["gam"]
        kd_s[...] = c["kd"]
        at_s[...] = jnp.where(upper, _bdot_nt(c["kn"], c["qn"]) * dect, 0.0)
        e_s[...] = jnp.broadcast_to(jnp.exp(c["gl"]), (n, 1, D_V))

        def step(j, ds):
            i = n - 1 - j
            s = s_ref[0, i]
            do_i = do_s[i]
            v_new = u_s[i] - _dot(w_s[i], s)
            dvn = _dot(at_s[i], do_i) + _dot(kd_s[i], ds)
            da_s[i] = _dot_nt(do_i, v_new)
            dat_s[i] = _dot_nt(v_new, do_i)
            dqg_s[i] = _dot_nt(do_i, s)
            dw_s[i] = -_dot_nt(dvn, s)
            dkd_s[i] = _dot_nt(v_new, ds)
            du_s[i] = dvn
            dgs_s[i] = jnp.broadcast_to(jnp.sum(jnp.sum(s * ds, axis=1, keepdims=True), axis=0, keepdims=True), (1, D_V))
            return _dot_tn(qg_s[i], do_i) + e_s[i] * ds - _dot_tn(w_s[i], dvn)

        lax.fori_loop(0, n, step, jnp.zeros((D_V, D_V), F32))

        du, dw, dqg, dkd = du_s[...], dw_s[...], dqg_s[...], dkd_s[...]
        lower, strict, dec = c["lower"], c["strict"], c["dec"]
        kn, kb, qn, gam, kdf = c["kn"], c["kb"], c["qn"], c["gam"], c["kdf"]
        drv = _bdot(tinv_t, du, exact=True)
        drk = _bdot(tinv_t, dw, exact=True)
        dl = jnp.where(strict, -(_bdot_nt(drv, c["u"]) + _bdot_nt(drk, c["w"])), 0.0)
        dlt = jnp.where(sup, -(_bdot_nt(c["u"], drv) + _bdot_nt(c["w"], drk)), 0.0)
        da = jnp.where(lower, da_s[...], 0.0)
        dat = jnp.where(upper, dat_s[...], 0.0)
        e = (dl * c["mm"] + da * c["qk"]) * dec
        col_sums = jnp.sum(e, axis=1, keepdims=True)
        dgc = jnp.sum(e, axis=2, keepdims=True) - jnp.sum(col_sums * c["eye"], axis=2, keepdims=True)
        dkb = _bdot(dl * dec, kn) + gam * drk
        dkn = _bdot(dlt * dect, kb) + _bdot(dat * dect, qn) + b * dkb + dkd * kdf
        dqn = _bdot(da * dec, kn) + gam * dqg
        dgam = jnp.sum(drk * kb, axis=-1, keepdims=True) + jnp.sum(dqg * qn, axis=-1, keepdims=True)
        dbeta = jnp.sum(dkb * kn, axis=-1, keepdims=True) + jnp.sum(drv * v3, axis=-1, keepdims=True)
        dv_ref[...] = (b * drv).reshape(t, D_V)
        ee = jnp.sum(dkd * kn, axis=-1, keepdims=True) * kdf
        dgc = dgc + dgam * gam - ee
        rowc = lax.broadcasted_iota(jnp.int32, (n, CHUNK, 1), 1)
        tail = jnp.sum(ee, axis=1, keepdims=True) + dgs_s[...][:, :, 0:1] * jnp.exp(c["gl"])
        dgc = dgc + jnp.where(rowc == CHUNK - 1, tail, 0.0)
        lane = lax.broadcasted_iota(jnp.int32, (t, LANES), 1)
        head = pl.program_id(0)
        dgt_ref[...] += (jnp.where(lane == B_LO + head, dbeta.reshape(t, 1), 0.0)
                         + jnp.where(lane == A_LO + head, dgc.reshape(t, 1), 0.0))
        sc = D_V ** -0.5
        qh, rq, rk = c["qh"], c["rq"], c["rk"]
        dq_ref[...] = (rq * (sc * dqn - qh * jnp.sum(sc * dqn * qh, axis=-1, keepdims=True))).reshape(t, D_V)
        dk_ref[...] = (rk * (dkn - kn * jnp.sum(dkn * kn, axis=-1, keepdims=True))).reshape(t, D_V)

    once = pl.Buffered(1)
    col = lambda off: pl.BlockSpec((t, D_V), lambda h: (0, off + h), pipeline_mode=once)
    out_col = pl.BlockSpec((t, D_V), lambda h: (0, h))
    lanes = pl.BlockSpec((t, LANES), lambda h: (0, 0))
    row = pl.BlockSpec((1, D_V), lambda h: (0, 0))
    big = pltpu.VMEM((n, CHUNK, D_V), F32)
    sq = pltpu.VMEM((n, CHUNK, CHUNK), F32)
    small = pltpu.VMEM((n, 1, D_V), F32)
    return pl.pallas_call(
        body, grid=(HEADS,),
        in_specs=[col(0), col(HEADS), col(2 * HEADS), col(z_off), lanes, lanes, row, col(0),
                  pl.BlockSpec((1, n, D_V, D_V), lambda h: (h, 0, 0, 0), pipeline_mode=once), col(0)],
        out_specs=[out_col, out_col, out_col, out_col, lanes, row],
        out_shape=[SDS((t, GDN_W), F32), SDS((t, GDN_W), F32), SDS((t, GDN_W), F32), SDS((t, GDN_W), BF16),
                   SDS((t, LANES), F32), SDS((1, D_V), F32)],
        scratch_shapes=[big, big, big, big, sq, small, big, big, big, big, big, sq, sq, small],
        name=name, compiler_params=_cp(("arbitrary",)))(yc, yc, yc, p_main, gates, gcum, gn, o_raw, states, dgated)


def _col_tile(d):
    return _pick(d, (512, 256, 128))


def _mix_fwd(y_a, y_b, p_main, name):
    t, d = y_a.shape
    tm, cw = _row_tile(t), _col_tile(d)
    off_a, off_b = MAIN_FIXED // cw, (MAIN_FIXED + d) // cw

    def body(ya_ref, yb_ref, ga_ref, gb_ref, u_ref):
        u_ref[...] = (_sigmoid(ga_ref[...]) * ya_ref[...] + _sigmoid(gb_ref[...]) * yb_ref[...]).astype(BF16)

    blk = pl.BlockSpec((tm, cw), lambda i, j: (i, j))
    return pl.pallas_call(
        body, grid=(t // tm, d // cw),
        in_specs=[blk, blk, pl.BlockSpec((tm, cw), lambda i, j: (i, off_a + j)), pl.BlockSpec((tm, cw), lambda i, j: (i, off_b + j))],
        out_specs=blk, out_shape=SDS((t, d), BF16), name=name,
        compiler_params=_cp(("parallel", "parallel")))(y_a, y_b, p_main, p_main)


def _mix_bwd(du, y_a, y_b, p_main, name):
    t, d = y_a.shape
    tm, cw = _row_tile(t), _col_tile(d)
    off_a, off_b = MAIN_FIXED // cw, (MAIN_FIXED + d) // cw
    nb = d // cw

    def body(du_ref, ya_ref, yb_ref, ga_ref, gb_ref, dya_ref, dyb_ref, dla_ref, dlb_ref):
        duv = du_ref[...]
        ga, gb = _sigmoid(ga_ref[...]), _sigmoid(gb_ref[...])
        dya_ref[...] = (duv * ga).astype(BF16)
        dyb_ref[...] = (duv * gb).astype(BF16)
        dla_ref[...] = (duv * ya_ref[...] * ga * (1.0 - ga)).astype(BF16)
        dlb_ref[...] = (duv * yb_ref[...] * gb * (1.0 - gb)).astype(BF16)

    blk = pl.BlockSpec((tm, cw), lambda i, j: (i, j))
    outs = pl.pallas_call(
        body, grid=(t // tm, nb),
        in_specs=[blk, blk, blk, pl.BlockSpec((tm, cw), lambda i, j: (i, off_a + j)),
                  pl.BlockSpec((tm, cw), lambda i, j: (i, off_b + j))],
        out_specs=[blk, blk, blk, blk],
        out_shape=[SDS((t, d), BF16), SDS((t, d), BF16), SDS((t, d), BF16), SDS((t, d), BF16)], name=name,
        compiler_params=_cp(("parallel", "parallel")))(du, y_a, y_b, p_main, p_main)
    return outs


def _gate_res(x, y, gt, name):
    t, d = x.shape
    tm = _row_tile(t)

    def body(x_ref, y_ref, g_ref, o_ref):
        o_ref[...] = x_ref[...] + g_ref[...] * y_ref[...]

    blk = pl.BlockSpec((tm, d), lambda i: (i, 0))
    return pl.pallas_call(
        body, grid=(t // tm,), in_specs=[blk, blk, pl.BlockSpec((1, d), lambda i: (0, 0))], out_specs=blk,
        out_shape=SDS((t, d), F32), name=name, compiler_params=_cp(("parallel",)))(x, y, gt)


def _gate_res_bwd(dx, y, gt, name):
    t, d = dx.shape
    tm = _row_tile(t)

    def body(dx_ref, y_ref, g_ref, dg_ref, dy_ref):
        @pl.when(pl.program_id(0) == 0)
        def _():
            dg_ref[...] = jnp.zeros_like(dg_ref)

        dxv = dx_ref[...]
        dg_ref[...] += jnp.sum(dxv * y_ref[...], axis=0, keepdims=True)
        dy_ref[...] = (dxv * g_ref[...]).astype(BF16)

    blk = pl.BlockSpec((tm, d), lambda i: (i, 0))
    row = pl.BlockSpec((1, d), lambda i: (0, 0))
    return pl.pallas_call(
        body, grid=(t // tm,), in_specs=[blk, blk, row], out_specs=[row, blk],
        out_shape=[SDS((1, d), F32), SDS((t, d), BF16)], name=name, compiler_params=_cp(("arbitrary",)))(dx, y, gt)


def _swiglu_fwd(gu, name):
    t, f2 = gu.shape
    f = f2 // 2
    tm, cw = _row_tile(t), _col_tile(f)
    nb = f // cw

    def body(g_ref, u_ref, o_ref):
        g = g_ref[...]
        o_ref[...] = (g * _sigmoid(g) * u_ref[...]).astype(BF16)

    return pl.pallas_call(
        body, grid=(t // tm, nb),
        in_specs=[pl.BlockSpec((tm, cw), lambda i, j: (i, j)), pl.BlockSpec((tm, cw), lambda i, j: (i, nb + j))],
        out_specs=pl.BlockSpec((tm, cw), lambda i, j: (i, j)), out_shape=SDS((t, f), BF16), name=name,
        compiler_params=_cp(("parallel", "parallel")))(gu, gu)


def _swiglu_bwd(gu, da, name):
    t, f2 = gu.shape
    f = f2 // 2
    tm, cw = _row_tile(t), _col_tile(f)
    nb = f // cw

    def body(g_ref, u_ref, da_ref, dg_ref, dup_ref):
        g = g_ref[...]
        dav = da_ref[...]
        sg = _sigmoid(g)
        dg_ref[...] = (dav * u_ref[...] * sg * (1.0 + g * (1.0 - sg))).astype(BF16)
        dup_ref[...] = (dav * g * sg).astype(BF16)

    blk = pl.BlockSpec((tm, cw), lambda i, j: (i, j))
    dg, dup = pl.pallas_call(
        body, grid=(t // tm, nb),
        in_specs=[blk, pl.BlockSpec((tm, cw), lambda i, j: (i, nb + j)), blk], out_specs=[blk, blk],
        out_shape=[SDS((t, f), BF16), SDS((t, f), BF16)], name=name,
        compiler_params=_cp(("parallel", "parallel")))(gu, gu, da)
    return dg, dup


def _loss_head(x, w, target, name):
    t, d = x.shape
    tm = _row_tile(t)

    def body(x_ref, w_ref, t_ref, l_ref, dx_ref, dw_ref):
        @pl.when(pl.program_id(0) == 0)
        def _():
            l_ref[...] = jnp.zeros_like(l_ref)
            dw_ref[...] = jnp.zeros_like(dw_ref)

        xv = x_ref[...]
        wv = w_ref[...]
        r = lax.rsqrt(jnp.mean(xv * xv, axis=-1, keepdims=True) + EPS)
        xh = xv * r
        err = xh * wv - t_ref[...]
        per_tok = jnp.mean(err * err, axis=-1, keepdims=True)
        l_ref[...] += 0.5 * jnp.sum(per_tok, axis=0, keepdims=True)
        dy = err * (1.0 / d)
        dw_ref[...] += jnp.sum(dy * xh, axis=0, keepdims=True)
        dxh = dy * wv
        dx_ref[...] = r * (dxh - xh * jnp.mean(dxh * xh, axis=-1, keepdims=True))

    blk = pl.BlockSpec((tm, d), lambda i: (i, 0))
    row = pl.BlockSpec((1, d), lambda i: (0, 0))
    return pl.pallas_call(
        body, grid=(t // tm,), in_specs=[blk, row, blk],
        out_specs=[pl.BlockSpec((1, LANES), lambda i: (0, 0)), blk, row],
        out_shape=[SDS((1, LANES), F32), SDS((t, d), F32), SDS((1, d), F32)], name=name,
        compiler_params=_cp(("arbitrary",)))(x, w, target)


def _adamw(w, g, m, v, name):
    shape = w.shape
    cols = shape[-1]
    rows = w.size // cols
    w2, g2, m2, v2 = (a.reshape(rows, cols) for a in (w, g, m, v))
    lanes_padded = -(-cols // LANES) * LANES
    budget_rows = max(8, (20 * 1024 * 1024) // (lanes_padded * 4 * 16))
    tr = rows
    if rows > budget_rows:
        tr = _pick(rows, tuple(c for c in (1024, 512, 256, 128, 64, 32, 16, 8) if c <= budget_rows))
    c1 = 1.0 / (1.0 - ADAM_B1 ** ADAM_STEP)
    c2 = 1.0 / (1.0 - ADAM_B2 ** ADAM_STEP)

    def body(w_ref, g_ref, m_ref, v_ref, d_ref, mo_ref, vo_ref):
        gv = g_ref[...]
        mn = ADAM_B1 * m_ref[...] + (1.0 - ADAM_B1) * gv
        vn = ADAM_B2 * v_ref[...] + (1.0 - ADAM_B2) * (gv * gv)
        mo_ref[...] = mn
        vo_ref[...] = vn
        d_ref[...] = -ADAM_LR * ((mn * c1) / (jnp.sqrt(vn * c2) + ADAM_EPS) + ADAM_WD * w_ref[...])

    blk = pl.BlockSpec((tr, cols), lambda i: (i, 0))
    outs = pl.pallas_call(
        body, grid=(rows // tr,), in_specs=[blk, blk, blk, blk], out_specs=[blk, blk, blk],
        out_shape=[SDS((rows, cols), F32)] * 3, name=name, compiler_params=_cp(("parallel",)))(w2, g2, m2, v2)
    return tuple(o.reshape(shape) for o in outs)


KPE_LO = 2 * Q_LORA
QKVZ_LO = KPE_LO + D_ROPE
BA_LO = QKVZ_LO + 4 * GDN_W
GATE_LO = BA_LO + 2 * HEADS


def _lay_w_in(w_in):
    d = w_in.shape[0]
    main = jnp.concatenate([w_in[:, :KPE_LO], w_in[:, QKVZ_LO:BA_LO], w_in[:, GATE_LO:]], axis=1)
    small = jnp.concatenate([w_in[:, KPE_LO:QKVZ_LO], w_in[:, BA_LO:GATE_LO],
                             jnp.zeros((d, LANES - D_ROPE - 2 * HEADS), w_in.dtype)], axis=1)
    return main, small


def _unlay_w_in(g_main, g_small):
    return jnp.concatenate([g_main[:, :KPE_LO], g_small[:, :D_ROPE], g_main[:, KPE_LO:KPE_LO + 4 * GDN_W],
                            g_small[:, D_ROPE:D_ROPE + 2 * HEADS], g_main[:, MAIN_FIXED:]], axis=1)


def _lay_w_uq(w_uq):
    r = w_uq.reshape(Q_LORA, HEADS, D_NOPE + D_ROPE)
    r = jnp.pad(r, ((0, 0), (0, 0), (0, D_QK - D_NOPE - D_ROPE)))
    return r.reshape(Q_LORA, HEADS * D_QK)


def _unlay_w_uq(g):
    rows = g.shape[0]
    return g.reshape(rows, HEADS, D_QK)[:, :, :D_NOPE + D_ROPE].reshape(rows, HEADS * (D_NOPE + D_ROPE))


def _lay_w_ukv(w_ukv):
    return w_ukv.reshape(KV_LORA, HEADS, 2, D_V).transpose(0, 2, 1, 3).reshape(KV_LORA, 2 * HEADS * D_V)


def _unlay_w_ukv(g):
    rows = g.shape[0]
    return g.reshape(rows, 2, HEADS, D_V).transpose(0, 2, 1, 3).reshape(rows, 2 * HEADS * D_V)


def _lane_row(vec, lo):
    return jnp.pad(vec.reshape(1, -1), ((0, 0), (lo, LANES - lo - vec.shape[0])))


def _rope_tables(positions):
    half = D_ROPE // 2
    inv_freq = 1.0 / (10000.0 ** (jnp.arange(0, D_ROPE, 2, dtype=F32) / D_ROPE))
    ang = positions.astype(F32)[:, None] * inv_freq
    cos, sin = jnp.cos(ang), jnp.sin(ang)
    t = positions.shape[0]
    zeros = lambda n: jnp.zeros((t, n), F32)
    tc = jnp.concatenate([cos, cos, zeros(LANES - D_ROPE)], axis=1)
    ts1 = jnp.concatenate([-sin, zeros(LANES - half)], axis=1)
    ts2 = jnp.concatenate([zeros(half), sin, zeros(LANES - D_ROPE)], axis=1)
    return tc, ts1, ts2


def _layer_fwd(x, mod, wt, tabs, tag, late_weights, after_gate_up=None):
    t, d = x.shape
    sh_a, sc_a, gt_a, sh_f, sc_f, gt_f = mod
    zero_l = jnp.zeros((1, Q_LORA), F32)
    s = dict(x=x)
    s["h1"] = _norm_fwd(x, 0, d, wt["norm_mix"], sc_a, sh_a, f"{tag}_norm_mix")
    s["p_main"] = _mm(s["h1"], wt["w_main"], name=f"{tag}_in_main")
    s["p_small"] = _mm(s["h1"], wt["w_small"], name=f"{tag}_in_small")
    s["cqn"] = _norm_fwd(s["p_main"], 0, Q_LORA, wt["q_a_norm"], zero_l, zero_l, f"{tag}_q_norm")
    s["ckvn"] = _norm_fwd(s["p_main"], 1, KV_LORA, wt["kv_a_norm"], zero_l, zero_l, f"{tag}_kv_norm")
    q_raw = _mm(s["cqn"], wt["w_uq"], name=f"{tag}_uq")
    s["kv_raw"] = _mm(s["ckvn"], wt["w_ukv"], name=f"{tag}_ukv")
    s["q_r"] = _rope_q(q_raw, *tabs, False, BF16, f"{tag}_rope_q")
    s["k_r"] = _k_assemble(s["kv_raw"], s["p_small"], *tabs, f"{tag}_k_asm")
    s["o"], s["lse"] = _attn_fwd(s["q_r"], s["k_r"], s["kv_raw"], HEADS, f"{tag}_attn")
    s["yc"] = _conv_fwd(s["p_main"], wt["conv_w"], f"{tag}_conv")
    s["gates"], s["gcum"] = _gdn_gates(s["p_small"], wt["a_row"], wt["dt_row"], f"{tag}_gates")
    s["o_raw"], s["gated"], s["states"] = _gdn_fwd(s["yc"], s["p_main"], s["gates"], s["gcum"], wt["gdn_norm"], f"{tag}_gdn")
    late, started = late_weights(s["gated"])
    wt = {**wt, **late}
    s["y_a"] = _mm(s["o"], wt["w_o_mla"], name=f"{tag}_o_mla")
    s["y_b"] = _mm(s["gated"], wt["w_o_gdn"], name=f"{tag}_o_gdn")
    s["u"] = _mix_fwd(s["y_a"], s["y_b"], s["p_main"], f"{tag}_mix")
    s["mixo"] = _mm(s["u"], wt["w_o"], name=f"{tag}_o")
    s["x2"] = _gate_res(x, s["mixo"], gt_a, f"{tag}_res_a")
    s["h2"] = _norm_fwd(s["x2"], 0, d, wt["norm_ffn"] + started, sc_f, sh_f, f"{tag}_norm_ffn")
    s["gu"] = _mm(s["h2"], wt["w_gate_up"], name=f"{tag}_gate_up")
    if after_gate_up is not None:
        gt_f = gt_f + after_gate_up(s["gu"])
    s["a"] = _swiglu_fwd(s["gu"], f"{tag}_swiglu")
    s["f"] = _mm(s["a"], wt["w_down"], name=f"{tag}_down")
    return _gate_res(s["x2"], s["f"], gt_f, f"{tag}_res_f"), s, wt


def _layer_bwd(dx3, s, mod, wt, tabs, tag, after_ffn=None, after_gdn=None):
    x = s["x"]
    t, d = x.shape
    sh_a, sc_a, gt_a, sh_f, sc_f, gt_f = mod
    zero_l = jnp.zeros((1, Q_LORA), F32)
    g = {}
    dgt_f, df = _gate_res_bwd(dx3, s["f"], gt_f, f"{tag}_b_res_f")
    da = _mm(df, wt["w_down"], tb=True, name=f"{tag}_b_down_x")
    g["w_down"] = _mm(s["a"].T, df, out_dtype=BF16, name=f"{tag}_b_down_w")
    dgate, dup = _swiglu_bwd(s["gu"], da, f"{tag}_b_swiglu")
    dgu = jnp.concatenate([dgate, dup], axis=1)
    dh2 = _mm(dgu, wt["w_gate_up"], tb=True, name=f"{tag}_b_gate_up_x")
    g["w_gate_up"] = _mm(s["h2"].T, dgu, out_dtype=BF16, name=f"{tag}_b_gate_up_w")
    if after_ffn is not None:
        gt_a = gt_a + after_ffn(g)
    dx2, g["norm_ffn"], dsc_f, dsh_f = _norm_bwd(s["x2"], 0, d, wt["norm_ffn"], sc_f, dh2, dx3, F32, f"{tag}_b_norm_ffn")
    dgt_a, dmixo = _gate_res_bwd(dx2, s["mixo"], gt_a, f"{tag}_b_res_a")
    du = _mm(dmixo, wt["w_o"], tb=True, name=f"{tag}_b_o_x")
    g["w_o"] = _mm(s["u"].T, dmixo, out_dtype=BF16, name=f"{tag}_b_o_w")
    dy_a, dy_b, dl_a, dl_b = _mix_bwd(du, s["y_a"], s["y_b"], s["p_main"], f"{tag}_b_mix")
    dgated = _mm(dy_b, wt["w_o_gdn"], tb=True, name=f"{tag}_b_o_gdn_x")
    g["w_o_gdn"] = _mm(s["gated"].T, dy_b, out_dtype=BF16, name=f"{tag}_b_o_gdn_w")
    dq_c, dk_c, dv_c, dz, dgates, g["gdn_norm"] = _gdn_bwd(
        s["yc"], s["p_main"], s["gates"], s["gcum"], wt["gdn_norm"], s["o_raw"], s["states"], dgated, f"{tag}_b_gdn")
    du_conv, g["conv_w"] = _conv_bwd(s["p_main"], wt["conv_w"], jnp.concatenate([dq_c, dk_c, dv_c], axis=1), f"{tag}_b_conv")
    do = _mm(dy_a, wt["w_o_mla"], tb=True, name=f"{tag}_b_o_mla_x")
    g["w_o_mla"] = _mm(s["o"].T, dy_a, out_dtype=BF16, name=f"{tag}_b_o_mla_w")
    dq_r, dk_r, dv = _attn_bwd(s["q_r"], s["k_r"], s["kv_raw"], HEADS, s["o"], s["lse"], do, f"{tag}_b_attn")
    q_a_norm = wt["q_a_norm"]
    if after_gdn is not None:
        q_a_norm = q_a_norm + after_gdn(du_conv)
    dq_raw = _rope_q(dq_r, *tabs, True, BF16, f"{tag}_b_rope_q")
    dkv_raw, dkpe = _k_assemble_bwd(dk_r, dv, *tabs, f"{tag}_b_k_asm")
    dcqn = _mm(dq_raw, wt["w_uq"], tb=True, name=f"{tag}_b_uq_x")
    g["w_uq"] = _mm(s["cqn"].T, dq_raw, out_dtype=BF16, name=f"{tag}_b_uq_w")
    dckvn = _mm(dkv_raw, wt["w_ukv"], tb=True, name=f"{tag}_b_ukv_x")
    g["w_ukv"] = _mm(s["ckvn"].T, dkv_raw, out_dtype=BF16, name=f"{tag}_b_ukv_w")
    dc_q, g["q_a_norm"], _, _ = _norm_bwd(s["p_main"], 0, Q_LORA, q_a_norm, zero_l, dcqn, None, BF16, f"{tag}_b_q_norm")
    dc_kv, g["kv_a_norm"], _, _ = _norm_bwd(s["p_main"], 1, KV_LORA, wt["kv_a_norm"], zero_l, dckvn, None, BF16,
                                            f"{tag}_b_kv_norm")
    dp_small, g["a_row"], g["dt_row"] = _gdn_gates_bwd(
        s["p_small"], wt["a_row"], wt["dt_row"], s["gates"], dgates, dkpe, f"{tag}_b_gates")
    dp_main = jnp.concatenate([dc_q, dc_kv, du_conv, dz, dl_a, dl_b], axis=1)
    h1t = s["h1"].T
    dh1 = _mm(dp_small, wt["w_small"], tb=True, name=f"{tag}_b_in_small_x")
    dh1 = _mm(dp_main, wt["w_main"], tb=True, acc_in=dh1, name=f"{tag}_b_in_main_x")
    g["w_main"] = _mm(h1t, dp_main, out_dtype=BF16, name=f"{tag}_b_in_main_w")
    g["w_small"] = _mm(h1t, dp_small, out_dtype=BF16, name=f"{tag}_b_in_small_w")
    dx, g["norm_mix"], dsc_a, dsh_a = _norm_bwd(x, 0, d, wt["norm_mix"], sc_a, dh1, dx2, F32, f"{tag}_b_norm_mix")
    return dx, (dsh_a, dsc_a, dgt_a, dsh_f, dsc_f, dgt_f), g


def _layer_weights(big, full, l):
    w_main, w_small = _lay_w_in(big["w_in"])
    return dict(
        w_main=w_main, w_small=w_small, w_uq=_lay_w_uq(big["w_uq"]), w_ukv=_lay_w_ukv(big["w_ukv"]),
        conv_w=full["conv_w"][l],
        norm_mix=full["norm_mix"][l][None], norm_ffn=full["norm_ffn"][l][None],
        q_a_norm=full["q_a_norm"][l][None], kv_a_norm=full["kv_a_norm"][l][None], gdn_norm=full["gdn_norm"][l][None],
        a_row=_lane_row(full["A_log"][l], A_LO), dt_row=_lane_row(full["dt_bias"][l], A_LO))


def _small_grads_ref_layout(g):
    return dict(
        conv_w=g["conv_w"], norm_mix=g["norm_mix"][0], norm_ffn=g["norm_ffn"][0], q_a_norm=g["q_a_norm"][0],
        kv_a_norm=g["kv_a_norm"][0], gdn_norm=g["gdn_norm"][0], A_log=g["a_row"][0, A_LO:A_HI],
        dt_bias=g["dt_row"][0, A_LO:A_HI])


def _local_step(x, mods, target, final_norm, full, positions):
    tabs = _rope_tables(positions)
    depth = len(mods)
    wts, saved = [None] * depth, []
    h = x
    for l in range(depth):
        early = _layer_weights({n: full[n][l] for n in FIRST_NEEDED}, full, l)
        h, s, wts[l] = _layer_fwd(h, mods[l], early, tabs, f"l{l}", lambda _, l=l: ({n: full[n][l] for n in LATER_NEEDED}, 0.0))
        saved.append(s)
    loss, dh, dfn = _loss_head(h, final_norm[None], target, "loss_head")
    dmods, grads = [None] * depth, [None] * depth
    for l in reversed(range(depth)):
        dh, dmods[l], grads[l] = _layer_bwd(dh, saved[l], mods[l], wts[l], tabs, f"l{l}")
    return loss, dh, dmods, grads, dfn[0]


HBM_SPEC = pl.BlockSpec(memory_space=pl.ANY)
VMEM_SPEC = pl.BlockSpec(memory_space=pltpu.VMEM)
N_CHIPS = 4
N_DEV = 8


def _me():
    return lax.axis_index("x"), lax.axis_index("y"), lax.axis_index("c")


def _flip(pos, f):
    mx, my, mc = pos
    fx, fy, fc = (f >> 2) & 1, (f >> 1) & 1, f & 1
    return ((mx + fx) % 2, (my + fy) % 2, (mc + fc) % 2)


def _all_gather_small(x, name):
    r, n = x.shape

    def body(x_ref, out_ref, send_sems, recv_sems, local_sem):
        me = _me()
        row = lambda p: 4 * p[0] + 2 * p[1] + p[2]
        mine = pltpu.make_async_copy(x_ref, out_ref.at[row(me)], local_sem)
        mine.start()

        def copy(f, origin):
            return pltpu.make_async_remote_copy(
                src_ref=x_ref, dst_ref=out_ref.at[row(origin)], send_sem=send_sems.at[f - 1], recv_sem=recv_sems.at[f - 1],
                device_id=_flip(me, f), device_id_type=MESH)

        sends = [copy(f, me) for f in range(1, N_DEV)]
        for cp in sends:
            cp.start()
        for f in range(1, N_DEV):
            copy(f, _flip(me, f)).wait_recv()
        for cp in sends:
            cp.wait_send()
        mine.wait()

    return pl.pallas_call(
        body, out_shape=SDS((N_DEV, r, n), x.dtype), in_specs=[VMEM_SPEC], out_specs=VMEM_SPEC,
        scratch_shapes=[pltpu.SemaphoreType.DMA((N_DEV - 1,)), pltpu.SemaphoreType.DMA((N_DEV - 1,)), pltpu.SemaphoreType.DMA],
        name=name, compiler_params=pltpu.CompilerParams(vmem_limit_bytes=VMEM_LIMIT))(x)


CHIP_FLIPS = (2, 4, 6)
SIBLING = 1


def _chip_of(pos):
    return 2 * pos[0] + pos[1]


def _dma_sems(n):
    return [pltpu.SemaphoreType.DMA((n,)), pltpu.SemaphoreType.DMA((n,))]


def _gather_weights(shards, name):
    n_arr = len(shards)

    def body(*refs):
        ins, outs = refs[:n_arr], refs[n_arr:2 * n_arr]
        send_sems, recv_sems = refs[2 * n_arr:]
        me = _me()
        layer = me[2]

        def ici(a, k, origin):
            return pltpu.make_async_remote_copy(
                src_ref=ins[a].at[layer], dst_ref=outs[a].at[layer, _chip_of(origin)], send_sem=send_sems.at[6 * a + k],
                recv_sem=recv_sems.at[6 * a + k], device_id=_flip(me, CHIP_FLIPS[k]), device_id_type=MESH)

        def d2d(a, k, lay):
            slab = outs[a].at[lay, _chip_of(_flip(me, CHIP_FLIPS[k]))]
            return pltpu.make_async_remote_copy(
                src_ref=slab, dst_ref=slab, send_sem=send_sems.at[6 * a + 3 + k], recv_sem=recv_sems.at[6 * a + 3 + k],
                device_id=_flip(me, SIBLING), device_id_type=MESH)

        sends = [ici(a, k, me) for a in range(n_arr) for k in range(3)]
        for cp in sends:
            cp.start()
        passed = []
        for a in range(n_arr):
            for k in range(3):
                ici(a, k, _flip(me, CHIP_FLIPS[k])).wait_recv()
                passed.append(d2d(a, k, layer))
                passed[-1].start()
        for a in range(n_arr):
            for k in range(3):
                d2d(a, k, 1 - layer).wait_recv()
        for cp in sends + passed:
            cp.wait_send()

    return pl.pallas_call(
        body, out_shape=[SDS((2, N_CHIPS) + s.shape[1:], s.dtype) for s in shards],
        in_specs=[HBM_SPEC] * n_arr, out_specs=[HBM_SPEC] * n_arr, scratch_shapes=_dma_sems(6 * n_arr), name=name)(*shards)


def _send_other_layer(g0, g1, name):
    n_arr = len(g0)

    def body(*refs):
        in0, in1, outs = refs[:n_arr], refs[n_arr:2 * n_arr], refs[2 * n_arr:3 * n_arr]
        send_sems, recv_sems = refs[3 * n_arr:]
        me = _me()

        def copy(a, src):
            return pltpu.make_async_remote_copy(
                src_ref=src, dst_ref=outs[a], send_sem=send_sems.at[a], recv_sem=recv_sems.at[a],
                device_id=_flip(me, SIBLING), device_id_type=MESH)

        @pl.when(me[2] == 0)
        def _():
            for a in range(n_arr):
                copy(a, in1[a]).start()

        @pl.when(me[2] == 1)
        def _():
            for a in range(n_arr):
                copy(a, in0[a]).start()

        for a in range(n_arr):
            copy(a, in0[a]).wait()

    return pl.pallas_call(
        body, out_shape=[SDS(g.shape, g.dtype) for g in g0], in_specs=[HBM_SPEC] * (2 * n_arr),
        out_specs=[HBM_SPEC] * n_arr, scratch_shapes=_dma_sems(n_arr), name=name)(*g0, *g1)


def _scatter_chips(ps, name):
    n_arr = len(ps)

    def body(*refs):
        ins, outs = refs[:n_arr], refs[n_arr:2 * n_arr]
        send_sems, recv_sems = refs[2 * n_arr:]
        me = _me()

        def copy(a, k):
            return pltpu.make_async_remote_copy(
                src_ref=ins[a].at[_chip_of(_flip(me, CHIP_FLIPS[k]))], dst_ref=outs[a].at[k],
                send_sem=send_sems.at[3 * a + k], recv_sem=recv_sems.at[3 * a + k],
                device_id=_flip(me, CHIP_FLIPS[k]), device_id_type=MESH)

        copies = [copy(a, k) for a in range(n_arr) for k in range(3)]
        for cp in copies:
            cp.start()
        for cp in copies:
            cp.wait_recv()
        for cp in copies:
            cp.wait_send()

    return pl.pallas_call(
        body, out_shape=[SDS((3,) + p.shape[1:], p.dtype) for p in ps], in_specs=[HBM_SPEC] * n_arr,
        out_specs=[HBM_SPEC] * n_arr, scratch_shapes=_dma_sems(3 * n_arr), name=name)(*ps)


def _swap_layers(rs, name):
    n_arr = len(rs)

    def body(*refs):
        ins, outs = refs[:n_arr], refs[n_arr:2 * n_arr]
        send_sems, recv_sems = refs[2 * n_arr:]
        me = _me()
        copies = [pltpu.make_async_remote_copy(
            src_ref=ins[a], dst_ref=outs[a], send_sem=send_sems.at[a], recv_sem=recv_sems.at[a],
            device_id=_flip(me, SIBLING), device_id_type=MESH) for a in range(n_arr)]
        for cp in copies:
            cp.start()
        for cp in copies:
            cp.wait()

    return pl.pallas_call(
        body, out_shape=[SDS(r.shape, r.dtype) for r in rs], in_specs=[HBM_SPEC] * n_arr, out_specs=[HBM_SPEC] * n_arr,
        scratch_shapes=_dma_sems(n_arr), name=name)(*rs)


def _add_sibling(g0, g1, got, layer, col_shards, name):
    k, n = g0.shape
    tr = _pick(k, (512, 256, 128))
    width = n // N_CHIPS if col_shards else n
    cw = _pick(width, (1024, 512, 256, 128))
    per = width // cw

    def body(l_ref, a0_ref, a1_ref, b_ref, o_ref):
        mine = jnp.where(l_ref[0] == 0, a0_ref[...], a1_ref[...])
        o_ref[...] = (mine.astype(F32) + b_ref[...].astype(F32)).astype(o_ref.dtype)

    blk = pl.BlockSpec((tr, cw), lambda i, j, l: (i, j))
    if col_shards:
        out_spec = pl.BlockSpec((None, tr, cw), lambda i, j, l: (j // per, i, j % per))
        out_shape = SDS((N_CHIPS, k, width), g0.dtype)
    else:
        out_spec, out_shape = blk, SDS((k, n), g0.dtype)
    grid_spec = pltpu.PrefetchScalarGridSpec(
        num_scalar_prefetch=1, grid=(k // tr, n // cw), in_specs=[blk, blk, blk], out_specs=out_spec)
    return pl.pallas_call(body, grid_spec=grid_spec, out_shape=out_shape, name=name,
                          compiler_params=_cp(("parallel", "parallel")))(layer, g0, g1, got)


def _sum_chips(p, got, chip, name):
    _, k, ns = p.shape
    tr = _pick(k, (256, 128, 64, 32, 16))

    def body(c_ref, a_ref, b_ref, o_ref):
        acc = a_ref[0].astype(F32)
        for j in range(3):
            acc = acc + b_ref[j].astype(F32)
        o_ref[...] = acc

    grid_spec = pltpu.PrefetchScalarGridSpec(
        num_scalar_prefetch=1, grid=(k // tr,),
        in_specs=[pl.BlockSpec((1, tr, ns), lambda i, c: (c[0], i, 0)), pl.BlockSpec((3, tr, ns), lambda i, c: (0, i, 0))],
        out_specs=pl.BlockSpec((tr, ns), lambda i, c: (i, 0)))
    return pl.pallas_call(
        body, grid_spec=grid_spec, out_shape=SDS((k, ns), F32), name=name, compiler_params=_cp(("parallel",)))(chip, p, got)


SEM_SPEC = pl.BlockSpec(memory_space=pltpu.SEMAPHORE)
HBM_ONLY = pl.BlockSpec(memory_space=pltpu.HBM)
DATAFLOW = pltpu.SideEffectType.DATAFLOW_SIDE_EFFECTING


def _in_hbm(a):
    return pltpu.with_memory_space_constraint(a, pltpu.HBM)


def _copies_start(name, srcs, lands, plan, n_copies):
    ns, nl = len(srcs), len(lands)

    def body(*refs):
        src_refs, land_refs = refs[:ns], refs[ns:ns + nl]
        send_sems, recv_sems = refs[ns + nl], refs[ns + nl + 1]
        token = refs[-1]
        for i, (src, dst, peer) in enumerate(plan(_me(), src_refs, land_refs)):
            pltpu.make_async_remote_copy(src_ref=src, dst_ref=dst, send_sem=send_sems.at[i], recv_sem=recv_sems.at[i],
                                         device_id=peer, device_id_type=MESH).start()
        token[...] = jnp.zeros_like(token)

    outs = pl.pallas_call(
        body, name=name,
        out_shape=(pltpu.SemaphoreType.DMA((n_copies,)), pltpu.SemaphoreType.DMA((n_copies,)),
                   *[pltpu.HBM(l.shape, l.dtype) for l in lands], SDS((8, LANES), F32)),
        in_specs=[HBM_ONLY] * (ns + nl), out_specs=(SEM_SPEC, SEM_SPEC, *[HBM_ONLY] * nl, VMEM_SPEC),
        input_output_aliases={ns + i: 2 + i for i in range(nl)},
        compiler_params=pltpu.CompilerParams(has_side_effects=DATAFLOW),
    )(*[_in_hbm(s) for s in srcs], *[_in_hbm(l) for l in lands])
    return outs[0], outs[1], list(outs[2:2 + nl]), outs[-1]


def _copies_wait(name, srcs, lands, send_sems, recv_sems, plan, after):
    ns, nl = len(srcs), len(lands)

    def body(*refs):
        src_refs, land_refs = refs[:ns], refs[ns:ns + nl]
        send_ref, recv_ref = refs[ns + nl], refs[ns + nl + 1]
        for i, (src, dst, peer) in enumerate(plan(_me(), src_refs, land_refs)):
            cp = pltpu.make_async_remote_copy(src_ref=src, dst_ref=dst, send_sem=send_ref.at[i], recv_sem=recv_ref.at[i],
                                              device_id=peer, device_id_type=MESH)
            cp.wait_send()
            cp.wait_recv()

    outs = pl.pallas_call(
        body, name=name, out_shape=[pltpu.HBM(l.shape, l.dtype) for l in lands],
        in_specs=[HBM_ONLY] * (ns + nl) + [SEM_SPEC, SEM_SPEC, HBM_SPEC], out_specs=[HBM_ONLY] * nl,
        input_output_aliases={ns + i: i for i in range(nl)},
        compiler_params=pltpu.CompilerParams(has_side_effects=DATAFLOW),
    )(*[_in_hbm(s) for s in srcs], *lands, send_sems, recv_sems, after)
    return list(outs)


def _half(ref, rows, axis):
    idx = [slice(None)] * axis + [rows]
    return ref.at[tuple(idx)]


def _gather_plans(layer, halves):
    def ici(me, srcs, lands):
        out = []
        for a, kh in enumerate(halves):
            rows = pl.ds(pl.multiple_of(me[2] * kh, 16), kh)
            for k in range(3):
                out.append((srcs[a].at[layer, rows], lands[a].at[_chip_of(me), rows], _flip(me, CHIP_FLIPS[k])))
        return out

    def d2d(me, srcs, lands):
        out = []
        for a, kh in enumerate(halves):
            rows = pl.ds(pl.multiple_of(me[2] * kh, 16), kh)
            for k in range(3):
                slab = lands[a].at[_chip_of(_flip(me, CHIP_FLIPS[k])), rows]
                out.append((slab, slab, _flip(me, SIBLING)))
        return out

    return ici, d2d


def _to_sibling_plan(halves, axes):
    def plan(me, srcs, lands):
        out = []
        for a, (kh, axis) in enumerate(zip(halves, axes)):
            rows = pl.ds(pl.multiple_of((1 - me[2]) * kh, 16), kh)
            out.append((_half(srcs[a], rows, axis), lands[a], _flip(me, SIBLING)))
        return out

    return plan


def _to_chips_plan(n_arr):
    def plan(me, srcs, lands):
        out = []
        for a in range(n_arr):
            for k in range(3):
                peer = _flip(me, CHIP_FLIPS[k])
                out.append((srcs[a].at[_chip_of(peer)], lands[a].at[k], peer))
        return out

    return plan


def _swap_plan(n_arr):
    def plan(me, srcs, lands):
        return [(srcs[a], lands[a], _flip(me, SIBLING)) for a in range(n_arr)]

    return plan


def _add_half(g, got, half, col_shards, name):
    s, kh, n = got.shape
    tr = _pick(kh, (512, 256, 128, 64, 32, 16))
    nrb = kh // tr
    width = n // N_CHIPS if col_shards else n
    cw = _pick(width, (1024, 512, 256, 128))
    per = width // cw

    def body(h_ref, a_ref, b_ref, o_ref):
        o_ref[...] = (a_ref[...].astype(F32) + b_ref[...].astype(F32)).astype(o_ref.dtype)

    in_specs = [pl.BlockSpec((None, tr, cw), lambda j, i, c, h: (j, h[0] * nrb + i, c)),
                pl.BlockSpec((None, tr, cw), lambda j, i, c, h: (j, i, c))]
    if col_shards:
        assert s == 1
        out_spec = pl.BlockSpec((None, tr, cw), lambda j, i, c, h: (c // per, i, c % per))
        out_shape = SDS((N_CHIPS, kh, width), g.dtype)
    else:
        out_spec, out_shape = in_specs[1], SDS((s, kh, n), g.dtype)
    grid_spec = pltpu.PrefetchScalarGridSpec(num_scalar_prefetch=1, grid=(s, nrb, n // cw), in_specs=in_specs,
                                             out_specs=out_spec)
    return pl.pallas_call(body, grid_spec=grid_spec, out_shape=out_shape, name=name,
                          compiler_params=_cp(("parallel", "parallel", "parallel")))(half, g, got)


def _sum_devices(g, name):
    _, _, n = g.shape

    def body(g_ref, o_ref):
        acc = g_ref[0]
        for k in range(1, N_DEV):
            acc = acc + g_ref[k]
        o_ref[...] = acc

    return pl.pallas_call(body, out_shape=SDS((1, n), F32), in_specs=[VMEM_SPEC], out_specs=VMEM_SPEC, name=name)(g)


def _silu_rows(c, name):
    def body(c_ref, o_ref):
        v = c_ref[...]
        o_ref[...] = v * _sigmoid(v)

    return pl.pallas_call(body, out_shape=SDS(c.shape, F32), in_specs=[VMEM_SPEC], out_specs=VMEM_SPEC, name=name)(c)


BIG = (("w_in", 2), ("w_uq", 2), ("w_ukv", 2), ("w_o_mla", 2), ("w_o_gdn", 2), ("w_o", 1), ("w_gate_up", 2), ("w_down", 1))
KERNEL_BIG = ("w_main", "w_small", "w_uq", "w_ukv", "w_o_mla", "w_o_gdn", "w_o", "w_gate_up", "w_down")
COL_SHARDED_AS_IS = ("w_o_mla", "w_o_gdn", "w_gate_up")
ROW_SHARDED = ("w_o", "w_down")
FIRST_NEEDED = ("w_in", "w_uq", "w_ukv")
LATER_NEEDED = ("w_o_mla", "w_o_gdn", "w_o", "w_gate_up", "w_down")
FFN_GRADS = ("w_gate_up", "w_down")
MIXER_GRADS = ("w_in", "w_uq", "w_ukv", "w_o_mla", "w_o_gdn", "w_o")
MIXER_GRADS_KERNEL = ("w_main", "w_small", "w_uq", "w_ukv", "w_o_mla", "w_o_gdn", "w_o")
SMALL = ("norm_mix", "norm_ffn", "q_a_norm", "kv_a_norm", "A_log", "dt_bias", "gdn_norm")
WEIGHTS = ("w_ada", "b_ada", "norm_mix", "norm_ffn", "w_in", "q_a_norm", "kv_a_norm", "w_uq", "w_ukv", "w_o_mla", "conv_w",
           "A_log", "dt_bias", "gdn_norm", "w_o_gdn", "w_o", "w_gate_up", "w_down", "final_norm")
ADA_PAD = 16
K_PAD = 128


def _pad_to(a, n, axis):
    pad = [(0, 0)] * a.ndim
    pad[axis] = (0, n - a.shape[axis])
    return jnp.pad(a, pad)


def kernel(x, c, positions, w_ada, b_ada, norm_mix, norm_ffn, w_in, q_a_norm, kv_a_norm, w_uq, w_ukv, w_o_mla, conv_w, A_log, dt_bias, gdn_norm, w_o_gdn, w_o, w_gate_up, w_down, final_norm, loss_target, m_w_ada, m_b_ada, m_norm_mix, m_norm_ffn, m_w_in, m_q_a_norm, m_kv_a_norm, m_w_uq, m_w_ukv, m_w_o_mla, m_conv_w, m_A_log, m_dt_bias, m_gdn_norm, m_w_o_gdn, m_w_o, m_w_gate_up, m_w_down, m_final_norm, v_w_ada, v_b_ada, v_norm_mix, v_norm_ffn, v_w_in, v_q_a_norm, v_kv_a_norm, v_w_uq, v_w_ukv, v_w_o_mla, v_conv_w, v_A_log, v_dt_bias, v_gdn_norm, v_w_o_gdn, v_w_o, v_w_gate_up, v_w_down, v_final_norm):
    env = dict(locals())
    w = {n: env[n] for n in WEIGHTS}
    depth, d = norm_mix.shape
    t = x.shape[1]
    me = _me()
    chip = _chip_of(me)
    dev = 4 * me[0] + 2 * me[1] + me[2]
    ada_cols = w_ada.shape[2]

    half_idx = me[2].astype(jnp.int32).reshape(1)
    chip_idx = chip.astype(jnp.int32).reshape(1)
    w16 = {n: w[n].astype(BF16) for n, _ in BIG}
    shard_axis = dict(BIG)
    gather = {}

    def start_group(key, layer, names, dep):
        srcs = [w16[n] for n in names]
        plans = _gather_plans(layer, [a.shape[1] // 2 for a in srcs])
        landing = [lax.empty((N_CHIPS,) + a.shape[1:], BF16) for a in srcs]
        send_s, recv_s, landing, tok = _copies_start(f"gather_{key}_ici_start", srcs + [dep], landing, plans[0], 3 * len(names))
        gather[key] = dict(layer=layer, names=names, srcs=srcs, plans=plans, ici=(send_s, recv_s, landing), tok=tok)
        return tok[0, 0]

    def pass_to_sibling(key, after):
        st = gather[key]
        send_s, recv_s, landing = st["ici"]
        landing = _copies_wait(f"gather_{key}_ici_wait", st["srcs"] + [st["tok"]], landing, send_s, recv_s, st["plans"][0],
                               st["tok"] if after is None else after)
        st["d2d"] = _copies_start(f"gather_{key}_d2d_start", [], landing, st["plans"][1], 3 * len(st["names"]))
        return st["d2d"][3]

    def gathered(key):
        st = gather[key]
        send_s, recv_s, landing, tok = st["d2d"]
        landing = _copies_wait(f"gather_{key}_d2d_wait", [], landing, send_s, recv_s, st["plans"][1], tok)
        return {n: jnp.concatenate([jnp.where(chip == j, own[st["layer"]], got[j]) for j in range(N_CHIPS)],
                                   axis=shard_axis[n] - 1)
                for n, own, got in zip(st["names"], st["srcs"], landing)}

    full = {}
    conv_all = _all_gather_small(conv_w.reshape(1, -1), "gather_conv").reshape((N_DEV,) + conv_w.shape)
    full["conv_w"] = jnp.concatenate([conv_all[2 * j] for j in range(N_CHIPS)], axis=2)
    for n in SMALL:
        full[n] = w[n]

    c_all = _all_gather_small(c, "gather_c").reshape(N_DEV, d)
    c_act = _silu_rows(_pad_to(c_all, ADA_PAD, 0), "silu_c")
    b_cols = lax.dynamic_slice_in_dim(b_ada, chip * ada_cols, ada_cols, axis=1)
    mod_cols = jnp.stack([
        _mm(c_act, w_ada[l], acc_in=jnp.broadcast_to(b_cols[l][None], (ADA_PAD, ada_cols)), name=f"ada_l{l}")[:N_DEV]
        for l in range(depth)])
    mod_all = _all_gather_small(mod_cols.reshape(depth * N_DEV, ada_cols), "gather_mod")
    mod_all = mod_all.reshape(N_DEV, depth, N_DEV, ada_cols)
    mods = []
    for l in range(depth):
        mine = jnp.concatenate([lax.dynamic_index_in_dim(mod_all[2 * j, l], dev, axis=0, keepdims=True)
                                for j in range(N_CHIPS)], axis=1)
        mods.append(tuple(mine[:, i * d:(i + 1) * d] for i in range(6)))

    tabs = _rope_tables(positions[0])
    start_group("l0a", 0, FIRST_NEEDED, mods[depth - 1][5])
    tie = start_group("l0b", 0, LATER_NEEDED, pass_to_sibling("l0a", None))

    def late_weights(key, next_key, next_layer, behind):
        tok = pass_to_sibling(key, behind)
        started = 0.0 if next_key is None else start_group(next_key, next_layer, FIRST_NEEDED, tok)
        return gathered(key), started

    def next_later_group(behind):
        return start_group("l1b", 1, LATER_NEEDED, pass_to_sibling("l1a", behind))

    wts, saved = [None] * depth, [None] * depth
    tied = (mods[0][0] + tie,) + mods[0][1:]
    h, saved[0], wts[0] = _layer_fwd(x[0], tied, _layer_weights(gathered("l0a"), full, 0), tabs, "l0",
                                     functools.partial(late_weights, "l0b", "l1a", 1), next_later_group)
    h, saved[1], wts[1] = _layer_fwd(h, mods[1], _layer_weights(gathered("l1a"), full, 1), tabs, "l1",
                                     functools.partial(late_weights, "l1b", None, None))
    loss_part, dh, dfn = _loss_head(h, final_norm[None], loss_target[0], "loss_head")
    dfn = dfn[0]

    def col_shards(g):
        return g.reshape(g.shape[0], N_CHIPS, g.shape[1] // N_CHIPS).transpose(1, 0, 2)

    def reduce_scatter_stages(tag, g, knames, names):
        srcs = [g[n].reshape(N_CHIPS, -1, g[n].shape[1]) if n in ROW_SHARDED else g[n] for n in knames]
        axes = [1 if n in ROW_SHARDED else 0 for n in knames]
        halves = [a.shape[ax] // 2 for a, ax in zip(srcs, axes)]
        got_shapes = [a.shape[:ax] + (kh,) + a.shape[ax + 1:] for a, ax, kh in zip(srcs, axes, halves)]
        plan_a, plan_c, plan_e = _to_sibling_plan(halves, axes), _to_chips_plan(len(names)), _swap_plan(len(names))
        st, out = {}, {}
        st["a"] = _copies_start(f"{tag}_sibling_start", srcs, [lax.empty(sh, BF16) for sh in got_shapes], plan_a, len(srcs))

        def after_or(tok, after):
            return tok if after is None else after

        def stage0(after):
            send_s, recv_s, landing, tok = st["a"]
            got = _copies_wait(f"{tag}_sibling_wait", srcs, landing, send_s, recv_s, plan_a, after_or(tok, after))
            sums = {}
            for n, a, b in zip(knames, srcs, got):
                a3, b3 = (v if v.ndim == 3 else v[None] for v in (a, b))
                r = _add_half(a3, b3, half_idx, n in COL_SHARDED_AS_IS, f"{tag}_add_{n}")
                sums[n] = r if (n in COL_SHARDED_AS_IS or n in ROW_SHARDED) else r[0]
            if "w_main" in sums:
                sums["w_in"] = col_shards(_unlay_w_in(sums["w_main"], sums["w_small"]))
                sums["w_uq"] = col_shards(_unlay_w_uq(sums["w_uq"]))
                sums["w_ukv"] = col_shards(_unlay_w_ukv(sums["w_ukv"]))
            st["p"] = [sums[n] for n in names]
            st["c"] = _copies_start(f"{tag}_chips_start", st["p"], [lax.empty((3,) + p.shape[1:], BF16) for p in st["p"]],
                                    plan_c, 3 * len(names))
            return st["c"][3][0, 0]

        def stage1(after):
            send_s, recv_s, landing, tok = st["c"]
            got = _copies_wait(f"{tag}_chips_wait", st["p"], landing, send_s, recv_s, plan_c, after_or(tok, after))
            st["r"] = [_sum_chips(p, q, chip_idx, f"{tag}_sum_{n}") for n, p, q in zip(names, st["p"], got)]
            st["e"] = _copies_start(f"{tag}_swap_start", st["r"], [lax.empty(r.shape, F32) for r in st["r"]], plan_e, len(names))
            return st["e"][3][0, 0]

        def stage2(after):
            send_s, recv_s, landing, tok = st["e"]
            got = _copies_wait(f"{tag}_swap_wait", st["r"], landing, send_s, recv_s, plan_e, after_or(tok, after))
            for n, mine, theirs in zip(names, st["r"], got):
                out[n] = jnp.where(me[2] == 0, jnp.concatenate([mine, theirs]), jnp.concatenate([theirs, mine]))

        return (stage0, stage1, stage2), out, st["a"][3][0, 0]

    dmods, grads, groups = [None] * depth, [None] * depth, {}

    def ffn_group_l1(g):
        groups["l1_ffn"] = reduce_scatter_stages("rs_l1_ffn", g, FFN_GRADS, FFN_GRADS)
        return groups["l1_ffn"][2]

    dh, dmods[1], grads[1] = _layer_bwd(dh, saved[1], mods[1], wts[1], tabs, "l1", after_ffn=ffn_group_l1)
    groups["l1_mix"] = reduce_scatter_stages("rs_l1_mix", grads[1], MIXER_GRADS_KERNEL, MIXER_GRADS)
    tied = mods[0][:5] + (mods[0][5] + groups["l1_mix"][2],)

    def ffn_group_l0(g):
        behind = g["w_gate_up"]
        tok = groups["l1_ffn"][0][0](behind) + groups["l1_mix"][0][0](behind)
        groups["l0_ffn"] = reduce_scatter_stages("rs_l0_ffn", g, FFN_GRADS, FFN_GRADS)
        return tok + groups["l0_ffn"][2]

    def after_gdn_l0(behind):
        return groups["l0_ffn"][0][0](behind)

    dx, dmods[0], grads[0] = _layer_bwd(dh, saved[0], tied, wts[0], tabs, "l0", after_ffn=ffn_group_l0, after_gdn=after_gdn_l0)
    groups["l0_mix"] = reduce_scatter_stages("rs_l0_mix", grads[0], MIXER_GRADS_KERNEL, MIXER_GRADS)
    for key in ("l1_ffn", "l1_mix", "l0_ffn"):
        groups[key][0][1](dx)
    g_out, deltas, new_m, new_v = {}, {}, {}, {}

    def reduced(names):
        for n in names:
            g_out[n] = jnp.stack([groups[f"l{l}_ffn" if n in FFN_GRADS else f"l{l}_mix"][1][n] for l in range(depth)])

    def update(names):
        for n in names:
            deltas[n], new_m[n], new_v[n] = _adamw(w[n], g_out[n], env["m_" + n], env["v_" + n], f"adamw_{n}")

    small = [_small_grads_ref_layout(grads[l]) for l in range(depth)]
    small_parts = [jnp.concatenate(dmods[l], axis=1).reshape(-1) for l in range(depth)]
    small_parts += [jnp.stack([small[l][n] for l in range(depth)]).reshape(-1) for n in SMALL]
    small_parts += [dfn, loss_part[0, :1]]
    small_sizes = [p.shape[0] for p in small_parts]
    packed = jnp.concatenate(small_parts)
    n_small = -(-packed.shape[0] // LANES) * LANES
    small_all = _all_gather_small(_pad_to(packed, n_small, 0).reshape(1, n_small), "gather_small_grads")
    small_sum = _sum_devices(small_all, "sum_small_grads")[0]
    offs = [0]
    for sz in small_sizes:
        offs.append(offs[-1] + sz)
    g_out["b_ada"] = jnp.stack([small_sum[offs[l]:offs[l + 1]] for l in range(depth)])
    for i, n in enumerate(SMALL):
        g_out[n] = small_sum[offs[depth + i]:offs[depth + i + 1]].reshape(w[n].shape)
    g_out["final_norm"] = small_sum[offs[depth + len(SMALL)]:offs[depth + len(SMALL) + 1]]
    loss = small_sum[offs[depth + len(SMALL) + 1]]

    c_act_t = _pad_to(c_act[:N_DEV].T, K_PAD, 1)
    g_ada = []
    for l in range(depth):
        dmod_l = small_all[:, 0, offs[l]:offs[l + 1]]
        dmod_cols = lax.dynamic_slice_in_dim(dmod_l, chip * ada_cols, ada_cols, axis=1)
        g_ada.append(_mm(c_act_t, _pad_to(dmod_cols, K_PAD, 0), name=f"ada_grad_l{l}"))
    g_out["w_ada"] = jnp.stack(g_ada)

    conv_g = jnp.stack([small[l]["conv_w"] for l in range(depth)])
    conv_all_g = _all_gather_small(conv_g.reshape(1, -1), "gather_conv_grads")
    conv_sum = _sum_devices(conv_all_g, "sum_conv_grads").reshape(conv_g.shape)
    n_cc = conv_w.shape[2]
    g_out["conv_w"] = lax.dynamic_slice_in_dim(conv_sum, chip * n_cc, n_cc, axis=2)

    mix0 = groups["l0_mix"][0]
    mix0[0](conv_sum.reshape(-1)[:LANES] + small_sum[:LANES])
    for key in ("l1_ffn", "l1_mix", "l0_ffn"):
        groups[key][0][2](None)
    reduced(FFN_GRADS)
    update(("w_ada", "b_ada", "final_norm", "conv_w") + SMALL + FFN_GRADS)
    mix0[1](deltas["w_down"])
    mix0[2](None)
    reduced(MIXER_GRADS)
    update(MIXER_GRADS)
    return (loss, dx[None], *[g_out[n] for n in WEIGHTS], *[deltas[n] for n in WEIGHTS],
            *[new_m[n] for n in WEIGHTS], *[new_v[n] for n in WEIGHTS])
```

```python
import functools
import math

import jax
import jax.numpy as jnp
from jax import lax
from jax.experimental import pallas as pl
from jax.experimental.pallas import tpu as pltpu

F32 = jnp.float32
BF16 = jnp.bfloat16
SDS = jax.ShapeDtypeStruct
MESH = pl.DeviceIdType.MESH
AXES = ("x", "y", "c")

EPS = 1e-6
HEADS = 8
D_NOPE = 128
D_ROPE = 64
D_QK = 256
D_V = 128
Q_LORA = 512
KV_LORA = 512
CHUNK = 64
CONV_WIDTH = 4
GDN_W = HEADS * D_V
MAIN_FIXED = 2 * Q_LORA + 4 * GDN_W
LANES = 128
VMEM_LIMIT = 56 * 1024 * 1024
ADAM_LR, ADAM_B1, ADAM_B2, ADAM_EPS, ADAM_WD, ADAM_STEP = 0.001, 0.9, 0.999, 1e-8, 0.01, 10


def _pick(n, cands):
    for cand in cands:
        if n % cand == 0:
            return cand
    return n


def _cp(sem):
    return pltpu.CompilerParams(dimension_semantics=sem, vmem_limit_bytes=VMEM_LIMIT)


def _row_tile(t):
    return _pick(t, (256, 128, 64, 32, 16, 8))


def _dot(a, b):
    return jnp.dot(a.astype(BF16), b.astype(BF16), preferred_element_type=F32)


def _dot_nt(a, b):
    return lax.dot_general(a.astype(BF16), b.astype(BF16), (((1,), (1,)), ((), ())), preferred_element_type=F32)


def _dot_tn(a, b):
    return lax.dot_general(a.astype(BF16), b.astype(BF16), (((0,), (0,)), ((), ())), preferred_element_type=F32)


def _bdot(a, b, exact=False):
    if exact:
        return lax.dot_general(a, b, (((2,), (1,)), ((0,), (0,))), precision=lax.Precision.HIGHEST,
                               preferred_element_type=F32)
    return lax.dot_general(a.astype(BF16), b.astype(BF16), (((2,), (1,)), ((0,), (0,))), preferred_element_type=F32)


def _bdot_nt(a, b):
    return lax.dot_general(a.astype(BF16), b.astype(BF16), (((2,), (2,)), ((0,), (0,))), preferred_element_type=F32)


def _sigmoid(x):
    return 1.0 / (1.0 + jnp.exp(-x))


def _mm(a, b, *, tb=False, out_dtype=F32, acc_in=None, name):
    m, k = a.shape
    n = b.shape[0] if tb else b.shape[1]
    assert (b.shape[1] if tb else b.shape[0]) == k
    tm = _pick(m, (1024, 512, 256, 128))
    tn = _pick(n, (1024, 512, 256, 128))
    tk = k if k <= 2048 else _pick(k, (512, 256, 128))
    nk = k // tk
    has_acc = acc_in is not None

    def body_one_step(*refs):
        a_ref, b_ref = refs[:2]
        o_ref = refs[-1]
        acc = _dot_nt(a_ref[...], b_ref[...]) if tb else _dot(a_ref[...], b_ref[...])
        if has_acc:
            acc = acc + refs[2][...].astype(F32)
        o_ref[...] = acc.astype(out_dtype)

    if nk == 1:
        in_specs = [pl.BlockSpec((tm, k), lambda i, j: (i, 0)),
                    pl.BlockSpec((tn, k), lambda i, j: (j, 0)) if tb else pl.BlockSpec((k, tn), lambda i, j: (0, j))]
        args = [a, b]
        if has_acc:
            in_specs.append(pl.BlockSpec((tm, tn), lambda i, j: (i, j)))
            args.append(acc_in)
        return pl.pallas_call(
            body_one_step, grid=(m // tm, n // tn), in_specs=in_specs, out_specs=pl.BlockSpec((tm, tn), lambda i, j: (i, j)),
            out_shape=SDS((m, n), out_dtype), name=name, compiler_params=_cp(("parallel", "parallel")))(*args)

    def body(*refs):
        if has_acc:
            a_ref, b_ref, c_ref, o_ref, acc = refs
        else:
            a_ref, b_ref, o_ref, acc = refs
        kk = pl.program_id(2)

        @pl.when(kk == 0)
        def _():
            if has_acc:
                acc[...] = c_ref[...].astype(F32)
            else:
                acc[...] = jnp.zeros_like(acc)

        if tb:
            acc[...] += _dot_nt(a_ref[...], b_ref[...])
        else:
            acc[...] += _dot(a_ref[...], b_ref[...])

        @pl.when(kk == nk - 1)
        def _():
            o_ref[...] = acc[...].astype(out_dtype)

    in_specs = [pl.BlockSpec((tm, tk), lambda i, j, kk: (i, kk)),
                pl.BlockSpec((tn, tk), lambda i, j, kk: (j, kk)) if tb
                else pl.BlockSpec((tk, tn), lambda i, j, kk: (kk, j))]
    args = [a, b]
    if has_acc:
        in_specs.append(pl.BlockSpec((tm, tn), lambda i, j, kk: (i, j)))
        args.append(acc_in)
    return pl.pallas_call(
        body, grid=(m // tm, n // tn, nk), in_specs=in_specs,
        out_specs=pl.BlockSpec((tm, tn), lambda i, j, kk: (i, j)),
        out_shape=SDS((m, n), out_dtype), scratch_shapes=[pltpu.VMEM((tm, tn), F32)],
        name=name, compiler_params=_cp(("parallel", "parallel", "arbitrary")))(*args)


def _norm_fwd(x, col, width, w, sc, sh, name):
    t = x.shape[0]
    tm = _row_tile(t)

    def body(x_ref, w_ref, sc_ref, sh_ref, o_ref):
        xv = x_ref[...]
        r = lax.rsqrt(jnp.mean(xv * xv, axis=-1, keepdims=True) + EPS)
        n = xv * r * w_ref[...]
        o_ref[...] = (n * (1.0 + sc_ref[...]) + sh_ref[...]).astype(o_ref.dtype)

    row = pl.BlockSpec((1, width), lambda i: (0, 0))
    return pl.pallas_call(
        body, grid=(t // tm,), in_specs=[pl.BlockSpec((tm, width), lambda i: (i, col)), row, row, row],
        out_specs=pl.BlockSpec((tm, width), lambda i: (i, 0)), out_shape=SDS((t, width), BF16),
        name=name, compiler_params=_cp(("parallel",)))(x, w, sc, sh)


def _norm_bwd(x, col, width, w, sc, dh, dres, out_dtype, name):
    t = x.shape[0]
    tm = _row_tile(t)
    has_res = dres is not None

    def body(*refs):
        if has_res:
            x_ref, w_ref, sc_ref, dh_ref, dres_ref, dx_ref, dw_ref, dsc_ref, dsh_ref = refs
        else:
            x_ref, w_ref, sc_ref, dh_ref, dx_ref, dw_ref, dsc_ref, dsh_ref = refs

        @pl.when(pl.program_id(0) == 0)
        def _():
            dw_ref[...] = jnp.zeros_like(dw_ref)
            dsc_ref[...] = jnp.zeros_like(dsc_ref)
            dsh_ref[...] = jnp.zeros_like(dsh_ref)

        xv = x_ref[...]
        dhv = dh_ref[...].astype(F32)
        wv = w_ref[...]
        r = lax.rsqrt(jnp.mean(xv * xv, axis=-1, keepdims=True) + EPS)
        xh = xv * r
        n = xh * wv
        dsh_ref[...] += jnp.sum(dhv, axis=0, keepdims=True)
        dsc_ref[...] += jnp.sum(dhv * n, axis=0, keepdims=True)
        dn = dhv * (1.0 + sc_ref[...])
        dw_ref[...] += jnp.sum(dn * xh, axis=0, keepdims=True)
        dxh = dn * wv
        dx = r * (dxh - xh * jnp.mean(dxh * xh, axis=-1, keepdims=True))
        if has_res:
            dx = dx + dres_ref[...]
        dx_ref[...] = dx.astype(out_dtype)

    row = pl.BlockSpec((1, width), lambda i: (0, 0))
    blk = pl.BlockSpec((tm, width), lambda i: (i, 0))
    in_specs = [pl.BlockSpec((tm, width), lambda i: (i, col)), row, row, blk]
    args = [x, w, sc, dh]
    if has_res:
        in_specs.append(blk)
        args.append(dres)
    return pl.pallas_call(
        body, grid=(t // tm,), in_specs=in_specs, out_specs=[blk, row, row, row],
        out_shape=[SDS((t, width), out_dtype), SDS((1, width), F32), SDS((1, width), F32), SDS((1, width), F32)],
        name=name, compiler_params=_cp(("arbitrary",)))(*args)


def _rope128(x, tc, ts1, ts2):
    return x * tc + pltpu.roll(x, 96, 1) * ts1 + pltpu.roll(x, 32, 1) * ts2


def _rope128_t(d, tc, ts1, ts2):
    return d * tc + pltpu.roll(d * ts1, 32, 1) + pltpu.roll(d * ts2, 96, 1)


def _rope_q(q_raw, tc, ts1, ts2, transpose, out_dtype, name):
    t = q_raw.shape[0]
    tm = _row_tile(t)

    def body(q_ref, tc_ref, s1_ref, s2_ref, o_ref):
        fn = _rope128_t if transpose else _rope128
        for h in range(HEADS):
            base = h * D_QK
            o_ref[:, base:base + LANES] = q_ref[:, base:base + LANES].astype(out_dtype)
            x = q_ref[:, base + LANES:base + D_QK].astype(F32)
            o_ref[:, base + LANES:base + D_QK] = fn(x, tc_ref[...], s1_ref[...], s2_ref[...]).astype(out_dtype)

    blk = pl.BlockSpec((tm, HEADS * D_QK), lambda i: (i, 0))
    tab = pl.BlockSpec((tm, LANES), lambda i: (i, 0))
    return pl.pallas_call(
        body, grid=(t // tm,), in_specs=[blk, tab, tab, tab], out_specs=blk,
        out_shape=SDS((t, HEADS * D_QK), out_dtype), name=name, compiler_params=_cp(("parallel",)))(q_raw, tc, ts1, ts2)


def _k_assemble(kv_raw, p_small, tc, ts1, ts2, name):
    t = kv_raw.shape[0]
    tm = _row_tile(t)

    def body(kn_ref, ps_ref, tc_ref, s1_ref, s2_ref, o_ref):
        kpe = _rope128(ps_ref[...], tc_ref[...], s1_ref[...], s2_ref[...]).astype(BF16)
        for h in range(HEADS):
            o_ref[:, h * D_QK:h * D_QK + LANES] = kn_ref[:, h * LANES:(h + 1) * LANES].astype(BF16)
            o_ref[:, h * D_QK + LANES:(h + 1) * D_QK] = kpe

    tab = pl.BlockSpec((tm, LANES), lambda i: (i, 0))
    return pl.pallas_call(
        body, grid=(t // tm,),
        in_specs=[pl.BlockSpec((tm, HEADS * LANES), lambda i: (i, 0)), tab, tab, tab, tab],
        out_specs=pl.BlockSpec((tm, HEADS * D_QK), lambda i: (i, 0)),
        out_shape=SDS((t, HEADS * D_QK), BF16), name=name, compiler_params=_cp(("parallel",)))(kv_raw, p_small, tc, ts1, ts2)


def _k_assemble_bwd(dk, dv, tc, ts1, ts2, name):
    t = dk.shape[0]
    tm = _row_tile(t)

    def body(dk_ref, dv_ref, tc_ref, s1_ref, s2_ref, o_ref, pe_ref):
        acc = jnp.zeros((tm, LANES), F32)
        for h in range(HEADS):
            o_ref[:, h * LANES:(h + 1) * LANES] = dk_ref[:, h * D_QK:h * D_QK + LANES].astype(BF16)
            acc = acc + dk_ref[:, h * D_QK + LANES:(h + 1) * D_QK].astype(F32)
        o_ref[:, HEADS * LANES:] = dv_ref[...].astype(BF16)
        pe_ref[...] = _rope128_t(acc, tc_ref[...], s1_ref[...], s2_ref[...])

    tab = pl.BlockSpec((tm, LANES), lambda i: (i, 0))
    return pl.pallas_call(
        body, grid=(t // tm,),
        in_specs=[pl.BlockSpec((tm, HEADS * D_QK), lambda i: (i, 0)), pl.BlockSpec((tm, HEADS * LANES), lambda i: (i, 0)),
                  tab, tab, tab],
        out_specs=[pl.BlockSpec((tm, 2 * HEADS * LANES), lambda i: (i, 0)), tab],
        out_shape=[SDS((t, 2 * HEADS * LANES), BF16), SDS((t, LANES), F32)],
        name=name, compiler_params=_cp(("parallel",)))(dk, dv, tc, ts1, ts2)


def _attn_tile(t):
    return _pick(t, (256, 128, 64))


def _attn_fwd(q, k, v, v_off, name):
    t = q.shape[0]
    tq = _attn_tile(t)
    scale = (D_NOPE + D_ROPE) ** -0.5

    def body(q_ref, k_ref, v_ref, o_ref, lse_ref):
        for i in range(t // tq):
            n_k = (i + 1) * tq
            s = _dot_nt(q_ref[i * tq:(i + 1) * tq, :], k_ref[0:n_k, :]) * scale
            row = lax.broadcasted_iota(jnp.int32, (tq, n_k), 0) + i * tq
            colv = lax.broadcasted_iota(jnp.int32, (tq, n_k), 1)
            s = jnp.where(colv <= row, s, -jnp.inf)
            m = jnp.max(s, axis=-1, keepdims=True)
            p = jnp.exp(s - m)
            l = jnp.sum(p, axis=-1, keepdims=True)
            o = _dot(p, v_ref[0:n_k, :]) / l
            o_ref[i * tq:(i + 1) * tq, :] = o.astype(BF16)
            lse_ref[0, i * tq:(i + 1) * tq, :] = m + jnp.log(l)

    return pl.pallas_call(
        body, grid=(HEADS,),
        in_specs=[pl.BlockSpec((t, D_QK), lambda h: (0, h)), pl.BlockSpec((t, D_QK), lambda h: (0, h)),
                  pl.BlockSpec((t, D_V), lambda h: (0, v_off + h))],
        out_specs=[pl.BlockSpec((t, D_V), lambda h: (0, h)), pl.BlockSpec((1, t, 1), lambda h: (h, 0, 0))],
        out_shape=[SDS((t, HEADS * D_V), BF16), SDS((HEADS, t, 1), F32)],
        name=name, compiler_params=_cp(("parallel",)))(q, k, v)


def _attn_bwd(q, k, v, v_off, o, lse, do, name):
    t = q.shape[0]
    tq = _attn_tile(t)
    scale = (D_NOPE + D_ROPE) ** -0.5

    def body(q_ref, k_ref, v_ref, o_ref, lse_ref, do_ref, dq_ref, dk_ref, dv_ref):
        dk_ref[...] = jnp.zeros_like(dk_ref)
        dv_ref[...] = jnp.zeros_like(dv_ref)
        for i in range(t // tq):
            n_k = (i + 1) * tq
            rows = slice(i * tq, (i + 1) * tq)
            qi = q_ref[rows, :]
            doi = do_ref[rows, :].astype(F32)
            s = _dot_nt(qi, k_ref[0:n_k, :]) * scale
            row = lax.broadcasted_iota(jnp.int32, (tq, n_k), 0) + i * tq
            colv = lax.broadcasted_iota(jnp.int32, (tq, n_k), 1)
            p = jnp.where(colv <= row, jnp.exp(s - lse_ref[0, rows, :]), 0.0)
            dp = _dot_nt(doi, v_ref[0:n_k, :])
            delta = jnp.sum(doi * o_ref[rows, :].astype(F32), axis=-1, keepdims=True)
            ds = p * (dp - delta) * scale
            dq_ref[rows, :] = _dot(ds, k_ref[0:n_k, :])
            dk_ref[0:n_k, :] += _dot_tn(ds, qi)
            dv_ref[0:n_k, :] += _dot_tn(p, doi)

    qk_spec = pl.BlockSpec((t, D_QK), lambda h: (0, h))
    v_spec = pl.BlockSpec((t, D_V), lambda h: (0, h))
    return pl.pallas_call(
        body, grid=(HEADS,),
        in_specs=[qk_spec, qk_spec, pl.BlockSpec((t, D_V), lambda h: (0, v_off + h)), v_spec,
                  pl.BlockSpec((1, t, 1), lambda h: (h, 0, 0)), v_spec],
        out_specs=[qk_spec, qk_spec, v_spec],
        out_shape=[SDS((t, HEADS * D_QK), F32), SDS((t, HEADS * D_QK), F32), SDS((t, HEADS * D_V), F32)],
        name=name, compiler_params=_cp(("parallel",)))(q, k, v, o, lse, do)


CONV_COLS = 256


def _conv_pre(u, w_ref, rowi):
    acc = u * w_ref[CONV_WIDTH - 1:CONV_WIDTH, :]
    for sft in range(1, CONV_WIDTH):
        shifted = jnp.where(rowi >= sft, pltpu.roll(u, sft, 0), 0.0)
        acc = acc + shifted * w_ref[CONV_WIDTH - 1 - sft:CONV_WIDTH - sft, :]
    return acc


def _conv_fwd(p_main, conv_w, name):
    t = p_main.shape[0]
    off = 2 * Q_LORA // CONV_COLS

    def body(u_ref, w_ref, y_ref):
        u = u_ref[...]
        rowi = lax.broadcasted_iota(jnp.int32, u.shape, 0)
        pre = _conv_pre(u, w_ref, rowi)
        y_ref[...] = pre * _sigmoid(pre)

    return pl.pallas_call(
        body, grid=(3 * GDN_W // CONV_COLS,),
        in_specs=[pl.BlockSpec((t, CONV_COLS), lambda j: (0, off + j)), pl.BlockSpec((CONV_WIDTH, CONV_COLS), lambda j: (0, j))],
        out_specs=pl.BlockSpec((t, CONV_COLS), lambda j: (0, j)), out_shape=SDS((t, 3 * GDN_W), F32),
        name=name, compiler_params=_cp(("parallel",)))(p_main, conv_w)


def _conv_bwd(p_main, conv_w, dyc, name):
    t = p_main.shape[0]
    off = 2 * Q_LORA // CONV_COLS

    def body(u_ref, w_ref, dy_ref, du_ref, dw_ref):
        u = u_ref[...]
        rowi = lax.broadcasted_iota(jnp.int32, u.shape, 0)
        pre = _conv_pre(u, w_ref, rowi)
        sg = _sigmoid(pre)
        dpre = dy_ref[...] * sg * (1.0 + pre * (1.0 - sg))
        du = dpre * w_ref[CONV_WIDTH - 1:CONV_WIDTH, :]
        dw_ref[CONV_WIDTH - 1:CONV_WIDTH, :] = jnp.sum(dpre * u, axis=0, keepdims=True)
        for sft in range(1, CONV_WIDTH):
            back = jnp.where(rowi < t - sft, pltpu.roll(dpre, t - sft, 0), 0.0)
            du = du + back * w_ref[CONV_WIDTH - 1 - sft:CONV_WIDTH - sft, :]
            shifted = jnp.where(rowi >= sft, pltpu.roll(u, sft, 0), 0.0)
            dw_ref[CONV_WIDTH - 1 - sft:CONV_WIDTH - sft, :] = jnp.sum(dpre * shifted, axis=0, keepdims=True)
        du_ref[...] = du.astype(BF16)

    blk = pl.BlockSpec((t, CONV_COLS), lambda j: (0, j))
    wblk = pl.BlockSpec((CONV_WIDTH, CONV_COLS), lambda j: (0, j))
    return pl.pallas_call(
        body, grid=(3 * GDN_W // CONV_COLS,),
        in_specs=[pl.BlockSpec((t, CONV_COLS), lambda j: (0, off + j)), wblk, blk],
        out_specs=[blk, wblk], out_shape=[SDS((t, 3 * GDN_W), BF16), SDS((CONV_WIDTH, 3 * GDN_W), F32)],
        name=name, compiler_params=_cp(("parallel",)))(p_main, conv_w, dyc)


B_LO, A_LO, A_HI = D_ROPE, D_ROPE + HEADS, D_ROPE + 2 * HEADS


def _softplus(z):
    e = jnp.exp(-jnp.abs(z))
    log1p = jnp.where(e < 0.01, e * (1.0 - e * (0.5 - e * (1.0 / 3.0))), jnp.log(1.0 + e))
    return jnp.maximum(z, 0.0) + log1p


def _gdn_gates(p_small, a_row, dt_row, name):
    t = p_small.shape[0]

    def body(ps_ref, a_ref, dt_ref, g_ref, gc_ref):
        x = ps_ref[...]
        lane = lax.broadcasted_iota(jnp.int32, x.shape, 1)
        is_g = (lane >= A_LO) & (lane < A_HI)
        g = jnp.where(is_g, -jnp.exp(a_ref[...]) * _softplus(x + dt_ref[...]), 0.0)
        g_ref[...] = jnp.where(is_g, g, _sigmoid(x))
        pos = lax.broadcasted_iota(jnp.int32, x.shape, 0) % CHUNK
        acc = g
        sft = 1
        while sft < CHUNK:
            acc = acc + jnp.where(pos >= sft, pltpu.roll(acc, sft, 0), 0.0)
            sft *= 2
        gc_ref[...] = acc

    full = pl.BlockSpec((t, LANES), lambda i: (0, 0))
    row = pl.BlockSpec((1, LANES), lambda i: (0, 0))
    return pl.pallas_call(
        body, grid=(1,), in_specs=[full, row, row], out_specs=[full, full],
        out_shape=[SDS((t, LANES), F32), SDS((t, LANES), F32)], name=name,
        compiler_params=_cp(("arbitrary",)))(p_small, a_row, dt_row)


def _gdn_gates_bwd(p_small, a_row, dt_row, gates, dgates, dkpe, name):
    t = p_small.shape[0]

    def body(ps_ref, a_ref, dt_ref, g_ref, db_ref, dkpe_ref, dp_ref, da_ref, ddt_ref):
        x = ps_ref[...]
        lane = lax.broadcasted_iota(jnp.int32, x.shape, 1)
        is_g = (lane >= A_LO) & (lane < A_HI)
        is_b = (lane >= B_LO) & (lane < A_LO)
        pos = lax.broadcasted_iota(jnp.int32, x.shape, 0) % CHUNK
        acc = jnp.where(is_g, db_ref[...], 0.0)
        sft = 1
        while sft < CHUNK:
            acc = acc + jnp.where(pos < CHUNK - sft, pltpu.roll(acc, t - sft, 0), 0.0)
            sft *= 2
        dg = acc
        gv = g_ref[...]
        dz = jnp.where(is_g, dg * (-jnp.exp(a_ref[...])) * _sigmoid(x + dt_ref[...]), 0.0)
        da_ref[...] = jnp.sum(jnp.where(is_g, dg * gv, 0.0), axis=0, keepdims=True)
        ddt_ref[...] = jnp.sum(dz, axis=0, keepdims=True)
        dlb = jnp.where(is_b, db_ref[...] * gv * (1.0 - gv), 0.0)
        dp_ref[...] = (jnp.where(lane < D_ROPE, dkpe_ref[...], 0.0) + dlb + dz).astype(BF16)

    full = pl.BlockSpec((t, LANES), lambda i: (0, 0))
    row = pl.BlockSpec((1, LANES), lambda i: (0, 0))
    return pl.pallas_call(
        body, grid=(1,), in_specs=[full, row, row, full, full, full], out_specs=[full, row, row],
        out_shape=[SDS((t, LANES), BF16), SDS((1, LANES), F32), SDS((1, LANES), F32)], name=name,
        compiler_params=_cp(("arbitrary",)))(p_small, a_row, dt_row, gates, dgates, dkpe)


def _tri_inv(l, eye):
    x = eye - l
    p = _bdot(l, l, exact=True)
    steps = int(math.log2(CHUNK)) - 1
    for s in range(steps):
        x = x + _bdot(x, p, exact=True)
        if s < steps - 1:
            p = _bdot(p, p, exact=True)
    return x


def _l2n(x3):
    r = lax.rsqrt(jnp.sum(x3 * x3, axis=-1, keepdims=True) + EPS)
    return x3 * r, r


def _head_col(a_ref, lane_lo, n):
    a = a_ref[...]
    lane = lax.broadcasted_iota(jnp.int32, a.shape, 1)
    col = jnp.sum(jnp.where(lane == lane_lo + pl.program_id(0), a, 0.0), axis=-1, keepdims=True)
    return col.reshape(n, CHUNK, 1)


def _gdn_common(q3, k3, v3, b, gc):
    n = q3.shape[0]
    ri = lax.broadcasted_iota(jnp.int32, (n, CHUNK, CHUNK), 1)
    ci = lax.broadcasted_iota(jnp.int32, (n, CHUNK, CHUNK), 2)
    lower, strict = ri >= ci, ri > ci
    eye = (ri == ci).astype(F32)
    gr = jnp.sum(gc * eye, axis=1, keepdims=True)
    qh, rq = _l2n(q3)
    qn = qh * (D_V ** -0.5)
    kn, rk = _l2n(k3)
    dec = jnp.where(lower, jnp.exp(jnp.where(lower, gc - gr, 0.0)), 0.0)
    kb = kn * b
    mm = _bdot_nt(kb, kn)
    tinv = _tri_inv(jnp.where(strict, mm * dec, 0.0), eye)
    gam = jnp.exp(gc)
    u = _bdot(tinv, v3 * b, exact=True)
    w = _bdot(tinv, kb * gam, exact=True)
    qk = _bdot_nt(qn, kn)
    aqk = jnp.where(lower, qk * dec, 0.0)
    gl = gc[:, CHUNK - 1:CHUNK, :]
    kdf = jnp.exp(gl - gc)
    return dict(ri=ri, ci=ci, lower=lower, strict=strict, eye=eye, qh=qh, rq=rq, qn=qn, kn=kn, rk=rk, dec=dec, kb=kb,
                mm=mm, gam=gam, u=u, w=w, qk=qk, aqk=aqk, gl=gl, kdf=kdf, kd=kn * kdf, gr=gr)


def _gdn_fwd(yc, p_main, gates, gcum, gn, name):
    t = yc.shape[0]
    n = t // CHUNK
    z_off = (2 * Q_LORA + 3 * GDN_W) // D_V

    def body(q_ref, k_ref, v_ref, z_ref, gt_ref, gcum_ref, gn_ref, o_ref, g_ref, s_ref, u_s, w_s, qg_s, kd_s, a_s, e_s):
        c = _gdn_common(q_ref[...].reshape(n, CHUNK, D_V), k_ref[...].reshape(n, CHUNK, D_V),
                        v_ref[...].reshape(n, CHUNK, D_V), _head_col(gt_ref, B_LO, n), _head_col(gcum_ref, A_LO, n))
        u_s[...] = c["u"]
        w_s[...] = c["w"]
        qg_s[...] = c["qn"] * c["gam"]
        kd_s[...] = c["kd"]
        a_s[...] = c["aqk"]
        e_s[...] = jnp.broadcast_to(jnp.exp(c["gl"]), (n, 1, D_V))

        def step(i, s):
            s_ref[0, i] = s
            v_new = u_s[i] - _dot(w_s[i], s)
            o = _dot(qg_s[i], s) + _dot(a_s[i], v_new)
            o_ref[pl.ds(pl.multiple_of(i * CHUNK, CHUNK), CHUNK), :] = o
            return s * e_s[i] + _dot_tn(kd_s[i], v_new)

        lax.fori_loop(0, n, step, jnp.zeros((D_V, D_V), F32))
        o = o_ref[...]
        zz = z_ref[...]
        on = o * lax.rsqrt(jnp.mean(o * o, axis=-1, keepdims=True) + EPS) * gn_ref[...]
        g_ref[...] = (on * zz * _sigmoid(zz)).astype(BF16)

    col = lambda off: pl.BlockSpec((t, D_V), lambda h: (0, off + h))
    lanes = pl.BlockSpec((t, LANES), lambda h: (0, 0))
    big = pltpu.VMEM((n, CHUNK, D_V), F32)
    return pl.pallas_call(
        body, grid=(HEADS,),
        in_specs=[col(0), col(HEADS), col(2 * HEADS), col(z_off), lanes, lanes, pl.BlockSpec((1, D_V), lambda h: (0, 0))],
        out_specs=[col(0), col(0), pl.BlockSpec((1, n, D_V, D_V), lambda h: (h, 0, 0, 0))],
        out_shape=[SDS((t, GDN_W), F32), SDS((t, GDN_W), BF16), SDS((HEADS, n, D_V, D_V), F32)],
        scratch_shapes=[big, big, big, big, pltpu.VMEM((n, CHUNK, CHUNK), F32), pltpu.VMEM((n, 1, D_V), F32)],
        name=name, compiler_params=_cp(("parallel",)))(yc, yc, yc, p_main, gates, gcum, gn)


def _gdn_bwd(yc, p_main, gates, gcum, gn, o_raw, states, dgated, name):
    t = yc.shape[0]
    n = t // CHUNK
    z_off = (2 * Q_LORA + 3 * GDN_W) // D_V

    def body(q_ref, k_ref, v_ref, z_ref, gt_ref, gcum_ref, gn_ref, o_ref, s_ref, dg_ref,
             dq_ref, dk_ref, dv_ref, dz_ref, dgt_ref, dgn_ref,
             u_s, w_s, qg_s, kd_s, at_s, e_s, do_s, du_s, dw_s, dqg_s, dkd_s, da_s, dat_s, dgs_s):
        @pl.when(pl.program_id(0) == 0)
        def _():
            dgn_ref[...] = jnp.zeros_like(dgn_ref)
            dgt_ref[...] = jnp.zeros_like(dgt_ref)

        o = o_ref[...]
        zz = z_ref[...]
        dgv = dg_ref[...]
        gnv = gn_ref[...]
        r = lax.rsqrt(jnp.mean(o * o, axis=-1, keepdims=True) + EPS)
        oh = o * r
        sg = _sigmoid(zz)
        don = dgv * zz * sg
        dz_ref[...] = (dgv * oh * gnv * sg * (1.0 + zz * (1.0 - sg))).astype(BF16)
        dgn_ref[...] += jnp.sum(don * oh, axis=0, keepdims=True)
        doh = don * gnv
        do_s[...] = (r * (doh - oh * jnp.mean(doh * oh, axis=-1, keepdims=True))).reshape(n, CHUNK, D_V)

        q3 = q_ref[...].reshape(n, CHUNK, D_V)
        k3 = k_ref[...].reshape(n, CHUNK, D_V)
        v3 = v_ref[...].reshape(n, CHUNK, D_V)
        b, gc = _head_col(gt_ref, B_LO, n), _head_col(gcum_ref, A_LO, n)
        c = _gdn_common(q3, k3, v3, b, gc)
        gr = c["gr"]
        ri, ci = c["ri"], c["ci"]
        upper, sup = ci >= ri, ci > ri
        dect = jnp.where(upper, jnp.exp(jnp.where(upper, gr - gc, 0.0)), 0.0)
        tinv_t = _tri_inv(jnp.where(sup, _bdot_nt(c["kn"], c["kb"]) * dect, 0.0), c["eye"])
        u_s[...] = c["u"]
        w_s[...] = c["w"]
        qg_s[...] = c["qn"] * c["gam"]
        kd_s[...] = c["kd"]
        at_s[...] = jnp.where(upper, _bdot_nt(c["kn"], c["qn"]) * dect, 0.0)
        e_s[...] = jnp.broadcast_to(jnp.exp(c["gl"]), (n, 1, D_V))

        def step(j, ds):
            i = n - 1 - j
            s = s_ref[0, i]
            do_i = do_s[i]
            v_new = u_s[i] - _dot(w_s[i], s)
            dvn = _dot(at_s[i], do_i) + _dot(kd_s[i], ds)
            da_s[i] = _dot_nt(do_i, v_new)
            dat_s[i] = _dot_nt(v_new, do_i)
            dqg_s[i] = _dot_nt(do_i, s)
            dw_s[i] = -_dot_nt(dvn, s)
            dkd_s[i] = _dot_nt(v_new, ds)
            du_s[i] = dvn
            dgs_s[i] = jnp.broadcast_to(jnp.sum(jnp.sum(s * ds, axis=1, keepdims=True), axis=0, keepdims=True), (1, D_V))
            return _dot_tn(qg_s[i], do_i) + e_s[i] * ds - _dot_tn(w_s[i], dvn)

        lax.fori_loop(0, n, step, jnp.zeros((D_V, D_V), F32))

        du, dw, dqg, dkd = du_s[...], dw_s[...], dqg_s[...], dkd_s[...]
        lower, strict, dec = c["lower"], c["strict"], c["dec"]
        kn, kb, qn, gam, kdf = c["kn"], c["kb"], c["qn"], c["gam"], c["kdf"]
        drv = _bdot(tinv_t, du, exact=True)
        drk = _bdot(tinv_t, dw, exact=True)
        dl = jnp.where(strict, -(_bdot_nt(drv, c["u"]) + _bdot_nt(drk, c["w"])), 0.0)
        dlt = jnp.where(sup, -(_bdot_nt(c["u"], drv) + _bdot_nt(c["w"], drk)), 0.0)
        da = jnp.where(lower, da_s[...], 0.0)
        dat = jnp.where(upper, dat_s[...], 0.0)
        e = (dl * c["mm"] + da * c["qk"]) * dec
        col_sums = jnp.sum(e, axis=1, keepdims=True)
        dgc = jnp.sum(e, axis=2, keepdims=True) - jnp.sum(col_sums * c["eye"], axis=2, keepdims=True)
        dkb = _bdot(dl * dec, kn) + gam * drk
        dkn = _bdot(dlt * dect, kb) + _bdot(dat * dect, qn) + b * dkb + dkd * kdf
        dqn = _bdot(da * dec, kn) + gam * dqg
        dgam = jnp.sum(drk * kb, axis=-1, keepdims=True) + jnp.sum(dqg * qn, axis=-1, keepdims=True)
        dbeta = jnp.sum(dkb * kn, axis=-1, keepdims=True) + jnp.sum(drv * v3, axis=-1, keepdims=True)
        dv_ref[...] = (b * drv).reshape(t, D_V)
        ee = jnp.sum(dkd * kn, axis=-1, keepdims=True) * kdf
        dgc = dgc + dgam * gam - ee
        rowc = lax.broadcasted_iota(jnp.int32, (n, CHUNK, 1), 1)
        tail = jnp.sum(ee, axis=1, keepdims=True) + dgs_s[...][:, :, 0:1] * jnp.exp(c["gl"])
        dgc = dgc + jnp.where(rowc == CHUNK - 1, tail, 0.0)
        lane = lax.broadcasted_iota(jnp.int32, (t, LANES), 1)
        head = pl.program_id(0)
        dgt_ref[...] += (jnp.where(lane == B_LO + head, dbeta.reshape(t, 1), 0.0)
                         + jnp.where(lane == A_LO + head, dgc.reshape(t, 1), 0.0))
        sc = D_V ** -0.5
        qh, rq, rk = c["qh"], c["rq"], c["rk"]
        dq_ref[...] = (rq * (sc * dqn - qh * jnp.sum(sc * dqn * qh, axis=-1, keepdims=True))).reshape(t, D_V)
        dk_ref[...] = (rk * (dkn - kn * jnp.sum(dkn * kn, axis=-1, keepdims=True))).reshape(t, D_V)

    once = pl.Buffered(1)
    col = lambda off: pl.BlockSpec((t, D_V), lambda h: (0, off + h), pipeline_mode=once)
    out_col = pl.BlockSpec((t, D_V), lambda h: (0, h))
    lanes = pl.BlockSpec((t, LANES), lambda h: (0, 0))
    row = pl.BlockSpec((1, D_V), lambda h: (0, 0))
    big = pltpu.VMEM((n, CHUNK, D_V), F32)
    sq = pltpu.VMEM((n, CHUNK, CHUNK), F32)
    small = pltpu.VMEM((n, 1, D_V), F32)
    return pl.pallas_call(
        body, grid=(HEADS,),
        in_specs=[col(0), col(HEADS), col(2 * HEADS), col(z_off), lanes, lanes, row, col(0),
                  pl.BlockSpec((1, n, D_V, D_V), lambda h: (h, 0, 0, 0), pipeline_mode=once), col(0)],
        out_specs=[out_col, out_col, out_col, out_col, lanes, row],
        out_shape=[SDS((t, GDN_W), F32), SDS((t, GDN_W), F32), SDS((t, GDN_W), F32), SDS((t, GDN_W), BF16),
                   SDS((t, LANES), F32), SDS((1, D_V), F32)],
        scratch_shapes=[big, big, big, big, sq, small, big, big, big, big, big, sq, sq, small],
        name=name, compiler_params=_cp(("arbitrary",)))(yc, yc, yc, p_main, gates, gcum, gn, o_raw, states, dgated)


def _col_tile(d):
    return _pick(d, (512, 256, 128))


def _mix_fwd(y_a, y_b, p_main, name):
    t, d = y_a.shape
    tm, cw = _row_tile(t), _col_tile(d)
    off_a, off_b = MAIN_FIXED // cw, (MAIN_FIXED + d) // cw

    def body(ya_ref, yb_ref, ga_ref, gb_ref, u_ref):
        u_ref[...] = (_sigmoid(ga_ref[...]) * ya_ref[...] + _sigmoid(gb_ref[...]) * yb_ref[...]).astype(BF16)

    blk = pl.BlockSpec((tm, cw), lambda i, j: (i, j))
    return pl.pallas_call(
        body, grid=(t // tm, d // cw),
        in_specs=[blk, blk, pl.BlockSpec((tm, cw), lambda i, j: (i, off_a + j)), pl.BlockSpec((tm, cw), lambda i, j: (i, off_b + j))],
        out_specs=blk, out_shape=SDS((t, d), BF16), name=name,
        compiler_params=_cp(("parallel", "parallel")))(y_a, y_b, p_main, p_main)


def _mix_bwd(du, y_a, y_b, p_main, name):
    t, d = y_a.shape
    tm, cw = _row_tile(t), _col_tile(d)
    off_a, off_b = MAIN_FIXED // cw, (MAIN_FIXED + d) // cw
    nb = d // cw

    def body(du_ref, ya_ref, yb_ref, ga_ref, gb_ref, dya_ref, dyb_ref, dla_ref, dlb_ref):
        duv = du_ref[...]
        ga, gb = _sigmoid(ga_ref[...]), _sigmoid(gb_ref[...])
        dya_ref[...] = (duv * ga).astype(BF16)
        dyb_ref[...] = (duv * gb).astype(BF16)
        dla_ref[...] = (duv * ya_ref[...] * ga * (1.0 - ga)).astype(BF16)
        dlb_ref[...] = (duv * yb_ref[...] * gb * (1.0 - gb)).astype(BF16)

    blk = pl.BlockSpec((tm, cw), lambda i, j: (i, j))
    outs = pl.pallas_call(
        body, grid=(t // tm, nb),
        in_specs=[blk, blk, blk, pl.BlockSpec((tm, cw), lambda i, j: (i, off_a + j)),
                  pl.BlockSpec((tm, cw), lambda i, j: (i, off_b + j))],
        out_specs=[blk, blk, blk, blk],
        out_shape=[SDS((t, d), BF16), SDS((t, d), BF16), SDS((t, d), BF16), SDS((t, d), BF16)], name=name,
        compiler_params=_cp(("parallel", "parallel")))(du, y_a, y_b, p_main, p_main)
    return outs


def _gate_res(x, y, gt, name):
    t, d = x.shape
    tm = _row_tile(t)

    def body(x_ref, y_ref, g_ref, o_ref):
        o_ref[...] = x_ref[...] + g_ref[...] * y_ref[...]

    blk = pl.BlockSpec((tm, d), lambda i: (i, 0))
    return pl.pallas_call(
        body, grid=(t // tm,), in_specs=[blk, blk, pl.BlockSpec((1, d), lambda i: (0, 0))], out_specs=blk,
        out_shape=SDS((t, d), F32), name=name, compiler_params=_cp(("parallel",)))(x, y, gt)


def _gate_res_bwd(dx, y, gt, name):
    t, d = dx.shape
    tm = _row_tile(t)

    def body(dx_ref, y_ref, g_ref, dg_ref, dy_ref):
        @pl.when(pl.program_id(0) == 0)
        def _():
            dg_ref[...] = jnp.zeros_like(dg_ref)

        dxv = dx_ref[...]
        dg_ref[...] += jnp.sum(dxv * y_ref[...], axis=0, keepdims=True)
        dy_ref[...] = (dxv * g_ref[...]).astype(BF16)

    blk = pl.BlockSpec((tm, d), lambda i: (i, 0))
    row = pl.BlockSpec((1, d), lambda i: (0, 0))
    return pl.pallas_call(
        body, grid=(t // tm,), in_specs=[blk, blk, row], out_specs=[row, blk],
        out_shape=[SDS((1, d), F32), SDS((t, d), BF16)], name=name, compiler_params=_cp(("arbitrary",)))(dx, y, gt)


def _swiglu_fwd(gu, name):
    t, f2 = gu.shape
    f = f2 // 2
    tm, cw = _row_tile(t), _col_tile(f)
    nb = f // cw

    def body(g_ref, u_ref, o_ref):
        g = g_ref[...]
        o_ref[...] = (g * _sigmoid(g) * u_ref[...]).astype(BF16)

    return pl.pallas_call(
        body, grid=(t // tm, nb),
        in_specs=[pl.BlockSpec((tm, cw), lambda i, j: (i, j)), pl.BlockSpec((tm, cw), lambda i, j: (i, nb + j))],
        out_specs=pl.BlockSpec((tm, cw), lambda i, j: (i, j)), out_shape=SDS((t, f), BF16), name=name,
        compiler_params=_cp(("parallel", "parallel")))(gu, gu)


def _swiglu_bwd(gu, da, name):
    t, f2 = gu.shape
    f = f2 // 2
    tm, cw = _row_tile(t), _col_tile(f)
    nb = f // cw

    def body(g_ref, u_ref, da_ref, dg_ref, dup_ref):
        g = g_ref[...]
        dav = da_ref[...]
        sg = _sigmoid(g)
        dg_ref[...] = (dav * u_ref[...] * sg * (1.0 + g * (1.0 - sg))).astype(BF16)
        dup_ref[...] = (dav * g * sg).astype(BF16)

    blk = pl.BlockSpec((tm, cw), lambda i, j: (i, j))
    dg, dup = pl.pallas_call(
        body, grid=(t // tm, nb),
        in_specs=[blk, pl.BlockSpec((tm, cw), lambda i, j: (i, nb + j)), blk], out_specs=[blk, blk],
        out_shape=[SDS((t, f), BF16), SDS((t, f), BF16)], name=name,
        compiler_params=_cp(("parallel", "parallel")))(gu, gu, da)
    return dg, dup


def _loss_head(x, w, target, name):
    t, d = x.shape
    tm = _row_tile(t)

    def body(x_ref, w_ref, t_ref, l_ref, dx_ref, dw_ref):
        @pl.when(pl.program_id(0) == 0)
        def _():
            l_ref[...] = jnp.zeros_like(l_ref)
            dw_ref[...] = jnp.zeros_like(dw_ref)

        xv = x_ref[...]
        wv = w_ref[...]
        r = lax.rsqrt(jnp.mean(xv * xv, axis=-1, keepdims=True) + EPS)
        xh = xv * r
        err = xh * wv - t_ref[...]
        per_tok = jnp.mean(err * err, axis=-1, keepdims=True)
        l_ref[...] += 0.5 * jnp.sum(per_tok, axis=0, keepdims=True)
        dy = err * (1.0 / d)
        dw_ref[...] += jnp.sum(dy * xh, axis=0, keepdims=True)
        dxh = dy * wv
        dx_ref[...] = r * (dxh - xh * jnp.mean(dxh * xh, axis=-1, keepdims=True))

    blk = pl.BlockSpec((tm, d), lambda i: (i, 0))
    row = pl.BlockSpec((1, d), lambda i: (0, 0))
    return pl.pallas_call(
        body, grid=(t // tm,), in_specs=[blk, row, blk],
        out_specs=[pl.BlockSpec((1, LANES), lambda i: (0, 0)), blk, row],
        out_shape=[SDS((1, LANES), F32), SDS((t, d), F32), SDS((1, d), F32)], name=name,
        compiler_params=_cp(("arbitrary",)))(x, w, target)


def _adamw(w, g, m, v, name):
    shape = w.shape
    per_layer = isinstance(g, (list, tuple))
    if w.ndim < 3:
        w, m, v = (a.reshape((1,) * (3 - a.ndim) + a.shape) for a in (w, m, v))
        g = g.reshape(w.shape)
    n_layers, rows, cols = w.shape
    lanes_padded = -(-cols // LANES) * LANES
    budget_rows = max(8, (24 * 1024 * 1024) // (lanes_padded * 4 * 18))
    tr = rows
    if rows > budget_rows:
        tr = _pick(rows, tuple(c for c in (1024, 512, 256, 128, 64, 32, 16, 8) if c <= budget_rows))
    c1 = 1.0 / (1.0 - ADAM_B1 ** ADAM_STEP)
    c2 = 1.0 / (1.0 - ADAM_B2 ** ADAM_STEP)
    n_g = len(g) if per_layer else 1

    def body(*refs):
        w_ref, m_ref, v_ref = refs[:3]
        g_refs = refs[3:3 + n_g]
        outs = refs[3 + n_g:]
        gv = g_refs[0][...]
        for l in range(1, n_g):
            gv = jnp.where(pl.program_id(0) == l, g_refs[l][...], gv)
        mn = ADAM_B1 * m_ref[...] + (1.0 - ADAM_B1) * gv
        vn = ADAM_B2 * v_ref[...] + (1.0 - ADAM_B2) * (gv * gv)
        outs[0][...] = -ADAM_LR * ((mn * c1) / (jnp.sqrt(vn * c2) + ADAM_EPS) + ADAM_WD * w_ref[...])
        outs[1][...] = mn
        outs[2][...] = vn
        if per_layer:
            outs[3][...] = gv

    blk = pl.BlockSpec((None, tr, cols), lambda l, i: (l, i, 0))
    g_specs = [pl.BlockSpec((tr, cols), lambda l, i: (i, 0))] * n_g if per_layer else [blk]
    n_out = 4 if per_layer else 3
    outs = pl.pallas_call(
        body, grid=(n_layers, rows // tr), in_specs=[blk, blk, blk] + g_specs, out_specs=[blk] * n_out,
        out_shape=[SDS(w.shape, F32)] * n_out, name=name,
        compiler_params=_cp(("parallel", "parallel")))(w, m, v, *(g if per_layer else [g]))
    g_out = outs[3] if per_layer else g
    return (g_out.reshape(shape),) + tuple(o.reshape(shape) for o in outs[:3])


KPE_LO = 2 * Q_LORA
QKVZ_LO = KPE_LO + D_ROPE
BA_LO = QKVZ_LO + 4 * GDN_W
GATE_LO = BA_LO + 2 * HEADS


def _lay_w_in(w_in):
    d = w_in.shape[0]
    main = jnp.concatenate([w_in[:, :KPE_LO], w_in[:, QKVZ_LO:BA_LO], w_in[:, GATE_LO:]], axis=1)
    small = jnp.concatenate([w_in[:, KPE_LO:QKVZ_LO], w_in[:, BA_LO:GATE_LO],
                             jnp.zeros((d, LANES - D_ROPE - 2 * HEADS), w_in.dtype)], axis=1)
    return main, small


def _unlay_w_in(g_main, g_small):
    return jnp.concatenate([g_main[:, :KPE_LO], g_small[:, :D_ROPE], g_main[:, KPE_LO:KPE_LO + 4 * GDN_W],
                            g_small[:, D_ROPE:D_ROPE + 2 * HEADS], g_main[:, MAIN_FIXED:]], axis=1)


def _lay_w_uq(w_uq):
    r = w_uq.reshape(Q_LORA, HEADS, D_NOPE + D_ROPE)
    r = jnp.pad(r, ((0, 0), (0, 0), (0, D_QK - D_NOPE - D_ROPE)))
    return r.reshape(Q_LORA, HEADS * D_QK)


def _unlay_w_uq(g):
    rows = g.shape[0]
    return g.reshape(rows, HEADS, D_QK)[:, :, :D_NOPE + D_ROPE].reshape(rows, HEADS * (D_NOPE + D_ROPE))


def _lay_w_ukv(w_ukv):
    return w_ukv.reshape(KV_LORA, HEADS, 2, D_V).transpose(0, 2, 1, 3).reshape(KV_LORA, 2 * HEADS * D_V)


def _unlay_w_ukv(g):
    rows = g.shape[0]
    return g.reshape(rows, 2, HEADS, D_V).transpose(0, 2, 1, 3).reshape(rows, 2 * HEADS * D_V)


def _lane_row(vec, lo):
    return jnp.pad(vec.reshape(1, -1), ((0, 0), (lo, LANES - lo - vec.shape[0])))


def _rope_tables(positions):
    half = D_ROPE // 2
    inv_freq = 1.0 / (10000.0 ** (jnp.arange(0, D_ROPE, 2, dtype=F32) / D_ROPE))
    ang = positions.astype(F32)[:, None] * inv_freq
    cos, sin = jnp.cos(ang), jnp.sin(ang)
    t = positions.shape[0]
    zeros = lambda n: jnp.zeros((t, n), F32)
    tc = jnp.concatenate([cos, cos, zeros(LANES - D_ROPE)], axis=1)
    ts1 = jnp.concatenate([-sin, zeros(LANES - half)], axis=1)
    ts2 = jnp.concatenate([zeros(half), sin, zeros(LANES - D_ROPE)], axis=1)
    return tc, ts1, ts2


def _layer_fwd(x, mod, wt, tabs, tag, late_weights, after_gate_up=None):
    t, d = x.shape
    sh_a, sc_a, gt_a, sh_f, sc_f, gt_f = mod
    zero_l = jnp.zeros((1, Q_LORA), F32)
    s = dict(x=x)
    s["h1"] = _norm_fwd(x, 0, d, wt["norm_mix"], sc_a, sh_a, f"{tag}_norm_mix")
    s["p_main"] = _mm(s["h1"], wt["w_main"], name=f"{tag}_in_main")
    s["p_small"] = _mm(s["h1"], wt["w_small"], name=f"{tag}_in_small")
    s["cqn"] = _norm_fwd(s["p_main"], 0, Q_LORA, wt["q_a_norm"], zero_l, zero_l, f"{tag}_q_norm")
    s["ckvn"] = _norm_fwd(s["p_main"], 1, KV_LORA, wt["kv_a_norm"], zero_l, zero_l, f"{tag}_kv_norm")
    q_raw = _mm(s["cqn"], wt["w_uq"], name=f"{tag}_uq")
    s["kv_raw"] = _mm(s["ckvn"], wt["w_ukv"], name=f"{tag}_ukv")
    s["q_r"] = _rope_q(q_raw, *tabs, False, BF16, f"{tag}_rope_q")
    s["k_r"] = _k_assemble(s["kv_raw"], s["p_small"], *tabs, f"{tag}_k_asm")
    s["o"], s["lse"] = _attn_fwd(s["q_r"], s["k_r"], s["kv_raw"], HEADS, f"{tag}_attn")
    s["yc"] = _conv_fwd(s["p_main"], wt["conv_w"], f"{tag}_conv")
    s["gates"], s["gcum"] = _gdn_gates(s["p_small"], wt["a_row"], wt["dt_row"], f"{tag}_gates")
    s["o_raw"], s["gated"], s["states"] = _gdn_fwd(s["yc"], s["p_main"], s["gates"], s["gcum"], wt["gdn_norm"], f"{tag}_gdn")
    late, started = late_weights(s["gated"])
    wt = {**wt, **late}
    s["y_a"] = _mm(s["o"], wt["w_o_mla"], name=f"{tag}_o_mla")
    s["y_b"] = _mm(s["gated"], wt["w_o_gdn"], name=f"{tag}_o_gdn")
    s["u"] = _mix_fwd(s["y_a"], s["y_b"], s["p_main"], f"{tag}_mix")
    s["mixo"] = _mm(s["u"], wt["w_o"], name=f"{tag}_o")
    s["x2"] = _gate_res(x, s["mixo"], gt_a, f"{tag}_res_a")
    s["h2"] = _norm_fwd(s["x2"], 0, d, wt["norm_ffn"] + started, sc_f, sh_f, f"{tag}_norm_ffn")
    s["gu"] = _mm(s["h2"], wt["w_gate_up"], name=f"{tag}_gate_up")
    if after_gate_up is not None:
        gt_f = gt_f + after_gate_up(s["gu"])
    s["a"] = _swiglu_fwd(s["gu"], f"{tag}_swiglu")
    s["f"] = _mm(s["a"], wt["w_down"], name=f"{tag}_down")
    return _gate_res(s["x2"], s["f"], gt_f, f"{tag}_res_f"), s, wt


def _layer_bwd(dx3, s, mod, wt, tabs, tag, after_ffn=None, after_gdn=None):
    x = s["x"]
    t, d = x.shape
    sh_a, sc_a, gt_a, sh_f, sc_f, gt_f = mod
    zero_l = jnp.zeros((1, Q_LORA), F32)
    g = {}
    dgt_f, df = _gate_res_bwd(dx3, s["f"], gt_f, f"{tag}_b_res_f")
    da = _mm(df, wt["w_down"], tb=True, name=f"{tag}_b_down_x")
    g["w_down"] = _mm(s["a"].T, df, out_dtype=BF16, name=f"{tag}_b_down_w")
    dgate, dup = _swiglu_bwd(s["gu"], da, f"{tag}_b_swiglu")
    dgu = jnp.concatenate([dgate, dup], axis=1)
    dh2 = _mm(dgu, wt["w_gate_up"], tb=True, name=f"{tag}_b_gate_up_x")
    g["w_gate_up"] = _mm(s["h2"].T, dgu, out_dtype=BF16, name=f"{tag}_b_gate_up_w")
    if after_ffn is not None:
        gt_a = gt_a + after_ffn(g)
    dx2, g["norm_ffn"], dsc_f, dsh_f = _norm_bwd(s["x2"], 0, d, wt["norm_ffn"], sc_f, dh2, dx3, F32, f"{tag}_b_norm_ffn")
    dgt_a, dmixo = _gate_res_bwd(dx2, s["mixo"], gt_a, f"{tag}_b_res_a")
    du = _mm(dmixo, wt["w_o"], tb=True, name=f"{tag}_b_o_x")
    g["w_o"] = _mm(s["u"].T, dmixo, out_dtype=BF16, name=f"{tag}_b_o_w")
    dy_a, dy_b, dl_a, dl_b = _mix_bwd(du, s["y_a"], s["y_b"], s["p_main"], f"{tag}_b_mix")
    dgated = _mm(dy_b, wt["w_o_gdn"], tb=True, name=f"{tag}_b_o_gdn_x")
    g["w_o_gdn"] = _mm(s["gated"].T, dy_b, out_dtype=BF16, name=f"{tag}_b_o_gdn_w")
    dq_c, dk_c, dv_c, dz, dgates, g["gdn_norm"] = _gdn_bwd(
        s["yc"], s["p_main"], s["gates"], s["gcum"], wt["gdn_norm"], s["o_raw"], s["states"], dgated, f"{tag}_b_gdn")
    du_conv, g["conv_w"] = _conv_bwd(s["p_main"], wt["conv_w"], jnp.concatenate([dq_c, dk_c, dv_c], axis=1), f"{tag}_b_conv")
    do = _mm(dy_a, wt["w_o_mla"], tb=True, name=f"{tag}_b_o_mla_x")
    g["w_o_mla"] = _mm(s["o"].T, dy_a, out_dtype=BF16, name=f"{tag}_b_o_mla_w")
    dq_r, dk_r, dv = _attn_bwd(s["q_r"], s["k_r"], s["kv_raw"], HEADS, s["o"], s["lse"], do, f"{tag}_b_attn")
    q_a_norm = wt["q_a_norm"]
    if after_gdn is not None:
        q_a_norm = q_a_norm + after_gdn(du_conv)
    dq_raw = _rope_q(dq_r, *tabs, True, BF16, f"{tag}_b_rope_q")
    dkv_raw, dkpe = _k_assemble_bwd(dk_r, dv, *tabs, f"{tag}_b_k_asm")
    dcqn = _mm(dq_raw, wt["w_uq"], tb=True, name=f"{tag}_b_uq_x")
    g["w_uq"] = _mm(s["cqn"].T, dq_raw, out_dtype=BF16, name=f"{tag}_b_uq_w")
    dckvn = _mm(dkv_raw, wt["w_ukv"], tb=True, name=f"{tag}_b_ukv_x")
    g["w_ukv"] = _mm(s["ckvn"].T, dkv_raw, out_dtype=BF16, name=f"{tag}_b_ukv_w")
    dc_q, g["q_a_norm"], _, _ = _norm_bwd(s["p_main"], 0, Q_LORA, q_a_norm, zero_l, dcqn, None, BF16, f"{tag}_b_q_norm")
    dc_kv, g["kv_a_norm"], _, _ = _norm_bwd(s["p_main"], 1, KV_LORA, wt["kv_a_norm"], zero_l, dckvn, None, BF16,
                                            f"{tag}_b_kv_norm")
    dp_small, g["a_row"], g["dt_row"] = _gdn_gates_bwd(
        s["p_small"], wt["a_row"], wt["dt_row"], s["gates"], dgates, dkpe, f"{tag}_b_gates")
    dp_main = jnp.concatenate([dc_q, dc_kv, du_conv, dz, dl_a, dl_b], axis=1)
    h1t = s["h1"].T
    dh1 = _mm(dp_small, wt["w_small"], tb=True, name=f"{tag}_b_in_small_x")
    dh1 = _mm(dp_main, wt["w_main"], tb=True, acc_in=dh1, name=f"{tag}_b_in_main_x")
    g["w_main"] = _mm(h1t, dp_main, out_dtype=BF16, name=f"{tag}_b_in_main_w")
    g["w_small"] = _mm(h1t, dp_small, out_dtype=BF16, name=f"{tag}_b_in_small_w")
    dx, g["norm_mix"], dsc_a, dsh_a = _norm_bwd(x, 0, d, wt["norm_mix"], sc_a, dh1, dx2, F32, f"{tag}_b_norm_mix")
    return dx, (dsh_a, dsc_a, dgt_a, dsh_f, dsc_f, dgt_f), g


def _layer_weights(big, full, l):
    w_main, w_small = _lay_w_in(big["w_in"])
    return dict(
        w_main=w_main, w_small=w_small, w_uq=_lay_w_uq(big["w_uq"]), w_ukv=_lay_w_ukv(big["w_ukv"]),
        conv_w=full["conv_w"][l],
        norm_mix=full["norm_mix"][l][None], norm_ffn=full["norm_ffn"][l][None],
        q_a_norm=full["q_a_norm"][l][None], kv_a_norm=full["kv_a_norm"][l][None], gdn_norm=full["gdn_norm"][l][None],
        a_row=_lane_row(full["A_log"][l], A_LO), dt_row=_lane_row(full["dt_bias"][l], A_LO))


def _small_grads_ref_layout(g):
    return dict(
        conv_w=g["conv_w"], norm_mix=g["norm_mix"][0], norm_ffn=g["norm_ffn"][0], q_a_norm=g["q_a_norm"][0],
        kv_a_norm=g["kv_a_norm"][0], gdn_norm=g["gdn_norm"][0], A_log=g["a_row"][0, A_LO:A_HI],
        dt_bias=g["dt_row"][0, A_LO:A_HI])


def _local_step(x, mods, target, final_norm, full, positions):
    tabs = _rope_tables(positions)
    depth = len(mods)
    wts, saved = [None] * depth, []
    h = x
    for l in range(depth):
        early = _layer_weights({n: full[n][l] for n in FIRST_NEEDED}, full, l)
        h, s, wts[l] = _layer_fwd(h, mods[l], early, tabs, f"l{l}", lambda _, l=l: ({n: full[n][l] for n in LATER_NEEDED}, 0.0))
        saved.append(s)
    loss, dh, dfn = _loss_head(h, final_norm[None], target, "loss_head")
    dmods, grads = [None] * depth, [None] * depth
    for l in reversed(range(depth)):
        dh, dmods[l], grads[l] = _layer_bwd(dh, saved[l], mods[l], wts[l], tabs, f"l{l}")
    return loss, dh, dmods, grads, dfn[0]


HBM_SPEC = pl.BlockSpec(memory_space=pl.ANY)
VMEM_SPEC = pl.BlockSpec(memory_space=pltpu.VMEM)
N_CHIPS = 4
N_DEV = 8


def _me():
    return lax.axis_index("x"), lax.axis_index("y"), lax.axis_index("c")


def _flip(pos, f):
    mx, my, mc = pos
    fx, fy, fc = (f >> 2) & 1, (f >> 1) & 1, f & 1
    return ((mx + fx) % 2, (my + fy) % 2, (mc + fc) % 2)


def _all_gather_small(x, name):
    r, n = x.shape

    def body(x_ref, out_ref, send_sems, recv_sems, local_sem):
        me = _me()
        row = lambda p: 4 * p[0] + 2 * p[1] + p[2]
        mine = pltpu.make_async_copy(x_ref, out_ref.at[row(me)], local_sem)
        mine.start()

        def copy(f, origin):
            return pltpu.make_async_remote_copy(
                src_ref=x_ref, dst_ref=out_ref.at[row(origin)], send_sem=send_sems.at[f - 1], recv_sem=recv_sems.at[f - 1],
                device_id=_flip(me, f), device_id_type=MESH)

        sends = [copy(f, me) for f in range(1, N_DEV)]
        for cp in sends:
            cp.start()
        for f in range(1, N_DEV):
            copy(f, _flip(me, f)).wait_recv()
        for cp in sends:
            cp.wait_send()
        mine.wait()

    return pl.pallas_call(
        body, out_shape=SDS((N_DEV, r, n), x.dtype), in_specs=[VMEM_SPEC], out_specs=VMEM_SPEC,
        scratch_shapes=[pltpu.SemaphoreType.DMA((N_DEV - 1,)), pltpu.SemaphoreType.DMA((N_DEV - 1,)), pltpu.SemaphoreType.DMA],
        name=name, compiler_params=pltpu.CompilerParams(vmem_limit_bytes=VMEM_LIMIT))(x)


CHIP_FLIPS = (2, 4, 6)
SIBLING = 1


def _chip_of(pos):
    return 2 * pos[0] + pos[1]


def _sum_chips(p, got, chip, half, name):
    _, kh, ns = p.shape
    tr = _pick(kh, (256, 128, 64, 32, 16))
    nrb = kh // tr

    def body(c_ref, h_ref, a_ref, b_ref, o_ref):
        acc = a_ref[0].astype(F32)
        for j in range(3):
            acc = acc + b_ref[j].astype(F32)
        o_ref[...] = acc

    grid_spec = pltpu.PrefetchScalarGridSpec(
        num_scalar_prefetch=2, grid=(nrb,),
        in_specs=[pl.BlockSpec((1, tr, ns), lambda i, c, h: (c[0], i, 0)),
                  pl.BlockSpec((3, tr, ns), lambda i, c, h: (0, i, 0))],
        out_specs=pl.BlockSpec((tr, ns), lambda i, c, h: (h[0] * nrb + i, 0)))
    return pl.pallas_call(body, grid_spec=grid_spec, out_shape=SDS((2 * kh, ns), F32), name=name,
                          compiler_params=_cp(("parallel",)))(chip, half, p, got)


SEM_SPEC = pl.BlockSpec(memory_space=pltpu.SEMAPHORE)
HBM_ONLY = pl.BlockSpec(memory_space=pltpu.HBM)
DATAFLOW = pltpu.SideEffectType.DATAFLOW_SIDE_EFFECTING


def _in_hbm(a):
    return pltpu.with_memory_space_constraint(a, pltpu.HBM)


def _copies_start(name, srcs, lands, plan, n_copies):
    ns, nl = len(srcs), len(lands)

    def body(*refs):
        src_refs, land_refs = refs[:ns], refs[ns:ns + nl]
        send_sems, recv_sems = refs[ns + nl], refs[ns + nl + 1]
        token = refs[-1]
        for i, (src, dst, peer) in enumerate(plan(_me(), src_refs, land_refs)):
            pltpu.make_async_remote_copy(src_ref=src, dst_ref=dst, send_sem=send_sems.at[i], recv_sem=recv_sems.at[i],
                                         device_id=peer, device_id_type=MESH).start()
        token[...] = jnp.zeros_like(token)

    outs = pl.pallas_call(
        body, name=name,
        out_shape=(pltpu.SemaphoreType.DMA((n_copies,)), pltpu.SemaphoreType.DMA((n_copies,)),
                   *[pltpu.HBM(l.shape, l.dtype) for l in lands], SDS((8, LANES), F32)),
        in_specs=[HBM_ONLY] * (ns + nl), out_specs=(SEM_SPEC, SEM_SPEC, *[HBM_ONLY] * nl, VMEM_SPEC),
        input_output_aliases={ns + i: 2 + i for i in range(nl)},
        compiler_params=pltpu.CompilerParams(has_side_effects=DATAFLOW),
    )(*[_in_hbm(s) for s in srcs], *[_in_hbm(l) for l in lands])
    return outs[0], outs[1], list(outs[2:2 + nl]), outs[-1]


def _copies_wait(name, srcs, lands, send_sems, recv_sems, plan, after):
    ns, nl = len(srcs), len(lands)

    def body(*refs):
        src_refs, land_refs = refs[:ns], refs[ns:ns + nl]
        send_ref, recv_ref = refs[ns + nl], refs[ns + nl + 1]
        for i, (src, dst, peer) in enumerate(plan(_me(), src_refs, land_refs)):
            cp = pltpu.make_async_remote_copy(src_ref=src, dst_ref=dst, send_sem=send_ref.at[i], recv_sem=recv_ref.at[i],
                                              device_id=peer, device_id_type=MESH)
            cp.wait_send()
            cp.wait_recv()

    outs = pl.pallas_call(
        body, name=name, out_shape=[pltpu.HBM(l.shape, l.dtype) for l in lands],
        in_specs=[HBM_ONLY] * (ns + nl) + [SEM_SPEC, SEM_SPEC, HBM_SPEC], out_specs=[HBM_ONLY] * nl,
        input_output_aliases={ns + i: i for i in range(nl)},
        compiler_params=pltpu.CompilerParams(has_side_effects=DATAFLOW),
    )(*[_in_hbm(s) for s in srcs], *lands, send_sems, recv_sems, after)
    return list(outs)


def _half(ref, rows, axis):
    idx = [slice(None)] * axis + [rows]
    return ref.at[tuple(idx)]


def _gather_plans(layer, halves):
    def ici(me, srcs, lands):
        out = []
        for a, kh in enumerate(halves):
            rows = pl.ds(pl.multiple_of(me[2] * kh, 16), kh)
            for k in range(3):
                out.append((srcs[a].at[layer, rows], lands[a].at[_chip_of(me), rows], _flip(me, CHIP_FLIPS[k])))
        return out

    def d2d(me, srcs, lands):
        out = []
        for a, kh in enumerate(halves):
            rows = pl.ds(pl.multiple_of(me[2] * kh, 16), kh)
            for k in range(3):
                slab = lands[a].at[_chip_of(_flip(me, CHIP_FLIPS[k])), rows]
                out.append((slab, slab, _flip(me, SIBLING)))
        return out

    return ici, d2d


def _to_sibling_plan(halves, axes):
    def plan(me, srcs, lands):
        out = []
        for a, (kh, axis) in enumerate(zip(halves, axes)):
            rows = pl.ds(pl.multiple_of((1 - me[2]) * kh, 16), kh)
            out.append((_half(srcs[a], rows, axis), lands[a], _flip(me, SIBLING)))
        return out

    return plan


def _to_chips_plan(n_arr):
    def plan(me, srcs, lands):
        out = []
        for a in range(n_arr):
            for k in range(3):
                peer = _flip(me, CHIP_FLIPS[k])
                out.append((srcs[a].at[_chip_of(peer)], lands[a].at[k], peer))
        return out

    return plan


def _swap_plan(halves):
    def plan(me, srcs, lands):
        out = []
        for a, kh in enumerate(halves):
            rows = pl.ds(pl.multiple_of(me[2] * kh, 16), kh)
            out.append((lands[a].at[rows], lands[a].at[rows], _flip(me, SIBLING)))
        return out

    return plan


def _add_half(g, got, half, col_shards, name):
    s, kh, n = got.shape
    tr = _pick(kh, (512, 256, 128, 64, 32, 16))
    nrb = kh // tr
    width = n // N_CHIPS if col_shards else n
    cw = _pick(width, (1024, 512, 256, 128))
    per = width // cw

    def body(h_ref, a_ref, b_ref, o_ref):
        o_ref[...] = (a_ref[...].astype(F32) + b_ref[...].astype(F32)).astype(o_ref.dtype)

    in_specs = [pl.BlockSpec((None, tr, cw), lambda j, i, c, h: (j, h[0] * nrb + i, c)),
                pl.BlockSpec((None, tr, cw), lambda j, i, c, h: (j, i, c))]
    if col_shards:
        assert s == 1
        out_spec = pl.BlockSpec((None, tr, cw), lambda j, i, c, h: (c // per, i, c % per))
        out_shape = SDS((N_CHIPS, kh, width), g.dtype)
    else:
        out_spec, out_shape = in_specs[1], SDS((s, kh, n), g.dtype)
    grid_spec = pltpu.PrefetchScalarGridSpec(num_scalar_prefetch=1, grid=(s, nrb, n // cw), in_specs=in_specs,
                                             out_specs=out_spec)
    return pl.pallas_call(body, grid_spec=grid_spec, out_shape=out_shape, name=name,
                          compiler_params=_cp(("parallel", "parallel", "parallel")))(half, g, got)


def _sum_devices(g, name):
    _, _, n = g.shape

    def body(g_ref, o_ref):
        acc = g_ref[0]
        for k in range(1, N_DEV):
            acc = acc + g_ref[k]
        o_ref[...] = acc

    return pl.pallas_call(body, out_shape=SDS((1, n), F32), in_specs=[VMEM_SPEC], out_specs=VMEM_SPEC, name=name)(g)


def _silu_rows(c, name):
    def body(c_ref, o_ref):
        v = c_ref[...]
        o_ref[...] = v * _sigmoid(v)

    return pl.pallas_call(body, out_shape=SDS(c.shape, F32), in_specs=[VMEM_SPEC], out_specs=VMEM_SPEC, name=name)(c)


BIG = (("w_in", 2), ("w_uq", 2), ("w_ukv", 2), ("w_o_mla", 2), ("w_o_gdn", 2), ("w_o", 1), ("w_gate_up", 2), ("w_down", 1))
KERNEL_BIG = ("w_main", "w_small", "w_uq", "w_ukv", "w_o_mla", "w_o_gdn", "w_o", "w_gate_up", "w_down")
COL_SHARDED_AS_IS = ("w_o_mla", "w_o_gdn", "w_gate_up")
ROW_SHARDED = ("w_o", "w_down")
FIRST_NEEDED = ("w_in", "w_uq", "w_ukv")
LATER_NEEDED = ("w_o_mla", "w_o_gdn", "w_o", "w_gate_up", "w_down")
FFN_GRADS = ("w_gate_up", "w_down")
MIXER_GRADS = ("w_in", "w_uq", "w_ukv", "w_o_mla", "w_o_gdn", "w_o")
MIXER_GRADS_KERNEL = ("w_main", "w_small", "w_uq", "w_ukv", "w_o_mla", "w_o_gdn", "w_o")
SMALL = ("norm_mix", "norm_ffn", "q_a_norm", "kv_a_norm", "A_log", "dt_bias", "gdn_norm")
WEIGHTS = ("w_ada", "b_ada", "norm_mix", "norm_ffn", "w_in", "q_a_norm", "kv_a_norm", "w_uq", "w_ukv", "w_o_mla", "conv_w",
           "A_log", "dt_bias", "gdn_norm", "w_o_gdn", "w_o", "w_gate_up", "w_down", "final_norm")
ADA_PAD = 16
K_PAD = 128


def _pad_to(a, n, axis):
    pad = [(0, 0)] * a.ndim
    pad[axis] = (0, n - a.shape[axis])
    return jnp.pad(a, pad)


def kernel(x, c, positions, w_ada, b_ada, norm_mix, norm_ffn, w_in, q_a_norm, kv_a_norm, w_uq, w_ukv, w_o_mla, conv_w, A_log, dt_bias, gdn_norm, w_o_gdn, w_o, w_gate_up, w_down, final_norm, loss_target, m_w_ada, m_b_ada, m_norm_mix, m_norm_ffn, m_w_in, m_q_a_norm, m_kv_a_norm, m_w_uq, m_w_ukv, m_w_o_mla, m_conv_w, m_A_log, m_dt_bias, m_gdn_norm, m_w_o_gdn, m_w_o, m_w_gate_up, m_w_down, m_final_norm, v_w_ada, v_b_ada, v_norm_mix, v_norm_ffn, v_w_in, v_q_a_norm, v_kv_a_norm, v_w_uq, v_w_ukv, v_w_o_mla, v_conv_w, v_A_log, v_dt_bias, v_gdn_norm, v_w_o_gdn, v_w_o, v_w_gate_up, v_w_down, v_final_norm):
    env = dict(locals())
    w = {n: env[n] for n in WEIGHTS}
    depth, d = norm_mix.shape
    t = x.shape[1]
    me = _me()
    chip = _chip_of(me)
    dev = 4 * me[0] + 2 * me[1] + me[2]
    ada_cols = w_ada.shape[2]

    half_idx = me[2].astype(jnp.int32).reshape(1)
    chip_idx = chip.astype(jnp.int32).reshape(1)
    w16 = {n: w[n].astype(BF16) for n, _ in BIG}
    shard_axis = dict(BIG)
    gather = {}

    def start_group(key, layer, names, dep):
        srcs = [w16[n] for n in names]
        plans = _gather_plans(layer, [a.shape[1] // 2 for a in srcs])
        landing = [lax.empty((N_CHIPS,) + a.shape[1:], BF16) for a in srcs]
        send_s, recv_s, landing, tok = _copies_start(f"gather_{key}_ici_start", srcs + [dep], landing, plans[0], 3 * len(names))
        gather[key] = dict(layer=layer, names=names, srcs=srcs, plans=plans, ici=(send_s, recv_s, landing), tok=tok)
        return tok[0, 0]

    def pass_to_sibling(key, after):
        st = gather[key]
        send_s, recv_s, landing = st["ici"]
        landing = _copies_wait(f"gather_{key}_ici_wait", st["srcs"] + [st["tok"]], landing, send_s, recv_s, st["plans"][0],
                               st["tok"] if after is None else after)
        st["d2d"] = _copies_start(f"gather_{key}_d2d_start", [], landing, st["plans"][1], 3 * len(st["names"]))
        return st["d2d"][3]

    def gathered(key):
        st = gather[key]
        send_s, recv_s, landing, tok = st["d2d"]
        landing = _copies_wait(f"gather_{key}_d2d_wait", [], landing, send_s, recv_s, st["plans"][1], tok)
        return {n: jnp.concatenate([jnp.where(chip == j, own[st["layer"]], got[j]) for j in range(N_CHIPS)],
                                   axis=shard_axis[n] - 1)
                for n, own, got in zip(st["names"], st["srcs"], landing)}

    full = {}
    conv_all = _all_gather_small(conv_w.reshape(1, -1), "gather_conv").reshape((N_DEV,) + conv_w.shape)
    full["conv_w"] = jnp.concatenate([conv_all[2 * j] for j in range(N_CHIPS)], axis=2)
    for n in SMALL:
        full[n] = w[n]

    c_all = _all_gather_small(c, "gather_c").reshape(N_DEV, d)
    c_act = _silu_rows(_pad_to(c_all, ADA_PAD, 0), "silu_c")
    b_cols = lax.dynamic_slice_in_dim(b_ada, chip * ada_cols, ada_cols, axis=1)
    mod_cols = jnp.stack([
        _mm(c_act, w_ada[l], acc_in=jnp.broadcast_to(b_cols[l][None], (ADA_PAD, ada_cols)), name=f"ada_l{l}")[:N_DEV]
        for l in range(depth)])
    mod_all = _all_gather_small(mod_cols.reshape(depth * N_DEV, ada_cols), "gather_mod")
    mod_all = mod_all.reshape(N_DEV, depth, N_DEV, ada_cols)
    mods = []
    for l in range(depth):
        mine = jnp.concatenate([lax.dynamic_index_in_dim(mod_all[2 * j, l], dev, axis=0, keepdims=True)
                                for j in range(N_CHIPS)], axis=1)
        mods.append(tuple(mine[:, i * d:(i + 1) * d] for i in range(6)))

    tabs = _rope_tables(positions[0])
    start_group("l0a", 0, FIRST_NEEDED, mods[depth - 1][5][:, :LANES] + full["conv_w"].reshape(1, -1)[:, :LANES])
    tie = start_group("l0b", 0, LATER_NEEDED, pass_to_sibling("l0a", None))

    def late_weights(key, next_key, next_layer, behind):
        tok = pass_to_sibling(key, behind)
        started = 0.0 if next_key is None else start_group(next_key, next_layer, FIRST_NEEDED, tok)
        return gathered(key), started

    def next_later_group(behind):
        return start_group("l1b", 1, LATER_NEEDED, pass_to_sibling("l1a", behind))

    wts, saved = [None] * depth, [None] * depth
    tied = (mods[0][0] + tie,) + mods[0][1:]
    h, saved[0], wts[0] = _layer_fwd(x[0], tied, _layer_weights(gathered("l0a"), full, 0), tabs, "l0",
                                     functools.partial(late_weights, "l0b", "l1a", 1), next_later_group)
    h, saved[1], wts[1] = _layer_fwd(h, mods[1], _layer_weights(gathered("l1a"), full, 1), tabs, "l1",
                                     functools.partial(late_weights, "l1b", None, None))
    loss_part, dh, dfn = _loss_head(h, final_norm[None], loss_target[0], "loss_head")
    dfn = dfn[0]

    def col_shards(g):
        return g.reshape(g.shape[0], N_CHIPS, g.shape[1] // N_CHIPS).transpose(1, 0, 2)

    def reduce_scatter_stages(tag, g, knames, names):
        srcs = [g[n].reshape(N_CHIPS, -1, g[n].shape[1]) if n in ROW_SHARDED else g[n] for n in knames]
        axes = [1 if n in ROW_SHARDED else 0 for n in knames]
        halves = [a.shape[ax] // 2 for a, ax in zip(srcs, axes)]
        got_shapes = [a.shape[:ax] + (kh,) + a.shape[ax + 1:] for a, ax, kh in zip(srcs, axes, halves)]
        plan_a, plan_c = _to_sibling_plan(halves, axes), _to_chips_plan(len(names))
        st, out = {}, {}
        st["a"] = _copies_start(f"{tag}_sibling_start", srcs, [lax.empty(sh, BF16) for sh in got_shapes], plan_a, len(srcs))

        def after_or(tok, after):
            return tok if after is None else after

        def stage0(after):
            send_s, recv_s, landing, tok = st["a"]
            got = _copies_wait(f"{tag}_sibling_wait", srcs, landing, send_s, recv_s, plan_a, after_or(tok, after))
            sums = {}
            for n, a, b in zip(knames, srcs, got):
                a3, b3 = (v if v.ndim == 3 else v[None] for v in (a, b))
                r = _add_half(a3, b3, half_idx, n in COL_SHARDED_AS_IS, f"{tag}_add_{n}")
                sums[n] = r if (n in COL_SHARDED_AS_IS or n in ROW_SHARDED) else r[0]
            if "w_main" in sums:
                sums["w_in"] = col_shards(_unlay_w_in(sums["w_main"], sums["w_small"]))
                sums["w_uq"] = col_shards(_unlay_w_uq(sums["w_uq"]))
                sums["w_ukv"] = col_shards(_unlay_w_ukv(sums["w_ukv"]))
            st["p"] = [sums[n] for n in names]
            st["c"] = _copies_start(f"{tag}_chips_start", st["p"], [lax.empty((3,) + p.shape[1:], BF16) for p in st["p"]],
                                    plan_c, 3 * len(names))
            return st["c"][3][0, 0]

        def stage1(after):
            send_s, recv_s, landing, tok = st["c"]
            got = _copies_wait(f"{tag}_chips_wait", st["p"], landing, send_s, recv_s, plan_c, after_or(tok, after))
            sums = [_sum_chips(p, q, chip_idx, half_idx, f"{tag}_sum_{n}") for n, p, q in zip(names, st["p"], got)]
            plan_e = _swap_plan([r.shape[0] // 2 for r in sums])
            st["e"] = _copies_start(f"{tag}_swap_start", [], sums, plan_e, len(names)) + (plan_e,)
            return st["e"][3][0, 0]

        def stage2(after):
            send_s, recv_s, landing, tok, plan_e = st["e"]
            got = _copies_wait(f"{tag}_swap_wait", [], landing, send_s, recv_s, plan_e, after_or(tok, after))
            out.update(zip(names, got))

        return (stage0, stage1, stage2), out, st["a"][3][0, 0]

    dmods, grads, groups = [None] * depth, [None] * depth, {}

    def ffn_group_l1(g):
        groups["l1_ffn"] = reduce_scatter_stages("rs_l1_ffn", g, FFN_GRADS, FFN_GRADS)
        return groups["l1_ffn"][2]

    dh, dmods[1], grads[1] = _layer_bwd(dh, saved[1], mods[1], wts[1], tabs, "l1", after_ffn=ffn_group_l1)
    groups["l1_mix"] = reduce_scatter_stages("rs_l1_mix", grads[1], MIXER_GRADS_KERNEL, MIXER_GRADS)
    tied = mods[0][:5] + (mods[0][5] + groups["l1_mix"][2],)

    def ffn_group_l0(g):
        behind = g["w_gate_up"]
        tok = groups["l1_ffn"][0][0](behind) + groups["l1_mix"][0][0](behind)
        groups["l0_ffn"] = reduce_scatter_stages("rs_l0_ffn", g, FFN_GRADS, FFN_GRADS)
        return tok + groups["l0_ffn"][2]

    def after_gdn_l0(behind):
        return groups["l0_ffn"][0][0](behind)

    dx, dmods[0], grads[0] = _layer_bwd(dh, saved[0], tied, wts[0], tabs, "l0", after_ffn=ffn_group_l0, after_gdn=after_gdn_l0)
    groups["l0_mix"] = reduce_scatter_stages("rs_l0_mix", grads[0], MIXER_GRADS_KERNEL, MIXER_GRADS)
    for key in ("l1_ffn", "l1_mix", "l0_ffn"):
        groups[key][0][1](dx)
    g_out, deltas, new_m, new_v = {}, {}, {}, {}

    def reduced(names):
        for n in names:
            g_out[n] = [groups[f"l{l}_ffn" if n in FFN_GRADS else f"l{l}_mix"][1][n] for l in range(depth)]

    def update(names):
        for n in names:
            g_out[n], deltas[n], new_m[n], new_v[n] = _adamw(w[n], g_out[n], env["m_" + n], env["v_" + n], f"adamw_{n}")

    small = [_small_grads_ref_layout(grads[l]) for l in range(depth)]
    small_parts = [jnp.concatenate(dmods[l], axis=1).reshape(-1) for l in range(depth)]
    small_parts += [jnp.stack([small[l][n] for l in range(depth)]).reshape(-1) for n in SMALL]
    small_parts += [dfn, loss_part[0, :1]]
    small_sizes = [p.shape[0] for p in small_parts]
    packed = jnp.concatenate(small_parts)
    n_small = -(-packed.shape[0] // LANES) * LANES
    small_all = _all_gather_small(_pad_to(packed, n_small, 0).reshape(1, n_small), "gather_small_grads")
    small_sum = _sum_devices(small_all, "sum_small_grads")[0]
    offs = [0]
    for sz in small_sizes:
        offs.append(offs[-1] + sz)
    g_out["b_ada"] = jnp.stack([small_sum[offs[l]:offs[l + 1]] for l in range(depth)])
    for i, n in enumerate(SMALL):
        g_out[n] = small_sum[offs[depth + i]:offs[depth + i + 1]].reshape(w[n].shape)
    g_out["final_norm"] = small_sum[offs[depth + len(SMALL)]:offs[depth + len(SMALL) + 1]]
    loss = small_sum[offs[depth + len(SMALL) + 1]]

    c_act_t = _pad_to(c_act[:N_DEV].T, K_PAD, 1)
    g_ada = []
    for l in range(depth):
        dmod_l = small_all[:, 0, offs[l]:offs[l + 1]]
        dmod_cols = lax.dynamic_slice_in_dim(dmod_l, chip * ada_cols, ada_cols, axis=1)
        g_ada.append(_mm(c_act_t, _pad_to(dmod_cols, K_PAD, 0), name=f"ada_grad_l{l}"))
    g_out["w_ada"] = jnp.stack(g_ada)

    conv_g = jnp.stack([small[l]["conv_w"] for l in range(depth)])
    conv_all_g = _all_gather_small(conv_g.reshape(1, -1), "gather_conv_grads")
    conv_sum = _sum_devices(conv_all_g, "sum_conv_grads").reshape(conv_g.shape)
    n_cc = conv_w.shape[2]
    g_out["conv_w"] = lax.dynamic_slice_in_dim(conv_sum, chip * n_cc, n_cc, axis=2)

    mix0 = groups["l0_mix"][0]
    mix0[0](conv_sum.reshape(-1)[:LANES] + small_sum[:LANES])
    for key in ("l1_ffn", "l1_mix", "l0_ffn"):
        groups[key][0][2](None)
    reduced(FFN_GRADS)
    first_updates = ("w_ada", "b_ada", "final_norm", "conv_w") + SMALL + FFN_GRADS
    update(first_updates)
    corner = lambda a: a.reshape((1,) * (3 - a.ndim) + a.shape)[0, :1, :LANES]
    mix0[1](sum(corner(deltas[n]) for n in first_updates if w[n].shape[-1] >= LANES))
    mix0[2](None)
    reduced(MIXER_GRADS)
    update(MIXER_GRADS)
    return (loss, dx[None], *[g_out[n] for n in WEIGHTS], *[deltas[n] for n in WEIGHTS],
            *[new_m[n] for n in WEIGHTS], *[new_v[n] for n in WEIGHTS])
```

```python
import functools
import math

import jax
import jax.numpy as jnp
from jax import lax
from jax.experimental import pallas as pl
from jax.experimental.pallas import tpu as pltpu

F32 = jnp.float32
BF16 = jnp.bfloat16
SDS = jax.ShapeDtypeStruct
MESH = pl.DeviceIdType.MESH
AXES = ("x", "y", "c")

EPS = 1e-6
HEADS = 8
D_NOPE = 128
D_ROPE = 64
D_QK = 256
D_V = 128
Q_LORA = 512
KV_LORA = 512
CHUNK = 64
CONV_WIDTH = 4
GDN_W = HEADS * D_V
MAIN_FIXED = 2 * Q_LORA + 4 * GDN_W
LANES = 128
VMEM_LIMIT = 56 * 1024 * 1024
ADAM_LR, ADAM_B1, ADAM_B2, ADAM_EPS, ADAM_WD, ADAM_STEP = 0.001, 0.9, 0.999, 1e-8, 0.01, 10


def _pick(n, cands):
    for cand in cands:
        if n % cand == 0:
            return cand
    return n


def _cp(sem):
    return pltpu.CompilerParams(dimension_semantics=sem, vmem_limit_bytes=VMEM_LIMIT)


def _row_tile(t):
    return _pick(t, (256, 128, 64, 32, 16, 8))


def _dot(a, b):
    return jnp.dot(a.astype(BF16), b.astype(BF16), preferred_element_type=F32)


def _dot_nt(a, b):
    return lax.dot_general(a.astype(BF16), b.astype(BF16), (((1,), (1,)), ((), ())), preferred_element_type=F32)


def _dot_tn(a, b):
    return lax.dot_general(a.astype(BF16), b.astype(BF16), (((0,), (0,)), ((), ())), preferred_element_type=F32)


def _bdot(a, b, exact=False):
    if exact:
        return lax.dot_general(a, b, (((2,), (1,)), ((0,), (0,))), precision=lax.Precision.HIGHEST,
                               preferred_element_type=F32)
    return lax.dot_general(a.astype(BF16), b.astype(BF16), (((2,), (1,)), ((0,), (0,))), preferred_element_type=F32)


def _bdot_nt(a, b):
    return lax.dot_general(a.astype(BF16), b.astype(BF16), (((2,), (2,)), ((0,), (0,))), preferred_element_type=F32)


def _sigmoid(x):
    return 1.0 / (1.0 + jnp.exp(-x))


def _mm(a, b, *, tb=False, out_dtype=F32, acc_in=None, name):
    m, k = a.shape
    n = b.shape[0] if tb else b.shape[1]
    assert (b.shape[1] if tb else b.shape[0]) == k
    tm = _pick(m, (1024, 512, 256, 128))
    tn = _pick(n, (1024, 512, 256, 128))
    tk = k if k <= 2048 else _pick(k, (1024, 512, 256, 128))
    nk = k // tk
    has_acc = acc_in is not None

    def body_one_step(*refs):
        a_ref, b_ref = refs[:2]
        o_ref = refs[-1]
        acc = _dot_nt(a_ref[...], b_ref[...]) if tb else _dot(a_ref[...], b_ref[...])
        if has_acc:
            acc = acc + refs[2][...].astype(F32)
        o_ref[...] = acc.astype(out_dtype)

    if nk == 1:
        in_specs = [pl.BlockSpec((tm, k), lambda i, j: (i, 0)),
                    pl.BlockSpec((tn, k), lambda i, j: (j, 0)) if tb else pl.BlockSpec((k, tn), lambda i, j: (0, j))]
        args = [a, b]
        if has_acc:
            in_specs.append(pl.BlockSpec((tm, tn), lambda i, j: (i, j)))
            args.append(acc_in)
        return pl.pallas_call(
            body_one_step, grid=(m // tm, n // tn), in_specs=in_specs, out_specs=pl.BlockSpec((tm, tn), lambda i, j: (i, j)),
            out_shape=SDS((m, n), out_dtype), name=name, compiler_params=_cp(("parallel", "parallel")))(*args)

    def body(*refs):
        if has_acc:
            a_ref, b_ref, c_ref, o_ref, acc = refs
        else:
            a_ref, b_ref, o_ref, acc = refs
        kk = pl.program_id(2)

        @pl.when(kk == 0)
        def _():
            if has_acc:
                acc[...] = c_ref[...].astype(F32)
            else:
                acc[...] = jnp.zeros_like(acc)

        if tb:
            acc[...] += _dot_nt(a_ref[...], b_ref[...])
        else:
            acc[...] += _dot(a_ref[...], b_ref[...])

        @pl.when(kk == nk - 1)
        def _():
            o_ref[...] = acc[...].astype(out_dtype)

    in_specs = [pl.BlockSpec((tm, tk), lambda i, j, kk: (i, kk)),
                pl.BlockSpec((tn, tk), lambda i, j, kk: (j, kk)) if tb
                else pl.BlockSpec((tk, tn), lambda i, j, kk: (kk, j))]
    args = [a, b]
    if has_acc:
        in_specs.append(pl.BlockSpec((tm, tn), lambda i, j, kk: (i, j)))
        args.append(acc_in)
    return pl.pallas_call(
        body, grid=(m // tm, n // tn, nk), in_specs=in_specs,
        out_specs=pl.BlockSpec((tm, tn), lambda i, j, kk: (i, j)),
        out_shape=SDS((m, n), out_dtype), scratch_shapes=[pltpu.VMEM((tm, tn), F32)],
        name=name, compiler_params=_cp(("parallel", "parallel", "arbitrary")))(*args)


def _norm_fwd(x, col, width, w, sc, sh, name):
    t = x.shape[0]
    tm = _row_tile(t)

    def body(x_ref, w_ref, sc_ref, sh_ref, o_ref):
        xv = x_ref[...]
        r = lax.rsqrt(jnp.mean(xv * xv, axis=-1, keepdims=True) + EPS)
        n = xv * r * w_ref[...]
        o_ref[...] = (n * (1.0 + sc_ref[...]) + sh_ref[...]).astype(o_ref.dtype)

    row = pl.BlockSpec((1, width), lambda i: (0, 0))
    return pl.pallas_call(
        body, grid=(t // tm,), in_specs=[pl.BlockSpec((tm, width), lambda i: (i, col)), row, row, row],
        out_specs=pl.BlockSpec((tm, width), lambda i: (i, 0)), out_shape=SDS((t, width), BF16),
        name=name, compiler_params=_cp(("parallel",)))(x, w, sc, sh)


def _norm_bwd(x, col, width, w, sc, dh, dres, out_dtype, name):
    t = x.shape[0]
    tm = _row_tile(t)
    has_res = dres is not None

    def body(*refs):
        if has_res:
            x_ref, w_ref, sc_ref, dh_ref, dres_ref, dx_ref, dw_ref, dsc_ref, dsh_ref = refs
        else:
            x_ref, w_ref, sc_ref, dh_ref, dx_ref, dw_ref, dsc_ref, dsh_ref = refs

        @pl.when(pl.program_id(0) == 0)
        def _():
            dw_ref[...] = jnp.zeros_like(dw_ref)
            dsc_ref[...] = jnp.zeros_like(dsc_ref)
            dsh_ref[...] = jnp.zeros_like(dsh_ref)

        xv = x_ref[...]
        dhv = dh_ref[...].astype(F32)
        wv = w_ref[...]
        r = lax.rsqrt(jnp.mean(xv * xv, axis=-1, keepdims=True) + EPS)
        xh = xv * r
        n = xh * wv
        dsh_ref[...] += jnp.sum(dhv, axis=0, keepdims=True)
        dsc_ref[...] += jnp.sum(dhv * n, axis=0, keepdims=True)
        dn = dhv * (1.0 + sc_ref[...])
        dw_ref[...] += jnp.sum(dn * xh, axis=0, keepdims=True)
        dxh = dn * wv
        dx = r * (dxh - xh * jnp.mean(dxh * xh, axis=-1, keepdims=True))
        if has_res:
            dx = dx + dres_ref[...]
        dx_ref[...] = dx.astype(out_dtype)

    row = pl.BlockSpec((1, width), lambda i: (0, 0))
    blk = pl.BlockSpec((tm, width), lambda i: (i, 0))
    in_specs = [pl.BlockSpec((tm, width), lambda i: (i, col)), row, row, blk]
    args = [x, w, sc, dh]
    if has_res:
        in_specs.append(blk)
        args.append(dres)
    return pl.pallas_call(
        body, grid=(t // tm,), in_specs=in_specs, out_specs=[blk, row, row, row],
        out_shape=[SDS((t, width), out_dtype), SDS((1, width), F32), SDS((1, width), F32), SDS((1, width), F32)],
        name=name, compiler_params=_cp(("arbitrary",)))(*args)


def _rope128(x, tc, ts1, ts2):
    return x * tc + pltpu.roll(x, 96, 1) * ts1 + pltpu.roll(x, 32, 1) * ts2


def _rope128_t(d, tc, ts1, ts2):
    return d * tc + pltpu.roll(d * ts1, 32, 1) + pltpu.roll(d * ts2, 96, 1)


def _rope_q(q_raw, tc, ts1, ts2, transpose, out_dtype, name):
    t = q_raw.shape[0]
    tm = _row_tile(t)

    def body(q_ref, tc_ref, s1_ref, s2_ref, o_ref):
        fn = _rope128_t if transpose else _rope128
        for h in range(HEADS):
            base = h * D_QK
            o_ref[:, base:base + LANES] = q_ref[:, base:base + LANES].astype(out_dtype)
            x = q_ref[:, base + LANES:base + D_QK].astype(F32)
            o_ref[:, base + LANES:base + D_QK] = fn(x, tc_ref[...], s1_ref[...], s2_ref[...]).astype(out_dtype)

    blk = pl.BlockSpec((tm, HEADS * D_QK), lambda i: (i, 0))
    tab = pl.BlockSpec((tm, LANES), lambda i: (i, 0))
    return pl.pallas_call(
        body, grid=(t // tm,), in_specs=[blk, tab, tab, tab], out_specs=blk,
        out_shape=SDS((t, HEADS * D_QK), out_dtype), name=name, compiler_params=_cp(("parallel",)))(q_raw, tc, ts1, ts2)


def _k_assemble(kv_raw, p_small, tc, ts1, ts2, name):
    t = kv_raw.shape[0]
    tm = _row_tile(t)

    def body(kn_ref, ps_ref, tc_ref, s1_ref, s2_ref, o_ref):
        kpe = _rope128(ps_ref[...], tc_ref[...], s1_ref[...], s2_ref[...]).astype(BF16)
        for h in range(HEADS):
            o_ref[:, h * D_QK:h * D_QK + LANES] = kn_ref[:, h * LANES:(h + 1) * LANES].astype(BF16)
            o_ref[:, h * D_QK + LANES:(h + 1) * D_QK] = kpe

    tab = pl.BlockSpec((tm, LANES), lambda i: (i, 0))
    return pl.pallas_call(
        body, grid=(t // tm,),
        in_specs=[pl.BlockSpec((tm, HEADS * LANES), lambda i: (i, 0)), tab, tab, tab, tab],
        out_specs=pl.BlockSpec((tm, HEADS * D_QK), lambda i: (i, 0)),
        out_shape=SDS((t, HEADS * D_QK), BF16), name=name, compiler_params=_cp(("parallel",)))(kv_raw, p_small, tc, ts1, ts2)


def _k_assemble_bwd(dk, dv, tc, ts1, ts2, name):
    t = dk.shape[0]
    tm = _row_tile(t)

    def body(dk_ref, dv_ref, tc_ref, s1_ref, s2_ref, o_ref, pe_ref):
        acc = jnp.zeros((tm, LANES), F32)
        for h in range(HEADS):
            o_ref[:, h * LANES:(h + 1) * LANES] = dk_ref[:, h * D_QK:h * D_QK + LANES].astype(BF16)
            acc = acc + dk_ref[:, h * D_QK + LANES:(h + 1) * D_QK].astype(F32)
        o_ref[:, HEADS * LANES:] = dv_ref[...].astype(BF16)
        pe_ref[...] = _rope128_t(acc, tc_ref[...], s1_ref[...], s2_ref[...])

    tab = pl.BlockSpec((tm, LANES), lambda i: (i, 0))
    return pl.pallas_call(
        body, grid=(t // tm,),
        in_specs=[pl.BlockSpec((tm, HEADS * D_QK), lambda i: (i, 0)), pl.BlockSpec((tm, HEADS * LANES), lambda i: (i, 0)),
                  tab, tab, tab],
        out_specs=[pl.BlockSpec((tm, 2 * HEADS * LANES), lambda i: (i, 0)), tab],
        out_shape=[SDS((t, 2 * HEADS * LANES), BF16), SDS((t, LANES), F32)],
        name=name, compiler_params=_cp(("parallel",)))(dk, dv, tc, ts1, ts2)


def _attn_tile(t):
    return _pick(t, (256, 128, 64))


def _attn_fwd(q, k, v, v_off, name):
    t = q.shape[0]
    tq = _attn_tile(t)
    scale = (D_NOPE + D_ROPE) ** -0.5

    def body(q_ref, k_ref, v_ref, o_ref, lse_ref):
        for i in range(t // tq):
            n_k = (i + 1) * tq
            s = _dot_nt(q_ref[i * tq:(i + 1) * tq, :], k_ref[0:n_k, :]) * scale
            row = lax.broadcasted_iota(jnp.int32, (tq, n_k), 0) + i * tq
            colv = lax.broadcasted_iota(jnp.int32, (tq, n_k), 1)
            s = jnp.where(colv <= row, s, -jnp.inf)
            m = jnp.max(s, axis=-1, keepdims=True)
            p = jnp.exp(s - m)
            l = jnp.sum(p, axis=-1, keepdims=True)
            o = _dot(p, v_ref[0:n_k, :]) / l
            o_ref[i * tq:(i + 1) * tq, :] = o.astype(BF16)
            lse_ref[0, i * tq:(i + 1) * tq, :] = m + jnp.log(l)

    return pl.pallas_call(
        body, grid=(HEADS,),
        in_specs=[pl.BlockSpec((t, D_QK), lambda h: (0, h)), pl.BlockSpec((t, D_QK), lambda h: (0, h)),
                  pl.BlockSpec((t, D_V), lambda h: (0, v_off + h))],
        out_specs=[pl.BlockSpec((t, D_V), lambda h: (0, h)), pl.BlockSpec((1, t, 1), lambda h: (h, 0, 0))],
        out_shape=[SDS((t, HEADS * D_V), BF16), SDS((HEADS, t, 1), F32)],
        name=name, compiler_params=_cp(("parallel",)))(q, k, v)


def _attn_bwd(q, k, v, v_off, o, lse, do, name):
    t = q.shape[0]
    tq = _attn_tile(t)
    scale = (D_NOPE + D_ROPE) ** -0.5

    def body(q_ref, k_ref, v_ref, o_ref, lse_ref, do_ref, dq_ref, dk_ref, dv_ref):
        dk_ref[...] = jnp.zeros_like(dk_ref)
        dv_ref[...] = jnp.zeros_like(dv_ref)
        for i in range(t // tq):
            n_k = (i + 1) * tq
            rows = slice(i * tq, (i + 1) * tq)
            qi = q_ref[rows, :]
            doi = do_ref[rows, :].astype(F32)
            s = _dot_nt(qi, k_ref[0:n_k, :]) * scale
            row = lax.broadcasted_iota(jnp.int32, (tq, n_k), 0) + i * tq
            colv = lax.broadcasted_iota(jnp.int32, (tq, n_k), 1)
            p = jnp.where(colv <= row, jnp.exp(s - lse_ref[0, rows, :]), 0.0)
            dp = _dot_nt(doi, v_ref[0:n_k, :])
            delta = jnp.sum(doi * o_ref[rows, :].astype(F32), axis=-1, keepdims=True)
            ds = p * (dp - delta) * scale
            dq_ref[rows, :] = _dot(ds, k_ref[0:n_k, :])
            dk_ref[0:n_k, :] += _dot_tn(ds, qi)
            dv_ref[0:n_k, :] += _dot_tn(p, doi)

    qk_spec = pl.BlockSpec((t, D_QK), lambda h: (0, h))
    v_spec = pl.BlockSpec((t, D_V), lambda h: (0, h))
    return pl.pallas_call(
        body, grid=(HEADS,),
        in_specs=[qk_spec, qk_spec, pl.BlockSpec((t, D_V), lambda h: (0, v_off + h)), v_spec,
                  pl.BlockSpec((1, t, 1), lambda h: (h, 0, 0)), v_spec],
        out_specs=[qk_spec, qk_spec, v_spec],
        out_shape=[SDS((t, HEADS * D_QK), F32), SDS((t, HEADS * D_QK), F32), SDS((t, HEADS * D_V), F32)],
        name=name, compiler_params=_cp(("parallel",)))(q, k, v, o, lse, do)


CONV_COLS = 256


def _conv_pre(u, w_ref, rowi):
    acc = u * w_ref[CONV_WIDTH - 1:CONV_WIDTH, :]
    for sft in range(1, CONV_WIDTH):
        shifted = jnp.where(rowi >= sft, pltpu.roll(u, sft, 0), 0.0)
        acc = acc + shifted * w_ref[CONV_WIDTH - 1 - sft:CONV_WIDTH - sft, :]
    return acc


def _conv_fwd(p_main, conv_w, name):
    t = p_main.shape[0]
    off = 2 * Q_LORA // CONV_COLS

    def body(u_ref, w_ref, y_ref):
        u = u_ref[...]
        rowi = lax.broadcasted_iota(jnp.int32, u.shape, 0)
        pre = _conv_pre(u, w_ref, rowi)
        y_ref[...] = pre * _sigmoid(pre)

    return pl.pallas_call(
        body, grid=(3 * GDN_W // CONV_COLS,),
        in_specs=[pl.BlockSpec((t, CONV_COLS), lambda j: (0, off + j)), pl.BlockSpec((CONV_WIDTH, CONV_COLS), lambda j: (0, j))],
        out_specs=pl.BlockSpec((t, CONV_COLS), lambda j: (0, j)), out_shape=SDS((t, 3 * GDN_W), F32),
        name=name, compiler_params=_cp(("parallel",)))(p_main, conv_w)


def _conv_bwd(p_main, conv_w, dyc, name):
    t = p_main.shape[0]
    off = 2 * Q_LORA // CONV_COLS

    def body(u_ref, w_ref, dy_ref, du_ref, dw_ref):
        u = u_ref[...]
        rowi = lax.broadcasted_iota(jnp.int32, u.shape, 0)
        pre = _conv_pre(u, w_ref, rowi)
        sg = _sigmoid(pre)
        dpre = dy_ref[...] * sg * (1.0 + pre * (1.0 - sg))
        du = dpre * w_ref[CONV_WIDTH - 1:CONV_WIDTH, :]
        dw_ref[CONV_WIDTH - 1:CONV_WIDTH, :] = jnp.sum(dpre * u, axis=0, keepdims=True)
        for sft in range(1, CONV_WIDTH):
            back = jnp.where(rowi < t - sft, pltpu.roll(dpre, t - sft, 0), 0.0)
            du = du + back * w_ref[CONV_WIDTH - 1 - sft:CONV_WIDTH - sft, :]
            shifted = jnp.where(rowi >= sft, pltpu.roll(u, sft, 0), 0.0)
            dw_ref[CONV_WIDTH - 1 - sft:CONV_WIDTH - sft, :] = jnp.sum(dpre * shifted, axis=0, keepdims=True)
        du_ref[...] = du.astype(BF16)

    blk = pl.BlockSpec((t, CONV_COLS), lambda j: (0, j))
    wblk = pl.BlockSpec((CONV_WIDTH, CONV_COLS), lambda j: (0, j))
    return pl.pallas_call(
        body, grid=(3 * GDN_W // CONV_COLS,),
        in_specs=[pl.BlockSpec((t, CONV_COLS), lambda j: (0, off + j)), wblk, blk],
        out_specs=[blk, wblk], out_shape=[SDS((t, 3 * GDN_W), BF16), SDS((CONV_WIDTH, 3 * GDN_W), F32)],
        name=name, compiler_params=_cp(("parallel",)))(p_main, conv_w, dyc)


B_LO, A_LO, A_HI = D_ROPE, D_ROPE + HEADS, D_ROPE + 2 * HEADS


def _softplus(z):
    e = jnp.exp(-jnp.abs(z))
    log1p = jnp.where(e < 0.01, e * (1.0 - e * (0.5 - e * (1.0 / 3.0))), jnp.log(1.0 + e))
    return jnp.maximum(z, 0.0) + log1p


def _gdn_gates(p_small, a_row, dt_row, name):
    t = p_small.shape[0]

    def body(ps_ref, a_ref, dt_ref, g_ref, gc_ref):
        x = ps_ref[...]
        lane = lax.broadcasted_iota(jnp.int32, x.shape, 1)
        is_g = (lane >= A_LO) & (lane < A_HI)
        g = jnp.where(is_g, -jnp.exp(a_ref[...]) * _softplus(x + dt_ref[...]), 0.0)
        g_ref[...] = jnp.where(is_g, g, _sigmoid(x))
        pos = lax.broadcasted_iota(jnp.int32, x.shape, 0) % CHUNK
        acc = g
        sft = 1
        while sft < CHUNK:
            acc = acc + jnp.where(pos >= sft, pltpu.roll(acc, sft, 0), 0.0)
            sft *= 2
        gc_ref[...] = acc

    full = pl.BlockSpec((t, LANES), lambda i: (0, 0))
    row = pl.BlockSpec((1, LANES), lambda i: (0, 0))
    return pl.pallas_call(
        body, grid=(1,), in_specs=[full, row, row], out_specs=[full, full],
        out_shape=[SDS((t, LANES), F32), SDS((t, LANES), F32)], name=name,
        compiler_params=_cp(("arbitrary",)))(p_small, a_row, dt_row)


def _gdn_gates_bwd(p_small, a_row, dt_row, gates, dgates, dkpe, name):
    t = p_small.shape[0]

    def body(ps_ref, a_ref, dt_ref, g_ref, db_ref, dkpe_ref, dp_ref, da_ref, ddt_ref):
        x = ps_ref[...]
        lane = lax.broadcasted_iota(jnp.int32, x.shape, 1)
        is_g = (lane >= A_LO) & (lane < A_HI)
        is_b = (lane >= B_LO) & (lane < A_LO)
        pos = lax.broadcasted_iota(jnp.int32, x.shape, 0) % CHUNK
        acc = jnp.where(is_g, db_ref[...], 0.0)
        sft = 1
        while sft < CHUNK:
            acc = acc + jnp.where(pos < CHUNK - sft, pltpu.roll(acc, t - sft, 0), 0.0)
            sft *= 2
        dg = acc
        gv = g_ref[...]
        dz = jnp.where(is_g, dg * (-jnp.exp(a_ref[...])) * _sigmoid(x + dt_ref[...]), 0.0)
        da_ref[...] = jnp.sum(jnp.where(is_g, dg * gv, 0.0), axis=0, keepdims=True)
        ddt_ref[...] = jnp.sum(dz, axis=0, keepdims=True)
        dlb = jnp.where(is_b, db_ref[...] * gv * (1.0 - gv), 0.0)
        dp_ref[...] = (jnp.where(lane < D_ROPE, dkpe_ref[...], 0.0) + dlb + dz).astype(BF16)

    full = pl.BlockSpec((t, LANES), lambda i: (0, 0))
    row = pl.BlockSpec((1, LANES), lambda i: (0, 0))
    return pl.pallas_call(
        body, grid=(1,), in_specs=[full, row, row, full, full, full], out_specs=[full, row, row],
        out_shape=[SDS((t, LANES), BF16), SDS((1, LANES), F32), SDS((1, LANES), F32)], name=name,
        compiler_params=_cp(("arbitrary",)))(p_small, a_row, dt_row, gates, dgates, dkpe)


def _tri_inv(l, eye):
    x = eye - l
    p = _bdot(l, l, exact=True)
    steps = int(math.log2(CHUNK)) - 1
    for s in range(steps):
        x = x + _bdot(x, p, exact=True)
        if s < steps - 1:
            p = _bdot(p, p, exact=True)
    return x


def _l2n(x3):
    r = lax.rsqrt(jnp.sum(x3 * x3, axis=-1, keepdims=True) + EPS)
    return x3 * r, r


def _head_col(a_ref, lane_lo, n):
    a = a_ref[...]
    lane = lax.broadcasted_iota(jnp.int32, a.shape, 1)
    col = jnp.sum(jnp.where(lane == lane_lo + pl.program_id(0), a, 0.0), axis=-1, keepdims=True)
    return col.reshape(n, CHUNK, 1)


def _gdn_common(q3, k3, v3, b, gc):
    n = q3.shape[0]
    ri = lax.broadcasted_iota(jnp.int32, (n, CHUNK, CHUNK), 1)
    ci = lax.broadcasted_iota(jnp.int32, (n, CHUNK, CHUNK), 2)
    lower, strict = ri >= ci, ri > ci
    eye = (ri == ci).astype(F32)
    gr = jnp.sum(gc * eye, axis=1, keepdims=True)
    qh, rq = _l2n(q3)
    qn = qh * (D_V ** -0.5)
    kn, rk = _l2n(k3)
    dec = jnp.where(lower, jnp.exp(jnp.where(lower, gc - gr, 0.0)), 0.0)
    kb = kn * b
    mm = _bdot_nt(kb, kn)
    tinv = _tri_inv(jnp.where(strict, mm * dec, 0.0), eye)
    gam = jnp.exp(gc)
    u = _bdot(tinv, v3 * b, exact=True)
    w = _bdot(tinv, kb * gam, exact=True)
    qk = _bdot_nt(qn, kn)
    aqk = jnp.where(lower, qk * dec, 0.0)
    gl = gc[:, CHUNK - 1:CHUNK, :]
    kdf = jnp.exp(gl - gc)
    return dict(ri=ri, ci=ci, lower=lower, strict=strict, eye=eye, qh=qh, rq=rq, qn=qn, kn=kn, rk=rk, dec=dec, kb=kb,
                mm=mm, gam=gam, u=u, w=w, qk=qk, aqk=aqk, gl=gl, kdf=kdf, kd=kn * kdf, gr=gr, tinv=tinv)


def _gdn_fwd(yc, p_main, gates, gcum, gn, name):
    t = yc.shape[0]
    n = t // CHUNK
    z_off = (2 * Q_LORA + 3 * GDN_W) // D_V

    def body(q_ref, k_ref, v_ref, z_ref, gt_ref, gcum_ref, gn_ref, o_ref, g_ref, s_ref, u_s, w_s, qg_s, kd_s, a_s, e_s):
        c = _gdn_common(q_ref[...].reshape(n, CHUNK, D_V), k_ref[...].reshape(n, CHUNK, D_V),
                        v_ref[...].reshape(n, CHUNK, D_V), _head_col(gt_ref, B_LO, n), _head_col(gcum_ref, A_LO, n))
        u_s[...] = c["u"]
        w_s[...] = c["w"]
        qg_s[...] = c["qn"] * c["gam"]
        kd_s[...] = c["kd"]
        a_s[...] = c["aqk"]
        e_s[...] = jnp.broadcast_to(jnp.exp(c["gl"]), (n, 1, D_V))

        def step(i, s):
            s_ref[0, i] = s
            v_new = u_s[i] - _dot(w_s[i], s)
            o = _dot(qg_s[i], s) + _dot(a_s[i], v_new)
            o_ref[pl.ds(pl.multiple_of(i * CHUNK, CHUNK), CHUNK), :] = o
            return s * e_s[i] + _dot_tn(kd_s[i], v_new)

        lax.fori_loop(0, n, step, jnp.zeros((D_V, D_V), F32))
        o = o_ref[...]
        zz = z_ref[...]
        on = o * lax.rsqrt(jnp.mean(o * o, axis=-1, keepdims=True) + EPS) * gn_ref[...]
        g_ref[...] = (on * zz * _sigmoid(zz)).astype(BF16)

    col = lambda off: pl.BlockSpec((t, D_V), lambda h: (0, off + h))
    lanes = pl.BlockSpec((t, LANES), lambda h: (0, 0))
    big = pltpu.VMEM((n, CHUNK, D_V), F32)
    return pl.pallas_call(
        body, grid=(HEADS,),
        in_specs=[col(0), col(HEADS), col(2 * HEADS), col(z_off), lanes, lanes, pl.BlockSpec((1, D_V), lambda h: (0, 0))],
        out_specs=[col(0), col(0), pl.BlockSpec((1, n, D_V, D_V), lambda h: (h, 0, 0, 0))],
        out_shape=[SDS((t, GDN_W), F32), SDS((t, GDN_W), BF16), SDS((HEADS, n, D_V, D_V), F32)],
        scratch_shapes=[big, big, big, big, pltpu.VMEM((n, CHUNK, CHUNK), F32), pltpu.VMEM((n, 1, D_V), F32)],
        name=name, compiler_params=_cp(("parallel",)))(yc, yc, yc, p_main, gates, gcum, gn)


def _gdn_bwd(yc, p_main, gates, gcum, gn, o_raw, states, dgated, name):
    t = yc.shape[0]
    n = t // CHUNK
    z_off = (2 * Q_LORA + 3 * GDN_W) // D_V

    def body(q_ref, k_ref, v_ref, z_ref, gt_ref, gcum_ref, gn_ref, o_ref, s_ref, dg_ref,
             dq_ref, dk_ref, dv_ref, dz_ref, dgt_ref, dgn_ref,
             u_s, w_s, qg_s, kd_s, at_s, e_s, do_s, du_s, dw_s, dqg_s, dkd_s, da_s, dat_s, dgs_s):
        @pl.when(pl.program_id(0) == 0)
        def _():
            dgn_ref[...] = jnp.zeros_like(dgn_ref)
            dgt_ref[...] = jnp.zeros_like(dgt_ref)

        o = o_ref[...]
        zz = z_ref[...]
        dgv = dg_ref[...]
        gnv = gn_ref[...]
        r = lax.rsqrt(jnp.mean(o * o, axis=-1, keepdims=True) + EPS)
        oh = o * r
        sg = _sigmoid(zz)
        don = dgv * zz * sg
        dz_ref[...] = (dgv * oh * gnv * sg * (1.0 + zz * (1.0 - sg))).astype(BF16)
        dgn_ref[...] += jnp.sum(don * oh, axis=0, keepdims=True)
        doh = don * gnv
        do_s[...] = (r * (doh - oh * jnp.mean(doh * oh, axis=-1, keepdims=True))).reshape(n, CHUNK, D_V)

        q3 = q_ref[...].reshape(n, CHUNK, D_V)
        k3 = k_ref[...].reshape(n, CHUNK, D_V)
        v3 = v_ref[...].reshape(n, CHUNK, D_V)
        b, gc = _head_col(gt_ref, B_LO, n), _head_col(gcum_ref, A_LO, n)
        c = _gdn_common(q3, k3, v3, b, gc)
        gr = c["gr"]
        ri, ci = c["ri"], c["ci"]
        upper, sup = ci >= ri, ci > ri
        dect = jnp.where(upper, jnp.exp(jnp.where(upper, gr - gc, 0.0)), 0.0)
        tinv_t = lax.dot_general(c["eye"], c["tinv"], (((2,), (2,)), ((0,), (0,))), precision=lax.Precision.HIGHEST,
                                 preferred_element_type=F32)
        u_s[...] = c["u"]
        w_s[...] = c["w"]
        qg_s[...] = c["qn"] * c["gam"]
        kd_s[...] = c["kd"]
        at_s[...] = jnp.where(upper, _bdot_nt(c["kn"], c["qn"]) * dect, 0.0)
        e_s[...] = jnp.broadcast_to(jnp.exp(c["gl"]), (n, 1, D_V))

        def step(j, ds):
            i = n - 1 - j
            s = s_ref[0, i]
            do_i = do_s[i]
            v_new = u_s[i] - _dot(w_s[i], s)
            dvn = _dot(at_s[i], do_i) + _dot(kd_s[i], ds)
            da_s[i] = _dot_nt(do_i, v_new)
            dat_s[i] = _dot_nt(v_new, do_i)
            dqg_s[i] = _dot_nt(do_i, s)
            dw_s[i] = -_dot_nt(dvn, s)
            dkd_s[i] = _dot_nt(v_new, ds)
            du_s[i] = dvn
            dgs_s[i] = jnp.broadcast_to(jnp.sum(jnp.sum(s * ds, axis=1, keepdims=True), axis=0, keepdims=True), (1, D_V))
            return _dot_tn(qg_s[i], do_i) + e_s[i] * ds - _dot_tn(w_s[i], dvn)

        lax.fori_loop(0, n, step, jnp.zeros((D_V, D_V), F32))

        du, dw, dqg, dkd = du_s[...], dw_s[...], dqg_s[...], dkd_s[...]
        lower, strict, dec = c["lower"], c["strict"], c["dec"]
        kn, kb, qn, gam, kdf = c["kn"], c["kb"], c["qn"], c["gam"], c["kdf"]
        drv = _bdot(tinv_t, du, exact=True)
        drk = _bdot(tinv_t, dw, exact=True)
        dl = jnp.where(strict, -(_bdot_nt(drv, c["u"]) + _bdot_nt(drk, c["w"])), 0.0)
        dlt = jnp.where(sup, -(_bdot_nt(c["u"], drv) + _bdot_nt(c["w"], drk)), 0.0)
        da = jnp.where(lower, da_s[...], 0.0)
        dat = jnp.where(upper, dat_s[...], 0.0)
        e = (dl * c["mm"] + da * c["qk"]) * dec
        col_sums = jnp.sum(e, axis=1, keepdims=True)
        dgc = jnp.sum(e, axis=2, keepdims=True) - jnp.sum(col_sums * c["eye"], axis=2, keepdims=True)
        dkb = _bdot(dl * dec, kn) + gam * drk
        dkn = _bdot(dlt * dect, kb) + _bdot(dat * dect, qn) + b * dkb + dkd * kdf
        dqn = _bdot(da * dec, kn) + gam * dqg
        dgam = jnp.sum(drk * kb, axis=-1, keepdims=True) + jnp.sum(dqg * qn, axis=-1, keepdims=True)
        dbeta = jnp.sum(dkb * kn, axis=-1, keepdims=True) + jnp.sum(drv * v3, axis=-1, keepdims=True)
        dv_ref[...] = (b * drv).reshape(t, D_V)
        ee = jnp.sum(dkd * kn, axis=-1, keepdims=True) * kdf
        dgc = dgc + dgam * gam - ee
        rowc = lax.broadcasted_iota(jnp.int32, (n, CHUNK, 1), 1)
        tail = jnp.sum(ee, axis=1, keepdims=True) + dgs_s[...][:, :, 0:1] * jnp.exp(c["gl"])
        dgc = dgc + jnp.where(rowc == CHUNK - 1, tail, 0.0)
        lane = lax.broadcasted_iota(jnp.int32, (t, LANES), 1)
        head = pl.program_id(0)
        dgt_ref[...] += (jnp.where(lane == B_LO + head, dbeta.reshape(t, 1), 0.0)
                         + jnp.where(lane == A_LO + head, dgc.reshape(t, 1), 0.0))
        sc = D_V ** -0.5
        qh, rq, rk = c["qh"], c["rq"], c["rk"]
        dq_ref[...] = (rq * (sc * dqn - qh * jnp.sum(sc * dqn * qh, axis=-1, keepdims=True))).reshape(t, D_V)
        dk_ref[...] = (rk * (dkn - kn * jnp.sum(dkn * kn, axis=-1, keepdims=True))).reshape(t, D_V)

    once = pl.Buffered(1)
    col = lambda off: pl.BlockSpec((t, D_V), lambda h: (0, off + h), pipeline_mode=once)
    out_col = pl.BlockSpec((t, D_V), lambda h: (0, h))
    lanes = pl.BlockSpec((t, LANES), lambda h: (0, 0))
    row = pl.BlockSpec((1, D_V), lambda h: (0, 0))
    big = pltpu.VMEM((n, CHUNK, D_V), F32)
    sq = pltpu.VMEM((n, CHUNK, CHUNK), F32)
    small = pltpu.VMEM((n, 1, D_V), F32)
    return pl.pallas_call(
        body, grid=(HEADS,),
        in_specs=[col(0), col(HEADS), col(2 * HEADS), col(z_off), lanes, lanes, row, col(0),
                  pl.BlockSpec((1, n, D_V, D_V), lambda h: (h, 0, 0, 0), pipeline_mode=once), col(0)],
        out_specs=[out_col, out_col, out_col, out_col, lanes, row],
        out_shape=[SDS((t, GDN_W), F32), SDS((t, GDN_W), F32), SDS((t, GDN_W), F32), SDS((t, GDN_W), BF16),
                   SDS((t, LANES), F32), SDS((1, D_V), F32)],
        scratch_shapes=[big, big, big, big, sq, small, big, big, big, big, big, sq, sq, small],
        name=name, compiler_params=_cp(("arbitrary",)))(yc, yc, yc, p_main, gates, gcum, gn, o_raw, states, dgated)


def _col_tile(d):
    return _pick(d, (512, 256, 128))


def _mix_fwd(y_a, y_b, p_main, name):
    t, d = y_a.shape
    tm, cw = _row_tile(t), _col_tile(d)
    off_a, off_b = MAIN_FIXED // cw, (MAIN_FIXED + d) // cw

    def body(ya_ref, yb_ref, ga_ref, gb_ref, u_ref):
        u_ref[...] = (_sigmoid(ga_ref[...]) * ya_ref[...] + _sigmoid(gb_ref[...]) * yb_ref[...]).astype(BF16)

    blk = pl.BlockSpec((tm, cw), lambda i, j: (i, j))
    return pl.pallas_call(
        body, grid=(t // tm, d // cw),
        in_specs=[blk, blk, pl.BlockSpec((tm, cw), lambda i, j: (i, off_a + j)), pl.BlockSpec((tm, cw), lambda i, j: (i, off_b + j))],
        out_specs=blk, out_shape=SDS((t, d), BF16), name=name,
        compiler_params=_cp(("parallel", "parallel")))(y_a, y_b, p_main, p_main)


def _mix_bwd(du, y_a, y_b, p_main, name):
    t, d = y_a.shape
    tm, cw = _row_tile(t), _col_tile(d)
    off_a, off_b = MAIN_FIXED // cw, (MAIN_FIXED + d) // cw
    nb = d // cw

    def body(du_ref, ya_ref, yb_ref, ga_ref, gb_ref, dya_ref, dyb_ref, dla_ref, dlb_ref):
        duv = du_ref[...]
        ga, gb = _sigmoid(ga_ref[...]), _sigmoid(gb_ref[...])
        dya_ref[...] = (duv * ga).astype(BF16)
        dyb_ref[...] = (duv * gb).astype(BF16)
        dla_ref[...] = (duv * ya_ref[...] * ga * (1.0 - ga)).astype(BF16)
        dlb_ref[...] = (duv * yb_ref[...] * gb * (1.0 - gb)).astype(BF16)

    blk = pl.BlockSpec((tm, cw), lambda i, j: (i, j))
    outs = pl.pallas_call(
        body, grid=(t // tm, nb),
        in_specs=[blk, blk, blk, pl.BlockSpec((tm, cw), lambda i, j: (i, off_a + j)),
                  pl.BlockSpec((tm, cw), lambda i, j: (i, off_b + j))],
        out_specs=[blk, blk, blk, blk],
        out_shape=[SDS((t, d), BF16), SDS((t, d), BF16), SDS((t, d), BF16), SDS((t, d), BF16)], name=name,
        compiler_params=_cp(("parallel", "parallel")))(du, y_a, y_b, p_main, p_main)
    return outs


def _gate_res(x, y, gt, name):
    t, d = x.shape
    tm = _row_tile(t)

    def body(x_ref, y_ref, g_ref, o_ref):
        o_ref[...] = x_ref[...] + g_ref[...] * y_ref[...]

    blk = pl.BlockSpec((tm, d), lambda i: (i, 0))
    return pl.pallas_call(
        body, grid=(t // tm,), in_specs=[blk, blk, pl.BlockSpec((1, d), lambda i: (0, 0))], out_specs=blk,
        out_shape=SDS((t, d), F32), name=name, compiler_params=_cp(("parallel",)))(x, y, gt)


def _gate_res_bwd(dx, y, gt, name):
    t, d = dx.shape
    tm = _row_tile(t)

    def body(dx_ref, y_ref, g_ref, dg_ref, dy_ref):
        @pl.when(pl.program_id(0) == 0)
        def _():
            dg_ref[...] = jnp.zeros_like(dg_ref)

        dxv = dx_ref[...]
        dg_ref[...] += jnp.sum(dxv * y_ref[...], axis=0, keepdims=True)
        dy_ref[...] = (dxv * g_ref[...]).astype(BF16)

    blk = pl.BlockSpec((tm, d), lambda i: (i, 0))
    row = pl.BlockSpec((1, d), lambda i: (0, 0))
    return pl.pallas_call(
        body, grid=(t // tm,), in_specs=[blk, blk, row], out_specs=[row, blk],
        out_shape=[SDS((1, d), F32), SDS((t, d), BF16)], name=name, compiler_params=_cp(("arbitrary",)))(dx, y, gt)


def _swiglu_fwd(gu, name):
    t, f2 = gu.shape
    f = f2 // 2
    tm, cw = _row_tile(t), _col_tile(f)
    nb = f // cw

    def body(g_ref, u_ref, o_ref):
        g = g_ref[...]
        o_ref[...] = (g * _sigmoid(g) * u_ref[...]).astype(BF16)

    return pl.pallas_call(
        body, grid=(t // tm, nb),
        in_specs=[pl.BlockSpec((tm, cw), lambda i, j: (i, j)), pl.BlockSpec((tm, cw), lambda i, j: (i, nb + j))],
        out_specs=pl.BlockSpec((tm, cw), lambda i, j: (i, j)), out_shape=SDS((t, f), BF16), name=name,
        compiler_params=_cp(("parallel", "parallel")))(gu, gu)


def _swiglu_bwd(gu, da, name):
    t, f2 = gu.shape
    f = f2 // 2
    tm, cw = _row_tile(t), _col_tile(f)
    nb = f // cw

    def body(g_ref, u_ref, da_ref, dg_ref, dup_ref):
        g = g_ref[...]
        dav = da_ref[...]
        sg = _sigmoid(g)
        dg_ref[...] = (dav * u_ref[...] * sg * (1.0 + g * (1.0 - sg))).astype(BF16)
        dup_ref[...] = (dav * g * sg).astype(BF16)

    blk = pl.BlockSpec((tm, cw), lambda i, j: (i, j))
    dg, dup = pl.pallas_call(
        body, grid=(t // tm, nb),
        in_specs=[blk, pl.BlockSpec((tm, cw), lambda i, j: (i, nb + j)), blk], out_specs=[blk, blk],
        out_shape=[SDS((t, f), BF16), SDS((t, f), BF16)], name=name,
        compiler_params=_cp(("parallel", "parallel")))(gu, gu, da)
    return dg, dup


def _loss_head(x, w, target, name):
    t, d = x.shape
    tm = _row_tile(t)

    def body(x_ref, w_ref, t_ref, l_ref, dx_ref, dw_ref):
        @pl.when(pl.program_id(0) == 0)
        def _():
            l_ref[...] = jnp.zeros_like(l_ref)
            dw_ref[...] = jnp.zeros_like(dw_ref)

        xv = x_ref[...]
        wv = w_ref[...]
        r = lax.rsqrt(jnp.mean(xv * xv, axis=-1, keepdims=True) + EPS)
        xh = xv * r
        err = xh * wv - t_ref[...]
        per_tok = jnp.mean(err * err, axis=-1, keepdims=True)
        l_ref[...] += 0.5 * jnp.sum(per_tok, axis=0, keepdims=True)
        dy = err * (1.0 / d)
        dw_ref[...] += jnp.sum(dy * xh, axis=0, keepdims=True)
        dxh = dy * wv
        dx_ref[...] = r * (dxh - xh * jnp.mean(dxh * xh, axis=-1, keepdims=True))

    blk = pl.BlockSpec((tm, d), lambda i: (i, 0))
    row = pl.BlockSpec((1, d), lambda i: (0, 0))
    return pl.pallas_call(
        body, grid=(t // tm,), in_specs=[blk, row, blk],
        out_specs=[pl.BlockSpec((1, LANES), lambda i: (0, 0)), blk, row],
        out_shape=[SDS((1, LANES), F32), SDS((t, d), F32), SDS((1, d), F32)], name=name,
        compiler_params=_cp(("arbitrary",)))(x, w, target)


def _adamw(w, g, m, v, tie, name):
    shape = w.shape
    per_layer = isinstance(g, (list, tuple))
    n_layers = shape[0] if (w.ndim == 3 and shape[1] % 8 == 0) else 1
    cols = shape[-1]
    rows = w.size // cols // n_layers
    w, m, v = (a.reshape(n_layers * rows, cols) for a in (w, m, v))
    if not per_layer:
        g = g.reshape(n_layers * rows, cols)
    lanes_padded = -(-cols // LANES) * LANES
    budget_rows = max(8, (24 * 1024 * 1024) // (lanes_padded * 4 * 18))
    tr = rows
    if rows > budget_rows:
        tr = _pick(rows, tuple(c for c in (1024, 512, 256, 128, 64, 32, 16, 8) if c <= budget_rows))
    nrb = rows // tr
    c1 = 1.0 / (1.0 - ADAM_B1 ** ADAM_STEP)
    c2 = 1.0 / (1.0 - ADAM_B2 ** ADAM_STEP)
    n_g = len(g) if per_layer else 1

    def body(*refs):
        w_ref, m_ref, v_ref = refs[:3]
        g_refs = refs[3:3 + n_g]
        outs = refs[4 + n_g:]
        gv = g_refs[0][...]
        for l in range(1, n_g):
            gv = jnp.where(pl.program_id(0) == l, g_refs[l][...], gv)
        mn = ADAM_B1 * m_ref[...] + (1.0 - ADAM_B1) * gv
        vn = ADAM_B2 * v_ref[...] + (1.0 - ADAM_B2) * (gv * gv)
        outs[0][...] = -ADAM_LR * ((mn * c1) / (jnp.sqrt(vn * c2) + ADAM_EPS) + ADAM_WD * w_ref[...])
        outs[1][...] = mn
        outs[2][...] = vn
        if per_layer:
            outs[3][...] = gv

    blk = pl.BlockSpec((tr, cols), lambda l, i: (l * nrb + i, 0))
    g_specs = [pl.BlockSpec((tr, cols), lambda l, i: (i, 0))] * n_g if per_layer else [blk]
    n_out = 4 if per_layer else 3
    outs = pl.pallas_call(
        body, grid=(n_layers, nrb), in_specs=[blk, blk, blk] + g_specs + [pl.BlockSpec((8, LANES), lambda l, i: (0, 0))],
        out_specs=[blk] * n_out, out_shape=[SDS(w.shape, F32)] * n_out, name=name,
        compiler_params=_cp(("parallel", "parallel")))(w, m, v, *(g if per_layer else [g]), tie)
    g_out = outs[3] if per_layer else g
    return (g_out.reshape(shape),) + tuple(o.reshape(shape) for o in outs[:3])


KPE_LO = 2 * Q_LORA
QKVZ_LO = KPE_LO + D_ROPE
BA_LO = QKVZ_LO + 4 * GDN_W
GATE_LO = BA_LO + 2 * HEADS


def _lay_w_in(w_in):
    d = w_in.shape[0]
    main = jnp.concatenate([w_in[:, :KPE_LO], w_in[:, QKVZ_LO:BA_LO], w_in[:, GATE_LO:]], axis=1)
    small = jnp.concatenate([w_in[:, KPE_LO:QKVZ_LO], w_in[:, BA_LO:GATE_LO],
                             jnp.zeros((d, LANES - D_ROPE - 2 * HEADS), w_in.dtype)], axis=1)
    return main, small


def _unlay_w_in(g_main, g_small):
    return jnp.concatenate([g_main[:, :KPE_LO], g_small[:, :D_ROPE], g_main[:, KPE_LO:KPE_LO + 4 * GDN_W],
                            g_small[:, D_ROPE:D_ROPE + 2 * HEADS], g_main[:, MAIN_FIXED:]], axis=1)


def _lay_w_uq(w_uq):
    r = w_uq.reshape(Q_LORA, HEADS, D_NOPE + D_ROPE)
    r = jnp.pad(r, ((0, 0), (0, 0), (0, D_QK - D_NOPE - D_ROPE)))
    return r.reshape(Q_LORA, HEADS * D_QK)


def _unlay_w_uq(g):
    rows = g.shape[0]
    return g.reshape(rows, HEADS, D_QK)[:, :, :D_NOPE + D_ROPE].reshape(rows, HEADS * (D_NOPE + D_ROPE))


def _lay_w_ukv(w_ukv):
    return w_ukv.reshape(KV_LORA, HEADS, 2, D_V).transpose(0, 2, 1, 3).reshape(KV_LORA, 2 * HEADS * D_V)


def _unlay_w_ukv(g):
    rows = g.shape[0]
    return g.reshape(rows, 2, HEADS, D_V).transpose(0, 2, 1, 3).reshape(rows, 2 * HEADS * D_V)


def _lane_row(vec, lo):
    return jnp.pad(vec.reshape(1, -1), ((0, 0), (lo, LANES - lo - vec.shape[0])))


def _rope_tables(positions):
    half = D_ROPE // 2
    inv_freq = 1.0 / (10000.0 ** (jnp.arange(0, D_ROPE, 2, dtype=F32) / D_ROPE))
    ang = positions.astype(F32)[:, None] * inv_freq
    cos, sin = jnp.cos(ang), jnp.sin(ang)
    t = positions.shape[0]
    zeros = lambda n: jnp.zeros((t, n), F32)
    tc = jnp.concatenate([cos, cos, zeros(LANES - D_ROPE)], axis=1)
    ts1 = jnp.concatenate([-sin, zeros(LANES - half)], axis=1)
    ts2 = jnp.concatenate([zeros(half), sin, zeros(LANES - D_ROPE)], axis=1)
    return tc, ts1, ts2


def _layer_fwd(x, mod, wt, tabs, tag, late_weights, after_gate_up=None):
    t, d = x.shape
    sh_a, sc_a, gt_a, sh_f, sc_f, gt_f = mod
    zero_l = jnp.zeros((1, Q_LORA), F32)
    s = dict(x=x)
    s["h1"] = _norm_fwd(x, 0, d, wt["norm_mix"], sc_a, sh_a, f"{tag}_norm_mix")
    s["p_main"] = _mm(s["h1"], wt["w_main"], name=f"{tag}_in_main")
    s["p_small"] = _mm(s["h1"], wt["w_small"], name=f"{tag}_in_small")
    s["cqn"] = _norm_fwd(s["p_main"], 0, Q_LORA, wt["q_a_norm"], zero_l, zero_l, f"{tag}_q_norm")
    s["ckvn"] = _norm_fwd(s["p_main"], 1, KV_LORA, wt["kv_a_norm"], zero_l, zero_l, f"{tag}_kv_norm")
    q_raw = _mm(s["cqn"], wt["w_uq"], name=f"{tag}_uq")
    s["kv_raw"] = _mm(s["ckvn"], wt["w_ukv"], name=f"{tag}_ukv")
    s["q_r"] = _rope_q(q_raw, *tabs, False, BF16, f"{tag}_rope_q")
    s["k_r"] = _k_assemble(s["kv_raw"], s["p_small"], *tabs, f"{tag}_k_asm")
    s["o"], s["lse"] = _attn_fwd(s["q_r"], s["k_r"], s["kv_raw"], HEADS, f"{tag}_attn")
    s["yc"] = _conv_fwd(s["p_main"], wt["conv_w"], f"{tag}_conv")
    s["gates"], s["gcum"] = _gdn_gates(s["p_small"], wt["a_row"], wt["dt_row"], f"{tag}_gates")
    s["o_raw"], s["gated"], s["states"] = _gdn_fwd(s["yc"], s["p_main"], s["gates"], s["gcum"], wt["gdn_norm"], f"{tag}_gdn")
    late, started = late_weights(s["gated"])
    wt = {**wt, **late}
    s["y_a"] = _mm(s["o"], wt["w_o_mla"], name=f"{tag}_o_mla")
    s["y_b"] = _mm(s["gated"], wt["w_o_gdn"], name=f"{tag}_o_gdn")
    s["u"] = _mix_fwd(s["y_a"], s["y_b"], s["p_main"], f"{tag}_mix")
    s["mixo"] = _mm(s["u"], wt["w_o"], name=f"{tag}_o")
    s["x2"] = _gate_res(x, s["mixo"], gt_a, f"{tag}_res_a")
    s["h2"] = _norm_fwd(s["x2"], 0, d, wt["norm_ffn"] + started, sc_f, sh_f, f"{tag}_norm_ffn")
    s["gu"] = _mm(s["h2"], wt["w_gate_up"], name=f"{tag}_gate_up")
    if after_gate_up is not None:
        gt_f = gt_f + after_gate_up(s["gu"])
    s["a"] = _swiglu_fwd(s["gu"], f"{tag}_swiglu")
    s["f"] = _mm(s["a"], wt["w_down"], name=f"{tag}_down")
    return _gate_res(s["x2"], s["f"], gt_f, f"{tag}_res_f"), s, wt


def _layer_bwd(dx3, s, mod, wt, tabs, tag, after_ffn=None, after_gdn=None):
    x = s["x"]
    t, d = x.shape
    sh_a, sc_a, gt_a, sh_f, sc_f, gt_f = mod
    zero_l = jnp.zeros((1, Q_LORA), F32)
    g = {}
    dgt_f, df = _gate_res_bwd(dx3, s["f"], gt_f, f"{tag}_b_res_f")
    da = _mm(df, wt["w_down"], tb=True, name=f"{tag}_b_down_x")
    g["w_down"] = _mm(s["a"].T, df, out_dtype=BF16, name=f"{tag}_b_down_w")
    dgate, dup = _swiglu_bwd(s["gu"], da, f"{tag}_b_swiglu")
    dgu = jnp.concatenate([dgate, dup], axis=1)
    dh2 = _mm(dgu, wt["w_gate_up"], tb=True, name=f"{tag}_b_gate_up_x")
    g["w_gate_up"] = _mm(s["h2"].T, dgu, out_dtype=BF16, name=f"{tag}_b_gate_up_w")
    if after_ffn is not None:
        gt_a = gt_a + after_ffn(g)
    dx2, g["norm_ffn"], dsc_f, dsh_f = _norm_bwd(s["x2"], 0, d, wt["norm_ffn"], sc_f, dh2, dx3, F32, f"{tag}_b_norm_ffn")
    dgt_a, dmixo = _gate_res_bwd(dx2, s["mixo"], gt_a, f"{tag}_b_res_a")
    du = _mm(dmixo, wt["w_o"], tb=True, name=f"{tag}_b_o_x")
    g["w_o"] = _mm(s["u"].T, dmixo, out_dtype=BF16, name=f"{tag}_b_o_w")
    dy_a, dy_b, dl_a, dl_b = _mix_bwd(du, s["y_a"], s["y_b"], s["p_main"], f"{tag}_b_mix")
    dgated = _mm(dy_b, wt["w_o_gdn"], tb=True, name=f"{tag}_b_o_gdn_x")
    g["w_o_gdn"] = _mm(s["gated"].T, dy_b, out_dtype=BF16, name=f"{tag}_b_o_gdn_w")
    dq_c, dk_c, dv_c, dz, dgates, g["gdn_norm"] = _gdn_bwd(
        s["yc"], s["p_main"], s["gates"], s["gcum"], wt["gdn_norm"], s["o_raw"], s["states"], dgated, f"{tag}_b_gdn")
    du_conv, g["conv_w"] = _conv_bwd(s["p_main"], wt["conv_w"], jnp.concatenate([dq_c, dk_c, dv_c], axis=1), f"{tag}_b_conv")
    do = _mm(dy_a, wt["w_o_mla"], tb=True, name=f"{tag}_b_o_mla_x")
    g["w_o_mla"] = _mm(s["o"].T, dy_a, out_dtype=BF16, name=f"{tag}_b_o_mla_w")
    dq_r, dk_r, dv = _attn_bwd(s["q_r"], s["k_r"], s["kv_raw"], HEADS, s["o"], s["lse"], do, f"{tag}_b_attn")
    q_a_norm = wt["q_a_norm"]
    if after_gdn is not None:
        q_a_norm = q_a_norm + after_gdn(du_conv)
    dq_raw = _rope_q(dq_r, *tabs, True, BF16, f"{tag}_b_rope_q")
    dkv_raw, dkpe = _k_assemble_bwd(dk_r, dv, *tabs, f"{tag}_b_k_asm")
    dcqn = _mm(dq_raw, wt["w_uq"], tb=True, name=f"{tag}_b_uq_x")
    g["w_uq"] = _mm(s["cqn"].T, dq_raw, out_dtype=BF16, name=f"{tag}_b_uq_w")
    dckvn = _mm(dkv_raw, wt["w_ukv"], tb=True, name=f"{tag}_b_ukv_x")
    g["w_ukv"] = _mm(s["ckvn"].T, dkv_raw, out_dtype=BF16, name=f"{tag}_b_ukv_w")
    dc_q, g["q_a_norm"], _, _ = _norm_bwd(s["p_main"], 0, Q_LORA, q_a_norm, zero_l, dcqn, None, BF16, f"{tag}_b_q_norm")
    dc_kv, g["kv_a_norm"], _, _ = _norm_bwd(s["p_main"], 1, KV_LORA, wt["kv_a_norm"], zero_l, dckvn, None, BF16,
                                            f"{tag}_b_kv_norm")
    dp_small, g["a_row"], g["dt_row"] = _gdn_gates_bwd(
        s["p_small"], wt["a_row"], wt["dt_row"], s["gates"], dgates, dkpe, f"{tag}_b_gates")
    dp_main = jnp.concatenate([dc_q, dc_kv, du_conv, dz, dl_a, dl_b], axis=1)
    h1t = s["h1"].T
    dh1 = _mm(dp_small, wt["w_small"], tb=True, name=f"{tag}_b_in_small_x")
    dh1 = _mm(dp_main, wt["w_main"], tb=True, acc_in=dh1, name=f"{tag}_b_in_main_x")
    g["w_main"] = _mm(h1t, dp_main, out_dtype=BF16, name=f"{tag}_b_in_main_w")
    g["w_small"] = _mm(h1t, dp_small, out_dtype=BF16, name=f"{tag}_b_in_small_w")
    dx, g["norm_mix"], dsc_a, dsh_a = _norm_bwd(x, 0, d, wt["norm_mix"], sc_a, dh1, dx2, F32, f"{tag}_b_norm_mix")
    return dx, (dsh_a, dsc_a, dgt_a, dsh_f, dsc_f, dgt_f), g


def _layer_weights(big, full, l):
    w_main, w_small = _lay_w_in(big["w_in"])
    return dict(
        w_main=w_main, w_small=w_small, w_uq=_lay_w_uq(big["w_uq"]), w_ukv=_lay_w_ukv(big["w_ukv"]),
        conv_w=full["conv_w"][l],
        norm_mix=full["norm_mix"][l][None], norm_ffn=full["norm_ffn"][l][None],
        q_a_norm=full["q_a_norm"][l][None], kv_a_norm=full["kv_a_norm"][l][None], gdn_norm=full["gdn_norm"][l][None],
        a_row=_lane_row(full["A_log"][l], A_LO), dt_row=_lane_row(full["dt_bias"][l], A_LO))


def _small_grads_ref_layout(g):
    return dict(
        conv_w=g["conv_w"], norm_mix=g["norm_mix"][0], norm_ffn=g["norm_ffn"][0], q_a_norm=g["q_a_norm"][0],
        kv_a_norm=g["kv_a_norm"][0], gdn_norm=g["gdn_norm"][0], A_log=g["a_row"][0, A_LO:A_HI],
        dt_bias=g["dt_row"][0, A_LO:A_HI])


def _local_step(x, mods, target, final_norm, full, positions):
    tabs = _rope_tables(positions)
    depth = len(mods)
    wts, saved = [None] * depth, []
    h = x
    for l in range(depth):
        early = _layer_weights({n: full[n][l] for n in FIRST_NEEDED}, full, l)
        h, s, wts[l] = _layer_fwd(h, mods[l], early, tabs, f"l{l}", lambda _, l=l: ({n: full[n][l] for n in LATER_NEEDED}, 0.0))
        saved.append(s)
    loss, dh, dfn = _loss_head(h, final_norm[None], target, "loss_head")
    dmods, grads = [None] * depth, [None] * depth
    for l in reversed(range(depth)):
        dh, dmods[l], grads[l] = _layer_bwd(dh, saved[l], mods[l], wts[l], tabs, f"l{l}")
    return loss, dh, dmods, grads, dfn[0]


HBM_SPEC = pl.BlockSpec(memory_space=pl.ANY)
VMEM_SPEC = pl.BlockSpec(memory_space=pltpu.VMEM)
N_CHIPS = 4
N_DEV = 8


def _me():
    return lax.axis_index("x"), lax.axis_index("y"), lax.axis_index("c")


def _flip(pos, f):
    mx, my, mc = pos
    fx, fy, fc = (f >> 2) & 1, (f >> 1) & 1, f & 1
    return ((mx + fx) % 2, (my + fy) % 2, (mc + fc) % 2)


def _all_gather_small(x, name):
    r, n = x.shape

    def body(x_ref, out_ref, send_sems, recv_sems, local_sem):
        me = _me()
        row = lambda p: 4 * p[0] + 2 * p[1] + p[2]
        mine = pltpu.make_async_copy(x_ref, out_ref.at[row(me)], local_sem)
        mine.start()

        def copy(f, origin):
            return pltpu.make_async_remote_copy(
                src_ref=x_ref, dst_ref=out_ref.at[row(origin)], send_sem=send_sems.at[f - 1], recv_sem=recv_sems.at[f - 1],
                device_id=_flip(me, f), device_id_type=MESH)

        sends = [copy(f, me) for f in range(1, N_DEV)]
        for cp in sends:
            cp.start()
        for f in range(1, N_DEV):
            copy(f, _flip(me, f)).wait_recv()
        for cp in sends:
            cp.wait_send()
        mine.wait()

    return pl.pallas_call(
        body, out_shape=SDS((N_DEV, r, n), x.dtype), in_specs=[VMEM_SPEC], out_specs=VMEM_SPEC,
        scratch_shapes=[pltpu.SemaphoreType.DMA((N_DEV - 1,)), pltpu.SemaphoreType.DMA((N_DEV - 1,)), pltpu.SemaphoreType.DMA],
        name=name, compiler_params=pltpu.CompilerParams(vmem_limit_bytes=VMEM_LIMIT))(x)


CHIP_FLIPS = (2, 4, 6)
SIBLING = 1


def _chip_of(pos):
    return 2 * pos[0] + pos[1]


def _sum_chips(p, got, chip, half, name):
    _, kh, ns = p.shape
    tr = _pick(kh, (256, 128, 64, 32, 16))
    nrb = kh // tr

    def body(c_ref, h_ref, a_ref, b_ref, o_ref):
        acc = a_ref[0].astype(F32)
        for j in range(3):
            acc = acc + b_ref[j].astype(F32)
        o_ref[...] = acc

    grid_spec = pltpu.PrefetchScalarGridSpec(
        num_scalar_prefetch=2, grid=(nrb,),
        in_specs=[pl.BlockSpec((1, tr, ns), lambda i, c, h: (c[0], i, 0)),
                  pl.BlockSpec((3, tr, ns), lambda i, c, h: (0, i, 0))],
        out_specs=pl.BlockSpec((tr, ns), lambda i, c, h: (h[0] * nrb + i, 0)))
    return pl.pallas_call(body, grid_spec=grid_spec, out_shape=SDS((2 * kh, ns), F32), name=name,
                          compiler_params=_cp(("parallel",)))(chip, half, p, got)


SEM_SPEC = pl.BlockSpec(memory_space=pltpu.SEMAPHORE)
HBM_ONLY = pl.BlockSpec(memory_space=pltpu.HBM)
DATAFLOW = pltpu.SideEffectType.DATAFLOW_SIDE_EFFECTING


def _in_hbm(a):
    return pltpu.with_memory_space_constraint(a, pltpu.HBM)


def _copies_start(name, srcs, lands, plan, n_copies):
    ns, nl = len(srcs), len(lands)

    def body(*refs):
        src_refs, land_refs = refs[:ns], refs[ns:ns + nl]
        send_sems, recv_sems = refs[ns + nl], refs[ns + nl + 1]
        token = refs[-1]
        for i, (src, dst, peer) in enumerate(plan(_me(), src_refs, land_refs)):
            pltpu.make_async_remote_copy(src_ref=src, dst_ref=dst, send_sem=send_sems.at[i], recv_sem=recv_sems.at[i],
                                         device_id=peer, device_id_type=MESH).start()
        token[...] = jnp.zeros_like(token)

    outs = pl.pallas_call(
        body, name=name,
        out_shape=(pltpu.SemaphoreType.DMA((n_copies,)), pltpu.SemaphoreType.DMA((n_copies,)),
                   *[pltpu.HBM(l.shape, l.dtype) for l in lands], SDS((8, LANES), F32)),
        in_specs=[HBM_ONLY] * (ns + nl), out_specs=(SEM_SPEC, SEM_SPEC, *[HBM_ONLY] * nl, VMEM_SPEC),
        input_output_aliases={ns + i: 2 + i for i in range(nl)},
        compiler_params=pltpu.CompilerParams(has_side_effects=DATAFLOW),
    )(*[_in_hbm(s) for s in srcs], *[_in_hbm(l) for l in lands])
    return outs[0], outs[1], list(outs[2:2 + nl]), outs[-1]


def _copies_wait(name, srcs, lands, send_sems, recv_sems, plan, after):
    ns, nl = len(srcs), len(lands)

    def body(*refs):
        src_refs, land_refs = refs[:ns], refs[ns:ns + nl]
        send_ref, recv_ref = refs[ns + nl], refs[ns + nl + 1]
        for i, (src, dst, peer) in enumerate(plan(_me(), src_refs, land_refs)):
            cp = pltpu.make_async_remote_copy(src_ref=src, dst_ref=dst, send_sem=send_ref.at[i], recv_sem=recv_ref.at[i],
                                              device_id=peer, device_id_type=MESH)
            cp.wait_send()
            cp.wait_recv()

    outs = pl.pallas_call(
        body, name=name, out_shape=[pltpu.HBM(l.shape, l.dtype) for l in lands],
        in_specs=[HBM_ONLY] * (ns + nl) + [SEM_SPEC, SEM_SPEC, HBM_SPEC], out_specs=[HBM_ONLY] * nl,
        input_output_aliases={ns + i: i for i in range(nl)},
        compiler_params=pltpu.CompilerParams(has_side_effects=DATAFLOW),
    )(*[_in_hbm(s) for s in srcs], *lands, send_sems, recv_sems, after)
    return list(outs)


def _half(ref, rows, axis):
    idx = [slice(None)] * axis + [rows]
    return ref.at[tuple(idx)]


def _gather_plans(layer, halves):
    def ici(me, srcs, lands):
        out = []
        for a, kh in enumerate(halves):
            rows = pl.ds(pl.multiple_of(me[2] * kh, 16), kh)
            for k in range(3):
                out.append((srcs[a].at[layer, rows], lands[a].at[_chip_of(me), rows], _flip(me, CHIP_FLIPS[k])))
        return out

    def d2d(me, srcs, lands):
        out = []
        for a, kh in enumerate(halves):
            rows = pl.ds(pl.multiple_of(me[2] * kh, 16), kh)
            for k in range(3):
                slab = lands[a].at[_chip_of(_flip(me, CHIP_FLIPS[k])), rows]
                out.append((slab, slab, _flip(me, SIBLING)))
        return out

    return ici, d2d


def _to_sibling_plan(halves, axes):
    def plan(me, srcs, lands):
        out = []
        for a, (kh, axis) in enumerate(zip(halves, axes)):
            rows = pl.ds(pl.multiple_of((1 - me[2]) * kh, 16), kh)
            out.append((_half(srcs[a], rows, axis), lands[a], _flip(me, SIBLING)))
        return out

    return plan


def _to_chips_plan(n_arr):
    def plan(me, srcs, lands):
        out = []
        for a in range(n_arr):
            for k in range(3):
                peer = _flip(me, CHIP_FLIPS[k])
                out.append((srcs[a].at[_chip_of(peer)], lands[a].at[k], peer))
        return out

    return plan


def _swap_plan(halves):
    def plan(me, srcs, lands):
        out = []
        for a, kh in enumerate(halves):
            rows = pl.ds(pl.multiple_of(me[2] * kh, 16), kh)
            out.append((lands[a].at[rows], lands[a].at[rows], _flip(me, SIBLING)))
        return out

    return plan


def _add_half(g, got, half, col_shards, name):
    s, kh, n = got.shape
    tr = _pick(kh, (512, 256, 128, 64, 32, 16))
    nrb = kh // tr
    width = n // N_CHIPS if col_shards else n
    cw = _pick(width, (1024, 512, 256, 128))
    per = width // cw

    def body(h_ref, a_ref, b_ref, o_ref):
        o_ref[...] = (a_ref[...].astype(F32) + b_ref[...].astype(F32)).astype(o_ref.dtype)

    in_specs = [pl.BlockSpec((None, tr, cw), lambda j, i, c, h: (j, h[0] * nrb + i, c)),
                pl.BlockSpec((None, tr, cw), lambda j, i, c, h: (j, i, c))]
    if col_shards:
        assert s == 1
        out_spec = pl.BlockSpec((None, tr, cw), lambda j, i, c, h: (c // per, i, c % per))
        out_shape = SDS((N_CHIPS, kh, width), g.dtype)
    else:
        out_spec, out_shape = in_specs[1], SDS((s, kh, n), g.dtype)
    grid_spec = pltpu.PrefetchScalarGridSpec(num_scalar_prefetch=1, grid=(s, nrb, n // cw), in_specs=in_specs,
                                             out_specs=out_spec)
    return pl.pallas_call(body, grid_spec=grid_spec, out_shape=out_shape, name=name,
                          compiler_params=_cp(("parallel", "parallel", "parallel")))(half, g, got)


def _sum_devices(g, name):
    _, _, n = g.shape

    def body(g_ref, o_ref):
        acc = g_ref[0]
        for k in range(1, N_DEV):
            acc = acc + g_ref[k]
        o_ref[...] = acc

    return pl.pallas_call(body, out_shape=SDS((1, n), F32), in_specs=[VMEM_SPEC], out_specs=VMEM_SPEC, name=name)(g)


def _silu_rows(c, name):
    def body(c_ref, o_ref):
        v = c_ref[...]
        o_ref[...] = v * _sigmoid(v)

    return pl.pallas_call(body, out_shape=SDS(c.shape, F32), in_specs=[VMEM_SPEC], out_specs=VMEM_SPEC, name=name)(c)


BIG = (("w_in", 2), ("w_uq", 2), ("w_ukv", 2), ("w_o_mla", 2), ("w_o_gdn", 2), ("w_o", 1), ("w_gate_up", 2), ("w_down", 1))
KERNEL_BIG = ("w_main", "w_small", "w_uq", "w_ukv", "w_o_mla", "w_o_gdn", "w_o", "w_gate_up", "w_down")
COL_SHARDED_AS_IS = ("w_o_mla", "w_o_gdn", "w_gate_up")
ROW_SHARDED = ("w_o", "w_down")
FIRST_NEEDED = ("w_in", "w_uq", "w_ukv")
LATER_NEEDED = ("w_o_mla", "w_o_gdn", "w_o", "w_gate_up", "w_down")
FFN_GRADS = ("w_gate_up", "w_down")
MIXER_GRADS = ("w_in", "w_uq", "w_ukv", "w_o_mla", "w_o_gdn", "w_o")
MIXER_GRADS_KERNEL = ("w_main", "w_small", "w_uq", "w_ukv", "w_o_mla", "w_o_gdn", "w_o")
SMALL = ("norm_mix", "norm_ffn", "q_a_norm", "kv_a_norm", "A_log", "dt_bias", "gdn_norm")
WEIGHTS = ("w_ada", "b_ada", "norm_mix", "norm_ffn", "w_in", "q_a_norm", "kv_a_norm", "w_uq", "w_ukv", "w_o_mla", "conv_w",
           "A_log", "dt_bias", "gdn_norm", "w_o_gdn", "w_o", "w_gate_up", "w_down", "final_norm")
ADA_PAD = 16
K_PAD = 128


def _pad_to(a, n, axis):
    pad = [(0, 0)] * a.ndim
    pad[axis] = (0, n - a.shape[axis])
    return jnp.pad(a, pad)


def kernel(x, c, positions, w_ada, b_ada, norm_mix, norm_ffn, w_in, q_a_norm, kv_a_norm, w_uq, w_ukv, w_o_mla, conv_w, A_log, dt_bias, gdn_norm, w_o_gdn, w_o, w_gate_up, w_down, final_norm, loss_target, m_w_ada, m_b_ada, m_norm_mix, m_norm_ffn, m_w_in, m_q_a_norm, m_kv_a_norm, m_w_uq, m_w_ukv, m_w_o_mla, m_conv_w, m_A_log, m_dt_bias, m_gdn_norm, m_w_o_gdn, m_w_o, m_w_gate_up, m_w_down, m_final_norm, v_w_ada, v_b_ada, v_norm_mix, v_norm_ffn, v_w_in, v_q_a_norm, v_kv_a_norm, v_w_uq, v_w_ukv, v_w_o_mla, v_conv_w, v_A_log, v_dt_bias, v_gdn_norm, v_w_o_gdn, v_w_o, v_w_gate_up, v_w_down, v_final_norm):
    env = dict(locals())
    w = {n: env[n] for n in WEIGHTS}
    depth, d = norm_mix.shape
    t = x.shape[1]
    me = _me()
    chip = _chip_of(me)
    dev = 4 * me[0] + 2 * me[1] + me[2]
    ada_cols = w_ada.shape[2]

    half_idx = me[2].astype(jnp.int32).reshape(1)
    chip_idx = chip.astype(jnp.int32).reshape(1)
    w16 = {n: w[n].astype(BF16) for n, _ in BIG}
    shard_axis = dict(BIG)
    gather = {}

    def start_group(key, layer, names, dep):
        srcs = [w16[n] for n in names]
        plans = _gather_plans(layer, [a.shape[1] // 2 for a in srcs])
        landing = [lax.empty((N_CHIPS,) + a.shape[1:], BF16) for a in srcs]
        send_s, recv_s, landing, tok = _copies_start(f"gather_{key}_ici_start", srcs + [dep], landing, plans[0], 3 * len(names))
        gather[key] = dict(layer=layer, names=names, srcs=srcs, plans=plans, ici=(send_s, recv_s, landing), tok=tok)
        return tok[0, 0]

    def pass_to_sibling(key, after):
        st = gather[key]
        send_s, recv_s, landing = st["ici"]
        landing = _copies_wait(f"gather_{key}_ici_wait", st["srcs"] + [st["tok"]], landing, send_s, recv_s, st["plans"][0],
                               st["tok"] if after is None else after)
        st["d2d"] = _copies_start(f"gather_{key}_d2d_start", [], landing, st["plans"][1], 3 * len(st["names"]))
        return st["d2d"][3]

    def gathered(key):
        st = gather[key]
        send_s, recv_s, landing, tok = st["d2d"]
        landing = _copies_wait(f"gather_{key}_d2d_wait", [], landing, send_s, recv_s, st["plans"][1], tok)
        return {n: jnp.concatenate([jnp.where(chip == j, own[st["layer"]], got[j]) for j in range(N_CHIPS)],
                                   axis=shard_axis[n] - 1)
                for n, own, got in zip(st["names"], st["srcs"], landing)}

    full = {}
    conv_all = _all_gather_small(conv_w.reshape(1, -1), "gather_conv").reshape((N_DEV,) + conv_w.shape)
    full["conv_w"] = jnp.concatenate([conv_all[2 * j] for j in range(N_CHIPS)], axis=2)
    for n in SMALL:
        full[n] = w[n]

    c_all = _all_gather_small(c, "gather_c").reshape(N_DEV, d)
    c_act = _silu_rows(_pad_to(c_all, ADA_PAD, 0), "silu_c")
    b_cols = lax.dynamic_slice_in_dim(b_ada, chip * ada_cols, ada_cols, axis=1)
    mod_cols = jnp.stack([
        _mm(c_act, w_ada[l], acc_in=jnp.broadcast_to(b_cols[l][None], (ADA_PAD, ada_cols)), name=f"ada_l{l}")[:N_DEV]
        for l in range(depth)])
    mod_all = _all_gather_small(mod_cols.reshape(depth * N_DEV, ada_cols), "gather_mod")
    mod_all = mod_all.reshape(N_DEV, depth, N_DEV, ada_cols)
    mods = []
    for l in range(depth):
        mine = jnp.concatenate([lax.dynamic_index_in_dim(mod_all[2 * j, l], dev, axis=0, keepdims=True)
                                for j in range(N_CHIPS)], axis=1)
        mods.append(tuple(mine[:, i * d:(i + 1) * d] for i in range(6)))

    tabs = _rope_tables(positions[0])
    start_group("l0a", 0, FIRST_NEEDED, mods[depth - 1][5][:, :LANES] + full["conv_w"].reshape(1, -1)[:, :LANES])
    tie = start_group("l0b", 0, LATER_NEEDED, pass_to_sibling("l0a", None))

    def late_weights(key, next_key, next_layer, behind):
        tok = pass_to_sibling(key, behind)
        started = 0.0 if next_key is None else start_group(next_key, next_layer, FIRST_NEEDED, tok)
        return gathered(key), started

    def next_later_group(behind):
        return start_group("l1b", 1, LATER_NEEDED, pass_to_sibling("l1a", behind))

    wts, saved = [None] * depth, [None] * depth
    tied = (mods[0][0] + tie,) + mods[0][1:]
    h, saved[0], wts[0] = _layer_fwd(x[0], tied, _layer_weights(gathered("l0a"), full, 0), tabs, "l0",
                                     functools.partial(late_weights, "l0b", "l1a", 1), next_later_group)
    h, saved[1], wts[1] = _layer_fwd(h, mods[1], _layer_weights(gathered("l1a"), full, 1), tabs, "l1",
                                     functools.partial(late_weights, "l1b", None, None))
    loss_part, dh, dfn = _loss_head(h, final_norm[None], loss_target[0], "loss_head")
    dfn = dfn[0]

    def col_shards(g):
        return g.reshape(g.shape[0], N_CHIPS, g.shape[1] // N_CHIPS).transpose(1, 0, 2)

    def reduce_scatter_stages(tag, g, knames, names):
        srcs = [g[n].reshape(N_CHIPS, -1, g[n].shape[1]) if n in ROW_SHARDED else g[n] for n in knames]
        axes = [1 if n in ROW_SHARDED else 0 for n in knames]
        halves = [a.shape[ax] // 2 for a, ax in zip(srcs, axes)]
        got_shapes = [a.shape[:ax] + (kh,) + a.shape[ax + 1:] for a, ax, kh in zip(srcs, axes, halves)]
        plan_a, plan_c = _to_sibling_plan(halves, axes), _to_chips_plan(len(names))
        st, out = {}, {}
        st["a"] = _copies_start(f"{tag}_sibling_start", srcs, [lax.empty(sh, BF16) for sh in got_shapes], plan_a, len(srcs))

        def after_or(tok, after):
            return tok if after is None else after

        def stage0(after):
            send_s, recv_s, landing, tok = st["a"]
            got = _copies_wait(f"{tag}_sibling_wait", srcs, landing, send_s, recv_s, plan_a, after_or(tok, after))
            sums = {}
            for n, a, b in zip(knames, srcs, got):
                a3, b3 = (v if v.ndim == 3 else v[None] for v in (a, b))
                r = _add_half(a3, b3, half_idx, n in COL_SHARDED_AS_IS, f"{tag}_add_{n}")
                sums[n] = r if (n in COL_SHARDED_AS_IS or n in ROW_SHARDED) else r[0]
            if "w_main" in sums:
                sums["w_in"] = col_shards(_unlay_w_in(sums["w_main"], sums["w_small"]))
                sums["w_uq"] = col_shards(_unlay_w_uq(sums["w_uq"]))
                sums["w_ukv"] = col_shards(_unlay_w_ukv(sums["w_ukv"]))
            st["p"] = [sums[n] for n in names]
            st["c"] = _copies_start(f"{tag}_chips_start", st["p"], [lax.empty((3,) + p.shape[1:], BF16) for p in st["p"]],
                                    plan_c, 3 * len(names))
            return st["c"][3][0, 0]

        def stage1(after):
            send_s, recv_s, landing, tok = st["c"]
            got = _copies_wait(f"{tag}_chips_wait", st["p"], landing, send_s, recv_s, plan_c, after_or(tok, after))
            sums = [_sum_chips(p, q, chip_idx, half_idx, f"{tag}_sum_{n}") for n, p, q in zip(names, st["p"], got)]
            plan_e = _swap_plan([r.shape[0] // 2 for r in sums])
            st["e"] = _copies_start(f"{tag}_swap_start", [], sums, plan_e, len(names)) + (plan_e,)
            return st["e"][3][0, 0]

        def stage2(after):
            send_s, recv_s, landing, tok, plan_e = st["e"]
            got = _copies_wait(f"{tag}_swap_wait", [], landing, send_s, recv_s, plan_e, after_or(tok, after))
            out.update(zip(names, got))

        return (stage0, stage1, stage2), out, st["a"][3][0, 0]

    dmods, grads, groups = [None] * depth, [None] * depth, {}

    def ffn_group_l1(g):
        groups["l1_ffn"] = reduce_scatter_stages("rs_l1_ffn", g, FFN_GRADS, FFN_GRADS)
        return groups["l1_ffn"][2]

    dh, dmods[1], grads[1] = _layer_bwd(dh, saved[1], mods[1], wts[1], tabs, "l1", after_ffn=ffn_group_l1)
    groups["l1_mix"] = reduce_scatter_stages("rs_l1_mix", grads[1], MIXER_GRADS_KERNEL, MIXER_GRADS)
    tied = mods[0][:5] + (mods[0][5] + groups["l1_mix"][2],)

    def ffn_group_l0(g):
        behind = g["w_gate_up"]
        tok = groups["l1_ffn"][0][0](behind) + groups["l1_mix"][0][0](behind)
        groups["l0_ffn"] = reduce_scatter_stages("rs_l0_ffn", g, FFN_GRADS, FFN_GRADS)
        return tok + groups["l0_ffn"][2]

    def after_gdn_l0(behind):
        return groups["l0_ffn"][0][0](behind)

    dx, dmods[0], grads[0] = _layer_bwd(dh, saved[0], tied, wts[0], tabs, "l0", after_ffn=ffn_group_l0, after_gdn=after_gdn_l0)
    groups["l0_mix"] = reduce_scatter_stages("rs_l0_mix", grads[0], MIXER_GRADS_KERNEL, MIXER_GRADS)
    for key in ("l1_ffn", "l1_mix", "l0_ffn"):
        groups[key][0][1](dx)
    g_out, deltas, new_m, new_v = {}, {}, {}, {}

    def reduced(names):
        for n in names:
            g_out[n] = [groups[f"l{l}_ffn" if n in FFN_GRADS else f"l{l}_mix"][1][n] for l in range(depth)]

    def update(names, tie):
        for n in names:
            g_out[n], deltas[n], new_m[n], new_v[n] = _adamw(w[n], g_out[n], env["m_" + n], env["v_" + n], tie,
                                                             f"adamw_{n}")

    small = [_small_grads_ref_layout(grads[l]) for l in range(depth)]
    small_parts = [jnp.concatenate(dmods[l], axis=1).reshape(-1) for l in range(depth)]
    small_parts += [jnp.stack([small[l][n] for l in range(depth)]).reshape(-1) for n in SMALL]
    small_parts += [dfn, loss_part[0, :1]]
    small_sizes = [p.shape[0] for p in small_parts]
    packed = jnp.concatenate(small_parts)
    n_small = -(-packed.shape[0] // LANES) * LANES
    small_all = _all_gather_small(_pad_to(packed, n_small, 0).reshape(1, n_small), "gather_small_grads")
    small_sum = _sum_devices(small_all, "sum_small_grads")[0]
    offs = [0]
    for sz in small_sizes:
        offs.append(offs[-1] + sz)
    g_out["b_ada"] = jnp.stack([small_sum[offs[l]:offs[l + 1]] for l in range(depth)])
    for i, n in enumerate(SMALL):
        g_out[n] = small_sum[offs[depth + i]:offs[depth + i + 1]].reshape(w[n].shape)
    g_out["final_norm"] = small_sum[offs[depth + len(SMALL)]:offs[depth + len(SMALL) + 1]]
    loss = small_sum[offs[depth + len(SMALL) + 1]]

    c_act_t = _pad_to(c_act[:N_DEV].T, K_PAD, 1)
    g_ada = []
    for l in range(depth):
        dmod_l = small_all[:, 0, offs[l]:offs[l + 1]]
        dmod_cols = lax.dynamic_slice_in_dim(dmod_l, chip * ada_cols, ada_cols, axis=1)
        g_ada.append(_mm(c_act_t, _pad_to(dmod_cols, K_PAD, 0), name=f"ada_grad_l{l}"))
    g_out["w_ada"] = jnp.stack(g_ada)

    conv_g = jnp.stack([small[l]["conv_w"] for l in range(depth)])
    conv_all_g = _all_gather_small(conv_g.reshape(1, -1), "gather_conv_grads")
    conv_sum = _sum_devices(conv_all_g, "sum_conv_grads").reshape(conv_g.shape)
    n_cc = conv_w.shape[2]
    g_out["conv_w"] = lax.dynamic_slice_in_dim(conv_sum, chip * n_cc, n_cc, axis=2)

    mix0 = groups["l0_mix"][0]
    started = mix0[0](conv_sum.reshape(-1)[:LANES] + small_sum[:LANES])
    for key in ("l1_ffn", "l1_mix", "l0_ffn"):
        groups[key][0][2](None)
    reduced(FFN_GRADS)
    first_updates = ("w_ada", "b_ada", "final_norm", "conv_w") + SMALL + FFN_GRADS
    update(first_updates, jnp.zeros((8, LANES), F32) + started)
    corner = lambda a: a.reshape((1,) * (3 - a.ndim) + a.shape)[0, :1, :LANES]
    mix0[1](sum(corner(deltas[n]) for n in first_updates if w[n].shape[-1] >= LANES))
    mix0[2](None)
    reduced(MIXER_GRADS)
    update(MIXER_GRADS, jnp.zeros((8, LANES), F32))
    return (loss, dx[None], *[g_out[n] for n in WEIGHTS], *[deltas[n] for n in WEIGHTS],
            *[new_m[n] for n in WEIGHTS], *[new_v[n] for n in WEIGHTS])
```

```python
import functools
import math

import jax
import jax.numpy as jnp
from jax import lax
from jax.experimental import pallas as pl
from jax.experimental.pallas import tpu as pltpu

F32 = jnp.float32
BF16 = jnp.bfloat16
SDS = jax.ShapeDtypeStruct
MESH = pl.DeviceIdType.MESH
AXES = ("x", "y", "c")

EPS = 1e-6
HEADS = 8
D_NOPE = 128
D_ROPE = 64
D_QK = 256
D_V = 128
Q_LORA = 512
KV_LORA = 512
CHUNK = 64
CONV_WIDTH = 4
GDN_W = HEADS * D_V
MAIN_FIXED = 2 * Q_LORA + 4 * GDN_W
LANES = 128
VMEM_LIMIT = 56 * 1024 * 1024
ADAM_LR, ADAM_B1, ADAM_B2, ADAM_EPS, ADAM_WD, ADAM_STEP = 0.001, 0.9, 0.999, 1e-8, 0.01, 10


def _pick(n, cands):
    for cand in cands:
        if n % cand == 0:
            return cand
    return n


def _cp(sem):
    return pltpu.CompilerParams(dimension_semantics=sem, vmem_limit_bytes=VMEM_LIMIT)


def _row_tile(t):
    return _pick(t, (256, 128, 64, 32, 16, 8))


def _dot(a, b):
    return jnp.dot(a.astype(BF16), b.astype(BF16), preferred_element_type=F32)


def _dot_nt(a, b):
    return lax.dot_general(a.astype(BF16), b.astype(BF16), (((1,), (1,)), ((), ())), preferred_element_type=F32)


def _dot_tn(a, b):
    return lax.dot_general(a.astype(BF16), b.astype(BF16), (((0,), (0,)), ((), ())), preferred_element_type=F32)


def _bdot(a, b, exact=False):
    dims = (((2,), (1,)), ((0,), (0,)))
    if exact:
        ah, bh = a.astype(BF16), b.astype(BF16)
        al, bl = (a - ah.astype(F32)).astype(BF16), (b - bh.astype(F32)).astype(BF16)
        return (lax.dot_general(ah, bh, dims, preferred_element_type=F32)
                + lax.dot_general(ah, bl, dims, preferred_element_type=F32)
                + lax.dot_general(al, bh, dims, preferred_element_type=F32))
    return lax.dot_general(a.astype(BF16), b.astype(BF16), dims, preferred_element_type=F32)


def _bdot_nt(a, b):
    return lax.dot_general(a.astype(BF16), b.astype(BF16), (((2,), (2,)), ((0,), (0,))), preferred_element_type=F32)


def _sigmoid(x):
    return 1.0 / (1.0 + jnp.exp(-x))


def _mm(a, b, *, tb=False, out_dtype=F32, acc_in=None, name):
    m, k = a.shape
    n = b.shape[0] if tb else b.shape[1]
    assert (b.shape[1] if tb else b.shape[0]) == k
    tm = _pick(m, (1024, 512, 256, 128))
    tn = _pick(n, (1024, 512, 256, 128))
    tk = k if k <= 2048 else _pick(k, (1024, 512, 256, 128))
    nk = k // tk
    has_acc = acc_in is not None

    def body_one_step(*refs):
        a_ref, b_ref = refs[:2]
        o_ref = refs[-1]
        acc = _dot_nt(a_ref[...], b_ref[...]) if tb else _dot(a_ref[...], b_ref[...])
        if has_acc:
            acc = acc + refs[2][...].astype(F32)
        o_ref[...] = acc.astype(out_dtype)

    if nk == 1:
        in_specs = [pl.BlockSpec((tm, k), lambda i, j: (i, 0)),
                    pl.BlockSpec((tn, k), lambda i, j: (j, 0)) if tb else pl.BlockSpec((k, tn), lambda i, j: (0, j))]
        args = [a, b]
        if has_acc:
            in_specs.append(pl.BlockSpec((tm, tn), lambda i, j: (i, j)))
            args.append(acc_in)
        return pl.pallas_call(
            body_one_step, grid=(m // tm, n // tn), in_specs=in_specs, out_specs=pl.BlockSpec((tm, tn), lambda i, j: (i, j)),
            out_shape=SDS((m, n), out_dtype), name=name, compiler_params=_cp(("parallel", "parallel")))(*args)

    def body(*refs):
        if has_acc:
            a_ref, b_ref, c_ref, o_ref, acc = refs
        else:
            a_ref, b_ref, o_ref, acc = refs
        kk = pl.program_id(2)

        @pl.when(kk == 0)
        def _():
            if has_acc:
                acc[...] = c_ref[...].astype(F32)
            else:
                acc[...] = jnp.zeros_like(acc)

        if tb:
            acc[...] += _dot_nt(a_ref[...], b_ref[...])
        else:
            acc[...] += _dot(a_ref[...], b_ref[...])

        @pl.when(kk == nk - 1)
        def _():
            o_ref[...] = acc[...].astype(out_dtype)

    in_specs = [pl.BlockSpec((tm, tk), lambda i, j, kk: (i, kk)),
                pl.BlockSpec((tn, tk), lambda i, j, kk: (j, kk)) if tb
                else pl.BlockSpec((tk, tn), lambda i, j, kk: (kk, j))]
    args = [a, b]
    if has_acc:
        in_specs.append(pl.BlockSpec((tm, tn), lambda i, j, kk: (i, j)))
        args.append(acc_in)
    return pl.pallas_call(
        body, grid=(m // tm, n // tn, nk), in_specs=in_specs,
        out_specs=pl.BlockSpec((tm, tn), lambda i, j, kk: (i, j)),
        out_shape=SDS((m, n), out_dtype), scratch_shapes=[pltpu.VMEM((tm, tn), F32)],
        name=name, compiler_params=_cp(("parallel", "parallel", "arbitrary")))(*args)


def _norm_fwd(x, col, width, w, sc, sh, name):
    t = x.shape[0]
    tm = _row_tile(t)

    def body(x_ref, w_ref, sc_ref, sh_ref, o_ref):
        xv = x_ref[...]
        r = lax.rsqrt(jnp.mean(xv * xv, axis=-1, keepdims=True) + EPS)
        n = xv * r * w_ref[...]
        o_ref[...] = (n * (1.0 + sc_ref[...]) + sh_ref[...]).astype(o_ref.dtype)

    row = pl.BlockSpec((1, width), lambda i: (0, 0))
    return pl.pallas_call(
        body, grid=(t // tm,), in_specs=[pl.BlockSpec((tm, width), lambda i: (i, col)), row, row, row],
        out_specs=pl.BlockSpec((tm, width), lambda i: (i, 0)), out_shape=SDS((t, width), BF16),
        name=name, compiler_params=_cp(("parallel",)))(x, w, sc, sh)


def _norm_bwd(x, col, width, w, sc, dh, dres, out_dtype, name):
    t = x.shape[0]
    tm = _row_tile(t)
    has_res = dres is not None

    def body(*refs):
        if has_res:
            x_ref, w_ref, sc_ref, dh_ref, dres_ref, dx_ref, dw_ref, dsc_ref, dsh_ref = refs
        else:
            x_ref, w_ref, sc_ref, dh_ref, dx_ref, dw_ref, dsc_ref, dsh_ref = refs

        @pl.when(pl.program_id(0) == 0)
        def _():
            dw_ref[...] = jnp.zeros_like(dw_ref)
            dsc_ref[...] = jnp.zeros_like(dsc_ref)
            dsh_ref[...] = jnp.zeros_like(dsh_ref)

        xv = x_ref[...]
        dhv = dh_ref[...].astype(F32)
        wv = w_ref[...]
        r = lax.rsqrt(jnp.mean(xv * xv, axis=-1, keepdims=True) + EPS)
        xh = xv * r
        n = xh * wv
        dsh_ref[...] += jnp.sum(dhv, axis=0, keepdims=True)
        dsc_ref[...] += jnp.sum(dhv * n, axis=0, keepdims=True)
        dn = dhv * (1.0 + sc_ref[...])
        dw_ref[...] += jnp.sum(dn * xh, axis=0, keepdims=True)
        dxh = dn * wv
        dx = r * (dxh - xh * jnp.mean(dxh * xh, axis=-1, keepdims=True))
        if has_res:
            dx = dx + dres_ref[...]
        dx_ref[...] = dx.astype(out_dtype)

    row = pl.BlockSpec((1, width), lambda i: (0, 0))
    blk = pl.BlockSpec((tm, width), lambda i: (i, 0))
    in_specs = [pl.BlockSpec((tm, width), lambda i: (i, col)), row, row, blk]
    args = [x, w, sc, dh]
    if has_res:
        in_specs.append(blk)
        args.append(dres)
    return pl.pallas_call(
        body, grid=(t // tm,), in_specs=in_specs, out_specs=[blk, row, row, row],
        out_shape=[SDS((t, width), out_dtype), SDS((1, width), F32), SDS((1, width), F32), SDS((1, width), F32)],
        name=name, compiler_params=_cp(("arbitrary",)))(*args)


def _rope128(x, tc, ts1, ts2):
    return x * tc + pltpu.roll(x, 96, 1) * ts1 + pltpu.roll(x, 32, 1) * ts2


def _rope128_t(d, tc, ts1, ts2):
    return d * tc + pltpu.roll(d * ts1, 32, 1) + pltpu.roll(d * ts2, 96, 1)


def _rope_q(q_raw, tc, ts1, ts2, transpose, out_dtype, name):
    t = q_raw.shape[0]
    tm = _row_tile(t)

    def body(q_ref, tc_ref, s1_ref, s2_ref, o_ref):
        fn = _rope128_t if transpose else _rope128
        for h in range(HEADS):
            base = h * D_QK
            o_ref[:, base:base + LANES] = q_ref[:, base:base + LANES].astype(out_dtype)
            x = q_ref[:, base + LANES:base + D_QK].astype(F32)
            o_ref[:, base + LANES:base + D_QK] = fn(x, tc_ref[...], s1_ref[...], s2_ref[...]).astype(out_dtype)

    blk = pl.BlockSpec((tm, HEADS * D_QK), lambda i: (i, 0))
    tab = pl.BlockSpec((tm, LANES), lambda i: (i, 0))
    return pl.pallas_call(
        body, grid=(t // tm,), in_specs=[blk, tab, tab, tab], out_specs=blk,
        out_shape=SDS((t, HEADS * D_QK), out_dtype), name=name, compiler_params=_cp(("parallel",)))(q_raw, tc, ts1, ts2)


def _k_assemble(kv_raw, p_small, tc, ts1, ts2, name):
    t = kv_raw.shape[0]
    tm = _row_tile(t)

    def body(kn_ref, ps_ref, tc_ref, s1_ref, s2_ref, o_ref):
        kpe = _rope128(ps_ref[...], tc_ref[...], s1_ref[...], s2_ref[...]).astype(BF16)
        for h in range(HEADS):
            o_ref[:, h * D_QK:h * D_QK + LANES] = kn_ref[:, h * LANES:(h + 1) * LANES].astype(BF16)
            o_ref[:, h * D_QK + LANES:(h + 1) * D_QK] = kpe

    tab = pl.BlockSpec((tm, LANES), lambda i: (i, 0))
    return pl.pallas_call(
        body, grid=(t // tm,),
        in_specs=[pl.BlockSpec((tm, HEADS * LANES), lambda i: (i, 0)), tab, tab, tab, tab],
        out_specs=pl.BlockSpec((tm, HEADS * D_QK), lambda i: (i, 0)),
        out_shape=SDS((t, HEADS * D_QK), BF16), name=name, compiler_params=_cp(("parallel",)))(kv_raw, p_small, tc, ts1, ts2)


def _k_assemble_bwd(dk, dv, tc, ts1, ts2, name):
    t = dk.shape[0]
    tm = _row_tile(t)

    def body(dk_ref, dv_ref, tc_ref, s1_ref, s2_ref, o_ref, pe_ref):
        acc = jnp.zeros((tm, LANES), F32)
        for h in range(HEADS):
            o_ref[:, h * LANES:(h + 1) * LANES] = dk_ref[:, h * D_QK:h * D_QK + LANES].astype(BF16)
            acc = acc + dk_ref[:, h * D_QK + LANES:(h + 1) * D_QK].astype(F32)
        o_ref[:, HEADS * LANES:] = dv_ref[...].astype(BF16)
        pe_ref[...] = _rope128_t(acc, tc_ref[...], s1_ref[...], s2_ref[...])

    tab = pl.BlockSpec((tm, LANES), lambda i: (i, 0))
    return pl.pallas_call(
        body, grid=(t // tm,),
        in_specs=[pl.BlockSpec((tm, HEADS * D_QK), lambda i: (i, 0)), pl.BlockSpec((tm, HEADS * LANES), lambda i: (i, 0)),
                  tab, tab, tab],
        out_specs=[pl.BlockSpec((tm, 2 * HEADS * LANES), lambda i: (i, 0)), tab],
        out_shape=[SDS((t, 2 * HEADS * LANES), BF16), SDS((t, LANES), F32)],
        name=name, compiler_params=_cp(("parallel",)))(dk, dv, tc, ts1, ts2)


def _attn_tile(t):
    return _pick(t, (256, 128, 64))


def _attn_fwd(q, k, v, v_off, name):
    t = q.shape[0]
    tq = _attn_tile(t)
    scale = (D_NOPE + D_ROPE) ** -0.5

    def body(q_ref, k_ref, v_ref, o_ref, lse_ref):
        for i in range(t // tq):
            n_k = (i + 1) * tq
            s = _dot_nt(q_ref[i * tq:(i + 1) * tq, :], k_ref[0:n_k, :]) * scale
            row = lax.broadcasted_iota(jnp.int32, (tq, n_k), 0) + i * tq
            colv = lax.broadcasted_iota(jnp.int32, (tq, n_k), 1)
            s = jnp.where(colv <= row, s, -jnp.inf)
            m = jnp.max(s, axis=-1, keepdims=True)
            p = jnp.exp(s - m)
            l = jnp.sum(p, axis=-1, keepdims=True)
            o = _dot(p, v_ref[0:n_k, :]) / l
            o_ref[i * tq:(i + 1) * tq, :] = o.astype(BF16)
            lse_ref[0, i * tq:(i + 1) * tq, :] = m + jnp.log(l)

    return pl.pallas_call(
        body, grid=(HEADS,),
        in_specs=[pl.BlockSpec((t, D_QK), lambda h: (0, h)), pl.BlockSpec((t, D_QK), lambda h: (0, h)),
                  pl.BlockSpec((t, D_V), lambda h: (0, v_off + h))],
        out_specs=[pl.BlockSpec((t, D_V), lambda h: (0, h)), pl.BlockSpec((1, t, 1), lambda h: (h, 0, 0))],
        out_shape=[SDS((t, HEADS * D_V), BF16), SDS((HEADS, t, 1), F32)],
        name=name, compiler_params=_cp(("parallel",)))(q, k, v)


def _attn_bwd(q, k, v, v_off, o, lse, do, name):
    t = q.shape[0]
    tq = _attn_tile(t)
    scale = (D_NOPE + D_ROPE) ** -0.5

    def body(q_ref, k_ref, v_ref, o_ref, lse_ref, do_ref, dq_ref, dk_ref, dv_ref):
        dk_ref[...] = jnp.zeros_like(dk_ref)
        dv_ref[...] = jnp.zeros_like(dv_ref)
        for i in range(t // tq):
            n_k = (i + 1) * tq
            rows = slice(i * tq, (i + 1) * tq)
            qi = q_ref[rows, :]
            doi = do_ref[rows, :].astype(F32)
            s = _dot_nt(qi, k_ref[0:n_k, :]) * scale
            row = lax.broadcasted_iota(jnp.int32, (tq, n_k), 0) + i * tq
            colv = lax.broadcasted_iota(jnp.int32, (tq, n_k), 1)
            p = jnp.where(colv <= row, jnp.exp(s - lse_ref[0, rows, :]), 0.0)
            dp = _dot_nt(doi, v_ref[0:n_k, :])
            delta = jnp.sum(doi * o_ref[rows, :].astype(F32), axis=-1, keepdims=True)
            ds = p * (dp - delta) * scale
            dq_ref[rows, :] = _dot(ds, k_ref[0:n_k, :])
            dk_ref[0:n_k, :] += _dot_tn(ds, qi)
            dv_ref[0:n_k, :] += _dot_tn(p, doi)

    qk_spec = pl.BlockSpec((t, D_QK), lambda h: (0, h))
    v_spec = pl.BlockSpec((t, D_V), lambda h: (0, h))
    return pl.pallas_call(
        body, grid=(HEADS,),
        in_specs=[qk_spec, qk_spec, pl.BlockSpec((t, D_V), lambda h: (0, v_off + h)), v_spec,
                  pl.BlockSpec((1, t, 1), lambda h: (h, 0, 0)), v_spec],
        out_specs=[qk_spec, qk_spec, v_spec],
        out_shape=[SDS((t, HEADS * D_QK), F32), SDS((t, HEADS * D_QK), F32), SDS((t, HEADS * D_V), F32)],
        name=name, compiler_params=_cp(("parallel",)))(q, k, v, o, lse, do)


CONV_COLS = 256


def _conv_pre(u, w_ref, rowi):
    acc = u * w_ref[CONV_WIDTH - 1:CONV_WIDTH, :]
    for sft in range(1, CONV_WIDTH):
        shifted = jnp.where(rowi >= sft, pltpu.roll(u, sft, 0), 0.0)
        acc = acc + shifted * w_ref[CONV_WIDTH - 1 - sft:CONV_WIDTH - sft, :]
    return acc


def _conv_fwd(p_main, conv_w, name):
    t = p_main.shape[0]
    off = 2 * Q_LORA // CONV_COLS

    def body(u_ref, w_ref, y_ref):
        u = u_ref[...]
        rowi = lax.broadcasted_iota(jnp.int32, u.shape, 0)
        pre = _conv_pre(u, w_ref, rowi)
        y_ref[...] = pre * _sigmoid(pre)

    return pl.pallas_call(
        body, grid=(3 * GDN_W // CONV_COLS,),
        in_specs=[pl.BlockSpec((t, CONV_COLS), lambda j: (0, off + j)), pl.BlockSpec((CONV_WIDTH, CONV_COLS), lambda j: (0, j))],
        out_specs=pl.BlockSpec((t, CONV_COLS), lambda j: (0, j)), out_shape=SDS((t, 3 * GDN_W), F32),
        name=name, compiler_params=_cp(("parallel",)))(p_main, conv_w)


def _conv_bwd(p_main, conv_w, dyc, name):
    t = p_main.shape[0]
    off = 2 * Q_LORA // CONV_COLS

    def body(u_ref, w_ref, dy_ref, du_ref, dw_ref):
        u = u_ref[...]
        rowi = lax.broadcasted_iota(jnp.int32, u.shape, 0)
        pre = _conv_pre(u, w_ref, rowi)
        sg = _sigmoid(pre)
        dpre = dy_ref[...] * sg * (1.0 + pre * (1.0 - sg))
        du = dpre * w_ref[CONV_WIDTH - 1:CONV_WIDTH, :]
        dw_ref[CONV_WIDTH - 1:CONV_WIDTH, :] = jnp.sum(dpre * u, axis=0, keepdims=True)
        for sft in range(1, CONV_WIDTH):
            back = jnp.where(rowi < t - sft, pltpu.roll(dpre, t - sft, 0), 0.0)
            du = du + back * w_ref[CONV_WIDTH - 1 - sft:CONV_WIDTH - sft, :]
            shifted = jnp.where(rowi >= sft, pltpu.roll(u, sft, 0), 0.0)
            dw_ref[CONV_WIDTH - 1 - sft:CONV_WIDTH - sft, :] = jnp.sum(dpre * shifted, axis=0, keepdims=True)
        du_ref[...] = du.astype(BF16)

    blk = pl.BlockSpec((t, CONV_COLS), lambda j: (0, j))
    wblk = pl.BlockSpec((CONV_WIDTH, CONV_COLS), lambda j: (0, j))
    return pl.pallas_call(
        body, grid=(3 * GDN_W // CONV_COLS,),
        in_specs=[pl.BlockSpec((t, CONV_COLS), lambda j: (0, off + j)), wblk, blk],
        out_specs=[blk, wblk], out_shape=[SDS((t, 3 * GDN_W), BF16), SDS((CONV_WIDTH, 3 * GDN_W), F32)],
        name=name, compiler_params=_cp(("parallel",)))(p_main, conv_w, dyc)


B_LO, A_LO, A_HI = D_ROPE, D_ROPE + HEADS, D_ROPE + 2 * HEADS


def _softplus(z):
    e = jnp.exp(-jnp.abs(z))
    log1p = jnp.where(e < 0.01, e * (1.0 - e * (0.5 - e * (1.0 / 3.0))), jnp.log(1.0 + e))
    return jnp.maximum(z, 0.0) + log1p


def _gdn_gates(p_small, a_row, dt_row, name):
    t = p_small.shape[0]

    def body(ps_ref, a_ref, dt_ref, g_ref, gc_ref):
        x = ps_ref[...]
        lane = lax.broadcasted_iota(jnp.int32, x.shape, 1)
        is_g = (lane >= A_LO) & (lane < A_HI)
        g = jnp.where(is_g, -jnp.exp(a_ref[...]) * _softplus(x + dt_ref[...]), 0.0)
        g_ref[...] = jnp.where(is_g, g, _sigmoid(x))
        pos = lax.broadcasted_iota(jnp.int32, x.shape, 0) % CHUNK
        acc = g
        sft = 1
        while sft < CHUNK:
            acc = acc + jnp.where(pos >= sft, pltpu.roll(acc, sft, 0), 0.0)
            sft *= 2
        gc_ref[...] = acc

    full = pl.BlockSpec((t, LANES), lambda i: (0, 0))
    row = pl.BlockSpec((1, LANES), lambda i: (0, 0))
    return pl.pallas_call(
        body, grid=(1,), in_specs=[full, row, row], out_specs=[full, full],
        out_shape=[SDS((t, LANES), F32), SDS((t, LANES), F32)], name=name,
        compiler_params=_cp(("arbitrary",)))(p_small, a_row, dt_row)


def _gdn_gates_bwd(p_small, a_row, dt_row, gates, dgates, dkpe, name):
    t = p_small.shape[0]

    def body(ps_ref, a_ref, dt_ref, g_ref, db_ref, dkpe_ref, dp_ref, da_ref, ddt_ref):
        x = ps_ref[...]
        lane = lax.broadcasted_iota(jnp.int32, x.shape, 1)
        is_g = (lane >= A_LO) & (lane < A_HI)
        is_b = (lane >= B_LO) & (lane < A_LO)
        pos = lax.broadcasted_iota(jnp.int32, x.shape, 0) % CHUNK
        acc = jnp.where(is_g, db_ref[...], 0.0)
        sft = 1
        while sft < CHUNK:
            acc = acc + jnp.where(pos < CHUNK - sft, pltpu.roll(acc, t - sft, 0), 0.0)
            sft *= 2
        dg = acc
        gv = g_ref[...]
        dz = jnp.where(is_g, dg * (-jnp.exp(a_ref[...])) * _sigmoid(x + dt_ref[...]), 0.0)
        da_ref[...] = jnp.sum(jnp.where(is_g, dg * gv, 0.0), axis=0, keepdims=True)
        ddt_ref[...] = jnp.sum(dz, axis=0, keepdims=True)
        dlb = jnp.where(is_b, db_ref[...] * gv * (1.0 - gv), 0.0)
        dp_ref[...] = (jnp.where(lane < D_ROPE, dkpe_ref[...], 0.0) + dlb + dz).astype(BF16)

    full = pl.BlockSpec((t, LANES), lambda i: (0, 0))
    row = pl.BlockSpec((1, LANES), lambda i: (0, 0))
    return pl.pallas_call(
        body, grid=(1,), in_specs=[full, row, row, full, full, full], out_specs=[full, row, row],
        out_shape=[SDS((t, LANES), BF16), SDS((1, LANES), F32), SDS((1, LANES), F32)], name=name,
        compiler_params=_cp(("arbitrary",)))(p_small, a_row, dt_row, gates, dgates, dkpe)


def _tri_inv(l, eye):
    x = eye - l
    p = _bdot(l, l, exact=True)
    steps = int(math.log2(CHUNK)) - 1
    for s in range(steps):
        x = x + _bdot(x, p, exact=True)
        if s < steps - 1:
            p = _bdot(p, p, exact=True)
    return x


def _l2n(x3):
    r = lax.rsqrt(jnp.sum(x3 * x3, axis=-1, keepdims=True) + EPS)
    return x3 * r, r


def _head_col(a_ref, lane_lo, n):
    a = a_ref[...]
    lane = lax.broadcasted_iota(jnp.int32, a.shape, 1)
    col = jnp.sum(jnp.where(lane == lane_lo + pl.program_id(0), a, 0.0), axis=-1, keepdims=True)
    return col.reshape(n, CHUNK, 1)


def _gdn_common(q3, k3, v3, b, gc):
    n = q3.shape[0]
    ri = lax.broadcasted_iota(jnp.int32, (n, CHUNK, CHUNK), 1)
    ci = lax.broadcasted_iota(jnp.int32, (n, CHUNK, CHUNK), 2)
    lower, strict = ri >= ci, ri > ci
    eye = (ri == ci).astype(F32)
    gr = jnp.sum(gc * eye, axis=1, keepdims=True)
    qh, rq = _l2n(q3)
    qn = qh * (D_V ** -0.5)
    kn, rk = _l2n(k3)
    dec = jnp.where(lower, jnp.exp(jnp.where(lower, gc - gr, 0.0)), 0.0)
    kb = kn * b
    mm = _bdot_nt(kb, kn)
    tinv = _tri_inv(jnp.where(strict, mm * dec, 0.0), eye)
    gam = jnp.exp(gc)
    u = _bdot(tinv, v3 * b, exact=True)
    w = _bdot(tinv, kb * gam, exact=True)
    qk = _bdot_nt(qn, kn)
    aqk = jnp.where(lower, qk * dec, 0.0)
    gl = gc[:, CHUNK - 1:CHUNK, :]
    kdf = jnp.exp(gl - gc)
    return dict(ri=ri, ci=ci, lower=lower, strict=strict, eye=eye, qh=qh, rq=rq, qn=qn, kn=kn, rk=rk, dec=dec, kb=kb,
                mm=mm, gam=gam, u=u, w=w, qk=qk, aqk=aqk, gl=gl, kdf=kdf, kd=kn * kdf, gr=gr, tinv=tinv)


def _gdn_fwd(yc, p_main, gates, gcum, gn, name):
    t = yc.shape[0]
    n = t // CHUNK
    z_off = (2 * Q_LORA + 3 * GDN_W) // D_V

    def body(q_ref, k_ref, v_ref, z_ref, gt_ref, gcum_ref, gn_ref, o_ref, g_ref, s_ref, u_s, w_s, qg_s, kd_s, a_s, e_s):
        c = _gdn_common(q_ref[...].reshape(n, CHUNK, D_V), k_ref[...].reshape(n, CHUNK, D_V),
                        v_ref[...].reshape(n, CHUNK, D_V), _head_col(gt_ref, B_LO, n), _head_col(gcum_ref, A_LO, n))
        u_s[...] = c["u"]
        w_s[...] = c["w"]
        qg_s[...] = c["qn"] * c["gam"]
        kd_s[...] = c["kd"]
        a_s[...] = c["aqk"]
        e_s[...] = jnp.broadcast_to(jnp.exp(c["gl"]), (n, 1, D_V))

        def step(i, s):
            s_ref[0, i] = s
            v_new = u_s[i] - _dot(w_s[i], s)
            o = _dot(qg_s[i], s) + _dot(a_s[i], v_new)
            o_ref[pl.ds(pl.multiple_of(i * CHUNK, CHUNK), CHUNK), :] = o
            return s * e_s[i] + _dot_tn(kd_s[i], v_new)

        lax.fori_loop(0, n, step, jnp.zeros((D_V, D_V), F32))
        o = o_ref[...]
        zz = z_ref[...]
        on = o * lax.rsqrt(jnp.mean(o * o, axis=-1, keepdims=True) + EPS) * gn_ref[...]
        g_ref[...] = (on * zz * _sigmoid(zz)).astype(BF16)

    col = lambda off: pl.BlockSpec((t, D_V), lambda h: (0, off + h))
    lanes = pl.BlockSpec((t, LANES), lambda h: (0, 0))
    big = pltpu.VMEM((n, CHUNK, D_V), F32)
    return pl.pallas_call(
        body, grid=(HEADS,),
        in_specs=[col(0), col(HEADS), col(2 * HEADS), col(z_off), lanes, lanes, pl.BlockSpec((1, D_V), lambda h: (0, 0))],
        out_specs=[col(0), col(0), pl.BlockSpec((1, n, D_V, D_V), lambda h: (h, 0, 0, 0))],
        out_shape=[SDS((t, GDN_W), F32), SDS((t, GDN_W), BF16), SDS((HEADS, n, D_V, D_V), F32)],
        scratch_shapes=[big, big, big, big, pltpu.VMEM((n, CHUNK, CHUNK), F32), pltpu.VMEM((n, 1, D_V), F32)],
        name=name, compiler_params=_cp(("parallel",)))(yc, yc, yc, p_main, gates, gcum, gn)


def _gdn_bwd(yc, p_main, gates, gcum, gn, o_raw, states, dgated, name):
    t = yc.shape[0]
    n = t // CHUNK
    z_off = (2 * Q_LORA + 3 * GDN_W) // D_V

    def body(q_ref, k_ref, v_ref, z_ref, gt_ref, gcum_ref, gn_ref, o_ref, s_ref, dg_ref,
             dq_ref, dk_ref, dv_ref, dz_ref, dgt_ref, dgn_ref,
             u_s, w_s, qg_s, kd_s, at_s, e_s, do_s, du_s, dw_s, dqg_s, dkd_s, da_s, dat_s, dgs_s):
        @pl.when(pl.program_id(0) == 0)
        def _():
            dgn_ref[...] = jnp.zeros_like(dgn_ref)
            dgt_ref[...] = jnp.zeros_like(dgt_ref)

        o = o_ref[...]
        zz = z_ref[...]
        dgv = dg_ref[...]
        gnv = gn_ref[...]
        r = lax.rsqrt(jnp.mean(o * o, axis=-1, keepdims=True) + EPS)
        oh = o * r
        sg = _sigmoid(zz)
        don = dgv * zz * sg
        dz_ref[...] = (dgv * oh * gnv * sg * (1.0 + zz * (1.0 - sg))).astype(BF16)
        dgn_ref[...] += jnp.sum(don * oh, axis=0, keepdims=True)
        doh = don * gnv
        do_s[...] = (r * (doh - oh * jnp.mean(doh * oh, axis=-1, keepdims=True))).reshape(n, CHUNK, D_V)

        q3 = q_ref[...].reshape(n, CHUNK, D_V)
        k3 = k_ref[...].reshape(n, CHUNK, D_V)
        v3 = v_ref[...].reshape(n, CHUNK, D_V)
        b, gc = _head_col(gt_ref, B_LO, n), _head_col(gcum_ref, A_LO, n)
        c = _gdn_common(q3, k3, v3, b, gc)
        gr = c["gr"]
        ri, ci = c["ri"], c["ci"]
        upper, sup = ci >= ri, ci > ri
        dect = jnp.where(upper, jnp.exp(jnp.where(upper, gr - gc, 0.0)), 0.0)
        tinv_t = lax.dot_general(c["eye"], c["tinv"], (((2,), (2,)), ((0,), (0,))), precision=lax.Precision.HIGHEST,
                                 preferred_element_type=F32)
        u_s[...] = c["u"]
        w_s[...] = c["w"]
        qg_s[...] = c["qn"] * c["gam"]
        kd_s[...] = c["kd"]
        at_s[...] = jnp.where(upper, _bdot_nt(c["kn"], c["qn"]) * dect, 0.0)
        e_s[...] = jnp.broadcast_to(jnp.exp(c["gl"]), (n, 1, D_V))

        def step(j, ds):
            i = n - 1 - j
            s = s_ref[0, i]
            do_i = do_s[i]
            v_new = u_s[i] - _dot(w_s[i], s)
            dvn = _dot(at_s[i], do_i) + _dot(kd_s[i], ds)
            da_s[i] = _dot_nt(do_i, v_new)
            dat_s[i] = _dot_nt(v_new, do_i)
            dqg_s[i] = _dot_nt(do_i, s)
            dw_s[i] = -_dot_nt(dvn, s)
            dkd_s[i] = _dot_nt(v_new, ds)
            du_s[i] = dvn
            dgs_s[i] = jnp.broadcast_to(jnp.sum(jnp.sum(s * ds, axis=1, keepdims=True), axis=0, keepdims=True), (1, D_V))
            return _dot_tn(qg_s[i], do_i) + e_s[i] * ds - _dot_tn(w_s[i], dvn)

        lax.fori_loop(0, n, step, jnp.zeros((D_V, D_V), F32))

        du, dw, dqg, dkd = du_s[...], dw_s[...], dqg_s[...], dkd_s[...]
        lower, strict, dec = c["lower"], c["strict"], c["dec"]
        kn, kb, qn, gam, kdf = c["kn"], c["kb"], c["qn"], c["gam"], c["kdf"]
        drv = _bdot(tinv_t, du, exact=True)
        drk = _bdot(tinv_t, dw, exact=True)
        dl = jnp.where(strict, -(_bdot_nt(drv, c["u"]) + _bdot_nt(drk, c["w"])), 0.0)
        dlt = jnp.where(sup, -(_bdot_nt(c["u"], drv) + _bdot_nt(c["w"], drk)), 0.0)
        da = jnp.where(lower, da_s[...], 0.0)
        dat = jnp.where(upper, dat_s[...], 0.0)
        e = (dl * c["mm"] + da * c["qk"]) * dec
        col_sums = jnp.sum(e, axis=1, keepdims=True)
        dgc = jnp.sum(e, axis=2, keepdims=True) - jnp.sum(col_sums * c["eye"], axis=2, keepdims=True)
        dkb = _bdot(dl * dec, kn) + gam * drk
        dkn = _bdot(dlt * dect, kb) + _bdot(dat * dect, qn) + b * dkb + dkd * kdf
        dqn = _bdot(da * dec, kn) + gam * dqg
        dgam = jnp.sum(drk * kb, axis=-1, keepdims=True) + jnp.sum(dqg * qn, axis=-1, keepdims=True)
        dbeta = jnp.sum(dkb * kn, axis=-1, keepdims=True) + jnp.sum(drv * v3, axis=-1, keepdims=True)
        dv_ref[...] = (b * drv).reshape(t, D_V)
        ee = jnp.sum(dkd * kn, axis=-1, keepdims=True) * kdf
        dgc = dgc + dgam * gam - ee
        rowc = lax.broadcasted_iota(jnp.int32, (n, CHUNK, 1), 1)
        tail = jnp.sum(ee, axis=1, keepdims=True) + dgs_s[...][:, :, 0:1] * jnp.exp(c["gl"])
        dgc = dgc + jnp.where(rowc == CHUNK - 1, tail, 0.0)
        lane = lax.broadcasted_iota(jnp.int32, (t, LANES), 1)
        head = pl.program_id(0)
        dgt_ref[...] += (jnp.where(lane == B_LO + head, dbeta.reshape(t, 1), 0.0)
                         + jnp.where(lane == A_LO + head, dgc.reshape(t, 1), 0.0))
        sc = D_V ** -0.5
        qh, rq, rk = c["qh"], c["rq"], c["rk"]
        dq_ref[...] = (rq * (sc * dqn - qh * jnp.sum(sc * dqn * qh, axis=-1, keepdims=True))).reshape(t, D_V)
        dk_ref[...] = (rk * (dkn - kn * jnp.sum(dkn * kn, axis=-1, keepdims=True))).reshape(t, D_V)

    once = pl.Buffered(1)
    col = lambda off: pl.BlockSpec((t, D_V), lambda h: (0, off + h), pipeline_mode=once)
    out_col = pl.BlockSpec((t, D_V), lambda h: (0, h))
    lanes = pl.BlockSpec((t, LANES), lambda h: (0, 0))
    row = pl.BlockSpec((1, D_V), lambda h: (0, 0))
    big = pltpu.VMEM((n, CHUNK, D_V), F32)
    sq = pltpu.VMEM((n, CHUNK, CHUNK), F32)
    small = pltpu.VMEM((n, 1, D_V), F32)
    return pl.pallas_call(
        body, grid=(HEADS,),
        in_specs=[col(0), col(HEADS), col(2 * HEADS), col(z_off), lanes, lanes, row, col(0),
                  pl.BlockSpec((1, n, D_V, D_V), lambda h: (h, 0, 0, 0), pipeline_mode=once), col(0)],
        out_specs=[out_col, out_col, out_col, out_col, lanes, row],
        out_shape=[SDS((t, GDN_W), F32), SDS((t, GDN_W), F32), SDS((t, GDN_W), F32), SDS((t, GDN_W), BF16),
                   SDS((t, LANES), F32), SDS((1, D_V), F32)],
        scratch_shapes=[big, big, big, big, sq, small, big, big, big, big, big, sq, sq, small],
        name=name, compiler_params=_cp(("arbitrary",)))(yc, yc, yc, p_main, gates, gcum, gn, o_raw, states, dgated)


def _col_tile(d):
    return _pick(d, (512, 256, 128))


def _mix_fwd(y_a, y_b, p_main, name):
    t, d = y_a.shape
    tm, cw = _row_tile(t), _col_tile(d)
    off_a, off_b = MAIN_FIXED // cw, (MAIN_FIXED + d) // cw

    def body(ya_ref, yb_ref, ga_ref, gb_ref, u_ref):
        u_ref[...] = (_sigmoid(ga_ref[...]) * ya_ref[...] + _sigmoid(gb_ref[...]) * yb_ref[...]).astype(BF16)

    blk = pl.BlockSpec((tm, cw), lambda i, j: (i, j))
    return pl.pallas_call(
        body, grid=(t // tm, d // cw),
        in_specs=[blk, blk, pl.BlockSpec((tm, cw), lambda i, j: (i, off_a + j)), pl.BlockSpec((tm, cw), lambda i, j: (i, off_b + j))],
        out_specs=blk, out_shape=SDS((t, d), BF16), name=name,
        compiler_params=_cp(("parallel", "parallel")))(y_a, y_b, p_main, p_main)


def _mix_bwd(du, y_a, y_b, p_main, name):
    t, d = y_a.shape
    tm, cw = _row_tile(t), _col_tile(d)
    off_a, off_b = MAIN_FIXED // cw, (MAIN_FIXED + d) // cw
    nb = d // cw

    def body(du_ref, ya_ref, yb_ref, ga_ref, gb_ref, dya_ref, dyb_ref, dla_ref, dlb_ref):
        duv = du_ref[...]
        ga, gb = _sigmoid(ga_ref[...]), _sigmoid(gb_ref[...])
        dya_ref[...] = (duv * ga).astype(BF16)
        dyb_ref[...] = (duv * gb).astype(BF16)
        dla_ref[...] = (duv * ya_ref[...] * ga * (1.0 - ga)).astype(BF16)
        dlb_ref[...] = (duv * yb_ref[...] * gb * (1.0 - gb)).astype(BF16)

    blk = pl.BlockSpec((tm, cw), lambda i, j: (i, j))
    outs = pl.pallas_call(
        body, grid=(t // tm, nb),
        in_specs=[blk, blk, blk, pl.BlockSpec((tm, cw), lambda i, j: (i, off_a + j)),
                  pl.BlockSpec((tm, cw), lambda i, j: (i, off_b + j))],
        out_specs=[blk, blk, blk, blk],
        out_shape=[SDS((t, d), BF16), SDS((t, d), BF16), SDS((t, d), BF16), SDS((t, d), BF16)], name=name,
        compiler_params=_cp(("parallel", "parallel")))(du, y_a, y_b, p_main, p_main)
    return outs


def _gate_res(x, y, gt, name):
    t, d = x.shape
    tm = _row_tile(t)

    def body(x_ref, y_ref, g_ref, o_ref):
        o_ref[...] = x_ref[...] + g_ref[...] * y_ref[...]

    blk = pl.BlockSpec((tm, d), lambda i: (i, 0))
    return pl.pallas_call(
        body, grid=(t // tm,), in_specs=[blk, blk, pl.BlockSpec((1, d), lambda i: (0, 0))], out_specs=blk,
        out_shape=SDS((t, d), F32), name=name, compiler_params=_cp(("parallel",)))(x, y, gt)


def _gate_res_bwd(dx, y, gt, name):
    t, d = dx.shape
    tm = _row_tile(t)

    def body(dx_ref, y_ref, g_ref, dg_ref, dy_ref):
        @pl.when(pl.program_id(0) == 0)
        def _():
            dg_ref[...] = jnp.zeros_like(dg_ref)

        dxv = dx_ref[...]
        dg_ref[...] += jnp.sum(dxv * y_ref[...], axis=0, keepdims=True)
        dy_ref[...] = (dxv * g_ref[...]).astype(BF16)

    blk = pl.BlockSpec((tm, d), lambda i: (i, 0))
    row = pl.BlockSpec((1, d), lambda i: (0, 0))
    return pl.pallas_call(
        body, grid=(t // tm,), in_specs=[blk, blk, row], out_specs=[row, blk],
        out_shape=[SDS((1, d), F32), SDS((t, d), BF16)], name=name, compiler_params=_cp(("arbitrary",)))(dx, y, gt)


def _swiglu_fwd(gu, name):
    t, f2 = gu.shape
    f = f2 // 2
    tm, cw = _row_tile(t), _col_tile(f)
    nb = f // cw

    def body(g_ref, u_ref, o_ref):
        g = g_ref[...]
        o_ref[...] = (g * _sigmoid(g) * u_ref[...]).astype(BF16)

    return pl.pallas_call(
        body, grid=(t // tm, nb),
        in_specs=[pl.BlockSpec((tm, cw), lambda i, j: (i, j)), pl.BlockSpec((tm, cw), lambda i, j: (i, nb + j))],
        out_specs=pl.BlockSpec((tm, cw), lambda i, j: (i, j)), out_shape=SDS((t, f), BF16), name=name,
        compiler_params=_cp(("parallel", "parallel")))(gu, gu)


def _swiglu_bwd(gu, da, name):
    t, f2 = gu.shape
    f = f2 // 2
    tm, cw = _row_tile(t), _col_tile(f)
    nb = f // cw

    def body(g_ref, u_ref, da_ref, dg_ref, dup_ref):
        g = g_ref[...]
        dav = da_ref[...]
        sg = _sigmoid(g)
        dg_ref[...] = (dav * u_ref[...] * sg * (1.0 + g * (1.0 - sg))).astype(BF16)
        dup_ref[...] = (dav * g * sg).astype(BF16)

    blk = pl.BlockSpec((tm, cw), lambda i, j: (i, j))
    dg, dup = pl.pallas_call(
        body, grid=(t // tm, nb),
        in_specs=[blk, pl.BlockSpec((tm, cw), lambda i, j: (i, nb + j)), blk], out_specs=[blk, blk],
        out_shape=[SDS((t, f), BF16), SDS((t, f), BF16)], name=name,
        compiler_params=_cp(("parallel", "parallel")))(gu, gu, da)
    return dg, dup


def _loss_head(x, w, target, name):
    t, d = x.shape
    tm = _row_tile(t)

    def body(x_ref, w_ref, t_ref, l_ref, dx_ref, dw_ref):
        @pl.when(pl.program_id(0) == 0)
        def _():
            l_ref[...] = jnp.zeros_like(l_ref)
            dw_ref[...] = jnp.zeros_like(dw_ref)

        xv = x_ref[...]
        wv = w_ref[...]
        r = lax.rsqrt(jnp.mean(xv * xv, axis=-1, keepdims=True) + EPS)
        xh = xv * r
        err = xh * wv - t_ref[...]
        per_tok = jnp.mean(err * err, axis=-1, keepdims=True)
        l_ref[...] += 0.5 * jnp.sum(per_tok, axis=0, keepdims=True)
        dy = err * (1.0 / d)
        dw_ref[...] += jnp.sum(dy * xh, axis=0, keepdims=True)
        dxh = dy * wv
        dx_ref[...] = r * (dxh - xh * jnp.mean(dxh * xh, axis=-1, keepdims=True))

    blk = pl.BlockSpec((tm, d), lambda i: (i, 0))
    row = pl.BlockSpec((1, d), lambda i: (0, 0))
    return pl.pallas_call(
        body, grid=(t // tm,), in_specs=[blk, row, blk],
        out_specs=[pl.BlockSpec((1, LANES), lambda i: (0, 0)), blk, row],
        out_shape=[SDS((1, LANES), F32), SDS((t, d), F32), SDS((1, d), F32)], name=name,
        compiler_params=_cp(("arbitrary",)))(x, w, target)


def _adamw(w, g, m, v, tie, name):
    shape = w.shape
    per_layer = isinstance(g, (list, tuple))
    n_layers = shape[0] if (w.ndim == 3 and shape[1] % 8 == 0) else 1
    cols = shape[-1]
    rows = w.size // cols // n_layers
    w, m, v = (a.reshape(n_layers * rows, cols) for a in (w, m, v))
    if not per_layer:
        g = g.reshape(n_layers * rows, cols)
    lanes_padded = -(-cols // LANES) * LANES
    budget_rows = max(8, (24 * 1024 * 1024) // (lanes_padded * 4 * 18))
    tr = rows
    if rows > budget_rows:
        tr = _pick(rows, tuple(c for c in (1024, 512, 256, 128, 64, 32, 16, 8) if c <= budget_rows))
    nrb = rows // tr
    c1 = 1.0 / (1.0 - ADAM_B1 ** ADAM_STEP)
    c2 = 1.0 / (1.0 - ADAM_B2 ** ADAM_STEP)
    n_g = len(g) if per_layer else 1

    def body(*refs):
        w_ref, m_ref, v_ref = refs[:3]
        g_refs = refs[3:3 + n_g]
        outs = refs[4 + n_g:]
        gv = g_refs[0][...]
        for l in range(1, n_g):
            gv = jnp.where(pl.program_id(0) == l, g_refs[l][...], gv)
        mn = ADAM_B1 * m_ref[...] + (1.0 - ADAM_B1) * gv
        vn = ADAM_B2 * v_ref[...] + (1.0 - ADAM_B2) * (gv * gv)
        outs[0][...] = -ADAM_LR * ((mn * c1) / (jnp.sqrt(vn * c2) + ADAM_EPS) + ADAM_WD * w_ref[...])
        outs[1][...] = mn
        outs[2][...] = vn
        if per_layer:
            outs[3][...] = gv

    blk = pl.BlockSpec((tr, cols), lambda l, i: (l * nrb + i, 0))
    g_specs = [pl.BlockSpec((tr, cols), lambda l, i: (i, 0))] * n_g if per_layer else [blk]
    n_out = 4 if per_layer else 3
    outs = pl.pallas_call(
        body, grid=(n_layers, nrb), in_specs=[blk, blk, blk] + g_specs + [pl.BlockSpec((8, LANES), lambda l, i: (0, 0))],
        out_specs=[blk] * n_out, out_shape=[SDS(w.shape, F32)] * n_out, name=name,
        compiler_params=_cp(("parallel", "parallel")))(w, m, v, *(g if per_layer else [g]), tie)
    g_out = outs[3] if per_layer else g
    return (g_out.reshape(shape),) + tuple(o.reshape(shape) for o in outs[:3])


KPE_LO = 2 * Q_LORA
QKVZ_LO = KPE_LO + D_ROPE
BA_LO = QKVZ_LO + 4 * GDN_W
GATE_LO = BA_LO + 2 * HEADS


def _lay_w_in(w_in):
    d = w_in.shape[0]
    main = jnp.concatenate([w_in[:, :KPE_LO], w_in[:, QKVZ_LO:BA_LO], w_in[:, GATE_LO:]], axis=1)
    small = jnp.concatenate([w_in[:, KPE_LO:QKVZ_LO], w_in[:, BA_LO:GATE_LO],
                             jnp.zeros((d, LANES - D_ROPE - 2 * HEADS), w_in.dtype)], axis=1)
    return main, small


def _unlay_w_in(g_main, g_small):
    return jnp.concatenate([g_main[:, :KPE_LO], g_small[:, :D_ROPE], g_main[:, KPE_LO:KPE_LO + 4 * GDN_W],
                            g_small[:, D_ROPE:D_ROPE + 2 * HEADS], g_main[:, MAIN_FIXED:]], axis=1)


def _lay_w_uq(w_uq):
    r = w_uq.reshape(Q_LORA, HEADS, D_NOPE + D_ROPE)
    r = jnp.pad(r, ((0, 0), (0, 0), (0, D_QK - D_NOPE - D_ROPE)))
    return r.reshape(Q_LORA, HEADS * D_QK)


def _unlay_w_uq(g):
    rows = g.shape[0]
    return g.reshape(rows, HEADS, D_QK)[:, :, :D_NOPE + D_ROPE].reshape(rows, HEADS * (D_NOPE + D_ROPE))


def _lay_w_ukv(w_ukv):
    return w_ukv.reshape(KV_LORA, HEADS, 2, D_V).transpose(0, 2, 1, 3).reshape(KV_LORA, 2 * HEADS * D_V)


def _unlay_w_ukv(g):
    rows = g.shape[0]
    return g.reshape(rows, 2, HEADS, D_V).transpose(0, 2, 1, 3).reshape(rows, 2 * HEADS * D_V)


def _lane_row(vec, lo):
    return jnp.pad(vec.reshape(1, -1), ((0, 0), (lo, LANES - lo - vec.shape[0])))


def _rope_tables(positions):
    half = D_ROPE // 2
    inv_freq = 1.0 / (10000.0 ** (jnp.arange(0, D_ROPE, 2, dtype=F32) / D_ROPE))
    ang = positions.astype(F32)[:, None] * inv_freq
    cos, sin = jnp.cos(ang), jnp.sin(ang)
    t = positions.shape[0]
    zeros = lambda n: jnp.zeros((t, n), F32)
    tc = jnp.concatenate([cos, cos, zeros(LANES - D_ROPE)], axis=1)
    ts1 = jnp.concatenate([-sin, zeros(LANES - half)], axis=1)
    ts2 = jnp.concatenate([zeros(half), sin, zeros(LANES - D_ROPE)], axis=1)
    return tc, ts1, ts2


def _layer_fwd(x, mod, wt, tabs, tag, late_weights, after_gate_up=None):
    t, d = x.shape
    sh_a, sc_a, gt_a, sh_f, sc_f, gt_f = mod
    zero_l = jnp.zeros((1, Q_LORA), F32)
    s = dict(x=x)
    s["h1"] = _norm_fwd(x, 0, d, wt["norm_mix"], sc_a, sh_a, f"{tag}_norm_mix")
    s["p_main"] = _mm(s["h1"], wt["w_main"], name=f"{tag}_in_main")
    s["p_small"] = _mm(s["h1"], wt["w_small"], name=f"{tag}_in_small")
    s["cqn"] = _norm_fwd(s["p_main"], 0, Q_LORA, wt["q_a_norm"], zero_l, zero_l, f"{tag}_q_norm")
    s["ckvn"] = _norm_fwd(s["p_main"], 1, KV_LORA, wt["kv_a_norm"], zero_l, zero_l, f"{tag}_kv_norm")
    q_raw = _mm(s["cqn"], wt["w_uq"], name=f"{tag}_uq")
    s["kv_raw"] = _mm(s["ckvn"], wt["w_ukv"], name=f"{tag}_ukv")
    s["q_r"] = _rope_q(q_raw, *tabs, False, BF16, f"{tag}_rope_q")
    s["k_r"] = _k_assemble(s["kv_raw"], s["p_small"], *tabs, f"{tag}_k_asm")
    s["o"], s["lse"] = _attn_fwd(s["q_r"], s["k_r"], s["kv_raw"], HEADS, f"{tag}_attn")
    s["yc"] = _conv_fwd(s["p_main"], wt["conv_w"], f"{tag}_conv")
    s["gates"], s["gcum"] = _gdn_gates(s["p_small"], wt["a_row"], wt["dt_row"], f"{tag}_gates")
    s["o_raw"], s["gated"], s["states"] = _gdn_fwd(s["yc"], s["p_main"], s["gates"], s["gcum"], wt["gdn_norm"], f"{tag}_gdn")
    late, started = late_weights(s["gated"])
    wt = {**wt, **late}
    s["y_a"] = _mm(s["o"], wt["w_o_mla"], name=f"{tag}_o_mla")
    s["y_b"] = _mm(s["gated"], wt["w_o_gdn"], name=f"{tag}_o_gdn")
    s["u"] = _mix_fwd(s["y_a"], s["y_b"], s["p_main"], f"{tag}_mix")
    s["mixo"] = _mm(s["u"], wt["w_o"], name=f"{tag}_o")
    s["x2"] = _gate_res(x, s["mixo"], gt_a, f"{tag}_res_a")
    s["h2"] = _norm_fwd(s["x2"], 0, d, wt["norm_ffn"] + started, sc_f, sh_f, f"{tag}_norm_ffn")
    s["gu"] = _mm(s["h2"], wt["w_gate_up"], name=f"{tag}_gate_up")
    if after_gate_up is not None:
        gt_f = gt_f + after_gate_up(s["gu"])
    s["a"] = _swiglu_fwd(s["gu"], f"{tag}_swiglu")
    s["f"] = _mm(s["a"], wt["w_down"], name=f"{tag}_down")
    return _gate_res(s["x2"], s["f"], gt_f, f"{tag}_res_f"), s, wt


def _layer_bwd(dx3, s, mod, wt, tabs, tag, after_ffn=None, after_gdn=None):
    x = s["x"]
    t, d = x.shape
    sh_a, sc_a, gt_a, sh_f, sc_f, gt_f = mod
    zero_l = jnp.zeros((1, Q_LORA), F32)
    g = {}
    dgt_f, df = _gate_res_bwd(dx3, s["f"], gt_f, f"{tag}_b_res_f")
    da = _mm(df, wt["w_down"], tb=True, name=f"{tag}_b_down_x")
    g["w_down"] = _mm(s["a"].T, df, out_dtype=BF16, name=f"{tag}_b_down_w")
    dgate, dup = _swiglu_bwd(s["gu"], da, f"{tag}_b_swiglu")
    dgu = jnp.concatenate([dgate, dup], axis=1)
    dh2 = _mm(dgu, wt["w_gate_up"], tb=True, name=f"{tag}_b_gate_up_x")
    g["w_gate_up"] = _mm(s["h2"].T, dgu, out_dtype=BF16, name=f"{tag}_b_gate_up_w")
    if after_ffn is not None:
        gt_a = gt_a + after_ffn(g)
    dx2, g["norm_ffn"], dsc_f, dsh_f = _norm_bwd(s["x2"], 0, d, wt["norm_ffn"], sc_f, dh2, dx3, F32, f"{tag}_b_norm_ffn")
    dgt_a, dmixo = _gate_res_bwd(dx2, s["mixo"], gt_a, f"{tag}_b_res_a")
    du = _mm(dmixo, wt["w_o"], tb=True, name=f"{tag}_b_o_x")
    g["w_o"] = _mm(s["u"].T, dmixo, out_dtype=BF16, name=f"{tag}_b_o_w")
    dy_a, dy_b, dl_a, dl_b = _mix_bwd(du, s["y_a"], s["y_b"], s["p_main"], f"{tag}_b_mix")
    dgated = _mm(dy_b, wt["w_o_gdn"], tb=True, name=f"{tag}_b_o_gdn_x")
    g["w_o_gdn"] = _mm(s["gated"].T, dy_b, out_dtype=BF16, name=f"{tag}_b_o_gdn_w")
    dq_c, dk_c, dv_c, dz, dgates, g["gdn_norm"] = _gdn_bwd(
        s["yc"], s["p_main"], s["gates"], s["gcum"], wt["gdn_norm"], s["o_raw"], s["states"], dgated, f"{tag}_b_gdn")
    du_conv, g["conv_w"] = _conv_bwd(s["p_main"], wt["conv_w"], jnp.concatenate([dq_c, dk_c, dv_c], axis=1), f"{tag}_b_conv")
    do = _mm(dy_a, wt["w_o_mla"], tb=True, name=f"{tag}_b_o_mla_x")
    g["w_o_mla"] = _mm(s["o"].T, dy_a, out_dtype=BF16, name=f"{tag}_b_o_mla_w")
    dq_r, dk_r, dv = _attn_bwd(s["q_r"], s["k_r"], s["kv_raw"], HEADS, s["o"], s["lse"], do, f"{tag}_b_attn")
    q_a_norm = wt["q_a_norm"]
    if after_gdn is not None:
        q_a_norm = q_a_norm + after_gdn(du_conv)
    dq_raw = _rope_q(dq_r, *tabs, True, BF16, f"{tag}_b_rope_q")
    dkv_raw, dkpe = _k_assemble_bwd(dk_r, dv, *tabs, f"{tag}_b_k_asm")
    dcqn = _mm(dq_raw, wt["w_uq"], tb=True, name=f"{tag}_b_uq_x")
    g["w_uq"] = _mm(s["cqn"].T, dq_raw, out_dtype=BF16, name=f"{tag}_b_uq_w")
    dckvn = _mm(dkv_raw, wt["w_ukv"], tb=True, name=f"{tag}_b_ukv_x")
    g["w_ukv"] = _mm(s["ckvn"].T, dkv_raw, out_dtype=BF16, name=f"{tag}_b_ukv_w")
    dc_q, g["q_a_norm"], _, _ = _norm_bwd(s["p_main"], 0, Q_LORA, q_a_norm, zero_l, dcqn, None, BF16, f"{tag}_b_q_norm")
    dc_kv, g["kv_a_norm"], _, _ = _norm_bwd(s["p_main"], 1, KV_LORA, wt["kv_a_norm"], zero_l, dckvn, None, BF16,
                                            f"{tag}_b_kv_norm")
    dp_small, g["a_row"], g["dt_row"] = _gdn_gates_bwd(
        s["p_small"], wt["a_row"], wt["dt_row"], s["gates"], dgates, dkpe, f"{tag}_b_gates")
    dp_main = jnp.concatenate([dc_q, dc_kv, du_conv, dz, dl_a, dl_b], axis=1)
    h1t = s["h1"].T
    dh1 = _mm(dp_small, wt["w_small"], tb=True, name=f"{tag}_b_in_small_x")
    dh1 = _mm(dp_main, wt["w_main"], tb=True, acc_in=dh1, name=f"{tag}_b_in_main_x")
    g["w_main"] = _mm(h1t, dp_main, out_dtype=BF16, name=f"{tag}_b_in_main_w")
    g["w_small"] = _mm(h1t, dp_small, out_dtype=BF16, name=f"{tag}_b_in_small_w")
    dx, g["norm_mix"], dsc_a, dsh_a = _norm_bwd(x, 0, d, wt["norm_mix"], sc_a, dh1, dx2, F32, f"{tag}_b_norm_mix")
    return dx, (dsh_a, dsc_a, dgt_a, dsh_f, dsc_f, dgt_f), g


def _layer_weights(big, full, l):
    w_main, w_small = _lay_w_in(big["w_in"])
    return dict(
        w_main=w_main, w_small=w_small, w_uq=_lay_w_uq(big["w_uq"]), w_ukv=_lay_w_ukv(big["w_ukv"]),
        conv_w=full["conv_w"][l],
        norm_mix=full["norm_mix"][l][None], norm_ffn=full["norm_ffn"][l][None],
        q_a_norm=full["q_a_norm"][l][None], kv_a_norm=full["kv_a_norm"][l][None], gdn_norm=full["gdn_norm"][l][None],
        a_row=_lane_row(full["A_log"][l], A_LO), dt_row=_lane_row(full["dt_bias"][l], A_LO))


def _small_grads_ref_layout(g):
    return dict(
        conv_w=g["conv_w"], norm_mix=g["norm_mix"][0], norm_ffn=g["norm_ffn"][0], q_a_norm=g["q_a_norm"][0],
        kv_a_norm=g["kv_a_norm"][0], gdn_norm=g["gdn_norm"][0], A_log=g["a_row"][0, A_LO:A_HI],
        dt_bias=g["dt_row"][0, A_LO:A_HI])


def _local_step(x, mods, target, final_norm, full, positions):
    tabs = _rope_tables(positions)
    depth = len(mods)
    wts, saved = [None] * depth, []
    h = x
    for l in range(depth):
        early = _layer_weights({n: full[n][l] for n in FIRST_NEEDED}, full, l)
        h, s, wts[l] = _layer_fwd(h, mods[l], early, tabs, f"l{l}", lambda _, l=l: ({n: full[n][l] for n in LATER_NEEDED}, 0.0))
        saved.append(s)
    loss, dh, dfn = _loss_head(h, final_norm[None], target, "loss_head")
    dmods, grads = [None] * depth, [None] * depth
    for l in reversed(range(depth)):
        dh, dmods[l], grads[l] = _layer_bwd(dh, saved[l], mods[l], wts[l], tabs, f"l{l}")
    return loss, dh, dmods, grads, dfn[0]


HBM_SPEC = pl.BlockSpec(memory_space=pl.ANY)
VMEM_SPEC = pl.BlockSpec(memory_space=pltpu.VMEM)
N_CHIPS = 4
N_DEV = 8


def _me():
    return lax.axis_index("x"), lax.axis_index("y"), lax.axis_index("c")


def _flip(pos, f):
    mx, my, mc = pos
    fx, fy, fc = (f >> 2) & 1, (f >> 1) & 1, f & 1
    return ((mx + fx) % 2, (my + fy) % 2, (mc + fc) % 2)


def _all_gather_small(x, name):
    r, n = x.shape

    def body(x_ref, out_ref, send_sems, recv_sems, local_sem):
        me = _me()
        row = lambda p: 4 * p[0] + 2 * p[1] + p[2]
        mine = pltpu.make_async_copy(x_ref, out_ref.at[row(me)], local_sem)
        mine.start()

        def copy(f, origin):
            return pltpu.make_async_remote_copy(
                src_ref=x_ref, dst_ref=out_ref.at[row(origin)], send_sem=send_sems.at[f - 1], recv_sem=recv_sems.at[f - 1],
                device_id=_flip(me, f), device_id_type=MESH)

        sends = [copy(f, me) for f in range(1, N_DEV)]
        for cp in sends:
            cp.start()
        for f in range(1, N_DEV):
            copy(f, _flip(me, f)).wait_recv()
        for cp in sends:
            cp.wait_send()
        mine.wait()

    return pl.pallas_call(
        body, out_shape=SDS((N_DEV, r, n), x.dtype), in_specs=[VMEM_SPEC], out_specs=VMEM_SPEC,
        scratch_shapes=[pltpu.SemaphoreType.DMA((N_DEV - 1,)), pltpu.SemaphoreType.DMA((N_DEV - 1,)), pltpu.SemaphoreType.DMA],
        name=name, compiler_params=pltpu.CompilerParams(vmem_limit_bytes=VMEM_LIMIT))(x)


CHIP_FLIPS = (2, 4, 6)
SIBLING = 1


def _chip_of(pos):
    return 2 * pos[0] + pos[1]


def _sum_chips(p, got, chip, half, name):
    _, kh, ns = p.shape
    tr = _pick(kh, (256, 128, 64, 32, 16))
    nrb = kh // tr

    def body(c_ref, h_ref, a_ref, b_ref, o_ref):
        acc = a_ref[0].astype(F32)
        for j in range(3):
            acc = acc + b_ref[j].astype(F32)
        o_ref[...] = acc

    grid_spec = pltpu.PrefetchScalarGridSpec(
        num_scalar_prefetch=2, grid=(nrb,),
        in_specs=[pl.BlockSpec((1, tr, ns), lambda i, c, h: (c[0], i, 0)),
                  pl.BlockSpec((3, tr, ns), lambda i, c, h: (0, i, 0))],
        out_specs=pl.BlockSpec((tr, ns), lambda i, c, h: (h[0] * nrb + i, 0)))
    return pl.pallas_call(body, grid_spec=grid_spec, out_shape=SDS((2 * kh, ns), F32), name=name,
                          compiler_params=_cp(("parallel",)))(chip, half, p, got)


SEM_SPEC = pl.BlockSpec(memory_space=pltpu.SEMAPHORE)
HBM_ONLY = pl.BlockSpec(memory_space=pltpu.HBM)
DATAFLOW = pltpu.SideEffectType.DATAFLOW_SIDE_EFFECTING


def _in_hbm(a):
    return pltpu.with_memory_space_constraint(a, pltpu.HBM)


def _copies_start(name, srcs, lands, plan, n_copies):
    ns, nl = len(srcs), len(lands)

    def body(*refs):
        src_refs, land_refs = refs[:ns], refs[ns:ns + nl]
        send_sems, recv_sems = refs[ns + nl], refs[ns + nl + 1]
        token = refs[-1]
        for i, (src, dst, peer) in enumerate(plan(_me(), src_refs, land_refs)):
            pltpu.make_async_remote_copy(src_ref=src, dst_ref=dst, send_sem=send_sems.at[i], recv_sem=recv_sems.at[i],
                                         device_id=peer, device_id_type=MESH).start()
        token[...] = jnp.zeros_like(token)

    outs = pl.pallas_call(
        body, name=name,
        out_shape=(pltpu.SemaphoreType.DMA((n_copies,)), pltpu.SemaphoreType.DMA((n_copies,)),
                   *[pltpu.HBM(l.shape, l.dtype) for l in lands], SDS((8, LANES), F32)),
        in_specs=[HBM_ONLY] * (ns + nl), out_specs=(SEM_SPEC, SEM_SPEC, *[HBM_ONLY] * nl, VMEM_SPEC),
        input_output_aliases={ns + i: 2 + i for i in range(nl)},
        compiler_params=pltpu.CompilerParams(has_side_effects=DATAFLOW),
    )(*[_in_hbm(s) for s in srcs], *[_in_hbm(l) for l in lands])
    return outs[0], outs[1], list(outs[2:2 + nl]), outs[-1]


def _copies_wait(name, srcs, lands, send_sems, recv_sems, plan, after):
    ns, nl = len(srcs), len(lands)

    def body(*refs):
        src_refs, land_refs = refs[:ns], refs[ns:ns + nl]
        send_ref, recv_ref = refs[ns + nl], refs[ns + nl + 1]
        for i, (src, dst, peer) in enumerate(plan(_me(), src_refs, land_refs)):
            cp = pltpu.make_async_remote_copy(src_ref=src, dst_ref=dst, send_sem=send_ref.at[i], recv_sem=recv_ref.at[i],
                                              device_id=peer, device_id_type=MESH)
            cp.wait_send()
            cp.wait_recv()

    outs = pl.pallas_call(
        body, name=name, out_shape=[pltpu.HBM(l.shape, l.dtype) for l in lands],
        in_specs=[HBM_ONLY] * (ns + nl) + [SEM_SPEC, SEM_SPEC, HBM_SPEC], out_specs=[HBM_ONLY] * nl,
        input_output_aliases={ns + i: i for i in range(nl)},
        compiler_params=pltpu.CompilerParams(has_side_effects=DATAFLOW),
    )(*[_in_hbm(s) for s in srcs], *lands, send_sems, recv_sems, after)
    return list(outs)


def _half(ref, rows, axis):
    idx = [slice(None)] * axis + [rows]
    return ref.at[tuple(idx)]


def _gather_plans(layer, halves):
    def ici(me, srcs, lands):
        out = []
        for a, kh in enumerate(halves):
            rows = pl.ds(pl.multiple_of(me[2] * kh, 16), kh)
            for k in range(3):
                out.append((srcs[a].at[layer, rows], lands[a].at[_chip_of(me), rows], _flip(me, CHIP_FLIPS[k])))
        return out

    def d2d(me, srcs, lands):
        out = []
        for a, kh in enumerate(halves):
            rows = pl.ds(pl.multiple_of(me[2] * kh, 16), kh)
            for k in range(3):
                slab = lands[a].at[_chip_of(_flip(me, CHIP_FLIPS[k])), rows]
                out.append((slab, slab, _flip(me, SIBLING)))
        return out

    return ici, d2d


def _to_sibling_plan(halves, axes):
    def plan(me, srcs, lands):
        out = []
        for a, (kh, axis) in enumerate(zip(halves, axes)):
            rows = pl.ds(pl.multiple_of((1 - me[2]) * kh, 16), kh)
            out.append((_half(srcs[a], rows, axis), lands[a], _flip(me, SIBLING)))
        return out

    return plan


def _to_chips_plan(n_arr):
    def plan(me, srcs, lands):
        out = []
        for a in range(n_arr):
            for k in range(3):
                peer = _flip(me, CHIP_FLIPS[k])
                out.append((srcs[a].at[_chip_of(peer)], lands[a].at[k], peer))
        return out

    return plan


def _swap_plan(halves):
    def plan(me, srcs, lands):
        out = []
        for a, kh in enumerate(halves):
            rows = pl.ds(pl.multiple_of(me[2] * kh, 16), kh)
            out.append((lands[a].at[rows], lands[a].at[rows], _flip(me, SIBLING)))
        return out

    return plan


def _add_half(g, got, half, col_shards, name):
    s, kh, n = got.shape
    tr = _pick(kh, (512, 256, 128, 64, 32, 16))
    nrb = kh // tr
    width = n // N_CHIPS if col_shards else n
    cw = _pick(width, (1024, 512, 256, 128))
    per = width // cw

    def body(h_ref, a_ref, b_ref, o_ref):
        o_ref[...] = (a_ref[...].astype(F32) + b_ref[...].astype(F32)).astype(o_ref.dtype)

    in_specs = [pl.BlockSpec((None, tr, cw), lambda j, i, c, h: (j, h[0] * nrb + i, c)),
                pl.BlockSpec((None, tr, cw), lambda j, i, c, h: (j, i, c))]
    if col_shards:
        assert s == 1
        out_spec = pl.BlockSpec((None, tr, cw), lambda j, i, c, h: (c // per, i, c % per))
        out_shape = SDS((N_CHIPS, kh, width), g.dtype)
    else:
        out_spec, out_shape = in_specs[1], SDS((s, kh, n), g.dtype)
    grid_spec = pltpu.PrefetchScalarGridSpec(num_scalar_prefetch=1, grid=(s, nrb, n // cw), in_specs=in_specs,
                                             out_specs=out_spec)
    return pl.pallas_call(body, grid_spec=grid_spec, out_shape=out_shape, name=name,
                          compiler_params=_cp(("parallel", "parallel", "parallel")))(half, g, got)


def _sum_devices(g, name):
    _, _, n = g.shape

    def body(g_ref, o_ref):
        acc = g_ref[0]
        for k in range(1, N_DEV):
            acc = acc + g_ref[k]
        o_ref[...] = acc

    return pl.pallas_call(body, out_shape=SDS((1, n), F32), in_specs=[VMEM_SPEC], out_specs=VMEM_SPEC, name=name)(g)


def _silu_rows(c, name):
    def body(c_ref, o_ref):
        v = c_ref[...]
        o_ref[...] = v * _sigmoid(v)

    return pl.pallas_call(body, out_shape=SDS(c.shape, F32), in_specs=[VMEM_SPEC], out_specs=VMEM_SPEC, name=name)(c)


BIG = (("w_in", 2), ("w_uq", 2), ("w_ukv", 2), ("w_o_mla", 2), ("w_o_gdn", 2), ("w_o", 1), ("w_gate_up", 2), ("w_down", 1))
KERNEL_BIG = ("w_main", "w_small", "w_uq", "w_ukv", "w_o_mla", "w_o_gdn", "w_o", "w_gate_up", "w_down")
COL_SHARDED_AS_IS = ("w_o_mla", "w_o_gdn", "w_gate_up")
ROW_SHARDED = ("w_o", "w_down")
FIRST_NEEDED = ("w_in", "w_uq", "w_ukv")
LATER_NEEDED = ("w_o_mla", "w_o_gdn", "w_o", "w_gate_up", "w_down")
FFN_GRADS = ("w_gate_up", "w_down")
MIXER_GRADS = ("w_in", "w_uq", "w_ukv", "w_o_mla", "w_o_gdn", "w_o")
MIXER_GRADS_KERNEL = ("w_main", "w_small", "w_uq", "w_ukv", "w_o_mla", "w_o_gdn", "w_o")
SMALL = ("norm_mix", "norm_ffn", "q_a_norm", "kv_a_norm", "A_log", "dt_bias", "gdn_norm")
WEIGHTS = ("w_ada", "b_ada", "norm_mix", "norm_ffn", "w_in", "q_a_norm", "kv_a_norm", "w_uq", "w_ukv", "w_o_mla", "conv_w",
           "A_log", "dt_bias", "gdn_norm", "w_o_gdn", "w_o", "w_gate_up", "w_down", "final_norm")
ADA_PAD = 16
K_PAD = 128


def _pad_to(a, n, axis):
    pad = [(0, 0)] * a.ndim
    pad[axis] = (0, n - a.shape[axis])
    return jnp.pad(a, pad)


def kernel(x, c, positions, w_ada, b_ada, norm_mix, norm_ffn, w_in, q_a_norm, kv_a_norm, w_uq, w_ukv, w_o_mla, conv_w, A_log, dt_bias, gdn_norm, w_o_gdn, w_o, w_gate_up, w_down, final_norm, loss_target, m_w_ada, m_b_ada, m_norm_mix, m_norm_ffn, m_w_in, m_q_a_norm, m_kv_a_norm, m_w_uq, m_w_ukv, m_w_o_mla, m_conv_w, m_A_log, m_dt_bias, m_gdn_norm, m_w_o_gdn, m_w_o, m_w_gate_up, m_w_down, m_final_norm, v_w_ada, v_b_ada, v_norm_mix, v_norm_ffn, v_w_in, v_q_a_norm, v_kv_a_norm, v_w_uq, v_w_ukv, v_w_o_mla, v_conv_w, v_A_log, v_dt_bias, v_gdn_norm, v_w_o_gdn, v_w_o, v_w_gate_up, v_w_down, v_final_norm):
    env = dict(locals())
    w = {n: env[n] for n in WEIGHTS}
    depth, d = norm_mix.shape
    t = x.shape[1]
    me = _me()
    chip = _chip_of(me)
    dev = 4 * me[0] + 2 * me[1] + me[2]
    ada_cols = w_ada.shape[2]

    half_idx = me[2].astype(jnp.int32).reshape(1)
    chip_idx = chip.astype(jnp.int32).reshape(1)
    w16 = {n: w[n].astype(BF16) for n, _ in BIG}
    shard_axis = dict(BIG)
    gather = {}

    def start_group(key, layer, names, dep):
        srcs = [w16[n] for n in names]
        plans = _gather_plans(layer, [a.shape[1] // 2 for a in srcs])
        landing = [lax.empty((N_CHIPS,) + a.shape[1:], BF16) for a in srcs]
        send_s, recv_s, landing, tok = _copies_start(f"gather_{key}_ici_start", srcs + [dep], landing, plans[0], 3 * len(names))
        gather[key] = dict(layer=layer, names=names, srcs=srcs, plans=plans, ici=(send_s, recv_s, landing), tok=tok)
        return tok[0, 0]

    def pass_to_sibling(key, after):
        st = gather[key]
        send_s, recv_s, landing = st["ici"]
        landing = _copies_wait(f"gather_{key}_ici_wait", st["srcs"] + [st["tok"]], landing, send_s, recv_s, st["plans"][0],
                               st["tok"] if after is None else after)
        st["d2d"] = _copies_start(f"gather_{key}_d2d_start", [], landing, st["plans"][1], 3 * len(st["names"]))
        return st["d2d"][3]

    def gathered(key):
        st = gather[key]
        send_s, recv_s, landing, tok = st["d2d"]
        landing = _copies_wait(f"gather_{key}_d2d_wait", [], landing, send_s, recv_s, st["plans"][1], tok)
        return {n: jnp.concatenate([jnp.where(chip == j, own[st["layer"]], got[j]) for j in range(N_CHIPS)],
                                   axis=shard_axis[n] - 1)
                for n, own, got in zip(st["names"], st["srcs"], landing)}

    first_started = start_group("l0a", 0, FIRST_NEEDED, jnp.zeros((8, LANES), F32))
    c = c + first_started
    full = {}
    conv_all = _all_gather_small(conv_w.reshape(1, -1) + first_started, "gather_conv").reshape((N_DEV,) + conv_w.shape)
    full["conv_w"] = jnp.concatenate([conv_all[2 * j] for j in range(N_CHIPS)], axis=2)
    for n in SMALL:
        full[n] = w[n]

    c_all = _all_gather_small(c, "gather_c").reshape(N_DEV, d)
    c_act = _silu_rows(_pad_to(c_all, ADA_PAD, 0), "silu_c")
    b_cols = lax.dynamic_slice_in_dim(b_ada, chip * ada_cols, ada_cols, axis=1)
    mod_cols = jnp.stack([
        _mm(c_act, w_ada[l], acc_in=jnp.broadcast_to(b_cols[l][None], (ADA_PAD, ada_cols)), name=f"ada_l{l}")[:N_DEV]
        for l in range(depth)])
    mod_all = _all_gather_small(mod_cols.reshape(depth * N_DEV, ada_cols), "gather_mod")
    mod_all = mod_all.reshape(N_DEV, depth, N_DEV, ada_cols)
    mods = []
    for l in range(depth):
        mine = jnp.concatenate([lax.dynamic_index_in_dim(mod_all[2 * j, l], dev, axis=0, keepdims=True)
                                for j in range(N_CHIPS)], axis=1)
        mods.append(tuple(mine[:, i * d:(i + 1) * d] for i in range(6)))

    tabs = _rope_tables(positions[0])
    tie = start_group("l0b", 0, LATER_NEEDED, pass_to_sibling(
        "l0a", mods[depth - 1][5][:, :LANES] + full["conv_w"].reshape(1, -1)[:, :LANES]))

    def late_weights(key, next_key, next_layer, behind):
        tok = pass_to_sibling(key, behind)
        started = 0.0 if next_key is None else start_group(next_key, next_layer, FIRST_NEEDED, tok)
        return gathered(key), started

    def next_later_group(behind):
        return start_group("l1b", 1, LATER_NEEDED, pass_to_sibling("l1a", behind))

    wts, saved = [None] * depth, [None] * depth
    tied = (mods[0][0] + tie,) + mods[0][1:]
    h, saved[0], wts[0] = _layer_fwd(x[0], tied, _layer_weights(gathered("l0a"), full, 0), tabs, "l0",
                                     functools.partial(late_weights, "l0b", "l1a", 1), next_later_group)
    h, saved[1], wts[1] = _layer_fwd(h, mods[1], _layer_weights(gathered("l1a"), full, 1), tabs, "l1",
                                     functools.partial(late_weights, "l1b", None, None))
    loss_part, dh, dfn = _loss_head(h, final_norm[None], loss_target[0], "loss_head")
    dfn = dfn[0]

    def col_shards(g):
        return g.reshape(g.shape[0], N_CHIPS, g.shape[1] // N_CHIPS).transpose(1, 0, 2)

    def reduce_scatter_stages(tag, g, knames, names):
        srcs = [g[n].reshape(N_CHIPS, -1, g[n].shape[1]) if n in ROW_SHARDED else g[n] for n in knames]
        axes = [1 if n in ROW_SHARDED else 0 for n in knames]
        halves = [a.shape[ax] // 2 for a, ax in zip(srcs, axes)]
        got_shapes = [a.shape[:ax] + (kh,) + a.shape[ax + 1:] for a, ax, kh in zip(srcs, axes, halves)]
        plan_a, plan_c = _to_sibling_plan(halves, axes), _to_chips_plan(len(names))
        st, out = {}, {}
        st["a"] = _copies_start(f"{tag}_sibling_start", srcs, [lax.empty(sh, BF16) for sh in got_shapes], plan_a, len(srcs))

        def after_or(tok, after):
            return tok if after is None else after

        def stage0(after):
            send_s, recv_s, landing, tok = st["a"]
            got = _copies_wait(f"{tag}_sibling_wait", srcs, landing, send_s, recv_s, plan_a, after_or(tok, after))
            sums = {}
            for n, a, b in zip(knames, srcs, got):
                a3, b3 = (v if v.ndim == 3 else v[None] for v in (a, b))
                r = _add_half(a3, b3, half_idx, n in COL_SHARDED_AS_IS, f"{tag}_add_{n}")
                sums[n] = r if (n in COL_SHARDED_AS_IS or n in ROW_SHARDED) else r[0]
            if "w_main" in sums:
                sums["w_in"] = col_shards(_unlay_w_in(sums["w_main"], sums["w_small"]))
                sums["w_uq"] = col_shards(_unlay_w_uq(sums["w_uq"]))
                sums["w_ukv"] = col_shards(_unlay_w_ukv(sums["w_ukv"]))
            st["p"] = [sums[n] for n in names]
            st["c"] = _copies_start(f"{tag}_chips_start", st["p"], [lax.empty((3,) + p.shape[1:], BF16) for p in st["p"]],
                                    plan_c, 3 * len(names))
            return st["c"][3][0, 0]

        def stage1(after):
            send_s, recv_s, landing, tok = st["c"]
            got = _copies_wait(f"{tag}_chips_wait", st["p"], landing, send_s, recv_s, plan_c, after_or(tok, after))
            sums = [_sum_chips(p, q, chip_idx, half_idx, f"{tag}_sum_{n}") for n, p, q in zip(names, st["p"], got)]
            plan_e = _swap_plan([r.shape[0] // 2 for r in sums])
            st["e"] = _copies_start(f"{tag}_swap_start", [], sums, plan_e, len(names)) + (plan_e,)
            return st["e"][3][0, 0]

        def stage2(after):
            send_s, recv_s, landing, tok, plan_e = st["e"]
            got = _copies_wait(f"{tag}_swap_wait", [], landing, send_s, recv_s, plan_e, after_or(tok, after))
            out.update(zip(names, got))

        return (stage0, stage1, stage2), out, st["a"][3][0, 0]

    dmods, grads, groups = [None] * depth, [None] * depth, {}

    def ffn_group_l1(g):
        groups["l1_ffn"] = reduce_scatter_stages("rs_l1_ffn", g, FFN_GRADS, FFN_GRADS)
        return groups["l1_ffn"][2]

    dh, dmods[1], grads[1] = _layer_bwd(dh, saved[1], mods[1], wts[1], tabs, "l1", after_ffn=ffn_group_l1)
    groups["l1_mix"] = reduce_scatter_stages("rs_l1_mix", grads[1], MIXER_GRADS_KERNEL, MIXER_GRADS)
    tied = mods[0][:5] + (mods[0][5] + groups["l1_mix"][2],)

    def ffn_group_l0(g):
        behind = g["w_gate_up"]
        tok = groups["l1_ffn"][0][0](behind) + groups["l1_mix"][0][0](behind)
        groups["l0_ffn"] = reduce_scatter_stages("rs_l0_ffn", g, FFN_GRADS, FFN_GRADS)
        return tok + groups["l0_ffn"][2]

    def after_gdn_l0(behind):
        return groups["l0_ffn"][0][0](behind)

    dx, dmods[0], grads[0] = _layer_bwd(dh, saved[0], tied, wts[0], tabs, "l0", after_ffn=ffn_group_l0, after_gdn=after_gdn_l0)
    groups["l0_mix"] = reduce_scatter_stages("rs_l0_mix", grads[0], MIXER_GRADS_KERNEL, MIXER_GRADS)
    for key in ("l1_ffn", "l1_mix", "l0_ffn"):
        groups[key][0][1](dx)
    g_out, deltas, new_m, new_v = {}, {}, {}, {}

    def reduced(names):
        for n in names:
            g_out[n] = [groups[f"l{l}_ffn" if n in FFN_GRADS else f"l{l}_mix"][1][n] for l in range(depth)]

    def update(names, tie):
        for n in names:
            g_out[n], deltas[n], new_m[n], new_v[n] = _adamw(w[n], g_out[n], env["m_" + n], env["v_" + n], tie,
                                                             f"adamw_{n}")

    small = [_small_grads_ref_layout(grads[l]) for l in range(depth)]
    small_parts = [jnp.concatenate(dmods[l], axis=1).reshape(-1) for l in range(depth)]
    small_parts += [jnp.stack([small[l][n] for l in range(depth)]).reshape(-1) for n in SMALL]
    small_parts += [dfn, loss_part[0, :1]]
    small_sizes = [p.shape[0] for p in small_parts]
    packed = jnp.concatenate(small_parts)
    n_small = -(-packed.shape[0] // LANES) * LANES
    small_all = _all_gather_small(_pad_to(packed, n_small, 0).reshape(1, n_small), "gather_small_grads")
    small_sum = _sum_devices(small_all, "sum_small_grads")[0]
    offs = [0]
    for sz in small_sizes:
        offs.append(offs[-1] + sz)
    g_out["b_ada"] = jnp.stack([small_sum[offs[l]:offs[l + 1]] for l in range(depth)])
    for i, n in enumerate(SMALL):
        g_out[n] = small_sum[offs[depth + i]:offs[depth + i + 1]].reshape(w[n].shape)
    g_out["final_norm"] = small_sum[offs[depth + len(SMALL)]:offs[depth + len(SMALL) + 1]]
    loss = small_sum[offs[depth + len(SMALL) + 1]]

    c_act_t = _pad_to(c_act[:N_DEV].T, K_PAD, 1)
    g_ada = []
    for l in range(depth):
        dmod_l = small_all[:, 0, offs[l]:offs[l + 1]]
        dmod_cols = lax.dynamic_slice_in_dim(dmod_l, chip * ada_cols, ada_cols, axis=1)
        g_ada.append(_mm(c_act_t, _pad_to(dmod_cols, K_PAD, 0), name=f"ada_grad_l{l}"))
    g_out["w_ada"] = jnp.stack(g_ada)

    conv_g = jnp.stack([small[l]["conv_w"] for l in range(depth)])
    conv_all_g = _all_gather_small(conv_g.reshape(1, -1), "gather_conv_grads")
    conv_sum = _sum_devices(conv_all_g, "sum_conv_grads").reshape(conv_g.shape)
    n_cc = conv_w.shape[2]
    g_out["conv_w"] = lax.dynamic_slice_in_dim(conv_sum, chip * n_cc, n_cc, axis=2)

    mix0 = groups["l0_mix"][0]
    started = mix0[0](conv_sum.reshape(-1)[:LANES] + small_sum[:LANES])
    for key in ("l1_ffn", "l1_mix", "l0_ffn"):
        groups[key][0][2](None)
    reduced(FFN_GRADS)
    first_updates = ("w_ada", "b_ada", "final_norm", "conv_w") + SMALL + FFN_GRADS
    update(first_updates, jnp.zeros((8, LANES), F32) + started)
    corner = lambda a: a.reshape((1,) * (3 - a.ndim) + a.shape)[0, :1, :LANES]
    mix0[1](sum(corner(deltas[n]) for n in first_updates if w[n].shape[-1] >= LANES))
    mix0[2](None)
    reduced(MIXER_GRADS)
    update(MIXER_GRADS, jnp.zeros((8, LANES), F32))
    return (loss, dx[None], *[g_out[n] for n in WEIGHTS], *[deltas[n] for n in WEIGHTS],
            *[new_m[n] for n in WEIGHTS], *[new_v[n] for n in WEIGHTS])
```

```python
import functools
import math

import jax
import jax.numpy as jnp
from jax import lax
from jax.experimental import pallas as pl
from jax.experimental.pallas import tpu as pltpu

F32 = jnp.float32
BF16 = jnp.bfloat16
SDS = jax.ShapeDtypeStruct
MESH = pl.DeviceIdType.MESH
AXES = ("x", "y", "c")

EPS = 1e-6
HEADS = 8
D_NOPE = 128
D_ROPE = 64
D_QK = 256
D_V = 128
Q_LORA = 512
KV_LORA = 512
CHUNK = 64
CONV_WIDTH = 4
GDN_W = HEADS * D_V
MAIN_FIXED = 2 * Q_LORA + 4 * GDN_W
LANES = 128
VMEM_LIMIT = 56 * 1024 * 1024
ADAM_LR, ADAM_B1, ADAM_B2, ADAM_EPS, ADAM_WD, ADAM_STEP = 0.001, 0.9, 0.999, 1e-8, 0.01, 10


def _pick(n, cands):
    for cand in cands:
        if n % cand == 0:
            return cand
    return n


def _cp(sem):
    return pltpu.CompilerParams(dimension_semantics=sem, vmem_limit_bytes=VMEM_LIMIT)


def _row_tile(t):
    return _pick(t, (256, 128, 64, 32, 16, 8))


def _dot(a, b):
    return jnp.dot(a.astype(BF16), b.astype(BF16), preferred_element_type=F32)


def _dot_nt(a, b):
    return lax.dot_general(a.astype(BF16), b.astype(BF16), (((1,), (1,)), ((), ())), preferred_element_type=F32)


def _dot_tn(a, b):
    return lax.dot_general(a.astype(BF16), b.astype(BF16), (((0,), (0,)), ((), ())), preferred_element_type=F32)


def _bdot(a, b, exact=False):
    dims = (((2,), (1,)), ((0,), (0,)))
    if exact:
        ah, bh = a.astype(BF16), b.astype(BF16)
        al, bl = (a - ah.astype(F32)).astype(BF16), (b - bh.astype(F32)).astype(BF16)
        return (lax.dot_general(ah, bh, dims, preferred_element_type=F32)
                + lax.dot_general(ah, bl, dims, preferred_element_type=F32)
                + lax.dot_general(al, bh, dims, preferred_element_type=F32))
    return lax.dot_general(a.astype(BF16), b.astype(BF16), dims, preferred_element_type=F32)


def _bdot_nt(a, b):
    return lax.dot_general(a.astype(BF16), b.astype(BF16), (((2,), (2,)), ((0,), (0,))), preferred_element_type=F32)


def _sigmoid(x):
    return 1.0 / (1.0 + jnp.exp(-x))


def _mm(a, b, *, tb=False, out_dtype=F32, acc_in=None, b_layer=None, name):
    m, k = a.shape
    if b_layer is not None:
        assert not tb and b.shape[1] == k and k <= 2048
        n = b.shape[2]
    else:
        n = b.shape[0] if tb else b.shape[1]
        assert (b.shape[1] if tb else b.shape[0]) == k
    tm = _pick(m, (1024, 512, 256, 128))
    tn = _pick(n, (1024, 512, 256, 128))
    tk = k if k <= 2048 else _pick(k, (1024, 512, 256, 128))
    nk = k // tk
    has_acc = acc_in is not None

    def body_one_step(*refs):
        a_ref, b_ref = refs[:2]
        o_ref = refs[-1]
        acc = _dot_nt(a_ref[...], b_ref[...]) if tb else _dot(a_ref[...], b_ref[...])
        if has_acc:
            acc = acc + refs[2][...].astype(F32)
        o_ref[...] = acc.astype(out_dtype)

    if nk == 1:
        if b_layer is not None:
            b_spec = pl.BlockSpec((None, k, tn), lambda i, j: (b_layer, 0, j))
        else:
            b_spec = pl.BlockSpec((tn, k), lambda i, j: (j, 0)) if tb else pl.BlockSpec((k, tn), lambda i, j: (0, j))
        in_specs = [pl.BlockSpec((tm, k), lambda i, j: (i, 0)), b_spec]
        args = [a, b]
        if has_acc:
            in_specs.append(pl.BlockSpec((tm, tn), lambda i, j: (i, j)))
            args.append(acc_in)
        return pl.pallas_call(
            body_one_step, grid=(m // tm, n // tn), in_specs=in_specs, out_specs=pl.BlockSpec((tm, tn), lambda i, j: (i, j)),
            out_shape=SDS((m, n), out_dtype), name=name, compiler_params=_cp(("parallel", "parallel")))(*args)

    def body(*refs):
        if has_acc:
            a_ref, b_ref, c_ref, o_ref, acc = refs
        else:
            a_ref, b_ref, o_ref, acc = refs
        kk = pl.program_id(2)

        @pl.when(kk == 0)
        def _():
            if has_acc:
                acc[...] = c_ref[...].astype(F32)
            else:
                acc[...] = jnp.zeros_like(acc)

        if tb:
            acc[...] += _dot_nt(a_ref[...], b_ref[...])
        else:
            acc[...] += _dot(a_ref[...], b_ref[...])

        @pl.when(kk == nk - 1)
        def _():
            o_ref[...] = acc[...].astype(out_dtype)

    in_specs = [pl.BlockSpec((tm, tk), lambda i, j, kk: (i, kk)),
                pl.BlockSpec((tn, tk), lambda i, j, kk: (j, kk)) if tb
                else pl.BlockSpec((tk, tn), lambda i, j, kk: (kk, j))]
    args = [a, b]
    if has_acc:
        in_specs.append(pl.BlockSpec((tm, tn), lambda i, j, kk: (i, j)))
        args.append(acc_in)
    return pl.pallas_call(
        body, grid=(m // tm, n // tn, nk), in_specs=in_specs,
        out_specs=pl.BlockSpec((tm, tn), lambda i, j, kk: (i, j)),
        out_shape=SDS((m, n), out_dtype), scratch_shapes=[pltpu.VMEM((tm, tn), F32)],
        name=name, compiler_params=_cp(("parallel", "parallel", "arbitrary")))(*args)


def _norm_fwd(x, col, width, w, sc, sh, name):
    t = x.shape[0]
    tm = _row_tile(t)

    def body(x_ref, w_ref, sc_ref, sh_ref, o_ref):
        xv = x_ref[...]
        r = lax.rsqrt(jnp.mean(xv * xv, axis=-1, keepdims=True) + EPS)
        n = xv * r * w_ref[...]
        o_ref[...] = (n * (1.0 + sc_ref[...]) + sh_ref[...]).astype(o_ref.dtype)

    row = pl.BlockSpec((1, width), lambda i: (0, 0))
    return pl.pallas_call(
        body, grid=(t // tm,), in_specs=[pl.BlockSpec((tm, width), lambda i: (i, col)), row, row, row],
        out_specs=pl.BlockSpec((tm, width), lambda i: (i, 0)), out_shape=SDS((t, width), BF16),
        name=name, compiler_params=_cp(("parallel",)))(x, w, sc, sh)


def _norm_bwd(x, col, width, w, sc, dh, dres, out_dtype, name):
    t = x.shape[0]
    tm = _row_tile(t)
    has_res = dres is not None

    def body(*refs):
        if has_res:
            x_ref, w_ref, sc_ref, dh_ref, dres_ref, dx_ref, dw_ref, dsc_ref, dsh_ref = refs
        else:
            x_ref, w_ref, sc_ref, dh_ref, dx_ref, dw_ref, dsc_ref, dsh_ref = refs

        @pl.when(pl.program_id(0) == 0)
        def _():
            dw_ref[...] = jnp.zeros_like(dw_ref)
            dsc_ref[...] = jnp.zeros_like(dsc_ref)
            dsh_ref[...] = jnp.zeros_like(dsh_ref)

        xv = x_ref[...]
        dhv = dh_ref[...].astype(F32)
        wv = w_ref[...]
        r = lax.rsqrt(jnp.mean(xv * xv, axis=-1, keepdims=True) + EPS)
        xh = xv * r
        n = xh * wv
        dsh_ref[...] += jnp.sum(dhv, axis=0, keepdims=True)
        dsc_ref[...] += jnp.sum(dhv * n, axis=0, keepdims=True)
        dn = dhv * (1.0 + sc_ref[...])
        dw_ref[...] += jnp.sum(dn * xh, axis=0, keepdims=True)
        dxh = dn * wv
        dx = r * (dxh - xh * jnp.mean(dxh * xh, axis=-1, keepdims=True))
        if has_res:
            dx = dx + dres_ref[...]
        dx_ref[...] = dx.astype(out_dtype)

    row = pl.BlockSpec((1, width), lambda i: (0, 0))
    blk = pl.BlockSpec((tm, width), lambda i: (i, 0))
    in_specs = [pl.BlockSpec((tm, width), lambda i: (i, col)), row, row, blk]
    args = [x, w, sc, dh]
    if has_res:
        in_specs.append(blk)
        args.append(dres)
    return pl.pallas_call(
        body, grid=(t // tm,), in_specs=in_specs, out_specs=[blk, row, row, row],
        out_shape=[SDS((t, width), out_dtype), SDS((1, width), F32), SDS((1, width), F32), SDS((1, width), F32)],
        name=name, compiler_params=_cp(("arbitrary",)))(*args)


def _rope128(x, tc, ts1, ts2):
    return x * tc + pltpu.roll(x, 96, 1) * ts1 + pltpu.roll(x, 32, 1) * ts2


def _rope128_t(d, tc, ts1, ts2):
    return d * tc + pltpu.roll(d * ts1, 32, 1) + pltpu.roll(d * ts2, 96, 1)


def _rope_q(q_raw, tc, ts1, ts2, transpose, out_dtype, name):
    t = q_raw.shape[0]
    tm = _row_tile(t)

    def body(q_ref, tc_ref, s1_ref, s2_ref, o_ref):
        fn = _rope128_t if transpose else _rope128
        for h in range(HEADS):
            base = h * D_QK
            o_ref[:, base:base + LANES] = q_ref[:, base:base + LANES].astype(out_dtype)
            x = q_ref[:, base + LANES:base + D_QK].astype(F32)
            o_ref[:, base + LANES:base + D_QK] = fn(x, tc_ref[...], s1_ref[...], s2_ref[...]).astype(out_dtype)

    blk = pl.BlockSpec((tm, HEADS * D_QK), lambda i: (i, 0))
    tab = pl.BlockSpec((tm, LANES), lambda i: (i, 0))
    return pl.pallas_call(
        body, grid=(t // tm,), in_specs=[blk, tab, tab, tab], out_specs=blk,
        out_shape=SDS((t, HEADS * D_QK), out_dtype), name=name, compiler_params=_cp(("parallel",)))(q_raw, tc, ts1, ts2)


def _k_assemble(kv_raw, p_small, tc, ts1, ts2, name):
    t = kv_raw.shape[0]
    tm = _row_tile(t)

    def body(kn_ref, ps_ref, tc_ref, s1_ref, s2_ref, o_ref):
        kpe = _rope128(ps_ref[...], tc_ref[...], s1_ref[...], s2_ref[...]).astype(BF16)
        for h in range(HEADS):
            o_ref[:, h * D_QK:h * D_QK + LANES] = kn_ref[:, h * LANES:(h + 1) * LANES].astype(BF16)
            o_ref[:, h * D_QK + LANES:(h + 1) * D_QK] = kpe

    tab = pl.BlockSpec((tm, LANES), lambda i: (i, 0))
    return pl.pallas_call(
        body, grid=(t // tm,),
        in_specs=[pl.BlockSpec((tm, HEADS * LANES), lambda i: (i, 0)), tab, tab, tab, tab],
        out_specs=pl.BlockSpec((tm, HEADS * D_QK), lambda i: (i, 0)),
        out_shape=SDS((t, HEADS * D_QK), BF16), name=name, compiler_params=_cp(("parallel",)))(kv_raw, p_small, tc, ts1, ts2)


def _k_assemble_bwd(dk, dv, tc, ts1, ts2, name):
    t = dk.shape[0]
    tm = _row_tile(t)

    def body(dk_ref, dv_ref, tc_ref, s1_ref, s2_ref, o_ref, pe_ref):
        acc = jnp.zeros((tm, LANES), F32)
        for h in range(HEADS):
            o_ref[:, h * LANES:(h + 1) * LANES] = dk_ref[:, h * D_QK:h * D_QK + LANES].astype(BF16)
            acc = acc + dk_ref[:, h * D_QK + LANES:(h + 1) * D_QK].astype(F32)
        o_ref[:, HEADS * LANES:] = dv_ref[...].astype(BF16)
        pe_ref[...] = _rope128_t(acc, tc_ref[...], s1_ref[...], s2_ref[...])

    tab = pl.BlockSpec((tm, LANES), lambda i: (i, 0))
    return pl.pallas_call(
        body, grid=(t // tm,),
        in_specs=[pl.BlockSpec((tm, HEADS * D_QK), lambda i: (i, 0)), pl.BlockSpec((tm, HEADS * LANES), lambda i: (i, 0)),
                  tab, tab, tab],
        out_specs=[pl.BlockSpec((tm, 2 * HEADS * LANES), lambda i: (i, 0)), tab],
        out_shape=[SDS((t, 2 * HEADS * LANES), BF16), SDS((t, LANES), F32)],
        name=name, compiler_params=_cp(("parallel",)))(dk, dv, tc, ts1, ts2)


def _attn_tile(t):
    return _pick(t, (256, 128, 64))


def _attn_fwd(q, k, v, v_off, name):
    t = q.shape[0]
    tq = _attn_tile(t)
    scale = (D_NOPE + D_ROPE) ** -0.5

    def body(q_ref, k_ref, v_ref, o_ref, lse_ref):
        for i in range(t // tq):
            n_k = (i + 1) * tq
            s = _dot_nt(q_ref[i * tq:(i + 1) * tq, :], k_ref[0:n_k, :]) * scale
            row = lax.broadcasted_iota(jnp.int32, (tq, n_k), 0) + i * tq
            colv = lax.broadcasted_iota(jnp.int32, (tq, n_k), 1)
            s = jnp.where(colv <= row, s, -jnp.inf)
            m = jnp.max(s, axis=-1, keepdims=True)
            p = jnp.exp(s - m)
            l = jnp.sum(p, axis=-1, keepdims=True)
            o = _dot(p, v_ref[0:n_k, :]) / l
            o_ref[i * tq:(i + 1) * tq, :] = o.astype(BF16)
            lse_ref[0, i * tq:(i + 1) * tq, :] = m + jnp.log(l)

    return pl.pallas_call(
        body, grid=(HEADS,),
        in_specs=[pl.BlockSpec((t, D_QK), lambda h: (0, h)), pl.BlockSpec((t, D_QK), lambda h: (0, h)),
                  pl.BlockSpec((t, D_V), lambda h: (0, v_off + h))],
        out_specs=[pl.BlockSpec((t, D_V), lambda h: (0, h)), pl.BlockSpec((1, t, 1), lambda h: (h, 0, 0))],
        out_shape=[SDS((t, HEADS * D_V), BF16), SDS((HEADS, t, 1), F32)],
        name=name, compiler_params=_cp(("parallel",)))(q, k, v)


def _attn_bwd(q, k, v, v_off, o, lse, do, name):
    t = q.shape[0]
    tq = _attn_tile(t)
    scale = (D_NOPE + D_ROPE) ** -0.5

    def body(q_ref, k_ref, v_ref, o_ref, lse_ref, do_ref, dq_ref, dk_ref, dv_ref):
        dk_ref[...] = jnp.zeros_like(dk_ref)
        dv_ref[...] = jnp.zeros_like(dv_ref)
        for i in range(t // tq):
            n_k = (i + 1) * tq
            rows = slice(i * tq, (i + 1) * tq)
            qi = q_ref[rows, :]
            doi = do_ref[rows, :].astype(F32)
            s = _dot_nt(qi, k_ref[0:n_k, :]) * scale
            row = lax.broadcasted_iota(jnp.int32, (tq, n_k), 0) + i * tq
            colv = lax.broadcasted_iota(jnp.int32, (tq, n_k), 1)
            p = jnp.where(colv <= row, jnp.exp(s - lse_ref[0, rows, :]), 0.0)
            dp = _dot_nt(doi, v_ref[0:n_k, :])
            delta = jnp.sum(doi * o_ref[rows, :].astype(F32), axis=-1, keepdims=True)
            ds = p * (dp - delta) * scale
            dq_ref[rows, :] = _dot(ds, k_ref[0:n_k, :])
            dk_ref[0:n_k, :] += _dot_tn(ds, qi)
            dv_ref[0:n_k, :] += _dot_tn(p, doi)

    qk_spec = pl.BlockSpec((t, D_QK), lambda h: (0, h))
    v_spec = pl.BlockSpec((t, D_V), lambda h: (0, h))
    return pl.pallas_call(
        body, grid=(HEADS,),
        in_specs=[qk_spec, qk_spec, pl.BlockSpec((t, D_V), lambda h: (0, v_off + h)), v_spec,
                  pl.BlockSpec((1, t, 1), lambda h: (h, 0, 0)), v_spec],
        out_specs=[qk_spec, qk_spec, v_spec],
        out_shape=[SDS((t, HEADS * D_QK), F32), SDS((t, HEADS * D_QK), F32), SDS((t, HEADS * D_V), F32)],
        name=name, compiler_params=_cp(("parallel",)))(q, k, v, o, lse, do)


CONV_COLS = 256


def _conv_pre(u, w_ref, rowi):
    acc = u * w_ref[CONV_WIDTH - 1:CONV_WIDTH, :]
    for sft in range(1, CONV_WIDTH):
        shifted = jnp.where(rowi >= sft, pltpu.roll(u, sft, 0), 0.0)
        acc = acc + shifted * w_ref[CONV_WIDTH - 1 - sft:CONV_WIDTH - sft, :]
    return acc


def _conv_fwd(p_main, conv_w, name):
    t = p_main.shape[0]
    off = 2 * Q_LORA // CONV_COLS

    def body(u_ref, w_ref, y_ref):
        u = u_ref[...]
        rowi = lax.broadcasted_iota(jnp.int32, u.shape, 0)
        pre = _conv_pre(u, w_ref, rowi)
        y_ref[...] = pre * _sigmoid(pre)

    return pl.pallas_call(
        body, grid=(3 * GDN_W // CONV_COLS,),
        in_specs=[pl.BlockSpec((t, CONV_COLS), lambda j: (0, off + j)), pl.BlockSpec((CONV_WIDTH, CONV_COLS), lambda j: (0, j))],
        out_specs=pl.BlockSpec((t, CONV_COLS), lambda j: (0, j)), out_shape=SDS((t, 3 * GDN_W), F32),
        name=name, compiler_params=_cp(("parallel",)))(p_main, conv_w)


def _conv_bwd(p_main, conv_w, dyc, name):
    t = p_main.shape[0]
    off = 2 * Q_LORA // CONV_COLS

    def body(u_ref, w_ref, dy_ref, du_ref, dw_ref):
        u = u_ref[...]
        rowi = lax.broadcasted_iota(jnp.int32, u.shape, 0)
        pre = _conv_pre(u, w_ref, rowi)
        sg = _sigmoid(pre)
        dpre = dy_ref[...] * sg * (1.0 + pre * (1.0 - sg))
        du = dpre * w_ref[CONV_WIDTH - 1:CONV_WIDTH, :]
        dw_ref[CONV_WIDTH - 1:CONV_WIDTH, :] = jnp.sum(dpre * u, axis=0, keepdims=True)
        for sft in range(1, CONV_WIDTH):
            back = jnp.where(rowi < t - sft, pltpu.roll(dpre, t - sft, 0), 0.0)
            du = du + back * w_ref[CONV_WIDTH - 1 - sft:CONV_WIDTH - sft, :]
            shifted = jnp.where(rowi >= sft, pltpu.roll(u, sft, 0), 0.0)
            dw_ref[CONV_WIDTH - 1 - sft:CONV_WIDTH - sft, :] = jnp.sum(dpre * shifted, axis=0, keepdims=True)
        du_ref[...] = du.astype(BF16)

    blk = pl.BlockSpec((t, CONV_COLS), lambda j: (0, j))
    wblk = pl.BlockSpec((CONV_WIDTH, CONV_COLS), lambda j: (0, j))
    return pl.pallas_call(
        body, grid=(3 * GDN_W // CONV_COLS,),
        in_specs=[pl.BlockSpec((t, CONV_COLS), lambda j: (0, off + j)), wblk, blk],
        out_specs=[blk, wblk], out_shape=[SDS((t, 3 * GDN_W), BF16), SDS((CONV_WIDTH, 3 * GDN_W), F32)],
        name=name, compiler_params=_cp(("parallel",)))(p_main, conv_w, dyc)


B_LO, A_LO, A_HI = D_ROPE, D_ROPE + HEADS, D_ROPE + 2 * HEADS


def _softplus(z):
    e = jnp.exp(-jnp.abs(z))
    log1p = jnp.where(e < 0.01, e * (1.0 - e * (0.5 - e * (1.0 / 3.0))), jnp.log(1.0 + e))
    return jnp.maximum(z, 0.0) + log1p


def _gdn_gates(p_small, a_row, dt_row, name):
    t = p_small.shape[0]

    def body(ps_ref, a_ref, dt_ref, g_ref, gc_ref):
        x = ps_ref[...]
        lane = lax.broadcasted_iota(jnp.int32, x.shape, 1)
        is_g = (lane >= A_LO) & (lane < A_HI)
        g = jnp.where(is_g, -jnp.exp(a_ref[...]) * _softplus(x + dt_ref[...]), 0.0)
        g_ref[...] = jnp.where(is_g, g, _sigmoid(x))
        pos = lax.broadcasted_iota(jnp.int32, x.shape, 0) % CHUNK
        acc = g
        sft = 1
        while sft < CHUNK:
            acc = acc + jnp.where(pos >= sft, pltpu.roll(acc, sft, 0), 0.0)
            sft *= 2
        gc_ref[...] = acc

    full = pl.BlockSpec((t, LANES), lambda i: (0, 0))
    row = pl.BlockSpec((1, LANES), lambda i: (0, 0))
    return pl.pallas_call(
        body, grid=(1,), in_specs=[full, row, row], out_specs=[full, full],
        out_shape=[SDS((t, LANES), F32), SDS((t, LANES), F32)], name=name,
        compiler_params=_cp(("arbitrary",)))(p_small, a_row, dt_row)


def _gdn_gates_bwd(p_small, a_row, dt_row, gates, dgates, dkpe, name):
    t = p_small.shape[0]

    def body(ps_ref, a_ref, dt_ref, g_ref, db_ref, dkpe_ref, dp_ref, da_ref, ddt_ref):
        x = ps_ref[...]
        lane = lax.broadcasted_iota(jnp.int32, x.shape, 1)
        is_g = (lane >= A_LO) & (lane < A_HI)
        is_b = (lane >= B_LO) & (lane < A_LO)
        pos = lax.broadcasted_iota(jnp.int32, x.shape, 0) % CHUNK
        acc = jnp.where(is_g, db_ref[...], 0.0)
        sft = 1
        while sft < CHUNK:
            acc = acc + jnp.where(pos < CHUNK - sft, pltpu.roll(acc, t - sft, 0), 0.0)
            sft *= 2
        dg = acc
        gv = g_ref[...]
        dz = jnp.where(is_g, dg * (-jnp.exp(a_ref[...])) * _sigmoid(x + dt_ref[...]), 0.0)
        da_ref[...] = jnp.sum(jnp.where(is_g, dg * gv, 0.0), axis=0, keepdims=True)
        ddt_ref[...] = jnp.sum(dz, axis=0, keepdims=True)
        dlb = jnp.where(is_b, db_ref[...] * gv * (1.0 - gv), 0.0)
        dp_ref[...] = (jnp.where(lane < D_ROPE, dkpe_ref[...], 0.0) + dlb + dz).astype(BF16)

    full = pl.BlockSpec((t, LANES), lambda i: (0, 0))
    row = pl.BlockSpec((1, LANES), lambda i: (0, 0))
    return pl.pallas_call(
        body, grid=(1,), in_specs=[full, row, row, full, full, full], out_specs=[full, row, row],
        out_shape=[SDS((t, LANES), BF16), SDS((1, LANES), F32), SDS((1, LANES), F32)], name=name,
        compiler_params=_cp(("arbitrary",)))(p_small, a_row, dt_row, gates, dgates, dkpe)


def _tri_inv(l, eye):
    x = eye - l
    p = _bdot(l, l, exact=True)
    steps = int(math.log2(CHUNK)) - 1
    for s in range(steps):
        x = x + _bdot(x, p, exact=True)
        if s < steps - 1:
            p = _bdot(p, p, exact=True)
    return x


def _l2n(x3):
    r = lax.rsqrt(jnp.sum(x3 * x3, axis=-1, keepdims=True) + EPS)
    return x3 * r, r


def _head_col(a_ref, lane_lo, n):
    a = a_ref[...]
    lane = lax.broadcasted_iota(jnp.int32, a.shape, 1)
    col = jnp.sum(jnp.where(lane == lane_lo + pl.program_id(0), a, 0.0), axis=-1, keepdims=True)
    return col.reshape(n, CHUNK, 1)


def _gdn_common(q3, k3, v3, b, gc):
    n = q3.shape[0]
    ri = lax.broadcasted_iota(jnp.int32, (n, CHUNK, CHUNK), 1)
    ci = lax.broadcasted_iota(jnp.int32, (n, CHUNK, CHUNK), 2)
    lower, strict = ri >= ci, ri > ci
    eye = (ri == ci).astype(F32)
    gr = jnp.sum(gc * eye, axis=1, keepdims=True)
    qh, rq = _l2n(q3)
    qn = qh * (D_V ** -0.5)
    kn, rk = _l2n(k3)
    dec = jnp.where(lower, jnp.exp(jnp.where(lower, gc - gr, 0.0)), 0.0)
    kb = kn * b
    mm = _bdot_nt(kb, kn)
    tinv = _tri_inv(jnp.where(strict, mm * dec, 0.0), eye)
    gam = jnp.exp(gc)
    u = _bdot(tinv, v3 * b, exact=True)
    w = _bdot(tinv, kb * gam, exact=True)
    qk = _bdot_nt(qn, kn)
    aqk = jnp.where(lower, qk * dec, 0.0)
    gl = gc[:, CHUNK - 1:CHUNK, :]
    kdf = jnp.exp(gl - gc)
    return dict(ri=ri, ci=ci, lower=lower, strict=strict, eye=eye, qh=qh, rq=rq, qn=qn, kn=kn, rk=rk, dec=dec, kb=kb,
                mm=mm, gam=gam, u=u, w=w, qk=qk, aqk=aqk, gl=gl, kdf=kdf, kd=kn * kdf, gr=gr, tinv=tinv)


def _gdn_fwd(yc, p_main, gates, gcum, gn, name):
    t = yc.shape[0]
    n = t // CHUNK
    z_off = (2 * Q_LORA + 3 * GDN_W) // D_V

    def body(q_ref, k_ref, v_ref, z_ref, gt_ref, gcum_ref, gn_ref, o_ref, g_ref, s_ref, u_s, w_s, qg_s, kd_s, a_s, e_s):
        c = _gdn_common(q_ref[...].reshape(n, CHUNK, D_V), k_ref[...].reshape(n, CHUNK, D_V),
                        v_ref[...].reshape(n, CHUNK, D_V), _head_col(gt_ref, B_LO, n), _head_col(gcum_ref, A_LO, n))
        u_s[...] = c["u"]
        w_s[...] = c["w"]
        qg_s[...] = c["qn"] * c["gam"]
        kd_s[...] = c["kd"]
        a_s[...] = c["aqk"]
        e_s[...] = jnp.broadcast_to(jnp.exp(c["gl"]), (n, 1, D_V))

        def step(i, s):
            s_ref[0, i] = s
            v_new = u_s[i] - _dot(w_s[i], s)
            o = _dot(qg_s[i], s) + _dot(a_s[i], v_new)
            o_ref[pl.ds(pl.multiple_of(i * CHUNK, CHUNK), CHUNK), :] = o
            return s * e_s[i] + _dot_tn(kd_s[i], v_new)

        lax.fori_loop(0, n, step, jnp.zeros((D_V, D_V), F32))
        o = o_ref[...]
        zz = z_ref[...]
        on = o * lax.rsqrt(jnp.mean(o * o, axis=-1, keepdims=True) + EPS) * gn_ref[...]
        g_ref[...] = (on * zz * _sigmoid(zz)).astype(BF16)

    col = lambda off: pl.BlockSpec((t, D_V), lambda h: (0, off + h))
    lanes = pl.BlockSpec((t, LANES), lambda h: (0, 0))
    big = pltpu.VMEM((n, CHUNK, D_V), F32)
    return pl.pallas_call(
        body, grid=(HEADS,),
        in_specs=[col(0), col(HEADS), col(2 * HEADS), col(z_off), lanes, lanes, pl.BlockSpec((1, D_V), lambda h: (0, 0))],
        out_specs=[col(0), col(0), pl.BlockSpec((1, n, D_V, D_V), lambda h: (h, 0, 0, 0))],
        out_shape=[SDS((t, GDN_W), F32), SDS((t, GDN_W), BF16), SDS((HEADS, n, D_V, D_V), F32)],
        scratch_shapes=[big, big, big, big, pltpu.VMEM((n, CHUNK, CHUNK), F32), pltpu.VMEM((n, 1, D_V), F32)],
        name=name, compiler_params=_cp(("parallel",)))(yc, yc, yc, p_main, gates, gcum, gn)


def _gdn_bwd(yc, p_main, gates, gcum, gn, o_raw, states, dgated, name):
    t = yc.shape[0]
    n = t // CHUNK
    z_off = (2 * Q_LORA + 3 * GDN_W) // D_V

    def body(q_ref, k_ref, v_ref, z_ref, gt_ref, gcum_ref, gn_ref, o_ref, s_ref, dg_ref,
             dq_ref, dk_ref, dv_ref, dz_ref, dgt_ref, dgn_ref,
             u_s, w_s, qg_s, kd_s, at_s, e_s, do_s, du_s, dw_s, dqg_s, dkd_s, da_s, dat_s, dgs_s):
        @pl.when(pl.program_id(0) == 0)
        def _():
            dgn_ref[...] = jnp.zeros_like(dgn_ref)
            dgt_ref[...] = jnp.zeros_like(dgt_ref)

        o = o_ref[...]
        zz = z_ref[...]
        dgv = dg_ref[...]
        gnv = gn_ref[...]
        r = lax.rsqrt(jnp.mean(o * o, axis=-1, keepdims=True) + EPS)
        oh = o * r
        sg = _sigmoid(zz)
        don = dgv * zz * sg
        dz_ref[...] = (dgv * oh * gnv * sg * (1.0 + zz * (1.0 - sg))).astype(BF16)
        dgn_ref[...] += jnp.sum(don * oh, axis=0, keepdims=True)
        doh = don * gnv
        do_s[...] = (r * (doh - oh * jnp.mean(doh * oh, axis=-1, keepdims=True))).reshape(n, CHUNK, D_V)

        q3 = q_ref[...].reshape(n, CHUNK, D_V)
        k3 = k_ref[...].reshape(n, CHUNK, D_V)
        v3 = v_ref[...].reshape(n, CHUNK, D_V)
        b, gc = _head_col(gt_ref, B_LO, n), _head_col(gcum_ref, A_LO, n)
        c = _gdn_common(q3, k3, v3, b, gc)
        gr = c["gr"]
        ri, ci = c["ri"], c["ci"]
        upper, sup = ci >= ri, ci > ri
        dect = jnp.where(upper, jnp.exp(jnp.where(upper, gr - gc, 0.0)), 0.0)
        tinv_t = lax.dot_general(c["eye"], c["tinv"], (((2,), (2,)), ((0,), (0,))), precision=lax.Precision.HIGHEST,
                                 preferred_element_type=F32)
        u_s[...] = c["u"]
        w_s[...] = c["w"]
        qg_s[...] = c["qn"] * c["gam"]
        kd_s[...] = c["kd"]
        at_s[...] = jnp.where(upper, _bdot_nt(c["kn"], c["qn"]) * dect, 0.0)
        e_s[...] = jnp.broadcast_to(jnp.exp(c["gl"]), (n, 1, D_V))

        def step(j, ds):
            i = n - 1 - j
            s = s_ref[0, i]
            do_i = do_s[i]
            v_new = u_s[i] - _dot(w_s[i], s)
            dvn = _dot(at_s[i], do_i) + _dot(kd_s[i], ds)
            da_s[i] = _dot_nt(do_i, v_new)
            dat_s[i] = _dot_nt(v_new, do_i)
            dqg_s[i] = _dot_nt(do_i, s)
            dw_s[i] = -_dot_nt(dvn, s)
            dkd_s[i] = _dot_nt(v_new, ds)
            du_s[i] = dvn
            dgs_s[i] = jnp.broadcast_to(jnp.sum(jnp.sum(s * ds, axis=1, keepdims=True), axis=0, keepdims=True), (1, D_V))
            return _dot_tn(qg_s[i], do_i) + e_s[i] * ds - _dot_tn(w_s[i], dvn)

        lax.fori_loop(0, n, step, jnp.zeros((D_V, D_V), F32))

        du, dw, dqg, dkd = du_s[...], dw_s[...], dqg_s[...], dkd_s[...]
        lower, strict, dec = c["lower"], c["strict"], c["dec"]
        kn, kb, qn, gam, kdf = c["kn"], c["kb"], c["qn"], c["gam"], c["kdf"]
        drv = _bdot(tinv_t, du, exact=True)
        drk = _bdot(tinv_t, dw, exact=True)
        dl = jnp.where(strict, -(_bdot_nt(drv, c["u"]) + _bdot_nt(drk, c["w"])), 0.0)
        dlt = jnp.where(sup, -(_bdot_nt(c["u"], drv) + _bdot_nt(c["w"], drk)), 0.0)
        da = jnp.where(lower, da_s[...], 0.0)
        dat = jnp.where(upper, dat_s[...], 0.0)
        e = (dl * c["mm"] + da * c["qk"]) * dec
        col_sums = jnp.sum(e, axis=1, keepdims=True)
        dgc = jnp.sum(e, axis=2, keepdims=True) - jnp.sum(col_sums * c["eye"], axis=2, keepdims=True)
        dkb = _bdot(dl * dec, kn) + gam * drk
        dkn = _bdot(dlt * dect, kb) + _bdot(dat * dect, qn) + b * dkb + dkd * kdf
        dqn = _bdot(da * dec, kn) + gam * dqg
        dgam = jnp.sum(drk * kb, axis=-1, keepdims=True) + jnp.sum(dqg * qn, axis=-1, keepdims=True)
        dbeta = jnp.sum(dkb * kn, axis=-1, keepdims=True) + jnp.sum(drv * v3, axis=-1, keepdims=True)
        dv_ref[...] = (b * drv).reshape(t, D_V)
        ee = jnp.sum(dkd * kn, axis=-1, keepdims=True) * kdf
        dgc = dgc + dgam * gam - ee
        rowc = lax.broadcasted_iota(jnp.int32, (n, CHUNK, 1), 1)
        tail = jnp.sum(ee, axis=1, keepdims=True) + dgs_s[...][:, :, 0:1] * jnp.exp(c["gl"])
        dgc = dgc + jnp.where(rowc == CHUNK - 1, tail, 0.0)
        lane = lax.broadcasted_iota(jnp.int32, (t, LANES), 1)
        head = pl.program_id(0)
        dgt_ref[...] += (jnp.where(lane == B_LO + head, dbeta.reshape(t, 1), 0.0)
                         + jnp.where(lane == A_LO + head, dgc.reshape(t, 1), 0.0))
        sc = D_V ** -0.5
        qh, rq, rk = c["qh"], c["rq"], c["rk"]
        dq_ref[...] = (rq * (sc * dqn - qh * jnp.sum(sc * dqn * qh, axis=-1, keepdims=True))).reshape(t, D_V)
        dk_ref[...] = (rk * (dkn - kn * jnp.sum(dkn * kn, axis=-1, keepdims=True))).reshape(t, D_V)

    once = pl.Buffered(1)
    col = lambda off: pl.BlockSpec((t, D_V), lambda h: (0, off + h), pipeline_mode=once)
    out_col = pl.BlockSpec((t, D_V), lambda h: (0, h))
    lanes = pl.BlockSpec((t, LANES), lambda h: (0, 0))
    row = pl.BlockSpec((1, D_V), lambda h: (0, 0))
    big = pltpu.VMEM((n, CHUNK, D_V), F32)
    sq = pltpu.VMEM((n, CHUNK, CHUNK), F32)
    small = pltpu.VMEM((n, 1, D_V), F32)
    return pl.pallas_call(
        body, grid=(HEADS,),
        in_specs=[col(0), col(HEADS), col(2 * HEADS), col(z_off), lanes, lanes, row, col(0),
                  pl.BlockSpec((1, n, D_V, D_V), lambda h: (h, 0, 0, 0), pipeline_mode=once), col(0)],
        out_specs=[out_col, out_col, out_col, out_col, lanes, row],
        out_shape=[SDS((t, GDN_W), F32), SDS((t, GDN_W), F32), SDS((t, GDN_W), F32), SDS((t, GDN_W), BF16),
                   SDS((t, LANES), F32), SDS((1, D_V), F32)],
        scratch_shapes=[big, big, big, big, sq, small, big, big, big, big, big, sq, sq, small],
        name=name, compiler_params=_cp(("arbitrary",)))(yc, yc, yc, p_main, gates, gcum, gn, o_raw, states, dgated)


def _col_tile(d):
    return _pick(d, (512, 256, 128))


def _mix_fwd(y_a, y_b, p_main, name):
    t, d = y_a.shape
    tm, cw = _row_tile(t), _col_tile(d)
    off_a, off_b = MAIN_FIXED // cw, (MAIN_FIXED + d) // cw

    def body(ya_ref, yb_ref, ga_ref, gb_ref, u_ref):
        u_ref[...] = (_sigmoid(ga_ref[...]) * ya_ref[...] + _sigmoid(gb_ref[...]) * yb_ref[...]).astype(BF16)

    blk = pl.BlockSpec((tm, cw), lambda i, j: (i, j))
    return pl.pallas_call(
        body, grid=(t // tm, d // cw),
        in_specs=[blk, blk, pl.BlockSpec((tm, cw), lambda i, j: (i, off_a + j)), pl.BlockSpec((tm, cw), lambda i, j: (i, off_b + j))],
        out_specs=blk, out_shape=SDS((t, d), BF16), name=name,
        compiler_params=_cp(("parallel", "parallel")))(y_a, y_b, p_main, p_main)


def _mix_bwd(du, y_a, y_b, p_main, name):
    t, d = y_a.shape
    tm, cw = _row_tile(t), _col_tile(d)
    off_a, off_b = MAIN_FIXED // cw, (MAIN_FIXED + d) // cw
    nb = d // cw

    def body(du_ref, ya_ref, yb_ref, ga_ref, gb_ref, dya_ref, dyb_ref, dla_ref, dlb_ref):
        duv = du_ref[...]
        ga, gb = _sigmoid(ga_ref[...]), _sigmoid(gb_ref[...])
        dya_ref[...] = (duv * ga).astype(BF16)
        dyb_ref[...] = (duv * gb).astype(BF16)
        dla_ref[...] = (duv * ya_ref[...] * ga * (1.0 - ga)).astype(BF16)
        dlb_ref[...] = (duv * yb_ref[...] * gb * (1.0 - gb)).astype(BF16)

    blk = pl.BlockSpec((tm, cw), lambda i, j: (i, j))
    outs = pl.pallas_call(
        body, grid=(t // tm, nb),
        in_specs=[blk, blk, blk, pl.BlockSpec((tm, cw), lambda i, j: (i, off_a + j)),
                  pl.BlockSpec((tm, cw), lambda i, j: (i, off_b + j))],
        out_specs=[blk, blk, blk, blk],
        out_shape=[SDS((t, d), BF16), SDS((t, d), BF16), SDS((t, d), BF16), SDS((t, d), BF16)], name=name,
        compiler_params=_cp(("parallel", "parallel")))(du, y_a, y_b, p_main, p_main)
    return outs


def _gate_res(x, y, gt, name):
    t, d = x.shape
    tm = _row_tile(t)

    def body(x_ref, y_ref, g_ref, o_ref):
        o_ref[...] = x_ref[...] + g_ref[...] * y_ref[...]

    blk = pl.BlockSpec((tm, d), lambda i: (i, 0))
    return pl.pallas_call(
        body, grid=(t // tm,), in_specs=[blk, blk, pl.BlockSpec((1, d), lambda i: (0, 0))], out_specs=blk,
        out_shape=SDS((t, d), F32), name=name, compiler_params=_cp(("parallel",)))(x, y, gt)


def _gate_res_bwd(dx, y, gt, name):
    t, d = dx.shape
    tm = _row_tile(t)

    def body(dx_ref, y_ref, g_ref, dg_ref, dy_ref):
        @pl.when(pl.program_id(0) == 0)
        def _():
            dg_ref[...] = jnp.zeros_like(dg_ref)

        dxv = dx_ref[...]
        dg_ref[...] += jnp.sum(dxv * y_ref[...], axis=0, keepdims=True)
        dy_ref[...] = (dxv * g_ref[...]).astype(BF16)

    blk = pl.BlockSpec((tm, d), lambda i: (i, 0))
    row = pl.BlockSpec((1, d), lambda i: (0, 0))
    return pl.pallas_call(
        body, grid=(t // tm,), in_specs=[blk, blk, row], out_specs=[row, blk],
        out_shape=[SDS((1, d), F32), SDS((t, d), BF16)], name=name, compiler_params=_cp(("arbitrary",)))(dx, y, gt)


def _swiglu_fwd(gu, name):
    t, f2 = gu.shape
    f = f2 // 2
    tm, cw = _row_tile(t), _col_tile(f)
    nb = f // cw

    def body(g_ref, u_ref, o_ref):
        g = g_ref[...]
        o_ref[...] = (g * _sigmoid(g) * u_ref[...]).astype(BF16)

    return pl.pallas_call(
        body, grid=(t // tm, nb),
        in_specs=[pl.BlockSpec((tm, cw), lambda i, j: (i, j)), pl.BlockSpec((tm, cw), lambda i, j: (i, nb + j))],
        out_specs=pl.BlockSpec((tm, cw), lambda i, j: (i, j)), out_shape=SDS((t, f), BF16), name=name,
        compiler_params=_cp(("parallel", "parallel")))(gu, gu)


def _swiglu_bwd(gu, da, name):
    t, f2 = gu.shape
    f = f2 // 2
    tm, cw = _row_tile(t), _col_tile(f)
    nb = f // cw

    def body(g_ref, u_ref, da_ref, dg_ref, dup_ref):
        g = g_ref[...]
        dav = da_ref[...]
        sg = _sigmoid(g)
        dg_ref[...] = (dav * u_ref[...] * sg * (1.0 + g * (1.0 - sg))).astype(BF16)
        dup_ref[...] = (dav * g * sg).astype(BF16)

    blk = pl.BlockSpec((tm, cw), lambda i, j: (i, j))
    dg, dup = pl.pallas_call(
        body, grid=(t // tm, nb),
        in_specs=[blk, pl.BlockSpec((tm, cw), lambda i, j: (i, nb + j)), blk], out_specs=[blk, blk],
        out_shape=[SDS((t, f), BF16), SDS((t, f), BF16)], name=name,
        compiler_params=_cp(("parallel", "parallel")))(gu, gu, da)
    return dg, dup


def _loss_head(x, w, target, name):
    t, d = x.shape
    tm = _row_tile(t)

    def body(x_ref, w_ref, t_ref, l_ref, dx_ref, dw_ref):
        @pl.when(pl.program_id(0) == 0)
        def _():
            l_ref[...] = jnp.zeros_like(l_ref)
            dw_ref[...] = jnp.zeros_like(dw_ref)

        xv = x_ref[...]
        wv = w_ref[...]
        r = lax.rsqrt(jnp.mean(xv * xv, axis=-1, keepdims=True) + EPS)
        xh = xv * r
        err = xh * wv - t_ref[...]
        per_tok = jnp.mean(err * err, axis=-1, keepdims=True)
        l_ref[...] += 0.5 * jnp.sum(per_tok, axis=0, keepdims=True)
        dy = err * (1.0 / d)
        dw_ref[...] += jnp.sum(dy * xh, axis=0, keepdims=True)
        dxh = dy * wv
        dx_ref[...] = r * (dxh - xh * jnp.mean(dxh * xh, axis=-1, keepdims=True))

    blk = pl.BlockSpec((tm, d), lambda i: (i, 0))
    row = pl.BlockSpec((1, d), lambda i: (0, 0))
    return pl.pallas_call(
        body, grid=(t // tm,), in_specs=[blk, row, blk],
        out_specs=[pl.BlockSpec((1, LANES), lambda i: (0, 0)), blk, row],
        out_shape=[SDS((1, LANES), F32), SDS((t, d), F32), SDS((1, d), F32)], name=name,
        compiler_params=_cp(("arbitrary",)))(x, w, target)


def _adamw(w, g, m, v, tie, name):
    shape = w.shape
    per_layer = isinstance(g, (list, tuple))
    n_layers = shape[0] if (w.ndim == 3 and shape[1] % 8 == 0) else 1
    cols = shape[-1]
    rows = w.size // cols // n_layers
    w, m, v = (a.reshape(n_layers * rows, cols) for a in (w, m, v))
    if not per_layer:
        g = g.reshape(n_layers * rows, cols)
    lanes_padded = -(-cols // LANES) * LANES
    budget_rows = max(8, (24 * 1024 * 1024) // (lanes_padded * 4 * 18))
    tr = rows
    if rows > budget_rows:
        tr = _pick(rows, tuple(c for c in (1024, 512, 256, 128, 64, 32, 16, 8) if c <= budget_rows))
    nrb = rows // tr
    c1 = 1.0 / (1.0 - ADAM_B1 ** ADAM_STEP)
    c2 = 1.0 / (1.0 - ADAM_B2 ** ADAM_STEP)
    n_g = len(g) if per_layer else 1

    def body(*refs):
        w_ref, m_ref, v_ref = refs[:3]
        g_refs = refs[3:3 + n_g]
        outs = refs[4 + n_g:]
        gv = g_refs[0][...]
        for l in range(1, n_g):
            gv = jnp.where(pl.program_id(0) == l, g_refs[l][...], gv)
        mn = ADAM_B1 * m_ref[...] + (1.0 - ADAM_B1) * gv
        vn = ADAM_B2 * v_ref[...] + (1.0 - ADAM_B2) * (gv * gv)
        outs[0][...] = -ADAM_LR * ((mn * c1) / (jnp.sqrt(vn * c2) + ADAM_EPS) + ADAM_WD * w_ref[...])
        outs[1][...] = mn
        outs[2][...] = vn
        if per_layer:
            outs[3][...] = gv

    blk = pl.BlockSpec((tr, cols), lambda l, i: (l * nrb + i, 0))
    g_specs = [pl.BlockSpec((tr, cols), lambda l, i: (i, 0))] * n_g if per_layer else [blk]
    n_out = 4 if per_layer else 3
    outs = pl.pallas_call(
        body, grid=(n_layers, nrb), in_specs=[blk, blk, blk] + g_specs + [pl.BlockSpec((8, LANES), lambda l, i: (0, 0))],
        out_specs=[blk] * n_out, out_shape=[SDS(w.shape, F32)] * n_out, name=name,
        compiler_params=_cp(("parallel", "parallel")))(w, m, v, *(g if per_layer else [g]), tie)
    g_out = outs[3] if per_layer else g
    return (g_out.reshape(shape),) + tuple(o.reshape(shape) for o in outs[:3])


KPE_LO = 2 * Q_LORA
QKVZ_LO = KPE_LO + D_ROPE
BA_LO = QKVZ_LO + 4 * GDN_W
GATE_LO = BA_LO + 2 * HEADS


def _lay_w_in(w_in):
    d = w_in.shape[0]
    main = jnp.concatenate([w_in[:, :KPE_LO], w_in[:, QKVZ_LO:BA_LO], w_in[:, GATE_LO:]], axis=1)
    small = jnp.concatenate([w_in[:, KPE_LO:QKVZ_LO], w_in[:, BA_LO:GATE_LO],
                             jnp.zeros((d, LANES - D_ROPE - 2 * HEADS), w_in.dtype)], axis=1)
    return main, small


def _unlay_w_in(g_main, g_small):
    return jnp.concatenate([g_main[:, :KPE_LO], g_small[:, :D_ROPE], g_main[:, KPE_LO:KPE_LO + 4 * GDN_W],
                            g_small[:, D_ROPE:D_ROPE + 2 * HEADS], g_main[:, MAIN_FIXED:]], axis=1)


def _lay_w_uq(w_uq):
    r = w_uq.reshape(Q_LORA, HEADS, D_NOPE + D_ROPE)
    r = jnp.pad(r, ((0, 0), (0, 0), (0, D_QK - D_NOPE - D_ROPE)))
    return r.reshape(Q_LORA, HEADS * D_QK)


def _unlay_w_uq(g):
    rows = g.shape[0]
    return g.reshape(rows, HEADS, D_QK)[:, :, :D_NOPE + D_ROPE].reshape(rows, HEADS * (D_NOPE + D_ROPE))


def _lay_w_ukv(w_ukv):
    return w_ukv.reshape(KV_LORA, HEADS, 2, D_V).transpose(0, 2, 1, 3).reshape(KV_LORA, 2 * HEADS * D_V)


def _unlay_w_ukv(g):
    rows = g.shape[0]
    return g.reshape(rows, 2, HEADS, D_V).transpose(0, 2, 1, 3).reshape(rows, 2 * HEADS * D_V)


def _lane_row(vec, lo):
    return jnp.pad(vec.reshape(1, -1), ((0, 0), (lo, LANES - lo - vec.shape[0])))


def _rope_tables(positions):
    half = D_ROPE // 2
    inv_freq = 1.0 / (10000.0 ** (jnp.arange(0, D_ROPE, 2, dtype=F32) / D_ROPE))
    ang = positions.astype(F32)[:, None] * inv_freq
    cos, sin = jnp.cos(ang), jnp.sin(ang)
    t = positions.shape[0]
    zeros = lambda n: jnp.zeros((t, n), F32)
    tc = jnp.concatenate([cos, cos, zeros(LANES - D_ROPE)], axis=1)
    ts1 = jnp.concatenate([-sin, zeros(LANES - half)], axis=1)
    ts2 = jnp.concatenate([zeros(half), sin, zeros(LANES - D_ROPE)], axis=1)
    return tc, ts1, ts2


def _layer_fwd(x, mod, wt, tabs, tag, late_weights, after_gate_up=None):
    t, d = x.shape
    sh_a, sc_a, gt_a, sh_f, sc_f, gt_f = mod
    zero_l = jnp.zeros((1, Q_LORA), F32)
    s = dict(x=x)
    s["h1"] = _norm_fwd(x, 0, d, wt["norm_mix"], sc_a, sh_a, f"{tag}_norm_mix")
    s["p_main"] = _mm(s["h1"], wt["w_main"], name=f"{tag}_in_main")
    s["p_small"] = _mm(s["h1"], wt["w_small"], name=f"{tag}_in_small")
    s["cqn"] = _norm_fwd(s["p_main"], 0, Q_LORA, wt["q_a_norm"], zero_l, zero_l, f"{tag}_q_norm")
    s["ckvn"] = _norm_fwd(s["p_main"], 1, KV_LORA, wt["kv_a_norm"], zero_l, zero_l, f"{tag}_kv_norm")
    q_raw = _mm(s["cqn"], wt["w_uq"], name=f"{tag}_uq")
    s["kv_raw"] = _mm(s["ckvn"], wt["w_ukv"], name=f"{tag}_ukv")
    s["q_r"] = _rope_q(q_raw, *tabs, False, BF16, f"{tag}_rope_q")
    s["k_r"] = _k_assemble(s["kv_raw"], s["p_small"], *tabs, f"{tag}_k_asm")
    s["o"], s["lse"] = _attn_fwd(s["q_r"], s["k_r"], s["kv_raw"], HEADS, f"{tag}_attn")
    s["yc"] = _conv_fwd(s["p_main"], wt["conv_w"], f"{tag}_conv")
    s["gates"], s["gcum"] = _gdn_gates(s["p_small"], wt["a_row"], wt["dt_row"], f"{tag}_gates")
    s["o_raw"], s["gated"], s["states"] = _gdn_fwd(s["yc"], s["p_main"], s["gates"], s["gcum"], wt["gdn_norm"], f"{tag}_gdn")
    late, started = late_weights(s["gated"])
    wt = {**wt, **late}
    s["y_a"] = _mm(s["o"], wt["w_o_mla"], name=f"{tag}_o_mla")
    s["y_b"] = _mm(s["gated"], wt["w_o_gdn"], name=f"{tag}_o_gdn")
    s["u"] = _mix_fwd(s["y_a"], s["y_b"], s["p_main"], f"{tag}_mix")
    s["mixo"] = _mm(s["u"], wt["w_o"], name=f"{tag}_o")
    s["x2"] = _gate_res(x, s["mixo"], gt_a, f"{tag}_res_a")
    s["h2"] = _norm_fwd(s["x2"], 0, d, wt["norm_ffn"] + started, sc_f, sh_f, f"{tag}_norm_ffn")
    s["gu"] = _mm(s["h2"], wt["w_gate_up"], name=f"{tag}_gate_up")
    if after_gate_up is not None:
        gt_f = gt_f + after_gate_up(s["gu"])
    s["a"] = _swiglu_fwd(s["gu"], f"{tag}_swiglu")
    s["f"] = _mm(s["a"], wt["w_down"], name=f"{tag}_down")
    return _gate_res(s["x2"], s["f"], gt_f, f"{tag}_res_f"), s, wt


def _layer_bwd(dx3, s, mod, wt, tabs, tag, after_ffn=None, after_gdn=None):
    x = s["x"]
    t, d = x.shape
    sh_a, sc_a, gt_a, sh_f, sc_f, gt_f = mod
    zero_l = jnp.zeros((1, Q_LORA), F32)
    g = {}
    dgt_f, df = _gate_res_bwd(dx3, s["f"], gt_f, f"{tag}_b_res_f")
    da = _mm(df, wt["w_down"], tb=True, name=f"{tag}_b_down_x")
    g["w_down"] = _mm(s["a"].T, df, out_dtype=BF16, name=f"{tag}_b_down_w")
    dgate, dup = _swiglu_bwd(s["gu"], da, f"{tag}_b_swiglu")
    dgu = jnp.concatenate([dgate, dup], axis=1)
    dh2 = _mm(dgu, wt["w_gate_up"], tb=True, name=f"{tag}_b_gate_up_x")
    g["w_gate_up"] = _mm(s["h2"].T, dgu, out_dtype=BF16, name=f"{tag}_b_gate_up_w")
    if after_ffn is not None:
        gt_a = gt_a + after_ffn(g)
    dx2, g["norm_ffn"], dsc_f, dsh_f = _norm_bwd(s["x2"], 0, d, wt["norm_ffn"], sc_f, dh2, dx3, F32, f"{tag}_b_norm_ffn")
    dgt_a, dmixo = _gate_res_bwd(dx2, s["mixo"], gt_a, f"{tag}_b_res_a")
    du = _mm(dmixo, wt["w_o"], tb=True, name=f"{tag}_b_o_x")
    g["w_o"] = _mm(s["u"].T, dmixo, out_dtype=BF16, name=f"{tag}_b_o_w")
    dy_a, dy_b, dl_a, dl_b = _mix_bwd(du, s["y_a"], s["y_b"], s["p_main"], f"{tag}_b_mix")
    dgated = _mm(dy_b, wt["w_o_gdn"], tb=True, name=f"{tag}_b_o_gdn_x")
    g["w_o_gdn"] = _mm(s["gated"].T, dy_b, out_dtype=BF16, name=f"{tag}_b_o_gdn_w")
    dq_c, dk_c, dv_c, dz, dgates, g["gdn_norm"] = _gdn_bwd(
        s["yc"], s["p_main"], s["gates"], s["gcum"], wt["gdn_norm"], s["o_raw"], s["states"], dgated, f"{tag}_b_gdn")
    du_conv, g["conv_w"] = _conv_bwd(s["p_main"], wt["conv_w"], jnp.concatenate([dq_c, dk_c, dv_c], axis=1), f"{tag}_b_conv")
    do = _mm(dy_a, wt["w_o_mla"], tb=True, name=f"{tag}_b_o_mla_x")
    g["w_o_mla"] = _mm(s["o"].T, dy_a, out_dtype=BF16, name=f"{tag}_b_o_mla_w")
    dq_r, dk_r, dv = _attn_bwd(s["q_r"], s["k_r"], s["kv_raw"], HEADS, s["o"], s["lse"], do, f"{tag}_b_attn")
    q_a_norm = wt["q_a_norm"]
    if after_gdn is not None:
        q_a_norm = q_a_norm + after_gdn(du_conv)
    dq_raw = _rope_q(dq_r, *tabs, True, BF16, f"{tag}_b_rope_q")
    dkv_raw, dkpe = _k_assemble_bwd(dk_r, dv, *tabs, f"{tag}_b_k_asm")
    dcqn = _mm(dq_raw, wt["w_uq"], tb=True, name=f"{tag}_b_uq_x")
    g["w_uq"] = _mm(s["cqn"].T, dq_raw, out_dtype=BF16, name=f"{tag}_b_uq_w")
    dckvn = _mm(dkv_raw, wt["w_ukv"], tb=True, name=f"{tag}_b_ukv_x")
    g["w_ukv"] = _mm(s["ckvn"].T, dkv_raw, out_dtype=BF16, name=f"{tag}_b_ukv_w")
    dc_q, g["q_a_norm"], _, _ = _norm_bwd(s["p_main"], 0, Q_LORA, q_a_norm, zero_l, dcqn, None, BF16, f"{tag}_b_q_norm")
    dc_kv, g["kv_a_norm"], _, _ = _norm_bwd(s["p_main"], 1, KV_LORA, wt["kv_a_norm"], zero_l, dckvn, None, BF16,
                                            f"{tag}_b_kv_norm")
    dp_small, g["a_row"], g["dt_row"] = _gdn_gates_bwd(
        s["p_small"], wt["a_row"], wt["dt_row"], s["gates"], dgates, dkpe, f"{tag}_b_gates")
    dp_main = jnp.concatenate([dc_q, dc_kv, du_conv, dz, dl_a, dl_b], axis=1)
    h1t = s["h1"].T
    dh1 = _mm(dp_small, wt["w_small"], tb=True, name=f"{tag}_b_in_small_x")
    dh1 = _mm(dp_main, wt["w_main"], tb=True, acc_in=dh1, name=f"{tag}_b_in_main_x")
    g["w_main"] = _mm(h1t, dp_main, out_dtype=BF16, name=f"{tag}_b_in_main_w")
    g["w_small"] = _mm(h1t, dp_small, out_dtype=BF16, name=f"{tag}_b_in_small_w")
    dx, g["norm_mix"], dsc_a, dsh_a = _norm_bwd(x, 0, d, wt["norm_mix"], sc_a, dh1, dx2, F32, f"{tag}_b_norm_mix")
    return dx, (dsh_a, dsc_a, dgt_a, dsh_f, dsc_f, dgt_f), g


def _layer_weights(big, full, l):
    w_main, w_small = _lay_w_in(big["w_in"])
    return dict(
        w_main=w_main, w_small=w_small, w_uq=_lay_w_uq(big["w_uq"]), w_ukv=_lay_w_ukv(big["w_ukv"]),
        conv_w=full["conv_w"][l],
        norm_mix=full["norm_mix"][l][None], norm_ffn=full["norm_ffn"][l][None],
        q_a_norm=full["q_a_norm"][l][None], kv_a_norm=full["kv_a_norm"][l][None], gdn_norm=full["gdn_norm"][l][None],
        a_row=_lane_row(full["A_log"][l], A_LO), dt_row=_lane_row(full["dt_bias"][l], A_LO))


def _small_grads_ref_layout(g):
    return dict(
        conv_w=g["conv_w"], norm_mix=g["norm_mix"][0], norm_ffn=g["norm_ffn"][0], q_a_norm=g["q_a_norm"][0],
        kv_a_norm=g["kv_a_norm"][0], gdn_norm=g["gdn_norm"][0], A_log=g["a_row"][0, A_LO:A_HI],
        dt_bias=g["dt_row"][0, A_LO:A_HI])


def _local_step(x, mods, target, final_norm, full, positions):
    tabs = _rope_tables(positions)
    depth = len(mods)
    wts, saved = [None] * depth, []
    h = x
    for l in range(depth):
        early = _layer_weights({n: full[n][l] for n in FIRST_NEEDED}, full, l)
        h, s, wts[l] = _layer_fwd(h, mods[l], early, tabs, f"l{l}", lambda _, l=l: ({n: full[n][l] for n in LATER_NEEDED}, 0.0))
        saved.append(s)
    loss, dh, dfn = _loss_head(h, final_norm[None], target, "loss_head")
    dmods, grads = [None] * depth, [None] * depth
    for l in reversed(range(depth)):
        dh, dmods[l], grads[l] = _layer_bwd(dh, saved[l], mods[l], wts[l], tabs, f"l{l}")
    return loss, dh, dmods, grads, dfn[0]


HBM_SPEC = pl.BlockSpec(memory_space=pl.ANY)
VMEM_SPEC = pl.BlockSpec(memory_space=pltpu.VMEM)
N_CHIPS = 4
N_DEV = 8


def _me():
    return lax.axis_index("x"), lax.axis_index("y"), lax.axis_index("c")


def _flip(pos, f):
    mx, my, mc = pos
    fx, fy, fc = (f >> 2) & 1, (f >> 1) & 1, f & 1
    return ((mx + fx) % 2, (my + fy) % 2, (mc + fc) % 2)


def _all_gather_small(x, name):
    r, n = x.shape

    def body(x_ref, out_ref, send_sems, recv_sems, local_sem):
        me = _me()
        row = lambda p: 4 * p[0] + 2 * p[1] + p[2]
        mine = pltpu.make_async_copy(x_ref, out_ref.at[row(me)], local_sem)
        mine.start()

        def copy(f, origin):
            return pltpu.make_async_remote_copy(
                src_ref=x_ref, dst_ref=out_ref.at[row(origin)], send_sem=send_sems.at[f - 1], recv_sem=recv_sems.at[f - 1],
                device_id=_flip(me, f), device_id_type=MESH)

        sends = [copy(f, me) for f in range(1, N_DEV)]
        for cp in sends:
            cp.start()
        for f in range(1, N_DEV):
            copy(f, _flip(me, f)).wait_recv()
        for cp in sends:
            cp.wait_send()
        mine.wait()

    return pl.pallas_call(
        body, out_shape=SDS((N_DEV, r, n), x.dtype), in_specs=[VMEM_SPEC], out_specs=VMEM_SPEC,
        scratch_shapes=[pltpu.SemaphoreType.DMA((N_DEV - 1,)), pltpu.SemaphoreType.DMA((N_DEV - 1,)), pltpu.SemaphoreType.DMA],
        name=name, compiler_params=pltpu.CompilerParams(vmem_limit_bytes=VMEM_LIMIT))(x)


CHIP_FLIPS = (2, 4, 6)
SIBLING = 1


def _chip_of(pos):
    return 2 * pos[0] + pos[1]


def _sum_chips(p, got, chip, half, name):
    _, kh, ns = p.shape
    tr = _pick(kh, (256, 128, 64, 32, 16))
    nrb = kh // tr

    def body(c_ref, h_ref, a_ref, b_ref, o_ref):
        acc = a_ref[0].astype(F32)
        for j in range(3):
            acc = acc + b_ref[j].astype(F32)
        o_ref[...] = acc

    grid_spec = pltpu.PrefetchScalarGridSpec(
        num_scalar_prefetch=2, grid=(nrb,),
        in_specs=[pl.BlockSpec((1, tr, ns), lambda i, c, h: (c[0], i, 0)),
                  pl.BlockSpec((3, tr, ns), lambda i, c, h: (0, i, 0))],
        out_specs=pl.BlockSpec((tr, ns), lambda i, c, h: (h[0] * nrb + i, 0)))
    return pl.pallas_call(body, grid_spec=grid_spec, out_shape=SDS((2 * kh, ns), F32), name=name,
                          compiler_params=_cp(("parallel",)))(chip, half, p, got)


SEM_SPEC = pl.BlockSpec(memory_space=pltpu.SEMAPHORE)
HBM_ONLY = pl.BlockSpec(memory_space=pltpu.HBM)
DATAFLOW = pltpu.SideEffectType.DATAFLOW_SIDE_EFFECTING


def _in_hbm(a):
    return pltpu.with_memory_space_constraint(a, pltpu.HBM)


def _copies_start(name, srcs, lands, plan, n_copies):
    ns, nl = len(srcs), len(lands)

    def body(*refs):
        src_refs, land_refs = refs[:ns], refs[ns:ns + nl]
        send_sems, recv_sems = refs[ns + nl], refs[ns + nl + 1]
        token = refs[-1]
        for i, (src, dst, peer) in enumerate(plan(_me(), src_refs, land_refs)):
            pltpu.make_async_remote_copy(src_ref=src, dst_ref=dst, send_sem=send_sems.at[i], recv_sem=recv_sems.at[i],
                                         device_id=peer, device_id_type=MESH).start()
        token[...] = jnp.zeros_like(token)

    outs = pl.pallas_call(
        body, name=name,
        out_shape=(pltpu.SemaphoreType.DMA((n_copies,)), pltpu.SemaphoreType.DMA((n_copies,)),
                   *[pltpu.HBM(l.shape, l.dtype) for l in lands], SDS((8, LANES), F32)),
        in_specs=[HBM_ONLY] * (ns + nl), out_specs=(SEM_SPEC, SEM_SPEC, *[HBM_ONLY] * nl, VMEM_SPEC),
        input_output_aliases={ns + i: 2 + i for i in range(nl)},
        compiler_params=pltpu.CompilerParams(has_side_effects=DATAFLOW),
    )(*[_in_hbm(s) for s in srcs], *[_in_hbm(l) for l in lands])
    return outs[0], outs[1], list(outs[2:2 + nl]), outs[-1]


def _copies_wait(name, srcs, lands, send_sems, recv_sems, plan, after):
    ns, nl = len(srcs), len(lands)

    def body(*refs):
        src_refs, land_refs = refs[:ns], refs[ns:ns + nl]
        send_ref, recv_ref = refs[ns + nl], refs[ns + nl + 1]
        for i, (src, dst, peer) in enumerate(plan(_me(), src_refs, land_refs)):
            cp = pltpu.make_async_remote_copy(src_ref=src, dst_ref=dst, send_sem=send_ref.at[i], recv_sem=recv_ref.at[i],
                                              device_id=peer, device_id_type=MESH)
            cp.wait_send()
            cp.wait_recv()

    outs = pl.pallas_call(
        body, name=name, out_shape=[pltpu.HBM(l.shape, l.dtype) for l in lands],
        in_specs=[HBM_ONLY] * (ns + nl) + [SEM_SPEC, SEM_SPEC, HBM_SPEC], out_specs=[HBM_ONLY] * nl,
        input_output_aliases={ns + i: i for i in range(nl)},
        compiler_params=pltpu.CompilerParams(has_side_effects=DATAFLOW),
    )(*[_in_hbm(s) for s in srcs], *lands, send_sems, recv_sems, after)
    return list(outs)


def _half(ref, rows, axis):
    idx = [slice(None)] * axis + [rows]
    return ref.at[tuple(idx)]


def _gather_plans(layer, halves):
    def ici(me, srcs, lands):
        out = []
        for a, kh in enumerate(halves):
            rows = pl.ds(pl.multiple_of(me[2] * kh, 16), kh)
            for k in range(3):
                out.append((srcs[a].at[layer, rows], lands[a].at[_chip_of(me), rows], _flip(me, CHIP_FLIPS[k])))
        return out

    def d2d(me, srcs, lands):
        out = []
        for a, kh in enumerate(halves):
            rows = pl.ds(pl.multiple_of(me[2] * kh, 16), kh)
            for k in range(3):
                slab = lands[a].at[_chip_of(_flip(me, CHIP_FLIPS[k])), rows]
                out.append((slab, slab, _flip(me, SIBLING)))
        return out

    return ici, d2d


def _to_sibling_plan(halves, axes):
    def plan(me, srcs, lands):
        out = []
        for a, (kh, axis) in enumerate(zip(halves, axes)):
            rows = pl.ds(pl.multiple_of((1 - me[2]) * kh, 16), kh)
            out.append((_half(srcs[a], rows, axis), lands[a], _flip(me, SIBLING)))
        return out

    return plan


def _to_chips_plan(n_arr):
    def plan(me, srcs, lands):
        out = []
        for a in range(n_arr):
            for k in range(3):
                peer = _flip(me, CHIP_FLIPS[k])
                out.append((srcs[a].at[_chip_of(peer)], lands[a].at[k], peer))
        return out

    return plan


def _swap_plan(halves):
    def plan(me, srcs, lands):
        out = []
        for a, kh in enumerate(halves):
            rows = pl.ds(pl.multiple_of(me[2] * kh, 16), kh)
            out.append((lands[a].at[rows], lands[a].at[rows], _flip(me, SIBLING)))
        return out

    return plan


def _add_half(g, got, half, col_shards, name):
    s, kh, n = got.shape
    tr = _pick(kh, (512, 256, 128, 64, 32, 16))
    nrb = kh // tr
    width = n // N_CHIPS if col_shards else n
    cw = _pick(width, (1024, 512, 256, 128))
    per = width // cw

    def body(h_ref, a_ref, b_ref, o_ref):
        o_ref[...] = (a_ref[...].astype(F32) + b_ref[...].astype(F32)).astype(o_ref.dtype)

    in_specs = [pl.BlockSpec((None, tr, cw), lambda j, i, c, h: (j, h[0] * nrb + i, c)),
                pl.BlockSpec((None, tr, cw), lambda j, i, c, h: (j, i, c))]
    if col_shards:
        assert s == 1
        out_spec = pl.BlockSpec((None, tr, cw), lambda j, i, c, h: (c // per, i, c % per))
        out_shape = SDS((N_CHIPS, kh, width), g.dtype)
    else:
        out_spec, out_shape = in_specs[1], SDS((s, kh, n), g.dtype)
    grid_spec = pltpu.PrefetchScalarGridSpec(num_scalar_prefetch=1, grid=(s, nrb, n // cw), in_specs=in_specs,
                                             out_specs=out_spec)
    return pl.pallas_call(body, grid_spec=grid_spec, out_shape=out_shape, name=name,
                          compiler_params=_cp(("parallel", "parallel", "parallel")))(half, g, got)


def _sum_devices(g, name):
    _, _, n = g.shape

    def body(g_ref, o_ref):
        acc = g_ref[0]
        for k in range(1, N_DEV):
            acc = acc + g_ref[k]
        o_ref[...] = acc

    return pl.pallas_call(body, out_shape=SDS((1, n), F32), in_specs=[VMEM_SPEC], out_specs=VMEM_SPEC, name=name)(g)


def _silu_rows(c, name):
    def body(c_ref, o_ref):
        v = c_ref[...]
        o_ref[...] = v * _sigmoid(v)

    return pl.pallas_call(body, out_shape=SDS(c.shape, F32), in_specs=[VMEM_SPEC], out_specs=VMEM_SPEC, name=name)(c)


BIG = (("w_in", 2), ("w_uq", 2), ("w_ukv", 2), ("w_o_mla", 2), ("w_o_gdn", 2), ("w_o", 1), ("w_gate_up", 2), ("w_down", 1))
KERNEL_BIG = ("w_main", "w_small", "w_uq", "w_ukv", "w_o_mla", "w_o_gdn", "w_o", "w_gate_up", "w_down")
COL_SHARDED_AS_IS = ("w_o_mla", "w_o_gdn", "w_gate_up")
ROW_SHARDED = ("w_o", "w_down")
FIRST_NEEDED = ("w_in", "w_uq", "w_ukv")
LATER_NEEDED = ("w_o_mla", "w_o_gdn", "w_o", "w_gate_up", "w_down")
FFN_GRADS = ("w_gate_up", "w_down")
MIXER_GRADS = ("w_in", "w_uq", "w_ukv", "w_o_mla", "w_o_gdn", "w_o")
MIXER_GRADS_KERNEL = ("w_main", "w_small", "w_uq", "w_ukv", "w_o_mla", "w_o_gdn", "w_o")
SMALL = ("norm_mix", "norm_ffn", "q_a_norm", "kv_a_norm", "A_log", "dt_bias", "gdn_norm")
WEIGHTS = ("w_ada", "b_ada", "norm_mix", "norm_ffn", "w_in", "q_a_norm", "kv_a_norm", "w_uq", "w_ukv", "w_o_mla", "conv_w",
           "A_log", "dt_bias", "gdn_norm", "w_o_gdn", "w_o", "w_gate_up", "w_down", "final_norm")
ADA_PAD = 16
K_PAD = 128


def _pad_to(a, n, axis):
    pad = [(0, 0)] * a.ndim
    pad[axis] = (0, n - a.shape[axis])
    return jnp.pad(a, pad)


def kernel(x, c, positions, w_ada, b_ada, norm_mix, norm_ffn, w_in, q_a_norm, kv_a_norm, w_uq, w_ukv, w_o_mla, conv_w, A_log, dt_bias, gdn_norm, w_o_gdn, w_o, w_gate_up, w_down, final_norm, loss_target, m_w_ada, m_b_ada, m_norm_mix, m_norm_ffn, m_w_in, m_q_a_norm, m_kv_a_norm, m_w_uq, m_w_ukv, m_w_o_mla, m_conv_w, m_A_log, m_dt_bias, m_gdn_norm, m_w_o_gdn, m_w_o, m_w_gate_up, m_w_down, m_final_norm, v_w_ada, v_b_ada, v_norm_mix, v_norm_ffn, v_w_in, v_q_a_norm, v_kv_a_norm, v_w_uq, v_w_ukv, v_w_o_mla, v_conv_w, v_A_log, v_dt_bias, v_gdn_norm, v_w_o_gdn, v_w_o, v_w_gate_up, v_w_down, v_final_norm):
    env = dict(locals())
    w = {n: env[n] for n in WEIGHTS}
    depth, d = norm_mix.shape
    t = x.shape[1]
    me = _me()
    chip = _chip_of(me)
    dev = 4 * me[0] + 2 * me[1] + me[2]
    ada_cols = w_ada.shape[2]

    half_idx = me[2].astype(jnp.int32).reshape(1)
    chip_idx = chip.astype(jnp.int32).reshape(1)
    w16 = {n: w[n].astype(BF16) for n, _ in BIG}
    shard_axis = dict(BIG)
    gather = {}

    def start_group(key, layer, names, dep):
        srcs = [w16[n] for n in names]
        plans = _gather_plans(layer, [a.shape[1] // 2 for a in srcs])
        landing = [lax.empty((N_CHIPS,) + a.shape[1:], BF16) for a in srcs]
        send_s, recv_s, landing, tok = _copies_start(f"gather_{key}_ici_start", srcs + [dep], landing, plans[0], 3 * len(names))
        gather[key] = dict(layer=layer, names=names, srcs=srcs, plans=plans, ici=(send_s, recv_s, landing), tok=tok)
        return tok[0, 0]

    def pass_to_sibling(key, after):
        st = gather[key]
        send_s, recv_s, landing = st["ici"]
        landing = _copies_wait(f"gather_{key}_ici_wait", st["srcs"] + [st["tok"]], landing, send_s, recv_s, st["plans"][0],
                               st["tok"] if after is None else after)
        st["d2d"] = _copies_start(f"gather_{key}_d2d_start", [], landing, st["plans"][1], 3 * len(st["names"]))
        return st["d2d"][3]

    def gathered(key):
        st = gather[key]
        send_s, recv_s, landing, tok = st["d2d"]
        landing = _copies_wait(f"gather_{key}_d2d_wait", [], landing, send_s, recv_s, st["plans"][1], tok)
        return {n: jnp.concatenate([jnp.where(chip == j, own[st["layer"]], got[j]) for j in range(N_CHIPS)],
                                   axis=shard_axis[n] - 1)
                for n, own, got in zip(st["names"], st["srcs"], landing)}

    full = {}
    conv_all = _all_gather_small(conv_w.reshape(1, -1), "gather_conv").reshape((N_DEV,) + conv_w.shape)
    full["conv_w"] = jnp.concatenate([conv_all[2 * j] for j in range(N_CHIPS)], axis=2)
    for n in SMALL:
        full[n] = w[n]

    c_all = _all_gather_small(c, "gather_c").reshape(N_DEV, d)
    c_act = _silu_rows(_pad_to(c_all, ADA_PAD, 0), "silu_c")
    b_cols = lax.dynamic_slice_in_dim(b_ada, chip * ada_cols, ada_cols, axis=1)
    mod_cols = jnp.stack([
        _mm(c_act, w_ada, b_layer=l, acc_in=jnp.broadcast_to(b_cols[l][None], (ADA_PAD, ada_cols)), name=f"ada_l{l}")[:N_DEV]
        for l in range(depth)])
    mod_all = _all_gather_small(mod_cols.reshape(depth * N_DEV, ada_cols), "gather_mod")
    mod_all = mod_all.reshape(N_DEV, depth, N_DEV, ada_cols)
    mods = []
    for l in range(depth):
        mine = jnp.concatenate([lax.dynamic_index_in_dim(mod_all[2 * j, l], dev, axis=0, keepdims=True)
                                for j in range(N_CHIPS)], axis=1)
        mods.append(tuple(mine[:, i * d:(i + 1) * d] for i in range(6)))

    tabs = _rope_tables(positions[0])
    first_started = start_group("l0a", 0, FIRST_NEEDED,
                                mods[depth - 1][5][:, :LANES] + full["conv_w"].reshape(1, -1)[:, :LANES])
    tie = start_group("l0b", 0, LATER_NEEDED, pass_to_sibling("l0a", None))

    def late_weights(key, next_key, next_layer, behind):
        tok = pass_to_sibling(key, behind)
        started = 0.0 if next_key is None else start_group(next_key, next_layer, FIRST_NEEDED, tok)
        return gathered(key), started

    def next_later_group(behind):
        return start_group("l1b", 1, LATER_NEEDED, pass_to_sibling("l1a", behind))

    wts, saved = [None] * depth, [None] * depth
    tied = (mods[0][0] + tie,) + mods[0][1:]
    h, saved[0], wts[0] = _layer_fwd(x[0], tied, _layer_weights(gathered("l0a"), full, 0), tabs, "l0",
                                     functools.partial(late_weights, "l0b", "l1a", 1), next_later_group)
    h, saved[1], wts[1] = _layer_fwd(h, mods[1], _layer_weights(gathered("l1a"), full, 1), tabs, "l1",
                                     functools.partial(late_weights, "l1b", None, None))
    loss_part, dh, dfn = _loss_head(h, final_norm[None], loss_target[0], "loss_head")
    dfn = dfn[0]

    def col_shards(g):
        return g.reshape(g.shape[0], N_CHIPS, g.shape[1] // N_CHIPS).transpose(1, 0, 2)

    def reduce_scatter_stages(tag, g, knames, names):
        srcs = [g[n].reshape(N_CHIPS, -1, g[n].shape[1]) if n in ROW_SHARDED else g[n] for n in knames]
        axes = [1 if n in ROW_SHARDED else 0 for n in knames]
        halves = [a.shape[ax] // 2 for a, ax in zip(srcs, axes)]
        got_shapes = [a.shape[:ax] + (kh,) + a.shape[ax + 1:] for a, ax, kh in zip(srcs, axes, halves)]
        plan_a, plan_c = _to_sibling_plan(halves, axes), _to_chips_plan(len(names))
        st, out = {}, {}
        st["a"] = _copies_start(f"{tag}_sibling_start", srcs, [lax.empty(sh, BF16) for sh in got_shapes], plan_a, len(srcs))

        def after_or(tok, after):
            return tok if after is None else after

        def stage0(after):
            send_s, recv_s, landing, tok = st["a"]
            got = _copies_wait(f"{tag}_sibling_wait", srcs, landing, send_s, recv_s, plan_a, after_or(tok, after))
            sums = {}
            for n, a, b in zip(knames, srcs, got):
                a3, b3 = (v if v.ndim == 3 else v[None] for v in (a, b))
                r = _add_half(a3, b3, half_idx, n in COL_SHARDED_AS_IS, f"{tag}_add_{n}")
                sums[n] = r if (n in COL_SHARDED_AS_IS or n in ROW_SHARDED) else r[0]
            if "w_main" in sums:
                sums["w_in"] = col_shards(_unlay_w_in(sums["w_main"], sums["w_small"]))
                sums["w_uq"] = col_shards(_unlay_w_uq(sums["w_uq"]))
                sums["w_ukv"] = col_shards(_unlay_w_ukv(sums["w_ukv"]))
            st["p"] = [sums[n] for n in names]
            st["c"] = _copies_start(f"{tag}_chips_start", st["p"], [lax.empty((3,) + p.shape[1:], BF16) for p in st["p"]],
                                    plan_c, 3 * len(names))
            return st["c"][3][0, 0]

        def stage1(after):
            send_s, recv_s, landing, tok = st["c"]
            got = _copies_wait(f"{tag}_chips_wait", st["p"], landing, send_s, recv_s, plan_c, after_or(tok, after))
            sums = [_sum_chips(p, q, chip_idx, half_idx, f"{tag}_sum_{n}") for n, p, q in zip(names, st["p"], got)]
            plan_e = _swap_plan([r.shape[0] // 2 for r in sums])
            st["e"] = _copies_start(f"{tag}_swap_start", [], sums, plan_e, len(names)) + (plan_e,)
            return st["e"][3][0, 0]

        def stage2(after):
            send_s, recv_s, landing, tok, plan_e = st["e"]
            got = _copies_wait(f"{tag}_swap_wait", [], landing, send_s, recv_s, plan_e, after_or(tok, after))
            out.update(zip(names, got))

        return (stage0, stage1, stage2), out, st["a"][3][0, 0]

    dmods, grads, groups = [None] * depth, [None] * depth, {}

    def ffn_group_l1(g):
        groups["l1_ffn"] = reduce_scatter_stages("rs_l1_ffn", g, FFN_GRADS, FFN_GRADS)
        return groups["l1_ffn"][2]

    dh, dmods[1], grads[1] = _layer_bwd(dh, saved[1], mods[1], wts[1], tabs, "l1", after_ffn=ffn_group_l1)
    groups["l1_mix"] = reduce_scatter_stages("rs_l1_mix", grads[1], MIXER_GRADS_KERNEL, MIXER_GRADS)
    tied = mods[0][:5] + (mods[0][5] + groups["l1_mix"][2],)

    def ffn_group_l0(g):
        behind = g["w_gate_up"]
        tok = groups["l1_ffn"][0][0](behind) + groups["l1_mix"][0][0](behind)
        groups["l0_ffn"] = reduce_scatter_stages("rs_l0_ffn", g, FFN_GRADS, FFN_GRADS)
        return tok + groups["l0_ffn"][2]

    def after_gdn_l0(behind):
        return groups["l0_ffn"][0][0](behind)

    dx, dmods[0], grads[0] = _layer_bwd(dh, saved[0], tied, wts[0], tabs, "l0", after_ffn=ffn_group_l0, after_gdn=after_gdn_l0)
    groups["l0_mix"] = reduce_scatter_stages("rs_l0_mix", grads[0], MIXER_GRADS_KERNEL, MIXER_GRADS)
    for key in ("l1_ffn", "l1_mix", "l0_ffn"):
        groups[key][0][1](dx)
    g_out, deltas, new_m, new_v = {}, {}, {}, {}

    def reduced(names):
        for n in names:
            g_out[n] = [groups[f"l{l}_ffn" if n in FFN_GRADS else f"l{l}_mix"][1][n] for l in range(depth)]

    def update(names, tie):
        for n in names:
            w_n = w[n] + first_started if n == "w_in" else w[n]
            g_out[n], deltas[n], new_m[n], new_v[n] = _adamw(w_n, g_out[n], env["m_" + n], env["v_" + n], tie, f"adamw_{n}")

    small = [_small_grads_ref_layout(grads[l]) for l in range(depth)]
    small_parts = [jnp.concatenate(dmods[l], axis=1).reshape(-1) for l in range(depth)]
    small_parts += [jnp.stack([small[l][n] for l in range(depth)]).reshape(-1) for n in SMALL]
    small_parts += [dfn, loss_part[0, :1]]
    small_sizes = [p.shape[0] for p in small_parts]
    packed = jnp.concatenate(small_parts)
    n_small = -(-packed.shape[0] // LANES) * LANES
    small_all = _all_gather_small(_pad_to(packed, n_small, 0).reshape(1, n_small), "gather_small_grads")
    small_sum = _sum_devices(small_all, "sum_small_grads")[0]
    offs = [0]
    for sz in small_sizes:
        offs.append(offs[-1] + sz)
    g_out["b_ada"] = jnp.stack([small_sum[offs[l]:offs[l + 1]] for l in range(depth)])
    for i, n in enumerate(SMALL):
        g_out[n] = small_sum[offs[depth + i]:offs[depth + i + 1]].reshape(w[n].shape)
    g_out["final_norm"] = small_sum[offs[depth + len(SMALL)]:offs[depth + len(SMALL) + 1]]
    loss = small_sum[offs[depth + len(SMALL) + 1]]

    c_act_t = _pad_to(c_act[:N_DEV].T, K_PAD, 1)
    g_ada = []
    for l in range(depth):
        dmod_l = small_all[:, 0, offs[l]:offs[l + 1]]
        dmod_cols = lax.dynamic_slice_in_dim(dmod_l, chip * ada_cols, ada_cols, axis=1)
        g_ada.append(_mm(c_act_t, _pad_to(dmod_cols, K_PAD, 0), name=f"ada_grad_l{l}"))
    g_out["w_ada"] = jnp.stack(g_ada)

    conv_g = jnp.stack([small[l]["conv_w"] for l in range(depth)])
    conv_all_g = _all_gather_small(conv_g.reshape(1, -1), "gather_conv_grads")
    conv_sum = _sum_devices(conv_all_g, "sum_conv_grads").reshape(conv_g.shape)
    n_cc = conv_w.shape[2]
    g_out["conv_w"] = lax.dynamic_slice_in_dim(conv_sum, chip * n_cc, n_cc, axis=2)

    mix0 = groups["l0_mix"][0]
    started = mix0[0](conv_sum.reshape(-1)[:LANES] + small_sum[:LANES])
    for key in ("l1_ffn", "l1_mix", "l0_ffn"):
        groups[key][0][2](None)
    reduced(FFN_GRADS)
    first_updates = ("w_ada", "b_ada", "final_norm", "conv_w") + SMALL + FFN_GRADS
    update(first_updates, jnp.zeros((8, LANES), F32) + started)
    corner = lambda a: a.reshape((1,) * (3 - a.ndim) + a.shape)[0, :1, :LANES]
    mix0[1](sum(corner(deltas[n]) for n in first_updates if w[n].shape[-1] >= LANES))
    mix0[2](None)
    reduced(MIXER_GRADS)
    update(MIXER_GRADS, jnp.zeros((8, LANES), F32))
    return (loss, dx[None], *[g_out[n] for n in WEIGHTS], *[deltas[n] for n in WEIGHTS],
            *[new_m[n] for n in WEIGHTS], *[new_v[n] for n in WEIGHTS])
```

```python
import functools
import math

import jax
import jax.numpy as jnp
from jax import lax
from jax.experimental import pallas as pl
from jax.experimental.pallas import tpu as pltpu

F32 = jnp.float32
BF16 = jnp.bfloat16
SDS = jax.ShapeDtypeStruct
MESH = pl.DeviceIdType.MESH
AXES = ("x", "y", "c")

EPS = 1e-6
HEADS = 8
D_NOPE = 128
D_ROPE = 64
D_QK = 256
D_V = 128
Q_LORA = 512
KV_LORA = 512
CHUNK = 64
CONV_WIDTH = 4
GDN_W = HEADS * D_V
MAIN_FIXED = 2 * Q_LORA + 4 * GDN_W
LANES = 128
VMEM_LIMIT = 56 * 1024 * 1024
ADAM_LR, ADAM_B1, ADAM_B2, ADAM_EPS, ADAM_WD, ADAM_STEP = 0.001, 0.9, 0.999, 1e-8, 0.01, 10


def _pick(n, cands):
    for cand in cands:
        if n % cand == 0:
            return cand
    return n


def _cp(sem):
    return pltpu.CompilerParams(dimension_semantics=sem, vmem_limit_bytes=VMEM_LIMIT)


def _row_tile(t):
    return _pick(t, (256, 128, 64, 32, 16, 8))


def _dot(a, b):
    return jnp.dot(a.astype(BF16), b.astype(BF16), preferred_element_type=F32)


def _dot_nt(a, b):
    return lax.dot_general(a.astype(BF16), b.astype(BF16), (((1,), (1,)), ((), ())), preferred_element_type=F32)


def _dot_tn(a, b):
    return lax.dot_general(a.astype(BF16), b.astype(BF16), (((0,), (0,)), ((), ())), preferred_element_type=F32)


def _bdot(a, b, exact=False):
    dims = (((2,), (1,)), ((0,), (0,)))
    if exact:
        ah, bh = a.astype(BF16), b.astype(BF16)
        al, bl = (a - ah.astype(F32)).astype(BF16), (b - bh.astype(F32)).astype(BF16)
        return (lax.dot_general(ah, bh, dims, preferred_element_type=F32)
                + lax.dot_general(ah, bl, dims, preferred_element_type=F32)
                + lax.dot_general(al, bh, dims, preferred_element_type=F32))
    return lax.dot_general(a.astype(BF16), b.astype(BF16), dims, preferred_element_type=F32)


def _bdot_nt(a, b):
    return lax.dot_general(a.astype(BF16), b.astype(BF16), (((2,), (2,)), ((0,), (0,))), preferred_element_type=F32)


def _sigmoid(x):
    return 1.0 / (1.0 + jnp.exp(-x))


def _mm(a, b, *, tb=False, out_dtype=F32, acc_in=None, b_layer=None, name):
    m, k = a.shape
    if b_layer is not None:
        assert not tb and b.shape[1] == k and k <= 2048
        n = b.shape[2]
    else:
        n = b.shape[0] if tb else b.shape[1]
        assert (b.shape[1] if tb else b.shape[0]) == k
    tm = _pick(m, (1024, 512, 256, 128))
    tn = _pick(n, (1024, 512, 256, 128))
    tk = k if k <= 2048 else _pick(k, (1024, 512, 256, 128))
    nk = k // tk
    has_acc = acc_in is not None

    def body_one_step(*refs):
        a_ref, b_ref = refs[:2]
        o_ref = refs[-1]
        acc = _dot_nt(a_ref[...], b_ref[...]) if tb else _dot(a_ref[...], b_ref[...])
        if has_acc:
            acc = acc + refs[2][...].astype(F32)
        o_ref[...] = acc.astype(out_dtype)

    if nk == 1:
        if b_layer is not None:
            b_spec = pl.BlockSpec((None, k, tn), lambda i, j: (b_layer, 0, j))
        else:
            b_spec = pl.BlockSpec((tn, k), lambda i, j: (j, 0)) if tb else pl.BlockSpec((k, tn), lambda i, j: (0, j))
        in_specs = [pl.BlockSpec((tm, k), lambda i, j: (i, 0)), b_spec]
        args = [a, b]
        if has_acc:
            in_specs.append(pl.BlockSpec((tm, tn), lambda i, j: (i, j)))
            args.append(acc_in)
        return pl.pallas_call(
            body_one_step, grid=(m // tm, n // tn), in_specs=in_specs, out_specs=pl.BlockSpec((tm, tn), lambda i, j: (i, j)),
            out_shape=SDS((m, n), out_dtype), name=name, compiler_params=_cp(("parallel", "parallel")))(*args)

    def body(*refs):
        if has_acc:
            a_ref, b_ref, c_ref, o_ref, acc = refs
        else:
            a_ref, b_ref, o_ref, acc = refs
        kk = pl.program_id(2)

        @pl.when(kk == 0)
        def _():
            if has_acc:
                acc[...] = c_ref[...].astype(F32)
            else:
                acc[...] = jnp.zeros_like(acc)

        if tb:
            acc[...] += _dot_nt(a_ref[...], b_ref[...])
        else:
            acc[...] += _dot(a_ref[...], b_ref[...])

        @pl.when(kk == nk - 1)
        def _():
            o_ref[...] = acc[...].astype(out_dtype)

    in_specs = [pl.BlockSpec((tm, tk), lambda i, j, kk: (i, kk)),
                pl.BlockSpec((tn, tk), lambda i, j, kk: (j, kk)) if tb
                else pl.BlockSpec((tk, tn), lambda i, j, kk: (kk, j))]
    args = [a, b]
    if has_acc:
        in_specs.append(pl.BlockSpec((tm, tn), lambda i, j, kk: (i, j)))
        args.append(acc_in)
    return pl.pallas_call(
        body, grid=(m // tm, n // tn, nk), in_specs=in_specs,
        out_specs=pl.BlockSpec((tm, tn), lambda i, j, kk: (i, j)),
        out_shape=SDS((m, n), out_dtype), scratch_shapes=[pltpu.VMEM((tm, tn), F32)],
        name=name, compiler_params=_cp(("parallel", "parallel", "arbitrary")))(*args)


def _norm_fwd(x, col, width, w, sc, sh, name):
    t = x.shape[0]
    tm = _row_tile(t)

    def body(x_ref, w_ref, sc_ref, sh_ref, o_ref):
        xv = x_ref[...]
        r = lax.rsqrt(jnp.mean(xv * xv, axis=-1, keepdims=True) + EPS)
        n = xv * r * w_ref[...]
        o_ref[...] = (n * (1.0 + sc_ref[...]) + sh_ref[...]).astype(o_ref.dtype)

    row = pl.BlockSpec((1, width), lambda i: (0, 0))
    return pl.pallas_call(
        body, grid=(t // tm,), in_specs=[pl.BlockSpec((tm, width), lambda i: (i, col)), row, row, row],
        out_specs=pl.BlockSpec((tm, width), lambda i: (i, 0)), out_shape=SDS((t, width), BF16),
        name=name, compiler_params=_cp(("parallel",)))(x, w, sc, sh)


def _norm_bwd(x, col, width, w, sc, dh, dres, out_dtype, name):
    t = x.shape[0]
    tm = _row_tile(t)
    has_res = dres is not None

    def body(*refs):
        if has_res:
            x_ref, w_ref, sc_ref, dh_ref, dres_ref, dx_ref, dw_ref, dsc_ref, dsh_ref = refs
        else:
            x_ref, w_ref, sc_ref, dh_ref, dx_ref, dw_ref, dsc_ref, dsh_ref = refs

        @pl.when(pl.program_id(0) == 0)
        def _():
            dw_ref[...] = jnp.zeros_like(dw_ref)
            dsc_ref[...] = jnp.zeros_like(dsc_ref)
            dsh_ref[...] = jnp.zeros_like(dsh_ref)

        xv = x_ref[...]
        dhv = dh_ref[...].astype(F32)
        wv = w_ref[...]
        r = lax.rsqrt(jnp.mean(xv * xv, axis=-1, keepdims=True) + EPS)
        xh = xv * r
        n = xh * wv
        dsh_ref[...] += jnp.sum(dhv, axis=0, keepdims=True)
        dsc_ref[...] += jnp.sum(dhv * n, axis=0, keepdims=True)
        dn = dhv * (1.0 + sc_ref[...])
        dw_ref[...] += jnp.sum(dn * xh, axis=0, keepdims=True)
        dxh = dn * wv
        dx = r * (dxh - xh * jnp.mean(dxh * xh, axis=-1, keepdims=True))
        if has_res:
            dx = dx + dres_ref[...]
        dx_ref[...] = dx.astype(out_dtype)

    row = pl.BlockSpec((1, width), lambda i: (0, 0))
    blk = pl.BlockSpec((tm, width), lambda i: (i, 0))
    in_specs = [pl.BlockSpec((tm, width), lambda i: (i, col)), row, row, blk]
    args = [x, w, sc, dh]
    if has_res:
        in_specs.append(blk)
        args.append(dres)
    return pl.pallas_call(
        body, grid=(t // tm,), in_specs=in_specs, out_specs=[blk, row, row, row],
        out_shape=[SDS((t, width), out_dtype), SDS((1, width), F32), SDS((1, width), F32), SDS((1, width), F32)],
        name=name, compiler_params=_cp(("arbitrary",)))(*args)


def _rope128(x, tc, ts1, ts2):
    return x * tc + pltpu.roll(x, 96, 1) * ts1 + pltpu.roll(x, 32, 1) * ts2


def _rope128_t(d, tc, ts1, ts2):
    return d * tc + pltpu.roll(d * ts1, 32, 1) + pltpu.roll(d * ts2, 96, 1)


def _rope_q(q_raw, tc, ts1, ts2, transpose, out_dtype, name):
    t = q_raw.shape[0]
    tm = _row_tile(t)

    def body(q_ref, tc_ref, s1_ref, s2_ref, o_ref):
        fn = _rope128_t if transpose else _rope128
        for h in range(HEADS):
            base = h * D_QK
            o_ref[:, base:base + LANES] = q_ref[:, base:base + LANES].astype(out_dtype)
            x = q_ref[:, base + LANES:base + D_QK].astype(F32)
            o_ref[:, base + LANES:base + D_QK] = fn(x, tc_ref[...], s1_ref[...], s2_ref[...]).astype(out_dtype)

    blk = pl.BlockSpec((tm, HEADS * D_QK), lambda i: (i, 0))
    tab = pl.BlockSpec((tm, LANES), lambda i: (i, 0))
    return pl.pallas_call(
        body, grid=(t // tm,), in_specs=[blk, tab, tab, tab], out_specs=blk,
        out_shape=SDS((t, HEADS * D_QK), out_dtype), name=name, compiler_params=_cp(("parallel",)))(q_raw, tc, ts1, ts2)


def _k_assemble(kv_raw, p_small, tc, ts1, ts2, name):
    t = kv_raw.shape[0]
    tm = _row_tile(t)

    def body(kn_ref, ps_ref, tc_ref, s1_ref, s2_ref, o_ref):
        kpe = _rope128(ps_ref[...], tc_ref[...], s1_ref[...], s2_ref[...]).astype(BF16)
        for h in range(HEADS):
            o_ref[:, h * D_QK:h * D_QK + LANES] = kn_ref[:, h * LANES:(h + 1) * LANES].astype(BF16)
            o_ref[:, h * D_QK + LANES:(h + 1) * D_QK] = kpe

    tab = pl.BlockSpec((tm, LANES), lambda i: (i, 0))
    return pl.pallas_call(
        body, grid=(t // tm,),
        in_specs=[pl.BlockSpec((tm, HEADS * LANES), lambda i: (i, 0)), tab, tab, tab, tab],
        out_specs=pl.BlockSpec((tm, HEADS * D_QK), lambda i: (i, 0)),
        out_shape=SDS((t, HEADS * D_QK), BF16), name=name, compiler_params=_cp(("parallel",)))(kv_raw, p_small, tc, ts1, ts2)


def _k_assemble_bwd(dk, dv, tc, ts1, ts2, name):
    t = dk.shape[0]
    tm = _row_tile(t)

    def body(dk_ref, dv_ref, tc_ref, s1_ref, s2_ref, o_ref, pe_ref):
        acc = jnp.zeros((tm, LANES), F32)
        for h in range(HEADS):
            o_ref[:, h * LANES:(h + 1) * LANES] = dk_ref[:, h * D_QK:h * D_QK + LANES].astype(BF16)
            acc = acc + dk_ref[:, h * D_QK + LANES:(h + 1) * D_QK].astype(F32)
        o_ref[:, HEADS * LANES:] = dv_ref[...].astype(BF16)
        pe_ref[...] = _rope128_t(acc, tc_ref[...], s1_ref[...], s2_ref[...])

    tab = pl.BlockSpec((tm, LANES), lambda i: (i, 0))
    return pl.pallas_call(
        body, grid=(t // tm,),
        in_specs=[pl.BlockSpec((tm, HEADS * D_QK), lambda i: (i, 0)), pl.BlockSpec((tm, HEADS * LANES), lambda i: (i, 0)),
                  tab, tab, tab],
        out_specs=[pl.BlockSpec((tm, 2 * HEADS * LANES), lambda i: (i, 0)), tab],
        out_shape=[SDS((t, 2 * HEADS * LANES), BF16), SDS((t, LANES), F32)],
        name=name, compiler_params=_cp(("parallel",)))(dk, dv, tc, ts1, ts2)


def _attn_tile(t):
    return _pick(t, (256, 128, 64))


def _attn_fwd(q, k, v, v_off, name):
    t = q.shape[0]
    tq = _attn_tile(t)
    scale = (D_NOPE + D_ROPE) ** -0.5

    def body(q_ref, k_ref, v_ref, o_ref, lse_ref):
        for i in range(t // tq):
            n_k = (i + 1) * tq
            s = _dot_nt(q_ref[i * tq:(i + 1) * tq, :], k_ref[0:n_k, :]) * scale
            row = lax.broadcasted_iota(jnp.int32, (tq, n_k), 0) + i * tq
            colv = lax.broadcasted_iota(jnp.int32, (tq, n_k), 1)
            s = jnp.where(colv <= row, s, -jnp.inf)
            m = jnp.max(s, axis=-1, keepdims=True)
            p = jnp.exp(s - m)
            l = jnp.sum(p, axis=-1, keepdims=True)
            o = _dot(p, v_ref[0:n_k, :]) / l
            o_ref[i * tq:(i + 1) * tq, :] = o.astype(BF16)
            lse_ref[0, i * tq:(i + 1) * tq, :] = m + jnp.log(l)

    return pl.pallas_call(
        body, grid=(HEADS,),
        in_specs=[pl.BlockSpec((t, D_QK), lambda h: (0, h)), pl.BlockSpec((t, D_QK), lambda h: (0, h)),
                  pl.BlockSpec((t, D_V), lambda h: (0, v_off + h))],
        out_specs=[pl.BlockSpec((t, D_V), lambda h: (0, h)), pl.BlockSpec((1, t, 1), lambda h: (h, 0, 0))],
        out_shape=[SDS((t, HEADS * D_V), BF16), SDS((HEADS, t, 1), F32)],
        name=name, compiler_params=_cp(("parallel",)))(q, k, v)


def _attn_bwd(q, k, v, v_off, o, lse, do, name):
    t = q.shape[0]
    tq = _attn_tile(t)
    scale = (D_NOPE + D_ROPE) ** -0.5

    def body(q_ref, k_ref, v_ref, o_ref, lse_ref, do_ref, dq_ref, dk_ref, dv_ref):
        dk_ref[...] = jnp.zeros_like(dk_ref)
        dv_ref[...] = jnp.zeros_like(dv_ref)
        for i in range(t // tq):
            n_k = (i + 1) * tq
            rows = slice(i * tq, (i + 1) * tq)
            qi = q_ref[rows, :]
            doi = do_ref[rows, :].astype(F32)
            s = _dot_nt(qi, k_ref[0:n_k, :]) * scale
            row = lax.broadcasted_iota(jnp.int32, (tq, n_k), 0) + i * tq
            colv = lax.broadcasted_iota(jnp.int32, (tq, n_k), 1)
            p = jnp.where(colv <= row, jnp.exp(s - lse_ref[0, rows, :]), 0.0)
            dp = _dot_nt(doi, v_ref[0:n_k, :])
            delta = jnp.sum(doi * o_ref[rows, :].astype(F32), axis=-1, keepdims=True)
            ds = p * (dp - delta) * scale
            dq_ref[rows, :] = _dot(ds, k_ref[0:n_k, :])
            dk_ref[0:n_k, :] += _dot_tn(ds, qi)
            dv_ref[0:n_k, :] += _dot_tn(p, doi)

    qk_spec = pl.BlockSpec((t, D_QK), lambda h: (0, h))
    v_spec = pl.BlockSpec((t, D_V), lambda h: (0, h))
    return pl.pallas_call(
        body, grid=(HEADS,),
        in_specs=[qk_spec, qk_spec, pl.BlockSpec((t, D_V), lambda h: (0, v_off + h)), v_spec,
                  pl.BlockSpec((1, t, 1), lambda h: (h, 0, 0)), v_spec],
        out_specs=[qk_spec, qk_spec, v_spec],
        out_shape=[SDS((t, HEADS * D_QK), F32), SDS((t, HEADS * D_QK), F32), SDS((t, HEADS * D_V), F32)],
        name=name, compiler_params=_cp(("parallel",)))(q, k, v, o, lse, do)


CONV_COLS = 256


def _conv_pre(u, w_ref, rowi):
    acc = u * w_ref[CONV_WIDTH - 1:CONV_WIDTH, :]
    for sft in range(1, CONV_WIDTH):
        shifted = jnp.where(rowi >= sft, pltpu.roll(u, sft, 0), 0.0)
        acc = acc + shifted * w_ref[CONV_WIDTH - 1 - sft:CONV_WIDTH - sft, :]
    return acc


def _conv_fwd(p_main, conv_w, name):
    t = p_main.shape[0]
    off = 2 * Q_LORA // CONV_COLS

    def body(u_ref, w_ref, y_ref):
        u = u_ref[...]
        rowi = lax.broadcasted_iota(jnp.int32, u.shape, 0)
        pre = _conv_pre(u, w_ref, rowi)
        y_ref[...] = pre * _sigmoid(pre)

    return pl.pallas_call(
        body, grid=(3 * GDN_W // CONV_COLS,),
        in_specs=[pl.BlockSpec((t, CONV_COLS), lambda j: (0, off + j)), pl.BlockSpec((CONV_WIDTH, CONV_COLS), lambda j: (0, j))],
        out_specs=pl.BlockSpec((t, CONV_COLS), lambda j: (0, j)), out_shape=SDS((t, 3 * GDN_W), F32),
        name=name, compiler_params=_cp(("parallel",)))(p_main, conv_w)


def _conv_bwd(p_main, conv_w, dyc, name):
    t = p_main.shape[0]
    off = 2 * Q_LORA // CONV_COLS

    def body(u_ref, w_ref, dy_ref, du_ref, dw_ref):
        u = u_ref[...]
        rowi = lax.broadcasted_iota(jnp.int32, u.shape, 0)
        pre = _conv_pre(u, w_ref, rowi)
        sg = _sigmoid(pre)
        dpre = dy_ref[...] * sg * (1.0 + pre * (1.0 - sg))
        du = dpre * w_ref[CONV_WIDTH - 1:CONV_WIDTH, :]
        dw_ref[CONV_WIDTH - 1:CONV_WIDTH, :] = jnp.sum(dpre * u, axis=0, keepdims=True)
        for sft in range(1, CONV_WIDTH):
            back = jnp.where(rowi < t - sft, pltpu.roll(dpre, t - sft, 0), 0.0)
            du = du + back * w_ref[CONV_WIDTH - 1 - sft:CONV_WIDTH - sft, :]
            shifted = jnp.where(rowi >= sft, pltpu.roll(u, sft, 0), 0.0)
            dw_ref[CONV_WIDTH - 1 - sft:CONV_WIDTH - sft, :] = jnp.sum(dpre * shifted, axis=0, keepdims=True)
        du_ref[...] = du.astype(BF16)

    blk = pl.BlockSpec((t, CONV_COLS), lambda j: (0, j))
    wblk = pl.BlockSpec((CONV_WIDTH, CONV_COLS), lambda j: (0, j))
    return pl.pallas_call(
        body, grid=(3 * GDN_W // CONV_COLS,),
        in_specs=[pl.BlockSpec((t, CONV_COLS), lambda j: (0, off + j)), wblk, blk],
        out_specs=[blk, wblk], out_shape=[SDS((t, 3 * GDN_W), BF16), SDS((CONV_WIDTH, 3 * GDN_W), F32)],
        name=name, compiler_params=_cp(("parallel",)))(p_main, conv_w, dyc)


B_LO, A_LO, A_HI = D_ROPE, D_ROPE + HEADS, D_ROPE + 2 * HEADS


def _softplus(z):
    e = jnp.exp(-jnp.abs(z))
    log1p = jnp.where(e < 0.01, e * (1.0 - e * (0.5 - e * (1.0 / 3.0))), jnp.log(1.0 + e))
    return jnp.maximum(z, 0.0) + log1p


def _gdn_gates(p_small, a_row, dt_row, name):
    t = p_small.shape[0]

    def body(ps_ref, a_ref, dt_ref, g_ref, gc_ref):
        x = ps_ref[...]
        lane = lax.broadcasted_iota(jnp.int32, x.shape, 1)
        is_g = (lane >= A_LO) & (lane < A_HI)
        g = jnp.where(is_g, -jnp.exp(a_ref[...]) * _softplus(x + dt_ref[...]), 0.0)
        g_ref[...] = jnp.where(is_g, g, _sigmoid(x))
        pos = lax.broadcasted_iota(jnp.int32, x.shape, 0) % CHUNK
        acc = g
        sft = 1
        while sft < CHUNK:
            acc = acc + jnp.where(pos >= sft, pltpu.roll(acc, sft, 0), 0.0)
            sft *= 2
        gc_ref[...] = acc

    full = pl.BlockSpec((t, LANES), lambda i: (0, 0))
    row = pl.BlockSpec((1, LANES), lambda i: (0, 0))
    return pl.pallas_call(
        body, grid=(1,), in_specs=[full, row, row], out_specs=[full, full],
        out_shape=[SDS((t, LANES), F32), SDS((t, LANES), F32)], name=name,
        compiler_params=_cp(("arbitrary",)))(p_small, a_row, dt_row)


def _gdn_gates_bwd(p_small, a_row, dt_row, gates, dgates, dkpe, name):
    t = p_small.shape[0]

    def body(ps_ref, a_ref, dt_ref, g_ref, db_ref, dkpe_ref, dp_ref, da_ref, ddt_ref):
        x = ps_ref[...]
        lane = lax.broadcasted_iota(jnp.int32, x.shape, 1)
        is_g = (lane >= A_LO) & (lane < A_HI)
        is_b = (lane >= B_LO) & (lane < A_LO)
        pos = lax.broadcasted_iota(jnp.int32, x.shape, 0) % CHUNK
        acc = jnp.where(is_g, db_ref[...], 0.0)
        sft = 1
        while sft < CHUNK:
            acc = acc + jnp.where(pos < CHUNK - sft, pltpu.roll(acc, t - sft, 0), 0.0)
            sft *= 2
        dg = acc
        gv = g_ref[...]
        dz = jnp.where(is_g, dg * (-jnp.exp(a_ref[...])) * _sigmoid(x + dt_ref[...]), 0.0)
        da_ref[...] = jnp.sum(jnp.where(is_g, dg * gv, 0.0), axis=0, keepdims=True)
        ddt_ref[...] = jnp.sum(dz, axis=0, keepdims=True)
        dlb = jnp.where(is_b, db_ref[...] * gv * (1.0 - gv), 0.0)
        dp_ref[...] = (jnp.where(lane < D_ROPE, dkpe_ref[...], 0.0) + dlb + dz).astype(BF16)

    full = pl.BlockSpec((t, LANES), lambda i: (0, 0))
    row = pl.BlockSpec((1, LANES), lambda i: (0, 0))
    return pl.pallas_call(
        body, grid=(1,), in_specs=[full, row, row, full, full, full], out_specs=[full, row, row],
        out_shape=[SDS((t, LANES), BF16), SDS((1, LANES), F32), SDS((1, LANES), F32)], name=name,
        compiler_params=_cp(("arbitrary",)))(p_small, a_row, dt_row, gates, dgates, dkpe)


def _tri_inv(l, eye):
    x = eye - l
    p = _bdot(l, l, exact=True)
    steps = int(math.log2(CHUNK)) - 1
    for s in range(steps):
        x = x + _bdot(x, p, exact=True)
        if s < steps - 1:
            p = _bdot(p, p, exact=True)
    return x


def _l2n(x3):
    r = lax.rsqrt(jnp.sum(x3 * x3, axis=-1, keepdims=True) + EPS)
    return x3 * r, r


def _head_col(a_ref, lane_lo, n):
    a = a_ref[...]
    lane = lax.broadcasted_iota(jnp.int32, a.shape, 1)
    col = jnp.sum(jnp.where(lane == lane_lo + pl.program_id(0), a, 0.0), axis=-1, keepdims=True)
    return col.reshape(n, CHUNK, 1)


def _gdn_common(q3, k3, v3, b, gc):
    n = q3.shape[0]
    ri = lax.broadcasted_iota(jnp.int32, (n, CHUNK, CHUNK), 1)
    ci = lax.broadcasted_iota(jnp.int32, (n, CHUNK, CHUNK), 2)
    lower, strict = ri >= ci, ri > ci
    eye = (ri == ci).astype(F32)
    gr = jnp.sum(gc * eye, axis=1, keepdims=True)
    qh, rq = _l2n(q3)
    qn = qh * (D_V ** -0.5)
    kn, rk = _l2n(k3)
    dec = jnp.where(lower, jnp.exp(jnp.where(lower, gc - gr, 0.0)), 0.0)
    kb = kn * b
    mm = _bdot_nt(kb, kn)
    tinv = _tri_inv(jnp.where(strict, mm * dec, 0.0), eye)
    gam = jnp.exp(gc)
    u = _bdot(tinv, v3 * b, exact=True)
    w = _bdot(tinv, kb * gam, exact=True)
    qk = _bdot_nt(qn, kn)
    aqk = jnp.where(lower, qk * dec, 0.0)
    gl = gc[:, CHUNK - 1:CHUNK, :]
    kdf = jnp.exp(gl - gc)
    return dict(ri=ri, ci=ci, lower=lower, strict=strict, eye=eye, qh=qh, rq=rq, qn=qn, kn=kn, rk=rk, dec=dec, kb=kb,
                mm=mm, gam=gam, u=u, w=w, qk=qk, aqk=aqk, gl=gl, kdf=kdf, kd=kn * kdf, gr=gr, tinv=tinv)


def _gdn_fwd(yc, p_main, gates, gcum, gn, name):
    t = yc.shape[0]
    n = t // CHUNK
    z_off = (2 * Q_LORA + 3 * GDN_W) // D_V

    def body(q_ref, k_ref, v_ref, z_ref, gt_ref, gcum_ref, gn_ref, o_ref, g_ref, s_ref, u_s, w_s, qg_s, kd_s, a_s, e_s):
        c = _gdn_common(q_ref[...].reshape(n, CHUNK, D_V), k_ref[...].reshape(n, CHUNK, D_V),
                        v_ref[...].reshape(n, CHUNK, D_V), _head_col(gt_ref, B_LO, n), _head_col(gcum_ref, A_LO, n))
        u_s[...] = c["u"]
        w_s[...] = c["w"]
        qg_s[...] = c["qn"] * c["gam"]
        kd_s[...] = c["kd"]
        a_s[...] = c["aqk"]
        e_s[...] = jnp.broadcast_to(jnp.exp(c["gl"]), (n, 1, D_V))

        def step(i, s):
            s_ref[0, i] = s
            v_new = u_s[i] - _dot(w_s[i], s)
            o = _dot(qg_s[i], s) + _dot(a_s[i], v_new)
            o_ref[pl.ds(pl.multiple_of(i * CHUNK, CHUNK), CHUNK), :] = o
            return s * e_s[i] + _dot_tn(kd_s[i], v_new)

        lax.fori_loop(0, n, step, jnp.zeros((D_V, D_V), F32))
        o = o_ref[...]
        zz = z_ref[...]
        on = o * lax.rsqrt(jnp.mean(o * o, axis=-1, keepdims=True) + EPS) * gn_ref[...]
        g_ref[...] = (on * zz * _sigmoid(zz)).astype(BF16)

    col = lambda off: pl.BlockSpec((t, D_V), lambda h: (0, off + h))
    lanes = pl.BlockSpec((t, LANES), lambda h: (0, 0))
    big = pltpu.VMEM((n, CHUNK, D_V), F32)
    return pl.pallas_call(
        body, grid=(HEADS,),
        in_specs=[col(0), col(HEADS), col(2 * HEADS), col(z_off), lanes, lanes, pl.BlockSpec((1, D_V), lambda h: (0, 0))],
        out_specs=[col(0), col(0), pl.BlockSpec((1, n, D_V, D_V), lambda h: (h, 0, 0, 0))],
        out_shape=[SDS((t, GDN_W), F32), SDS((t, GDN_W), BF16), SDS((HEADS, n, D_V, D_V), F32)],
        scratch_shapes=[big, big, big, big, pltpu.VMEM((n, CHUNK, CHUNK), F32), pltpu.VMEM((n, 1, D_V), F32)],
        name=name, compiler_params=_cp(("parallel",)))(yc, yc, yc, p_main, gates, gcum, gn)


def _gdn_bwd(yc, p_main, gates, gcum, gn, o_raw, states, dgated, name):
    t = yc.shape[0]
    n = t // CHUNK
    z_off = (2 * Q_LORA + 3 * GDN_W) // D_V

    def body(q_ref, k_ref, v_ref, z_ref, gt_ref, gcum_ref, gn_ref, o_ref, s_ref, dg_ref,
             dq_ref, dk_ref, dv_ref, dz_ref, dgt_ref, dgn_ref,
             u_s, w_s, qg_s, kd_s, at_s, e_s, do_s, du_s, dw_s, dqg_s, dkd_s, da_s, dat_s, dgs_s):
        @pl.when(pl.program_id(0) == 0)
        def _():
            dgn_ref[...] = jnp.zeros_like(dgn_ref)
            dgt_ref[...] = jnp.zeros_like(dgt_ref)

        o = o_ref[...]
        zz = z_ref[...]
        dgv = dg_ref[...]
        gnv = gn_ref[...]
        r = lax.rsqrt(jnp.mean(o * o, axis=-1, keepdims=True) + EPS)
        oh = o * r
        sg = _sigmoid(zz)
        don = dgv * zz * sg
        dz_ref[...] = (dgv * oh * gnv * sg * (1.0 + zz * (1.0 - sg))).astype(BF16)
        dgn_ref[...] += jnp.sum(don * oh, axis=0, keepdims=True)
        doh = don * gnv
        do_s[...] = (r * (doh - oh * jnp.mean(doh * oh, axis=-1, keepdims=True))).reshape(n, CHUNK, D_V)

        q3 = q_ref[...].reshape(n, CHUNK, D_V)
        k3 = k_ref[...].reshape(n, CHUNK, D_V)
        v3 = v_ref[...].reshape(n, CHUNK, D_V)
        b, gc = _head_col(gt_ref, B_LO, n), _head_col(gcum_ref, A_LO, n)
        c = _gdn_common(q3, k3, v3, b, gc)
        gr = c["gr"]
        ri, ci = c["ri"], c["ci"]
        upper, sup = ci >= ri, ci > ri
        dect = jnp.where(upper, jnp.exp(jnp.where(upper, gr - gc, 0.0)), 0.0)
        tinv_t = lax.dot_general(c["eye"], c["tinv"], (((2,), (2,)), ((0,), (0,))), precision=lax.Precision.HIGHEST,
                                 preferred_element_type=F32)
        u_s[...] = c["u"]
        w_s[...] = c["w"]
        qg_s[...] = c["qn"] * c["gam"]
        kd_s[...] = c["kd"]
        at_s[...] = jnp.where(upper, _bdot_nt(c["kn"], c["qn"]) * dect, 0.0)
        e_s[...] = jnp.broadcast_to(jnp.exp(c["gl"]), (n, 1, D_V))

        def step(j, ds):
            i = n - 1 - j
            s = s_ref[0, i]
            do_i = do_s[i]
            v_new = u_s[i] - _dot(w_s[i], s)
            dvn = _dot(at_s[i], do_i) + _dot(kd_s[i], ds)
            da_s[i] = _dot_nt(do_i, v_new)
            dat_s[i] = _dot_nt(v_new, do_i)
            dqg_s[i] = _dot_nt(do_i, s)
            dw_s[i] = -_dot_nt(dvn, s)
            dkd_s[i] = _dot_nt(v_new, ds)
            du_s[i] = dvn
            dgs_s[i] = jnp.broadcast_to(jnp.sum(jnp.sum(s * ds, axis=1, keepdims=True), axis=0, keepdims=True), (1, D_V))
            return _dot_tn(qg_s[i], do_i) + e_s[i] * ds - _dot_tn(w_s[i], dvn)

        lax.fori_loop(0, n, step, jnp.zeros((D_V, D_V), F32))

        du, dw, dqg, dkd = du_s[...], dw_s[...], dqg_s[...], dkd_s[...]
        lower, strict, dec = c["lower"], c["strict"], c["dec"]
        kn, kb, qn, gam, kdf = c["kn"], c["kb"], c["qn"], c["gam"], c["kdf"]
        drv = _bdot(tinv_t, du, exact=True)
        drk = _bdot(tinv_t, dw, exact=True)
        dl = jnp.where(strict, -(_bdot_nt(drv, c["u"]) + _bdot_nt(drk, c["w"])), 0.0)
        dlt = jnp.where(sup, -(_bdot_nt(c["u"], drv) + _bdot_nt(c["w"], drk)), 0.0)
        da = jnp.where(lower, da_s[...], 0.0)
        dat = jnp.where(upper, dat_s[...], 0.0)
        e = (dl * c["mm"] + da * c["qk"]) * dec
        col_sums = jnp.sum(e, axis=1, keepdims=True)
        dgc = jnp.sum(e, axis=2, keepdims=True) - jnp.sum(col_sums * c["eye"], axis=2, keepdims=True)
        dkb = _bdot(dl * dec, kn) + gam * drk
        dkn = _bdot(dlt * dect, kb) + _bdot(dat * dect, qn) + b * dkb + dkd * kdf
        dqn = _bdot(da * dec, kn) + gam * dqg
        dgam = jnp.sum(drk * kb, axis=-1, keepdims=True) + jnp.sum(dqg * qn, axis=-1, keepdims=True)
        dbeta = jnp.sum(dkb * kn, axis=-1, keepdims=True) + jnp.sum(drv * v3, axis=-1, keepdims=True)
        dv_ref[...] = (b * drv).reshape(t, D_V)
        ee = jnp.sum(dkd * kn, axis=-1, keepdims=True) * kdf
        dgc = dgc + dgam * gam - ee
        rowc = lax.broadcasted_iota(jnp.int32, (n, CHUNK, 1), 1)
        tail = jnp.sum(ee, axis=1, keepdims=True) + dgs_s[...][:, :, 0:1] * jnp.exp(c["gl"])
        dgc = dgc + jnp.where(rowc == CHUNK - 1, tail, 0.0)
        lane = lax.broadcasted_iota(jnp.int32, (t, LANES), 1)
        head = pl.program_id(0)
        dgt_ref[...] += (jnp.where(lane == B_LO + head, dbeta.reshape(t, 1), 0.0)
                         + jnp.where(lane == A_LO + head, dgc.reshape(t, 1), 0.0))
        sc = D_V ** -0.5
        qh, rq, rk = c["qh"], c["rq"], c["rk"]
        dq_ref[...] = (rq * (sc * dqn - qh * jnp.sum(sc * dqn * qh, axis=-1, keepdims=True))).reshape(t, D_V)
        dk_ref[...] = (rk * (dkn - kn * jnp.sum(dkn * kn, axis=-1, keepdims=True))).reshape(t, D_V)

    once = pl.Buffered(1)
    col = lambda off: pl.BlockSpec((t, D_V), lambda h: (0, off + h), pipeline_mode=once)
    out_col = pl.BlockSpec((t, D_V), lambda h: (0, h))
    lanes = pl.BlockSpec((t, LANES), lambda h: (0, 0))
    row = pl.BlockSpec((1, D_V), lambda h: (0, 0))
    big = pltpu.VMEM((n, CHUNK, D_V), F32)
    sq = pltpu.VMEM((n, CHUNK, CHUNK), F32)
    small = pltpu.VMEM((n, 1, D_V), F32)
    return pl.pallas_call(
        body, grid=(HEADS,),
        in_specs=[col(0), col(HEADS), col(2 * HEADS), col(z_off), lanes, lanes, row, col(0),
                  pl.BlockSpec((1, n, D_V, D_V), lambda h: (h, 0, 0, 0), pipeline_mode=once), col(0)],
        out_specs=[out_col, out_col, out_col, out_col, lanes, row],
        out_shape=[SDS((t, GDN_W), F32), SDS((t, GDN_W), F32), SDS((t, GDN_W), F32), SDS((t, GDN_W), BF16),
                   SDS((t, LANES), F32), SDS((1, D_V), F32)],
        scratch_shapes=[big, big, big, big, sq, small, big, big, big, big, big, sq, sq, small],
        name=name, compiler_params=_cp(("arbitrary",)))(yc, yc, yc, p_main, gates, gcum, gn, o_raw, states, dgated)


def _col_tile(d):
    return _pick(d, (512, 256, 128))


def _mix_fwd(y_a, y_b, p_main, name):
    t, d = y_a.shape
    tm, cw = _row_tile(t), _col_tile(d)
    off_a, off_b = MAIN_FIXED // cw, (MAIN_FIXED + d) // cw

    def body(ya_ref, yb_ref, ga_ref, gb_ref, u_ref):
        u_ref[...] = (_sigmoid(ga_ref[...]) * ya_ref[...] + _sigmoid(gb_ref[...]) * yb_ref[...]).astype(BF16)

    blk = pl.BlockSpec((tm, cw), lambda i, j: (i, j))
    return pl.pallas_call(
        body, grid=(t // tm, d // cw),
        in_specs=[blk, blk, pl.BlockSpec((tm, cw), lambda i, j: (i, off_a + j)), pl.BlockSpec((tm, cw), lambda i, j: (i, off_b + j))],
        out_specs=blk, out_shape=SDS((t, d), BF16), name=name,
        compiler_params=_cp(("parallel", "parallel")))(y_a, y_b, p_main, p_main)


def _mix_bwd(du, y_a, y_b, p_main, name):
    t, d = y_a.shape
    tm, cw = _row_tile(t), _col_tile(d)
    off_a, off_b = MAIN_FIXED // cw, (MAIN_FIXED + d) // cw
    nb = d // cw

    def body(du_ref, ya_ref, yb_ref, ga_ref, gb_ref, dya_ref, dyb_ref, dla_ref, dlb_ref):
        duv = du_ref[...]
        ga, gb = _sigmoid(ga_ref[...]), _sigmoid(gb_ref[...])
        dya_ref[...] = (duv * ga).astype(BF16)
        dyb_ref[...] = (duv * gb).astype(BF16)
        dla_ref[...] = (duv * ya_ref[...] * ga * (1.0 - ga)).astype(BF16)
        dlb_ref[...] = (duv * yb_ref[...] * gb * (1.0 - gb)).astype(BF16)

    blk = pl.BlockSpec((tm, cw), lambda i, j: (i, j))
    outs = pl.pallas_call(
        body, grid=(t // tm, nb),
        in_specs=[blk, blk, blk, pl.BlockSpec((tm, cw), lambda i, j: (i, off_a + j)),
                  pl.BlockSpec((tm, cw), lambda i, j: (i, off_b + j))],
        out_specs=[blk, blk, blk, blk],
        out_shape=[SDS((t, d), BF16), SDS((t, d), BF16), SDS((t, d), BF16), SDS((t, d), BF16)], name=name,
        compiler_params=_cp(("parallel", "parallel")))(du, y_a, y_b, p_main, p_main)
    return outs


def _gate_res(x, y, gt, name):
    t, d = x.shape
    tm = _row_tile(t)

    def body(x_ref, y_ref, g_ref, o_ref):
        o_ref[...] = x_ref[...] + g_ref[...] * y_ref[...]

    blk = pl.BlockSpec((tm, d), lambda i: (i, 0))
    return pl.pallas_call(
        body, grid=(t // tm,), in_specs=[blk, blk, pl.BlockSpec((1, d), lambda i: (0, 0))], out_specs=blk,
        out_shape=SDS((t, d), F32), name=name, compiler_params=_cp(("parallel",)))(x, y, gt)


def _gate_res_bwd(dx, y, gt, name):
    t, d = dx.shape
    tm = _row_tile(t)

    def body(dx_ref, y_ref, g_ref, dg_ref, dy_ref):
        @pl.when(pl.program_id(0) == 0)
        def _():
            dg_ref[...] = jnp.zeros_like(dg_ref)

        dxv = dx_ref[...]
        dg_ref[...] += jnp.sum(dxv * y_ref[...], axis=0, keepdims=True)
        dy_ref[...] = (dxv * g_ref[...]).astype(BF16)

    blk = pl.BlockSpec((tm, d), lambda i: (i, 0))
    row = pl.BlockSpec((1, d), lambda i: (0, 0))
    return pl.pallas_call(
        body, grid=(t // tm,), in_specs=[blk, blk, row], out_specs=[row, blk],
        out_shape=[SDS((1, d), F32), SDS((t, d), BF16)], name=name, compiler_params=_cp(("arbitrary",)))(dx, y, gt)


def _swiglu_fwd(gu, name):
    t, f2 = gu.shape
    f = f2 // 2
    tm, cw = _row_tile(t), _col_tile(f)
    nb = f // cw

    def body(g_ref, u_ref, o_ref):
        g = g_ref[...]
        o_ref[...] = (g * _sigmoid(g) * u_ref[...]).astype(BF16)

    return pl.pallas_call(
        body, grid=(t // tm, nb),
        in_specs=[pl.BlockSpec((tm, cw), lambda i, j: (i, j)), pl.BlockSpec((tm, cw), lambda i, j: (i, nb + j))],
        out_specs=pl.BlockSpec((tm, cw), lambda i, j: (i, j)), out_shape=SDS((t, f), BF16), name=name,
        compiler_params=_cp(("parallel", "parallel")))(gu, gu)


def _swiglu_bwd(gu, da, name):
    t, f2 = gu.shape
    f = f2 // 2
    tm, cw = _row_tile(t), _col_tile(f)
    nb = f // cw

    def body(g_ref, u_ref, da_ref, dg_ref, dup_ref):
        g = g_ref[...]
        dav = da_ref[...]
        sg = _sigmoid(g)
        dg_ref[...] = (dav * u_ref[...] * sg * (1.0 + g * (1.0 - sg))).astype(BF16)
        dup_ref[...] = (dav * g * sg).astype(BF16)

    blk = pl.BlockSpec((tm, cw), lambda i, j: (i, j))
    dg, dup = pl.pallas_call(
        body, grid=(t // tm, nb),
        in_specs=[blk, pl.BlockSpec((tm, cw), lambda i, j: (i, nb + j)), blk], out_specs=[blk, blk],
        out_shape=[SDS((t, f), BF16), SDS((t, f), BF16)], name=name,
        compiler_params=_cp(("parallel", "parallel")))(gu, gu, da)
    return dg, dup


def _loss_head(x, w, target, name):
    t, d = x.shape
    tm = _row_tile(t)

    def body(x_ref, w_ref, t_ref, l_ref, dx_ref, dw_ref):
        @pl.when(pl.program_id(0) == 0)
        def _():
            l_ref[...] = jnp.zeros_like(l_ref)
            dw_ref[...] = jnp.zeros_like(dw_ref)

        xv = x_ref[...]
        wv = w_ref[...]
        r = lax.rsqrt(jnp.mean(xv * xv, axis=-1, keepdims=True) + EPS)
        xh = xv * r
        err = xh * wv - t_ref[...]
        per_tok = jnp.mean(err * err, axis=-1, keepdims=True)
        l_ref[...] += 0.5 * jnp.sum(per_tok, axis=0, keepdims=True)
        dy = err * (1.0 / d)
        dw_ref[...] += jnp.sum(dy * xh, axis=0, keepdims=True)
        dxh = dy * wv
        dx_ref[...] = r * (dxh - xh * jnp.mean(dxh * xh, axis=-1, keepdims=True))

    blk = pl.BlockSpec((tm, d), lambda i: (i, 0))
    row = pl.BlockSpec((1, d), lambda i: (0, 0))
    return pl.pallas_call(
        body, grid=(t // tm,), in_specs=[blk, row, blk],
        out_specs=[pl.BlockSpec((1, LANES), lambda i: (0, 0)), blk, row],
        out_shape=[SDS((1, LANES), F32), SDS((t, d), F32), SDS((1, d), F32)], name=name,
        compiler_params=_cp(("arbitrary",)))(x, w, target)


def _adamw(w, g, m, v, tie, name):
    shape = w.shape
    per_layer = isinstance(g, (list, tuple))
    n_layers = shape[0] if (w.ndim == 3 and shape[1] % 8 == 0) else 1
    cols = shape[-1]
    rows = w.size // cols // n_layers
    w, m, v = (a.reshape(n_layers * rows, cols) for a in (w, m, v))
    if not per_layer:
        g = g.reshape(n_layers * rows, cols)
    lanes_padded = -(-cols // LANES) * LANES
    budget_rows = max(8, (24 * 1024 * 1024) // (lanes_padded * 4 * 18))
    tr = rows
    if rows > budget_rows:
        tr = _pick(rows, tuple(c for c in (1024, 512, 256, 128, 64, 32, 16, 8) if c <= budget_rows))
    nrb = rows // tr
    c1 = 1.0 / (1.0 - ADAM_B1 ** ADAM_STEP)
    c2 = 1.0 / (1.0 - ADAM_B2 ** ADAM_STEP)
    n_g = len(g) if per_layer else 1

    def body(*refs):
        w_ref, m_ref, v_ref = refs[:3]
        g_refs = refs[3:3 + n_g]
        outs = refs[4 + n_g:]
        gv = g_refs[0][...]
        for l in range(1, n_g):
            gv = jnp.where(pl.program_id(0) == l, g_refs[l][...], gv)
        mn = ADAM_B1 * m_ref[...] + (1.0 - ADAM_B1) * gv
        vn = ADAM_B2 * v_ref[...] + (1.0 - ADAM_B2) * (gv * gv)
        outs[0][...] = -ADAM_LR * ((mn * c1) / (jnp.sqrt(vn * c2) + ADAM_EPS) + ADAM_WD * w_ref[...])
        outs[1][...] = mn
        outs[2][...] = vn
        if per_layer:
            outs[3][...] = gv

    blk = pl.BlockSpec((tr, cols), lambda l, i: (l * nrb + i, 0))
    g_specs = [pl.BlockSpec((tr, cols), lambda l, i: (i, 0))] * n_g if per_layer else [blk]
    n_out = 4 if per_layer else 3
    outs = pl.pallas_call(
        body, grid=(n_layers, nrb), in_specs=[blk, blk, blk] + g_specs + [pl.BlockSpec((8, LANES), lambda l, i: (0, 0))],
        out_specs=[blk] * n_out, out_shape=[SDS(w.shape, F32)] * n_out, name=name,
        compiler_params=_cp(("parallel", "parallel")))(w, m, v, *(g if per_layer else [g]), tie)
    g_out = outs[3] if per_layer else g
    return (g_out.reshape(shape),) + tuple(o.reshape(shape) for o in outs[:3])


KPE_LO = 2 * Q_LORA
QKVZ_LO = KPE_LO + D_ROPE
BA_LO = QKVZ_LO + 4 * GDN_W
GATE_LO = BA_LO + 2 * HEADS


def _lay_w_in(w_in):
    d = w_in.shape[0]
    main = jnp.concatenate([w_in[:, :KPE_LO], w_in[:, QKVZ_LO:BA_LO], w_in[:, GATE_LO:]], axis=1)
    small = jnp.concatenate([w_in[:, KPE_LO:QKVZ_LO], w_in[:, BA_LO:GATE_LO],
                             jnp.zeros((d, LANES - D_ROPE - 2 * HEADS), w_in.dtype)], axis=1)
    return main, small


def _unlay_w_in(g_main, g_small):
    return jnp.concatenate([g_main[:, :KPE_LO], g_small[:, :D_ROPE], g_main[:, KPE_LO:KPE_LO + 4 * GDN_W],
                            g_small[:, D_ROPE:D_ROPE + 2 * HEADS], g_main[:, MAIN_FIXED:]], axis=1)


def _lay_w_uq(w_uq):
    r = w_uq.reshape(Q_LORA, HEADS, D_NOPE + D_ROPE)
    r = jnp.pad(r, ((0, 0), (0, 0), (0, D_QK - D_NOPE - D_ROPE)))
    return r.reshape(Q_LORA, HEADS * D_QK)


def _unlay_w_uq(g):
    rows = g.shape[0]
    return g.reshape(rows, HEADS, D_QK)[:, :, :D_NOPE + D_ROPE].reshape(rows, HEADS * (D_NOPE + D_ROPE))


def _lay_w_ukv(w_ukv):
    return w_ukv.reshape(KV_LORA, HEADS, 2, D_V).transpose(0, 2, 1, 3).reshape(KV_LORA, 2 * HEADS * D_V)


def _unlay_w_ukv(g):
    rows = g.shape[0]
    return g.reshape(rows, 2, HEADS, D_V).transpose(0, 2, 1, 3).reshape(rows, 2 * HEADS * D_V)


def _lane_row(vec, lo):
    return jnp.pad(vec.reshape(1, -1), ((0, 0), (lo, LANES - lo - vec.shape[0])))


def _rope_tables(positions):
    half = D_ROPE // 2
    inv_freq = 1.0 / (10000.0 ** (jnp.arange(0, D_ROPE, 2, dtype=F32) / D_ROPE))
    ang = positions.astype(F32)[:, None] * inv_freq
    cos, sin = jnp.cos(ang), jnp.sin(ang)
    t = positions.shape[0]
    zeros = lambda n: jnp.zeros((t, n), F32)
    tc = jnp.concatenate([cos, cos, zeros(LANES - D_ROPE)], axis=1)
    ts1 = jnp.concatenate([-sin, zeros(LANES - half)], axis=1)
    ts2 = jnp.concatenate([zeros(half), sin, zeros(LANES - D_ROPE)], axis=1)
    return tc, ts1, ts2


def _layer_fwd(x, mod, wt, tabs, tag, late_weights, after_gate_up=None):
    t, d = x.shape
    sh_a, sc_a, gt_a, sh_f, sc_f, gt_f = mod
    zero_l = jnp.zeros((1, Q_LORA), F32)
    s = dict(x=x)
    s["h1"] = _norm_fwd(x, 0, d, wt["norm_mix"], sc_a, sh_a, f"{tag}_norm_mix")
    s["p_main"] = _mm(s["h1"], wt["w_main"], name=f"{tag}_in_main")
    s["p_small"] = _mm(s["h1"], wt["w_small"], name=f"{tag}_in_small")
    s["cqn"] = _norm_fwd(s["p_main"], 0, Q_LORA, wt["q_a_norm"], zero_l, zero_l, f"{tag}_q_norm")
    s["ckvn"] = _norm_fwd(s["p_main"], 1, KV_LORA, wt["kv_a_norm"], zero_l, zero_l, f"{tag}_kv_norm")
    q_raw = _mm(s["cqn"], wt["w_uq"], name=f"{tag}_uq")
    s["kv_raw"] = _mm(s["ckvn"], wt["w_ukv"], name=f"{tag}_ukv")
    s["q_r"] = _rope_q(q_raw, *tabs, False, BF16, f"{tag}_rope_q")
    s["k_r"] = _k_assemble(s["kv_raw"], s["p_small"], *tabs, f"{tag}_k_asm")
    s["o"], s["lse"] = _attn_fwd(s["q_r"], s["k_r"], s["kv_raw"], HEADS, f"{tag}_attn")
    s["yc"] = _conv_fwd(s["p_main"], wt["conv_w"], f"{tag}_conv")
    s["gates"], s["gcum"] = _gdn_gates(s["p_small"], wt["a_row"], wt["dt_row"], f"{tag}_gates")
    s["o_raw"], s["gated"], s["states"] = _gdn_fwd(s["yc"], s["p_main"], s["gates"], s["gcum"], wt["gdn_norm"], f"{tag}_gdn")
    late, started = late_weights(s["gated"])
    wt = {**wt, **late}
    s["y_a"] = _mm(s["o"], wt["w_o_mla"], name=f"{tag}_o_mla")
    s["y_b"] = _mm(s["gated"], wt["w_o_gdn"], name=f"{tag}_o_gdn")
    s["u"] = _mix_fwd(s["y_a"], s["y_b"], s["p_main"], f"{tag}_mix")
    s["mixo"] = _mm(s["u"], wt["w_o"], name=f"{tag}_o")
    s["x2"] = _gate_res(x, s["mixo"], gt_a, f"{tag}_res_a")
    s["h2"] = _norm_fwd(s["x2"], 0, d, wt["norm_ffn"] + started, sc_f, sh_f, f"{tag}_norm_ffn")
    s["gu"] = _mm(s["h2"], wt["w_gate_up"], name=f"{tag}_gate_up")
    if after_gate_up is not None:
        gt_f = gt_f + after_gate_up(s["gu"])
    s["a"] = _swiglu_fwd(s["gu"], f"{tag}_swiglu")
    s["f"] = _mm(s["a"], wt["w_down"], name=f"{tag}_down")
    return _gate_res(s["x2"], s["f"], gt_f, f"{tag}_res_f"), s, wt


def _layer_bwd(dx3, s, mod, wt, tabs, tag, after_ffn=None, after_gdn=None):
    x = s["x"]
    t, d = x.shape
    sh_a, sc_a, gt_a, sh_f, sc_f, gt_f = mod
    zero_l = jnp.zeros((1, Q_LORA), F32)
    g = {}
    dgt_f, df = _gate_res_bwd(dx3, s["f"], gt_f, f"{tag}_b_res_f")
    da = _mm(df, wt["w_down"], tb=True, name=f"{tag}_b_down_x")
    g["w_down"] = _mm(s["a"].T, df, out_dtype=BF16, name=f"{tag}_b_down_w")
    dgate, dup = _swiglu_bwd(s["gu"], da, f"{tag}_b_swiglu")
    dgu = jnp.concatenate([dgate, dup], axis=1)
    dh2 = _mm(dgu, wt["w_gate_up"], tb=True, name=f"{tag}_b_gate_up_x")
    g["w_gate_up"] = _mm(s["h2"].T, dgu, out_dtype=BF16, name=f"{tag}_b_gate_up_w")
    if after_ffn is not None:
        gt_a = gt_a + after_ffn(g)
    dx2, g["norm_ffn"], dsc_f, dsh_f = _norm_bwd(s["x2"], 0, d, wt["norm_ffn"], sc_f, dh2, dx3, F32, f"{tag}_b_norm_ffn")
    dgt_a, dmixo = _gate_res_bwd(dx2, s["mixo"], gt_a, f"{tag}_b_res_a")
    du = _mm(dmixo, wt["w_o"], tb=True, name=f"{tag}_b_o_x")
    g["w_o"] = _mm(s["u"].T, dmixo, out_dtype=BF16, name=f"{tag}_b_o_w")
    dy_a, dy_b, dl_a, dl_b = _mix_bwd(du, s["y_a"], s["y_b"], s["p_main"], f"{tag}_b_mix")
    dgated = _mm(dy_b, wt["w_o_gdn"], tb=True, name=f"{tag}_b_o_gdn_x")
    g["w_o_gdn"] = _mm(s["gated"].T, dy_b, out_dtype=BF16, name=f"{tag}_b_o_gdn_w")
    dq_c, dk_c, dv_c, dz, dgates, g["gdn_norm"] = _gdn_bwd(
        s["yc"], s["p_main"], s["gates"], s["gcum"], wt["gdn_norm"], s["o_raw"], s["states"], dgated, f"{tag}_b_gdn")
    du_conv, g["conv_w"] = _conv_bwd(s["p_main"], wt["conv_w"], jnp.concatenate([dq_c, dk_c, dv_c], axis=1), f"{tag}_b_conv")
    do = _mm(dy_a, wt["w_o_mla"], tb=True, name=f"{tag}_b_o_mla_x")
    g["w_o_mla"] = _mm(s["o"].T, dy_a, out_dtype=BF16, name=f"{tag}_b_o_mla_w")
    dq_r, dk_r, dv = _attn_bwd(s["q_r"], s["k_r"], s["kv_raw"], HEADS, s["o"], s["lse"], do, f"{tag}_b_attn")
    q_a_norm = wt["q_a_norm"]
    if after_gdn is not None:
        q_a_norm = q_a_norm + after_gdn(du_conv)
    dq_raw = _rope_q(dq_r, *tabs, True, BF16, f"{tag}_b_rope_q")
    dkv_raw, dkpe = _k_assemble_bwd(dk_r, dv, *tabs, f"{tag}_b_k_asm")
    dcqn = _mm(dq_raw, wt["w_uq"], tb=True, name=f"{tag}_b_uq_x")
    g["w_uq"] = _mm(s["cqn"].T, dq_raw, out_dtype=BF16, name=f"{tag}_b_uq_w")
    dckvn = _mm(dkv_raw, wt["w_ukv"], tb=True, name=f"{tag}_b_ukv_x")
    g["w_ukv"] = _mm(s["ckvn"].T, dkv_raw, out_dtype=BF16, name=f"{tag}_b_ukv_w")
    dc_q, g["q_a_norm"], _, _ = _norm_bwd(s["p_main"], 0, Q_LORA, q_a_norm, zero_l, dcqn, None, BF16, f"{tag}_b_q_norm")
    dc_kv, g["kv_a_norm"], _, _ = _norm_bwd(s["p_main"], 1, KV_LORA, wt["kv_a_norm"], zero_l, dckvn, None, BF16,
                                            f"{tag}_b_kv_norm")
    dp_small, g["a_row"], g["dt_row"] = _gdn_gates_bwd(
        s["p_small"], wt["a_row"], wt["dt_row"], s["gates"], dgates, dkpe, f"{tag}_b_gates")
    dp_main = jnp.concatenate([dc_q, dc_kv, du_conv, dz, dl_a, dl_b], axis=1)
    h1t = s["h1"].T
    dh1 = _mm(dp_small, wt["w_small"], tb=True, name=f"{tag}_b_in_small_x")
    dh1 = _mm(dp_main, wt["w_main"], tb=True, acc_in=dh1, name=f"{tag}_b_in_main_x")
    g["w_main"] = _mm(h1t, dp_main, out_dtype=BF16, name=f"{tag}_b_in_main_w")
    g["w_small"] = _mm(h1t, dp_small, out_dtype=BF16, name=f"{tag}_b_in_small_w")
    dx, g["norm_mix"], dsc_a, dsh_a = _norm_bwd(x, 0, d, wt["norm_mix"], sc_a, dh1, dx2, F32, f"{tag}_b_norm_mix")
    return dx, (dsh_a, dsc_a, dgt_a, dsh_f, dsc_f, dgt_f), g


def _layer_weights(big, full, l):
    w_main, w_small = _lay_w_in(big["w_in"])
    return dict(
        w_main=w_main, w_small=w_small, w_uq=_lay_w_uq(big["w_uq"]), w_ukv=_lay_w_ukv(big["w_ukv"]),
        conv_w=full["conv_w"][l],
        norm_mix=full["norm_mix"][l][None], norm_ffn=full["norm_ffn"][l][None],
        q_a_norm=full["q_a_norm"][l][None], kv_a_norm=full["kv_a_norm"][l][None], gdn_norm=full["gdn_norm"][l][None],
        a_row=_lane_row(full["A_log"][l], A_LO), dt_row=_lane_row(full["dt_bias"][l], A_LO))


def _small_grads_ref_layout(g):
    return dict(
        conv_w=g["conv_w"], norm_mix=g["norm_mix"][0], norm_ffn=g["norm_ffn"][0], q_a_norm=g["q_a_norm"][0],
        kv_a_norm=g["kv_a_norm"][0], gdn_norm=g["gdn_norm"][0], A_log=g["a_row"][0, A_LO:A_HI],
        dt_bias=g["dt_row"][0, A_LO:A_HI])


def _local_step(x, mods, target, final_norm, full, positions):
    tabs = _rope_tables(positions)
    depth = len(mods)
    wts, saved = [None] * depth, []
    h = x
    for l in range(depth):
        early = _layer_weights({n: full[n][l] for n in FIRST_NEEDED}, full, l)
        h, s, wts[l] = _layer_fwd(h, mods[l], early, tabs, f"l{l}", lambda _, l=l: ({n: full[n][l] for n in LATER_NEEDED}, 0.0))
        saved.append(s)
    loss, dh, dfn = _loss_head(h, final_norm[None], target, "loss_head")
    dmods, grads = [None] * depth, [None] * depth
    for l in reversed(range(depth)):
        dh, dmods[l], grads[l] = _layer_bwd(dh, saved[l], mods[l], wts[l], tabs, f"l{l}")
    return loss, dh, dmods, grads, dfn[0]


HBM_SPEC = pl.BlockSpec(memory_space=pl.ANY)
VMEM_SPEC = pl.BlockSpec(memory_space=pltpu.VMEM)
N_CHIPS = 4
N_DEV = 8


def _me():
    return lax.axis_index("x"), lax.axis_index("y"), lax.axis_index("c")


def _flip(pos, f):
    mx, my, mc = pos
    fx, fy, fc = (f >> 2) & 1, (f >> 1) & 1, f & 1
    return ((mx + fx) % 2, (my + fy) % 2, (mc + fc) % 2)


def _all_gather_small(x, name):
    r, n = x.shape

    def body(x_ref, out_ref, send_sems, recv_sems, local_sem):
        me = _me()
        row = lambda p: 4 * p[0] + 2 * p[1] + p[2]
        mine = pltpu.make_async_copy(x_ref, out_ref.at[row(me)], local_sem)
        mine.start()

        def copy(f, origin):
            return pltpu.make_async_remote_copy(
                src_ref=x_ref, dst_ref=out_ref.at[row(origin)], send_sem=send_sems.at[f - 1], recv_sem=recv_sems.at[f - 1],
                device_id=_flip(me, f), device_id_type=MESH)

        sends = [copy(f, me) for f in range(1, N_DEV)]
        for cp in sends:
            cp.start()
        for f in range(1, N_DEV):
            copy(f, _flip(me, f)).wait_recv()
        for cp in sends:
            cp.wait_send()
        mine.wait()

    return pl.pallas_call(
        body, out_shape=SDS((N_DEV, r, n), x.dtype), in_specs=[VMEM_SPEC], out_specs=VMEM_SPEC,
        scratch_shapes=[pltpu.SemaphoreType.DMA((N_DEV - 1,)), pltpu.SemaphoreType.DMA((N_DEV - 1,)), pltpu.SemaphoreType.DMA],
        name=name, compiler_params=pltpu.CompilerParams(vmem_limit_bytes=VMEM_LIMIT))(x)


CHIP_FLIPS = (2, 4, 6)
SIBLING = 1


def _chip_of(pos):
    return 2 * pos[0] + pos[1]


def _sum_chips(p, got, chip, half, name):
    _, kh, ns = p.shape
    tr = _pick(kh, (256, 128, 64, 32, 16))
    nrb = kh // tr

    def body(c_ref, h_ref, a_ref, b_ref, o_ref):
        acc = a_ref[0].astype(F32)
        for j in range(3):
            acc = acc + b_ref[j].astype(F32)
        o_ref[...] = acc

    grid_spec = pltpu.PrefetchScalarGridSpec(
        num_scalar_prefetch=2, grid=(nrb,),
        in_specs=[pl.BlockSpec((1, tr, ns), lambda i, c, h: (c[0], i, 0)),
                  pl.BlockSpec((3, tr, ns), lambda i, c, h: (0, i, 0))],
        out_specs=pl.BlockSpec((tr, ns), lambda i, c, h: (h[0] * nrb + i, 0)))
    return pl.pallas_call(body, grid_spec=grid_spec, out_shape=SDS((2 * kh, ns), F32), name=name,
                          compiler_params=_cp(("parallel",)))(chip, half, p, got)


SEM_SPEC = pl.BlockSpec(memory_space=pltpu.SEMAPHORE)
HBM_ONLY = pl.BlockSpec(memory_space=pltpu.HBM)
DATAFLOW = pltpu.SideEffectType.DATAFLOW_SIDE_EFFECTING


def _in_hbm(a):
    return pltpu.with_memory_space_constraint(a, pltpu.HBM)


def _copies_start(name, srcs, lands, plan, n_copies):
    ns, nl = len(srcs), len(lands)

    def body(*refs):
        src_refs, land_refs = refs[:ns], refs[ns:ns + nl]
        send_sems, recv_sems = refs[ns + nl], refs[ns + nl + 1]
        token = refs[-1]
        for i, (src, dst, peer) in enumerate(plan(_me(), src_refs, land_refs)):
            pltpu.make_async_remote_copy(src_ref=src, dst_ref=dst, send_sem=send_sems.at[i], recv_sem=recv_sems.at[i],
                                         device_id=peer, device_id_type=MESH).start()
        token[...] = jnp.zeros_like(token)

    outs = pl.pallas_call(
        body, name=name,
        out_shape=(pltpu.SemaphoreType.DMA((n_copies,)), pltpu.SemaphoreType.DMA((n_copies,)),
                   *[pltpu.HBM(l.shape, l.dtype) for l in lands], SDS((8, LANES), F32)),
        in_specs=[HBM_ONLY] * (ns + nl), out_specs=(SEM_SPEC, SEM_SPEC, *[HBM_ONLY] * nl, VMEM_SPEC),
        input_output_aliases={ns + i: 2 + i for i in range(nl)},
        compiler_params=pltpu.CompilerParams(has_side_effects=DATAFLOW),
    )(*[_in_hbm(s) for s in srcs], *[_in_hbm(l) for l in lands])
    return outs[0], outs[1], list(outs[2:2 + nl]), outs[-1]


def _copies_wait(name, srcs, lands, send_sems, recv_sems, plan, after):
    ns, nl = len(srcs), len(lands)

    def body(*refs):
        src_refs, land_refs = refs[:ns], refs[ns:ns + nl]
        send_ref, recv_ref = refs[ns + nl], refs[ns + nl + 1]
        for i, (src, dst, peer) in enumerate(plan(_me(), src_refs, land_refs)):
            cp = pltpu.make_async_remote_copy(src_ref=src, dst_ref=dst, send_sem=send_ref.at[i], recv_sem=recv_ref.at[i],
                                              device_id=peer, device_id_type=MESH)
            cp.wait_send()
            cp.wait_recv()

    outs = pl.pallas_call(
        body, name=name, out_shape=[pltpu.HBM(l.shape, l.dtype) for l in lands],
        in_specs=[HBM_ONLY] * (ns + nl) + [SEM_SPEC, SEM_SPEC, HBM_SPEC], out_specs=[HBM_ONLY] * nl,
        input_output_aliases={ns + i: i for i in range(nl)},
        compiler_params=pltpu.CompilerParams(has_side_effects=DATAFLOW),
    )(*[_in_hbm(s) for s in srcs], *lands, send_sems, recv_sems, after)
    return list(outs)


def _half(ref, rows, axis):
    idx = [slice(None)] * axis + [rows]
    return ref.at[tuple(idx)]


def _gather_plans(layer, halves):
    def ici(me, srcs, lands):
        out = []
        for a, kh in enumerate(halves):
            rows = pl.ds(pl.multiple_of(me[2] * kh, 16), kh)
            for k in range(3):
                out.append((srcs[a].at[layer, rows], lands[a].at[_chip_of(me), rows], _flip(me, CHIP_FLIPS[k])))
        return out

    def d2d(me, srcs, lands):
        out = []
        for a, kh in enumerate(halves):
            rows = pl.ds(pl.multiple_of(me[2] * kh, 16), kh)
            for k in range(3):
                slab = lands[a].at[_chip_of(_flip(me, CHIP_FLIPS[k])), rows]
                out.append((slab, slab, _flip(me, SIBLING)))
        return out

    return ici, d2d


def _to_sibling_plan(halves, axes):
    def plan(me, srcs, lands):
        out = []
        for a, (kh, axis) in enumerate(zip(halves, axes)):
            rows = pl.ds(pl.multiple_of((1 - me[2]) * kh, 16), kh)
            out.append((_half(srcs[a], rows, axis), lands[a], _flip(me, SIBLING)))
        return out

    return plan


def _to_chips_plan(n_arr):
    def plan(me, srcs, lands):
        out = []
        for a in range(n_arr):
            for k in range(3):
                peer = _flip(me, CHIP_FLIPS[k])
                out.append((srcs[a].at[_chip_of(peer)], lands[a].at[k], peer))
        return out

    return plan


def _swap_plan(halves):
    def plan(me, srcs, lands):
        out = []
        for a, kh in enumerate(halves):
            rows = pl.ds(pl.multiple_of(me[2] * kh, 16), kh)
            out.append((lands[a].at[rows], lands[a].at[rows], _flip(me, SIBLING)))
        return out

    return plan


def _add_half(g, got, half, col_shards, name):
    s, kh, n = got.shape
    tr = _pick(kh, (512, 256, 128, 64, 32, 16))
    nrb = kh // tr
    width = n // N_CHIPS if col_shards else n
    cw = _pick(width, (1024, 512, 256, 128))
    per = width // cw

    def body(h_ref, a_ref, b_ref, o_ref):
        o_ref[...] = (a_ref[...].astype(F32) + b_ref[...].astype(F32)).astype(o_ref.dtype)

    in_specs = [pl.BlockSpec((None, tr, cw), lambda j, i, c, h: (j, h[0] * nrb + i, c)),
                pl.BlockSpec((None, tr, cw), lambda j, i, c, h: (j, i, c))]
    if col_shards:
        assert s == 1
        out_spec = pl.BlockSpec((None, tr, cw), lambda j, i, c, h: (c // per, i, c % per))
        out_shape = SDS((N_CHIPS, kh, width), g.dtype)
    else:
        out_spec, out_shape = in_specs[1], SDS((s, kh, n), g.dtype)
    grid_spec = pltpu.PrefetchScalarGridSpec(num_scalar_prefetch=1, grid=(s, nrb, n // cw), in_specs=in_specs,
                                             out_specs=out_spec)
    return pl.pallas_call(body, grid_spec=grid_spec, out_shape=out_shape, name=name,
                          compiler_params=_cp(("parallel", "parallel", "parallel")))(half, g, got)


def _sum_devices(g, name):
    _, _, n = g.shape

    def body(g_ref, o_ref):
        acc = g_ref[0]
        for k in range(1, N_DEV):
            acc = acc + g_ref[k]
        o_ref[...] = acc

    return pl.pallas_call(body, out_shape=SDS((1, n), F32), in_specs=[VMEM_SPEC], out_specs=VMEM_SPEC, name=name)(g)


def _silu_rows(c, name):
    def body(c_ref, o_ref):
        v = c_ref[...]
        o_ref[...] = v * _sigmoid(v)

    return pl.pallas_call(body, out_shape=SDS(c.shape, F32), in_specs=[VMEM_SPEC], out_specs=VMEM_SPEC, name=name)(c)


BIG = (("w_in", 2), ("w_uq", 2), ("w_ukv", 2), ("w_o_mla", 2), ("w_o_gdn", 2), ("w_o", 1), ("w_gate_up", 2), ("w_down", 1))
KERNEL_BIG = ("w_main", "w_small", "w_uq", "w_ukv", "w_o_mla", "w_o_gdn", "w_o", "w_gate_up", "w_down")
COL_SHARDED_AS_IS = ("w_o_mla", "w_o_gdn", "w_gate_up")
ROW_SHARDED = ("w_o", "w_down")
FIRST_NEEDED = ("w_in", "w_uq", "w_ukv")
LATER_NEEDED = ("w_o_mla", "w_o_gdn", "w_o", "w_gate_up", "w_down")
FFN_GRADS = ("w_gate_up", "w_down")
MIXER_GRADS = ("w_in", "w_uq", "w_ukv", "w_o_mla", "w_o_gdn", "w_o")
MIXER_GRADS_KERNEL = ("w_main", "w_small", "w_uq", "w_ukv", "w_o_mla", "w_o_gdn", "w_o")
SMALL = ("norm_mix", "norm_ffn", "q_a_norm", "kv_a_norm", "A_log", "dt_bias", "gdn_norm")
WEIGHTS = ("w_ada", "b_ada", "norm_mix", "norm_ffn", "w_in", "q_a_norm", "kv_a_norm", "w_uq", "w_ukv", "w_o_mla", "conv_w",
           "A_log", "dt_bias", "gdn_norm", "w_o_gdn", "w_o", "w_gate_up", "w_down", "final_norm")
ADA_PAD = 16
K_PAD = 128


def _pad_to(a, n, axis):
    pad = [(0, 0)] * a.ndim
    pad[axis] = (0, n - a.shape[axis])
    return jnp.pad(a, pad)


def kernel(x, c, positions, w_ada, b_ada, norm_mix, norm_ffn, w_in, q_a_norm, kv_a_norm, w_uq, w_ukv, w_o_mla, conv_w, A_log, dt_bias, gdn_norm, w_o_gdn, w_o, w_gate_up, w_down, final_norm, loss_target, m_w_ada, m_b_ada, m_norm_mix, m_norm_ffn, m_w_in, m_q_a_norm, m_kv_a_norm, m_w_uq, m_w_ukv, m_w_o_mla, m_conv_w, m_A_log, m_dt_bias, m_gdn_norm, m_w_o_gdn, m_w_o, m_w_gate_up, m_w_down, m_final_norm, v_w_ada, v_b_ada, v_norm_mix, v_norm_ffn, v_w_in, v_q_a_norm, v_kv_a_norm, v_w_uq, v_w_ukv, v_w_o_mla, v_conv_w, v_A_log, v_dt_bias, v_gdn_norm, v_w_o_gdn, v_w_o, v_w_gate_up, v_w_down, v_final_norm):
    env = dict(locals())
    w = {n: env[n] for n in WEIGHTS}
    depth, d = norm_mix.shape
    t = x.shape[1]
    me = _me()
    chip = _chip_of(me)
    dev = 4 * me[0] + 2 * me[1] + me[2]
    ada_cols = w_ada.shape[2]

    half_idx = me[2].astype(jnp.int32).reshape(1)
    chip_idx = chip.astype(jnp.int32).reshape(1)
    w16 = {n: w[n].astype(BF16) for n, _ in BIG}
    shard_axis = dict(BIG)
    gather = {}

    def start_group(key, layer, names, dep):
        srcs = [w16[n] for n in names]
        plans = _gather_plans(layer, [a.shape[1] // 2 for a in srcs])
        landing = [lax.empty((N_CHIPS,) + a.shape[1:], BF16) for a in srcs]
        send_s, recv_s, landing, tok = _copies_start(f"gather_{key}_ici_start", srcs + [dep], landing, plans[0], 3 * len(names))
        gather[key] = dict(layer=layer, names=names, srcs=srcs, plans=plans, ici=(send_s, recv_s, landing), tok=tok)
        return tok[0, 0]

    def pass_to_sibling(key, after):
        st = gather[key]
        send_s, recv_s, landing = st["ici"]
        landing = _copies_wait(f"gather_{key}_ici_wait", st["srcs"] + [st["tok"]], landing, send_s, recv_s, st["plans"][0],
                               st["tok"] if after is None else after)
        st["d2d"] = _copies_start(f"gather_{key}_d2d_start", [], landing, st["plans"][1], 3 * len(st["names"]))
        return st["d2d"][3]

    def gathered(key):
        st = gather[key]
        send_s, recv_s, landing, tok = st["d2d"]
        landing = _copies_wait(f"gather_{key}_d2d_wait", [], landing, send_s, recv_s, st["plans"][1], tok)
        return {n: jnp.concatenate([jnp.where(chip == j, own[st["layer"]], got[j]) for j in range(N_CHIPS)],
                                   axis=shard_axis[n] - 1)
                for n, own, got in zip(st["names"], st["srcs"], landing)}

    first_started = start_group("l0a", 0, FIRST_NEEDED, jnp.zeros((8, LANES), F32))
    c = c + first_started
    w_in_adam = tuple(env[p + "w_in"] + first_started for p in ("", "m_", "v_"))
    busy = sum(jnp.minimum(jnp.abs(a[0, :1, :1].astype(F32)), 0.0)
               for a in (*w_in_adam, *[w16[n] for n in LATER_NEEDED]))[0, 0]
    full = {}
    conv_all = _all_gather_small(conv_w.reshape(1, -1) + (first_started + busy), "gather_conv").reshape(
        (N_DEV,) + conv_w.shape)
    full["conv_w"] = jnp.concatenate([conv_all[2 * j] for j in range(N_CHIPS)], axis=2)
    for n in SMALL:
        full[n] = w[n]

    c_all = _all_gather_small(c, "gather_c").reshape(N_DEV, d)
    c_act = _silu_rows(_pad_to(c_all, ADA_PAD, 0), "silu_c")
    b_cols = lax.dynamic_slice_in_dim(b_ada, chip * ada_cols, ada_cols, axis=1)
    mod_cols = jnp.stack([
        _mm(c_act, w_ada, b_layer=l, acc_in=jnp.broadcast_to(b_cols[l][None], (ADA_PAD, ada_cols)), name=f"ada_l{l}")[:N_DEV]
        for l in range(depth)])
    mod_all = _all_gather_small(mod_cols.reshape(depth * N_DEV, ada_cols), "gather_mod")
    mod_all = mod_all.reshape(N_DEV, depth, N_DEV, ada_cols)
    mods = []
    for l in range(depth):
        mine = jnp.concatenate([lax.dynamic_index_in_dim(mod_all[2 * j, l], dev, axis=0, keepdims=True)
                                for j in range(N_CHIPS)], axis=1)
        mods.append(tuple(mine[:, i * d:(i + 1) * d] for i in range(6)))

    tabs = _rope_tables(positions[0])
    tie = start_group("l0b", 0, LATER_NEEDED, pass_to_sibling(
        "l0a", mods[depth - 1][5][:, :LANES] + full["conv_w"].reshape(1, -1)[:, :LANES]))

    def late_weights(key, next_key, next_layer, behind):
        tok = pass_to_sibling(key, behind)
        started = 0.0 if next_key is None else start_group(next_key, next_layer, FIRST_NEEDED, tok)
        return gathered(key), started

    def next_later_group(behind):
        return start_group("l1b", 1, LATER_NEEDED, pass_to_sibling("l1a", behind))

    wts, saved = [None] * depth, [None] * depth
    tied = (mods[0][0] + tie,) + mods[0][1:]
    h, saved[0], wts[0] = _layer_fwd(x[0], tied, _layer_weights(gathered("l0a"), full, 0), tabs, "l0",
                                     functools.partial(late_weights, "l0b", "l1a", 1), next_later_group)
    h, saved[1], wts[1] = _layer_fwd(h, mods[1], _layer_weights(gathered("l1a"), full, 1), tabs, "l1",
                                     functools.partial(late_weights, "l1b", None, None))
    loss_part, dh, dfn = _loss_head(h, final_norm[None], loss_target[0], "loss_head")
    dfn = dfn[0]

    def col_shards(g):
        return g.reshape(g.shape[0], N_CHIPS, g.shape[1] // N_CHIPS).transpose(1, 0, 2)

    def reduce_scatter_stages(tag, g, knames, names):
        srcs = [g[n].reshape(N_CHIPS, -1, g[n].shape[1]) if n in ROW_SHARDED else g[n] for n in knames]
        axes = [1 if n in ROW_SHARDED else 0 for n in knames]
        halves = [a.shape[ax] // 2 for a, ax in zip(srcs, axes)]
        got_shapes = [a.shape[:ax] + (kh,) + a.shape[ax + 1:] for a, ax, kh in zip(srcs, axes, halves)]
        plan_a, plan_c = _to_sibling_plan(halves, axes), _to_chips_plan(len(names))
        st, out = {}, {}
        st["a"] = _copies_start(f"{tag}_sibling_start", srcs, [lax.empty(sh, BF16) for sh in got_shapes], plan_a, len(srcs))

        def after_or(tok, after):
            return tok if after is None else after

        def stage0(after):
            send_s, recv_s, landing, tok = st["a"]
            got = _copies_wait(f"{tag}_sibling_wait", srcs, landing, send_s, recv_s, plan_a, after_or(tok, after))
            sums = {}
            for n, a, b in zip(knames, srcs, got):
                a3, b3 = (v if v.ndim == 3 else v[None] for v in (a, b))
                r = _add_half(a3, b3, half_idx, n in COL_SHARDED_AS_IS, f"{tag}_add_{n}")
                sums[n] = r if (n in COL_SHARDED_AS_IS or n in ROW_SHARDED) else r[0]
            if "w_main" in sums:
                sums["w_in"] = col_shards(_unlay_w_in(sums["w_main"], sums["w_small"]))
                sums["w_uq"] = col_shards(_unlay_w_uq(sums["w_uq"]))
                sums["w_ukv"] = col_shards(_unlay_w_ukv(sums["w_ukv"]))
            st["p"] = [sums[n] for n in names]
            st["c"] = _copies_start(f"{tag}_chips_start", st["p"], [lax.empty((3,) + p.shape[1:], BF16) for p in st["p"]],
                                    plan_c, 3 * len(names))
            return st["c"][3][0, 0]

        def stage1(after):
            send_s, recv_s, landing, tok = st["c"]
            got = _copies_wait(f"{tag}_chips_wait", st["p"], landing, send_s, recv_s, plan_c, after_or(tok, after))
            sums = [_sum_chips(p, q, chip_idx, half_idx, f"{tag}_sum_{n}") for n, p, q in zip(names, st["p"], got)]
            plan_e = _swap_plan([r.shape[0] // 2 for r in sums])
            st["e"] = _copies_start(f"{tag}_swap_start", [], sums, plan_e, len(names)) + (plan_e,)
            return st["e"][3][0, 0]

        def stage2(after):
            send_s, recv_s, landing, tok, plan_e = st["e"]
            got = _copies_wait(f"{tag}_swap_wait", [], landing, send_s, recv_s, plan_e, after_or(tok, after))
            out.update(zip(names, got))

        return (stage0, stage1, stage2), out, st["a"][3][0, 0]

    dmods, grads, groups = [None] * depth, [None] * depth, {}

    def ffn_group_l1(g):
        groups["l1_ffn"] = reduce_scatter_stages("rs_l1_ffn", g, FFN_GRADS, FFN_GRADS)
        return groups["l1_ffn"][2]

    dh, dmods[1], grads[1] = _layer_bwd(dh, saved[1], mods[1], wts[1], tabs, "l1", after_ffn=ffn_group_l1)
    groups["l1_mix"] = reduce_scatter_stages("rs_l1_mix", grads[1], MIXER_GRADS_KERNEL, MIXER_GRADS)
    tied = mods[0][:5] + (mods[0][5] + groups["l1_mix"][2],)

    def ffn_group_l0(g):
        behind = g["w_gate_up"]
        tok = groups["l1_ffn"][0][0](behind) + groups["l1_mix"][0][0](behind)
        groups["l0_ffn"] = reduce_scatter_stages("rs_l0_ffn", g, FFN_GRADS, FFN_GRADS)
        return tok + groups["l0_ffn"][2]

    def after_gdn_l0(behind):
        return groups["l0_ffn"][0][0](behind)

    dx, dmods[0], grads[0] = _layer_bwd(dh, saved[0], tied, wts[0], tabs, "l0", after_ffn=ffn_group_l0, after_gdn=after_gdn_l0)
    groups["l0_mix"] = reduce_scatter_stages("rs_l0_mix", grads[0], MIXER_GRADS_KERNEL, MIXER_GRADS)
    for key in ("l1_ffn", "l1_mix", "l0_ffn"):
        groups[key][0][1](dx)
    g_out, deltas, new_m, new_v = {}, {}, {}, {}

    def reduced(names):
        for n in names:
            g_out[n] = [groups[f"l{l}_ffn" if n in FFN_GRADS else f"l{l}_mix"][1][n] for l in range(depth)]

    def update(names, tie):
        for n in names:
            w_n, m_n, v_n = w_in_adam if n == "w_in" else (w[n], env["m_" + n], env["v_" + n])
            g_out[n], deltas[n], new_m[n], new_v[n] = _adamw(w_n, g_out[n], m_n, v_n, tie, f"adamw_{n}")

    small = [_small_grads_ref_layout(grads[l]) for l in range(depth)]
    small_parts = [jnp.concatenate(dmods[l], axis=1).reshape(-1) for l in range(depth)]
    small_parts += [jnp.stack([small[l][n] for l in range(depth)]).reshape(-1) for n in SMALL]
    small_parts += [dfn, loss_part[0, :1]]
    small_sizes = [p.shape[0] for p in small_parts]
    packed = jnp.concatenate(small_parts)
    n_small = -(-packed.shape[0] // LANES) * LANES
    small_all = _all_gather_small(_pad_to(packed, n_small, 0).reshape(1, n_small), "gather_small_grads")
    small_sum = _sum_devices(small_all, "sum_small_grads")[0]
    offs = [0]
    for sz in small_sizes:
        offs.append(offs[-1] + sz)
    g_out["b_ada"] = jnp.stack([small_sum[offs[l]:offs[l + 1]] for l in range(depth)])
    for i, n in enumerate(SMALL):
        g_out[n] = small_sum[offs[depth + i]:offs[depth + i + 1]].reshape(w[n].shape)
    g_out["final_norm"] = small_sum[offs[depth + len(SMALL)]:offs[depth + len(SMALL) + 1]]
    loss = small_sum[offs[depth + len(SMALL) + 1]]

    c_act_t = _pad_to(c_act[:N_DEV].T, K_PAD, 1)
    g_ada = []
    for l in range(depth):
        dmod_l = small_all[:, 0, offs[l]:offs[l + 1]]
        dmod_cols = lax.dynamic_slice_in_dim(dmod_l, chip * ada_cols, ada_cols, axis=1)
        g_ada.append(_mm(c_act_t, _pad_to(dmod_cols, K_PAD, 0), name=f"ada_grad_l{l}"))
    g_out["w_ada"] = jnp.stack(g_ada)

    conv_g = jnp.stack([small[l]["conv_w"] for l in range(depth)])
    conv_all_g = _all_gather_small(conv_g.reshape(1, -1), "gather_conv_grads")
    conv_sum = _sum_devices(conv_all_g, "sum_conv_grads").reshape(conv_g.shape)
    n_cc = conv_w.shape[2]
    g_out["conv_w"] = lax.dynamic_slice_in_dim(conv_sum, chip * n_cc, n_cc, axis=2)

    mix0 = groups["l0_mix"][0]
    started = mix0[0](conv_sum.reshape(-1)[:LANES] + small_sum[:LANES])
    for key in ("l1_ffn", "l1_mix", "l0_ffn"):
        groups[key][0][2](None)
    reduced(FFN_GRADS)
    first_updates = ("w_ada", "b_ada", "final_norm", "conv_w") + SMALL + FFN_GRADS
    update(first_updates, jnp.zeros((8, LANES), F32) + started)
    corner = lambda a: a.reshape((1,) * (3 - a.ndim) + a.shape)[0, :1, :LANES]
    mix0[1](sum(corner(deltas[n]) for n in first_updates if w[n].shape[-1] >= LANES))
    mix0[2](None)
    reduced(MIXER_GRADS)
    update(MIXER_GRADS, jnp.zeros((8, LANES), F32))
    return (loss, dx[None], *[g_out[n] for n in WEIGHTS], *[deltas[n] for n in WEIGHTS],
            *[new_m[n] for n in WEIGHTS], *[new_v[n] for n in WEIGHTS])
```

```python
import functools
import math

import jax
import jax.numpy as jnp
from jax import lax
from jax.experimental import pallas as pl
from jax.experimental.pallas import tpu as pltpu

F32 = jnp.float32
BF16 = jnp.bfloat16
SDS = jax.ShapeDtypeStruct
MESH = pl.DeviceIdType.MESH
AXES = ("x", "y", "c")

EPS = 1e-6
HEADS = 8
D_NOPE = 128
D_ROPE = 64
D_QK = 256
D_V = 128
Q_LORA = 512
KV_LORA = 512
CHUNK = 64
CONV_WIDTH = 4
GDN_W = HEADS * D_V
MAIN_FIXED = 2 * Q_LORA + 4 * GDN_W
LANES = 128
VMEM_LIMIT = 56 * 1024 * 1024
ADAM_LR, ADAM_B1, ADAM_B2, ADAM_EPS, ADAM_WD, ADAM_STEP = 0.001, 0.9, 0.999, 1e-8, 0.01, 10


def _pick(n, cands):
    for cand in cands:
        if n % cand == 0:
            return cand
    return n


def _cp(sem):
    return pltpu.CompilerParams(dimension_semantics=sem, vmem_limit_bytes=VMEM_LIMIT)


def _row_tile(t):
    return _pick(t, (256, 128, 64, 32, 16, 8))


def _dot(a, b):
    return jnp.dot(a.astype(BF16), b.astype(BF16), preferred_element_type=F32)


def _dot_nt(a, b):
    return lax.dot_general(a.astype(BF16), b.astype(BF16), (((1,), (1,)), ((), ())), preferred_element_type=F32)


def _dot_tn(a, b):
    return lax.dot_general(a.astype(BF16), b.astype(BF16), (((0,), (0,)), ((), ())), preferred_element_type=F32)


def _bdot(a, b, exact=False):
    dims = (((2,), (1,)), ((0,), (0,)))
    if exact:
        ah, bh = a.astype(BF16), b.astype(BF16)
        al, bl = (a - ah.astype(F32)).astype(BF16), (b - bh.astype(F32)).astype(BF16)
        return (lax.dot_general(ah, bh, dims, preferred_element_type=F32)
                + lax.dot_general(ah, bl, dims, preferred_element_type=F32)
                + lax.dot_general(al, bh, dims, preferred_element_type=F32))
    return lax.dot_general(a.astype(BF16), b.astype(BF16), dims, preferred_element_type=F32)


def _bdot_nt(a, b):
    return lax.dot_general(a.astype(BF16), b.astype(BF16), (((2,), (2,)), ((0,), (0,))), preferred_element_type=F32)


def _sigmoid(x):
    return 1.0 / (1.0 + jnp.exp(-x))


def _mm(a, b, *, tb=False, out_dtype=F32, acc_in=None, b_layer=None, name):
    m, k = a.shape
    if b_layer is not None:
        assert not tb and b.shape[1] == k and k <= 2048
        n = b.shape[2]
    else:
        n = b.shape[0] if tb else b.shape[1]
        assert (b.shape[1] if tb else b.shape[0]) == k
    tm = _pick(m, (1024, 512, 256, 128))
    tn = _pick(n, (1024, 512, 256, 128))
    tk = k if k <= 2048 else _pick(k, (1024, 512, 256, 128))
    nk = k // tk
    has_acc = acc_in is not None

    def body_one_step(*refs):
        a_ref, b_ref = refs[:2]
        o_ref = refs[-1]
        acc = _dot_nt(a_ref[...], b_ref[...]) if tb else _dot(a_ref[...], b_ref[...])
        if has_acc:
            acc = acc + refs[2][...].astype(F32)
        o_ref[...] = acc.astype(out_dtype)

    if nk == 1:
        if b_layer is not None:
            b_spec = pl.BlockSpec((None, k, tn), lambda i, j: (b_layer, 0, j))
        else:
            b_spec = pl.BlockSpec((tn, k), lambda i, j: (j, 0)) if tb else pl.BlockSpec((k, tn), lambda i, j: (0, j))
        in_specs = [pl.BlockSpec((tm, k), lambda i, j: (i, 0)), b_spec]
        args = [a, b]
        if has_acc:
            in_specs.append(pl.BlockSpec((tm, tn), lambda i, j: (i, j)))
            args.append(acc_in)
        return pl.pallas_call(
            body_one_step, grid=(m // tm, n // tn), in_specs=in_specs, out_specs=pl.BlockSpec((tm, tn), lambda i, j: (i, j)),
            out_shape=SDS((m, n), out_dtype), name=name, compiler_params=_cp(("parallel", "parallel")))(*args)

    def body(*refs):
        if has_acc:
            a_ref, b_ref, c_ref, o_ref, acc = refs
        else:
            a_ref, b_ref, o_ref, acc = refs
        kk = pl.program_id(2)

        @pl.when(kk == 0)
        def _():
            if has_acc:
                acc[...] = c_ref[...].astype(F32)
            else:
                acc[...] = jnp.zeros_like(acc)

        if tb:
            acc[...] += _dot_nt(a_ref[...], b_ref[...])
        else:
            acc[...] += _dot(a_ref[...], b_ref[...])

        @pl.when(kk == nk - 1)
        def _():
            o_ref[...] = acc[...].astype(out_dtype)

    in_specs = [pl.BlockSpec((tm, tk), lambda i, j, kk: (i, kk)),
                pl.BlockSpec((tn, tk), lambda i, j, kk: (j, kk)) if tb
                else pl.BlockSpec((tk, tn), lambda i, j, kk: (kk, j))]
    args = [a, b]
    if has_acc:
        in_specs.append(pl.BlockSpec((tm, tn), lambda i, j, kk: (i, j)))
        args.append(acc_in)
    return pl.pallas_call(
        body, grid=(m // tm, n // tn, nk), in_specs=in_specs,
        out_specs=pl.BlockSpec((tm, tn), lambda i, j, kk: (i, j)),
        out_shape=SDS((m, n), out_dtype), scratch_shapes=[pltpu.VMEM((tm, tn), F32)],
        name=name, compiler_params=_cp(("parallel", "parallel", "arbitrary")))(*args)


def _norm_fwd(x, col, width, w, sc, sh, name):
    t = x.shape[0]
    tm = _row_tile(t)

    def body(x_ref, w_ref, sc_ref, sh_ref, o_ref):
        xv = x_ref[...]
        r = lax.rsqrt(jnp.mean(xv * xv, axis=-1, keepdims=True) + EPS)
        n = xv * r * w_ref[...]
        o_ref[...] = (n * (1.0 + sc_ref[...]) + sh_ref[...]).astype(o_ref.dtype)

    row = pl.BlockSpec((1, width), lambda i: (0, 0))
    return pl.pallas_call(
        body, grid=(t // tm,), in_specs=[pl.BlockSpec((tm, width), lambda i: (i, col)), row, row, row],
        out_specs=pl.BlockSpec((tm, width), lambda i: (i, 0)), out_shape=SDS((t, width), BF16),
        name=name, compiler_params=_cp(("parallel",)))(x, w, sc, sh)


def _norm_bwd(x, col, width, w, sc, dh, dres, out_dtype, name):
    t = x.shape[0]
    tm = _row_tile(t)
    has_res = dres is not None

    def body(*refs):
        if has_res:
            x_ref, w_ref, sc_ref, dh_ref, dres_ref, dx_ref, dw_ref, dsc_ref, dsh_ref = refs
        else:
            x_ref, w_ref, sc_ref, dh_ref, dx_ref, dw_ref, dsc_ref, dsh_ref = refs

        @pl.when(pl.program_id(0) == 0)
        def _():
            dw_ref[...] = jnp.zeros_like(dw_ref)
            dsc_ref[...] = jnp.zeros_like(dsc_ref)
            dsh_ref[...] = jnp.zeros_like(dsh_ref)

        xv = x_ref[...]
        dhv = dh_ref[...].astype(F32)
        wv = w_ref[...]
        r = lax.rsqrt(jnp.mean(xv * xv, axis=-1, keepdims=True) + EPS)
        xh = xv * r
        n = xh * wv
        dsh_ref[...] += jnp.sum(dhv, axis=0, keepdims=True)
        dsc_ref[...] += jnp.sum(dhv * n, axis=0, keepdims=True)
        dn = dhv * (1.0 + sc_ref[...])
        dw_ref[...] += jnp.sum(dn * xh, axis=0, keepdims=True)
        dxh = dn * wv
        dx = r * (dxh - xh * jnp.mean(dxh * xh, axis=-1, keepdims=True))
        if has_res:
            dx = dx + dres_ref[...]
        dx_ref[...] = dx.astype(out_dtype)

    row = pl.BlockSpec((1, width), lambda i: (0, 0))
    blk = pl.BlockSpec((tm, width), lambda i: (i, 0))
    in_specs = [pl.BlockSpec((tm, width), lambda i: (i, col)), row, row, blk]
    args = [x, w, sc, dh]
    if has_res:
        in_specs.append(blk)
        args.append(dres)
    return pl.pallas_call(
        body, grid=(t // tm,), in_specs=in_specs, out_specs=[blk, row, row, row],
        out_shape=[SDS((t, width), out_dtype), SDS((1, width), F32), SDS((1, width), F32), SDS((1, width), F32)],
        name=name, compiler_params=_cp(("arbitrary",)))(*args)


def _rope128(x, tc, ts1, ts2):
    return x * tc + pltpu.roll(x, 96, 1) * ts1 + pltpu.roll(x, 32, 1) * ts2


def _rope128_t(d, tc, ts1, ts2):
    return d * tc + pltpu.roll(d * ts1, 32, 1) + pltpu.roll(d * ts2, 96, 1)


def _rope_q(q_raw, tc, ts1, ts2, transpose, out_dtype, name):
    t = q_raw.shape[0]
    tm = _row_tile(t)

    def body(q_ref, tc_ref, s1_ref, s2_ref, o_ref):
        fn = _rope128_t if transpose else _rope128
        for h in range(HEADS):
            base = h * D_QK
            o_ref[:, base:base + LANES] = q_ref[:, base:base + LANES].astype(out_dtype)
            x = q_ref[:, base + LANES:base + D_QK].astype(F32)
            o_ref[:, base + LANES:base + D_QK] = fn(x, tc_ref[...], s1_ref[...], s2_ref[...]).astype(out_dtype)

    blk = pl.BlockSpec((tm, HEADS * D_QK), lambda i: (i, 0))
    tab = pl.BlockSpec((tm, LANES), lambda i: (i, 0))
    return pl.pallas_call(
        body, grid=(t // tm,), in_specs=[blk, tab, tab, tab], out_specs=blk,
        out_shape=SDS((t, HEADS * D_QK), out_dtype), name=name, compiler_params=_cp(("parallel",)))(q_raw, tc, ts1, ts2)


def _k_assemble(kv_raw, p_small, tc, ts1, ts2, name):
    t = kv_raw.shape[0]
    tm = _row_tile(t)

    def body(kn_ref, ps_ref, tc_ref, s1_ref, s2_ref, o_ref):
        kpe = _rope128(ps_ref[...], tc_ref[...], s1_ref[...], s2_ref[...]).astype(BF16)
        for h in range(HEADS):
            o_ref[:, h * D_QK:h * D_QK + LANES] = kn_ref[:, h * LANES:(h + 1) * LANES].astype(BF16)
            o_ref[:, h * D_QK + LANES:(h + 1) * D_QK] = kpe

    tab = pl.BlockSpec((tm, LANES), lambda i: (i, 0))
    return pl.pallas_call(
        body, grid=(t // tm,),
        in_specs=[pl.BlockSpec((tm, HEADS * LANES), lambda i: (i, 0)), tab, tab, tab, tab],
        out_specs=pl.BlockSpec((tm, HEADS * D_QK), lambda i: (i, 0)),
        out_shape=SDS((t, HEADS * D_QK), BF16), name=name, compiler_params=_cp(("parallel",)))(kv_raw, p_small, tc, ts1, ts2)


def _k_assemble_bwd(dk, dv, tc, ts1, ts2, name):
    t = dk.shape[0]
    tm = _row_tile(t)

    def body(dk_ref, dv_ref, tc_ref, s1_ref, s2_ref, o_ref, pe_ref):
        acc = jnp.zeros((tm, LANES), F32)
        for h in range(HEADS):
            o_ref[:, h * LANES:(h + 1) * LANES] = dk_ref[:, h * D_QK:h * D_QK + LANES].astype(BF16)
            acc = acc + dk_ref[:, h * D_QK + LANES:(h + 1) * D_QK].astype(F32)
        o_ref[:, HEADS * LANES:] = dv_ref[...].astype(BF16)
        pe_ref[...] = _rope128_t(acc, tc_ref[...], s1_ref[...], s2_ref[...])

    tab = pl.BlockSpec((tm, LANES), lambda i: (i, 0))
    return pl.pallas_call(
        body, grid=(t // tm,),
        in_specs=[pl.BlockSpec((tm, HEADS * D_QK), lambda i: (i, 0)), pl.BlockSpec((tm, HEADS * LANES), lambda i: (i, 0)),
                  tab, tab, tab],
        out_specs=[pl.BlockSpec((tm, 2 * HEADS * LANES), lambda i: (i, 0)), tab],
        out_shape=[SDS((t, 2 * HEADS * LANES), BF16), SDS((t, LANES), F32)],
        name=name, compiler_params=_cp(("parallel",)))(dk, dv, tc, ts1, ts2)


def _attn_tile(t):
    return _pick(t, (256, 128, 64))


def _attn_fwd(q, k, v, v_off, name):
    t = q.shape[0]
    tq = _attn_tile(t)
    scale = (D_NOPE + D_ROPE) ** -0.5

    def body(q_ref, k_ref, v_ref, o_ref, lse_ref):
        for i in range(t // tq):
            n_k = (i + 1) * tq
            s = _dot_nt(q_ref[i * tq:(i + 1) * tq, :], k_ref[0:n_k, :]) * scale
            row = lax.broadcasted_iota(jnp.int32, (tq, n_k), 0) + i * tq
            colv = lax.broadcasted_iota(jnp.int32, (tq, n_k), 1)
            s = jnp.where(colv <= row, s, -jnp.inf)
            m = jnp.max(s, axis=-1, keepdims=True)
            p = jnp.exp(s - m)
            l = jnp.sum(p, axis=-1, keepdims=True)
            o = _dot(p, v_ref[0:n_k, :]) / l
            o_ref[i * tq:(i + 1) * tq, :] = o.astype(BF16)
            lse_ref[0, i * tq:(i + 1) * tq, :] = m + jnp.log(l)

    return pl.pallas_call(
        body, grid=(HEADS,),
        in_specs=[pl.BlockSpec((t, D_QK), lambda h: (0, h)), pl.BlockSpec((t, D_QK), lambda h: (0, h)),
                  pl.BlockSpec((t, D_V), lambda h: (0, v_off + h))],
        out_specs=[pl.BlockSpec((t, D_V), lambda h: (0, h)), pl.BlockSpec((1, t, 1), lambda h: (h, 0, 0))],
        out_shape=[SDS((t, HEADS * D_V), BF16), SDS((HEADS, t, 1), F32)],
        name=name, compiler_params=_cp(("parallel",)))(q, k, v)


def _attn_bwd(q, k, v, v_off, o, lse, do, name):
    t = q.shape[0]
    tq = _attn_tile(t)
    scale = (D_NOPE + D_ROPE) ** -0.5

    def body(q_ref, k_ref, v_ref, o_ref, lse_ref, do_ref, dq_ref, dk_ref, dv_ref):
        dk_ref[...] = jnp.zeros_like(dk_ref)
        dv_ref[...] = jnp.zeros_like(dv_ref)
        for i in range(t // tq):
            n_k = (i + 1) * tq
            rows = slice(i * tq, (i + 1) * tq)
            qi = q_ref[rows, :]
            doi = do_ref[rows, :].astype(F32)
            s = _dot_nt(qi, k_ref[0:n_k, :]) * scale
            row = lax.broadcasted_iota(jnp.int32, (tq, n_k), 0) + i * tq
            colv = lax.broadcasted_iota(jnp.int32, (tq, n_k), 1)
            p = jnp.where(colv <= row, jnp.exp(s - lse_ref[0, rows, :]), 0.0)
            dp = _dot_nt(doi, v_ref[0:n_k, :])
            delta = jnp.sum(doi * o_ref[rows, :].astype(F32), axis=-1, keepdims=True)
            ds = p * (dp - delta) * scale
            dq_ref[rows, :] = _dot(ds, k_ref[0:n_k, :])
            dk_ref[0:n_k, :] += _dot_tn(ds, qi)
            dv_ref[0:n_k, :] += _dot_tn(p, doi)

    qk_spec = pl.BlockSpec((t, D_QK), lambda h: (0, h))
    v_spec = pl.BlockSpec((t, D_V), lambda h: (0, h))
    return pl.pallas_call(
        body, grid=(HEADS,),
        in_specs=[qk_spec, qk_spec, pl.BlockSpec((t, D_V), lambda h: (0, v_off + h)), v_spec,
                  pl.BlockSpec((1, t, 1), lambda h: (h, 0, 0)), v_spec],
        out_specs=[qk_spec, qk_spec, v_spec],
        out_shape=[SDS((t, HEADS * D_QK), F32), SDS((t, HEADS * D_QK), F32), SDS((t, HEADS * D_V), F32)],
        name=name, compiler_params=_cp(("parallel",)))(q, k, v, o, lse, do)


CONV_COLS = 256


def _conv_pre(u, w_ref, rowi):
    acc = u * w_ref[CONV_WIDTH - 1:CONV_WIDTH, :]
    for sft in range(1, CONV_WIDTH):
        shifted = jnp.where(rowi >= sft, pltpu.roll(u, sft, 0), 0.0)
        acc = acc + shifted * w_ref[CONV_WIDTH - 1 - sft:CONV_WIDTH - sft, :]
    return acc


def _conv_fwd(p_main, conv_w, name):
    t = p_main.shape[0]
    off = 2 * Q_LORA // CONV_COLS

    def body(u_ref, w_ref, y_ref):
        u = u_ref[...]
        rowi = lax.broadcasted_iota(jnp.int32, u.shape, 0)
        pre = _conv_pre(u, w_ref, rowi)
        y_ref[...] = pre * _sigmoid(pre)

    return pl.pallas_call(
        body, grid=(3 * GDN_W // CONV_COLS,),
        in_specs=[pl.BlockSpec((t, CONV_COLS), lambda j: (0, off + j)), pl.BlockSpec((CONV_WIDTH, CONV_COLS), lambda j: (0, j))],
        out_specs=pl.BlockSpec((t, CONV_COLS), lambda j: (0, j)), out_shape=SDS((t, 3 * GDN_W), F32),
        name=name, compiler_params=_cp(("parallel",)))(p_main, conv_w)


def _conv_bwd(p_main, conv_w, dyc, name):
    t = p_main.shape[0]
    off = 2 * Q_LORA // CONV_COLS

    def body(u_ref, w_ref, dy_ref, du_ref, dw_ref):
        u = u_ref[...]
        rowi = lax.broadcasted_iota(jnp.int32, u.shape, 0)
        pre = _conv_pre(u, w_ref, rowi)
        sg = _sigmoid(pre)
        dpre = dy_ref[...] * sg * (1.0 + pre * (1.0 - sg))
        du = dpre * w_ref[CONV_WIDTH - 1:CONV_WIDTH, :]
        dw_ref[CONV_WIDTH - 1:CONV_WIDTH, :] = jnp.sum(dpre * u, axis=0, keepdims=True)
        for sft in range(1, CONV_WIDTH):
            back = jnp.where(rowi < t - sft, pltpu.roll(dpre, t - sft, 0), 0.0)
            du = du + back * w_ref[CONV_WIDTH - 1 - sft:CONV_WIDTH - sft, :]
            shifted = jnp.where(rowi >= sft, pltpu.roll(u, sft, 0), 0.0)
            dw_ref[CONV_WIDTH - 1 - sft:CONV_WIDTH - sft, :] = jnp.sum(dpre * shifted, axis=0, keepdims=True)
        du_ref[...] = du.astype(BF16)

    blk = pl.BlockSpec((t, CONV_COLS), lambda j: (0, j))
    wblk = pl.BlockSpec((CONV_WIDTH, CONV_COLS), lambda j: (0, j))
    return pl.pallas_call(
        body, grid=(3 * GDN_W // CONV_COLS,),
        in_specs=[pl.BlockSpec((t, CONV_COLS), lambda j: (0, off + j)), wblk, blk],
        out_specs=[blk, wblk], out_shape=[SDS((t, 3 * GDN_W), BF16), SDS((CONV_WIDTH, 3 * GDN_W), F32)],
        name=name, compiler_params=_cp(("parallel",)))(p_main, conv_w, dyc)


B_LO, A_LO, A_HI = D_ROPE, D_ROPE + HEADS, D_ROPE + 2 * HEADS


def _softplus(z):
    e = jnp.exp(-jnp.abs(z))
    log1p = jnp.where(e < 0.01, e * (1.0 - e * (0.5 - e * (1.0 / 3.0))), jnp.log(1.0 + e))
    return jnp.maximum(z, 0.0) + log1p


def _gdn_gates(p_small, a_row, dt_row, name):
    t = p_small.shape[0]

    def body(ps_ref, a_ref, dt_ref, g_ref, gc_ref):
        x = ps_ref[...]
        lane = lax.broadcasted_iota(jnp.int32, x.shape, 1)
        is_g = (lane >= A_LO) & (lane < A_HI)
        g = jnp.where(is_g, -jnp.exp(a_ref[...]) * _softplus(x + dt_ref[...]), 0.0)
        g_ref[...] = jnp.where(is_g, g, _sigmoid(x))
        pos = lax.broadcasted_iota(jnp.int32, x.shape, 0) % CHUNK
        acc = g
        sft = 1
        while sft < CHUNK:
            acc = acc + jnp.where(pos >= sft, pltpu.roll(acc, sft, 0), 0.0)
            sft *= 2
        gc_ref[...] = acc

    full = pl.BlockSpec((t, LANES), lambda i: (0, 0))
    row = pl.BlockSpec((1, LANES), lambda i: (0, 0))
    return pl.pallas_call(
        body, grid=(1,), in_specs=[full, row, row], out_specs=[full, full],
        out_shape=[SDS((t, LANES), F32), SDS((t, LANES), F32)], name=name,
        compiler_params=_cp(("arbitrary",)))(p_small, a_row, dt_row)


def _gdn_gates_bwd(p_small, a_row, dt_row, gates, dgates, dkpe, name):
    t = p_small.shape[0]

    def body(ps_ref, a_ref, dt_ref, g_ref, db_ref, dkpe_ref, dp_ref, da_ref, ddt_ref):
        x = ps_ref[...]
        lane = lax.broadcasted_iota(jnp.int32, x.shape, 1)
        is_g = (lane >= A_LO) & (lane < A_HI)
        is_b = (lane >= B_LO) & (lane < A_LO)
        pos = lax.broadcasted_iota(jnp.int32, x.shape, 0) % CHUNK
        acc = jnp.where(is_g, db_ref[...], 0.0)
        sft = 1
        while sft < CHUNK:
            acc = acc + jnp.where(pos < CHUNK - sft, pltpu.roll(acc, t - sft, 0), 0.0)
            sft *= 2
        dg = acc
        gv = g_ref[...]
        dz = jnp.where(is_g, dg * (-jnp.exp(a_ref[...])) * _sigmoid(x + dt_ref[...]), 0.0)
        da_ref[...] = jnp.sum(jnp.where(is_g, dg * gv, 0.0), axis=0, keepdims=True)
        ddt_ref[...] = jnp.sum(dz, axis=0, keepdims=True)
        dlb = jnp.where(is_b, db_ref[...] * gv * (1.0 - gv), 0.0)
        dp_ref[...] = (jnp.where(lane < D_ROPE, dkpe_ref[...], 0.0) + dlb + dz).astype(BF16)

    full = pl.BlockSpec((t, LANES), lambda i: (0, 0))
    row = pl.BlockSpec((1, LANES), lambda i: (0, 0))
    return pl.pallas_call(
        body, grid=(1,), in_specs=[full, row, row, full, full, full], out_specs=[full, row, row],
        out_shape=[SDS((t, LANES), BF16), SDS((1, LANES), F32), SDS((1, LANES), F32)], name=name,
        compiler_params=_cp(("arbitrary",)))(p_small, a_row, dt_row, gates, dgates, dkpe)


def _tri_inv(l, eye):
    x = eye - l
    p = _bdot(l, l, exact=True)
    steps = int(math.log2(CHUNK)) - 1
    for s in range(steps):
        x = x + _bdot(x, p, exact=True)
        if s < steps - 1:
            p = _bdot(p, p, exact=True)
    return x


def _l2n(x3):
    r = lax.rsqrt(jnp.sum(x3 * x3, axis=-1, keepdims=True) + EPS)
    return x3 * r, r


def _head_col(a_ref, lane_lo, n):
    a = a_ref[...]
    lane = lax.broadcasted_iota(jnp.int32, a.shape, 1)
    col = jnp.sum(jnp.where(lane == lane_lo + pl.program_id(0), a, 0.0), axis=-1, keepdims=True)
    return col.reshape(n, CHUNK, 1)


def _gdn_common(q3, k3, v3, b, gc):
    n = q3.shape[0]
    ri = lax.broadcasted_iota(jnp.int32, (n, CHUNK, CHUNK), 1)
    ci = lax.broadcasted_iota(jnp.int32, (n, CHUNK, CHUNK), 2)
    lower, strict = ri >= ci, ri > ci
    eye = (ri == ci).astype(F32)
    gr = jnp.sum(gc * eye, axis=1, keepdims=True)
    qh, rq = _l2n(q3)
    qn = qh * (D_V ** -0.5)
    kn, rk = _l2n(k3)
    dec = jnp.where(lower, jnp.exp(jnp.where(lower, gc - gr, 0.0)), 0.0)
    kb = kn * b
    mm = _bdot_nt(kb, kn)
    tinv = _tri_inv(jnp.where(strict, mm * dec, 0.0), eye)
    gam = jnp.exp(gc)
    u = _bdot(tinv, v3 * b, exact=True)
    w = _bdot(tinv, kb * gam, exact=True)
    qk = _bdot_nt(qn, kn)
    aqk = jnp.where(lower, qk * dec, 0.0)
    gl = gc[:, CHUNK - 1:CHUNK, :]
    kdf = jnp.exp(gl - gc)
    return dict(ri=ri, ci=ci, lower=lower, strict=strict, eye=eye, qh=qh, rq=rq, qn=qn, kn=kn, rk=rk, dec=dec, kb=kb,
                mm=mm, gam=gam, u=u, w=w, qk=qk, aqk=aqk, gl=gl, kdf=kdf, kd=kn * kdf, gr=gr, tinv=tinv)


def _gdn_fwd(yc, p_main, gates, gcum, gn, name):
    t = yc.shape[0]
    n = t // CHUNK
    z_off = (2 * Q_LORA + 3 * GDN_W) // D_V

    def body(q_ref, k_ref, v_ref, z_ref, gt_ref, gcum_ref, gn_ref, o_ref, g_ref, s_ref, u_s, w_s, qg_s, kd_s, a_s, e_s):
        c = _gdn_common(q_ref[...].reshape(n, CHUNK, D_V), k_ref[...].reshape(n, CHUNK, D_V),
                        v_ref[...].reshape(n, CHUNK, D_V), _head_col(gt_ref, B_LO, n), _head_col(gcum_ref, A_LO, n))
        u_s[...] = c["u"]
        w_s[...] = c["w"]
        qg_s[...] = c["qn"] * c["gam"]
        kd_s[...] = c["kd"]
        a_s[...] = c["aqk"]
        e_s[...] = jnp.broadcast_to(jnp.exp(c["gl"]), (n, 1, D_V))

        def step(i, s):
            s_ref[0, i] = s
            v_new = u_s[i] - _dot(w_s[i], s)
            o = _dot(qg_s[i], s) + _dot(a_s[i], v_new)
            o_ref[pl.ds(pl.multiple_of(i * CHUNK, CHUNK), CHUNK), :] = o
            return s * e_s[i] + _dot_tn(kd_s[i], v_new)

        lax.fori_loop(0, n, step, jnp.zeros((D_V, D_V), F32))
        o = o_ref[...]
        zz = z_ref[...]
        on = o * lax.rsqrt(jnp.mean(o * o, axis=-1, keepdims=True) + EPS) * gn_ref[...]
        g_ref[...] = (on * zz * _sigmoid(zz)).astype(BF16)

    col = lambda off: pl.BlockSpec((t, D_V), lambda h: (0, off + h))
    lanes = pl.BlockSpec((t, LANES), lambda h: (0, 0))
    big = pltpu.VMEM((n, CHUNK, D_V), F32)
    return pl.pallas_call(
        body, grid=(HEADS,),
        in_specs=[col(0), col(HEADS), col(2 * HEADS), col(z_off), lanes, lanes, pl.BlockSpec((1, D_V), lambda h: (0, 0))],
        out_specs=[col(0), col(0), pl.BlockSpec((1, n, D_V, D_V), lambda h: (h, 0, 0, 0))],
        out_shape=[SDS((t, GDN_W), F32), SDS((t, GDN_W), BF16), SDS((HEADS, n, D_V, D_V), F32)],
        scratch_shapes=[big, big, big, big, pltpu.VMEM((n, CHUNK, CHUNK), F32), pltpu.VMEM((n, 1, D_V), F32)],
        name=name, compiler_params=_cp(("parallel",)))(yc, yc, yc, p_main, gates, gcum, gn)


def _gdn_bwd(yc, p_main, gates, gcum, gn, o_raw, states, dgated, name):
    t = yc.shape[0]
    n = t // CHUNK
    z_off = (2 * Q_LORA + 3 * GDN_W) // D_V

    def body(q_ref, k_ref, v_ref, z_ref, gt_ref, gcum_ref, gn_ref, o_ref, s_ref, dg_ref,
             dq_ref, dk_ref, dv_ref, dz_ref, dgt_ref, dgn_ref,
             u_s, w_s, qg_s, kd_s, at_s, e_s, do_s, du_s, dw_s, dqg_s, dkd_s, da_s, dat_s, dgs_s):
        @pl.when(pl.program_id(0) == 0)
        def _():
            dgn_ref[...] = jnp.zeros_like(dgn_ref)
            dgt_ref[...] = jnp.zeros_like(dgt_ref)

        o = o_ref[...]
        zz = z_ref[...]
        dgv = dg_ref[...]
        gnv = gn_ref[...]
        r = lax.rsqrt(jnp.mean(o * o, axis=-1, keepdims=True) + EPS)
        oh = o * r
        sg = _sigmoid(zz)
        don = dgv * zz * sg
        dz_ref[...] = (dgv * oh * gnv * sg * (1.0 + zz * (1.0 - sg))).astype(BF16)
        dgn_ref[...] += jnp.sum(don * oh, axis=0, keepdims=True)
        doh = don * gnv
        do_s[...] = (r * (doh - oh * jnp.mean(doh * oh, axis=-1, keepdims=True))).reshape(n, CHUNK, D_V)

        q3 = q_ref[...].reshape(n, CHUNK, D_V)
        k3 = k_ref[...].reshape(n, CHUNK, D_V)
        v3 = v_ref[...].reshape(n, CHUNK, D_V)
        b, gc = _head_col(gt_ref, B_LO, n), _head_col(gcum_ref, A_LO, n)
        c = _gdn_common(q3, k3, v3, b, gc)
        gr = c["gr"]
        ri, ci = c["ri"], c["ci"]
        upper, sup = ci >= ri, ci > ri
        dect = jnp.where(upper, jnp.exp(jnp.where(upper, gr - gc, 0.0)), 0.0)
        tinv_t = lax.dot_general(c["eye"], c["tinv"], (((2,), (2,)), ((0,), (0,))), precision=lax.Precision.HIGHEST,
                                 preferred_element_type=F32)
        u_s[...] = c["u"]
        w_s[...] = c["w"]
        qg_s[...] = c["qn"] * c["gam"]
        kd_s[...] = c["kd"]
        at_s[...] = jnp.where(upper, _bdot_nt(c["kn"], c["qn"]) * dect, 0.0)
        e_s[...] = jnp.broadcast_to(jnp.exp(c["gl"]), (n, 1, D_V))

        def step(j, ds):
            i = n - 1 - j
            s = s_ref[0, i]
            do_i = do_s[i]
            v_new = u_s[i] - _dot(w_s[i], s)
            dvn = _dot(at_s[i], do_i) + _dot(kd_s[i], ds)
            da_s[i] = _dot_nt(do_i, v_new)
            dat_s[i] = _dot_nt(v_new, do_i)
            dqg_s[i] = _dot_nt(do_i, s)
            dw_s[i] = -_dot_nt(dvn, s)
            dkd_s[i] = _dot_nt(v_new, ds)
            du_s[i] = dvn
            dgs_s[i] = jnp.broadcast_to(jnp.sum(jnp.sum(s * ds, axis=1, keepdims=True), axis=0, keepdims=True), (1, D_V))
            return _dot_tn(qg_s[i], do_i) + e_s[i] * ds - _dot_tn(w_s[i], dvn)

        lax.fori_loop(0, n, step, jnp.zeros((D_V, D_V), F32))

        du, dw, dqg, dkd = du_s[...], dw_s[...], dqg_s[...], dkd_s[...]
        lower, strict, dec = c["lower"], c["strict"], c["dec"]
        kn, kb, qn, gam, kdf = c["kn"], c["kb"], c["qn"], c["gam"], c["kdf"]
        drv = _bdot(tinv_t, du, exact=True)
        drk = _bdot(tinv_t, dw, exact=True)
        dl = jnp.where(strict, -(_bdot_nt(drv, c["u"]) + _bdot_nt(drk, c["w"])), 0.0)
        dlt = jnp.where(sup, -(_bdot_nt(c["u"], drv) + _bdot_nt(c["w"], drk)), 0.0)
        da = jnp.where(lower, da_s[...], 0.0)
        dat = jnp.where(upper, dat_s[...], 0.0)
        e = (dl * c["mm"] + da * c["qk"]) * dec
        col_sums = jnp.sum(e, axis=1, keepdims=True)
        dgc = jnp.sum(e, axis=2, keepdims=True) - jnp.sum(col_sums * c["eye"], axis=2, keepdims=True)
        dkb = _bdot(dl * dec, kn) + gam * drk
        dkn = _bdot(dlt * dect, kb) + _bdot(dat * dect, qn) + b * dkb + dkd * kdf
        dqn = _bdot(da * dec, kn) + gam * dqg
        dgam = jnp.sum(drk * kb, axis=-1, keepdims=True) + jnp.sum(dqg * qn, axis=-1, keepdims=True)
        dbeta = jnp.sum(dkb * kn, axis=-1, keepdims=True) + jnp.sum(drv * v3, axis=-1, keepdims=True)
        dv_ref[...] = (b * drv).reshape(t, D_V)
        ee = jnp.sum(dkd * kn, axis=-1, keepdims=True) * kdf
        dgc = dgc + dgam * gam - ee
        rowc = lax.broadcasted_iota(jnp.int32, (n, CHUNK, 1), 1)
        tail = jnp.sum(ee, axis=1, keepdims=True) + dgs_s[...][:, :, 0:1] * jnp.exp(c["gl"])
        dgc = dgc + jnp.where(rowc == CHUNK - 1, tail, 0.0)
        lane = lax.broadcasted_iota(jnp.int32, (t, LANES), 1)
        head = pl.program_id(0)
        dgt_ref[...] += (jnp.where(lane == B_LO + head, dbeta.reshape(t, 1), 0.0)
                         + jnp.where(lane == A_LO + head, dgc.reshape(t, 1), 0.0))
        sc = D_V ** -0.5
        qh, rq, rk = c["qh"], c["rq"], c["rk"]
        dq_ref[...] = (rq * (sc * dqn - qh * jnp.sum(sc * dqn * qh, axis=-1, keepdims=True))).reshape(t, D_V)
        dk_ref[...] = (rk * (dkn - kn * jnp.sum(dkn * kn, axis=-1, keepdims=True))).reshape(t, D_V)

    once = pl.Buffered(1)
    col = lambda off: pl.BlockSpec((t, D_V), lambda h: (0, off + h), pipeline_mode=once)
    out_col = pl.BlockSpec((t, D_V), lambda h: (0, h))
    lanes = pl.BlockSpec((t, LANES), lambda h: (0, 0))
    row = pl.BlockSpec((1, D_V), lambda h: (0, 0))
    big = pltpu.VMEM((n, CHUNK, D_V), F32)
    sq = pltpu.VMEM((n, CHUNK, CHUNK), F32)
    small = pltpu.VMEM((n, 1, D_V), F32)
    return pl.pallas_call(
        body, grid=(HEADS,),
        in_specs=[col(0), col(HEADS), col(2 * HEADS), col(z_off), lanes, lanes, row, col(0),
                  pl.BlockSpec((1, n, D_V, D_V), lambda h: (h, 0, 0, 0), pipeline_mode=once), col(0)],
        out_specs=[out_col, out_col, out_col, out_col, lanes, row],
        out_shape=[SDS((t, GDN_W), F32), SDS((t, GDN_W), F32), SDS((t, GDN_W), F32), SDS((t, GDN_W), BF16),
                   SDS((t, LANES), F32), SDS((1, D_V), F32)],
        scratch_shapes=[big, big, big, big, sq, small, big, big, big, big, big, sq, sq, small],
        name=name, compiler_params=_cp(("arbitrary",)))(yc, yc, yc, p_main, gates, gcum, gn, o_raw, states, dgated)


def _col_tile(d):
    return _pick(d, (512, 256, 128))


def _mix_fwd(y_a, y_b, p_main, name):
    t, d = y_a.shape
    tm, cw = _row_tile(t), _col_tile(d)
    off_a, off_b = MAIN_FIXED // cw, (MAIN_FIXED + d) // cw

    def body(ya_ref, yb_ref, ga_ref, gb_ref, u_ref):
        u_ref[...] = (_sigmoid(ga_ref[...]) * ya_ref[...] + _sigmoid(gb_ref[...]) * yb_ref[...]).astype(BF16)

    blk = pl.BlockSpec((tm, cw), lambda i, j: (i, j))
    return pl.pallas_call(
        body, grid=(t // tm, d // cw),
        in_specs=[blk, blk, pl.BlockSpec((tm, cw), lambda i, j: (i, off_a + j)), pl.BlockSpec((tm, cw), lambda i, j: (i, off_b + j))],
        out_specs=blk, out_shape=SDS((t, d), BF16), name=name,
        compiler_params=_cp(("parallel", "parallel")))(y_a, y_b, p_main, p_main)


def _mix_bwd(du, y_a, y_b, p_main, name):
    t, d = y_a.shape
    tm, cw = _row_tile(t), _col_tile(d)
    off_a, off_b = MAIN_FIXED // cw, (MAIN_FIXED + d) // cw
    nb = d // cw

    def body(du_ref, ya_ref, yb_ref, ga_ref, gb_ref, dya_ref, dyb_ref, dla_ref, dlb_ref):
        duv = du_ref[...]
        ga, gb = _sigmoid(ga_ref[...]), _sigmoid(gb_ref[...])
        dya_ref[...] = (duv * ga).astype(BF16)
        dyb_ref[...] = (duv * gb).astype(BF16)
        dla_ref[...] = (duv * ya_ref[...] * ga * (1.0 - ga)).astype(BF16)
        dlb_ref[...] = (duv * yb_ref[...] * gb * (1.0 - gb)).astype(BF16)

    blk = pl.BlockSpec((tm, cw), lambda i, j: (i, j))
    outs = pl.pallas_call(
        body, grid=(t // tm, nb),
        in_specs=[blk, blk, blk, pl.BlockSpec((tm, cw), lambda i, j: (i, off_a + j)),
                  pl.BlockSpec((tm, cw), lambda i, j: (i, off_b + j))],
        out_specs=[blk, blk, blk, blk],
        out_shape=[SDS((t, d), BF16), SDS((t, d), BF16), SDS((t, d), BF16), SDS((t, d), BF16)], name=name,
        compiler_params=_cp(("parallel", "parallel")))(du, y_a, y_b, p_main, p_main)
    return outs


def _gate_res(x, y, gt, name):
    t, d = x.shape
    tm = _row_tile(t)

    def body(x_ref, y_ref, g_ref, o_ref):
        o_ref[...] = x_ref[...] + g_ref[...] * y_ref[...]

    blk = pl.BlockSpec((tm, d), lambda i: (i, 0))
    return pl.pallas_call(
        body, grid=(t // tm,), in_specs=[blk, blk, pl.BlockSpec((1, d), lambda i: (0, 0))], out_specs=blk,
        out_shape=SDS((t, d), F32), name=name, compiler_params=_cp(("parallel",)))(x, y, gt)


def _gate_res_bwd(dx, y, gt, name):
    t, d = dx.shape
    tm = _row_tile(t)

    def body(dx_ref, y_ref, g_ref, dg_ref, dy_ref):
        @pl.when(pl.program_id(0) == 0)
        def _():
            dg_ref[...] = jnp.zeros_like(dg_ref)

        dxv = dx_ref[...]
        dg_ref[...] += jnp.sum(dxv * y_ref[...], axis=0, keepdims=True)
        dy_ref[...] = (dxv * g_ref[...]).astype(BF16)

    blk = pl.BlockSpec((tm, d), lambda i: (i, 0))
    row = pl.BlockSpec((1, d), lambda i: (0, 0))
    return pl.pallas_call(
        body, grid=(t // tm,), in_specs=[blk, blk, row], out_specs=[row, blk],
        out_shape=[SDS((1, d), F32), SDS((t, d), BF16)], name=name, compiler_params=_cp(("arbitrary",)))(dx, y, gt)


def _swiglu_fwd(gu, name):
    t, f2 = gu.shape
    f = f2 // 2
    tm, cw = _row_tile(t), _col_tile(f)
    nb = f // cw

    def body(g_ref, u_ref, o_ref):
        g = g_ref[...]
        o_ref[...] = (g * _sigmoid(g) * u_ref[...]).astype(BF16)

    return pl.pallas_call(
        body, grid=(t // tm, nb),
        in_specs=[pl.BlockSpec((tm, cw), lambda i, j: (i, j)), pl.BlockSpec((tm, cw), lambda i, j: (i, nb + j))],
        out_specs=pl.BlockSpec((tm, cw), lambda i, j: (i, j)), out_shape=SDS((t, f), BF16), name=name,
        compiler_params=_cp(("parallel", "parallel")))(gu, gu)


def _swiglu_bwd(gu, da, name):
    t, f2 = gu.shape
    f = f2 // 2
    tm, cw = _row_tile(t), _col_tile(f)
    nb = f // cw

    def body(g_ref, u_ref, da_ref, dg_ref, dup_ref):
        g = g_ref[...]
        dav = da_ref[...]
        sg = _sigmoid(g)
        dg_ref[...] = (dav * u_ref[...] * sg * (1.0 + g * (1.0 - sg))).astype(BF16)
        dup_ref[...] = (dav * g * sg).astype(BF16)

    blk = pl.BlockSpec((tm, cw), lambda i, j: (i, j))
    dg, dup = pl.pallas_call(
        body, grid=(t // tm, nb),
        in_specs=[blk, pl.BlockSpec((tm, cw), lambda i, j: (i, nb + j)), blk], out_specs=[blk, blk],
        out_shape=[SDS((t, f), BF16), SDS((t, f), BF16)], name=name,
        compiler_params=_cp(("parallel", "parallel")))(gu, gu, da)
    return dg, dup


def _loss_head(x, w, target, name):
    t, d = x.shape
    tm = _row_tile(t)

    def body(x_ref, w_ref, t_ref, l_ref, dx_ref, dw_ref):
        @pl.when(pl.program_id(0) == 0)
        def _():
            l_ref[...] = jnp.zeros_like(l_ref)
            dw_ref[...] = jnp.zeros_like(dw_ref)

        xv = x_ref[...]
        wv = w_ref[...]
        r = lax.rsqrt(jnp.mean(xv * xv, axis=-1, keepdims=True) + EPS)
        xh = xv * r
        err = xh * wv - t_ref[...]
        per_tok = jnp.mean(err * err, axis=-1, keepdims=True)
        l_ref[...] += 0.5 * jnp.sum(per_tok, axis=0, keepdims=True)
        dy = err * (1.0 / d)
        dw_ref[...] += jnp.sum(dy * xh, axis=0, keepdims=True)
        dxh = dy * wv
        dx_ref[...] = r * (dxh - xh * jnp.mean(dxh * xh, axis=-1, keepdims=True))

    blk = pl.BlockSpec((tm, d), lambda i: (i, 0))
    row = pl.BlockSpec((1, d), lambda i: (0, 0))
    return pl.pallas_call(
        body, grid=(t // tm,), in_specs=[blk, row, blk],
        out_specs=[pl.BlockSpec((1, LANES), lambda i: (0, 0)), blk, row],
        out_shape=[SDS((1, LANES), F32), SDS((t, d), F32), SDS((1, d), F32)], name=name,
        compiler_params=_cp(("arbitrary",)))(x, w, target)


def _adamw(w, g, m, v, tie, name):
    shape = w.shape
    per_layer = isinstance(g, (list, tuple))
    n_layers = shape[0] if (w.ndim == 3 and shape[1] % 8 == 0) else 1
    cols = shape[-1]
    rows = w.size // cols // n_layers
    w, m, v = (a.reshape(n_layers * rows, cols) for a in (w, m, v))
    if not per_layer:
        g = g.reshape(n_layers * rows, cols)
    lanes_padded = -(-cols // LANES) * LANES
    budget_rows = max(8, (24 * 1024 * 1024) // (lanes_padded * 4 * 18))
    tr = rows
    if rows > budget_rows:
        tr = _pick(rows, tuple(c for c in (1024, 512, 256, 128, 64, 32, 16, 8) if c <= budget_rows))
    nrb = rows // tr
    c1 = 1.0 / (1.0 - ADAM_B1 ** ADAM_STEP)
    c2 = 1.0 / (1.0 - ADAM_B2 ** ADAM_STEP)
    n_g = len(g) if per_layer else 1

    def body(*refs):
        w_ref, m_ref, v_ref = refs[:3]
        g_refs = refs[3:3 + n_g]
        outs = refs[4 + n_g:]
        gv = g_refs[0][...]
        for l in range(1, n_g):
            gv = jnp.where(pl.program_id(0) == l, g_refs[l][...], gv)
        mn = ADAM_B1 * m_ref[...] + (1.0 - ADAM_B1) * gv
        vn = ADAM_B2 * v_ref[...] + (1.0 - ADAM_B2) * (gv * gv)
        outs[0][...] = -ADAM_LR * ((mn * c1) / (jnp.sqrt(vn * c2) + ADAM_EPS) + ADAM_WD * w_ref[...])
        outs[1][...] = mn
        outs[2][...] = vn
        if per_layer:
            outs[3][...] = gv

    blk = pl.BlockSpec((tr, cols), lambda l, i: (l * nrb + i, 0))
    g_specs = [pl.BlockSpec((tr, cols), lambda l, i: (i, 0))] * n_g if per_layer else [blk]
    n_out = 4 if per_layer else 3
    outs = pl.pallas_call(
        body, grid=(n_layers, nrb), in_specs=[blk, blk, blk] + g_specs + [pl.BlockSpec((8, LANES), lambda l, i: (0, 0))],
        out_specs=[blk] * n_out, out_shape=[SDS(w.shape, F32)] * n_out, name=name,
        compiler_params=_cp(("parallel", "parallel")))(w, m, v, *(g if per_layer else [g]), tie)
    g_out = outs[3] if per_layer else g
    return (g_out.reshape(shape),) + tuple(o.reshape(shape) for o in outs[:3])


KPE_LO = 2 * Q_LORA
QKVZ_LO = KPE_LO + D_ROPE
BA_LO = QKVZ_LO + 4 * GDN_W
GATE_LO = BA_LO + 2 * HEADS


def _lay_w_in(w_in):
    d = w_in.shape[0]
    main = jnp.concatenate([w_in[:, :KPE_LO], w_in[:, QKVZ_LO:BA_LO], w_in[:, GATE_LO:]], axis=1)
    small = jnp.concatenate([w_in[:, KPE_LO:QKVZ_LO], w_in[:, BA_LO:GATE_LO],
                             jnp.zeros((d, LANES - D_ROPE - 2 * HEADS), w_in.dtype)], axis=1)
    return main, small


def _unlay_w_in(g_main, g_small):
    return jnp.concatenate([g_main[:, :KPE_LO], g_small[:, :D_ROPE], g_main[:, KPE_LO:KPE_LO + 4 * GDN_W],
                            g_small[:, D_ROPE:D_ROPE + 2 * HEADS], g_main[:, MAIN_FIXED:]], axis=1)


def _lay_w_uq(w_uq):
    r = w_uq.reshape(Q_LORA, HEADS, D_NOPE + D_ROPE)
    r = jnp.pad(r, ((0, 0), (0, 0), (0, D_QK - D_NOPE - D_ROPE)))
    return r.reshape(Q_LORA, HEADS * D_QK)


def _unlay_w_uq(g):
    rows = g.shape[0]
    return g.reshape(rows, HEADS, D_QK)[:, :, :D_NOPE + D_ROPE].reshape(rows, HEADS * (D_NOPE + D_ROPE))


def _lay_w_ukv(w_ukv):
    return w_ukv.reshape(KV_LORA, HEADS, 2, D_V).transpose(0, 2, 1, 3).reshape(KV_LORA, 2 * HEADS * D_V)


def _unlay_w_ukv(g):
    rows = g.shape[0]
    return g.reshape(rows, 2, HEADS, D_V).transpose(0, 2, 1, 3).reshape(rows, 2 * HEADS * D_V)


def _lane_row(vec, lo):
    return jnp.pad(vec.reshape(1, -1), ((0, 0), (lo, LANES - lo - vec.shape[0])))


def _rope_tables(positions):
    half = D_ROPE // 2
    inv_freq = 1.0 / (10000.0 ** (jnp.arange(0, D_ROPE, 2, dtype=F32) / D_ROPE))
    ang = positions.astype(F32)[:, None] * inv_freq
    cos, sin = jnp.cos(ang), jnp.sin(ang)
    t = positions.shape[0]
    zeros = lambda n: jnp.zeros((t, n), F32)
    tc = jnp.concatenate([cos, cos, zeros(LANES - D_ROPE)], axis=1)
    ts1 = jnp.concatenate([-sin, zeros(LANES - half)], axis=1)
    ts2 = jnp.concatenate([zeros(half), sin, zeros(LANES - D_ROPE)], axis=1)
    return tc, ts1, ts2


def _layer_fwd(x, mod, wt, tabs, tag, late_weights, after_gate_up=None):
    t, d = x.shape
    sh_a, sc_a, gt_a, sh_f, sc_f, gt_f = mod
    zero_l = jnp.zeros((1, Q_LORA), F32)
    s = dict(x=x)
    s["h1"] = _norm_fwd(x, 0, d, wt["norm_mix"], sc_a, sh_a, f"{tag}_norm_mix")
    s["p_main"] = _mm(s["h1"], wt["w_main"], name=f"{tag}_in_main")
    s["p_small"] = _mm(s["h1"], wt["w_small"], name=f"{tag}_in_small")
    s["cqn"] = _norm_fwd(s["p_main"], 0, Q_LORA, wt["q_a_norm"], zero_l, zero_l, f"{tag}_q_norm")
    s["ckvn"] = _norm_fwd(s["p_main"], 1, KV_LORA, wt["kv_a_norm"], zero_l, zero_l, f"{tag}_kv_norm")
    q_raw = _mm(s["cqn"], wt["w_uq"], name=f"{tag}_uq")
    s["kv_raw"] = _mm(s["ckvn"], wt["w_ukv"], name=f"{tag}_ukv")
    s["q_r"] = _rope_q(q_raw, *tabs, False, BF16, f"{tag}_rope_q")
    s["k_r"] = _k_assemble(s["kv_raw"], s["p_small"], *tabs, f"{tag}_k_asm")
    s["o"], s["lse"] = _attn_fwd(s["q_r"], s["k_r"], s["kv_raw"], HEADS, f"{tag}_attn")
    s["yc"] = _conv_fwd(s["p_main"], wt["conv_w"], f"{tag}_conv")
    s["gates"], s["gcum"] = _gdn_gates(s["p_small"], wt["a_row"], wt["dt_row"], f"{tag}_gates")
    s["o_raw"], s["gated"], s["states"] = _gdn_fwd(s["yc"], s["p_main"], s["gates"], s["gcum"], wt["gdn_norm"], f"{tag}_gdn")
    late, started = late_weights(s["gated"])
    wt = {**wt, **late}
    s["y_a"] = _mm(s["o"], wt["w_o_mla"], name=f"{tag}_o_mla")
    s["y_b"] = _mm(s["gated"], wt["w_o_gdn"], name=f"{tag}_o_gdn")
    s["u"] = _mix_fwd(s["y_a"], s["y_b"], s["p_main"], f"{tag}_mix")
    s["mixo"] = _mm(s["u"], wt["w_o"], name=f"{tag}_o")
    s["x2"] = _gate_res(x, s["mixo"], gt_a, f"{tag}_res_a")
    s["h2"] = _norm_fwd(s["x2"], 0, d, wt["norm_ffn"] + started, sc_f, sh_f, f"{tag}_norm_ffn")
    s["gu"] = _mm(s["h2"], wt["w_gate_up"], name=f"{tag}_gate_up")
    if after_gate_up is not None:
        gt_f = gt_f + after_gate_up(s["gu"])
    s["a"] = _swiglu_fwd(s["gu"], f"{tag}_swiglu")
    s["f"] = _mm(s["a"], wt["w_down"], name=f"{tag}_down")
    return _gate_res(s["x2"], s["f"], gt_f, f"{tag}_res_f"), s, wt


def _layer_bwd(dx3, s, mod, wt, tabs, tag, after_ffn=None, after_gdn=None):
    x = s["x"]
    t, d = x.shape
    sh_a, sc_a, gt_a, sh_f, sc_f, gt_f = mod
    zero_l = jnp.zeros((1, Q_LORA), F32)
    g = {}
    dgt_f, df = _gate_res_bwd(dx3, s["f"], gt_f, f"{tag}_b_res_f")
    da = _mm(df, wt["w_down"], tb=True, name=f"{tag}_b_down_x")
    g["w_down"] = _mm(s["a"].T, df, out_dtype=BF16, name=f"{tag}_b_down_w")
    dgate, dup = _swiglu_bwd(s["gu"], da, f"{tag}_b_swiglu")
    dgu = jnp.concatenate([dgate, dup], axis=1)
    dh2 = _mm(dgu, wt["w_gate_up"], tb=True, name=f"{tag}_b_gate_up_x")
    g["w_gate_up"] = _mm(s["h2"].T, dgu, out_dtype=BF16, name=f"{tag}_b_gate_up_w")
    if after_ffn is not None:
        gt_a = gt_a + after_ffn(g)
    dx2, g["norm_ffn"], dsc_f, dsh_f = _norm_bwd(s["x2"], 0, d, wt["norm_ffn"], sc_f, dh2, dx3, F32, f"{tag}_b_norm_ffn")
    dgt_a, dmixo = _gate_res_bwd(dx2, s["mixo"], gt_a, f"{tag}_b_res_a")
    du = _mm(dmixo, wt["w_o"], tb=True, name=f"{tag}_b_o_x")
    g["w_o"] = _mm(s["u"].T, dmixo, out_dtype=BF16, name=f"{tag}_b_o_w")
    dy_a, dy_b, dl_a, dl_b = _mix_bwd(du, s["y_a"], s["y_b"], s["p_main"], f"{tag}_b_mix")
    dgated = _mm(dy_b, wt["w_o_gdn"], tb=True, name=f"{tag}_b_o_gdn_x")
    g["w_o_gdn"] = _mm(s["gated"].T, dy_b, out_dtype=BF16, name=f"{tag}_b_o_gdn_w")
    dq_c, dk_c, dv_c, dz, dgates, g["gdn_norm"] = _gdn_bwd(
        s["yc"], s["p_main"], s["gates"], s["gcum"], wt["gdn_norm"], s["o_raw"], s["states"], dgated, f"{tag}_b_gdn")
    du_conv, g["conv_w"] = _conv_bwd(s["p_main"], wt["conv_w"], jnp.concatenate([dq_c, dk_c, dv_c], axis=1), f"{tag}_b_conv")
    do = _mm(dy_a, wt["w_o_mla"], tb=True, name=f"{tag}_b_o_mla_x")
    g["w_o_mla"] = _mm(s["o"].T, dy_a, out_dtype=BF16, name=f"{tag}_b_o_mla_w")
    dq_r, dk_r, dv = _attn_bwd(s["q_r"], s["k_r"], s["kv_raw"], HEADS, s["o"], s["lse"], do, f"{tag}_b_attn")
    q_a_norm = wt["q_a_norm"]
    if after_gdn is not None:
        q_a_norm = q_a_norm + after_gdn(du_conv)
    dq_raw = _rope_q(dq_r, *tabs, True, BF16, f"{tag}_b_rope_q")
    dkv_raw, dkpe = _k_assemble_bwd(dk_r, dv, *tabs, f"{tag}_b_k_asm")
    dcqn = _mm(dq_raw, wt["w_uq"], tb=True, name=f"{tag}_b_uq_x")
    g["w_uq"] = _mm(s["cqn"].T, dq_raw, out_dtype=BF16, name=f"{tag}_b_uq_w")
    dckvn = _mm(dkv_raw, wt["w_ukv"], tb=True, name=f"{tag}_b_ukv_x")
    g["w_ukv"] = _mm(s["ckvn"].T, dkv_raw, out_dtype=BF16, name=f"{tag}_b_ukv_w")
    dc_q, g["q_a_norm"], _, _ = _norm_bwd(s["p_main"], 0, Q_LORA, q_a_norm, zero_l, dcqn, None, BF16, f"{tag}_b_q_norm")
    dc_kv, g["kv_a_norm"], _, _ = _norm_bwd(s["p_main"], 1, KV_LORA, wt["kv_a_norm"], zero_l, dckvn, None, BF16,
                                            f"{tag}_b_kv_norm")
    dp_small, g["a_row"], g["dt_row"] = _gdn_gates_bwd(
        s["p_small"], wt["a_row"], wt["dt_row"], s["gates"], dgates, dkpe, f"{tag}_b_gates")
    dp_main = jnp.concatenate([dc_q, dc_kv, du_conv, dz, dl_a, dl_b], axis=1)
    h1t = s["h1"].T
    dh1 = _mm(dp_small, wt["w_small"], tb=True, name=f"{tag}_b_in_small_x")
    dh1 = _mm(dp_main, wt["w_main"], tb=True, acc_in=dh1, name=f"{tag}_b_in_main_x")
    g["w_main"] = _mm(h1t, dp_main, out_dtype=BF16, name=f"{tag}_b_in_main_w")
    g["w_small"] = _mm(h1t, dp_small, out_dtype=BF16, name=f"{tag}_b_in_small_w")
    dx, g["norm_mix"], dsc_a, dsh_a = _norm_bwd(x, 0, d, wt["norm_mix"], sc_a, dh1, dx2, F32, f"{tag}_b_norm_mix")
    return dx, (dsh_a, dsc_a, dgt_a, dsh_f, dsc_f, dgt_f), g


def _layer_weights(big, full, l):
    w_main, w_small = _lay_w_in(big["w_in"])
    return dict(
        w_main=w_main, w_small=w_small, w_uq=_lay_w_uq(big["w_uq"]), w_ukv=_lay_w_ukv(big["w_ukv"]),
        conv_w=full["conv_w"][l],
        norm_mix=full["norm_mix"][l][None], norm_ffn=full["norm_ffn"][l][None],
        q_a_norm=full["q_a_norm"][l][None], kv_a_norm=full["kv_a_norm"][l][None], gdn_norm=full["gdn_norm"][l][None],
        a_row=_lane_row(full["A_log"][l], A_LO), dt_row=_lane_row(full["dt_bias"][l], A_LO))


def _small_grads_ref_layout(g):
    return dict(
        conv_w=g["conv_w"], norm_mix=g["norm_mix"][0], norm_ffn=g["norm_ffn"][0], q_a_norm=g["q_a_norm"][0],
        kv_a_norm=g["kv_a_norm"][0], gdn_norm=g["gdn_norm"][0], A_log=g["a_row"][0, A_LO:A_HI],
        dt_bias=g["dt_row"][0, A_LO:A_HI])


def _local_step(x, mods, target, final_norm, full, positions):
    tabs = _rope_tables(positions)
    depth = len(mods)
    wts, saved = [None] * depth, []
    h = x
    for l in range(depth):
        early = _layer_weights({n: full[n][l] for n in FIRST_NEEDED}, full, l)
        h, s, wts[l] = _layer_fwd(h, mods[l], early, tabs, f"l{l}", lambda _, l=l: ({n: full[n][l] for n in LATER_NEEDED}, 0.0))
        saved.append(s)
    loss, dh, dfn = _loss_head(h, final_norm[None], target, "loss_head")
    dmods, grads = [None] * depth, [None] * depth
    for l in reversed(range(depth)):
        dh, dmods[l], grads[l] = _layer_bwd(dh, saved[l], mods[l], wts[l], tabs, f"l{l}")
    return loss, dh, dmods, grads, dfn[0]


HBM_SPEC = pl.BlockSpec(memory_space=pl.ANY)
VMEM_SPEC = pl.BlockSpec(memory_space=pltpu.VMEM)
N_CHIPS = 4
N_DEV = 8


def _me():
    return lax.axis_index("x"), lax.axis_index("y"), lax.axis_index("c")


def _flip(pos, f):
    mx, my, mc = pos
    fx, fy, fc = (f >> 2) & 1, (f >> 1) & 1, f & 1
    return ((mx + fx) % 2, (my + fy) % 2, (mc + fc) % 2)


def _all_gather_small(x, name):
    r, n = x.shape

    def body(x_ref, out_ref, send_sems, recv_sems, local_sem):
        me = _me()
        row = lambda p: 4 * p[0] + 2 * p[1] + p[2]
        mine = pltpu.make_async_copy(x_ref, out_ref.at[row(me)], local_sem)
        mine.start()

        def copy(f, origin):
            return pltpu.make_async_remote_copy(
                src_ref=x_ref, dst_ref=out_ref.at[row(origin)], send_sem=send_sems.at[f - 1], recv_sem=recv_sems.at[f - 1],
                device_id=_flip(me, f), device_id_type=MESH)

        sends = [copy(f, me) for f in range(1, N_DEV)]
        for cp in sends:
            cp.start()
        for f in range(1, N_DEV):
            copy(f, _flip(me, f)).wait_recv()
        for cp in sends:
            cp.wait_send()
        mine.wait()

    return pl.pallas_call(
        body, out_shape=SDS((N_DEV, r, n), x.dtype), in_specs=[VMEM_SPEC], out_specs=VMEM_SPEC,
        scratch_shapes=[pltpu.SemaphoreType.DMA((N_DEV - 1,)), pltpu.SemaphoreType.DMA((N_DEV - 1,)), pltpu.SemaphoreType.DMA],
        name=name, compiler_params=pltpu.CompilerParams(vmem_limit_bytes=VMEM_LIMIT))(x)


CHIP_FLIPS = (2, 4, 6)
SIBLING = 1


def _chip_of(pos):
    return 2 * pos[0] + pos[1]


def _sum_chips(p, got, chip, half, name):
    _, kh, ns = p.shape
    tr = _pick(kh, (256, 128, 64, 32, 16))
    nrb = kh // tr

    def body(c_ref, h_ref, a_ref, b_ref, o_ref):
        acc = a_ref[0].astype(F32)
        for j in range(3):
            acc = acc + b_ref[j].astype(F32)
        o_ref[...] = acc

    grid_spec = pltpu.PrefetchScalarGridSpec(
        num_scalar_prefetch=2, grid=(nrb,),
        in_specs=[pl.BlockSpec((1, tr, ns), lambda i, c, h: (c[0], i, 0)),
                  pl.BlockSpec((3, tr, ns), lambda i, c, h: (0, i, 0))],
        out_specs=pl.BlockSpec((tr, ns), lambda i, c, h: (h[0] * nrb + i, 0)))
    return pl.pallas_call(body, grid_spec=grid_spec, out_shape=SDS((2 * kh, ns), F32), name=name,
                          compiler_params=_cp(("parallel",)))(chip, half, p, got)


SEM_SPEC = pl.BlockSpec(memory_space=pltpu.SEMAPHORE)
HBM_ONLY = pl.BlockSpec(memory_space=pltpu.HBM)
DATAFLOW = pltpu.SideEffectType.DATAFLOW_SIDE_EFFECTING


def _in_hbm(a):
    return pltpu.with_memory_space_constraint(a, pltpu.HBM)


def _copies_start(name, srcs, lands, plan, n_copies):
    ns, nl = len(srcs), len(lands)

    def body(*refs):
        src_refs, land_refs = refs[:ns], refs[ns:ns + nl]
        send_sems, recv_sems = refs[ns + nl], refs[ns + nl + 1]
        token = refs[-1]
        for i, (src, dst, peer) in enumerate(plan(_me(), src_refs, land_refs)):
            pltpu.make_async_remote_copy(src_ref=src, dst_ref=dst, send_sem=send_sems.at[i], recv_sem=recv_sems.at[i],
                                         device_id=peer, device_id_type=MESH).start()
        token[...] = jnp.zeros_like(token)

    outs = pl.pallas_call(
        body, name=name,
        out_shape=(pltpu.SemaphoreType.DMA((n_copies,)), pltpu.SemaphoreType.DMA((n_copies,)),
                   *[pltpu.HBM(l.shape, l.dtype) for l in lands], SDS((8, LANES), F32)),
        in_specs=[HBM_ONLY] * (ns + nl), out_specs=(SEM_SPEC, SEM_SPEC, *[HBM_ONLY] * nl, VMEM_SPEC),
        input_output_aliases={ns + i: 2 + i for i in range(nl)},
        compiler_params=pltpu.CompilerParams(has_side_effects=DATAFLOW),
    )(*[_in_hbm(s) for s in srcs], *[_in_hbm(l) for l in lands])
    return outs[0], outs[1], list(outs[2:2 + nl]), outs[-1]


def _copies_wait(name, srcs, lands, send_sems, recv_sems, plan, after):
    ns, nl = len(srcs), len(lands)

    def body(*refs):
        src_refs, land_refs = refs[:ns], refs[ns:ns + nl]
        send_ref, recv_ref = refs[ns + nl], refs[ns + nl + 1]
        for i, (src, dst, peer) in enumerate(plan(_me(), src_refs, land_refs)):
            cp = pltpu.make_async_remote_copy(src_ref=src, dst_ref=dst, send_sem=send_ref.at[i], recv_sem=recv_ref.at[i],
                                              device_id=peer, device_id_type=MESH)
            cp.wait_send()
            cp.wait_recv()

    outs = pl.pallas_call(
        body, name=name, out_shape=[pltpu.HBM(l.shape, l.dtype) for l in lands],
        in_specs=[HBM_ONLY] * (ns + nl) + [SEM_SPEC, SEM_SPEC, HBM_SPEC], out_specs=[HBM_ONLY] * nl,
        input_output_aliases={ns + i: i for i in range(nl)},
        compiler_params=pltpu.CompilerParams(has_side_effects=DATAFLOW),
    )(*[_in_hbm(s) for s in srcs], *lands, send_sems, recv_sems, after)
    return list(outs)


def _half(ref, rows, axis):
    idx = [slice(None)] * axis + [rows]
    return ref.at[tuple(idx)]


def _gather_plans(layer, halves):
    def ici(me, srcs, lands):
        out = []
        for a, kh in enumerate(halves):
            rows = pl.ds(pl.multiple_of(me[2] * kh, 16), kh)
            for k in range(3):
                out.append((srcs[a].at[layer, rows], lands[a].at[_chip_of(me), rows], _flip(me, CHIP_FLIPS[k])))
        return out

    def d2d(me, srcs, lands):
        out = []
        for a, kh in enumerate(halves):
            rows = pl.ds(pl.multiple_of(me[2] * kh, 16), kh)
            for k in range(3):
                slab = lands[a].at[_chip_of(_flip(me, CHIP_FLIPS[k])), rows]
                out.append((slab, slab, _flip(me, SIBLING)))
        return out

    return ici, d2d


def _to_sibling_plan(halves, axes):
    def plan(me, srcs, lands):
        out = []
        for a, (kh, axis) in enumerate(zip(halves, axes)):
            rows = pl.ds(pl.multiple_of((1 - me[2]) * kh, 16), kh)
            out.append((_half(srcs[a], rows, axis), lands[a], _flip(me, SIBLING)))
        return out

    return plan


def _to_chips_plan(n_arr):
    def plan(me, srcs, lands):
        out = []
        for a in range(n_arr):
            for k in range(3):
                peer = _flip(me, CHIP_FLIPS[k])
                out.append((srcs[a].at[_chip_of(peer)], lands[a].at[k], peer))
        return out

    return plan


def _swap_plan(halves):
    def plan(me, srcs, lands):
        out = []
        for a, kh in enumerate(halves):
            rows = pl.ds(pl.multiple_of(me[2] * kh, 16), kh)
            out.append((lands[a].at[rows], lands[a].at[rows], _flip(me, SIBLING)))
        return out

    return plan


def _add_half(g, got, half, col_shards, name):
    s, kh, n = got.shape
    tr = _pick(kh, (512, 256, 128, 64, 32, 16))
    nrb = kh // tr
    width = n // N_CHIPS if col_shards else n
    cw = _pick(width, (1024, 512, 256, 128))
    per = width // cw

    def body(h_ref, a_ref, b_ref, o_ref):
        o_ref[...] = (a_ref[...].astype(F32) + b_ref[...].astype(F32)).astype(o_ref.dtype)

    in_specs = [pl.BlockSpec((None, tr, cw), lambda j, i, c, h: (j, h[0] * nrb + i, c)),
                pl.BlockSpec((None, tr, cw), lambda j, i, c, h: (j, i, c))]
    if col_shards:
        assert s == 1
        out_spec = pl.BlockSpec((None, tr, cw), lambda j, i, c, h: (c // per, i, c % per))
        out_shape = SDS((N_CHIPS, kh, width), g.dtype)
    else:
        out_spec, out_shape = in_specs[1], SDS((s, kh, n), g.dtype)
    grid_spec = pltpu.PrefetchScalarGridSpec(num_scalar_prefetch=1, grid=(s, nrb, n // cw), in_specs=in_specs,
                                             out_specs=out_spec)
    return pl.pallas_call(body, grid_spec=grid_spec, out_shape=out_shape, name=name,
                          compiler_params=_cp(("parallel", "parallel", "parallel")))(half, g, got)


def _sum_devices(g, name):
    _, _, n = g.shape

    def body(g_ref, o_ref):
        acc = g_ref[0]
        for k in range(1, N_DEV):
            acc = acc + g_ref[k]
        o_ref[...] = acc

    return pl.pallas_call(body, out_shape=SDS((1, n), F32), in_specs=[VMEM_SPEC], out_specs=VMEM_SPEC, name=name)(g)


def _silu_rows(c, name):
    def body(c_ref, o_ref):
        v = c_ref[...]
        o_ref[...] = v * _sigmoid(v)

    return pl.pallas_call(body, out_shape=SDS(c.shape, F32), in_specs=[VMEM_SPEC], out_specs=VMEM_SPEC, name=name)(c)


BIG = (("w_in", 2), ("w_uq", 2), ("w_ukv", 2), ("w_o_mla", 2), ("w_o_gdn", 2), ("w_o", 1), ("w_gate_up", 2), ("w_down", 1))
KERNEL_BIG = ("w_main", "w_small", "w_uq", "w_ukv", "w_o_mla", "w_o_gdn", "w_o", "w_gate_up", "w_down")
COL_SHARDED_AS_IS = ("w_o_mla", "w_o_gdn", "w_gate_up")
ROW_SHARDED = ("w_o", "w_down")
FIRST_NEEDED = ("w_in", "w_uq", "w_ukv")
LATER_NEEDED = ("w_o_mla", "w_o_gdn", "w_o", "w_gate_up", "w_down")
FFN_GRADS = ("w_gate_up", "w_down")
MIXER_GRADS = ("w_in", "w_uq", "w_ukv", "w_o_mla", "w_o_gdn", "w_o")
MIXER_GRADS_KERNEL = ("w_main", "w_small", "w_uq", "w_ukv", "w_o_mla", "w_o_gdn", "w_o")
SMALL = ("norm_mix", "norm_ffn", "q_a_norm", "kv_a_norm", "A_log", "dt_bias", "gdn_norm")
WEIGHTS = ("w_ada", "b_ada", "norm_mix", "norm_ffn", "w_in", "q_a_norm", "kv_a_norm", "w_uq", "w_ukv", "w_o_mla", "conv_w",
           "A_log", "dt_bias", "gdn_norm", "w_o_gdn", "w_o", "w_gate_up", "w_down", "final_norm")
ADA_PAD = 16
K_PAD = 128


def _pad_to(a, n, axis):
    pad = [(0, 0)] * a.ndim
    pad[axis] = (0, n - a.shape[axis])
    return jnp.pad(a, pad)


def kernel(x, c, positions, w_ada, b_ada, norm_mix, norm_ffn, w_in, q_a_norm, kv_a_norm, w_uq, w_ukv, w_o_mla, conv_w, A_log, dt_bias, gdn_norm, w_o_gdn, w_o, w_gate_up, w_down, final_norm, loss_target, m_w_ada, m_b_ada, m_norm_mix, m_norm_ffn, m_w_in, m_q_a_norm, m_kv_a_norm, m_w_uq, m_w_ukv, m_w_o_mla, m_conv_w, m_A_log, m_dt_bias, m_gdn_norm, m_w_o_gdn, m_w_o, m_w_gate_up, m_w_down, m_final_norm, v_w_ada, v_b_ada, v_norm_mix, v_norm_ffn, v_w_in, v_q_a_norm, v_kv_a_norm, v_w_uq, v_w_ukv, v_w_o_mla, v_conv_w, v_A_log, v_dt_bias, v_gdn_norm, v_w_o_gdn, v_w_o, v_w_gate_up, v_w_down, v_final_norm):
    env = dict(locals())
    w = {n: env[n] for n in WEIGHTS}
    depth, d = norm_mix.shape
    t = x.shape[1]
    me = _me()
    chip = _chip_of(me)
    dev = 4 * me[0] + 2 * me[1] + me[2]
    ada_cols = w_ada.shape[2]

    half_idx = me[2].astype(jnp.int32).reshape(1)
    chip_idx = chip.astype(jnp.int32).reshape(1)
    w16 = {n: w[n].astype(BF16) for n, _ in BIG}
    shard_axis = dict(BIG)
    gather = {}

    def start_group(key, layer, names, dep):
        srcs = [w16[n] for n in names]
        plans = _gather_plans(layer, [a.shape[1] // 2 for a in srcs])
        landing = [lax.empty((N_CHIPS,) + a.shape[1:], BF16) for a in srcs]
        send_s, recv_s, landing, tok = _copies_start(f"gather_{key}_ici_start", srcs + [dep], landing, plans[0], 3 * len(names))
        gather[key] = dict(layer=layer, names=names, srcs=srcs, plans=plans, ici=(send_s, recv_s, landing), tok=tok)
        return tok[0, 0]

    def pass_to_sibling(key, after):
        st = gather[key]
        send_s, recv_s, landing = st["ici"]
        landing = _copies_wait(f"gather_{key}_ici_wait", st["srcs"] + [st["tok"]], landing, send_s, recv_s, st["plans"][0],
                               st["tok"] if after is None else after)
        st["d2d"] = _copies_start(f"gather_{key}_d2d_start", [], landing, st["plans"][1], 3 * len(st["names"]))
        return st["d2d"][3]

    def gathered(key, after=None):
        st = gather[key]
        send_s, recv_s, landing, tok = st["d2d"]
        landing = _copies_wait(f"gather_{key}_d2d_wait", [], landing, send_s, recv_s, st["plans"][1],
                               tok if after is None else after)
        return {n: jnp.concatenate([jnp.where(chip == j, own[st["layer"]], got[j]) for j in range(N_CHIPS)],
                                   axis=shard_axis[n] - 1)
                for n, own, got in zip(st["names"], st["srcs"], landing)}

    first_started = start_group("l0a", 0, FIRST_NEEDED, jnp.zeros((8, LANES), F32))
    c = c + first_started
    w_in_adam = tuple(env[p + "w_in"] + first_started for p in ("", "m_", "v_"))
    busy = sum(jnp.minimum(jnp.abs(a[0, :1, :1].astype(F32)), 0.0)
               for a in (*w_in_adam, *[w16[n] for n in LATER_NEEDED]))[0, 0]
    full = {}
    conv_all = _all_gather_small(conv_w.reshape(1, -1) + (first_started + busy), "gather_conv").reshape(
        (N_DEV,) + conv_w.shape)
    full["conv_w"] = jnp.concatenate([conv_all[2 * j] for j in range(N_CHIPS)], axis=2)
    for n in SMALL:
        full[n] = w[n]

    c_all = _all_gather_small(c, "gather_c").reshape(N_DEV, d)
    c_act = _silu_rows(_pad_to(c_all, ADA_PAD, 0), "silu_c")
    b_cols = lax.dynamic_slice_in_dim(b_ada, chip * ada_cols, ada_cols, axis=1)
    mod_cols = jnp.stack([
        _mm(c_act, w_ada, b_layer=l, acc_in=jnp.broadcast_to(b_cols[l][None], (ADA_PAD, ada_cols)), name=f"ada_l{l}")[:N_DEV]
        for l in range(depth)])
    mod_all = _all_gather_small(mod_cols.reshape(depth * N_DEV, ada_cols), "gather_mod")
    mod_all = mod_all.reshape(N_DEV, depth, N_DEV, ada_cols)
    mods = []
    for l in range(depth):
        mine = jnp.concatenate([lax.dynamic_index_in_dim(mod_all[2 * j, l], dev, axis=0, keepdims=True)
                                for j in range(N_CHIPS)], axis=1)
        mods.append(tuple(mine[:, i * d:(i + 1) * d] for i in range(6)))

    tabs = _rope_tables(positions[0])
    tie = start_group("l0b", 0, LATER_NEEDED, pass_to_sibling(
        "l0a", mods[depth - 1][5][:, :LANES] + full["conv_w"].reshape(1, -1)[:, :LANES]))

    def late_weights(key, next_key, next_layer, behind):
        tok = pass_to_sibling(key, behind)
        started = 0.0 if next_key is None else start_group(next_key, next_layer, FIRST_NEEDED, tok)
        return gathered(key), started

    def next_later_group(behind):
        return start_group("l1b", 1, LATER_NEEDED, pass_to_sibling("l1a", behind))

    wts, saved = [None] * depth, [None] * depth
    tied = (mods[0][0] + tie,) + mods[0][1:]
    h, saved[0], wts[0] = _layer_fwd(x[0], tied, _layer_weights(gathered("l0a", gather["l0b"]["tok"]), full, 0), tabs, "l0",
                                     functools.partial(late_weights, "l0b", "l1a", 1), next_later_group)
    h, saved[1], wts[1] = _layer_fwd(h, mods[1], _layer_weights(gathered("l1a"), full, 1), tabs, "l1",
                                     functools.partial(late_weights, "l1b", None, None))
    loss_part, dh, dfn = _loss_head(h, final_norm[None], loss_target[0], "loss_head")
    dfn = dfn[0]

    def col_shards(g):
        return g.reshape(g.shape[0], N_CHIPS, g.shape[1] // N_CHIPS).transpose(1, 0, 2)

    def reduce_scatter_stages(tag, g, knames, names):
        srcs = [g[n].reshape(N_CHIPS, -1, g[n].shape[1]) if n in ROW_SHARDED else g[n] for n in knames]
        axes = [1 if n in ROW_SHARDED else 0 for n in knames]
        halves = [a.shape[ax] // 2 for a, ax in zip(srcs, axes)]
        got_shapes = [a.shape[:ax] + (kh,) + a.shape[ax + 1:] for a, ax, kh in zip(srcs, axes, halves)]
        plan_a, plan_c = _to_sibling_plan(halves, axes), _to_chips_plan(len(names))
        st, out = {}, {}
        st["a"] = _copies_start(f"{tag}_sibling_start", srcs, [lax.empty(sh, BF16) for sh in got_shapes], plan_a, len(srcs))

        def after_or(tok, after):
            return tok if after is None else after

        def stage0(after):
            send_s, recv_s, landing, tok = st["a"]
            got = _copies_wait(f"{tag}_sibling_wait", srcs, landing, send_s, recv_s, plan_a, after_or(tok, after))
            sums = {}
            for n, a, b in zip(knames, srcs, got):
                a3, b3 = (v if v.ndim == 3 else v[None] for v in (a, b))
                r = _add_half(a3, b3, half_idx, n in COL_SHARDED_AS_IS, f"{tag}_add_{n}")
                sums[n] = r if (n in COL_SHARDED_AS_IS or n in ROW_SHARDED) else r[0]
            if "w_main" in sums:
                sums["w_in"] = col_shards(_unlay_w_in(sums["w_main"], sums["w_small"]))
                sums["w_uq"] = col_shards(_unlay_w_uq(sums["w_uq"]))
                sums["w_ukv"] = col_shards(_unlay_w_ukv(sums["w_ukv"]))
            st["p"] = [sums[n] for n in names]
            st["c"] = _copies_start(f"{tag}_chips_start", st["p"], [lax.empty((3,) + p.shape[1:], BF16) for p in st["p"]],
                                    plan_c, 3 * len(names))
            return st["c"][3][0, 0]

        def stage1(after):
            send_s, recv_s, landing, tok = st["c"]
            got = _copies_wait(f"{tag}_chips_wait", st["p"], landing, send_s, recv_s, plan_c, after_or(tok, after))
            sums = [_sum_chips(p, q, chip_idx, half_idx, f"{tag}_sum_{n}") for n, p, q in zip(names, st["p"], got)]
            plan_e = _swap_plan([r.shape[0] // 2 for r in sums])
            st["e"] = _copies_start(f"{tag}_swap_start", [], sums, plan_e, len(names)) + (plan_e,)
            return st["e"][3][0, 0]

        def stage2(after):
            send_s, recv_s, landing, tok, plan_e = st["e"]
            got = _copies_wait(f"{tag}_swap_wait", [], landing, send_s, recv_s, plan_e, after_or(tok, after))
            out.update(zip(names, got))

        return (stage0, stage1, stage2), out, st["a"][3][0, 0]

    dmods, grads, groups = [None] * depth, [None] * depth, {}

    def ffn_group_l1(g):
        groups["l1_ffn"] = reduce_scatter_stages("rs_l1_ffn", g, FFN_GRADS, FFN_GRADS)
        return groups["l1_ffn"][2]

    dh, dmods[1], grads[1] = _layer_bwd(dh, saved[1], mods[1], wts[1], tabs, "l1", after_ffn=ffn_group_l1)
    groups["l1_mix"] = reduce_scatter_stages("rs_l1_mix", grads[1], MIXER_GRADS_KERNEL, MIXER_GRADS)
    tied = mods[0][:5] + (mods[0][5] + groups["l1_mix"][2],)

    def ffn_group_l0(g):
        behind = g["w_gate_up"]
        tok = groups["l1_ffn"][0][0](behind) + groups["l1_mix"][0][0](behind)
        groups["l0_ffn"] = reduce_scatter_stages("rs_l0_ffn", g, FFN_GRADS, FFN_GRADS)
        return tok + groups["l0_ffn"][2]

    def after_gdn_l0(behind):
        return groups["l0_ffn"][0][0](behind)

    dx, dmods[0], grads[0] = _layer_bwd(dh, saved[0], tied, wts[0], tabs, "l0", after_ffn=ffn_group_l0, after_gdn=after_gdn_l0)
    groups["l0_mix"] = reduce_scatter_stages("rs_l0_mix", grads[0], MIXER_GRADS_KERNEL, MIXER_GRADS)
    for key in ("l1_ffn", "l1_mix", "l0_ffn"):
        groups[key][0][1](dx)
    g_out, deltas, new_m, new_v = {}, {}, {}, {}

    def reduced(names):
        for n in names:
            g_out[n] = [groups[f"l{l}_ffn" if n in FFN_GRADS else f"l{l}_mix"][1][n] for l in range(depth)]

    def update(names, tie):
        for n in names:
            w_n, m_n, v_n = w_in_adam if n == "w_in" else (w[n], env["m_" + n], env["v_" + n])
            g_out[n], deltas[n], new_m[n], new_v[n] = _adamw(w_n, g_out[n], m_n, v_n, tie, f"adamw_{n}")

    small = [_small_grads_ref_layout(grads[l]) for l in range(depth)]
    small_parts = [jnp.concatenate(dmods[l], axis=1).reshape(-1) for l in range(depth)]
    small_parts += [jnp.stack([small[l][n] for l in range(depth)]).reshape(-1) for n in SMALL]
    small_parts += [dfn, loss_part[0, :1]]
    small_sizes = [p.shape[0] for p in small_parts]
    packed = jnp.concatenate(small_parts)
    n_small = -(-packed.shape[0] // LANES) * LANES
    small_all = _all_gather_small(_pad_to(packed, n_small, 0).reshape(1, n_small), "gather_small_grads")
    small_sum = _sum_devices(small_all, "sum_small_grads")[0]
    offs = [0]
    for sz in small_sizes:
        offs.append(offs[-1] + sz)
    g_out["b_ada"] = jnp.stack([small_sum[offs[l]:offs[l + 1]] for l in range(depth)])
    for i, n in enumerate(SMALL):
        g_out[n] = small_sum[offs[depth + i]:offs[depth + i + 1]].reshape(w[n].shape)
    g_out["final_norm"] = small_sum[offs[depth + len(SMALL)]:offs[depth + len(SMALL) + 1]]
    loss = small_sum[offs[depth + len(SMALL) + 1]]

    c_act_t = _pad_to(c_act[:N_DEV].T, K_PAD, 1)
    g_ada = []
    for l in range(depth):
        dmod_l = small_all[:, 0, offs[l]:offs[l + 1]]
        dmod_cols = lax.dynamic_slice_in_dim(dmod_l, chip * ada_cols, ada_cols, axis=1)
        g_ada.append(_mm(c_act_t, _pad_to(dmod_cols, K_PAD, 0), name=f"ada_grad_l{l}"))
    g_out["w_ada"] = jnp.stack(g_ada)

    conv_g = jnp.stack([small[l]["conv_w"] for l in range(depth)])
    conv_all_g = _all_gather_small(conv_g.reshape(1, -1), "gather_conv_grads")
    conv_sum = _sum_devices(conv_all_g, "sum_conv_grads").reshape(conv_g.shape)
    n_cc = conv_w.shape[2]
    g_out["conv_w"] = lax.dynamic_slice_in_dim(conv_sum, chip * n_cc, n_cc, axis=2)

    mix0 = groups["l0_mix"][0]
    started = mix0[0](conv_sum.reshape(-1)[:LANES] + small_sum[:LANES])
    for key in ("l1_ffn", "l1_mix", "l0_ffn"):
        groups[key][0][2](None)
    reduced(FFN_GRADS)
    first_updates = ("w_ada", "b_ada", "final_norm", "conv_w") + SMALL + FFN_GRADS
    update(first_updates, jnp.zeros((8, LANES), F32) + started)
    corner = lambda a: a.reshape((1,) * (3 - a.ndim) + a.shape)[0, :1, :LANES]
    mix0[1](sum(corner(deltas[n]) for n in first_updates if w[n].shape[-1] >= LANES))
    mix0[2](None)
    reduced(MIXER_GRADS)
    update(MIXER_GRADS, jnp.zeros((8, LANES), F32))
    return (loss, dx[None], *[g_out[n] for n in WEIGHTS], *[deltas[n] for n in WEIGHTS],
            *[new_m[n] for n in WEIGHTS], *[new_v[n] for n in WEIGHTS])
```

```python
import functools
import math

import jax
import jax.numpy as jnp
from jax import lax
from jax.experimental import pallas as pl
from jax.experimental.pallas import tpu as pltpu

F32 = jnp.float32
BF16 = jnp.bfloat16
SDS = jax.ShapeDtypeStruct
MESH = pl.DeviceIdType.MESH
AXES = ("x", "y", "c")

EPS = 1e-6
HEADS = 8
D_NOPE = 128
D_ROPE = 64
D_QK = 256
D_V = 128
Q_LORA = 512
KV_LORA = 512
CHUNK = 64
CONV_WIDTH = 4
GDN_W = HEADS * D_V
MAIN_FIXED = 2 * Q_LORA + 4 * GDN_W
LANES = 128
VMEM_LIMIT = 56 * 1024 * 1024
ADAM_LR, ADAM_B1, ADAM_B2, ADAM_EPS, ADAM_WD, ADAM_STEP = 0.001, 0.9, 0.999, 1e-8, 0.01, 10


def _pick(n, cands):
    for cand in cands:
        if n % cand == 0:
            return cand
    return n


def _cp(sem):
    return pltpu.CompilerParams(dimension_semantics=sem, vmem_limit_bytes=VMEM_LIMIT)


def _row_tile(t):
    return _pick(t, (256, 128, 64, 32, 16, 8))


def _dot(a, b):
    return jnp.dot(a.astype(BF16), b.astype(BF16), preferred_element_type=F32)


def _dot_nt(a, b):
    return lax.dot_general(a.astype(BF16), b.astype(BF16), (((1,), (1,)), ((), ())), preferred_element_type=F32)


def _dot_tn(a, b):
    return lax.dot_general(a.astype(BF16), b.astype(BF16), (((0,), (0,)), ((), ())), preferred_element_type=F32)


def _bdot(a, b, exact=False):
    dims = (((2,), (1,)), ((0,), (0,)))
    if exact:
        ah, bh = a.astype(BF16), b.astype(BF16)
        al, bl = (a - ah.astype(F32)).astype(BF16), (b - bh.astype(F32)).astype(BF16)
        return (lax.dot_general(ah, bh, dims, preferred_element_type=F32)
                + lax.dot_general(ah, bl, dims, preferred_element_type=F32)
                + lax.dot_general(al, bh, dims, preferred_element_type=F32))
    return lax.dot_general(a.astype(BF16), b.astype(BF16), dims, preferred_element_type=F32)


def _bdot_nt(a, b):
    return lax.dot_general(a.astype(BF16), b.astype(BF16), (((2,), (2,)), ((0,), (0,))), preferred_element_type=F32)


def _sigmoid(x):
    return 1.0 / (1.0 + jnp.exp(-x))


def _mm(a, b, *, tb=False, out_dtype=F32, acc_in=None, b_layer=None, name):
    m, k = a.shape
    if b_layer is not None:
        assert not tb and b.shape[1] == k and k <= 2048
        n = b.shape[2]
    else:
        n = b.shape[0] if tb else b.shape[1]
        assert (b.shape[1] if tb else b.shape[0]) == k
    tm = _pick(m, (1024, 512, 256, 128))
    tn = _pick(n, (1024, 512, 256, 128))
    tk = k if k <= 2048 else _pick(k, (1024, 512, 256, 128))
    nk = k // tk
    has_acc = acc_in is not None

    def body_one_step(*refs):
        a_ref, b_ref = refs[:2]
        o_ref = refs[-1]
        acc = _dot_nt(a_ref[...], b_ref[...]) if tb else _dot(a_ref[...], b_ref[...])
        if has_acc:
            acc = acc + refs[2][...].astype(F32)
        o_ref[...] = acc.astype(out_dtype)

    if nk == 1:
        if b_layer is not None:
            b_spec = pl.BlockSpec((None, k, tn), lambda i, j: (b_layer, 0, j))
        else:
            b_spec = pl.BlockSpec((tn, k), lambda i, j: (j, 0)) if tb else pl.BlockSpec((k, tn), lambda i, j: (0, j))
        in_specs = [pl.BlockSpec((tm, k), lambda i, j: (i, 0)), b_spec]
        args = [a, b]
        if has_acc:
            in_specs.append(pl.BlockSpec((tm, tn), lambda i, j: (i, j)))
            args.append(acc_in)
        return pl.pallas_call(
            body_one_step, grid=(m // tm, n // tn), in_specs=in_specs, out_specs=pl.BlockSpec((tm, tn), lambda i, j: (i, j)),
            out_shape=SDS((m, n), out_dtype), name=name, compiler_params=_cp(("parallel", "parallel")))(*args)

    def body(*refs):
        if has_acc:
            a_ref, b_ref, c_ref, o_ref, acc = refs
        else:
            a_ref, b_ref, o_ref, acc = refs
        kk = pl.program_id(2)

        @pl.when(kk == 0)
        def _():
            if has_acc:
                acc[...] = c_ref[...].astype(F32)
            else:
                acc[...] = jnp.zeros_like(acc)

        if tb:
            acc[...] += _dot_nt(a_ref[...], b_ref[...])
        else:
            acc[...] += _dot(a_ref[...], b_ref[...])

        @pl.when(kk == nk - 1)
        def _():
            o_ref[...] = acc[...].astype(out_dtype)

    in_specs = [pl.BlockSpec((tm, tk), lambda i, j, kk: (i, kk)),
                pl.BlockSpec((tn, tk), lambda i, j, kk: (j, kk)) if tb
                else pl.BlockSpec((tk, tn), lambda i, j, kk: (kk, j))]
    args = [a, b]
    if has_acc:
        in_specs.append(pl.BlockSpec((tm, tn), lambda i, j, kk: (i, j)))
        args.append(acc_in)
    return pl.pallas_call(
        body, grid=(m // tm, n // tn, nk), in_specs=in_specs,
        out_specs=pl.BlockSpec((tm, tn), lambda i, j, kk: (i, j)),
        out_shape=SDS((m, n), out_dtype), scratch_shapes=[pltpu.VMEM((tm, tn), F32)],
        name=name, compiler_params=_cp(("parallel", "parallel", "arbitrary")))(*args)


def _norm_fwd(x, col, width, w, sc, sh, name):
    t = x.shape[0]
    tm = _row_tile(t)

    def body(x_ref, w_ref, sc_ref, sh_ref, o_ref):
        xv = x_ref[...]
        r = lax.rsqrt(jnp.mean(xv * xv, axis=-1, keepdims=True) + EPS)
        n = xv * r * w_ref[...]
        o_ref[...] = (n * (1.0 + sc_ref[...]) + sh_ref[...]).astype(o_ref.dtype)

    row = pl.BlockSpec((1, width), lambda i: (0, 0))
    return pl.pallas_call(
        body, grid=(t // tm,), in_specs=[pl.BlockSpec((tm, width), lambda i: (i, col)), row, row, row],
        out_specs=pl.BlockSpec((tm, width), lambda i: (i, 0)), out_shape=SDS((t, width), BF16),
        name=name, compiler_params=_cp(("parallel",)))(x, w, sc, sh)


def _norm_bwd(x, col, width, w, sc, dh, dres, out_dtype, name):
    t = x.shape[0]
    tm = _row_tile(t)
    has_res = dres is not None

    def body(*refs):
        if has_res:
            x_ref, w_ref, sc_ref, dh_ref, dres_ref, dx_ref, dw_ref, dsc_ref, dsh_ref = refs
        else:
            x_ref, w_ref, sc_ref, dh_ref, dx_ref, dw_ref, dsc_ref, dsh_ref = refs

        @pl.when(pl.program_id(0) == 0)
        def _():
            dw_ref[...] = jnp.zeros_like(dw_ref)
            dsc_ref[...] = jnp.zeros_like(dsc_ref)
            dsh_ref[...] = jnp.zeros_like(dsh_ref)

        xv = x_ref[...]
        dhv = dh_ref[...].astype(F32)
        wv = w_ref[...]
        r = lax.rsqrt(jnp.mean(xv * xv, axis=-1, keepdims=True) + EPS)
        xh = xv * r
        n = xh * wv
        dsh_ref[...] += jnp.sum(dhv, axis=0, keepdims=True)
        dsc_ref[...] += jnp.sum(dhv * n, axis=0, keepdims=True)
        dn = dhv * (1.0 + sc_ref[...])
        dw_ref[...] += jnp.sum(dn * xh, axis=0, keepdims=True)
        dxh = dn * wv
        dx = r * (dxh - xh * jnp.mean(dxh * xh, axis=-1, keepdims=True))
        if has_res:
            dx = dx + dres_ref[...]
        dx_ref[...] = dx.astype(out_dtype)

    row = pl.BlockSpec((1, width), lambda i: (0, 0))
    blk = pl.BlockSpec((tm, width), lambda i: (i, 0))
    in_specs = [pl.BlockSpec((tm, width), lambda i: (i, col)), row, row, blk]
    args = [x, w, sc, dh]
    if has_res:
        in_specs.append(blk)
        args.append(dres)
    return pl.pallas_call(
        body, grid=(t // tm,), in_specs=in_specs, out_specs=[blk, row, row, row],
        out_shape=[SDS((t, width), out_dtype), SDS((1, width), F32), SDS((1, width), F32), SDS((1, width), F32)],
        name=name, compiler_params=_cp(("arbitrary",)))(*args)


def _rope128(x, tc, ts1, ts2):
    return x * tc + pltpu.roll(x, 96, 1) * ts1 + pltpu.roll(x, 32, 1) * ts2


def _rope128_t(d, tc, ts1, ts2):
    return d * tc + pltpu.roll(d * ts1, 32, 1) + pltpu.roll(d * ts2, 96, 1)


def _rope_q(q_raw, tc, ts1, ts2, transpose, out_dtype, name):
    t = q_raw.shape[0]
    tm = _row_tile(t)

    def body(q_ref, tc_ref, s1_ref, s2_ref, o_ref):
        fn = _rope128_t if transpose else _rope128
        for h in range(HEADS):
            base = h * D_QK
            o_ref[:, base:base + LANES] = q_ref[:, base:base + LANES].astype(out_dtype)
            x = q_ref[:, base + LANES:base + D_QK].astype(F32)
            o_ref[:, base + LANES:base + D_QK] = fn(x, tc_ref[...], s1_ref[...], s2_ref[...]).astype(out_dtype)

    blk = pl.BlockSpec((tm, HEADS * D_QK), lambda i: (i, 0))
    tab = pl.BlockSpec((tm, LANES), lambda i: (i, 0))
    return pl.pallas_call(
        body, grid=(t // tm,), in_specs=[blk, tab, tab, tab], out_specs=blk,
        out_shape=SDS((t, HEADS * D_QK), out_dtype), name=name, compiler_params=_cp(("parallel",)))(q_raw, tc, ts1, ts2)


def _k_assemble(kv_raw, p_small, tc, ts1, ts2, name):
    t = kv_raw.shape[0]
    tm = _row_tile(t)

    def body(kn_ref, ps_ref, tc_ref, s1_ref, s2_ref, o_ref):
        kpe = _rope128(ps_ref[...], tc_ref[...], s1_ref[...], s2_ref[...]).astype(BF16)
        for h in range(HEADS):
            o_ref[:, h * D_QK:h * D_QK + LANES] = kn_ref[:, h * LANES:(h + 1) * LANES].astype(BF16)
            o_ref[:, h * D_QK + LANES:(h + 1) * D_QK] = kpe

    tab = pl.BlockSpec((tm, LANES), lambda i: (i, 0))
    return pl.pallas_call(
        body, grid=(t // tm,),
        in_specs=[pl.BlockSpec((tm, HEADS * LANES), lambda i: (i, 0)), tab, tab, tab, tab],
        out_specs=pl.BlockSpec((tm, HEADS * D_QK), lambda i: (i, 0)),
        out_shape=SDS((t, HEADS * D_QK), BF16), name=name, compiler_params=_cp(("parallel",)))(kv_raw, p_small, tc, ts1, ts2)


def _k_assemble_bwd(dk, dv, tc, ts1, ts2, name):
    t = dk.shape[0]
    tm = _row_tile(t)

    def body(dk_ref, dv_ref, tc_ref, s1_ref, s2_ref, o_ref, pe_ref):
        acc = jnp.zeros((tm, LANES), F32)
        for h in range(HEADS):
            o_ref[:, h * LANES:(h + 1) * LANES] = dk_ref[:, h * D_QK:h * D_QK + LANES].astype(BF16)
            acc = acc + dk_ref[:, h * D_QK + LANES:(h + 1) * D_QK].astype(F32)
        o_ref[:, HEADS * LANES:] = dv_ref[...].astype(BF16)
        pe_ref[...] = _rope128_t(acc, tc_ref[...], s1_ref[...], s2_ref[...])

    tab = pl.BlockSpec((tm, LANES), lambda i: (i, 0))
    return pl.pallas_call(
        body, grid=(t // tm,),
        in_specs=[pl.BlockSpec((tm, HEADS * D_QK), lambda i: (i, 0)), pl.BlockSpec((tm, HEADS * LANES), lambda i: (i, 0)),
                  tab, tab, tab],
        out_specs=[pl.BlockSpec((tm, 2 * HEADS * LANES), lambda i: (i, 0)), tab],
        out_shape=[SDS((t, 2 * HEADS * LANES), BF16), SDS((t, LANES), F32)],
        name=name, compiler_params=_cp(("parallel",)))(dk, dv, tc, ts1, ts2)


def _attn_tile(t):
    return _pick(t, (256, 128, 64))


def _attn_fwd(q, k, v, v_off, name):
    t = q.shape[0]
    tq = _attn_tile(t)
    scale = (D_NOPE + D_ROPE) ** -0.5

    def body(q_ref, k_ref, v_ref, o_ref, lse_ref):
        for i in range(t // tq):
            n_k = (i + 1) * tq
            s = _dot_nt(q_ref[i * tq:(i + 1) * tq, :], k_ref[0:n_k, :]) * scale
            row = lax.broadcasted_iota(jnp.int32, (tq, n_k), 0) + i * tq
            colv = lax.broadcasted_iota(jnp.int32, (tq, n_k), 1)
            s = jnp.where(colv <= row, s, -jnp.inf)
            m = jnp.max(s, axis=-1, keepdims=True)
            p = jnp.exp(s - m)
            l = jnp.sum(p, axis=-1, keepdims=True)
            o = _dot(p, v_ref[0:n_k, :]) / l
            o_ref[i * tq:(i + 1) * tq, :] = o.astype(BF16)
            lse_ref[0, i * tq:(i + 1) * tq, :] = m + jnp.log(l)

    return pl.pallas_call(
        body, grid=(HEADS,),
        in_specs=[pl.BlockSpec((t, D_QK), lambda h: (0, h)), pl.BlockSpec((t, D_QK), lambda h: (0, h)),
                  pl.BlockSpec((t, D_V), lambda h: (0, v_off + h))],
        out_specs=[pl.BlockSpec((t, D_V), lambda h: (0, h)), pl.BlockSpec((1, t, 1), lambda h: (h, 0, 0))],
        out_shape=[SDS((t, HEADS * D_V), BF16), SDS((HEADS, t, 1), F32)],
        name=name, compiler_params=_cp(("parallel",)))(q, k, v)


def _attn_bwd(q, k, v, v_off, o, lse, do, name):
    t = q.shape[0]
    tq = _attn_tile(t)
    scale = (D_NOPE + D_ROPE) ** -0.5

    def body(q_ref, k_ref, v_ref, o_ref, lse_ref, do_ref, dq_ref, dk_ref, dv_ref):
        dk_ref[...] = jnp.zeros_like(dk_ref)
        dv_ref[...] = jnp.zeros_like(dv_ref)
        for i in range(t // tq):
            n_k = (i + 1) * tq
            rows = slice(i * tq, (i + 1) * tq)
            qi = q_ref[rows, :]
            doi = do_ref[rows, :].astype(F32)
            s = _dot_nt(qi, k_ref[0:n_k, :]) * scale
            row = lax.broadcasted_iota(jnp.int32, (tq, n_k), 0) + i * tq
            colv = lax.broadcasted_iota(jnp.int32, (tq, n_k), 1)
            p = jnp.where(colv <= row, jnp.exp(s - lse_ref[0, rows, :]), 0.0)
            dp = _dot_nt(doi, v_ref[0:n_k, :])
            delta = jnp.sum(doi * o_ref[rows, :].astype(F32), axis=-1, keepdims=True)
            ds = p * (dp - delta) * scale
            dq_ref[rows, :] = _dot(ds, k_ref[0:n_k, :])
            dk_ref[0:n_k, :] += _dot_tn(ds, qi)
            dv_ref[0:n_k, :] += _dot_tn(p, doi)

    qk_spec = pl.BlockSpec((t, D_QK), lambda h: (0, h))
    v_spec = pl.BlockSpec((t, D_V), lambda h: (0, h))
    return pl.pallas_call(
        body, grid=(HEADS,),
        in_specs=[qk_spec, qk_spec, pl.BlockSpec((t, D_V), lambda h: (0, v_off + h)), v_spec,
                  pl.BlockSpec((1, t, 1), lambda h: (h, 0, 0)), v_spec],
        out_specs=[qk_spec, qk_spec, v_spec],
        out_shape=[SDS((t, HEADS * D_QK), F32), SDS((t, HEADS * D_QK), F32), SDS((t, HEADS * D_V), F32)],
        name=name, compiler_params=_cp(("parallel",)))(q, k, v, o, lse, do)


CONV_COLS = 256


def _conv_pre(u, w_ref, rowi):
    acc = u * w_ref[CONV_WIDTH - 1:CONV_WIDTH, :]
    for sft in range(1, CONV_WIDTH):
        shifted = jnp.where(rowi >= sft, pltpu.roll(u, sft, 0), 0.0)
        acc = acc + shifted * w_ref[CONV_WIDTH - 1 - sft:CONV_WIDTH - sft, :]
    return acc


def _conv_fwd(p_main, conv_w, name):
    t = p_main.shape[0]
    off = 2 * Q_LORA // CONV_COLS

    def body(u_ref, w_ref, y_ref):
        u = u_ref[...]
        rowi = lax.broadcasted_iota(jnp.int32, u.shape, 0)
        pre = _conv_pre(u, w_ref, rowi)
        y_ref[...] = pre * _sigmoid(pre)

    return pl.pallas_call(
        body, grid=(3 * GDN_W // CONV_COLS,),
        in_specs=[pl.BlockSpec((t, CONV_COLS), lambda j: (0, off + j)), pl.BlockSpec((CONV_WIDTH, CONV_COLS), lambda j: (0, j))],
        out_specs=pl.BlockSpec((t, CONV_COLS), lambda j: (0, j)), out_shape=SDS((t, 3 * GDN_W), F32),
        name=name, compiler_params=_cp(("parallel",)))(p_main, conv_w)


def _conv_bwd(p_main, conv_w, dyc, name):
    t = p_main.shape[0]
    off = 2 * Q_LORA // CONV_COLS

    def body(u_ref, w_ref, dy_ref, du_ref, dw_ref):
        u = u_ref[...]
        rowi = lax.broadcasted_iota(jnp.int32, u.shape, 0)
        pre = _conv_pre(u, w_ref, rowi)
        sg = _sigmoid(pre)
        dpre = dy_ref[...] * sg * (1.0 + pre * (1.0 - sg))
        du = dpre * w_ref[CONV_WIDTH - 1:CONV_WIDTH, :]
        dw_ref[CONV_WIDTH - 1:CONV_WIDTH, :] = jnp.sum(dpre * u, axis=0, keepdims=True)
        for sft in range(1, CONV_WIDTH):
            back = jnp.where(rowi < t - sft, pltpu.roll(dpre, t - sft, 0), 0.0)
            du = du + back * w_ref[CONV_WIDTH - 1 - sft:CONV_WIDTH - sft, :]
            shifted = jnp.where(rowi >= sft, pltpu.roll(u, sft, 0), 0.0)
            dw_ref[CONV_WIDTH - 1 - sft:CONV_WIDTH - sft, :] = jnp.sum(dpre * shifted, axis=0, keepdims=True)
        du_ref[...] = du.astype(BF16)

    blk = pl.BlockSpec((t, CONV_COLS), lambda j: (0, j))
    wblk = pl.BlockSpec((CONV_WIDTH, CONV_COLS), lambda j: (0, j))
    return pl.pallas_call(
        body, grid=(3 * GDN_W // CONV_COLS,),
        in_specs=[pl.BlockSpec((t, CONV_COLS), lambda j: (0, off + j)), wblk, blk],
        out_specs=[blk, wblk], out_shape=[SDS((t, 3 * GDN_W), BF16), SDS((CONV_WIDTH, 3 * GDN_W), F32)],
        name=name, compiler_params=_cp(("parallel",)))(p_main, conv_w, dyc)


B_LO, A_LO, A_HI = D_ROPE, D_ROPE + HEADS, D_ROPE + 2 * HEADS


def _softplus(z):
    e = jnp.exp(-jnp.abs(z))
    log1p = jnp.where(e < 0.01, e * (1.0 - e * (0.5 - e * (1.0 / 3.0))), jnp.log(1.0 + e))
    return jnp.maximum(z, 0.0) + log1p


def _gdn_gates(p_small, a_row, dt_row, name):
    t = p_small.shape[0]

    def body(ps_ref, a_ref, dt_ref, g_ref, gc_ref):
        x = ps_ref[...]
        lane = lax.broadcasted_iota(jnp.int32, x.shape, 1)
        is_g = (lane >= A_LO) & (lane < A_HI)
        g = jnp.where(is_g, -jnp.exp(a_ref[...]) * _softplus(x + dt_ref[...]), 0.0)
        g_ref[...] = jnp.where(is_g, g, _sigmoid(x))
        pos = lax.broadcasted_iota(jnp.int32, x.shape, 0) % CHUNK
        acc = g
        sft = 1
        while sft < CHUNK:
            acc = acc + jnp.where(pos >= sft, pltpu.roll(acc, sft, 0), 0.0)
            sft *= 2
        gc_ref[...] = acc

    full = pl.BlockSpec((t, LANES), lambda i: (0, 0))
    row = pl.BlockSpec((1, LANES), lambda i: (0, 0))
    return pl.pallas_call(
        body, grid=(1,), in_specs=[full, row, row], out_specs=[full, full],
        out_shape=[SDS((t, LANES), F32), SDS((t, LANES), F32)], name=name,
        compiler_params=_cp(("arbitrary",)))(p_small, a_row, dt_row)


def _gdn_gates_bwd(p_small, a_row, dt_row, gates, dgates, dkpe, name):
    t = p_small.shape[0]

    def body(ps_ref, a_ref, dt_ref, g_ref, db_ref, dkpe_ref, dp_ref, da_ref, ddt_ref):
        x = ps_ref[...]
        lane = lax.broadcasted_iota(jnp.int32, x.shape, 1)
        is_g = (lane >= A_LO) & (lane < A_HI)
        is_b = (lane >= B_LO) & (lane < A_LO)
        pos = lax.broadcasted_iota(jnp.int32, x.shape, 0) % CHUNK
        acc = jnp.where(is_g, db_ref[...], 0.0)
        sft = 1
        while sft < CHUNK:
            acc = acc + jnp.where(pos < CHUNK - sft, pltpu.roll(acc, t - sft, 0), 0.0)
            sft *= 2
        dg = acc
        gv = g_ref[...]
        dz = jnp.where(is_g, dg * (-jnp.exp(a_ref[...])) * _sigmoid(x + dt_ref[...]), 0.0)
        da_ref[...] = jnp.sum(jnp.where(is_g, dg * gv, 0.0), axis=0, keepdims=True)
        ddt_ref[...] = jnp.sum(dz, axis=0, keepdims=True)
        dlb = jnp.where(is_b, db_ref[...] * gv * (1.0 - gv), 0.0)
        dp_ref[...] = (jnp.where(lane < D_ROPE, dkpe_ref[...], 0.0) + dlb + dz).astype(BF16)

    full = pl.BlockSpec((t, LANES), lambda i: (0, 0))
    row = pl.BlockSpec((1, LANES), lambda i: (0, 0))
    return pl.pallas_call(
        body, grid=(1,), in_specs=[full, row, row, full, full, full], out_specs=[full, row, row],
        out_shape=[SDS((t, LANES), BF16), SDS((1, LANES), F32), SDS((1, LANES), F32)], name=name,
        compiler_params=_cp(("arbitrary",)))(p_small, a_row, dt_row, gates, dgates, dkpe)


def _tri_inv(l, eye):
    x = eye - l
    p = _bdot(l, l, exact=True)
    steps = int(math.log2(CHUNK)) - 1
    for s in range(steps):
        x = x + _bdot(x, p, exact=True)
        if s < steps - 1:
            p = _bdot(p, p, exact=True)
    return x


def _l2n(x3):
    r = lax.rsqrt(jnp.sum(x3 * x3, axis=-1, keepdims=True) + EPS)
    return x3 * r, r


def _head_col(a_ref, lane_lo, n):
    a = a_ref[...]
    lane = lax.broadcasted_iota(jnp.int32, a.shape, 1)
    col = jnp.sum(jnp.where(lane == lane_lo + pl.program_id(0), a, 0.0), axis=-1, keepdims=True)
    return col.reshape(n, CHUNK, 1)


def _gdn_common(q3, k3, v3, b, gc):
    n = q3.shape[0]
    ri = lax.broadcasted_iota(jnp.int32, (n, CHUNK, CHUNK), 1)
    ci = lax.broadcasted_iota(jnp.int32, (n, CHUNK, CHUNK), 2)
    lower, strict = ri >= ci, ri > ci
    eye = (ri == ci).astype(F32)
    gr = jnp.sum(gc * eye, axis=1, keepdims=True)
    qh, rq = _l2n(q3)
    qn = qh * (D_V ** -0.5)
    kn, rk = _l2n(k3)
    dec = jnp.where(lower, jnp.exp(jnp.where(lower, gc - gr, 0.0)), 0.0)
    kb = kn * b
    mm = _bdot_nt(kb, kn)
    tinv = _tri_inv(jnp.where(strict, mm * dec, 0.0), eye)
    gam = jnp.exp(gc)
    u = _bdot(tinv, v3 * b, exact=True)
    w = _bdot(tinv, kb * gam, exact=True)
    qk = _bdot_nt(qn, kn)
    aqk = jnp.where(lower, qk * dec, 0.0)
    gl = gc[:, CHUNK - 1:CHUNK, :]
    kdf = jnp.exp(gl - gc)
    return dict(ri=ri, ci=ci, lower=lower, strict=strict, eye=eye, qh=qh, rq=rq, qn=qn, kn=kn, rk=rk, dec=dec, kb=kb,
                mm=mm, gam=gam, u=u, w=w, qk=qk, aqk=aqk, gl=gl, kdf=kdf, kd=kn * kdf, gr=gr, tinv=tinv)


def _gdn_fwd(yc, p_main, gates, gcum, gn, name):
    t = yc.shape[0]
    n = t // CHUNK
    z_off = (2 * Q_LORA + 3 * GDN_W) // D_V

    def body(q_ref, k_ref, v_ref, z_ref, gt_ref, gcum_ref, gn_ref, o_ref, g_ref, s_ref, u_s, w_s, qg_s, kd_s, a_s, e_s):
        c = _gdn_common(q_ref[...].reshape(n, CHUNK, D_V), k_ref[...].reshape(n, CHUNK, D_V),
                        v_ref[...].reshape(n, CHUNK, D_V), _head_col(gt_ref, B_LO, n), _head_col(gcum_ref, A_LO, n))
        u_s[...] = c["u"]
        w_s[...] = c["w"]
        qg_s[...] = c["qn"] * c["gam"]
        kd_s[...] = c["kd"]
        a_s[...] = c["aqk"]
        e_s[...] = jnp.broadcast_to(jnp.exp(c["gl"]), (n, 1, D_V))

        def step(i, s):
            s_ref[0, i] = s
            v_new = u_s[i] - _dot(w_s[i], s)
            o = _dot(qg_s[i], s) + _dot(a_s[i], v_new)
            o_ref[pl.ds(pl.multiple_of(i * CHUNK, CHUNK), CHUNK), :] = o
            return s * e_s[i] + _dot_tn(kd_s[i], v_new)

        lax.fori_loop(0, n, step, jnp.zeros((D_V, D_V), F32))
        o = o_ref[...]
        zz = z_ref[...]
        on = o * lax.rsqrt(jnp.mean(o * o, axis=-1, keepdims=True) + EPS) * gn_ref[...]
        g_ref[...] = (on * zz * _sigmoid(zz)).astype(BF16)

    col = lambda off: pl.BlockSpec((t, D_V), lambda h: (0, off + h))
    lanes = pl.BlockSpec((t, LANES), lambda h: (0, 0))
    big = pltpu.VMEM((n, CHUNK, D_V), F32)
    return pl.pallas_call(
        body, grid=(HEADS,),
        in_specs=[col(0), col(HEADS), col(2 * HEADS), col(z_off), lanes, lanes, pl.BlockSpec((1, D_V), lambda h: (0, 0))],
        out_specs=[col(0), col(0), pl.BlockSpec((1, n, D_V, D_V), lambda h: (h, 0, 0, 0))],
        out_shape=[SDS((t, GDN_W), F32), SDS((t, GDN_W), BF16), SDS((HEADS, n, D_V, D_V), F32)],
        scratch_shapes=[big, big, big, big, pltpu.VMEM((n, CHUNK, CHUNK), F32), pltpu.VMEM((n, 1, D_V), F32)],
        name=name, compiler_params=_cp(("parallel",)))(yc, yc, yc, p_main, gates, gcum, gn)


def _gdn_bwd(yc, p_main, gates, gcum, gn, o_raw, states, dgated, name):
    t = yc.shape[0]
    n = t // CHUNK
    z_off = (2 * Q_LORA + 3 * GDN_W) // D_V

    def body(q_ref, k_ref, v_ref, z_ref, gt_ref, gcum_ref, gn_ref, o_ref, s_ref, dg_ref,
             dq_ref, dk_ref, dv_ref, dz_ref, dgt_ref, dgn_ref,
             u_s, w_s, qg_s, kd_s, at_s, e_s, do_s, du_s, dw_s, dqg_s, dkd_s, da_s, dat_s, dgs_s):
        @pl.when(pl.program_id(0) == 0)
        def _():
            dgn_ref[...] = jnp.zeros_like(dgn_ref)
            dgt_ref[...] = jnp.zeros_like(dgt_ref)

        o = o_ref[...]
        zz = z_ref[...]
        dgv = dg_ref[...]
        gnv = gn_ref[...]
        r = lax.rsqrt(jnp.mean(o * o, axis=-1, keepdims=True) + EPS)
        oh = o * r
        sg = _sigmoid(zz)
        don = dgv * zz * sg
        dz_ref[...] = (dgv * oh * gnv * sg * (1.0 + zz * (1.0 - sg))).astype(BF16)
        dgn_ref[...] += jnp.sum(don * oh, axis=0, keepdims=True)
        doh = don * gnv
        do_s[...] = (r * (doh - oh * jnp.mean(doh * oh, axis=-1, keepdims=True))).reshape(n, CHUNK, D_V)

        q3 = q_ref[...].reshape(n, CHUNK, D_V)
        k3 = k_ref[...].reshape(n, CHUNK, D_V)
        v3 = v_ref[...].reshape(n, CHUNK, D_V)
        b, gc = _head_col(gt_ref, B_LO, n), _head_col(gcum_ref, A_LO, n)
        c = _gdn_common(q3, k3, v3, b, gc)
        gr = c["gr"]
        ri, ci = c["ri"], c["ci"]
        upper, sup = ci >= ri, ci > ri
        dect = jnp.where(upper, jnp.exp(jnp.where(upper, gr - gc, 0.0)), 0.0)
        tinv_t = lax.dot_general(c["eye"], c["tinv"], (((2,), (2,)), ((0,), (0,))), precision=lax.Precision.HIGHEST,
                                 preferred_element_type=F32)
        u_s[...] = c["u"]
        w_s[...] = c["w"]
        qg_s[...] = c["qn"] * c["gam"]
        kd_s[...] = c["kd"]
        at_s[...] = jnp.where(upper, _bdot_nt(c["kn"], c["qn"]) * dect, 0.0)
        e_s[...] = jnp.broadcast_to(jnp.exp(c["gl"]), (n, 1, D_V))

        def step(j, ds):
            i = n - 1 - j
            s = s_ref[0, i]
            do_i = do_s[i]
            v_new = u_s[i] - _dot(w_s[i], s)
            dvn = _dot(at_s[i], do_i) + _dot(kd_s[i], ds)
            da_s[i] = _dot_nt(do_i, v_new)
            dat_s[i] = _dot_nt(v_new, do_i)
            dqg_s[i] = _dot_nt(do_i, s)
            dw_s[i] = -_dot_nt(dvn, s)
            dkd_s[i] = _dot_nt(v_new, ds)
            du_s[i] = dvn
            dgs_s[i] = jnp.broadcast_to(jnp.sum(jnp.sum(s * ds, axis=1, keepdims=True), axis=0, keepdims=True), (1, D_V))
            return _dot_tn(qg_s[i], do_i) + e_s[i] * ds - _dot_tn(w_s[i], dvn)

        lax.fori_loop(0, n, step, jnp.zeros((D_V, D_V), F32))

        du, dw, dqg, dkd = du_s[...], dw_s[...], dqg_s[...], dkd_s[...]
        lower, strict, dec = c["lower"], c["strict"], c["dec"]
        kn, kb, qn, gam, kdf = c["kn"], c["kb"], c["qn"], c["gam"], c["kdf"]
        drv = _bdot(tinv_t, du, exact=True)
        drk = _bdot(tinv_t, dw, exact=True)
        dl = jnp.where(strict, -(_bdot_nt(drv, c["u"]) + _bdot_nt(drk, c["w"])), 0.0)
        dlt = jnp.where(sup, -(_bdot_nt(c["u"], drv) + _bdot_nt(c["w"], drk)), 0.0)
        da = jnp.where(lower, da_s[...], 0.0)
        dat = jnp.where(upper, dat_s[...], 0.0)
        e = (dl * c["mm"] + da * c["qk"]) * dec
        col_sums = jnp.sum(e, axis=1, keepdims=True)
        dgc = jnp.sum(e, axis=2, keepdims=True) - jnp.sum(col_sums * c["eye"], axis=2, keepdims=True)
        dkb = _bdot(dl * dec, kn) + gam * drk
        dkn = _bdot(dlt * dect, kb) + _bdot(dat * dect, qn) + b * dkb + dkd * kdf
        dqn = _bdot(da * dec, kn) + gam * dqg
        dgam = jnp.sum(drk * kb, axis=-1, keepdims=True) + jnp.sum(dqg * qn, axis=-1, keepdims=True)
        dbeta = jnp.sum(dkb * kn, axis=-1, keepdims=True) + jnp.sum(drv * v3, axis=-1, keepdims=True)
        dv_ref[...] = (b * drv).reshape(t, D_V)
        ee = jnp.sum(dkd * kn, axis=-1, keepdims=True) * kdf
        dgc = dgc + dgam * gam - ee
        rowc = lax.broadcasted_iota(jnp.int32, (n, CHUNK, 1), 1)
        tail = jnp.sum(ee, axis=1, keepdims=True) + dgs_s[...][:, :, 0:1] * jnp.exp(c["gl"])
        dgc = dgc + jnp.where(rowc == CHUNK - 1, tail, 0.0)
        lane = lax.broadcasted_iota(jnp.int32, (t, LANES), 1)
        head = pl.program_id(0)
        dgt_ref[...] += (jnp.where(lane == B_LO + head, dbeta.reshape(t, 1), 0.0)
                         + jnp.where(lane == A_LO + head, dgc.reshape(t, 1), 0.0))
        sc = D_V ** -0.5
        qh, rq, rk = c["qh"], c["rq"], c["rk"]
        dq_ref[...] = (rq * (sc * dqn - qh * jnp.sum(sc * dqn * qh, axis=-1, keepdims=True))).reshape(t, D_V)
        dk_ref[...] = (rk * (dkn - kn * jnp.sum(dkn * kn, axis=-1, keepdims=True))).reshape(t, D_V)

    once = pl.Buffered(1)
    col = lambda off: pl.BlockSpec((t, D_V), lambda h: (0, off + h), pipeline_mode=once)
    out_col = pl.BlockSpec((t, D_V), lambda h: (0, h))
    lanes = pl.BlockSpec((t, LANES), lambda h: (0, 0))
    row = pl.BlockSpec((1, D_V), lambda h: (0, 0))
    big = pltpu.VMEM((n, CHUNK, D_V), F32)
    sq = pltpu.VMEM((n, CHUNK, CHUNK), F32)
    small = pltpu.VMEM((n, 1, D_V), F32)
    return pl.pallas_call(
        body, grid=(HEADS,),
        in_specs=[col(0), col(HEADS), col(2 * HEADS), col(z_off), lanes, lanes, row, col(0),
                  pl.BlockSpec((1, n, D_V, D_V), lambda h: (h, 0, 0, 0), pipeline_mode=once), col(0)],
        out_specs=[out_col, out_col, out_col, out_col, lanes, row],
        out_shape=[SDS((t, GDN_W), F32), SDS((t, GDN_W), F32), SDS((t, GDN_W), F32), SDS((t, GDN_W), BF16),
                   SDS((t, LANES), F32), SDS((1, D_V), F32)],
        scratch_shapes=[big, big, big, big, sq, small, big, big, big, big, big, sq, sq, small],
        name=name, compiler_params=_cp(("arbitrary",)))(yc, yc, yc, p_main, gates, gcum, gn, o_raw, states, dgated)


def _col_tile(d):
    return _pick(d, (512, 256, 128))


def _mix_fwd(y_a, y_b, p_main, name):
    t, d = y_a.shape
    tm, cw = _row_tile(t), _col_tile(d)
    off_a, off_b = MAIN_FIXED // cw, (MAIN_FIXED + d) // cw

    def body(ya_ref, yb_ref, ga_ref, gb_ref, u_ref):
        u_ref[...] = (_sigmoid(ga_ref[...]) * ya_ref[...] + _sigmoid(gb_ref[...]) * yb_ref[...]).astype(BF16)

    blk = pl.BlockSpec((tm, cw), lambda i, j: (i, j))
    return pl.pallas_call(
        body, grid=(t // tm, d // cw),
        in_specs=[blk, blk, pl.BlockSpec((tm, cw), lambda i, j: (i, off_a + j)), pl.BlockSpec((tm, cw), lambda i, j: (i, off_b + j))],
        out_specs=blk, out_shape=SDS((t, d), BF16), name=name,
        compiler_params=_cp(("parallel", "parallel")))(y_a, y_b, p_main, p_main)


def _mix_bwd(du, y_a, y_b, p_main, name):
    t, d = y_a.shape
    tm, cw = _row_tile(t), _col_tile(d)
    off_a, off_b = MAIN_FIXED // cw, (MAIN_FIXED + d) // cw
    nb = d // cw

    def body(du_ref, ya_ref, yb_ref, ga_ref, gb_ref, dya_ref, dyb_ref, dla_ref, dlb_ref):
        duv = du_ref[...]
        ga, gb = _sigmoid(ga_ref[...]), _sigmoid(gb_ref[...])
        dya_ref[...] = (duv * ga).astype(BF16)
        dyb_ref[...] = (duv * gb).astype(BF16)
        dla_ref[...] = (duv * ya_ref[...] * ga * (1.0 - ga)).astype(BF16)
        dlb_ref[...] = (duv * yb_ref[...] * gb * (1.0 - gb)).astype(BF16)

    blk = pl.BlockSpec((tm, cw), lambda i, j: (i, j))
    outs = pl.pallas_call(
        body, grid=(t // tm, nb),
        in_specs=[blk, blk, blk, pl.BlockSpec((tm, cw), lambda i, j: (i, off_a + j)),
                  pl.BlockSpec((tm, cw), lambda i, j: (i, off_b + j))],
        out_specs=[blk, blk, blk, blk],
        out_shape=[SDS((t, d), BF16), SDS((t, d), BF16), SDS((t, d), BF16), SDS((t, d), BF16)], name=name,
        compiler_params=_cp(("parallel", "parallel")))(du, y_a, y_b, p_main, p_main)
    return outs


def _gate_res(x, y, gt, name):
    t, d = x.shape
    tm = _row_tile(t)

    def body(x_ref, y_ref, g_ref, o_ref):
        o_ref[...] = x_ref[...] + g_ref[...] * y_ref[...]

    blk = pl.BlockSpec((tm, d), lambda i: (i, 0))
    return pl.pallas_call(
        body, grid=(t // tm,), in_specs=[blk, blk, pl.BlockSpec((1, d), lambda i: (0, 0))], out_specs=blk,
        out_shape=SDS((t, d), F32), name=name, compiler_params=_cp(("parallel",)))(x, y, gt)


def _gate_res_bwd(dx, y, gt, name):
    t, d = dx.shape
    tm = _row_tile(t)

    def body(dx_ref, y_ref, g_ref, dg_ref, dy_ref):
        @pl.when(pl.program_id(0) == 0)
        def _():
            dg_ref[...] = jnp.zeros_like(dg_ref)

        dxv = dx_ref[...]
        dg_ref[...] += jnp.sum(dxv * y_ref[...], axis=0, keepdims=True)
        dy_ref[...] = (dxv * g_ref[...]).astype(BF16)

    blk = pl.BlockSpec((tm, d), lambda i: (i, 0))
    row = pl.BlockSpec((1, d), lambda i: (0, 0))
    return pl.pallas_call(
        body, grid=(t // tm,), in_specs=[blk, blk, row], out_specs=[row, blk],
        out_shape=[SDS((1, d), F32), SDS((t, d), BF16)], name=name, compiler_params=_cp(("arbitrary",)))(dx, y, gt)


def _swiglu_fwd(gu, name):
    t, f2 = gu.shape
    f = f2 // 2
    tm, cw = _row_tile(t), _col_tile(f)
    nb = f // cw

    def body(g_ref, u_ref, o_ref):
        g = g_ref[...]
        o_ref[...] = (g * _sigmoid(g) * u_ref[...]).astype(BF16)

    return pl.pallas_call(
        body, grid=(t // tm, nb),
        in_specs=[pl.BlockSpec((tm, cw), lambda i, j: (i, j)), pl.BlockSpec((tm, cw), lambda i, j: (i, nb + j))],
        out_specs=pl.BlockSpec((tm, cw), lambda i, j: (i, j)), out_shape=SDS((t, f), BF16), name=name,
        compiler_params=_cp(("parallel", "parallel")))(gu, gu)


def _swiglu_bwd(gu, da, name):
    t, f2 = gu.shape
    f = f2 // 2
    tm, cw = _row_tile(t), _col_tile(f)
    nb = f // cw

    def body(g_ref, u_ref, da_ref, dg_ref, dup_ref):
        g = g_ref[...]
        dav = da_ref[...]
        sg = _sigmoid(g)
        dg_ref[...] = (dav * u_ref[...] * sg * (1.0 + g * (1.0 - sg))).astype(BF16)
        dup_ref[...] = (dav * g * sg).astype(BF16)

    blk = pl.BlockSpec((tm, cw), lambda i, j: (i, j))
    dg, dup = pl.pallas_call(
        body, grid=(t // tm, nb),
        in_specs=[blk, pl.BlockSpec((tm, cw), lambda i, j: (i, nb + j)), blk], out_specs=[blk, blk],
        out_shape=[SDS((t, f), BF16), SDS((t, f), BF16)], name=name,
        compiler_params=_cp(("parallel", "parallel")))(gu, gu, da)
    return dg, dup


def _loss_head(x, w, target, name):
    t, d = x.shape
    tm = _row_tile(t)

    def body(x_ref, w_ref, t_ref, l_ref, dx_ref, dw_ref):
        @pl.when(pl.program_id(0) == 0)
        def _():
            l_ref[...] = jnp.zeros_like(l_ref)
            dw_ref[...] = jnp.zeros_like(dw_ref)

        xv = x_ref[...]
        wv = w_ref[...]
        r = lax.rsqrt(jnp.mean(xv * xv, axis=-1, keepdims=True) + EPS)
        xh = xv * r
        err = xh * wv - t_ref[...]
        per_tok = jnp.mean(err * err, axis=-1, keepdims=True)
        l_ref[...] += 0.5 * jnp.sum(per_tok, axis=0, keepdims=True)
        dy = err * (1.0 / d)
        dw_ref[...] += jnp.sum(dy * xh, axis=0, keepdims=True)
        dxh = dy * wv
        dx_ref[...] = r * (dxh - xh * jnp.mean(dxh * xh, axis=-1, keepdims=True))

    blk = pl.BlockSpec((tm, d), lambda i: (i, 0))
    row = pl.BlockSpec((1, d), lambda i: (0, 0))
    return pl.pallas_call(
        body, grid=(t // tm,), in_specs=[blk, row, blk],
        out_specs=[pl.BlockSpec((1, LANES), lambda i: (0, 0)), blk, row],
        out_shape=[SDS((1, LANES), F32), SDS((t, d), F32), SDS((1, d), F32)], name=name,
        compiler_params=_cp(("arbitrary",)))(x, w, target)


def _adamw(w, g, m, v, tie, name):
    shape = w.shape
    per_layer = isinstance(g, (list, tuple))
    n_layers = shape[0] if (w.ndim == 3 and shape[1] % 8 == 0) else 1
    cols = shape[-1]
    rows = w.size // cols // n_layers
    w, m, v = (a.reshape(n_layers * rows, cols) for a in (w, m, v))
    if not per_layer:
        g = g.reshape(n_layers * rows, cols)
    lanes_padded = -(-cols // LANES) * LANES
    budget_rows = max(8, (24 * 1024 * 1024) // (lanes_padded * 4 * 18))
    tr = rows
    if rows > budget_rows:
        tr = _pick(rows, tuple(c for c in (1024, 512, 256, 128, 64, 32, 16, 8) if c <= budget_rows))
    nrb = rows // tr
    c1 = 1.0 / (1.0 - ADAM_B1 ** ADAM_STEP)
    c2 = 1.0 / (1.0 - ADAM_B2 ** ADAM_STEP)
    n_g = len(g) if per_layer else 1

    def body(*refs):
        w_ref, m_ref, v_ref = refs[:3]
        g_refs = refs[3:3 + n_g]
        outs = refs[4 + n_g:]
        gv = g_refs[0][...]
        for l in range(1, n_g):
            gv = jnp.where(pl.program_id(0) == l, g_refs[l][...], gv)
        mn = ADAM_B1 * m_ref[...] + (1.0 - ADAM_B1) * gv
        vn = ADAM_B2 * v_ref[...] + (1.0 - ADAM_B2) * (gv * gv)
        outs[0][...] = -ADAM_LR * ((mn * c1) / (jnp.sqrt(vn * c2) + ADAM_EPS) + ADAM_WD * w_ref[...])
        outs[1][...] = mn
        outs[2][...] = vn
        if per_layer:
            outs[3][...] = gv

    blk = pl.BlockSpec((tr, cols), lambda l, i: (l * nrb + i, 0))
    g_specs = [pl.BlockSpec((tr, cols), lambda l, i: (i, 0))] * n_g if per_layer else [blk]
    n_out = 4 if per_layer else 3
    outs = pl.pallas_call(
        body, grid=(n_layers, nrb), in_specs=[blk, blk, blk] + g_specs + [pl.BlockSpec((8, LANES), lambda l, i: (0, 0))],
        out_specs=[blk] * n_out, out_shape=[SDS(w.shape, F32)] * n_out, name=name,
        compiler_params=_cp(("parallel", "parallel")))(w, m, v, *(g if per_layer else [g]), tie)
    g_out = outs[3] if per_layer else g
    return (g_out.reshape(shape),) + tuple(o.reshape(shape) for o in outs[:3])


KPE_LO = 2 * Q_LORA
QKVZ_LO = KPE_LO + D_ROPE
BA_LO = QKVZ_LO + 4 * GDN_W
GATE_LO = BA_LO + 2 * HEADS


def _lay_w_in(w_in):
    d = w_in.shape[0]
    main = jnp.concatenate([w_in[:, :KPE_LO], w_in[:, QKVZ_LO:BA_LO], w_in[:, GATE_LO:]], axis=1)
    small = jnp.concatenate([w_in[:, KPE_LO:QKVZ_LO], w_in[:, BA_LO:GATE_LO],
                             jnp.zeros((d, LANES - D_ROPE - 2 * HEADS), w_in.dtype)], axis=1)
    return main, small


def _unlay_w_in(g_main, g_small):
    return jnp.concatenate([g_main[:, :KPE_LO], g_small[:, :D_ROPE], g_main[:, KPE_LO:KPE_LO + 4 * GDN_W],
                            g_small[:, D_ROPE:D_ROPE + 2 * HEADS], g_main[:, MAIN_FIXED:]], axis=1)


def _lay_w_uq(w_uq):
    r = w_uq.reshape(Q_LORA, HEADS, D_NOPE + D_ROPE)
    r = jnp.pad(r, ((0, 0), (0, 0), (0, D_QK - D_NOPE - D_ROPE)))
    return r.reshape(Q_LORA, HEADS * D_QK)


def _unlay_w_uq(g):
    rows = g.shape[0]
    return g.reshape(rows, HEADS, D_QK)[:, :, :D_NOPE + D_ROPE].reshape(rows, HEADS * (D_NOPE + D_ROPE))


def _lay_w_ukv(w_ukv):
    return w_ukv.reshape(KV_LORA, HEADS, 2, D_V).transpose(0, 2, 1, 3).reshape(KV_LORA, 2 * HEADS * D_V)


def _unlay_w_ukv(g):
    rows = g.shape[0]
    return g.reshape(rows, 2, HEADS, D_V).transpose(0, 2, 1, 3).reshape(rows, 2 * HEADS * D_V)


def _lane_row(vec, lo):
    return jnp.pad(vec.reshape(1, -1), ((0, 0), (lo, LANES - lo - vec.shape[0])))


def _rope_tables(positions):
    half = D_ROPE // 2
    inv_freq = 1.0 / (10000.0 ** (jnp.arange(0, D_ROPE, 2, dtype=F32) / D_ROPE))
    ang = positions.astype(F32)[:, None] * inv_freq
    cos, sin = jnp.cos(ang), jnp.sin(ang)
    t = positions.shape[0]
    zeros = lambda n: jnp.zeros((t, n), F32)
    tc = jnp.concatenate([cos, cos, zeros(LANES - D_ROPE)], axis=1)
    ts1 = jnp.concatenate([-sin, zeros(LANES - half)], axis=1)
    ts2 = jnp.concatenate([zeros(half), sin, zeros(LANES - D_ROPE)], axis=1)
    return tc, ts1, ts2


def _layer_fwd(x, mod, wt, tabs, tag, late_weights, after_gate_up=None):
    t, d = x.shape
    sh_a, sc_a, gt_a, sh_f, sc_f, gt_f = mod
    zero_l = jnp.zeros((1, Q_LORA), F32)
    s = dict(x=x)
    s["h1"] = _norm_fwd(x, 0, d, wt["norm_mix"], sc_a, sh_a, f"{tag}_norm_mix")
    s["p_main"] = _mm(s["h1"], wt["w_main"], name=f"{tag}_in_main")
    s["p_small"] = _mm(s["h1"], wt["w_small"], name=f"{tag}_in_small")
    s["cqn"] = _norm_fwd(s["p_main"], 0, Q_LORA, wt["q_a_norm"], zero_l, zero_l, f"{tag}_q_norm")
    s["ckvn"] = _norm_fwd(s["p_main"], 1, KV_LORA, wt["kv_a_norm"], zero_l, zero_l, f"{tag}_kv_norm")
    q_raw = _mm(s["cqn"], wt["w_uq"], name=f"{tag}_uq")
    s["kv_raw"] = _mm(s["ckvn"], wt["w_ukv"], name=f"{tag}_ukv")
    s["q_r"] = _rope_q(q_raw, *tabs, False, BF16, f"{tag}_rope_q")
    s["k_r"] = _k_assemble(s["kv_raw"], s["p_small"], *tabs, f"{tag}_k_asm")
    s["o"], s["lse"] = _attn_fwd(s["q_r"], s["k_r"], s["kv_raw"], HEADS, f"{tag}_attn")
    s["yc"] = _conv_fwd(s["p_main"], wt["conv_w"], f"{tag}_conv")
    s["gates"], s["gcum"] = _gdn_gates(s["p_small"], wt["a_row"], wt["dt_row"], f"{tag}_gates")
    s["o_raw"], s["gated"], s["states"] = _gdn_fwd(s["yc"], s["p_main"], s["gates"], s["gcum"], wt["gdn_norm"], f"{tag}_gdn")
    late, started = late_weights(s["gated"])
    wt = {**wt, **late}
    s["y_a"] = _mm(s["o"], wt["w_o_mla"], name=f"{tag}_o_mla")
    s["y_b"] = _mm(s["gated"], wt["w_o_gdn"], name=f"{tag}_o_gdn")
    s["u"] = _mix_fwd(s["y_a"], s["y_b"], s["p_main"], f"{tag}_mix")
    s["mixo"] = _mm(s["u"], wt["w_o"], name=f"{tag}_o")
    s["x2"] = _gate_res(x, s["mixo"], gt_a, f"{tag}_res_a")
    s["h2"] = _norm_fwd(s["x2"], 0, d, wt["norm_ffn"] + started, sc_f, sh_f, f"{tag}_norm_ffn")
    s["gu"] = _mm(s["h2"], wt["w_gate_up"], name=f"{tag}_gate_up")
    if after_gate_up is not None:
        gt_f = gt_f + after_gate_up(s["gu"])
    s["a"] = _swiglu_fwd(s["gu"], f"{tag}_swiglu")
    s["f"] = _mm(s["a"], wt["w_down"], name=f"{tag}_down")
    return _gate_res(s["x2"], s["f"], gt_f, f"{tag}_res_f"), s, wt


def _layer_bwd(dx3, s, mod, wt, tabs, tag, after_ffn=None, after_gdn=None):
    x = s["x"]
    t, d = x.shape
    sh_a, sc_a, gt_a, sh_f, sc_f, gt_f = mod
    zero_l = jnp.zeros((1, Q_LORA), F32)
    g = {}
    dgt_f, df = _gate_res_bwd(dx3, s["f"], gt_f, f"{tag}_b_res_f")
    da = _mm(df, wt["w_down"], tb=True, name=f"{tag}_b_down_x")
    g["w_down"] = _mm(s["a"].T, df, out_dtype=BF16, name=f"{tag}_b_down_w")
    dgate, dup = _swiglu_bwd(s["gu"], da, f"{tag}_b_swiglu")
    dgu = jnp.concatenate([dgate, dup], axis=1)
    dh2 = _mm(dgu, wt["w_gate_up"], tb=True, name=f"{tag}_b_gate_up_x")
    g["w_gate_up"] = _mm(s["h2"].T, dgu, out_dtype=BF16, name=f"{tag}_b_gate_up_w")
    if after_ffn is not None:
        gt_a = gt_a + after_ffn(g)
    dx2, g["norm_ffn"], dsc_f, dsh_f = _norm_bwd(s["x2"], 0, d, wt["norm_ffn"], sc_f, dh2, dx3, F32, f"{tag}_b_norm_ffn")
    dgt_a, dmixo = _gate_res_bwd(dx2, s["mixo"], gt_a, f"{tag}_b_res_a")
    du = _mm(dmixo, wt["w_o"], tb=True, name=f"{tag}_b_o_x")
    g["w_o"] = _mm(s["u"].T, dmixo, out_dtype=BF16, name=f"{tag}_b_o_w")
    dy_a, dy_b, dl_a, dl_b = _mix_bwd(du, s["y_a"], s["y_b"], s["p_main"], f"{tag}_b_mix")
    dgated = _mm(dy_b, wt["w_o_gdn"], tb=True, name=f"{tag}_b_o_gdn_x")
    g["w_o_gdn"] = _mm(s["gated"].T, dy_b, out_dtype=BF16, name=f"{tag}_b_o_gdn_w")
    dq_c, dk_c, dv_c, dz, dgates, g["gdn_norm"] = _gdn_bwd(
        s["yc"], s["p_main"], s["gates"], s["gcum"], wt["gdn_norm"], s["o_raw"], s["states"], dgated, f"{tag}_b_gdn")
    du_conv, g["conv_w"] = _conv_bwd(s["p_main"], wt["conv_w"], jnp.concatenate([dq_c, dk_c, dv_c], axis=1), f"{tag}_b_conv")
    do = _mm(dy_a, wt["w_o_mla"], tb=True, name=f"{tag}_b_o_mla_x")
    g["w_o_mla"] = _mm(s["o"].T, dy_a, out_dtype=BF16, name=f"{tag}_b_o_mla_w")
    dq_r, dk_r, dv = _attn_bwd(s["q_r"], s["k_r"], s["kv_raw"], HEADS, s["o"], s["lse"], do, f"{tag}_b_attn")
    q_a_norm = wt["q_a_norm"]
    if after_gdn is not None:
        q_a_norm = q_a_norm + after_gdn(du_conv)
    dq_raw = _rope_q(dq_r, *tabs, True, BF16, f"{tag}_b_rope_q")
    dkv_raw, dkpe = _k_assemble_bwd(dk_r, dv, *tabs, f"{tag}_b_k_asm")
    dcqn = _mm(dq_raw, wt["w_uq"], tb=True, name=f"{tag}_b_uq_x")
    g["w_uq"] = _mm(s["cqn"].T, dq_raw, out_dtype=BF16, name=f"{tag}_b_uq_w")
    dckvn = _mm(dkv_raw, wt["w_ukv"], tb=True, name=f"{tag}_b_ukv_x")
    g["w_ukv"] = _mm(s["ckvn"].T, dkv_raw, out_dtype=BF16, name=f"{tag}_b_ukv_w")
    dc_q, g["q_a_norm"], _, _ = _norm_bwd(s["p_main"], 0, Q_LORA, q_a_norm, zero_l, dcqn, None, BF16, f"{tag}_b_q_norm")
    dc_kv, g["kv_a_norm"], _, _ = _norm_bwd(s["p_main"], 1, KV_LORA, wt["kv_a_norm"], zero_l, dckvn, None, BF16,
                                            f"{tag}_b_kv_norm")
    dp_small, g["a_row"], g["dt_row"] = _gdn_gates_bwd(
        s["p_small"], wt["a_row"], wt["dt_row"], s["gates"], dgates, dkpe, f"{tag}_b_gates")
    dp_main = jnp.concatenate([dc_q, dc_kv, du_conv, dz, dl_a, dl_b], axis=1)
    h1t = s["h1"].T
    dh1 = _mm(dp_small, wt["w_small"], tb=True, name=f"{tag}_b_in_small_x")
    dh1 = _mm(dp_main, wt["w_main"], tb=True, acc_in=dh1, name=f"{tag}_b_in_main_x")
    g["w_main"] = _mm(h1t, dp_main, out_dtype=BF16, name=f"{tag}_b_in_main_w")
    g["w_small"] = _mm(h1t, dp_small, out_dtype=BF16, name=f"{tag}_b_in_small_w")
    dx, g["norm_mix"], dsc_a, dsh_a = _norm_bwd(x, 0, d, wt["norm_mix"], sc_a, dh1, dx2, F32, f"{tag}_b_norm_mix")
    return dx, (dsh_a, dsc_a, dgt_a, dsh_f, dsc_f, dgt_f), g


def _layer_weights(big, full, l):
    w_main, w_small = _lay_w_in(big["w_in"])
    return dict(
        w_main=w_main, w_small=w_small, w_uq=_lay_w_uq(big["w_uq"]), w_ukv=_lay_w_ukv(big["w_ukv"]),
        conv_w=full["conv_w"][l],
        norm_mix=full["norm_mix"][l][None], norm_ffn=full["norm_ffn"][l][None],
        q_a_norm=full["q_a_norm"][l][None], kv_a_norm=full["kv_a_norm"][l][None], gdn_norm=full["gdn_norm"][l][None],
        a_row=_lane_row(full["A_log"][l], A_LO), dt_row=_lane_row(full["dt_bias"][l], A_LO))


def _small_grads_ref_layout(g):
    return dict(
        conv_w=g["conv_w"], norm_mix=g["norm_mix"][0], norm_ffn=g["norm_ffn"][0], q_a_norm=g["q_a_norm"][0],
        kv_a_norm=g["kv_a_norm"][0], gdn_norm=g["gdn_norm"][0], A_log=g["a_row"][0, A_LO:A_HI],
        dt_bias=g["dt_row"][0, A_LO:A_HI])


def _local_step(x, mods, target, final_norm, full, positions):
    tabs = _rope_tables(positions)
    depth = len(mods)
    wts, saved = [None] * depth, []
    h = x
    for l in range(depth):
        early = _layer_weights({n: full[n][l] for n in FIRST_NEEDED}, full, l)
        h, s, wts[l] = _layer_fwd(h, mods[l], early, tabs, f"l{l}", lambda _, l=l: ({n: full[n][l] for n in LATER_NEEDED}, 0.0))
        saved.append(s)
    loss, dh, dfn = _loss_head(h, final_norm[None], target, "loss_head")
    dmods, grads = [None] * depth, [None] * depth
    for l in reversed(range(depth)):
        dh, dmods[l], grads[l] = _layer_bwd(dh, saved[l], mods[l], wts[l], tabs, f"l{l}")
    return loss, dh, dmods, grads, dfn[0]


HBM_SPEC = pl.BlockSpec(memory_space=pl.ANY)
VMEM_SPEC = pl.BlockSpec(memory_space=pltpu.VMEM)
N_CHIPS = 4
N_DEV = 8


def _me():
    return lax.axis_index("x"), lax.axis_index("y"), lax.axis_index("c")


def _flip(pos, f):
    mx, my, mc = pos
    fx, fy, fc = (f >> 2) & 1, (f >> 1) & 1, f & 1
    return ((mx + fx) % 2, (my + fy) % 2, (mc + fc) % 2)


def _all_gather_small(x, name):
    r, n = x.shape

    def body(x_ref, out_ref, send_sems, recv_sems, local_sem):
        me = _me()
        row = lambda p: 4 * p[0] + 2 * p[1] + p[2]
        mine = pltpu.make_async_copy(x_ref, out_ref.at[row(me)], local_sem)
        mine.start()

        def copy(f, origin):
            return pltpu.make_async_remote_copy(
                src_ref=x_ref, dst_ref=out_ref.at[row(origin)], send_sem=send_sems.at[f - 1], recv_sem=recv_sems.at[f - 1],
                device_id=_flip(me, f), device_id_type=MESH)

        sends = [copy(f, me) for f in range(1, N_DEV)]
        for cp in sends:
            cp.start()
        for f in range(1, N_DEV):
            copy(f, _flip(me, f)).wait_recv()
        for cp in sends:
            cp.wait_send()
        mine.wait()

    return pl.pallas_call(
        body, out_shape=SDS((N_DEV, r, n), x.dtype), in_specs=[VMEM_SPEC], out_specs=VMEM_SPEC,
        scratch_shapes=[pltpu.SemaphoreType.DMA((N_DEV - 1,)), pltpu.SemaphoreType.DMA((N_DEV - 1,)), pltpu.SemaphoreType.DMA],
        name=name, compiler_params=pltpu.CompilerParams(vmem_limit_bytes=VMEM_LIMIT))(x)


CHIP_FLIPS = (2, 4, 6)
SIBLING = 1


def _chip_of(pos):
    return 2 * pos[0] + pos[1]


def _sum_chips(p, got, chip, half, name):
    _, kh, ns = p.shape
    tr = _pick(kh, (256, 128, 64, 32, 16))
    nrb = kh // tr

    def body(c_ref, h_ref, a_ref, b_ref, o_ref):
        acc = a_ref[0].astype(F32)
        for j in range(3):
            acc = acc + b_ref[j].astype(F32)
        o_ref[...] = acc

    grid_spec = pltpu.PrefetchScalarGridSpec(
        num_scalar_prefetch=2, grid=(nrb,),
        in_specs=[pl.BlockSpec((1, tr, ns), lambda i, c, h: (c[0], i, 0)),
                  pl.BlockSpec((3, tr, ns), lambda i, c, h: (0, i, 0))],
        out_specs=pl.BlockSpec((tr, ns), lambda i, c, h: (h[0] * nrb + i, 0)))
    return pl.pallas_call(body, grid_spec=grid_spec, out_shape=SDS((2 * kh, ns), F32), name=name,
                          compiler_params=_cp(("parallel",)))(chip, half, p, got)


SEM_SPEC = pl.BlockSpec(memory_space=pltpu.SEMAPHORE)
HBM_ONLY = pl.BlockSpec(memory_space=pltpu.HBM)
DATAFLOW = pltpu.SideEffectType.DATAFLOW_SIDE_EFFECTING


def _in_hbm(a):
    return pltpu.with_memory_space_constraint(a, pltpu.HBM)


def _copies_start(name, srcs, lands, plan, n_copies):
    ns, nl = len(srcs), len(lands)

    def body(*refs):
        src_refs, land_refs = refs[:ns], refs[ns:ns + nl]
        send_sems, recv_sems = refs[ns + nl], refs[ns + nl + 1]
        token = refs[-1]
        for i, (src, dst, peer) in enumerate(plan(_me(), src_refs, land_refs)):
            pltpu.make_async_remote_copy(src_ref=src, dst_ref=dst, send_sem=send_sems.at[i], recv_sem=recv_sems.at[i],
                                         device_id=peer, device_id_type=MESH).start()
        token[...] = jnp.zeros_like(token)

    outs = pl.pallas_call(
        body, name=name,
        out_shape=(pltpu.SemaphoreType.DMA((n_copies,)), pltpu.SemaphoreType.DMA((n_copies,)),
                   *[pltpu.HBM(l.shape, l.dtype) for l in lands], SDS((8, LANES), F32)),
        in_specs=[HBM_ONLY] * (ns + nl), out_specs=(SEM_SPEC, SEM_SPEC, *[HBM_ONLY] * nl, VMEM_SPEC),
        input_output_aliases={ns + i: 2 + i for i in range(nl)},
        compiler_params=pltpu.CompilerParams(has_side_effects=DATAFLOW),
    )(*[_in_hbm(s) for s in srcs], *[_in_hbm(l) for l in lands])
    return outs[0], outs[1], list(outs[2:2 + nl]), outs[-1]


def _copies_wait(name, srcs, lands, send_sems, recv_sems, plan, after):
    ns, nl = len(srcs), len(lands)

    def body(*refs):
        src_refs, land_refs = refs[:ns], refs[ns:ns + nl]
        send_ref, recv_ref = refs[ns + nl], refs[ns + nl + 1]
        for i, (src, dst, peer) in enumerate(plan(_me(), src_refs, land_refs)):
            cp = pltpu.make_async_remote_copy(src_ref=src, dst_ref=dst, send_sem=send_ref.at[i], recv_sem=recv_ref.at[i],
                                              device_id=peer, device_id_type=MESH)
            cp.wait_send()
            cp.wait_recv()

    outs = pl.pallas_call(
        body, name=name, out_shape=[pltpu.HBM(l.shape, l.dtype) for l in lands],
        in_specs=[HBM_ONLY] * (ns + nl) + [SEM_SPEC, SEM_SPEC, HBM_SPEC], out_specs=[HBM_ONLY] * nl,
        input_output_aliases={ns + i: i for i in range(nl)},
        compiler_params=pltpu.CompilerParams(has_side_effects=DATAFLOW),
    )(*[_in_hbm(s) for s in srcs], *lands, send_sems, recv_sems, after)
    return list(outs)


def _half(ref, rows, axis):
    idx = [slice(None)] * axis + [rows]
    return ref.at[tuple(idx)]


def _gather_plans(layer, halves):
    def ici(me, srcs, lands):
        out = []
        for a, kh in enumerate(halves):
            rows = pl.ds(pl.multiple_of(me[2] * kh, 16), kh)
            for k in range(3):
                out.append((srcs[a].at[layer, rows], lands[a].at[_chip_of(me), rows], _flip(me, CHIP_FLIPS[k])))
        return out

    def d2d(me, srcs, lands):
        out = []
        for a, kh in enumerate(halves):
            rows = pl.ds(pl.multiple_of(me[2] * kh, 16), kh)
            for k in range(3):
                slab = lands[a].at[_chip_of(_flip(me, CHIP_FLIPS[k])), rows]
                out.append((slab, slab, _flip(me, SIBLING)))
        return out

    return ici, d2d


def _to_sibling_plan(halves, axes):
    def plan(me, srcs, lands):
        out = []
        for a, (kh, axis) in enumerate(zip(halves, axes)):
            rows = pl.ds(pl.multiple_of((1 - me[2]) * kh, 16), kh)
            out.append((_half(srcs[a], rows, axis), lands[a], _flip(me, SIBLING)))
        return out

    return plan


def _to_chips_plan(n_arr):
    def plan(me, srcs, lands):
        out = []
        for a in range(n_arr):
            for k in range(3):
                peer = _flip(me, CHIP_FLIPS[k])
                out.append((srcs[a].at[_chip_of(peer)], lands[a].at[k], peer))
        return out

    return plan


def _swap_plan(halves):
    def plan(me, srcs, lands):
        out = []
        for a, kh in enumerate(halves):
            rows = pl.ds(pl.multiple_of(me[2] * kh, 16), kh)
            out.append((lands[a].at[rows], lands[a].at[rows], _flip(me, SIBLING)))
        return out

    return plan


def _add_half(g, got, half, col_shards, name):
    s, kh, n = got.shape
    tr = _pick(kh, (512, 256, 128, 64, 32, 16))
    nrb = kh // tr
    width = n // N_CHIPS if col_shards else n
    cw = _pick(width, (1024, 512, 256, 128))
    per = width // cw

    def body(h_ref, a_ref, b_ref, o_ref):
        o_ref[...] = (a_ref[...].astype(F32) + b_ref[...].astype(F32)).astype(o_ref.dtype)

    in_specs = [pl.BlockSpec((None, tr, cw), lambda j, i, c, h: (j, h[0] * nrb + i, c)),
                pl.BlockSpec((None, tr, cw), lambda j, i, c, h: (j, i, c))]
    if col_shards:
        assert s == 1
        out_spec = pl.BlockSpec((None, tr, cw), lambda j, i, c, h: (c // per, i, c % per))
        out_shape = SDS((N_CHIPS, kh, width), g.dtype)
    else:
        out_spec, out_shape = in_specs[1], SDS((s, kh, n), g.dtype)
    grid_spec = pltpu.PrefetchScalarGridSpec(num_scalar_prefetch=1, grid=(s, nrb, n // cw), in_specs=in_specs,
                                             out_specs=out_spec)
    return pl.pallas_call(body, grid_spec=grid_spec, out_shape=out_shape, name=name,
                          compiler_params=_cp(("parallel", "parallel", "parallel")))(half, g, got)


def _sum_devices(g, name):
    _, _, n = g.shape

    def body(g_ref, o_ref):
        acc = g_ref[0]
        for k in range(1, N_DEV):
            acc = acc + g_ref[k]
        o_ref[...] = acc

    return pl.pallas_call(body, out_shape=SDS((1, n), F32), in_specs=[VMEM_SPEC], out_specs=VMEM_SPEC, name=name)(g)


def _silu_rows(c, name):
    def body(c_ref, o_ref):
        v = c_ref[...]
        o_ref[...] = v * _sigmoid(v)

    return pl.pallas_call(body, out_shape=SDS(c.shape, F32), in_specs=[VMEM_SPEC], out_specs=VMEM_SPEC, name=name)(c)


BIG = (("w_in", 2), ("w_uq", 2), ("w_ukv", 2), ("w_o_mla", 2), ("w_o_gdn", 2), ("w_o", 1), ("w_gate_up", 2), ("w_down", 1))
KERNEL_BIG = ("w_main", "w_small", "w_uq", "w_ukv", "w_o_mla", "w_o_gdn", "w_o", "w_gate_up", "w_down")
COL_SHARDED_AS_IS = ("w_o_mla", "w_o_gdn", "w_gate_up")
ROW_SHARDED = ("w_o", "w_down")
FIRST_NEEDED = ("w_in", "w_uq", "w_ukv")
LATER_NEEDED = ("w_o_mla", "w_o_gdn", "w_o", "w_gate_up", "w_down")
FFN_GRADS = ("w_gate_up", "w_down")
MIXER_GRADS = ("w_in", "w_uq", "w_ukv", "w_o_mla", "w_o_gdn", "w_o")
MIXER_GRADS_KERNEL = ("w_main", "w_small", "w_uq", "w_ukv", "w_o_mla", "w_o_gdn", "w_o")
SMALL = ("norm_mix", "norm_ffn", "q_a_norm", "kv_a_norm", "A_log", "dt_bias", "gdn_norm")
WEIGHTS = ("w_ada", "b_ada", "norm_mix", "norm_ffn", "w_in", "q_a_norm", "kv_a_norm", "w_uq", "w_ukv", "w_o_mla", "conv_w",
           "A_log", "dt_bias", "gdn_norm", "w_o_gdn", "w_o", "w_gate_up", "w_down", "final_norm")
ADA_PAD = 16
K_PAD = 128


def _pad_to(a, n, axis):
    pad = [(0, 0)] * a.ndim
    pad[axis] = (0, n - a.shape[axis])
    return jnp.pad(a, pad)


def kernel(x, c, positions, w_ada, b_ada, norm_mix, norm_ffn, w_in, q_a_norm, kv_a_norm, w_uq, w_ukv, w_o_mla, conv_w, A_log, dt_bias, gdn_norm, w_o_gdn, w_o, w_gate_up, w_down, final_norm, loss_target, m_w_ada, m_b_ada, m_norm_mix, m_norm_ffn, m_w_in, m_q_a_norm, m_kv_a_norm, m_w_uq, m_w_ukv, m_w_o_mla, m_conv_w, m_A_log, m_dt_bias, m_gdn_norm, m_w_o_gdn, m_w_o, m_w_gate_up, m_w_down, m_final_norm, v_w_ada, v_b_ada, v_norm_mix, v_norm_ffn, v_w_in, v_q_a_norm, v_kv_a_norm, v_w_uq, v_w_ukv, v_w_o_mla, v_conv_w, v_A_log, v_dt_bias, v_gdn_norm, v_w_o_gdn, v_w_o, v_w_gate_up, v_w_down, v_final_norm):
    env = dict(locals())
    w = {n: env[n] for n in WEIGHTS}
    depth, d = norm_mix.shape
    t = x.shape[1]
    me = _me()
    chip = _chip_of(me)
    dev = 4 * me[0] + 2 * me[1] + me[2]
    ada_cols = w_ada.shape[2]

    half_idx = me[2].astype(jnp.int32).reshape(1)
    chip_idx = chip.astype(jnp.int32).reshape(1)
    w16 = {n: w[n].astype(BF16) for n, _ in BIG}
    shard_axis = dict(BIG)
    gather = {}

    def start_group(key, layer, names, dep):
        srcs = [w16[n] for n in names]
        plans = _gather_plans(layer, [a.shape[1] // 2 for a in srcs])
        landing = [lax.empty((N_CHIPS,) + a.shape[1:], BF16) for a in srcs]
        send_s, recv_s, landing, tok = _copies_start(f"gather_{key}_ici_start", srcs + [dep], landing, plans[0], 3 * len(names))
        gather[key] = dict(layer=layer, names=names, srcs=srcs, plans=plans, ici=(send_s, recv_s, landing), tok=tok)
        return tok[0, 0]

    def pass_to_sibling(key, after):
        st = gather[key]
        send_s, recv_s, landing = st["ici"]
        landing = _copies_wait(f"gather_{key}_ici_wait", st["srcs"] + [st["tok"]], landing, send_s, recv_s, st["plans"][0],
                               st["tok"] if after is None else after)
        st["d2d"] = _copies_start(f"gather_{key}_d2d_start", [], landing, st["plans"][1], 3 * len(st["names"]))
        return st["d2d"][3]

    def gathered(key, after=None):
        st = gather[key]
        send_s, recv_s, landing, tok = st["d2d"]
        landing = _copies_wait(f"gather_{key}_d2d_wait", [], landing, send_s, recv_s, st["plans"][1],
                               tok if after is None else after)
        return {n: jnp.concatenate([jnp.where(chip == j, own[st["layer"]], got[j]) for j in range(N_CHIPS)],
                                   axis=shard_axis[n] - 1)
                for n, own, got in zip(st["names"], st["srcs"], landing)}

    first_started = start_group("l0a", 0, FIRST_NEEDED, jnp.zeros((8, LANES), F32))
    c = c + first_started
    w_in_adam = tuple(env[p + "w_in"] + first_started for p in ("", "m_", "v_"))
    busy = sum(jnp.minimum(jnp.abs(a[0, :1, :1].astype(F32)), 0.0)
               for a in (*w_in_adam, *[w16[n] for n in LATER_NEEDED]))[0, 0]
    full = {}
    conv_all = _all_gather_small(conv_w.reshape(1, -1) + (first_started + busy), "gather_conv").reshape(
        (N_DEV,) + conv_w.shape)
    full["conv_w"] = jnp.concatenate([conv_all[2 * j] for j in range(N_CHIPS)], axis=2)
    for n in SMALL:
        full[n] = w[n]

    c_all = _all_gather_small(c, "gather_c").reshape(N_DEV, d)
    c_act = _silu_rows(_pad_to(c_all, ADA_PAD, 0), "silu_c")
    b_cols = lax.dynamic_slice_in_dim(b_ada, chip * ada_cols, ada_cols, axis=1)
    mod_cols = jnp.stack([
        _mm(c_act, w_ada, b_layer=l, acc_in=jnp.broadcast_to(b_cols[l][None], (ADA_PAD, ada_cols)), name=f"ada_l{l}")[:N_DEV]
        for l in range(depth)])
    mod_all = _all_gather_small(mod_cols.reshape(depth * N_DEV, ada_cols), "gather_mod")
    mod_all = mod_all.reshape(N_DEV, depth, N_DEV, ada_cols)
    mods = []
    for l in range(depth):
        mine = jnp.concatenate([lax.dynamic_index_in_dim(mod_all[2 * j, l], dev, axis=0, keepdims=True)
                                for j in range(N_CHIPS)], axis=1)
        mods.append(tuple(mine[:, i * d:(i + 1) * d] for i in range(6)))

    tabs = _rope_tables(positions[0])
    tie = start_group("l0b", 0, LATER_NEEDED, pass_to_sibling(
        "l0a", mods[depth - 1][5][:, :LANES] + full["conv_w"].reshape(1, -1)[:, :LANES]))

    def late_weights(key, next_key, next_layer, behind):
        tok = pass_to_sibling(key, behind)
        started = 0.0 if next_key is None else start_group(next_key, next_layer, FIRST_NEEDED, tok)
        return gathered(key, None if next_key is None else gather[next_key]["tok"]), started

    def next_later_group(behind):
        return start_group("l1b", 1, LATER_NEEDED, pass_to_sibling("l1a", behind))

    wts, saved = [None] * depth, [None] * depth
    tied = (mods[0][0] + tie,) + mods[0][1:]
    h, saved[0], wts[0] = _layer_fwd(x[0], tied, _layer_weights(gathered("l0a", gather["l0b"]["tok"]), full, 0), tabs, "l0",
                                     functools.partial(late_weights, "l0b", "l1a", 1), next_later_group)
    h, saved[1], wts[1] = _layer_fwd(h, mods[1], _layer_weights(gathered("l1a", gather["l1b"]["tok"]), full, 1), tabs, "l1",
                                     functools.partial(late_weights, "l1b", None, None))
    loss_part, dh, dfn = _loss_head(h, final_norm[None], loss_target[0], "loss_head")
    dfn = dfn[0]

    def col_shards(g):
        return g.reshape(g.shape[0], N_CHIPS, g.shape[1] // N_CHIPS).transpose(1, 0, 2)

    def reduce_scatter_stages(tag, g, knames, names):
        srcs = [g[n].reshape(N_CHIPS, -1, g[n].shape[1]) if n in ROW_SHARDED else g[n] for n in knames]
        axes = [1 if n in ROW_SHARDED else 0 for n in knames]
        halves = [a.shape[ax] // 2 for a, ax in zip(srcs, axes)]
        got_shapes = [a.shape[:ax] + (kh,) + a.shape[ax + 1:] for a, ax, kh in zip(srcs, axes, halves)]
        plan_a, plan_c = _to_sibling_plan(halves, axes), _to_chips_plan(len(names))
        st, out = {}, {}
        st["a"] = _copies_start(f"{tag}_sibling_start", srcs, [lax.empty(sh, BF16) for sh in got_shapes], plan_a, len(srcs))

        def after_or(tok, after):
            return tok if after is None else after

        def stage0(after):
            send_s, recv_s, landing, tok = st["a"]
            got = _copies_wait(f"{tag}_sibling_wait", srcs, landing, send_s, recv_s, plan_a, after_or(tok, after))
            sums = {}
            for n, a, b in zip(knames, srcs, got):
                a3, b3 = (v if v.ndim == 3 else v[None] for v in (a, b))
                r = _add_half(a3, b3, half_idx, n in COL_SHARDED_AS_IS, f"{tag}_add_{n}")
                sums[n] = r if (n in COL_SHARDED_AS_IS or n in ROW_SHARDED) else r[0]
            if "w_main" in sums:
                sums["w_in"] = col_shards(_unlay_w_in(sums["w_main"], sums["w_small"]))
                sums["w_uq"] = col_shards(_unlay_w_uq(sums["w_uq"]))
                sums["w_ukv"] = col_shards(_unlay_w_ukv(sums["w_ukv"]))
            st["p"] = [sums[n] for n in names]
            st["c"] = _copies_start(f"{tag}_chips_start", st["p"], [lax.empty((3,) + p.shape[1:], BF16) for p in st["p"]],
                                    plan_c, 3 * len(names))
            return st["c"][3][0, 0]

        def stage1(after):
            send_s, recv_s, landing, tok = st["c"]
            got = _copies_wait(f"{tag}_chips_wait", st["p"], landing, send_s, recv_s, plan_c, after_or(tok, after))
            sums = [_sum_chips(p, q, chip_idx, half_idx, f"{tag}_sum_{n}") for n, p, q in zip(names, st["p"], got)]
            plan_e = _swap_plan([r.shape[0] // 2 for r in sums])
            st["e"] = _copies_start(f"{tag}_swap_start", [], sums, plan_e, len(names)) + (plan_e,)
            return st["e"][3][0, 0]

        def stage2(after):
            send_s, recv_s, landing, tok, plan_e = st["e"]
            got = _copies_wait(f"{tag}_swap_wait", [], landing, send_s, recv_s, plan_e, after_or(tok, after))
            out.update(zip(names, got))

        return (stage0, stage1, stage2), out, st["a"][3][0, 0]

    dmods, grads, groups = [None] * depth, [None] * depth, {}

    def ffn_group_l1(g):
        groups["l1_ffn"] = reduce_scatter_stages("rs_l1_ffn", g, FFN_GRADS, FFN_GRADS)
        return groups["l1_ffn"][2]

    dh, dmods[1], grads[1] = _layer_bwd(dh, saved[1], mods[1], wts[1], tabs, "l1", after_ffn=ffn_group_l1)
    groups["l1_mix"] = reduce_scatter_stages("rs_l1_mix", grads[1], MIXER_GRADS_KERNEL, MIXER_GRADS)
    tied = mods[0][:5] + (mods[0][5] + groups["l1_mix"][2],)

    def ffn_group_l0(g):
        behind = g["w_gate_up"]
        tok = groups["l1_ffn"][0][0](behind) + groups["l1_mix"][0][0](behind)
        groups["l0_ffn"] = reduce_scatter_stages("rs_l0_ffn", g, FFN_GRADS, FFN_GRADS)
        return tok + groups["l0_ffn"][2]

    def after_gdn_l0(behind):
        return groups["l0_ffn"][0][0](behind)

    dx, dmods[0], grads[0] = _layer_bwd(dh, saved[0], tied, wts[0], tabs, "l0", after_ffn=ffn_group_l0, after_gdn=after_gdn_l0)
    groups["l0_mix"] = reduce_scatter_stages("rs_l0_mix", grads[0], MIXER_GRADS_KERNEL, MIXER_GRADS)
    for key in ("l1_ffn", "l1_mix", "l0_ffn"):
        groups[key][0][1](dx)
    g_out, deltas, new_m, new_v = {}, {}, {}, {}

    def reduced(names):
        for n in names:
            g_out[n] = [groups[f"l{l}_ffn" if n in FFN_GRADS else f"l{l}_mix"][1][n] for l in range(depth)]

    def update(names, tie):
        for n in names:
            w_n, m_n, v_n = w_in_adam if n == "w_in" else (w[n], env["m_" + n], env["v_" + n])
            g_out[n], deltas[n], new_m[n], new_v[n] = _adamw(w_n, g_out[n], m_n, v_n, tie, f"adamw_{n}")

    small = [_small_grads_ref_layout(grads[l]) for l in range(depth)]
    small_parts = [jnp.concatenate(dmods[l], axis=1).reshape(-1) for l in range(depth)]
    small_parts += [jnp.stack([small[l][n] for l in range(depth)]).reshape(-1) for n in SMALL]
    small_parts += [dfn, loss_part[0, :1]]
    small_sizes = [p.shape[0] for p in small_parts]
    packed = jnp.concatenate(small_parts)
    n_small = -(-packed.shape[0] // LANES) * LANES
    small_all = _all_gather_small(_pad_to(packed, n_small, 0).reshape(1, n_small), "gather_small_grads")
    small_sum = _sum_devices(small_all, "sum_small_grads")[0]
    offs = [0]
    for sz in small_sizes:
        offs.append(offs[-1] + sz)
    g_out["b_ada"] = jnp.stack([small_sum[offs[l]:offs[l + 1]] for l in range(depth)])
    for i, n in enumerate(SMALL):
        g_out[n] = small_sum[offs[depth + i]:offs[depth + i + 1]].reshape(w[n].shape)
    g_out["final_norm"] = small_sum[offs[depth + len(SMALL)]:offs[depth + len(SMALL) + 1]]
    loss = small_sum[offs[depth + len(SMALL) + 1]]

    c_act_t = _pad_to(c_act[:N_DEV].T, K_PAD, 1)
    g_ada = []
    for l in range(depth):
        dmod_l = small_all[:, 0, offs[l]:offs[l + 1]]
        dmod_cols = lax.dynamic_slice_in_dim(dmod_l, chip * ada_cols, ada_cols, axis=1)
        g_ada.append(_mm(c_act_t, _pad_to(dmod_cols, K_PAD, 0), name=f"ada_grad_l{l}"))
    g_out["w_ada"] = jnp.stack(g_ada)

    conv_g = jnp.stack([small[l]["conv_w"] for l in range(depth)])
    conv_all_g = _all_gather_small(conv_g.reshape(1, -1), "gather_conv_grads")
    conv_sum = _sum_devices(conv_all_g, "sum_conv_grads").reshape(conv_g.shape)
    n_cc = conv_w.shape[2]
    g_out["conv_w"] = lax.dynamic_slice_in_dim(conv_sum, chip * n_cc, n_cc, axis=2)

    mix0 = groups["l0_mix"][0]
    started = mix0[0](conv_sum.reshape(-1)[:LANES] + small_sum[:LANES])
    for key in ("l1_ffn", "l1_mix", "l0_ffn"):
        groups[key][0][2](None)
    reduced(FFN_GRADS)
    first_updates = ("w_ada", "b_ada", "final_norm", "conv_w") + SMALL + FFN_GRADS
    update(first_updates, jnp.zeros((8, LANES), F32) + started)
    corner = lambda a: a.reshape((1,) * (3 - a.ndim) + a.shape)[0, :1, :LANES]
    mix0[1](sum(corner(deltas[n]) for n in first_updates if w[n].shape[-1] >= LANES))
    mix0[2](None)
    reduced(MIXER_GRADS)
    update(MIXER_GRADS, jnp.zeros((8, LANES), F32))
    return (loss, dx[None], *[g_out[n] for n in WEIGHTS], *[deltas[n] for n in WEIGHTS],
            *[new_m[n] for n in WEIGHTS], *[new_v[n] for n in WEIGHTS])
```
